```python
import jax
import jax.numpy as jnp
from jax import lax
import numpy as np

D_MODEL = 1024
BATCH = 1
SEQ = 16384
DEPTH = 2

PLE_DIM = 256
MIX_WIDTH = D_MODEL
GROUP_WIDTH = MIX_WIDTH // 4
HEAD_DIM = 64
GROUP_HEADS = GROUP_WIDTH // HEAD_DIM

GLA_HEADS = GROUP_HEADS
GLA_DK = HEAD_DIM // 2
GLA_DV = GROUP_WIDTH // GLA_HEADS
GLA_GATE_RANK = 16
GLA_GATE_NORMALIZER = 16.0

HGRN_HEADS = GROUP_HEADS
HGRN_EXPAND = HEAD_DIM
HGRN_DV = GROUP_WIDTH // HGRN_HEADS

SWA_Q_HEADS = GROUP_HEADS
SWA_KV_HEADS = GROUP_HEADS // 2
SWA_HEAD_DIM = GROUP_WIDTH // SWA_Q_HEADS
SWA_WINDOW = 128

RWKV_HEADS = GROUP_HEADS
RWKV_N = GROUP_WIDTH // RWKV_HEADS
RWKV_W_RANK = 16
RWKV_A_RANK = 16
RWKV_V_RANK = 8
RWKV_G_RANK = 32

CHUNK = 16
LN_EPS = 1e-5
RMS_EPS = 1e-6
RWKV_GN_EPS = 64e-5

D_FF = 2816
N_EXPERTS = 8
TOP_K = 2
D_FF_EXPERT = 3584
MOE_ROWS = 128

N_DENSE = (DEPTH + 1) // 2
N_MOE = DEPTH // 2
DEEPNORM_ALPHA = (2.0 * DEPTH) ** 0.25
DEEPNORM_BETA = (8.0 * DEPTH) ** -0.25

RWKV_COLS = 3 * GROUP_WIDTH + RWKV_W_RANK + RWKV_A_RANK + RWKV_G_RANK
IN_SIZES = (
    GLA_HEADS * GLA_DK, GLA_HEADS * GLA_DK, GROUP_WIDTH, GROUP_WIDTH, GLA_GATE_RANK,
    HGRN_HEADS * HGRN_EXPAND, HGRN_HEADS * HGRN_EXPAND, GROUP_WIDTH, GROUP_WIDTH,
    SWA_Q_HEADS * SWA_HEAD_DIM, SWA_KV_HEADS * SWA_HEAD_DIM, SWA_KV_HEADS * SWA_HEAD_DIM,
    RWKV_COLS,
)
P_IN = sum(IN_SIZES)

kernel_name = 'hybrid_gla_hgrn2_swa_rwkv7_moe_deepnorm'


def split_cols(z, sizes):
    out, o = [], 0
    for s in sizes:
        out.append(z[..., o:o + s])
        o += s
    return out


def layer_norm(x, g, b):
    xf = x.astype(jnp.float32)
    mu = jnp.mean(xf, axis=-1, keepdims=True)
    var = jnp.mean(jnp.square(xf - mu), axis=-1, keepdims=True)
    return ((xf - mu) * lax.rsqrt(var + LN_EPS) * g + b).astype(x.dtype)


def rms_norm(x, w):
    xf = x.astype(jnp.float32)
    return xf * lax.rsqrt(jnp.mean(jnp.square(xf), axis=-1, keepdims=True) + RMS_EPS) * w


def token_shift_lerp(z, mu):
    prev = jnp.pad(z, ((0, 0), (1, 0), (0, 0)))[:, :-1]
    return z + (prev - z) * mu


def chunk_gated_linear_attn(q, k, v, log_f):
    bsz, seq, nh, dk = q.shape
    dv = v.shape[-1]
    n = seq // CHUNK
    q, k, log_f = (t.reshape(bsz, n, CHUNK, nh, dk) for t in (q, k, log_f))
    v = v.reshape(bsz, n, CHUNK, nh, dv)
    b = jnp.cumsum(log_f, axis=2)
    causal = jnp.tril(jnp.ones((CHUNK, CHUNK), dtype=bool))[None, None, :, :, None, None]
    rel = jnp.exp(jnp.where(causal, b[:, :, :, None] - b[:, :, None, :], -jnp.inf))
    scores = jnp.einsum('bntshk,bnshk->bnhts', q[:, :, :, None] * rel, k)
    o_intra = jnp.einsum('bnhts,bnshv->bnthv', scores, v)
    b_last = b[:, :, -1]
    kv = jnp.einsum('bnshk,bnshv->nbhkv', k * jnp.exp(b_last[:, :, None] - b), v)
    chunk_decay = jnp.moveaxis(jnp.exp(b_last), 1, 0)

    def step(state, inp):
        decay, update = inp
        return state * decay[..., None] + update, state

    s0 = jnp.zeros((bsz, nh, dk, dv), q.dtype)
    _, s_prev = lax.scan(step, s0, (chunk_decay, kv))
    o_inter = jnp.einsum('bnthk,nbhkv->bnthv', q * jnp.exp(b), s_prev)
    return (o_intra + o_inter).reshape(bsz, seq, nh, dv)


def gla_mixer(q, k, v, g, gk_down, gk_up, gk_bias, norm_w):
    bsz, seq, _ = q.shape
    q = q.reshape(bsz, seq, GLA_HEADS, GLA_DK) * GLA_DK ** -0.5
    k = k.reshape(bsz, seq, GLA_HEADS, GLA_DK)
    v = v.reshape(bsz, seq, GLA_HEADS, GLA_DV)
    log_f = jax.nn.log_sigmoid(gk_down @ gk_up + gk_bias) / GLA_GATE_NORMALIZER
    o = chunk_gated_linear_attn(q, k, v, log_f.reshape(bsz, seq, GLA_HEADS, GLA_DK))
    o = rms_norm(o, norm_w) * jax.nn.silu(g.reshape(bsz, seq, GLA_HEADS, GLA_DV))
    return o.reshape(bsz, seq, GROUP_WIDTH)


def hgrn2_mixer(q, f, i_in, g, lb, norm_w):
    bsz, seq, _ = q.shape
    q = jax.nn.silu(q)
    log_f = jnp.logaddexp(jnp.log(lb), jnp.log1p(-lb) + jax.nn.log_sigmoid(f))
    k = (1.0 - lb) * jax.nn.sigmoid(-f)
    hk = lambda t: t.reshape(bsz, seq, HGRN_HEADS, HGRN_EXPAND)
    o = chunk_gated_linear_attn(hk(q), hk(k), i_in.reshape(bsz, seq, HGRN_HEADS, HGRN_DV), hk(log_f))
    o = rms_norm(o, norm_w) * jax.nn.silu(g.reshape(bsz, seq, HGRN_HEADS, HGRN_DV))
    return o.reshape(bsz, seq, GROUP_WIDTH)


def swa_sink_mixer(q, k, v, sinks):
    bsz, seq, _ = q.shape
    w = SWA_WINDOW
    nb = seq // w
    grp = SWA_Q_HEADS // SWA_KV_HEADS
    q = q.reshape(bsz, nb, w, SWA_KV_HEADS, grp, SWA_HEAD_DIM)
    k = k.reshape(bsz, nb, w, SWA_KV_HEADS, SWA_HEAD_DIM)
    v = v.reshape(bsz, nb, w, SWA_KV_HEADS, SWA_HEAD_DIM)
    shift = lambda t: jnp.concatenate([jnp.zeros_like(t[:, :1]), t[:, :-1]], axis=1)
    kw = jnp.concatenate([shift(k), k], axis=2)
    vw = jnp.concatenate([shift(v), v], axis=2)
    s = jnp.einsum('bnqhgd,bnkhd->bnhgqk', q, kw) * SWA_HEAD_DIM ** -0.5
    q_pos = jnp.arange(nb)[:, None] * w + jnp.arange(w)[None, :]
    k_pos = jnp.arange(nb)[:, None] * w - w + jnp.arange(2 * w)[None, :]
    dist = q_pos[:, :, None] - k_pos[:, None, :]
    mask = (dist >= 0) & (dist < w) & (k_pos[:, None, :] >= 0)
    s = jnp.where(mask[None, :, None, None], s, -jnp.inf).astype(jnp.float32)
    sink = jnp.broadcast_to(sinks.astype(jnp.float32).reshape(1, 1, SWA_KV_HEADS, grp, 1, 1), s.shape[:-1] + (1,))
    prob = jax.nn.softmax(jnp.concatenate([s, sink], axis=-1), axis=-1)[..., :-1]
    o = jnp.einsum('bnhgqk,bnkhd->bnqhgd', prob, vw)
    return o.reshape(bsz, seq, GROUP_WIDTH)


def rwkv7_recurrence(r, log_w, k, v, a, b):
    bsz, seq, nh, n = r.shape
    xs = tuple(jnp.moveaxis(t, 1, 0) for t in (r, jnp.exp(log_w), k, v, a, b))

    def step(state, inp):
        r_t, w_t, k_t, v_t, a_t, b_t = inp
        sa = jnp.einsum('bhvk,bhk->bhv', state, a_t)
        state = state * w_t[:, :, None, :] + sa[..., None] * b_t[:, :, None, :] + v_t[..., None] * k_t[:, :, None, :]
        return state, jnp.einsum('bhvk,bhk->bhv', state, r_t)

    _, y = lax.scan(step, jnp.zeros((bsz, nh, n, n), r.dtype), xs)
    return jnp.moveaxis(y, 0, 1)


def rwkv7_mixer(zr, mu, w0, w_up, a0, a_up, g_up, k_k, k_a, r_k, lnx_w, lnx_b, v_res):
    bsz, seq, _ = zr.shape
    zr = token_shift_lerp(zr, mu)
    r, k, v, wd, ad, gd = split_cols(zr, (GROUP_WIDTH, GROUP_WIDTH, GROUP_WIDTH, RWKV_W_RANK, RWKV_A_RANK, RWKV_G_RANK))
    w = -jax.nn.softplus(-(w0 + jnp.tanh(wd) @ w_up)) - 0.5
    log_decay = -jnp.exp(w)
    a = jax.nn.sigmoid(a0 + ad @ a_up)
    g = jax.nn.sigmoid(gd) @ g_up
    v_layer = v
    if v_res is not None:
        v_first, vd, v0, v_up = v_res
        v = v + (v_first - v) * jax.nn.sigmoid(v0 + vd @ v_up)
    heads = lambda t: t.reshape(bsz, seq, RWKV_HEADS, RWKV_N)
    kk = heads(k * k_k)
    kk = kk / jnp.maximum(jnp.sqrt(jnp.sum(jnp.square(kk), axis=-1, keepdims=True)), 1e-12)
    k = k * (1.0 + (a - 1.0) * k_a)
    r_h, k_h, v_h, a_h = heads(r), heads(k), heads(v), heads(a)
    y = rwkv7_recurrence(r_h, heads(log_decay), k_h, v_h, -kk, kk * a_h)
    mu_y = jnp.mean(y, axis=-1, keepdims=True)
    var_y = jnp.mean(jnp.square(y - mu_y), axis=-1, keepdims=True)
    y = ((y - mu_y) * lax.rsqrt(var_y + RWKV_GN_EPS)).reshape(bsz, seq, GROUP_WIDTH) * lnx_w + lnx_b
    bonus = jnp.sum(r_h * k_h * r_k, axis=-1, keepdims=True) * v_h
    y = y + bonus.reshape(bsz, seq, GROUP_WIDTH)
    return y * g, v_layer


def swiglu(x, w_gate, w_up, w_down):
    return (jax.nn.silu(x @ w_gate) * (x @ w_up)) @ w_down


def moe_swiglu(x, router, w_gate, w_up, w_down):
    bsz, seq, d = x.shape
    n_tok = bsz * seq
    n_assign = n_tok * TOP_K
    xt = x.reshape(n_tok, d)
    logits = jnp.matmul(xt, router).astype(jnp.float32)
    top_logit, top_e = lax.top_k(logits, TOP_K)
    gates = jax.nn.softmax(top_logit, axis=-1)
    e_flat = top_e.reshape(-1)
    tok_flat = jnp.arange(n_assign, dtype=jnp.int32) // TOP_K
    order = jnp.argsort(e_flat)
    e_sorted = e_flat[order]
    counts = jnp.bincount(e_flat, length=N_EXPERTS)
    padded = (counts + MOE_ROWS - 1) // MOE_ROWS * MOE_ROWS
    start = jnp.cumsum(counts) - counts
    pad_end = jnp.cumsum(padded)
    pad_start = pad_end - padded
    dest = pad_start[e_sorted] + jnp.arange(n_assign, dtype=jnp.int32) - start[e_sorted]
    n_blocks = -(-n_assign // MOE_ROWS) + N_EXPERTS
    n_rows = n_blocks * MOE_ROWS
    row_tok = jnp.zeros((n_rows,), jnp.int32).at[dest].set(tok_flat[order])
    row_gate = jnp.zeros((n_rows,), jnp.float32).at[dest].set(gates.reshape(-1)[order])
    block_start = jnp.arange(n_blocks, dtype=jnp.int32) * MOE_ROWS
    block_e = jnp.minimum(jnp.searchsorted(pad_end, block_start, side='right'), N_EXPERTS - 1)
    x_rows = xt[row_tok].reshape(n_blocks, MOE_ROWS, d)

    def expert_block(args):
        xb, e = args
        return (jax.nn.silu(xb @ w_gate[e]) * (xb @ w_up[e])) @ w_down[e]

    y_rows = lax.map(expert_block, (x_rows, block_e)).reshape(n_rows, d)
    y = jnp.zeros((n_tok, d), jnp.float32).at[row_tok].add(y_rows * row_gate[:, None])
    return y.astype(x.dtype).reshape(bsz, seq, d)


def setup_inputs(seed: int = 0) -> dict:
    key = jax.random.key(seed)
    keys = jax.random.split(key, 64)
    counter = [0]
    f32 = jnp.float32

    def nxt():
        kk = keys[counter[0]]
        counter[0] += 1
        return kk

    def nrm(shape, scale):
        return jax.random.normal(nxt(), shape, f32) * scale

    def unif(shape, lo, hi):
        return jax.random.uniform(nxt(), shape, f32, lo, hi)

    L, D, GW = DEPTH, D_MODEL, GROUP_WIDTH
    beta = DEEPNORM_BETA
    return {
        'x': nrm((BATCH, SEQ, D), 1.0),
        'p': nrm((L, BATCH, SEQ, PLE_DIM), 1.0),
        'w_in': nrm((L, D, P_IN), D ** -0.5),
        'w_out': nrm((L, MIX_WIDTH, D), beta * MIX_WIDTH ** -0.5),
        'gla_gk_up': nrm((L, GLA_GATE_RANK, GLA_HEADS * GLA_DK), GLA_GATE_RANK ** -0.5),
        'gla_gk_bias': nrm((L, GLA_HEADS * GLA_DK), 0.1),
        'gla_norm_w': 1.0 + nrm((L, GLA_DV), 0.02),
        'hgrn_lower_bounds': nrm((L, HGRN_HEADS * HGRN_EXPAND), 0.1),
        'hgrn_norm_w': 1.0 + nrm((L, HGRN_DV), 0.02),
        'swa_sinks': nrm((L, SWA_Q_HEADS), 1.0),
        'rwkv_mu': unif((L, RWKV_COLS), 0.0, 1.0),
        'rwkv_w0': unif((L, GW), -6.0, -1.0),
        'rwkv_w_up': nrm((L, RWKV_W_RANK, GW), RWKV_W_RANK ** -0.5),
        'rwkv_a0': nrm((L, GW), 0.1),
        'rwkv_a_up': nrm((L, RWKV_A_RANK, GW), RWKV_A_RANK ** -0.5),
        'rwkv_g_up': nrm((L, RWKV_G_RANK, GW), RWKV_G_RANK ** -0.5),
        'rwkv_k_k': 0.85 + nrm((L, GW), 0.02),
        'rwkv_k_a': 1.0 + nrm((L, GW), 0.02),
        'rwkv_r_k': nrm((L, RWKV_HEADS, RWKV_N), 0.1),
        'rwkv_lnx_w': 1.0 + nrm((L, GW), 0.02),
        'rwkv_lnx_b': nrm((L, GW), 0.02),
        'rwkv_vres_down': nrm((L - 1, D, RWKV_V_RANK), D ** -0.5),
        'rwkv_vres_mu': unif((L - 1, RWKV_V_RANK), 0.0, 1.0),
        'rwkv_v0': nrm((L - 1, GW), 0.1),
        'rwkv_vres_up': nrm((L - 1, RWKV_V_RANK, GW), RWKV_V_RANK ** -0.5),
        'ln1_g': 1.0 + nrm((L, D), 0.02),
        'ln1_b': nrm((L, D), 0.02),
        'ln2_g': 1.0 + nrm((L, D), 0.02),
        'ln2_b': nrm((L, D), 0.02),
        'ffn_w_gate': nrm((N_DENSE, D, D_FF), D ** -0.5),
        'ffn_w_up': nrm((N_DENSE, D, D_FF), D ** -0.5),
        'ffn_w_down': nrm((N_DENSE, D_FF, D), beta * D_FF ** -0.5),
        'moe_router': nrm((N_MOE, D, N_EXPERTS), D ** -0.5),
        'moe_w_gate': nrm((N_MOE, N_EXPERTS, D, D_FF_EXPERT), D ** -0.5),
        'moe_w_up': nrm((N_MOE, N_EXPERTS, D, D_FF_EXPERT), D ** -0.5),
        'moe_w_down': nrm((N_MOE, N_EXPERTS, D_FF_EXPERT, D), beta * D_FF_EXPERT ** -0.5),
        'ple_proj': nrm((L, PLE_DIM, D), beta * PLE_DIM ** -0.5),
        'ple_gate': nrm((L, D, D), D ** -0.5),
    }


def reference(x, p, w_in, w_out, gla_gk_up, gla_gk_bias, gla_norm_w, hgrn_lower_bounds, hgrn_norm_w,
              swa_sinks, rwkv_mu, rwkv_w0, rwkv_w_up, rwkv_a0, rwkv_a_up, rwkv_g_up, rwkv_k_k, rwkv_k_a,
              rwkv_r_k, rwkv_lnx_w, rwkv_lnx_b, rwkv_vres_down, rwkv_vres_mu, rwkv_v0, rwkv_vres_up,
              ln1_g, ln1_b, ln2_g, ln2_b, ffn_w_gate, ffn_w_up, ffn_w_down, moe_router, moe_w_gate,
              moe_w_up, moe_w_down, ple_proj, ple_gate):
    f32 = jnp.float32
    lbs = jnp.cumsum(jax.nn.softmax(hgrn_lower_bounds.astype(f32), axis=0), axis=0)
    lbs = lbs - lbs[0]
    v_first = None
    for i in range(DEPTH):
        w = w_in[i] if i == 0 else jnp.concatenate([w_in[i], rwkv_vres_down[i - 1]], axis=1)
        z = jnp.matmul(x, w).astype(f32)
        (gq, gk, gv, gg, ggk, hq, hf, hi, hg, sq, sk, sv, rz) = split_cols(z, IN_SIZES)
        o_gla = gla_mixer(gq, gk, gv, gg, ggk, gla_gk_up[i], gla_gk_bias[i], gla_norm_w[i])
        o_hgrn = hgrn2_mixer(hq, hf, hi, hg, lbs[i], hgrn_norm_w[i])
        o_swa = swa_sink_mixer(sq, sk, sv, swa_sinks[i])
        if i == 0:
            v_res = None
        else:
            vd = token_shift_lerp(z[..., P_IN:], rwkv_vres_mu[i - 1])
            v_res = (v_first, vd, rwkv_v0[i - 1], rwkv_vres_up[i - 1])
        o_rwkv, v_layer = rwkv7_mixer(rz, rwkv_mu[i], rwkv_w0[i], rwkv_w_up[i], rwkv_a0[i], rwkv_a_up[i],
                                      rwkv_g_up[i], rwkv_k_k[i], rwkv_k_a[i], rwkv_r_k[i], rwkv_lnx_w[i],
                                      rwkv_lnx_b[i], v_res)
        if i == 0:
            v_first = v_layer
        mix = jnp.concatenate([o_gla, o_hgrn, o_swa, o_rwkv], axis=-1).astype(x.dtype)
        x = layer_norm(DEEPNORM_ALPHA * x + mix @ w_out[i], ln1_g[i], ln1_b[i])
        j = i // 2
        if i % 2 == 0:
            f = swiglu(x, ffn_w_gate[j], ffn_w_up[j], ffn_w_down[j])
        else:
            f = moe_swiglu(x, moe_router[j], moe_w_gate[j], moe_w_up[j], moe_w_down[j])
        x = layer_norm(DEEPNORM_ALPHA * x + f, ln2_g[i], ln2_b[i])
        x = x + jax.nn.sigmoid(x @ ple_gate[i]) * (p[i] @ ple_proj[i])
    return x
```

```python
import functools

import jax
import jax.numpy as jnp
from jax import lax
from jax.experimental import pallas as pl
from jax.experimental.pallas import tpu as pltpu

F32 = jnp.float32
BF16 = jnp.bfloat16
HIGHEST = lax.Precision.HIGHEST

D_MODEL = 1024
GROUP_WIDTH = 256
N_HEADS = 4
HEAD_DIM = 64
GLA_DK = 32
GLA_GATE_RANK = 16
GLA_GATE_NORMALIZER = 16.0
SWA_WINDOW = 128
RWKV_COLS = 3 * GROUP_WIDTH + 16 + 16 + 32
RWKV_V_RANK = 8
LN_EPS = 1e-5
RMS_EPS = 1e-6
RWKV_GN_EPS = 64e-5
D_FF = 2816
N_EXPERTS = 8
D_FF_EXPERT = 3584
PLE_DIM = 256

LANES = 128
GLA_W = 896
HGRN_W = 1024
SWA_W = 512
RWKV_W = 896
Z_W = GLA_W + HGRN_W + SWA_W + RWKV_W

GLA_SUB = 16
GLA_TILE = 128
RWKV_CHUNK = 64
ROW_TILE = 512
FF_CHUNK = 256
MOE_ROWS = 512
MOE_FF_TILE = 512
COMBINE_TILE = 256
VMEM_LIMIT = 56 * 1024 * 1024


def _iota(shape, dim):
    return lax.broadcasted_iota(jnp.int32, shape, dim)


def _idiv(x, n):
    return jnp.right_shift(x, n.bit_length() - 1)


def _imod(x, n):
    return jnp.bitwise_and(x, n - 1)


def _dot(a, b, precision=None):
    return jnp.dot(a, b, preferred_element_type=F32, precision=precision)


def _dot_nt(a, b, precision=None):
    return lax.dot_general(a, b, (((1,), (1,)), ((), ())), preferred_element_type=F32, precision=precision)


def _dot_tn(a, b, precision=None):
    return lax.dot_general(a, b, (((0,), (0,)), ((), ())), preferred_element_type=F32, precision=precision)


def _bdot(a, b):
    return _dot(a.astype(BF16), b.astype(BF16))


def _bdot_nt(a, b):
    return _dot_nt(a.astype(BF16), b.astype(BF16))


def _sigmoid(x):
    return 1.0 / (1.0 + jnp.exp(-x))


def _silu(x):
    return x * _sigmoid(x)


def _log_sigmoid(x):
    return jnp.minimum(x, 0.0) - jnp.log1p(jnp.exp(-jnp.abs(x)))


def _layer_norm(y, g, b):
    mu = jnp.mean(y, axis=-1, keepdims=True)
    d = y - mu
    var = jnp.mean(d * d, axis=-1, keepdims=True)
    return d * lax.rsqrt(var + LN_EPS) * g + b


def _expand_heads(x, head_width):
    lane_head = _idiv(_iota(x.shape, 1), head_width)
    return jnp.concatenate([jnp.where(lane_head == h, x, 0.0) for h in range(N_HEADS)], axis=0)


def _head_group_matrix(width, head_width, value):
    same = _idiv(_iota((width, width), 0), head_width) == _idiv(_iota((width, width), 1), head_width)
    return jnp.where(same, value, 0.0).astype(F32)


def _resident(shape):
    nd = len(shape)
    return pl.BlockSpec(shape, lambda *_: (0,) * nd, pipeline_mode=pl.Buffered(1))


def _params(semantics):
    return pltpu.CompilerParams(dimension_semantics=semantics, vmem_limit_bytes=VMEM_LIMIT)


def _inproj_kernel(x_ref, w_ref, gla_ref, hgrn_ref, swa_ref, rwkv_ref):
    xb = x_ref[...].astype(BF16)
    o = 0
    for ref, width in ((gla_ref, GLA_W), (hgrn_ref, HGRN_W), (swa_ref, SWA_W), (rwkv_ref, RWKV_W)):
        ref[...] = _dot(xb, w_ref[:, o:o + width])
        o += width


def _in_proj(x, w):
    t = x.shape[0]
    widths = (GLA_W, HGRN_W, SWA_W, RWKV_W)
    return pl.pallas_call(
        _inproj_kernel,
        grid=(t // ROW_TILE,),
        in_specs=[pl.BlockSpec((ROW_TILE, D_MODEL), lambda i: (i, 0)), _resident((D_MODEL, Z_W))],
        out_specs=[pl.BlockSpec((ROW_TILE, w_), lambda i: (i, 0)) for w_ in widths],
        out_shape=[jax.ShapeDtypeStruct((t, w_), F32) for w_ in widths],
        compiler_params=_params(("parallel",)),
        name="in_proj",
    )(x, w)


def _group_in_weights(w_in, vres_down):
    gla, hgrn, swa, rwkv = jnp.split(w_in, (784, 784 + 1024, 784 + 1024 + 512), axis=1)
    if vres_down is not None:
        rwkv = jnp.concatenate([rwkv, vres_down], axis=1)
    pad = lambda a, w_: jnp.pad(a, ((0, 0), (0, w_ - a.shape[1])))
    return jnp.concatenate([pad(gla, GLA_W), hgrn, swa, pad(rwkv, RWKV_W)], axis=1).astype(BF16)


def _gated_linear_attention_tile(q, k, v, log_f, state_ref):
    length, kw = q.shape
    head_k = kw // N_HEADS
    n_sub = length // GLA_SUB
    row = _iota((length, length), 0)
    col = _iota((length, length), 1)
    same_sub = _idiv(row, GLA_SUB) == _idiv(col, GLA_SUB)
    m_local = jnp.where(same_sub & (col <= row), 1.0, 0.0).astype(F32)
    m_prev = jnp.where(_idiv(col, GLA_SUB) < _idiv(row, GLA_SUB), 1.0, 0.0).astype(F32)
    b_local = _dot(m_local, log_f, HIGHEST)
    b_start = _dot(m_prev, log_f, HIGHEST)
    b_full = b_start + b_local
    q_local = q * jnp.exp(b_local)

    key_pos = _iota((length, kw), 0)
    q_pos = _imod(_iota((N_HEADS * GLA_SUB, length), 0), GLA_SUB)
    s_pos = _iota((N_HEADS * GLA_SUB, length), 1)
    probs = []
    for c in range(n_sub):
        r0, r1 = c * GLA_SUB, (c + 1) * GLA_SUB
        expo = jnp.where(key_pos < r0, b_start[r0:r0 + 1, :] - b_full,
                         jnp.where(key_pos < r1, -b_local, -jnp.inf))
        k_ref = k * jnp.exp(expo)
        q_heads = _expand_heads(q_local[r0:r1, :], head_k)
        s = _bdot_nt(q_heads, k_ref)
        probs.append(jnp.where(s_pos <= q_pos + r0, s, 0.0))
    o_heads = _bdot(jnp.concatenate(probs, axis=0), v)
    v_head = _idiv(_iota((GLA_SUB, GROUP_WIDTH), 1), HEAD_DIM)
    rows = []
    for c in range(n_sub):
        base = c * N_HEADS * GLA_SUB
        acc = jnp.zeros((GLA_SUB, GROUP_WIDTH), F32)
        for h in range(N_HEADS):
            acc = acc + jnp.where(v_head == h, o_heads[base + h * GLA_SUB:base + (h + 1) * GLA_SUB, :], 0.0)
        rows.append(acc)
    o = jnp.concatenate(rows, axis=0)

    state_t = state_ref[...]
    o = o + _bdot_nt(q * jnp.exp(b_full), state_t)
    b_total = b_full[length - 1:length, :]
    k_end = k * jnp.exp(b_total - b_full)
    upd = _dot_tn(v.astype(BF16), k_end.astype(BF16))
    same_head = _idiv(_iota((GROUP_WIDTH, kw), 0), HEAD_DIM) == _idiv(_iota((GROUP_WIDTH, kw), 1), head_k)
    state_ref[...] = state_t * jnp.exp(b_total) + jnp.where(same_head, upd, 0.0)
    return o


def _head_rms_gate(o, norm_w, gate):
    ms = _dot(o * o, _head_group_matrix(GROUP_WIDTH, HEAD_DIM, 1.0 / HEAD_DIM), HIGHEST)
    return o * lax.rsqrt(ms + RMS_EPS) * norm_w * _silu(gate)


def _gla_kernel(z_ref, gk_up_ref, gk_bias_ref, norm_w_ref, o_ref, state_ref):
    @pl.when(pl.program_id(0) == 0)
    def _():
        state_ref[...] = jnp.zeros_like(state_ref)

    z = z_ref[...]
    q = z[:, 0:128] * (GLA_DK ** -0.5)
    k = z[:, 128:256]
    v = z[:, 256:512]
    g = z[:, 512:768]
    gate_in = _dot(z[:, 768:896], gk_up_ref[...], HIGHEST) + gk_bias_ref[...]
    log_f = _log_sigmoid(gate_in) * (1.0 / GLA_GATE_NORMALIZER)
    o = _gated_linear_attention_tile(q, k, v, log_f, state_ref)
    o_ref[...] = _head_rms_gate(o, norm_w_ref[...], g)


def _hgrn_kernel(z_ref, lb_ref, log_lb_ref, norm_w_ref, o_ref, state_ref):
    @pl.when(pl.program_id(0) == 0)
    def _():
        state_ref[...] = jnp.zeros_like(state_ref)

    z = z_ref[...]
    q = _silu(z[:, 0:256])
    f = z[:, 256:512]
    v = z[:, 512:768]
    g = z[:, 768:1024]
    lb = lb_ref[...]
    a = log_lb_ref[...]
    c = jnp.log1p(-lb) + _log_sigmoid(f)
    log_f = jnp.maximum(a, c) + jnp.log1p(jnp.exp(-jnp.abs(a - c)))
    k = (1.0 - lb) * _sigmoid(-f)
    o = _gated_linear_attention_tile(q, k, v, log_f, state_ref)
    o_ref[...] = _head_rms_gate(o, norm_w_ref[...], g)


def _gla_mixer(z, gk_up, gk_bias, norm_w):
    t = z.shape[0]
    gk_up_pad = jnp.zeros((LANES, N_HEADS * GLA_DK), F32).at[:GLA_GATE_RANK].set(gk_up)
    return pl.pallas_call(
        _gla_kernel,
        grid=(t // GLA_TILE,),
        in_specs=[pl.BlockSpec((GLA_TILE, GLA_W), lambda i: (i, 0)),
                  _resident((LANES, N_HEADS * GLA_DK)), _resident((1, N_HEADS * GLA_DK)),
                  _resident((1, GROUP_WIDTH))],
        out_specs=pl.BlockSpec((GLA_TILE, GROUP_WIDTH), lambda i: (i, 0)),
        out_shape=jax.ShapeDtypeStruct((t, GROUP_WIDTH), F32),
        scratch_shapes=[pltpu.VMEM((GROUP_WIDTH, N_HEADS * GLA_DK), F32)],
        compiler_params=_params(("arbitrary",)),
        name="gla_mixer",
    )(z, gk_up_pad, gk_bias.reshape(1, -1), jnp.tile(norm_w, N_HEADS).reshape(1, -1))


def _hgrn_mixer(z, lb, norm_w):
    t = z.shape[0]
    return pl.pallas_call(
        _hgrn_kernel,
        grid=(t // GLA_TILE,),
        in_specs=[pl.BlockSpec((GLA_TILE, HGRN_W), lambda i: (i, 0)),
                  _resident((1, GROUP_WIDTH)), _resident((1, GROUP_WIDTH)), _resident((1, GROUP_WIDTH))],
        out_specs=pl.BlockSpec((GLA_TILE, GROUP_WIDTH), lambda i: (i, 0)),
        out_shape=jax.ShapeDtypeStruct((t, GROUP_WIDTH), F32),
        scratch_shapes=[pltpu.VMEM((GROUP_WIDTH, GROUP_WIDTH), F32)],
        compiler_params=_params(("arbitrary",)),
        name="hgrn_mixer",
    )(z, lb.reshape(1, -1), jnp.log(lb).reshape(1, -1), jnp.tile(norm_w, N_HEADS).reshape(1, -1))


def _swa_kernel(q_ref, k_ref, v_ref, kp_ref, vp_ref, sink_ref, o_ref):
    w = SWA_WINDOW
    has_prev = pl.program_id(0) > 0
    q = q_ref[...] * (HEAD_DIM ** -0.5)
    kw = jnp.concatenate([kp_ref[...], k_ref[...]], axis=0)
    vw = jnp.concatenate([vp_ref[...], v_ref[...]], axis=0)
    q_pos = _iota((w, 2 * w), 0) + w
    k_pos = _iota((w, 2 * w), 1)
    dist = q_pos - k_pos
    visible = (dist >= 0) & (dist < w) & ((k_pos >= w) | has_prev)
    sinks = sink_ref[...]
    outs = []
    for h in range(N_HEADS):
        kv = h // 2
        qh = q[:, h * HEAD_DIM:(h + 1) * HEAD_DIM]
        kh = kw[:, kv * HEAD_DIM:(kv + 1) * HEAD_DIM]
        vh = vw[:, kv * HEAD_DIM:(kv + 1) * HEAD_DIM]
        s = jnp.where(visible, _bdot_nt(qh, kh), -jnp.inf)
        sink = sinks[:, h:h + 1]
        m = jnp.maximum(jnp.max(s, axis=-1, keepdims=True), sink)
        p = jnp.exp(s - m)
        denom = jnp.sum(p, axis=-1, keepdims=True) + jnp.exp(sink - m)
        outs.append(_bdot(p, vh) / denom)
    o_ref[...] = jnp.concatenate(outs, axis=-1)


def _swa_mixer(z, sinks):
    t = z.shape[0]
    w = SWA_WINDOW
    prev = lambda col: (lambda i: (jnp.maximum(i - 1, 0), col))
    return pl.pallas_call(
        _swa_kernel,
        grid=(t // w,),
        in_specs=[pl.BlockSpec((w, 256), lambda i: (i, 0)),
                  pl.BlockSpec((w, 128), lambda i: (i, 2)), pl.BlockSpec((w, 128), lambda i: (i, 3)),
                  pl.BlockSpec((w, 128), prev(2)), pl.BlockSpec((w, 128), prev(3)),
                  _resident((1, N_HEADS))],
        out_specs=pl.BlockSpec((w, GROUP_WIDTH), lambda i: (i, 0)),
        out_shape=jax.ShapeDtypeStruct((t, GROUP_WIDTH), F32),
        compiler_params=_params(("parallel",)),
        name="swa_mixer",
    )(z, z, z, z, z, sinks.reshape(1, -1))


def _rwkv_chunk(r, k, v, a_vec, b_vec, log_w, state_ref):
    c = r.shape[0]
    width = r.shape[1]
    tri = jnp.where(_iota((c, c), 1) <= _iota((c, c), 0), 1.0, 0.0).astype(F32)
    p = _dot(tri, log_w, HIGHEST)
    p_total = p[c - 1:c, :]
    decay_in = jnp.exp(p)
    decay_out = jnp.exp(-p)
    decay_end = jnp.exp(p_total - p)
    a_in = a_vec * jnp.exp(p - log_w)
    r_in = r * decay_in
    b_out = b_vec * decay_out
    k_out = k * decay_out
    b_end = b_vec * decay_end
    k_end = k * decay_end

    t_pos = _iota((c, width), 0)
    assert width == N_HEADS * c
    s_pos = _imod(_iota((c, width), 1), c)
    strict = s_pos < t_pos
    incl = s_pos <= t_pos
    expand = lambda x: _expand_heads(x, HEAD_DIM)

    scores = _bdot_nt(jnp.concatenate([a_in, r_in], axis=0),
                      jnp.concatenate([expand(b_out), expand(k_out)], axis=0))
    a_ab = jnp.where(strict, scores[0:c, 0:width], 0.0)
    a_ak = jnp.where(strict, scores[0:c, width:2 * width], 0.0)
    a_rb = jnp.where(incl, scores[c:2 * c, 0:width], 0.0)
    a_rk = jnp.where(incl, scores[c:2 * c, width:2 * width], 0.0)

    t_inv = jnp.where(s_pos == t_pos, 1.0, 0.0) + a_ab
    power = a_ab
    n_factors = (c - 1).bit_length()
    for _ in range(n_factors - 1):
        power = _bdot(power, expand(power))
        t_inv = t_inv + _bdot(t_inv, expand(power))

    v_heads = expand(v)
    x1 = _bdot(a_ak, v_heads)
    sol = _bdot(t_inv, jnp.concatenate([expand(x1), expand(a_in)], axis=1))
    u0 = sol[:, 0:width]
    w_mat = sol[:, width:2 * width]

    state = state_ref[...]
    from_state = _dot_nt(jnp.concatenate([w_mat, r_in], axis=0), state, HIGHEST)
    u = u0 + from_state[0:c]
    y = _bdot(a_rb, expand(u)) + _bdot(a_rk, v_heads) + from_state[c:2 * c]
    upd = _dot_tn(jnp.concatenate([u, v], axis=0), jnp.concatenate([b_end, k_end], axis=0), HIGHEST)
    same_head = _idiv(_iota((width, width), 0), HEAD_DIM) == _idiv(_iota((width, width), 1), HEAD_DIM)
    state_ref[...] = state * jnp.exp(p_total) + jnp.where(same_head, upd, 0.0)
    return y


def _rwkv_kernel(has_vres, *refs):
    if has_vres:
        (z_ref, zp_ref, vfirst_ref, mu_ref, w0_ref, wup_ref, a0_ref, aup_ref, gup_ref, kk_ref, ka_ref,
         rk_ref, lnw_ref, lnb_ref, v0_ref, vup_ref, o_ref, state_ref) = refs
    else:
        (z_ref, zp_ref, mu_ref, w0_ref, wup_ref, a0_ref, aup_ref, gup_ref, kk_ref, ka_ref,
         rk_ref, lnw_ref, lnb_ref, o_ref, vout_ref, state_ref) = refs
    step = pl.program_id(0)

    @pl.when(step == 0)
    def _():
        state_ref[...] = jnp.zeros_like(state_ref)

    z = z_ref[...]
    c = z.shape[0]
    last_prev = jnp.where(step > 0, zp_ref[7:8, :], 0.0)
    prev = jnp.where(_iota(z.shape, 0) == 0, last_prev, pltpu.roll(z, 1, axis=0))
    zr = z + (prev - z) * mu_ref[...]
    r = zr[:, 0:256]
    k = zr[:, 256:512]
    v = zr[:, 512:768]
    low = zr[:, 768:896]
    w_pre = w0_ref[...] + _dot(jnp.tanh(low), wup_ref[...], HIGHEST)
    w_log = -(jnp.maximum(-w_pre, 0.0) + jnp.log1p(jnp.exp(-jnp.abs(w_pre)))) - 0.5
    log_w = -jnp.exp(w_log)
    a = _sigmoid(a0_ref[...] + _dot(low, aup_ref[...], HIGHEST))
    g = _dot(_sigmoid(low), gup_ref[...], HIGHEST)
    if has_vres:
        v = v + (vfirst_ref[...] - v) * _sigmoid(v0_ref[...] + _dot(low, vup_ref[...], HIGHEST))
    else:
        vout_ref[...] = v
    head_sum = _head_group_matrix(GROUP_WIDTH, HEAD_DIM, 1.0)
    kk = k * kk_ref[...]
    kk = kk / jnp.maximum(jnp.sqrt(_dot(kk * kk, head_sum, HIGHEST)), 1e-12)
    k = k * (1.0 + (a - 1.0) * ka_ref[...])

    y = _rwkv_chunk(r, k, v, -kk, kk * a, log_w, state_ref)

    head_mean = head_sum * (1.0 / HEAD_DIM)
    mu_y = _dot(y, head_mean, HIGHEST)
    d = y - mu_y
    var_y = _dot(d * d, head_mean, HIGHEST)
    y = d * lax.rsqrt(var_y + RWKV_GN_EPS) * lnw_ref[...] + lnb_ref[...]
    bonus = _dot(r * k * rk_ref[...], head_sum, HIGHEST) * v
    o_ref[...] = (y + bonus) * g


def _rwkv_mixer(z, mu, w0, w_up, a0, a_up, g_up, k_k, k_a, r_k, lnx_w, lnx_b, vres):
    t = z.shape[0]
    c = RWKV_CHUNK
    row = lambda a: a.reshape(1, -1)
    low_rows = lambda a, start: jnp.zeros((LANES, GROUP_WIDTH), F32).at[start:start + a.shape[0]].set(a)
    has_vres = vres is not None
    mu_full = jnp.zeros((RWKV_W,), F32).at[:RWKV_COLS].set(mu)
    vec = _resident((1, GROUP_WIDTH))
    mat = _resident((LANES, GROUP_WIDTH))
    tile = pl.BlockSpec((c, GROUP_WIDTH), lambda i: (i, 0))
    z_specs = [pl.BlockSpec((c, RWKV_W), lambda i: (i, 0)),
               pl.BlockSpec((8, RWKV_W), lambda i: (jnp.maximum(i * (c // 8) - 1, 0), 0))]
    common = [row(w0), low_rows(w_up, 0), row(a0), low_rows(a_up, 16), low_rows(g_up, 32),
              row(k_k), row(k_a), row(r_k), row(lnx_w), row(lnx_b)]
    common_specs = [vec, mat, vec, mat, mat, vec, vec, vec, vec, vec]
    if has_vres:
        v_first, vres_mu, v0, v_up = vres
        mu_full = mu_full.at[RWKV_COLS:RWKV_COLS + RWKV_V_RANK].set(vres_mu)
        args = [z, z, v_first, row(mu_full)] + common + [row(v0), low_rows(v_up, 64)]
        in_specs = z_specs + [tile, _resident((1, RWKV_W))] + common_specs + [vec, mat]
        out_specs = tile
        out_shape = jax.ShapeDtypeStruct((t, GROUP_WIDTH), F32)
    else:
        args = [z, z, row(mu_full)] + common
        in_specs = z_specs + [_resident((1, RWKV_W))] + common_specs
        out_specs = [tile, tile]
        out_shape = [jax.ShapeDtypeStruct((t, GROUP_WIDTH), F32)] * 2
    return pl.pallas_call(
        functools.partial(_rwkv_kernel, has_vres),
        grid=(t // c,),
        in_specs=in_specs,
        out_specs=out_specs,
        out_shape=out_shape,
        scratch_shapes=[pltpu.VMEM((GROUP_WIDTH, GROUP_WIDTH), F32)],
        compiler_params=_params(("arbitrary",)),
        name="rwkv_mixer",
    )(*args)


def _outproj_kernel(alpha, x_ref, o0_ref, o1_ref, o2_ref, o3_ref, w_ref, g_ref, b_ref, y_ref):
    acc = alpha * x_ref[...]
    for h, ref in enumerate((o0_ref, o1_ref, o2_ref, o3_ref)):
        acc = acc + _dot(ref[...].astype(BF16), w_ref[h * GROUP_WIDTH:(h + 1) * GROUP_WIDTH, :])
    y_ref[...] = _layer_norm(acc, g_ref[...], b_ref[...])


def _out_proj_ln(alpha, x, mixes, w_out, g, b):
    t = x.shape[0]
    row_d = pl.BlockSpec((ROW_TILE, D_MODEL), lambda i: (i, 0))
    row_g = pl.BlockSpec((ROW_TILE, GROUP_WIDTH), lambda i: (i, 0))
    return pl.pallas_call(
        functools.partial(_outproj_kernel, alpha),
        grid=(t // ROW_TILE,),
        in_specs=[row_d, row_g, row_g, row_g, row_g, _resident((D_MODEL, D_MODEL)),
                  _resident((1, D_MODEL)), _resident((1, D_MODEL))],
        out_specs=row_d,
        out_shape=jax.ShapeDtypeStruct((t, D_MODEL), F32),
        compiler_params=_params(("parallel",)),
        name="out_proj_ln",
    )(x, *mixes, w_out.astype(BF16), g.reshape(1, -1), b.reshape(1, -1))


def _ln_embed(y, ln_g, ln_b, p, ple_gate, ple_proj):
    x = _layer_norm(y, ln_g, ln_b)
    gate = _sigmoid(_dot(x.astype(BF16), ple_gate))
    return x + gate * _dot(p.astype(BF16), ple_proj)


def _dense_ffn_kernel(alpha, x_ref, p_ref, wg_ref, wu_ref, wd_ref, g_ref, b_ref, pg_ref, pp_ref, y_ref, acc_ref):
    x = x_ref[...]
    xb = x.astype(BF16)
    acc_ref[...] = alpha * x
    for j in range(D_FF // FF_CHUNK):
        cols = slice(j * FF_CHUNK, (j + 1) * FF_CHUNK)
        h = _silu(_dot(xb, wg_ref[:, cols])) * _dot(xb, wu_ref[:, cols])
        acc_ref[...] += _dot(h.astype(BF16), wd_ref[cols, :])
    y_ref[...] = _ln_embed(acc_ref[...], g_ref[...], b_ref[...], p_ref[...], pg_ref[...], pp_ref[...])


def _dense_ffn_tail(alpha, x, p, w_gate, w_up, w_down, g, b, ple_gate, ple_proj):
    t = x.shape[0]
    row_d = pl.BlockSpec((ROW_TILE, D_MODEL), lambda i: (i, 0))
    return pl.pallas_call(
        functools.partial(_dense_ffn_kernel, alpha),
        grid=(t // ROW_TILE,),
        in_specs=[row_d, pl.BlockSpec((ROW_TILE, PLE_DIM), lambda i: (i, 0)),
                  _resident((D_MODEL, D_FF)), _resident((D_MODEL, D_FF)), _resident((D_FF, D_MODEL)),
                  _resident((1, D_MODEL)), _resident((1, D_MODEL)),
                  _resident((D_MODEL, D_MODEL)), _resident((PLE_DIM, D_MODEL))],
        out_specs=row_d,
        out_shape=jax.ShapeDtypeStruct((t, D_MODEL), F32),
        scratch_shapes=[pltpu.VMEM((ROW_TILE, D_MODEL), F32)],
        compiler_params=_params(("parallel",)),
        name="dense_ffn_tail",
    )(x, p, w_gate.astype(BF16), w_up.astype(BF16), w_down.astype(BF16), g.reshape(1, -1), b.reshape(1, -1),
      ple_gate.astype(BF16), ple_proj.astype(BF16))


def _router_kernel(x_ref, w_ref, o_ref):
    logits = _dot(x_ref[...], w_ref[...], HIGHEST)
    lane = _iota(logits.shape, 1).astype(F32)
    logits = jnp.where(lane < N_EXPERTS, logits, -jnp.inf)
    m1 = jnp.max(logits, axis=-1, keepdims=True)
    i1 = jnp.min(jnp.where(logits == m1, lane, LANES), axis=-1, keepdims=True)
    rest = jnp.where(lane == i1, -jnp.inf, logits)
    m2 = jnp.max(rest, axis=-1, keepdims=True)
    i2 = jnp.min(jnp.where(rest == m2, lane, LANES), axis=-1, keepdims=True)
    e2 = jnp.exp(m2 - m1)
    g1 = 1.0 / (1.0 + e2)
    g2 = e2 * g1
    o_ref[...] = jnp.where(lane == 0, i1, jnp.where(lane == 1, i2,
                           jnp.where(lane == 2, g1, jnp.where(lane == 3, g2, 0.0))))


def _route(x, router):
    t = x.shape[0]
    w = jnp.zeros((D_MODEL, LANES), F32).at[:, :N_EXPERTS].set(router)
    out = pl.pallas_call(
        _router_kernel,
        grid=(t // ROW_TILE,),
        in_specs=[pl.BlockSpec((ROW_TILE, D_MODEL), lambda i: (i, 0)), _resident((D_MODEL, LANES))],
        out_specs=pl.BlockSpec((ROW_TILE, LANES), lambda i: (i, 0)),
        out_shape=jax.ShapeDtypeStruct((t, LANES), F32),
        compiler_params=_params(("parallel",)),
        name="moe_router",
    )(x, w)
    return out[:, 0:2].astype(jnp.int32), out[:, 2:4]


def _expert_kernel(row_tok_ref, block_e_ref, x_hbm, wg_ref, wu_ref, wd_ref, y_ref, rows_ref, acc_ref, sem):
    i = pl.program_id(0)
    j = pl.program_id(1)
    row_copy = lambda tok, r: pltpu.make_async_copy(x_hbm.at[pl.ds(tok, 1)], rows_ref.at[pl.ds(r, 1)], sem)

    @pl.when(j == 0)
    def _():
        base = i * MOE_ROWS

        def start(r, carry):
            row_copy(row_tok_ref[base + r], r).start()
            return carry

        lax.fori_loop(0, MOE_ROWS, start, 0)

        def wait(r, carry):
            row_copy(0, r).wait()
            return carry

        lax.fori_loop(0, MOE_ROWS, wait, 0)
        acc_ref[...] = jnp.zeros_like(acc_ref)

    xb = rows_ref[...].astype(BF16)
    h = _silu(_dot(xb, wg_ref[0])) * _dot(xb, wu_ref[0])
    acc_ref[...] += _dot(h.astype(BF16), wd_ref[0])

    @pl.when(j == pl.num_programs(1) - 1)
    def _():
        y_ref[...] = acc_ref[...]


def _expert_rows(x, row_tok, block_e, w_gate, w_up, w_down):
    n_rows = row_tok.shape[0]
    grid_spec = pltpu.PrefetchScalarGridSpec(
        num_scalar_prefetch=2,
        grid=(n_rows // MOE_ROWS, D_FF_EXPERT // MOE_FF_TILE),
        in_specs=[pl.BlockSpec(memory_space=pl.ANY),
                  pl.BlockSpec((1, D_MODEL, MOE_FF_TILE), lambda i, j, tok, be: (be[i], 0, j)),
                  pl.BlockSpec((1, D_MODEL, MOE_FF_TILE), lambda i, j, tok, be: (be[i], 0, j)),
                  pl.BlockSpec((1, MOE_FF_TILE, D_MODEL), lambda i, j, tok, be: (be[i], j, 0))],
        out_specs=pl.BlockSpec((MOE_ROWS, D_MODEL), lambda i, j, tok, be: (i, 0)),
        scratch_shapes=[pltpu.VMEM((MOE_ROWS, D_MODEL), F32), pltpu.VMEM((MOE_ROWS, D_MODEL), F32),
                        pltpu.SemaphoreType.DMA(())],
    )
    return pl.pallas_call(
        _expert_kernel,
        grid_spec=grid_spec,
        out_shape=jax.ShapeDtypeStruct((n_rows, D_MODEL), F32),
        compiler_params=_params(("arbitrary", "arbitrary")),
        name="moe_experts",
    )(row_tok, block_e, x, w_gate.astype(BF16), w_up.astype(BF16), w_down.astype(BF16))


def _combine_kernel(alpha, pos_ref, x_ref, gates_ref, p_ref, y_hbm, g_ref, b_ref, pg_ref, pp_ref, o_ref,
                    buf_ref, sem):
    i = pl.program_id(0)
    n = COMBINE_TILE
    row_copy = lambda src, k, r: pltpu.make_async_copy(y_hbm.at[pl.ds(src, 1)], buf_ref.at[k, pl.ds(r, 1)], sem)
    base = i * n

    def start(r, carry):
        row_copy(pos_ref[2 * (base + r)], 0, r).start()
        row_copy(pos_ref[2 * (base + r) + 1], 1, r).start()
        return carry

    lax.fori_loop(0, n, start, 0)

    def wait(r, carry):
        row_copy(0, 0, r).wait()
        row_copy(0, 1, r).wait()
        return carry

    lax.fori_loop(0, n, wait, 0)
    gates = gates_ref[...]
    f = buf_ref[0] * gates[:, 0:1] + buf_ref[1] * gates[:, 1:2]
    o_ref[...] = _ln_embed(alpha * x_ref[...] + f, g_ref[...], b_ref[...], p_ref[...], pg_ref[...], pp_ref[...])


def _moe_combine_tail(alpha, x, p, y_rows, pos, gates, g, b, ple_gate, ple_proj):
    t = x.shape[0]
    n = COMBINE_TILE
    res = lambda shape: pl.BlockSpec(shape, lambda i, pos_: (0,) * len(shape), pipeline_mode=pl.Buffered(1))
    grid_spec = pltpu.PrefetchScalarGridSpec(
        num_scalar_prefetch=1,
        grid=(t // n,),
        in_specs=[pl.BlockSpec((n, D_MODEL), lambda i, pos_: (i, 0)),
                  pl.BlockSpec((n, 2), lambda i, pos_: (i, 0)),
                  pl.BlockSpec((n, PLE_DIM), lambda i, pos_: (i, 0)),
                  pl.BlockSpec(memory_space=pl.ANY),
                  res((1, D_MODEL)), res((1, D_MODEL)), res((D_MODEL, D_MODEL)), res((PLE_DIM, D_MODEL))],
        out_specs=pl.BlockSpec((n, D_MODEL), lambda i, pos_: (i, 0)),
        scratch_shapes=[pltpu.VMEM((2, n, D_MODEL), F32), pltpu.SemaphoreType.DMA(())],
    )
    return pl.pallas_call(
        functools.partial(_combine_kernel, alpha),
        grid_spec=grid_spec,
        out_shape=jax.ShapeDtypeStruct((t, D_MODEL), F32),
        compiler_params=_params(("arbitrary",)),
        name="moe_combine_tail",
    )(pos.reshape(-1), x, gates, p, y_rows, g.reshape(1, -1), b.reshape(1, -1),
      ple_gate.astype(BF16), ple_proj.astype(BF16))


def _moe_tail(alpha, x, p, router, w_gate, w_up, w_down, g, b, ple_gate, ple_proj):
    t = x.shape[0]
    experts, gates = _route(x, router)
    e_flat = experts.reshape(-1)
    onehot = (e_flat[:, None] == jnp.arange(N_EXPERTS, dtype=jnp.int32)[None, :]).astype(jnp.int32)
    running = jnp.cumsum(onehot, axis=0)
    rank = jnp.sum(onehot * (running - 1), axis=1)
    counts = running[-1]
    padded = (counts + MOE_ROWS - 1) // MOE_ROWS * MOE_ROWS
    pad_end = jnp.cumsum(padded)
    pad_start = pad_end - padded
    dest = (pad_start[e_flat] + rank).astype(jnp.int32)
    n_blocks = (2 * t) // MOE_ROWS + N_EXPERTS
    n_rows = n_blocks * MOE_ROWS
    row_tok = jnp.zeros((n_rows,), jnp.int32).at[dest].set(jnp.arange(2 * t, dtype=jnp.int32) // 2)
    block_start = jnp.arange(n_blocks, dtype=jnp.int32) * MOE_ROWS
    block_e = jnp.minimum(jnp.searchsorted(pad_end, block_start, side="right"), N_EXPERTS - 1).astype(jnp.int32)
    y_rows = _expert_rows(x, row_tok, block_e, w_gate, w_up, w_down)
    return _moe_combine_tail(alpha, x, p, y_rows, dest.reshape(t, 2), gates, g, b, ple_gate, ple_proj)


def kernel(x, p, w_in, w_out, gla_gk_up, gla_gk_bias, gla_norm_w, hgrn_lower_bounds, hgrn_norm_w, swa_sinks, rwkv_mu, rwkv_w0, rwkv_w_up, rwkv_a0, rwkv_a_up, rwkv_g_up, rwkv_k_k, rwkv_k_a, rwkv_r_k, rwkv_lnx_w, rwkv_lnx_b, rwkv_vres_down, rwkv_vres_mu, rwkv_v0, rwkv_vres_up, ln1_g, ln1_b, ln2_g, ln2_b, ffn_w_gate, ffn_w_up, ffn_w_down, moe_router, moe_w_gate, moe_w_up, moe_w_down, ple_proj, ple_gate):
    bsz, seq, d = x.shape
    depth = w_in.shape[0]
    alpha = (2.0 * depth) ** 0.25
    lbs = jnp.cumsum(jax.nn.softmax(hgrn_lower_bounds.astype(F32), axis=0), axis=0)
    lbs = lbs - lbs[0]
    outs = []
    for bi in range(bsz):
        xt = x[bi]
        v_first = None
        for i in range(depth):
            w = _group_in_weights(w_in[i], None if i == 0 else rwkv_vres_down[i - 1])
            z_gla, z_hgrn, z_swa, z_rwkv = _in_proj(xt, w)
            o_gla = _gla_mixer(z_gla, gla_gk_up[i], gla_gk_bias[i], gla_norm_w[i])
            o_hgrn = _hgrn_mixer(z_hgrn, lbs[i], hgrn_norm_w[i])
            o_swa = _swa_mixer(z_swa, swa_sinks[i])
            vres = None if i == 0 else (v_first, rwkv_vres_mu[i - 1], rwkv_v0[i - 1], rwkv_vres_up[i - 1])
            rw = _rwkv_mixer(z_rwkv, rwkv_mu[i], rwkv_w0[i], rwkv_w_up[i], rwkv_a0[i], rwkv_a_up[i],
                             rwkv_g_up[i], rwkv_k_k[i], rwkv_k_a[i], rwkv_r_k[i].reshape(-1),
                             rwkv_lnx_w[i], rwkv_lnx_b[i], vres)
            if i == 0:
                o_rwkv, v_first = rw
            else:
                o_rwkv = rw
            xt = _out_proj_ln(alpha, xt, (o_gla, o_hgrn, o_swa, o_rwkv), w_out[i], ln1_g[i], ln1_b[i])
            j = i // 2
            if i % 2 == 0:
                xt = _dense_ffn_tail(alpha, xt, p[i, bi], ffn_w_gate[j], ffn_w_up[j], ffn_w_down[j],
                                     ln2_g[i], ln2_b[i], ple_gate[i], ple_proj[i])
            else:
                xt = _moe_tail(alpha, xt, p[i, bi], moe_router[j], moe_w_gate[j], moe_w_up[j], moe_w_down[j],
                               ln2_g[i], ln2_b[i], ple_gate[i], ple_proj[i])
        outs.append(xt)
    return jnp.stack(outs, axis=0)
```

```python
import functools

import jax
import jax.numpy as jnp
from jax import lax
from jax.experimental import pallas as pl
from jax.experimental.pallas import tpu as pltpu

F32 = jnp.float32
BF16 = jnp.bfloat16
HIGHEST = lax.Precision.HIGHEST

D_MODEL = 1024
GROUP_WIDTH = 256
N_HEADS = 4
HEAD_DIM = 64
GLA_DK = 32
GLA_GATE_RANK = 16
GLA_GATE_NORMALIZER = 16.0
SWA_WINDOW = 128
RWKV_COLS = 3 * GROUP_WIDTH + 16 + 16 + 32
RWKV_V_RANK = 8
LN_EPS = 1e-5
RMS_EPS = 1e-6
RWKV_GN_EPS = 64e-5
D_FF = 2816
N_EXPERTS = 8
D_FF_EXPERT = 3584
PLE_DIM = 256

LANES = 128
GLA_W = 896
HGRN_W = 1024
SWA_W = 512
RWKV_W = 896
Z_W = GLA_W + HGRN_W + SWA_W + RWKV_W

GLA_SUB = 16
GLA_TILE = 128
RWKV_CHUNK = 64
RWKV_TILE = 256
ROW_TILE = 512
FF_CHUNK = 256
SWA_TILE = 128
MOE_ROWS = 512
VMEM_LIMIT = 56 * 1024 * 1024


def _iota(shape, dim):
    return lax.broadcasted_iota(jnp.int32, shape, dim)


def _idiv(x, n):
    return jnp.right_shift(x, n.bit_length() - 1)


def _imod(x, n):
    return jnp.bitwise_and(x, n - 1)


def _dot(a, b, precision=None):
    return jnp.dot(a, b, preferred_element_type=F32, precision=precision)


def _dot_nt(a, b, precision=None):
    return lax.dot_general(a, b, (((1,), (1,)), ((), ())), preferred_element_type=F32, precision=precision)


def _dot_tn(a, b, precision=None):
    return lax.dot_general(a, b, (((0,), (0,)), ((), ())), preferred_element_type=F32, precision=precision)


def _bdot(a, b):
    return _dot(a.astype(BF16), b.astype(BF16))


def _bdot_nt(a, b):
    return _dot_nt(a.astype(BF16), b.astype(BF16))


def _dot_hilo(x, m):
    hi = x.astype(BF16)
    lo = (x - hi.astype(F32)).astype(BF16)
    return _dot(hi, m) + _dot(lo, m)


def _sigmoid(x):
    return 1.0 / (1.0 + jnp.exp(-x))


def _silu(x):
    return x * _sigmoid(x)


def _log_sigmoid(x):
    return jnp.minimum(x, 0.0) - jnp.log1p(jnp.exp(-jnp.abs(x)))


def _layer_norm(y, g, b):
    mu = jnp.mean(y, axis=-1, keepdims=True)
    d = y - mu
    var = jnp.mean(d * d, axis=-1, keepdims=True)
    return d * lax.rsqrt(var + LN_EPS) * g + b


def _expand_heads(x, head_width):
    lane_head = _idiv(_iota(x.shape, 1), head_width)
    return jnp.concatenate([jnp.where(lane_head == h, x, 0.0) for h in range(N_HEADS)], axis=0)


def _head_group_matrix(width, head_width, value):
    same = _idiv(_iota((width, width), 0), head_width) == _idiv(_iota((width, width), 1), head_width)
    return jnp.where(same, value, 0.0).astype(F32)


def _resident(shape):
    nd = len(shape)
    return pl.BlockSpec(shape, lambda *_: (0,) * nd, pipeline_mode=pl.Buffered(1))


def _params(semantics):
    return pltpu.CompilerParams(dimension_semantics=semantics, vmem_limit_bytes=VMEM_LIMIT)


def _inproj_kernel(x_ref, w_ref, gla_ref, hgrn_ref, swa_ref, rwkv_ref):
    xb = x_ref[...].astype(BF16)
    o = 0
    for ref, width in ((gla_ref, GLA_W), (hgrn_ref, HGRN_W), (swa_ref, SWA_W), (rwkv_ref, RWKV_W)):
        ref[...] = _dot(xb, w_ref[:, o:o + width])
        o += width


def _in_proj(x, w):
    t = x.shape[0]
    widths = (GLA_W, HGRN_W, SWA_W, RWKV_W)
    return pl.pallas_call(
        _inproj_kernel,
        grid=(t // ROW_TILE,),
        in_specs=[pl.BlockSpec((ROW_TILE, D_MODEL), lambda i: (i, 0)), _resident((D_MODEL, Z_W))],
        out_specs=[pl.BlockSpec((ROW_TILE, w_), lambda i: (i, 0)) for w_ in widths],
        out_shape=[jax.ShapeDtypeStruct((t, w_), F32) for w_ in widths],
        compiler_params=_params(("parallel",)),
        name="in_proj",
    )(x, w)


def _group_in_weights(w_in, vres_down):
    gla, hgrn, swa, rwkv = jnp.split(w_in, (784, 784 + 1024, 784 + 1024 + 512), axis=1)
    if vres_down is not None:
        rwkv = jnp.concatenate([rwkv, vres_down], axis=1)
    pad = lambda a, w_: jnp.pad(a, ((0, 0), (0, w_ - a.shape[1])))
    return jnp.concatenate([pad(gla, GLA_W), hgrn, swa, pad(rwkv, RWKV_W)], axis=1).astype(BF16)


def _gated_linear_attention_tile(q, k, v, log_f, state_ref):
    length, kw = q.shape
    head_k = kw // N_HEADS
    n_sub = length // GLA_SUB
    row = _iota((length, length), 0)
    col = _iota((length, length), 1)
    same_sub = _idiv(row, GLA_SUB) == _idiv(col, GLA_SUB)
    m_local = jnp.where(same_sub & (col <= row), 1.0, 0.0).astype(F32)
    m_prev = jnp.where(_idiv(col, GLA_SUB) < _idiv(row, GLA_SUB), 1.0, 0.0).astype(F32)
    b_local = _dot(m_local, log_f, HIGHEST)
    b_start = _dot(m_prev, log_f, HIGHEST)
    b_full = b_start + b_local
    q_local = q * jnp.exp(b_local)

    q_pos = _imod(_iota((N_HEADS * GLA_SUB, length), 0), GLA_SUB)
    s_pos = _iota((N_HEADS * GLA_SUB, length), 1)
    probs = []
    for c in range(n_sub):
        r0, r1 = c * GLA_SUB, (c + 1) * GLA_SUB
        expo = jnp.where(_iota((r1, kw), 0) < r0, b_start[r0:r0 + 1, :] - b_full[0:r1], -b_local[0:r1])
        k_ref = k[0:r1] * jnp.exp(expo)
        if r1 < length:
            k_ref = jnp.concatenate([k_ref, jnp.zeros((length - r1, kw), F32)], axis=0)
        q_heads = _expand_heads(q_local[r0:r1, :], head_k)
        s = _bdot_nt(q_heads, k_ref)
        probs.append(jnp.where(s_pos <= q_pos + r0, s, 0.0))
    o_heads = _bdot(jnp.concatenate(probs, axis=0), v)
    v_head = _idiv(_iota((GLA_SUB, GROUP_WIDTH), 1), HEAD_DIM)
    rows = []
    for c in range(n_sub):
        base = c * N_HEADS * GLA_SUB
        acc = jnp.zeros((GLA_SUB, GROUP_WIDTH), F32)
        for h in range(N_HEADS):
            acc = acc + jnp.where(v_head == h, o_heads[base + h * GLA_SUB:base + (h + 1) * GLA_SUB, :], 0.0)
        rows.append(acc)
    o = jnp.concatenate(rows, axis=0)

    state_t = state_ref[...]
    o = o + _bdot_nt(q * jnp.exp(b_full), state_t)
    b_total = b_full[length - 1:length, :]
    k_end = k * jnp.exp(b_total - b_full)
    upd = _dot_tn(v.astype(BF16), k_end.astype(BF16))
    same_head = _idiv(_iota((GROUP_WIDTH, kw), 0), HEAD_DIM) == _idiv(_iota((GROUP_WIDTH, kw), 1), head_k)
    state_ref[...] = state_t * jnp.exp(b_total) + jnp.where(same_head, upd, 0.0)
    return o


def _head_rms_gate(o, norm_w, gate):
    ms = _dot_hilo(o * o, _head_group_matrix(GROUP_WIDTH, HEAD_DIM, 1.0 / HEAD_DIM).astype(BF16))
    return o * lax.rsqrt(ms + RMS_EPS) * norm_w * _silu(gate)


def _gla_kernel(z_ref, gk_up_ref, gk_bias_ref, norm_w_ref, o_ref, state_ref):
    @pl.when(pl.program_id(0) == 0)
    def _():
        state_ref[...] = jnp.zeros_like(state_ref)

    z = z_ref[...]
    q = z[:, 0:128] * (GLA_DK ** -0.5)
    k = z[:, 128:256]
    v = z[:, 256:512]
    g = z[:, 512:768]
    gate_in = _dot(z[:, 768:896], gk_up_ref[...], HIGHEST) + gk_bias_ref[...]
    log_f = _log_sigmoid(gate_in) * (1.0 / GLA_GATE_NORMALIZER)
    o = _gated_linear_attention_tile(q, k, v, log_f, state_ref)
    o_ref[...] = _head_rms_gate(o, norm_w_ref[...], g)


def _hgrn_kernel(z_ref, lb_ref, log_lb_ref, norm_w_ref, o_ref, state_ref):
    @pl.when(pl.program_id(0) == 0)
    def _():
        state_ref[...] = jnp.zeros_like(state_ref)

    z = z_ref[...]
    q = _silu(z[:, 0:256])
    f = z[:, 256:512]
    v = z[:, 512:768]
    g = z[:, 768:1024]
    lb = lb_ref[...]
    a = log_lb_ref[...]
    c = jnp.log1p(-lb) + _log_sigmoid(f)
    log_f = jnp.maximum(a, c) + jnp.log1p(jnp.exp(-jnp.abs(a - c)))
    k = (1.0 - lb) * _sigmoid(-f)
    o = _gated_linear_attention_tile(q, k, v, log_f, state_ref)
    o_ref[...] = _head_rms_gate(o, norm_w_ref[...], g)


def _gla_mixer(z, gk_up, gk_bias, norm_w):
    t = z.shape[0]
    gk_up_pad = jnp.zeros((LANES, N_HEADS * GLA_DK), F32).at[:GLA_GATE_RANK].set(gk_up)
    return pl.pallas_call(
        _gla_kernel,
        grid=(t // GLA_TILE,),
        in_specs=[pl.BlockSpec((GLA_TILE, GLA_W), lambda i: (i, 0)),
                  _resident((LANES, N_HEADS * GLA_DK)), _resident((1, N_HEADS * GLA_DK)),
                  _resident((1, GROUP_WIDTH))],
        out_specs=pl.BlockSpec((GLA_TILE, GROUP_WIDTH), lambda i: (i, 0)),
        out_shape=jax.ShapeDtypeStruct((t, GROUP_WIDTH), F32),
        scratch_shapes=[pltpu.VMEM((GROUP_WIDTH, N_HEADS * GLA_DK), F32)],
        compiler_params=_params(("arbitrary",)),
        name="gla_mixer",
    )(z, gk_up_pad, gk_bias.reshape(1, -1), jnp.tile(norm_w, N_HEADS).reshape(1, -1))


def _hgrn_mixer(z, lb, norm_w):
    t = z.shape[0]
    return pl.pallas_call(
        _hgrn_kernel,
        grid=(t // GLA_TILE,),
        in_specs=[pl.BlockSpec((GLA_TILE, HGRN_W), lambda i: (i, 0)),
                  _resident((1, GROUP_WIDTH)), _resident((1, GROUP_WIDTH)), _resident((1, GROUP_WIDTH))],
        out_specs=pl.BlockSpec((GLA_TILE, GROUP_WIDTH), lambda i: (i, 0)),
        out_shape=jax.ShapeDtypeStruct((t, GROUP_WIDTH), F32),
        scratch_shapes=[pltpu.VMEM((GROUP_WIDTH, GROUP_WIDTH), F32)],
        compiler_params=_params(("arbitrary",)),
        name="hgrn_mixer",
    )(z, lb.reshape(1, -1), jnp.log(lb).reshape(1, -1), jnp.tile(norm_w, N_HEADS).reshape(1, -1))


def _swa_kernel(q_ref, k_ref, v_ref, kp_ref, vp_ref, sink_ref, o_ref):
    w = SWA_WINDOW
    n = SWA_TILE
    has_prev = pl.program_id(0) > 0
    q = q_ref[...] * (HEAD_DIM ** -0.5)
    kw = jnp.concatenate([kp_ref[...], k_ref[...]], axis=0)
    vw = jnp.concatenate([vp_ref[...], v_ref[...]], axis=0)
    q_pos = _iota((n, w + n), 0) + w
    k_pos = _iota((n, w + n), 1)
    dist = q_pos - k_pos
    visible = (dist >= 0) & (dist < w) & ((k_pos >= w) | has_prev)
    sinks = sink_ref[...]
    outs = []
    for kv in range(N_HEADS // 2):
        heads = (2 * kv, 2 * kv + 1)
        qs = jnp.concatenate([q[:, h * HEAD_DIM:(h + 1) * HEAD_DIM] for h in heads], axis=0)
        kh = kw[:, kv * HEAD_DIM:(kv + 1) * HEAD_DIM]
        vh = vw[:, kv * HEAD_DIM:(kv + 1) * HEAD_DIM]
        scores = _bdot_nt(qs, kh)
        probs, denoms = [], []
        for half, h in enumerate(heads):
            s = jnp.where(visible, scores[half * n:(half + 1) * n], -jnp.inf)
            sink = sinks[:, h:h + 1]
            m = jnp.maximum(jnp.max(s, axis=-1, keepdims=True), sink)
            p = jnp.exp(s - m)
            probs.append(p)
            denoms.append(jnp.sum(p, axis=-1, keepdims=True) + jnp.exp(sink - m))
        o = _bdot(jnp.concatenate(probs, axis=0), vh)
        outs += [o[half * n:(half + 1) * n] / denoms[half] for half in range(2)]
    o_ref[...] = jnp.concatenate(outs, axis=-1)


def _swa_mixer(z, sinks):
    t = z.shape[0]
    w = SWA_WINDOW
    n = SWA_TILE
    prev = lambda col: (lambda i: (jnp.maximum(i * (n // w) - 1, 0), col))
    return pl.pallas_call(
        _swa_kernel,
        grid=(t // n,),
        in_specs=[pl.BlockSpec((n, 256), lambda i: (i, 0)),
                  pl.BlockSpec((n, 128), lambda i: (i, 2)), pl.BlockSpec((n, 128), lambda i: (i, 3)),
                  pl.BlockSpec((w, 128), prev(2)), pl.BlockSpec((w, 128), prev(3)),
                  _resident((1, N_HEADS))],
        out_specs=pl.BlockSpec((n, GROUP_WIDTH), lambda i: (i, 0)),
        out_shape=jax.ShapeDtypeStruct((t, GROUP_WIDTH), F32),
        compiler_params=_params(("parallel",)),
        name="swa_mixer",
    )(z, z, z, z, z, sinks.reshape(1, -1))


def _rwkv_chunk(r, k, v, a_vec, b_vec, log_w, state_ref):
    c = r.shape[0]
    width = r.shape[1]
    tri = jnp.where(_iota((c, c), 1) <= _iota((c, c), 0), 1.0, 0.0).astype(F32)
    p = _dot(tri, log_w, HIGHEST)
    p_total = p[c - 1:c, :]
    decay_in = jnp.exp(p)
    decay_out = jnp.exp(-p)
    decay_end = jnp.exp(p_total - p)
    a_in = a_vec * jnp.exp(p - log_w)
    r_in = r * decay_in
    b_out = b_vec * decay_out
    k_out = k * decay_out
    b_end = b_vec * decay_end
    k_end = k * decay_end

    t_pos = _iota((c, width), 0)
    assert width == N_HEADS * c
    s_pos = _imod(_iota((c, width), 1), c)
    strict = s_pos < t_pos
    incl = s_pos <= t_pos
    expand = lambda x: _expand_heads(x, HEAD_DIM)

    scores = _bdot_nt(jnp.concatenate([a_in, r_in], axis=0),
                      jnp.concatenate([expand(b_out), expand(k_out)], axis=0))
    a_ab = jnp.where(strict, scores[0:c, 0:width], 0.0)
    a_ak = jnp.where(strict, scores[0:c, width:2 * width], 0.0)
    a_rb = jnp.where(incl, scores[c:2 * c, 0:width], 0.0)
    a_rk = jnp.where(incl, scores[c:2 * c, width:2 * width], 0.0)

    t_inv = jnp.where(s_pos == t_pos, 1.0, 0.0) + a_ab
    power = a_ab
    n_factors = (c - 1).bit_length()
    for _ in range(n_factors - 1):
        power = _bdot(power, expand(power))
        t_inv = t_inv + _bdot(t_inv, expand(power))

    v_heads = expand(v)
    x1 = _bdot(a_ak, v_heads)
    sol = _bdot(t_inv, jnp.concatenate([expand(x1), expand(a_in)], axis=1))
    u0 = sol[:, 0:width]
    w_mat = sol[:, width:2 * width]

    state = state_ref[...]
    from_state = _bdot_nt(jnp.concatenate([w_mat, r_in], axis=0), state)
    u = u0 + from_state[0:c]
    y = _bdot(a_rb, expand(u)) + _bdot(a_rk, v_heads) + from_state[c:2 * c]
    upd = _dot_tn(jnp.concatenate([u, v], axis=0).astype(BF16),
                  jnp.concatenate([b_end, k_end], axis=0).astype(BF16))
    same_head = _idiv(_iota((width, width), 0), HEAD_DIM) == _idiv(_iota((width, width), 1), HEAD_DIM)
    state_ref[...] = state * jnp.exp(p_total) + jnp.where(same_head, upd, 0.0)
    return y


def _rwkv_kernel(has_vres, *refs):
    if has_vres:
        (z_ref, zp_ref, vfirst_ref, mu_ref, w0_ref, wup_ref, a0_ref, aup_ref, gup_ref, kk_ref, ka_ref,
         rk_ref, lnw_ref, lnb_ref, v0_ref, vup_ref, o_ref, state_ref) = refs
    else:
        (z_ref, zp_ref, mu_ref, w0_ref, wup_ref, a0_ref, aup_ref, gup_ref, kk_ref, ka_ref,
         rk_ref, lnw_ref, lnb_ref, o_ref, vout_ref, state_ref) = refs
    step = pl.program_id(0)

    @pl.when(step == 0)
    def _():
        state_ref[...] = jnp.zeros_like(state_ref)

    z = z_ref[...]
    last_prev = jnp.where(step > 0, zp_ref[7:8, :], 0.0)
    prev = jnp.where(_iota(z.shape, 0) == 0, last_prev, pltpu.roll(z, 1, axis=0))
    zr = z + (prev - z) * mu_ref[...]
    r = zr[:, 0:256]
    k = zr[:, 256:512]
    v = zr[:, 512:768]
    low = zr[:, 768:896]
    w_pre = w0_ref[...] + _dot(jnp.tanh(low), wup_ref[...], HIGHEST)
    w_log = -(jnp.maximum(-w_pre, 0.0) + jnp.log1p(jnp.exp(-jnp.abs(w_pre)))) - 0.5
    log_w = -jnp.exp(w_log)
    a = _sigmoid(a0_ref[...] + _dot(low, aup_ref[...], HIGHEST))
    g = _dot(_sigmoid(low), gup_ref[...], HIGHEST)
    if has_vres:
        v = v + (vfirst_ref[...] - v) * _sigmoid(v0_ref[...] + _dot(low, vup_ref[...], HIGHEST))
    else:
        vout_ref[...] = v
    head_sum = _head_group_matrix(GROUP_WIDTH, HEAD_DIM, 1.0).astype(BF16)
    kk = k * kk_ref[...]
    kk = kk / jnp.maximum(jnp.sqrt(_dot_hilo(kk * kk, head_sum)), 1e-12)
    k = k * (1.0 + (a - 1.0) * ka_ref[...])
    a_vec = -kk
    b_vec = kk * a

    c = RWKV_CHUNK
    chunks = []
    for n in range(z.shape[0] // c):
        rows = slice(n * c, (n + 1) * c)
        chunks.append(_rwkv_chunk(r[rows], k[rows], v[rows], a_vec[rows], b_vec[rows], log_w[rows], state_ref))
    y = jnp.concatenate(chunks, axis=0)

    head_mean = _head_group_matrix(GROUP_WIDTH, HEAD_DIM, 1.0 / HEAD_DIM).astype(BF16)
    mu_y = _dot_hilo(y, head_mean)
    d = y - mu_y
    var_y = _dot_hilo(d * d, head_mean)
    y = d * lax.rsqrt(var_y + RWKV_GN_EPS) * lnw_ref[...] + lnb_ref[...]
    bonus = _dot_hilo(r * k * rk_ref[...], head_sum) * v
    o_ref[...] = (y + bonus) * g


def _rwkv_mixer(z, mu, w0, w_up, a0, a_up, g_up, k_k, k_a, r_k, lnx_w, lnx_b, vres):
    t = z.shape[0]
    c = RWKV_TILE
    row = lambda a: a.reshape(1, -1)
    low_rows = lambda a, start: jnp.zeros((LANES, GROUP_WIDTH), F32).at[start:start + a.shape[0]].set(a)
    has_vres = vres is not None
    mu_full = jnp.zeros((RWKV_W,), F32).at[:RWKV_COLS].set(mu)
    vec = _resident((1, GROUP_WIDTH))
    mat = _resident((LANES, GROUP_WIDTH))
    tile = pl.BlockSpec((c, GROUP_WIDTH), lambda i: (i, 0))
    z_specs = [pl.BlockSpec((c, RWKV_W), lambda i: (i, 0)),
               pl.BlockSpec((8, RWKV_W), lambda i: (jnp.maximum(i * (c // 8) - 1, 0), 0))]
    common = [row(w0), low_rows(w_up, 0), row(a0), low_rows(a_up, 16), low_rows(g_up, 32),
              row(k_k), row(k_a), row(r_k), row(lnx_w), row(lnx_b)]
    common_specs = [vec, mat, vec, mat, mat, vec, vec, vec, vec, vec]
    if has_vres:
        v_first, vres_mu, v0, v_up = vres
        mu_full = mu_full.at[RWKV_COLS:RWKV_COLS + RWKV_V_RANK].set(vres_mu)
        args = [z, z, v_first, row(mu_full)] + common + [row(v0), low_rows(v_up, 64)]
        in_specs = z_specs + [tile, _resident((1, RWKV_W))] + common_specs + [vec, mat]
        out_specs = tile
        out_shape = jax.ShapeDtypeStruct((t, GROUP_WIDTH), F32)
    else:
        args = [z, z, row(mu_full)] + common
        in_specs = z_specs + [_resident((1, RWKV_W))] + common_specs
        out_specs = [tile, tile]
        out_shape = [jax.ShapeDtypeStruct((t, GROUP_WIDTH), F32)] * 2
    return pl.pallas_call(
        functools.partial(_rwkv_kernel, has_vres),
        grid=(t // c,),
        in_specs=in_specs,
        out_specs=out_specs,
        out_shape=out_shape,
        scratch_shapes=[pltpu.VMEM((GROUP_WIDTH, GROUP_WIDTH), F32)],
        compiler_params=_params(("arbitrary",)),
        name="rwkv_mixer",
    )(*args)


def _outproj_kernel(alpha, x_ref, o0_ref, o1_ref, o2_ref, o3_ref, w_ref, g_ref, b_ref, y_ref):
    acc = alpha * x_ref[...]
    for h, ref in enumerate((o0_ref, o1_ref, o2_ref, o3_ref)):
        acc = acc + _dot(ref[...].astype(BF16), w_ref[h * GROUP_WIDTH:(h + 1) * GROUP_WIDTH, :])
    y_ref[...] = _layer_norm(acc, g_ref[...], b_ref[...])


def _out_proj_ln(alpha, x, mixes, w_out, g, b):
    t = x.shape[0]
    row_d = pl.BlockSpec((ROW_TILE, D_MODEL), lambda i: (i, 0))
    row_g = pl.BlockSpec((ROW_TILE, GROUP_WIDTH), lambda i: (i, 0))
    return pl.pallas_call(
        functools.partial(_outproj_kernel, alpha),
        grid=(t // ROW_TILE,),
        in_specs=[row_d, row_g, row_g, row_g, row_g, _resident((D_MODEL, D_MODEL)),
                  _resident((1, D_MODEL)), _resident((1, D_MODEL))],
        out_specs=row_d,
        out_shape=jax.ShapeDtypeStruct((t, D_MODEL), F32),
        compiler_params=_params(("parallel",)),
        name="out_proj_ln",
    )(x, *mixes, w_out.astype(BF16), g.reshape(1, -1), b.reshape(1, -1))


def _ln_embed(y, ln_g, ln_b, p, ple_gate, ple_proj):
    x = _layer_norm(y, ln_g, ln_b)
    gate = _sigmoid(_dot(x.astype(BF16), ple_gate))
    return x + gate * _dot(p.astype(BF16), ple_proj)


def _dense_ffn_kernel(alpha, x_ref, p_ref, wg_ref, wu_ref, wd_ref, g_ref, b_ref, pg_ref, pp_ref, y_ref, acc_ref):
    x = x_ref[...]
    xb = x.astype(BF16)
    acc_ref[...] = alpha * x
    for j in range(D_FF // FF_CHUNK):
        cols = slice(j * FF_CHUNK, (j + 1) * FF_CHUNK)
        h = _silu(_dot(xb, wg_ref[:, cols])) * _dot(xb, wu_ref[:, cols])
        acc_ref[...] += _dot(h.astype(BF16), wd_ref[cols, :])
    y_ref[...] = _ln_embed(acc_ref[...], g_ref[...], b_ref[...], p_ref[...], pg_ref[...], pp_ref[...])


def _dense_ffn_tail(alpha, x, p, w_gate, w_up, w_down, g, b, ple_gate, ple_proj):
    t = x.shape[0]
    row_d = pl.BlockSpec((ROW_TILE, D_MODEL), lambda i: (i, 0))
    return pl.pallas_call(
        functools.partial(_dense_ffn_kernel, alpha),
        grid=(t // ROW_TILE,),
        in_specs=[row_d, pl.BlockSpec((ROW_TILE, PLE_DIM), lambda i: (i, 0)),
                  _resident((D_MODEL, D_FF)), _resident((D_MODEL, D_FF)), _resident((D_FF, D_MODEL)),
                  _resident((1, D_MODEL)), _resident((1, D_MODEL)),
                  _resident((D_MODEL, D_MODEL)), _resident((PLE_DIM, D_MODEL))],
        out_specs=row_d,
        out_shape=jax.ShapeDtypeStruct((t, D_MODEL), F32),
        scratch_shapes=[pltpu.VMEM((ROW_TILE, D_MODEL), F32)],
        compiler_params=_params(("parallel",)),
        name="dense_ffn_tail",
    )(x, p, w_gate.astype(BF16), w_up.astype(BF16), w_down.astype(BF16), g.reshape(1, -1), b.reshape(1, -1),
      ple_gate.astype(BF16), ple_proj.astype(BF16))


def _router_kernel(x_ref, w_ref, o_ref):
    logits = _dot(x_ref[...], w_ref[...], HIGHEST)
    lane = _iota(logits.shape, 1).astype(F32)
    logits = jnp.where(lane < N_EXPERTS, logits, -jnp.inf)
    m1 = jnp.max(logits, axis=-1, keepdims=True)
    i1 = jnp.min(jnp.where(logits == m1, lane, LANES), axis=-1, keepdims=True)
    rest = jnp.where(lane == i1, -jnp.inf, logits)
    m2 = jnp.max(rest, axis=-1, keepdims=True)
    i2 = jnp.min(jnp.where(rest == m2, lane, LANES), axis=-1, keepdims=True)
    e2 = jnp.exp(m2 - m1)
    g1 = 1.0 / (1.0 + e2)
    g2 = e2 * g1
    o_ref[...] = jnp.where(lane == 0, i1, jnp.where(lane == 1, i2,
                           jnp.where(lane == 2, g1, jnp.where(lane == 3, g2, 0.0))))


def _route(x, router):
    t = x.shape[0]
    w = jnp.zeros((D_MODEL, LANES), F32).at[:, :N_EXPERTS].set(router)
    out = pl.pallas_call(
        _router_kernel,
        grid=(t // ROW_TILE,),
        in_specs=[pl.BlockSpec((ROW_TILE, D_MODEL), lambda i: (i, 0)), _resident((D_MODEL, LANES))],
        out_specs=pl.BlockSpec((ROW_TILE, LANES), lambda i: (i, 0)),
        out_shape=jax.ShapeDtypeStruct((t, LANES), F32),
        compiler_params=_params(("parallel",)),
        name="moe_router",
    )(x, w)
    return out[:, 0:2].astype(jnp.int32), out[:, 2:4]


def _expert_kernel(row_tok_ref, row_dst_ref, block_e_ref, x_hbm, wg_ref, wu_ref, wd_ref, y_hbm,
                   rows_ref, xb_ref, ybuf_ref, gather_sem, scatter_sem):
    i = pl.program_id(0)
    n = MOE_ROWS
    slot = lax.rem(i, 2)
    other = 1 - slot
    gather_row = lambda tok, s, r: pltpu.make_async_copy(
        x_hbm.at[pl.ds(tok, 1)], rows_ref.at[s, pl.ds(r, 1)], gather_sem.at[s])
    scatter_row = lambda dst, s, r: pltpu.make_async_copy(
        ybuf_ref.at[s, pl.ds(r, 1)], y_hbm.at[pl.ds(dst, 1)], scatter_sem.at[s])
    gather_block = lambda s: pltpu.make_async_copy(x_hbm.at[pl.ds(0, n)], rows_ref.at[s], gather_sem.at[s])
    scatter_block = lambda s: pltpu.make_async_copy(ybuf_ref.at[s], y_hbm.at[pl.ds(0, n)], scatter_sem.at[s])

    @pl.when(i == 0)
    def _():
        ybuf_ref[1] = jnp.zeros((n, D_MODEL), F32)

        def start(r, carry):
            gather_row(row_tok_ref[r], 0, r).start()
            return carry

        lax.fori_loop(0, n, start, 0)

    gather_block(slot).wait()
    xb_ref[...] = rows_ref[slot].astype(BF16)
    n_chunks = D_FF_EXPERT // FF_CHUNK
    rows_per_chunk = -(-n // n_chunks)
    for j in range(n_chunks):
        cols = slice(j * FF_CHUNK, (j + 1) * FF_CHUNK)
        xb = xb_ref[...]
        h = _silu(_dot(xb, wg_ref[0, :, cols])) * _dot(xb, wu_ref[0, :, cols])
        part = _dot(h.astype(BF16), wd_ref[0, cols, :])
        if j == 0:
            ybuf_ref[slot] = part
        else:
            ybuf_ref[slot] += part
        for r in range(j * rows_per_chunk, min((j + 1) * rows_per_chunk, n)):
            gather_row(row_tok_ref[(i + 1) * n + r], other, r).start()
            scatter_row(row_dst_ref[i * n + r], other, r).start()
    scatter_block(other).wait()

    @pl.when(i == pl.num_programs(0) - 1)
    def _():
        gather_block(other).wait()

        def start(r, carry):
            scatter_row(row_dst_ref[(i + 1) * n + r], slot, r).start()
            return carry

        lax.fori_loop(0, n, start, 0)
        scatter_block(slot).wait()


def _expert_rows(x, row_tok, row_dst, block_e, n_out_rows, w_gate, w_up, w_down):
    n_blocks = block_e.shape[0]
    whole = lambda shape: pl.BlockSpec((1,) + shape, lambda i, tok, dst, be: (be[i], 0, 0),
                                       pipeline_mode=pl.Buffered(1))
    grid_spec = pltpu.PrefetchScalarGridSpec(
        num_scalar_prefetch=3,
        grid=(n_blocks,),
        in_specs=[pl.BlockSpec(memory_space=pl.ANY),
                  whole((D_MODEL, D_FF_EXPERT)), whole((D_MODEL, D_FF_EXPERT)), whole((D_FF_EXPERT, D_MODEL))],
        out_specs=pl.BlockSpec(memory_space=pl.ANY),
        scratch_shapes=[pltpu.VMEM((2, MOE_ROWS, D_MODEL), F32), pltpu.VMEM((MOE_ROWS, D_MODEL), BF16),
                        pltpu.VMEM((2, MOE_ROWS, D_MODEL), F32),
                        pltpu.SemaphoreType.DMA((2,)), pltpu.SemaphoreType.DMA((2,))],
    )
    return pl.pallas_call(
        _expert_kernel,
        grid_spec=grid_spec,
        out_shape=jax.ShapeDtypeStruct((n_out_rows, D_MODEL), F32),
        compiler_params=_params(("arbitrary",)),
        name="moe_experts",
    )(row_tok, row_dst, block_e, x, w_gate.astype(BF16), w_up.astype(BF16), w_down.astype(BF16))


def _combine_kernel(alpha, x_ref, y0_ref, y1_ref, gates_ref, p_ref, g_ref, b_ref, pg_ref, pp_ref, o_ref):
    gates = gates_ref[...]
    f = y0_ref[...] * gates[:, 0:1] + y1_ref[...] * gates[:, 1:2]
    o_ref[...] = _ln_embed(alpha * x_ref[...] + f, g_ref[...], b_ref[...], p_ref[...], pg_ref[...], pp_ref[...])


def _moe_combine_tail(alpha, x, p, y, gates, g, b, ple_gate, ple_proj):
    t = x.shape[0]
    n = ROW_TILE
    row_d = pl.BlockSpec((n, D_MODEL), lambda i: (i, 0))
    return pl.pallas_call(
        functools.partial(_combine_kernel, alpha),
        grid=(t // n,),
        in_specs=[row_d, row_d, pl.BlockSpec((n, D_MODEL), lambda i: (t // n + i, 0)),
                  pl.BlockSpec((n, 2), lambda i: (i, 0)), pl.BlockSpec((n, PLE_DIM), lambda i: (i, 0)),
                  _resident((1, D_MODEL)), _resident((1, D_MODEL)),
                  _resident((D_MODEL, D_MODEL)), _resident((PLE_DIM, D_MODEL))],
        out_specs=row_d,
        out_shape=jax.ShapeDtypeStruct((t, D_MODEL), F32),
        compiler_params=_params(("parallel",)),
        name="moe_combine_tail",
    )(x, y, y, gates, p, g.reshape(1, -1), b.reshape(1, -1), ple_gate.astype(BF16), ple_proj.astype(BF16))


def _moe_tail(alpha, x, p, router, w_gate, w_up, w_down, g, b, ple_gate, ple_proj):
    t = x.shape[0]
    experts, gates = _route(x, router)
    e_flat = experts.reshape(-1)
    onehot = (e_flat[:, None] == jnp.arange(N_EXPERTS, dtype=jnp.int32)[None, :]).astype(jnp.int32)
    running = jnp.cumsum(onehot, axis=0)
    rank = jnp.sum(onehot * (running - 1), axis=1)
    counts = running[-1]
    padded = (counts + MOE_ROWS - 1) // MOE_ROWS * MOE_ROWS
    pad_end = jnp.cumsum(padded)
    pad_start = pad_end - padded
    dest = (pad_start[e_flat] + rank).astype(jnp.int32)
    n_blocks = (2 * t) // MOE_ROWS + N_EXPERTS
    n_rows = n_blocks * MOE_ROWS
    assign = jnp.arange(2 * t, dtype=jnp.int32)
    row_assign = jnp.full((n_rows,), -1, jnp.int32).at[dest].set(assign)
    used = row_assign >= 0
    spare = 2 * t + jnp.cumsum(jnp.where(used, 0, 1).astype(jnp.int32)) - 1
    row_tok = jnp.where(used, row_assign // 2, 0)
    row_dst = jnp.where(used, (row_assign % 2) * t + row_assign // 2, spare)
    n_spare = n_rows - 2 * t
    first_dst = 2 * t + n_spare + jnp.arange(MOE_ROWS, dtype=jnp.int32)
    row_tok = jnp.concatenate([row_tok, jnp.zeros((MOE_ROWS,), jnp.int32)])
    row_dst = jnp.concatenate([first_dst, row_dst])
    block_start = jnp.arange(n_blocks, dtype=jnp.int32) * MOE_ROWS
    block_e = jnp.minimum(jnp.sum((block_start[:, None] >= pad_end[None, :]).astype(jnp.int32), axis=1),
                          N_EXPERTS - 1)
    y = _expert_rows(x, row_tok, row_dst, block_e, 2 * t + n_spare + MOE_ROWS, w_gate, w_up, w_down)
    return _moe_combine_tail(alpha, x, p, y, gates, g, b, ple_gate, ple_proj)


def kernel(x, p, w_in, w_out, gla_gk_up, gla_gk_bias, gla_norm_w, hgrn_lower_bounds, hgrn_norm_w, swa_sinks, rwkv_mu, rwkv_w0, rwkv_w_up, rwkv_a0, rwkv_a_up, rwkv_g_up, rwkv_k_k, rwkv_k_a, rwkv_r_k, rwkv_lnx_w, rwkv_lnx_b, rwkv_vres_down, rwkv_vres_mu, rwkv_v0, rwkv_vres_up, ln1_g, ln1_b, ln2_g, ln2_b, ffn_w_gate, ffn_w_up, ffn_w_down, moe_router, moe_w_gate, moe_w_up, moe_w_down, ple_proj, ple_gate):
    bsz, seq, d = x.shape
    depth = w_in.shape[0]
    alpha = (2.0 * depth) ** 0.25
    lbs = jnp.cumsum(jax.nn.softmax(hgrn_lower_bounds.astype(F32), axis=0), axis=0)
    lbs = lbs - lbs[0]
    outs = []
    for bi in range(bsz):
        xt = x[bi]
        v_first = None
        for i in range(depth):
            w = _group_in_weights(w_in[i], None if i == 0 else rwkv_vres_down[i - 1])
            z_gla, z_hgrn, z_swa, z_rwkv = _in_proj(xt, w)
            o_gla = _gla_mixer(z_gla, gla_gk_up[i], gla_gk_bias[i], gla_norm_w[i])
            o_hgrn = _hgrn_mixer(z_hgrn, lbs[i], hgrn_norm_w[i])
            o_swa = _swa_mixer(z_swa, swa_sinks[i])
            vres = None if i == 0 else (v_first, rwkv_vres_mu[i - 1], rwkv_v0[i - 1], rwkv_vres_up[i - 1])
            rw = _rwkv_mixer(z_rwkv, rwkv_mu[i], rwkv_w0[i], rwkv_w_up[i], rwkv_a0[i], rwkv_a_up[i],
                             rwkv_g_up[i], rwkv_k_k[i], rwkv_k_a[i], rwkv_r_k[i].reshape(-1),
                             rwkv_lnx_w[i], rwkv_lnx_b[i], vres)
            if i == 0:
                o_rwkv, v_first = rw
            else:
                o_rwkv = rw
            xt = _out_proj_ln(alpha, xt, (o_gla, o_hgrn, o_swa, o_rwkv), w_out[i], ln1_g[i], ln1_b[i])
            j = i // 2
            if i % 2 == 0:
                xt = _dense_ffn_tail(alpha, xt, p[i, bi], ffn_w_gate[j], ffn_w_up[j], ffn_w_down[j],
                                     ln2_g[i], ln2_b[i], ple_gate[i], ple_proj[i])
            else:
                xt = _moe_tail(alpha, xt, p[i, bi], moe_router[j], moe_w_gate[j], moe_w_up[j], moe_w_down[j],
                               ln2_g[i], ln2_b[i], ple_gate[i], ple_proj[i])
        outs.append(xt)
    return jnp.stack(outs, axis=0)
```

```python
import functools

import jax
import jax.numpy as jnp
from jax import lax
from jax.experimental import pallas as pl
from jax.experimental.pallas import tpu as pltpu

F32 = jnp.float32
BF16 = jnp.bfloat16
HIGHEST = lax.Precision.HIGHEST

D_MODEL = 1024
GROUP_WIDTH = 256
N_HEADS = 4
HEAD_DIM = 64
GLA_DK = 32
GLA_GATE_RANK = 16
GLA_GATE_NORMALIZER = 16.0
SWA_WINDOW = 128
RWKV_COLS = 3 * GROUP_WIDTH + 16 + 16 + 32
RWKV_V_RANK = 8
LN_EPS = 1e-5
RMS_EPS = 1e-6
RWKV_GN_EPS = 64e-5
D_FF = 2816
N_EXPERTS = 8
D_FF_EXPERT = 3584
PLE_DIM = 256

LANES = 128
GLA_W = 896
HGRN_W = 1024
SWA_W = 512
RWKV_W = 896
Z_W = GLA_W + HGRN_W + SWA_W + RWKV_W

GLA_SUB = 16
GLA_TILE = 128
RWKV_CHUNK = 64
RWKV_TILE = 512
RWKV_STAGGER = 2
ROW_TILE = 512
FF_CHUNK = 256
SWA_TILE = 128
MOE_ROWS = 512
MOE_DMA_FREE_CHUNKS = 4
VMEM_LIMIT = 56 * 1024 * 1024


def _iota(shape, dim):
    return lax.broadcasted_iota(jnp.int32, shape, dim)


def _idiv(x, n):
    return jnp.right_shift(x, n.bit_length() - 1)


def _imod(x, n):
    return jnp.bitwise_and(x, n - 1)


def _dot(a, b, precision=None):
    return jnp.dot(a, b, preferred_element_type=F32, precision=precision)


def _dot_nt(a, b, precision=None):
    return lax.dot_general(a, b, (((1,), (1,)), ((), ())), preferred_element_type=F32, precision=precision)


def _dot_tn(a, b, precision=None):
    return lax.dot_general(a, b, (((0,), (0,)), ((), ())), preferred_element_type=F32, precision=precision)


def _bdot(a, b):
    return _dot(a.astype(BF16), b.astype(BF16))


def _bdot_nt(a, b):
    return _dot_nt(a.astype(BF16), b.astype(BF16))


def _dot_hilo(x, m):
    hi = x.astype(BF16)
    lo = (x - hi.astype(F32)).astype(BF16)
    return _dot(hi, m) + _dot(lo, m)


def _dot_mask(m, x):
    m = m.astype(BF16)
    x1 = x.astype(BF16)
    r1 = x - x1.astype(F32)
    x2 = r1.astype(BF16)
    x3 = (r1 - x2.astype(F32)).astype(BF16)
    return _dot(m, x1) + _dot(m, x2) + _dot(m, x3)


def _dot_3pass(x, w):
    x_hi = x.astype(BF16)
    x_lo = (x - x_hi.astype(F32)).astype(BF16)
    w_hi = w.astype(BF16)
    w_lo = (w - w_hi.astype(F32)).astype(BF16)
    return _dot(x_hi, w_hi) + _dot(x_lo, w_hi) + _dot(x_hi, w_lo)


def _sigmoid(x):
    return 1.0 / (1.0 + jnp.exp(-x))


def _silu(x):
    return x * _sigmoid(x)


def _log_sigmoid(x):
    return jnp.minimum(x, 0.0) - jnp.log1p(jnp.exp(-jnp.abs(x)))


def _layer_norm(y, g, b):
    mu = jnp.mean(y, axis=-1, keepdims=True)
    d = y - mu
    var = jnp.mean(d * d, axis=-1, keepdims=True)
    return d * lax.rsqrt(var + LN_EPS) * g + b


def _expand_heads(x, head_width):
    lane_head = _idiv(_iota(x.shape, 1), head_width)
    return jnp.concatenate([jnp.where(lane_head == h, x, 0.0) for h in range(N_HEADS)], axis=0)


def _head_group_matrix(width, head_width, value):
    same = _idiv(_iota((width, width), 0), head_width) == _idiv(_iota((width, width), 1), head_width)
    return jnp.where(same, value, 0.0).astype(F32)


def _resident(shape):
    nd = len(shape)
    return pl.BlockSpec(shape, lambda *_: (0,) * nd, pipeline_mode=pl.Buffered(1))


def _params(semantics):
    return pltpu.CompilerParams(dimension_semantics=semantics, vmem_limit_bytes=VMEM_LIMIT)


def _inproj_kernel(x_ref, w_ref, gla_ref, hgrn_ref, swa_ref, rwkv_ref):
    xb = x_ref[...].astype(BF16)
    o = 0
    for ref, width in ((gla_ref, GLA_W), (hgrn_ref, HGRN_W), (swa_ref, SWA_W), (rwkv_ref, RWKV_W)):
        ref[...] = _dot(xb, w_ref[:, o:o + width])
        o += width


def _in_proj(x, w):
    t = x.shape[0]
    widths = (GLA_W, HGRN_W, SWA_W, RWKV_W)
    return pl.pallas_call(
        _inproj_kernel,
        grid=(t // ROW_TILE,),
        in_specs=[pl.BlockSpec((ROW_TILE, D_MODEL), lambda i: (i, 0)), _resident((D_MODEL, Z_W))],
        out_specs=[pl.BlockSpec((ROW_TILE, w_), lambda i: (i, 0)) for w_ in widths],
        out_shape=[jax.ShapeDtypeStruct((t, w_), F32) for w_ in widths],
        compiler_params=_params(("parallel",)),
        name="in_proj",
    )(x, w)


def _group_in_weights(w_in, vres_down):
    gla, hgrn, swa, rwkv = jnp.split(w_in, (784, 784 + 1024, 784 + 1024 + 512), axis=1)
    if vres_down is not None:
        rwkv = jnp.concatenate([rwkv, vres_down], axis=1)
    pad = lambda a, w_: jnp.pad(a, ((0, 0), (0, w_ - a.shape[1])))
    return jnp.concatenate([pad(gla, GLA_W), hgrn, swa, pad(rwkv, RWKV_W)], axis=1).astype(BF16)


def _gated_linear_attention_tile(q, k, v, log_f, state_ref):
    length, kw = q.shape
    head_k = kw // N_HEADS
    n_sub = length // GLA_SUB
    row = _iota((length, length), 0)
    col = _iota((length, length), 1)
    same_sub = _idiv(row, GLA_SUB) == _idiv(col, GLA_SUB)
    m_local = jnp.where(same_sub & (col <= row), 1.0, 0.0).astype(F32)
    m_prev = jnp.where(_idiv(col, GLA_SUB) < _idiv(row, GLA_SUB), 1.0, 0.0).astype(F32)
    b_local = _dot_mask(m_local, log_f)
    b_start = _dot_mask(m_prev, log_f)
    b_full = b_start + b_local
    q_local = q * jnp.exp(b_local)

    q_pos = _imod(_iota((N_HEADS * GLA_SUB, length), 0), GLA_SUB)
    s_pos = _iota((N_HEADS * GLA_SUB, length), 1)
    probs = []
    for c in range(n_sub):
        r0, r1 = c * GLA_SUB, (c + 1) * GLA_SUB
        expo = jnp.where(_iota((r1, kw), 0) < r0, b_start[r0:r0 + 1, :] - b_full[0:r1], -b_local[0:r1])
        k_ref = k[0:r1] * jnp.exp(expo)
        if r1 < length:
            k_ref = jnp.concatenate([k_ref, jnp.zeros((length - r1, kw), F32)], axis=0)
        q_heads = _expand_heads(q_local[r0:r1, :], head_k)
        s = _bdot_nt(q_heads, k_ref)
        probs.append(jnp.where(s_pos <= q_pos + r0, s, 0.0))
    o_heads = _bdot(jnp.concatenate(probs, axis=0), v)
    v_head = _idiv(_iota((GLA_SUB, GROUP_WIDTH), 1), HEAD_DIM)
    rows = []
    for c in range(n_sub):
        base = c * N_HEADS * GLA_SUB
        acc = jnp.zeros((GLA_SUB, GROUP_WIDTH), F32)
        for h in range(N_HEADS):
            acc = acc + jnp.where(v_head == h, o_heads[base + h * GLA_SUB:base + (h + 1) * GLA_SUB, :], 0.0)
        rows.append(acc)
    o = jnp.concatenate(rows, axis=0)

    state_t = state_ref[...]
    o = o + _bdot_nt(q * jnp.exp(b_full), state_t)
    b_total = b_full[length - 1:length, :]
    k_end = k * jnp.exp(b_total - b_full)
    upd = _dot_tn(v.astype(BF16), k_end.astype(BF16))
    same_head = _idiv(_iota((GROUP_WIDTH, kw), 0), HEAD_DIM) == _idiv(_iota((GROUP_WIDTH, kw), 1), head_k)
    state_ref[...] = state_t * jnp.exp(b_total) + jnp.where(same_head, upd, 0.0)
    return o


def _head_rms_gate(o, norm_w, gate):
    ms = _dot_hilo(o * o, _head_group_matrix(GROUP_WIDTH, HEAD_DIM, 1.0 / HEAD_DIM).astype(BF16))
    return o * lax.rsqrt(ms + RMS_EPS) * norm_w * _silu(gate)


def _gla_kernel(z_ref, gk_up_ref, gk_bias_ref, norm_w_ref, o_ref, state_ref):
    @pl.when(pl.program_id(0) == 0)
    def _():
        state_ref[...] = jnp.zeros_like(state_ref)

    z = z_ref[...]
    q = z[:, 0:128] * (GLA_DK ** -0.5)
    k = z[:, 128:256]
    v = z[:, 256:512]
    g = z[:, 512:768]
    gate_in = _dot_3pass(z[:, 768:896], gk_up_ref[...]) + gk_bias_ref[...]
    log_f = _log_sigmoid(gate_in) * (1.0 / GLA_GATE_NORMALIZER)
    o = _gated_linear_attention_tile(q, k, v, log_f, state_ref)
    o_ref[...] = _head_rms_gate(o, norm_w_ref[...], g)


def _hgrn_kernel(z_ref, lb_ref, log_lb_ref, norm_w_ref, o_ref, state_ref):
    @pl.when(pl.program_id(0) == 0)
    def _():
        state_ref[...] = jnp.zeros_like(state_ref)

    z = z_ref[...]
    q = _silu(z[:, 0:256])
    f = z[:, 256:512]
    v = z[:, 512:768]
    g = z[:, 768:1024]
    lb = lb_ref[...]
    a = log_lb_ref[...]
    c = jnp.log1p(-lb) + _log_sigmoid(f)
    log_f = jnp.maximum(a, c) + jnp.log1p(jnp.exp(-jnp.abs(a - c)))
    k = (1.0 - lb) * _sigmoid(-f)
    o = _gated_linear_attention_tile(q, k, v, log_f, state_ref)
    o_ref[...] = _head_rms_gate(o, norm_w_ref[...], g)


def _gla_mixer(z, gk_up, gk_bias, norm_w):
    t = z.shape[0]
    gk_up_pad = jnp.zeros((LANES, N_HEADS * GLA_DK), F32).at[:GLA_GATE_RANK].set(gk_up)
    return pl.pallas_call(
        _gla_kernel,
        grid=(t // GLA_TILE,),
        in_specs=[pl.BlockSpec((GLA_TILE, GLA_W), lambda i: (i, 0)),
                  _resident((LANES, N_HEADS * GLA_DK)), _resident((1, N_HEADS * GLA_DK)),
                  _resident((1, GROUP_WIDTH))],
        out_specs=pl.BlockSpec((GLA_TILE, GROUP_WIDTH), lambda i: (i, 0)),
        out_shape=jax.ShapeDtypeStruct((t, GROUP_WIDTH), F32),
        scratch_shapes=[pltpu.VMEM((GROUP_WIDTH, N_HEADS * GLA_DK), F32)],
        compiler_params=_params(("arbitrary",)),
        name="gla_mixer",
    )(z, gk_up_pad, gk_bias.reshape(1, -1), jnp.tile(norm_w, N_HEADS).reshape(1, -1))


def _hgrn_mixer(z, lb, norm_w):
    t = z.shape[0]
    return pl.pallas_call(
        _hgrn_kernel,
        grid=(t // GLA_TILE,),
        in_specs=[pl.BlockSpec((GLA_TILE, HGRN_W), lambda i: (i, 0)),
                  _resident((1, GROUP_WIDTH)), _resident((1, GROUP_WIDTH)), _resident((1, GROUP_WIDTH))],
        out_specs=pl.BlockSpec((GLA_TILE, GROUP_WIDTH), lambda i: (i, 0)),
        out_shape=jax.ShapeDtypeStruct((t, GROUP_WIDTH), F32),
        scratch_shapes=[pltpu.VMEM((GROUP_WIDTH, GROUP_WIDTH), F32)],
        compiler_params=_params(("arbitrary",)),
        name="hgrn_mixer",
    )(z, lb.reshape(1, -1), jnp.log(lb).reshape(1, -1), jnp.tile(norm_w, N_HEADS).reshape(1, -1))


def _swa_kernel(q_ref, k_ref, v_ref, kp_ref, vp_ref, sink_ref, o_ref):
    w = SWA_WINDOW
    n = SWA_TILE
    has_prev = pl.program_id(0) > 0
    q = q_ref[...] * (HEAD_DIM ** -0.5)
    kw = jnp.concatenate([kp_ref[...], k_ref[...]], axis=0)
    vw = jnp.concatenate([vp_ref[...], v_ref[...]], axis=0)
    q_pos = _iota((n, w + n), 0) + w
    k_pos = _iota((n, w + n), 1)
    dist = q_pos - k_pos
    visible = (dist >= 0) & (dist < w) & ((k_pos >= w) | has_prev)
    sinks = sink_ref[...]
    head_cols = lambda x, h: x[:, h * HEAD_DIM:(h + 1) * HEAD_DIM]
    outs = []
    for kv in range(N_HEADS // 2):
        heads = (2 * kv, 2 * kv + 1)
        scores = _bdot_nt(jnp.concatenate([head_cols(q, h) for h in heads], axis=0), head_cols(kw, kv))
        probs, denoms = [], []
        for half, h in enumerate(heads):
            s = jnp.where(visible, scores[half * n:(half + 1) * n], -jnp.inf)
            sink = sinks[:, h:h + 1]
            m = jnp.maximum(jnp.max(s, axis=-1, keepdims=True), sink)
            p = jnp.exp(s - m)
            probs.append(p)
            denoms.append(jnp.sum(p, axis=-1, keepdims=True) + jnp.exp(sink - m))
        o = _bdot(jnp.concatenate(probs, axis=0), head_cols(vw, kv))
        outs += [o[half * n:(half + 1) * n] / denoms[half] for half in range(2)]
    o_ref[...] = jnp.concatenate(outs, axis=-1)


def _swa_mixer(z, sinks):
    t = z.shape[0]
    w = SWA_WINDOW
    n = SWA_TILE
    prev = lambda col: (lambda i: (jnp.maximum(i * (n // w) - 1, 0), col))
    return pl.pallas_call(
        _swa_kernel,
        grid=(t // n,),
        in_specs=[pl.BlockSpec((n, 256), lambda i: (i, 0)),
                  pl.BlockSpec((n, 128), lambda i: (i, 2)), pl.BlockSpec((n, 128), lambda i: (i, 3)),
                  pl.BlockSpec((w, 128), prev(2)), pl.BlockSpec((w, 128), prev(3)),
                  _resident((1, N_HEADS))],
        out_specs=pl.BlockSpec((n, GROUP_WIDTH), lambda i: (i, 0)),
        out_shape=jax.ShapeDtypeStruct((t, GROUP_WIDTH), F32),
        compiler_params=_params(("parallel",)),
        name="swa_mixer",
    )(z, z, z, z, z, sinks.reshape(1, -1))


def _run_staggered(stage_generators, stagger):
    live = dict(enumerate(stage_generators))
    rnd = 0
    while live:
        for n in sorted(live):
            if rnd >= n * stagger:
                try:
                    next(live[n])
                except StopIteration:
                    del live[n]
        rnd += 1


def _rwkv_chunk(index, r, k, v, a_vec, b_vec, log_w, state_box, out_box):
    c = r.shape[0]
    width = r.shape[1]
    tri = jnp.where(_iota((c, c), 1) <= _iota((c, c), 0), 1.0, 0.0).astype(F32)
    p = _dot_mask(tri, log_w)
    yield
    p_total = p[c - 1:c, :]
    decay_in = jnp.exp(p)
    decay_out = jnp.exp(-p)
    decay_end = jnp.exp(p_total - p)
    a_in = a_vec * jnp.exp(p - log_w)
    r_in = r * decay_in
    b_out = b_vec * decay_out
    k_out = k * decay_out
    b_end = b_vec * decay_end
    k_end = k * decay_end

    t_pos = _iota((c, width), 0)
    assert width == N_HEADS * c
    s_pos = _imod(_iota((c, width), 1), c)
    strict = s_pos < t_pos
    incl = s_pos <= t_pos
    expand = lambda x: _expand_heads(x, HEAD_DIM)

    scores = _bdot_nt(jnp.concatenate([a_in, r_in], axis=0),
                      jnp.concatenate([expand(b_out), expand(k_out)], axis=0))
    yield
    a_ab = jnp.where(strict, scores[0:c, 0:width], 0.0)
    a_ak = jnp.where(strict, scores[0:c, width:2 * width], 0.0)
    a_rb = jnp.where(incl, scores[c:2 * c, 0:width], 0.0)
    a_rk = jnp.where(incl, scores[c:2 * c, width:2 * width], 0.0)

    t_inv = jnp.where(s_pos == t_pos, 1.0, 0.0) + a_ab
    v_heads = expand(v)
    x1 = _bdot(a_ak, v_heads)
    power = _bdot(a_ab, expand(a_ab))
    yield
    n_factors = (c - 1).bit_length()
    for _ in range(n_factors - 2):
        power_heads = expand(power)
        t_inv = t_inv + _bdot(t_inv, power_heads)
        power = _bdot(power, power_heads)
        yield
    t_inv = t_inv + _bdot(t_inv, expand(power))
    yield
    sol = _bdot(t_inv, jnp.concatenate([expand(x1), expand(a_in)], axis=1))
    yield
    u0 = sol[:, 0:width]
    w_mat = sol[:, width:2 * width]

    assert len(out_box) == index, "the previous chunk must have replaced the state before it is read"
    state = state_box[0]
    from_state = _bdot_nt(jnp.concatenate([w_mat, r_in], axis=0), state)
    yield
    u = u0 + from_state[0:c]
    y = _bdot(a_rb, expand(u)) + _bdot(a_rk, v_heads) + from_state[c:2 * c]
    upd = _dot_tn(jnp.concatenate([u, v], axis=0).astype(BF16),
                  jnp.concatenate([b_end, k_end], axis=0).astype(BF16))
    same_head = _idiv(_iota((width, width), 0), HEAD_DIM) == _idiv(_iota((width, width), 1), HEAD_DIM)
    state_box[0] = state * jnp.exp(p_total) + jnp.where(same_head, upd, 0.0)
    out_box.append(y)


def _rwkv_kernel(has_vres, *refs):
    if has_vres:
        (z_ref, zp_ref, vfirst_ref, mu_ref, w0_ref, wup_ref, a0_ref, aup_ref, gup_ref, kk_ref, ka_ref,
         rk_ref, lnw_ref, lnb_ref, v0_ref, vup_ref, o_ref, state_ref) = refs
    else:
        (z_ref, zp_ref, mu_ref, w0_ref, wup_ref, a0_ref, aup_ref, gup_ref, kk_ref, ka_ref,
         rk_ref, lnw_ref, lnb_ref, o_ref, vout_ref, state_ref) = refs
    step = pl.program_id(0)

    @pl.when(step == 0)
    def _():
        state_ref[...] = jnp.zeros_like(state_ref)

    z = z_ref[...]
    last_prev = jnp.where(step > 0, zp_ref[7:8, :], 0.0)
    prev = jnp.where(_iota(z.shape, 0) == 0, last_prev, pltpu.roll(z, 1, axis=0))
    zr = z + (prev - z) * mu_ref[...]
    r = zr[:, 0:256]
    k = zr[:, 256:512]
    v = zr[:, 512:768]
    low = zr[:, 768:896]
    w_pre = w0_ref[...] + _dot_3pass(jnp.tanh(low), wup_ref[...])
    w_log = -(jnp.maximum(-w_pre, 0.0) + jnp.log1p(jnp.exp(-jnp.abs(w_pre)))) - 0.5
    log_w = -jnp.exp(w_log)
    a = _sigmoid(a0_ref[...] + _dot_3pass(low, aup_ref[...]))
    g = _dot_3pass(_sigmoid(low), gup_ref[...])
    if has_vres:
        v = v + (vfirst_ref[...] - v) * _sigmoid(v0_ref[...] + _dot_3pass(low, vup_ref[...]))
    else:
        vout_ref[...] = v
    head_sum = _head_group_matrix(GROUP_WIDTH, HEAD_DIM, 1.0).astype(BF16)
    kk = k * kk_ref[...]
    kk = kk / jnp.maximum(jnp.sqrt(_dot_hilo(kk * kk, head_sum)), 1e-12)
    k = k * (1.0 + (a - 1.0) * ka_ref[...])
    a_vec = -kk
    b_vec = kk * a

    c = RWKV_CHUNK
    state_box = [state_ref[...]]
    chunks = []
    stages = []
    for n in range(z.shape[0] // c):
        rows = slice(n * c, (n + 1) * c)
        stages.append(_rwkv_chunk(n, r[rows], k[rows], v[rows], a_vec[rows], b_vec[rows], log_w[rows],
                                  state_box, chunks))
    _run_staggered(stages, RWKV_STAGGER)
    state_ref[...] = state_box[0]
    y = jnp.concatenate(chunks, axis=0)

    head_mean = _head_group_matrix(GROUP_WIDTH, HEAD_DIM, 1.0 / HEAD_DIM).astype(BF16)
    mu_y = _dot_hilo(y, head_mean)
    d = y - mu_y
    var_y = _dot_hilo(d * d, head_mean)
    y = d * lax.rsqrt(var_y + RWKV_GN_EPS) * lnw_ref[...] + lnb_ref[...]
    bonus = _dot_hilo(r * k * rk_ref[...], head_sum) * v
    o_ref[...] = (y + bonus) * g


def _rwkv_mixer(z, mu, w0, w_up, a0, a_up, g_up, k_k, k_a, r_k, lnx_w, lnx_b, vres):
    t = z.shape[0]
    c = RWKV_TILE
    row = lambda a: a.reshape(1, -1)
    low_rows = lambda a, start: jnp.zeros((LANES, GROUP_WIDTH), F32).at[start:start + a.shape[0]].set(a)
    has_vres = vres is not None
    mu_full = jnp.zeros((RWKV_W,), F32).at[:RWKV_COLS].set(mu)
    vec = _resident((1, GROUP_WIDTH))
    mat = _resident((LANES, GROUP_WIDTH))
    tile = pl.BlockSpec((c, GROUP_WIDTH), lambda i: (i, 0))
    z_specs = [pl.BlockSpec((c, RWKV_W), lambda i: (i, 0)),
               pl.BlockSpec((8, RWKV_W), lambda i: (jnp.maximum(i * (c // 8) - 1, 0), 0))]
    common = [row(w0), low_rows(w_up, 0), row(a0), low_rows(a_up, 16), low_rows(g_up, 32),
              row(k_k), row(k_a), row(r_k), row(lnx_w), row(lnx_b)]
    common_specs = [vec, mat, vec, mat, mat, vec, vec, vec, vec, vec]
    if has_vres:
        v_first, vres_mu, v0, v_up = vres
        mu_full = mu_full.at[RWKV_COLS:RWKV_COLS + RWKV_V_RANK].set(vres_mu)
        args = [z, z, v_first, row(mu_full)] + common + [row(v0), low_rows(v_up, 64)]
        in_specs = z_specs + [tile, _resident((1, RWKV_W))] + common_specs + [vec, mat]
        out_specs = tile
        out_shape = jax.ShapeDtypeStruct((t, GROUP_WIDTH), F32)
    else:
        args = [z, z, row(mu_full)] + common
        in_specs = z_specs + [_resident((1, RWKV_W))] + common_specs
        out_specs = [tile, tile]
        out_shape = [jax.ShapeDtypeStruct((t, GROUP_WIDTH), F32)] * 2
    return pl.pallas_call(
        functools.partial(_rwkv_kernel, has_vres),
        grid=(t // c,),
        in_specs=in_specs,
        out_specs=out_specs,
        out_shape=out_shape,
        scratch_shapes=[pltpu.VMEM((GROUP_WIDTH, GROUP_WIDTH), F32)],
        compiler_params=_params(("arbitrary",)),
        name="rwkv_mixer",
    )(*args)


def _outproj_kernel(alpha, x_ref, o0_ref, o1_ref, o2_ref, o3_ref, w_ref, g_ref, b_ref, y_ref):
    acc = alpha * x_ref[...]
    for h, ref in enumerate((o0_ref, o1_ref, o2_ref, o3_ref)):
        acc = acc + _dot(ref[...].astype(BF16), w_ref[h * GROUP_WIDTH:(h + 1) * GROUP_WIDTH, :])
    y_ref[...] = _layer_norm(acc, g_ref[...], b_ref[...])


def _out_proj_ln(alpha, x, mixes, w_out, g, b):
    t = x.shape[0]
    row_d = pl.BlockSpec((ROW_TILE, D_MODEL), lambda i: (i, 0))
    row_g = pl.BlockSpec((ROW_TILE, GROUP_WIDTH), lambda i: (i, 0))
    return pl.pallas_call(
        functools.partial(_outproj_kernel, alpha),
        grid=(t // ROW_TILE,),
        in_specs=[row_d, row_g, row_g, row_g, row_g, _resident((D_MODEL, D_MODEL)),
                  _resident((1, D_MODEL)), _resident((1, D_MODEL))],
        out_specs=row_d,
        out_shape=jax.ShapeDtypeStruct((t, D_MODEL), F32),
        compiler_params=_params(("parallel",)),
        name="out_proj_ln",
    )(x, *mixes, w_out.astype(BF16), g.reshape(1, -1), b.reshape(1, -1))


def _ln_embed(y, ln_g, ln_b, p, ple_gate, ple_proj):
    x = _layer_norm(y, ln_g, ln_b)
    gate = _sigmoid(_dot(x.astype(BF16), ple_gate))
    return x + gate * _dot(p.astype(BF16), ple_proj)


def _dense_ffn_kernel(alpha, x_ref, p_ref, wg_ref, wu_ref, wd_ref, g_ref, b_ref, pg_ref, pp_ref, y_ref, acc_ref):
    x = x_ref[...]
    xb = x.astype(BF16)
    acc_ref[...] = alpha * x
    for j in range(D_FF // FF_CHUNK):
        cols = slice(j * FF_CHUNK, (j + 1) * FF_CHUNK)
        h = _silu(_dot(xb, wg_ref[:, cols])) * _dot(xb, wu_ref[:, cols])
        acc_ref[...] += _dot(h.astype(BF16), wd_ref[cols, :])
    y_ref[...] = _ln_embed(acc_ref[...], g_ref[...], b_ref[...], p_ref[...], pg_ref[...], pp_ref[...])


def _dense_ffn_tail(alpha, x, p, w_gate, w_up, w_down, g, b, ple_gate, ple_proj):
    t = x.shape[0]
    row_d = pl.BlockSpec((ROW_TILE, D_MODEL), lambda i: (i, 0))
    return pl.pallas_call(
        functools.partial(_dense_ffn_kernel, alpha),
        grid=(t // ROW_TILE,),
        in_specs=[row_d, pl.BlockSpec((ROW_TILE, PLE_DIM), lambda i: (i, 0)),
                  _resident((D_MODEL, D_FF)), _resident((D_MODEL, D_FF)), _resident((D_FF, D_MODEL)),
                  _resident((1, D_MODEL)), _resident((1, D_MODEL)),
                  _resident((D_MODEL, D_MODEL)), _resident((PLE_DIM, D_MODEL))],
        out_specs=row_d,
        out_shape=jax.ShapeDtypeStruct((t, D_MODEL), F32),
        scratch_shapes=[pltpu.VMEM((ROW_TILE, D_MODEL), F32)],
        compiler_params=_params(("parallel",)),
        name="dense_ffn_tail",
    )(x, p, w_gate.astype(BF16), w_up.astype(BF16), w_down.astype(BF16), g.reshape(1, -1), b.reshape(1, -1),
      ple_gate.astype(BF16), ple_proj.astype(BF16))


def _router_kernel(x_ref, w_ref, o_ref):
    logits = _dot(x_ref[...], w_ref[...], HIGHEST)
    lane = _iota(logits.shape, 1).astype(F32)
    logits = jnp.where(lane < N_EXPERTS, logits, -jnp.inf)
    m1 = jnp.max(logits, axis=-1, keepdims=True)
    i1 = jnp.min(jnp.where(logits == m1, lane, LANES), axis=-1, keepdims=True)
    rest = jnp.where(lane == i1, -jnp.inf, logits)
    m2 = jnp.max(rest, axis=-1, keepdims=True)
    i2 = jnp.min(jnp.where(rest == m2, lane, LANES), axis=-1, keepdims=True)
    e2 = jnp.exp(m2 - m1)
    g1 = 1.0 / (1.0 + e2)
    g2 = e2 * g1
    o_ref[...] = jnp.where(lane == 0, i1, jnp.where(lane == 1, i2,
                           jnp.where(lane == 2, g1, jnp.where(lane == 3, g2, 0.0))))


def _route(x, router):
    t = x.shape[0]
    w = jnp.zeros((D_MODEL, LANES), F32).at[:, :N_EXPERTS].set(router)
    out = pl.pallas_call(
        _router_kernel,
        grid=(t // ROW_TILE,),
        in_specs=[pl.BlockSpec((ROW_TILE, D_MODEL), lambda i: (i, 0)), _resident((D_MODEL, LANES))],
        out_specs=pl.BlockSpec((ROW_TILE, LANES), lambda i: (i, 0)),
        out_shape=jax.ShapeDtypeStruct((t, LANES), F32),
        compiler_params=_params(("parallel",)),
        name="moe_router",
    )(x, w)
    return out[:, 0:2].astype(jnp.int32), out[:, 2:4]


def _expert_kernel(row_tok_ref, row_dst_ref, block_e_ref, x_hbm, wg_ref, wu_ref, wd_ref, y_hbm,
                   rows_ref, xb_ref, ybuf_ref, gather_sem, scatter_sem):
    i = pl.program_id(0)
    n = MOE_ROWS
    slot = lax.rem(i, 2)
    other = 1 - slot
    gather_row = lambda tok, s, r: pltpu.make_async_copy(
        x_hbm.at[pl.ds(tok, 1)], rows_ref.at[s, pl.ds(r, 1)], gather_sem.at[s])
    scatter_row = lambda dst, s, r: pltpu.make_async_copy(
        ybuf_ref.at[s, pl.ds(r, 1)], y_hbm.at[pl.ds(dst, 1)], scatter_sem.at[s])
    gather_block = lambda s: pltpu.make_async_copy(x_hbm.at[pl.ds(0, n)], rows_ref.at[s], gather_sem.at[s])
    scatter_block = lambda s: pltpu.make_async_copy(ybuf_ref.at[s], y_hbm.at[pl.ds(0, n)], scatter_sem.at[s])

    @pl.when(i == 0)
    def _():
        ybuf_ref[1] = jnp.zeros((n, D_MODEL), F32)

        def start(r, carry):
            gather_row(row_tok_ref[r], 0, r).start()
            return carry

        lax.fori_loop(0, n, start, 0)

    gather_block(slot).wait()
    xb_ref[...] = rows_ref[slot].astype(BF16)
    n_chunks = D_FF_EXPERT // FF_CHUNK
    rows_per_chunk = -(-n // (n_chunks - MOE_DMA_FREE_CHUNKS))
    for j in range(n_chunks):
        cols = slice(j * FF_CHUNK, (j + 1) * FF_CHUNK)
        xb = xb_ref[...]
        h = _silu(_dot(xb, wg_ref[0, :, cols])) * _dot(xb, wu_ref[0, :, cols])
        part = _dot(h.astype(BF16), wd_ref[0, cols, :])
        if j == 0:
            ybuf_ref[slot] = part
        else:
            ybuf_ref[slot] += part
        for r in range(j * rows_per_chunk, min((j + 1) * rows_per_chunk, n)):
            gather_row(row_tok_ref[(i + 1) * n + r], other, r).start()
            scatter_row(row_dst_ref[i * n + r], other, r).start()
    scatter_block(other).wait()

    @pl.when(i == pl.num_programs(0) - 1)
    def _():
        gather_block(other).wait()

        def start(r, carry):
            scatter_row(row_dst_ref[(i + 1) * n + r], slot, r).start()
            return carry

        lax.fori_loop(0, n, start, 0)
        scatter_block(slot).wait()


def _expert_rows(x, row_tok, row_dst, block_e, n_out_rows, w_gate, w_up, w_down):
    n_blocks = block_e.shape[0]
    whole = lambda shape: pl.BlockSpec((1,) + shape, lambda i, tok, dst, be: (be[i], 0, 0),
                                       pipeline_mode=pl.Buffered(1))
    grid_spec = pltpu.PrefetchScalarGridSpec(
        num_scalar_prefetch=3,
        grid=(n_blocks,),
        in_specs=[pl.BlockSpec(memory_space=pl.ANY),
                  whole((D_MODEL, D_FF_EXPERT)), whole((D_MODEL, D_FF_EXPERT)), whole((D_FF_EXPERT, D_MODEL))],
        out_specs=pl.BlockSpec(memory_space=pl.ANY),
        scratch_shapes=[pltpu.VMEM((2, MOE_ROWS, D_MODEL), F32), pltpu.VMEM((MOE_ROWS, D_MODEL), BF16),
                        pltpu.VMEM((2, MOE_ROWS, D_MODEL), F32),
                        pltpu.SemaphoreType.DMA((2,)), pltpu.SemaphoreType.DMA((2,))],
    )
    return pl.pallas_call(
        _expert_kernel,
        grid_spec=grid_spec,
        out_shape=jax.ShapeDtypeStruct((n_out_rows, D_MODEL), F32),
        compiler_params=_params(("arbitrary",)),
        name="moe_experts",
    )(row_tok, row_dst, block_e, x, w_gate.astype(BF16), w_up.astype(BF16), w_down.astype(BF16))


def _combine_kernel(alpha, x_ref, y0_ref, y1_ref, gates_ref, p_ref, g_ref, b_ref, pg_ref, pp_ref, o_ref):
    gates = gates_ref[...]
    f = y0_ref[...] * gates[:, 0:1] + y1_ref[...] * gates[:, 1:2]
    o_ref[...] = _ln_embed(alpha * x_ref[...] + f, g_ref[...], b_ref[...], p_ref[...], pg_ref[...], pp_ref[...])


def _moe_combine_tail(alpha, x, p, y, gates, g, b, ple_gate, ple_proj):
    t = x.shape[0]
    n = ROW_TILE
    row_d = pl.BlockSpec((n, D_MODEL), lambda i: (i, 0))
    return pl.pallas_call(
        functools.partial(_combine_kernel, alpha),
        grid=(t // n,),
        in_specs=[row_d, row_d, pl.BlockSpec((n, D_MODEL), lambda i: (t // n + i, 0)),
                  pl.BlockSpec((n, 2), lambda i: (i, 0)), pl.BlockSpec((n, PLE_DIM), lambda i: (i, 0)),
                  _resident((1, D_MODEL)), _resident((1, D_MODEL)),
                  _resident((D_MODEL, D_MODEL)), _resident((PLE_DIM, D_MODEL))],
        out_specs=row_d,
        out_shape=jax.ShapeDtypeStruct((t, D_MODEL), F32),
        compiler_params=_params(("parallel",)),
        name="moe_combine_tail",
    )(x, y, y, gates, p, g.reshape(1, -1), b.reshape(1, -1), ple_gate.astype(BF16), ple_proj.astype(BF16))


def _moe_tail(alpha, x, p, router, w_gate, w_up, w_down, g, b, ple_gate, ple_proj):
    t = x.shape[0]
    experts, gates = _route(x, router)
    e_flat = experts.reshape(-1)
    onehot = (e_flat[:, None] == jnp.arange(N_EXPERTS, dtype=jnp.int32)[None, :]).astype(jnp.int32)
    running = jnp.cumsum(onehot, axis=0)
    rank = jnp.sum(onehot * (running - 1), axis=1)
    counts = running[-1]
    padded = (counts + MOE_ROWS - 1) // MOE_ROWS * MOE_ROWS
    pad_end = jnp.cumsum(padded)
    pad_start = pad_end - padded
    dest = (pad_start[e_flat] + rank).astype(jnp.int32)
    n_blocks = (2 * t) // MOE_ROWS + N_EXPERTS
    n_rows = n_blocks * MOE_ROWS
    assign = jnp.arange(2 * t, dtype=jnp.int32)
    row_assign = jnp.full((n_rows,), -1, jnp.int32).at[dest].set(assign)
    used = row_assign >= 0
    spare = 2 * t + jnp.cumsum(jnp.where(used, 0, 1).astype(jnp.int32)) - 1
    row_tok = jnp.where(used, row_assign // 2, 0)
    row_dst = jnp.where(used, (row_assign % 2) * t + row_assign // 2, spare)
    n_spare = n_rows - 2 * t
    first_dst = 2 * t + n_spare + jnp.arange(MOE_ROWS, dtype=jnp.int32)
    row_tok = jnp.concatenate([row_tok, jnp.zeros((MOE_ROWS,), jnp.int32)])
    row_dst = jnp.concatenate([first_dst, row_dst])
    block_start = jnp.arange(n_blocks, dtype=jnp.int32) * MOE_ROWS
    block_e = jnp.minimum(jnp.sum((block_start[:, None] >= pad_end[None, :]).astype(jnp.int32), axis=1),
                          N_EXPERTS - 1)
    y = _expert_rows(x, row_tok, row_dst, block_e, 2 * t + n_spare + MOE_ROWS, w_gate, w_up, w_down)
    return _moe_combine_tail(alpha, x, p, y, gates, g, b, ple_gate, ple_proj)


def kernel(x, p, w_in, w_out, gla_gk_up, gla_gk_bias, gla_norm_w, hgrn_lower_bounds, hgrn_norm_w, swa_sinks, rwkv_mu, rwkv_w0, rwkv_w_up, rwkv_a0, rwkv_a_up, rwkv_g_up, rwkv_k_k, rwkv_k_a, rwkv_r_k, rwkv_lnx_w, rwkv_lnx_b, rwkv_vres_down, rwkv_vres_mu, rwkv_v0, rwkv_vres_up, ln1_g, ln1_b, ln2_g, ln2_b, ffn_w_gate, ffn_w_up, ffn_w_down, moe_router, moe_w_gate, moe_w_up, moe_w_down, ple_proj, ple_gate):
    bsz, seq, d = x.shape
    depth = w_in.shape[0]
    alpha = (2.0 * depth) ** 0.25
    lbs = jnp.cumsum(jax.nn.softmax(hgrn_lower_bounds.astype(F32), axis=0), axis=0)
    lbs = lbs - lbs[0]
    outs = []
    for bi in range(bsz):
        xt = x[bi]
        v_first = None
        for i in range(depth):
            w = _group_in_weights(w_in[i], None if i == 0 else rwkv_vres_down[i - 1])
            z_gla, z_hgrn, z_swa, z_rwkv = _in_proj(xt, w)
            o_gla = _gla_mixer(z_gla, gla_gk_up[i], gla_gk_bias[i], gla_norm_w[i])
            o_hgrn = _hgrn_mixer(z_hgrn, lbs[i], hgrn_norm_w[i])
            o_swa = _swa_mixer(z_swa, swa_sinks[i])
            vres = None if i == 0 else (v_first, rwkv_vres_mu[i - 1], rwkv_v0[i - 1], rwkv_vres_up[i - 1])
            rw = _rwkv_mixer(z_rwkv, rwkv_mu[i], rwkv_w0[i], rwkv_w_up[i], rwkv_a0[i], rwkv_a_up[i],
                             rwkv_g_up[i], rwkv_k_k[i], rwkv_k_a[i], rwkv_r_k[i].reshape(-1),
                             rwkv_lnx_w[i], rwkv_lnx_b[i], vres)
            if i == 0:
                o_rwkv, v_first = rw
            else:
                o_rwkv = rw
            xt = _out_proj_ln(alpha, xt, (o_gla, o_hgrn, o_swa, o_rwkv), w_out[i], ln1_g[i], ln1_b[i])
            j = i // 2
            if i % 2 == 0:
                xt = _dense_ffn_tail(alpha, xt, p[i, bi], ffn_w_gate[j], ffn_w_up[j], ffn_w_down[j],
                                     ln2_g[i], ln2_b[i], ple_gate[i], ple_proj[i])
            else:
                xt = _moe_tail(alpha, xt, p[i, bi], moe_router[j], moe_w_gate[j], moe_w_up[j], moe_w_down[j],
                               ln2_g[i], ln2_b[i], ple_gate[i], ple_proj[i])
        outs.append(xt)
    return jnp.stack(outs, axis=0)
```

```python
import functools

import jax
import jax.numpy as jnp
from jax import lax
from jax.experimental import pallas as pl
from jax.experimental.pallas import tpu as pltpu

F32 = jnp.float32
BF16 = jnp.bfloat16
HIGHEST = lax.Precision.HIGHEST
MIX_DTYPE = BF16

D_MODEL = 1024
GROUP_WIDTH = 256
N_HEADS = 4
HEAD_DIM = 64
GLA_DK = 32
GLA_GATE_RANK = 16
GLA_GATE_NORMALIZER = 16.0
SWA_WINDOW = 128
RWKV_COLS = 3 * GROUP_WIDTH + 16 + 16 + 32
RWKV_V_RANK = 8
LN_EPS = 1e-5
RMS_EPS = 1e-6
RWKV_GN_EPS = 64e-5
D_FF = 2816
N_EXPERTS = 8
D_FF_EXPERT = 3584
PLE_DIM = 256

LANES = 128
GLA_W = 896
HGRN_W = 1024
SWA_W = 512
RWKV_W = 896
Z_W = GLA_W + HGRN_W + SWA_W + RWKV_W

GLA_SUB = 16
GLA_TILE = 128
RWKV_CHUNK = 64
RWKV_TILE = 512
RWKV_STAGGER = 2
ROW_TILE = 512
FF_CHUNK = 256
SWA_TILE = 128
MOE_ROWS = 512
MOE_DMA_FREE_CHUNKS = 4
GATHER_DMA_PRIORITY = 0
SCATTER_DMA_PRIORITY = 1
VMEM_LIMIT = 56 * 1024 * 1024


def _iota(shape, dim):
    return lax.broadcasted_iota(jnp.int32, shape, dim)


def _idiv(x, n):
    return jnp.right_shift(x, n.bit_length() - 1)


def _imod(x, n):
    return jnp.bitwise_and(x, n - 1)


def _dot(a, b, precision=None):
    return jnp.dot(a, b, preferred_element_type=F32, precision=precision)


def _dot_nt(a, b, precision=None):
    return lax.dot_general(a, b, (((1,), (1,)), ((), ())), preferred_element_type=F32, precision=precision)


def _dot_tn(a, b, precision=None):
    return lax.dot_general(a, b, (((0,), (0,)), ((), ())), preferred_element_type=F32, precision=precision)


def _bdot(a, b):
    return _dot(a.astype(BF16), b.astype(BF16))


def _bdot_nt(a, b):
    return _dot_nt(a.astype(BF16), b.astype(BF16))


def _dot_hilo(x, m):
    hi = x.astype(BF16)
    lo = (x - hi.astype(F32)).astype(BF16)
    return _dot(hi, m) + _dot(lo, m)


def _dot_mask(m, x):
    m = m.astype(BF16)
    x1 = x.astype(BF16)
    r1 = x - x1.astype(F32)
    x2 = r1.astype(BF16)
    x3 = (r1 - x2.astype(F32)).astype(BF16)
    return _dot(m, x1) + _dot(m, x2) + _dot(m, x3)


def _dot_3pass(x, w):
    x_hi = x.astype(BF16)
    x_lo = (x - x_hi.astype(F32)).astype(BF16)
    w_hi = w.astype(BF16)
    w_lo = (w - w_hi.astype(F32)).astype(BF16)
    return _dot(x_hi, w_hi) + _dot(x_lo, w_hi) + _dot(x_hi, w_lo)


def _sigmoid(x):
    return 1.0 / (1.0 + jnp.exp(-x))


def _silu(x):
    return x * _sigmoid(x)


def _log_sigmoid(x):
    return jnp.minimum(x, 0.0) - jnp.log1p(jnp.exp(-jnp.abs(x)))


def _layer_norm(y, g, b):
    mu = jnp.mean(y, axis=-1, keepdims=True)
    d = y - mu
    var = jnp.mean(d * d, axis=-1, keepdims=True)
    return d * lax.rsqrt(var + LN_EPS) * g + b


def _expand_heads(x, head_width):
    lane_head = _idiv(_iota(x.shape, 1), head_width)
    return jnp.concatenate([jnp.where(lane_head == h, x, 0.0) for h in range(N_HEADS)], axis=0)


def _head_group_matrix(width, head_width, value):
    same = _idiv(_iota((width, width), 0), head_width) == _idiv(_iota((width, width), 1), head_width)
    return jnp.where(same, value, 0.0).astype(F32)


def _resident(shape):
    nd = len(shape)
    return pl.BlockSpec(shape, lambda *_: (0,) * nd, pipeline_mode=pl.Buffered(1))


def _params(semantics):
    return pltpu.CompilerParams(dimension_semantics=semantics, vmem_limit_bytes=VMEM_LIMIT)


def _inproj_kernel(x_ref, w_ref, gla_ref, hgrn_ref, swa_ref, rwkv_ref):
    xb = x_ref[...].astype(BF16)
    o = 0
    for ref, width in ((gla_ref, GLA_W), (hgrn_ref, HGRN_W), (swa_ref, SWA_W), (rwkv_ref, RWKV_W)):
        ref[...] = _dot(xb, w_ref[:, o:o + width])
        o += width


def _in_proj(x, w):
    t = x.shape[0]
    widths = (GLA_W, HGRN_W, SWA_W, RWKV_W)
    return pl.pallas_call(
        _inproj_kernel,
        grid=(t // ROW_TILE,),
        in_specs=[pl.BlockSpec((ROW_TILE, D_MODEL), lambda i: (i, 0)), _resident((D_MODEL, Z_W))],
        out_specs=[pl.BlockSpec((ROW_TILE, w_), lambda i: (i, 0)) for w_ in widths],
        out_shape=[jax.ShapeDtypeStruct((t, w_), F32) for w_ in widths],
        compiler_params=_params(("parallel",)),
        name="in_proj",
    )(x, w)


def _group_in_weights(w_in, vres_down):
    gla, hgrn, swa, rwkv = jnp.split(w_in, (784, 784 + 1024, 784 + 1024 + 512), axis=1)
    if vres_down is not None:
        rwkv = jnp.concatenate([rwkv, vres_down], axis=1)
    pad = lambda a, w_: jnp.pad(a, ((0, 0), (0, w_ - a.shape[1])))
    return jnp.concatenate([pad(gla, GLA_W), hgrn, swa, pad(rwkv, RWKV_W)], axis=1).astype(BF16)


def _gated_linear_attention_tile(q, k, v, log_f, state_ref):
    length, kw = q.shape
    head_k = kw // N_HEADS
    n_sub = length // GLA_SUB
    row = _iota((length, length), 0)
    col = _iota((length, length), 1)
    same_sub = _idiv(row, GLA_SUB) == _idiv(col, GLA_SUB)
    m_local = jnp.where(same_sub & (col <= row), 1.0, 0.0).astype(F32)
    m_prev = jnp.where(_idiv(col, GLA_SUB) < _idiv(row, GLA_SUB), 1.0, 0.0).astype(F32)
    b_local = _dot_mask(m_local, log_f)
    b_start = _dot_mask(m_prev, log_f)
    b_full = b_start + b_local
    q_local = q * jnp.exp(b_local)

    q_pos = _imod(_iota((N_HEADS * GLA_SUB, length), 0), GLA_SUB)
    s_pos = _iota((N_HEADS * GLA_SUB, length), 1)
    probs = []
    for c in range(n_sub):
        r0, r1 = c * GLA_SUB, (c + 1) * GLA_SUB
        expo = jnp.where(_iota((r1, kw), 0) < r0, b_start[r0:r0 + 1, :] - b_full[0:r1], -b_local[0:r1])
        k_ref = k[0:r1] * jnp.exp(expo)
        if r1 < length:
            k_ref = jnp.concatenate([k_ref, jnp.zeros((length - r1, kw), F32)], axis=0)
        q_heads = _expand_heads(q_local[r0:r1, :], head_k)
        s = _bdot_nt(q_heads, k_ref)
        probs.append(jnp.where(s_pos <= q_pos + r0, s, 0.0))
    o_heads = _bdot(jnp.concatenate(probs, axis=0), v)
    v_head = _idiv(_iota((GLA_SUB, GROUP_WIDTH), 1), HEAD_DIM)
    rows = []
    for c in range(n_sub):
        base = c * N_HEADS * GLA_SUB
        acc = jnp.zeros((GLA_SUB, GROUP_WIDTH), F32)
        for h in range(N_HEADS):
            acc = acc + jnp.where(v_head == h, o_heads[base + h * GLA_SUB:base + (h + 1) * GLA_SUB, :], 0.0)
        rows.append(acc)
    o = jnp.concatenate(rows, axis=0)

    state_t = state_ref[...]
    o = o + _bdot_nt(q * jnp.exp(b_full), state_t)
    b_total = b_full[length - 1:length, :]
    k_end = k * jnp.exp(b_total - b_full)
    upd = _dot_tn(v.astype(BF16), k_end.astype(BF16))
    same_head = _idiv(_iota((GROUP_WIDTH, kw), 0), HEAD_DIM) == _idiv(_iota((GROUP_WIDTH, kw), 1), head_k)
    state_ref[...] = state_t * jnp.exp(b_total) + jnp.where(same_head, upd, 0.0)
    return o


def _head_rms_gate(o, norm_w, gate):
    ms = _dot_hilo(o * o, _head_group_matrix(GROUP_WIDTH, HEAD_DIM, 1.0 / HEAD_DIM).astype(BF16))
    return o * lax.rsqrt(ms + RMS_EPS) * norm_w * _silu(gate)


def _gla_kernel(z_ref, gk_up_ref, gk_bias_ref, norm_w_ref, o_ref, state_ref):
    @pl.when(pl.program_id(0) == 0)
    def _():
        state_ref[...] = jnp.zeros_like(state_ref)

    z = z_ref[...]
    q = z[:, 0:128] * (GLA_DK ** -0.5)
    k = z[:, 128:256]
    v = z[:, 256:512]
    g = z[:, 512:768]
    gate_in = _dot_3pass(z[:, 768:896], gk_up_ref[...]) + gk_bias_ref[...]
    log_f = _log_sigmoid(gate_in) * (1.0 / GLA_GATE_NORMALIZER)
    o = _gated_linear_attention_tile(q, k, v, log_f, state_ref)
    o_ref[...] = _head_rms_gate(o, norm_w_ref[...], g).astype(o_ref.dtype)


def _hgrn_kernel(z_ref, lb_ref, log_lb_ref, norm_w_ref, o_ref, state_ref):
    @pl.when(pl.program_id(0) == 0)
    def _():
        state_ref[...] = jnp.zeros_like(state_ref)

    z = z_ref[...]
    q = _silu(z[:, 0:256])
    f = z[:, 256:512]
    v = z[:, 512:768]
    g = z[:, 768:1024]
    lb = lb_ref[...]
    a = log_lb_ref[...]
    c = jnp.log1p(-lb) + _log_sigmoid(f)
    log_f = jnp.maximum(a, c) + jnp.log1p(jnp.exp(-jnp.abs(a - c)))
    k = (1.0 - lb) * _sigmoid(-f)
    o = _gated_linear_attention_tile(q, k, v, log_f, state_ref)
    o_ref[...] = _head_rms_gate(o, norm_w_ref[...], g).astype(o_ref.dtype)


def _gla_mixer(z, gk_up, gk_bias, norm_w):
    t = z.shape[0]
    gk_up_pad = jnp.zeros((LANES, N_HEADS * GLA_DK), F32).at[:GLA_GATE_RANK].set(gk_up)
    return pl.pallas_call(
        _gla_kernel,
        grid=(t // GLA_TILE,),
        in_specs=[pl.BlockSpec((GLA_TILE, GLA_W), lambda i: (i, 0)),
                  _resident((LANES, N_HEADS * GLA_DK)), _resident((1, N_HEADS * GLA_DK)),
                  _resident((1, GROUP_WIDTH))],
        out_specs=pl.BlockSpec((GLA_TILE, GROUP_WIDTH), lambda i: (i, 0)),
        out_shape=jax.ShapeDtypeStruct((t, GROUP_WIDTH), MIX_DTYPE),
        scratch_shapes=[pltpu.VMEM((GROUP_WIDTH, N_HEADS * GLA_DK), F32)],
        compiler_params=_params(("arbitrary",)),
        name="gla_mixer",
    )(z, gk_up_pad, gk_bias.reshape(1, -1), jnp.tile(norm_w, N_HEADS).reshape(1, -1))


def _hgrn_mixer(z, lb, norm_w):
    t = z.shape[0]
    return pl.pallas_call(
        _hgrn_kernel,
        grid=(t // GLA_TILE,),
        in_specs=[pl.BlockSpec((GLA_TILE, HGRN_W), lambda i: (i, 0)),
                  _resident((1, GROUP_WIDTH)), _resident((1, GROUP_WIDTH)), _resident((1, GROUP_WIDTH))],
        out_specs=pl.BlockSpec((GLA_TILE, GROUP_WIDTH), lambda i: (i, 0)),
        out_shape=jax.ShapeDtypeStruct((t, GROUP_WIDTH), MIX_DTYPE),
        scratch_shapes=[pltpu.VMEM((GROUP_WIDTH, GROUP_WIDTH), F32)],
        compiler_params=_params(("arbitrary",)),
        name="hgrn_mixer",
    )(z, lb.reshape(1, -1), jnp.log(lb).reshape(1, -1), jnp.tile(norm_w, N_HEADS).reshape(1, -1))


def _swa_kernel(q_ref, k_ref, v_ref, kp_ref, vp_ref, sink_ref, o_ref):
    w = SWA_WINDOW
    n = SWA_TILE
    has_prev = pl.program_id(0) > 0
    q = q_ref[...] * (HEAD_DIM ** -0.5)
    kw = jnp.concatenate([kp_ref[...], k_ref[...]], axis=0)
    vw = jnp.concatenate([vp_ref[...], v_ref[...]], axis=0)
    q_pos = _iota((n, w + n), 0) + w
    k_pos = _iota((n, w + n), 1)
    dist = q_pos - k_pos
    visible = (dist >= 0) & (dist < w) & ((k_pos >= w) | has_prev)
    sinks = sink_ref[...]
    head_cols = lambda x, h: x[:, h * HEAD_DIM:(h + 1) * HEAD_DIM]
    outs = []
    for kv in range(N_HEADS // 2):
        heads = (2 * kv, 2 * kv + 1)
        scores = _bdot_nt(jnp.concatenate([head_cols(q, h) for h in heads], axis=0), head_cols(kw, kv))
        probs, denoms = [], []
        for half, h in enumerate(heads):
            s = jnp.where(visible, scores[half * n:(half + 1) * n], -jnp.inf)
            sink = sinks[:, h:h + 1]
            m = jnp.maximum(jnp.max(s, axis=-1, keepdims=True), sink)
            p = jnp.exp(s - m)
            probs.append(p)
            denoms.append(jnp.sum(p, axis=-1, keepdims=True) + jnp.exp(sink - m))
        o = _bdot(jnp.concatenate(probs, axis=0), head_cols(vw, kv))
        outs += [o[half * n:(half + 1) * n] / denoms[half] for half in range(2)]
    o_ref[...] = jnp.concatenate(outs, axis=-1).astype(o_ref.dtype)


def _swa_mixer(z, sinks):
    t = z.shape[0]
    w = SWA_WINDOW
    n = SWA_TILE
    prev = lambda col: (lambda i: (jnp.maximum(i * (n // w) - 1, 0), col))
    return pl.pallas_call(
        _swa_kernel,
        grid=(t // n,),
        in_specs=[pl.BlockSpec((n, 256), lambda i: (i, 0)),
                  pl.BlockSpec((n, 128), lambda i: (i, 2)), pl.BlockSpec((n, 128), lambda i: (i, 3)),
                  pl.BlockSpec((w, 128), prev(2)), pl.BlockSpec((w, 128), prev(3)),
                  _resident((1, N_HEADS))],
        out_specs=pl.BlockSpec((n, GROUP_WIDTH), lambda i: (i, 0)),
        out_shape=jax.ShapeDtypeStruct((t, GROUP_WIDTH), MIX_DTYPE),
        compiler_params=_params(("parallel",)),
        name="swa_mixer",
    )(z, z, z, z, z, sinks.reshape(1, -1))


def _run_staggered(stage_generators, stagger):
    live = dict(enumerate(stage_generators))
    rnd = 0
    while live:
        for n in sorted(live):
            if rnd >= n * stagger:
                try:
                    next(live[n])
                except StopIteration:
                    del live[n]
        rnd += 1


def _rwkv_chunk(index, r, k, v, a_vec, b_vec, log_w, state_box, out_box):
    c = r.shape[0]
    width = r.shape[1]
    tri = jnp.where(_iota((c, c), 1) <= _iota((c, c), 0), 1.0, 0.0).astype(F32)
    p = _dot_mask(tri, log_w)
    yield
    p_total = p[c - 1:c, :]
    decay_in = jnp.exp(p)
    decay_out = jnp.exp(-p)
    decay_end = jnp.exp(p_total - p)
    a_in = a_vec * jnp.exp(p - log_w)
    r_in = r * decay_in
    b_out = b_vec * decay_out
    k_out = k * decay_out
    b_end = b_vec * decay_end
    k_end = k * decay_end

    t_pos = _iota((c, width), 0)
    assert width == N_HEADS * c
    s_pos = _imod(_iota((c, width), 1), c)
    strict = s_pos < t_pos
    incl = s_pos <= t_pos
    expand = lambda x: _expand_heads(x, HEAD_DIM)

    scores = _bdot_nt(jnp.concatenate([a_in, r_in], axis=0),
                      jnp.concatenate([expand(b_out), expand(k_out)], axis=0))
    yield
    a_ab = jnp.where(strict, scores[0:c, 0:width], 0.0)
    a_ak = jnp.where(strict, scores[0:c, width:2 * width], 0.0)
    a_rb = jnp.where(incl, scores[c:2 * c, 0:width], 0.0)
    a_rk = jnp.where(incl, scores[c:2 * c, width:2 * width], 0.0)

    t_inv = jnp.where(s_pos == t_pos, 1.0, 0.0) + a_ab
    v_heads = expand(v)
    x1 = _bdot(a_ak, v_heads)
    power = _bdot(a_ab, expand(a_ab))
    yield
    n_factors = (c - 1).bit_length()
    for _ in range(n_factors - 2):
        power_heads = expand(power)
        t_inv = t_inv + _bdot(t_inv, power_heads)
        power = _bdot(power, power_heads)
        yield
    t_inv = t_inv + _bdot(t_inv, expand(power))
    yield
    sol = _bdot(t_inv, jnp.concatenate([expand(x1), expand(a_in)], axis=1))
    yield
    u0 = sol[:, 0:width]
    w_mat = sol[:, width:2 * width]

    assert len(out_box) == index, "the previous chunk must have replaced the state before it is read"
    state = state_box[0]
    from_state = _bdot_nt(jnp.concatenate([w_mat, r_in], axis=0), state)
    yield
    u = u0 + from_state[0:c]
    y = _bdot(a_rb, expand(u)) + _bdot(a_rk, v_heads) + from_state[c:2 * c]
    upd = _dot_tn(jnp.concatenate([u, v], axis=0).astype(BF16),
                  jnp.concatenate([b_end, k_end], axis=0).astype(BF16))
    same_head = _idiv(_iota((width, width), 0), HEAD_DIM) == _idiv(_iota((width, width), 1), HEAD_DIM)
    state_box[0] = state * jnp.exp(p_total) + jnp.where(same_head, upd, 0.0)
    out_box.append(y)


def _rwkv_kernel(has_vres, *refs):
    if has_vres:
        (z_ref, zp_ref, vfirst_ref, mu_ref, w0_ref, wup_ref, a0_ref, aup_ref, gup_ref, kk_ref, ka_ref,
         rk_ref, lnw_ref, lnb_ref, v0_ref, vup_ref, o_ref, state_ref) = refs
    else:
        (z_ref, zp_ref, mu_ref, w0_ref, wup_ref, a0_ref, aup_ref, gup_ref, kk_ref, ka_ref,
         rk_ref, lnw_ref, lnb_ref, o_ref, vout_ref, state_ref) = refs
    step = pl.program_id(0)

    @pl.when(step == 0)
    def _():
        state_ref[...] = jnp.zeros_like(state_ref)

    z = z_ref[...]
    last_prev = jnp.where(step > 0, zp_ref[7:8, :], 0.0)
    prev = jnp.where(_iota(z.shape, 0) == 0, last_prev, pltpu.roll(z, 1, axis=0))
    zr = z + (prev - z) * mu_ref[...]
    r = zr[:, 0:256]
    k = zr[:, 256:512]
    v = zr[:, 512:768]
    low = zr[:, 768:896]
    w_pre = w0_ref[...] + _dot_3pass(jnp.tanh(low), wup_ref[...])
    w_log = -(jnp.maximum(-w_pre, 0.0) + jnp.log1p(jnp.exp(-jnp.abs(w_pre)))) - 0.5
    log_w = -jnp.exp(w_log)
    a = _sigmoid(a0_ref[...] + _dot_3pass(low, aup_ref[...]))
    g = _dot_3pass(_sigmoid(low), gup_ref[...])
    if has_vres:
        v = v + (vfirst_ref[...] - v) * _sigmoid(v0_ref[...] + _dot_3pass(low, vup_ref[...]))
    else:
        vout_ref[...] = v
    head_sum = _head_group_matrix(GROUP_WIDTH, HEAD_DIM, 1.0).astype(BF16)
    kk = k * kk_ref[...]
    kk = kk / jnp.maximum(jnp.sqrt(_dot_hilo(kk * kk, head_sum)), 1e-12)
    k = k * (1.0 + (a - 1.0) * ka_ref[...])
    a_vec = -kk
    b_vec = kk * a

    c = RWKV_CHUNK
    state_box = [state_ref[...]]
    chunks = []
    stages = []
    for n in range(z.shape[0] // c):
        rows = slice(n * c, (n + 1) * c)
        stages.append(_rwkv_chunk(n, r[rows], k[rows], v[rows], a_vec[rows], b_vec[rows], log_w[rows],
                                  state_box, chunks))
    _run_staggered(stages, RWKV_STAGGER)
    state_ref[...] = state_box[0]
    y = jnp.concatenate(chunks, axis=0)

    head_mean = _head_group_matrix(GROUP_WIDTH, HEAD_DIM, 1.0 / HEAD_DIM).astype(BF16)
    mu_y = _dot_hilo(y, head_mean)
    d = y - mu_y
    var_y = _dot_hilo(d * d, head_mean)
    y = d * lax.rsqrt(var_y + RWKV_GN_EPS) * lnw_ref[...] + lnb_ref[...]
    bonus = _dot_hilo(r * k * rk_ref[...], head_sum) * v
    o_ref[...] = ((y + bonus) * g).astype(o_ref.dtype)


def _rwkv_mixer(z, mu, w0, w_up, a0, a_up, g_up, k_k, k_a, r_k, lnx_w, lnx_b, vres):
    t = z.shape[0]
    c = RWKV_TILE
    row = lambda a: a.reshape(1, -1)
    low_rows = lambda a, start: jnp.zeros((LANES, GROUP_WIDTH), F32).at[start:start + a.shape[0]].set(a)
    has_vres = vres is not None
    mu_full = jnp.zeros((RWKV_W,), F32).at[:RWKV_COLS].set(mu)
    vec = _resident((1, GROUP_WIDTH))
    mat = _resident((LANES, GROUP_WIDTH))
    tile = pl.BlockSpec((c, GROUP_WIDTH), lambda i: (i, 0))
    z_specs = [pl.BlockSpec((c, RWKV_W), lambda i: (i, 0)),
               pl.BlockSpec((8, RWKV_W), lambda i: (jnp.maximum(i * (c // 8) - 1, 0), 0))]
    common = [row(w0), low_rows(w_up, 0), row(a0), low_rows(a_up, 16), low_rows(g_up, 32),
              row(k_k), row(k_a), row(r_k), row(lnx_w), row(lnx_b)]
    common_specs = [vec, mat, vec, mat, mat, vec, vec, vec, vec, vec]
    if has_vres:
        v_first, vres_mu, v0, v_up = vres
        mu_full = mu_full.at[RWKV_COLS:RWKV_COLS + RWKV_V_RANK].set(vres_mu)
        args = [z, z, v_first, row(mu_full)] + common + [row(v0), low_rows(v_up, 64)]
        in_specs = z_specs + [tile, _resident((1, RWKV_W))] + common_specs + [vec, mat]
        out_specs = tile
        out_shape = jax.ShapeDtypeStruct((t, GROUP_WIDTH), MIX_DTYPE)
    else:
        args = [z, z, row(mu_full)] + common
        in_specs = z_specs + [_resident((1, RWKV_W))] + common_specs
        out_specs = [tile, tile]
        out_shape = [jax.ShapeDtypeStruct((t, GROUP_WIDTH), MIX_DTYPE), jax.ShapeDtypeStruct((t, GROUP_WIDTH), F32)]
    return pl.pallas_call(
        functools.partial(_rwkv_kernel, has_vres),
        grid=(t // c,),
        in_specs=in_specs,
        out_specs=out_specs,
        out_shape=out_shape,
        scratch_shapes=[pltpu.VMEM((GROUP_WIDTH, GROUP_WIDTH), F32)],
        compiler_params=_params(("arbitrary",)),
        name="rwkv_mixer",
    )(*args)


def _outproj_kernel(alpha, x_ref, o0_ref, o1_ref, o2_ref, o3_ref, w_ref, g_ref, b_ref, y_ref):
    acc = alpha * x_ref[...]
    for h, ref in enumerate((o0_ref, o1_ref, o2_ref, o3_ref)):
        acc = acc + _dot(ref[...], w_ref[h * GROUP_WIDTH:(h + 1) * GROUP_WIDTH, :])
    y_ref[...] = _layer_norm(acc, g_ref[...], b_ref[...])


def _out_proj_ln(alpha, x, mixes, w_out, g, b):
    t = x.shape[0]
    row_d = pl.BlockSpec((ROW_TILE, D_MODEL), lambda i: (i, 0))
    row_g = pl.BlockSpec((ROW_TILE, GROUP_WIDTH), lambda i: (i, 0))
    return pl.pallas_call(
        functools.partial(_outproj_kernel, alpha),
        grid=(t // ROW_TILE,),
        in_specs=[row_d, row_g, row_g, row_g, row_g, _resident((D_MODEL, D_MODEL)),
                  _resident((1, D_MODEL)), _resident((1, D_MODEL))],
        out_specs=row_d,
        out_shape=jax.ShapeDtypeStruct((t, D_MODEL), F32),
        compiler_params=_params(("parallel",)),
        name="out_proj_ln",
    )(x, *mixes, w_out.astype(BF16), g.reshape(1, -1), b.reshape(1, -1))


def _ln_embed(y, ln_g, ln_b, p, ple_gate, ple_proj):
    x = _layer_norm(y, ln_g, ln_b)
    gate = _sigmoid(_dot(x.astype(BF16), ple_gate))
    return x + gate * _dot(p.astype(BF16), ple_proj)


def _dense_ffn_kernel(alpha, x_ref, p_ref, wg_ref, wu_ref, wd_ref, g_ref, b_ref, pg_ref, pp_ref, y_ref, acc_ref):
    x = x_ref[...]
    xb = x.astype(BF16)
    acc_ref[...] = alpha * x
    for j in range(D_FF // FF_CHUNK):
        cols = slice(j * FF_CHUNK, (j + 1) * FF_CHUNK)
        h = _silu(_dot(xb, wg_ref[:, cols])) * _dot(xb, wu_ref[:, cols])
        acc_ref[...] += _dot(h.astype(BF16), wd_ref[cols, :])
    y_ref[...] = _ln_embed(acc_ref[...], g_ref[...], b_ref[...], p_ref[...], pg_ref[...], pp_ref[...])


def _dense_ffn_tail(alpha, x, p, w_gate, w_up, w_down, g, b, ple_gate, ple_proj):
    t = x.shape[0]
    row_d = pl.BlockSpec((ROW_TILE, D_MODEL), lambda i: (i, 0))
    return pl.pallas_call(
        functools.partial(_dense_ffn_kernel, alpha),
        grid=(t // ROW_TILE,),
        in_specs=[row_d, pl.BlockSpec((ROW_TILE, PLE_DIM), lambda i: (i, 0)),
                  _resident((D_MODEL, D_FF)), _resident((D_MODEL, D_FF)), _resident((D_FF, D_MODEL)),
                  _resident((1, D_MODEL)), _resident((1, D_MODEL)),
                  _resident((D_MODEL, D_MODEL)), _resident((PLE_DIM, D_MODEL))],
        out_specs=row_d,
        out_shape=jax.ShapeDtypeStruct((t, D_MODEL), F32),
        scratch_shapes=[pltpu.VMEM((ROW_TILE, D_MODEL), F32)],
        compiler_params=_params(("parallel",)),
        name="dense_ffn_tail",
    )(x, p, w_gate.astype(BF16), w_up.astype(BF16), w_down.astype(BF16), g.reshape(1, -1), b.reshape(1, -1),
      ple_gate.astype(BF16), ple_proj.astype(BF16))


def _router_kernel(x_ref, w_ref, o_ref):
    logits = _dot(x_ref[...], w_ref[...], HIGHEST)
    lane = _iota(logits.shape, 1).astype(F32)
    logits = jnp.where(lane < N_EXPERTS, logits, -jnp.inf)
    m1 = jnp.max(logits, axis=-1, keepdims=True)
    i1 = jnp.min(jnp.where(logits == m1, lane, LANES), axis=-1, keepdims=True)
    rest = jnp.where(lane == i1, -jnp.inf, logits)
    m2 = jnp.max(rest, axis=-1, keepdims=True)
    i2 = jnp.min(jnp.where(rest == m2, lane, LANES), axis=-1, keepdims=True)
    e2 = jnp.exp(m2 - m1)
    g1 = 1.0 / (1.0 + e2)
    g2 = e2 * g1
    o_ref[...] = jnp.where(lane == 0, i1, jnp.where(lane == 1, i2,
                           jnp.where(lane == 2, g1, jnp.where(lane == 3, g2, 0.0))))


def _route(x, router):
    t = x.shape[0]
    w = jnp.zeros((D_MODEL, LANES), F32).at[:, :N_EXPERTS].set(router)
    out = pl.pallas_call(
        _router_kernel,
        grid=(t // ROW_TILE,),
        in_specs=[pl.BlockSpec((ROW_TILE, D_MODEL), lambda i: (i, 0)), _resident((D_MODEL, LANES))],
        out_specs=pl.BlockSpec((ROW_TILE, LANES), lambda i: (i, 0)),
        out_shape=jax.ShapeDtypeStruct((t, LANES), F32),
        compiler_params=_params(("parallel",)),
        name="moe_router",
    )(x, w)
    return out[:, 0:2].astype(jnp.int32), out[:, 2:4]


def _expert_kernel(row_tok_ref, row_dst_ref, block_e_ref, x_hbm, wg_ref, wu_ref, wd_ref, y_hbm,
                   rows_ref, xb_ref, ybuf_ref, gather_sem, scatter_sem):
    i = pl.program_id(0)
    n = MOE_ROWS
    slot = lax.rem(i, 2)
    other = 1 - slot
    gather_row = lambda tok, s, r: pltpu.make_async_copy(
        x_hbm.at[pl.ds(tok, 1)], rows_ref.at[s, pl.ds(r, 1)], gather_sem.at[s])
    scatter_row = lambda dst, s, r: pltpu.make_async_copy(
        ybuf_ref.at[s, pl.ds(r, 1)], y_hbm.at[pl.ds(dst, 1)], scatter_sem.at[s])
    gather_block = lambda s: pltpu.make_async_copy(x_hbm.at[pl.ds(0, n)], rows_ref.at[s], gather_sem.at[s])
    scatter_block = lambda s: pltpu.make_async_copy(ybuf_ref.at[s], y_hbm.at[pl.ds(0, n)], scatter_sem.at[s])

    @pl.when(i == 0)
    def _():
        ybuf_ref[1] = jnp.zeros((n, D_MODEL), F32)

        def start(r, carry):
            gather_row(row_tok_ref[r], 0, r).start()
            return carry

        lax.fori_loop(0, n, start, 0)

    gather_block(slot).wait()
    xb_ref[...] = rows_ref[slot].astype(BF16)
    n_chunks = D_FF_EXPERT // FF_CHUNK
    rows_per_chunk = -(-n // (n_chunks - MOE_DMA_FREE_CHUNKS))
    for j in range(n_chunks):
        cols = slice(j * FF_CHUNK, (j + 1) * FF_CHUNK)
        xb = xb_ref[...]
        h = _silu(_dot(xb, wg_ref[0, :, cols])) * _dot(xb, wu_ref[0, :, cols])
        part = _dot(h.astype(BF16), wd_ref[0, cols, :])
        if j == 0:
            ybuf_ref[slot] = part
        else:
            ybuf_ref[slot] += part
        for r in range(j * rows_per_chunk, min((j + 1) * rows_per_chunk, n)):
            gather_row(row_tok_ref[(i + 1) * n + r], other, r).start(priority=GATHER_DMA_PRIORITY)
            scatter_row(row_dst_ref[i * n + r], other, r).start(priority=SCATTER_DMA_PRIORITY)
    scatter_block(other).wait()

    @pl.when(i == pl.num_programs(0) - 1)
    def _():
        gather_block(other).wait()

        def start(r, carry):
            scatter_row(row_dst_ref[(i + 1) * n + r], slot, r).start()
            return carry

        lax.fori_loop(0, n, start, 0)
        scatter_block(slot).wait()


def _expert_rows(x, row_tok, row_dst, block_e, n_out_rows, w_gate, w_up, w_down):
    n_blocks = block_e.shape[0]
    whole = lambda shape: pl.BlockSpec((1,) + shape, lambda i, tok, dst, be: (be[i], 0, 0),
                                       pipeline_mode=pl.Buffered(1))
    grid_spec = pltpu.PrefetchScalarGridSpec(
        num_scalar_prefetch=3,
        grid=(n_blocks,),
        in_specs=[pl.BlockSpec(memory_space=pl.ANY),
                  whole((D_MODEL, D_FF_EXPERT)), whole((D_MODEL, D_FF_EXPERT)), whole((D_FF_EXPERT, D_MODEL))],
        out_specs=pl.BlockSpec(memory_space=pl.ANY),
        scratch_shapes=[pltpu.VMEM((2, MOE_ROWS, D_MODEL), F32), pltpu.VMEM((MOE_ROWS, D_MODEL), BF16),
                        pltpu.VMEM((2, MOE_ROWS, D_MODEL), F32),
                        pltpu.SemaphoreType.DMA((2,)), pltpu.SemaphoreType.DMA((2,))],
    )
    return pl.pallas_call(
        _expert_kernel,
        grid_spec=grid_spec,
        out_shape=jax.ShapeDtypeStruct((n_out_rows, D_MODEL), F32),
        compiler_params=_params(("arbitrary",)),
        name="moe_experts",
    )(row_tok, row_dst, block_e, x, w_gate.astype(BF16), w_up.astype(BF16), w_down.astype(BF16))


def _combine_kernel(alpha, x_ref, y0_ref, y1_ref, gates_ref, p_ref, g_ref, b_ref, pg_ref, pp_ref, o_ref):
    gates = gates_ref[...]
    f = y0_ref[...] * gates[:, 0:1] + y1_ref[...] * gates[:, 1:2]
    o_ref[...] = _ln_embed(alpha * x_ref[...] + f, g_ref[...], b_ref[...], p_ref[...], pg_ref[...], pp_ref[...])


def _moe_combine_tail(alpha, x, p, y, gates, g, b, ple_gate, ple_proj):
    t = x.shape[0]
    n = ROW_TILE
    row_d = pl.BlockSpec((n, D_MODEL), lambda i: (i, 0))
    return pl.pallas_call(
        functools.partial(_combine_kernel, alpha),
        grid=(t // n,),
        in_specs=[row_d, row_d, pl.BlockSpec((n, D_MODEL), lambda i: (t // n + i, 0)),
                  pl.BlockSpec((n, 2), lambda i: (i, 0)), pl.BlockSpec((n, PLE_DIM), lambda i: (i, 0)),
                  _resident((1, D_MODEL)), _resident((1, D_MODEL)),
                  _resident((D_MODEL, D_MODEL)), _resident((PLE_DIM, D_MODEL))],
        out_specs=row_d,
        out_shape=jax.ShapeDtypeStruct((t, D_MODEL), F32),
        compiler_params=_params(("parallel",)),
        name="moe_combine_tail",
    )(x, y, y, gates, p, g.reshape(1, -1), b.reshape(1, -1), ple_gate.astype(BF16), ple_proj.astype(BF16))


def _moe_tail(alpha, x, p, router, w_gate, w_up, w_down, g, b, ple_gate, ple_proj):
    t = x.shape[0]
    experts, gates = _route(x, router)
    e_flat = experts.reshape(-1)
    onehot = (e_flat[:, None] == jnp.arange(N_EXPERTS, dtype=jnp.int32)[None, :]).astype(jnp.int32)
    running = jnp.cumsum(onehot, axis=0)
    rank = jnp.sum(onehot * (running - 1), axis=1)
    counts = running[-1]
    padded = (counts + MOE_ROWS - 1) // MOE_ROWS * MOE_ROWS
    pad_end = jnp.cumsum(padded)
    pad_start = pad_end - padded
    dest = (pad_start[e_flat] + rank).astype(jnp.int32)
    n_blocks = (2 * t) // MOE_ROWS + N_EXPERTS
    n_rows = n_blocks * MOE_ROWS
    assign = jnp.arange(2 * t, dtype=jnp.int32)
    row_assign = jnp.full((n_rows,), -1, jnp.int32).at[dest].set(assign)
    used = row_assign >= 0
    spare = 2 * t + jnp.cumsum(jnp.where(used, 0, 1).astype(jnp.int32)) - 1
    row_tok = jnp.where(used, row_assign // 2, 0)
    row_dst = jnp.where(used, (row_assign % 2) * t + row_assign // 2, spare)
    n_spare = n_rows - 2 * t
    first_dst = 2 * t + n_spare + jnp.arange(MOE_ROWS, dtype=jnp.int32)
    row_tok = jnp.concatenate([row_tok, jnp.zeros((MOE_ROWS,), jnp.int32)])
    row_dst = jnp.concatenate([first_dst, row_dst])
    block_start = jnp.arange(n_blocks, dtype=jnp.int32) * MOE_ROWS
    block_e = jnp.minimum(jnp.sum((block_start[:, None] >= pad_end[None, :]).astype(jnp.int32), axis=1),
                          N_EXPERTS - 1)
    y = _expert_rows(x, row_tok, row_dst, block_e, 2 * t + n_spare + MOE_ROWS, w_gate, w_up, w_down)
    return _moe_combine_tail(alpha, x, p, y, gates, g, b, ple_gate, ple_proj)


def kernel(x, p, w_in, w_out, gla_gk_up, gla_gk_bias, gla_norm_w, hgrn_lower_bounds, hgrn_norm_w, swa_sinks, rwkv_mu, rwkv_w0, rwkv_w_up, rwkv_a0, rwkv_a_up, rwkv_g_up, rwkv_k_k, rwkv_k_a, rwkv_r_k, rwkv_lnx_w, rwkv_lnx_b, rwkv_vres_down, rwkv_vres_mu, rwkv_v0, rwkv_vres_up, ln1_g, ln1_b, ln2_g, ln2_b, ffn_w_gate, ffn_w_up, ffn_w_down, moe_router, moe_w_gate, moe_w_up, moe_w_down, ple_proj, ple_gate):
    bsz, seq, d = x.shape
    depth = w_in.shape[0]
    alpha = (2.0 * depth) ** 0.25
    lbs = jnp.cumsum(jax.nn.softmax(hgrn_lower_bounds.astype(F32), axis=0), axis=0)
    lbs = lbs - lbs[0]
    outs = []
    for bi in range(bsz):
        xt = x[bi]
        v_first = None
        for i in range(depth):
            w = _group_in_weights(w_in[i], None if i == 0 else rwkv_vres_down[i - 1])
            z_gla, z_hgrn, z_swa, z_rwkv = _in_proj(xt, w)
            o_gla = _gla_mixer(z_gla, gla_gk_up[i], gla_gk_bias[i], gla_norm_w[i])
            o_hgrn = _hgrn_mixer(z_hgrn, lbs[i], hgrn_norm_w[i])
            o_swa = _swa_mixer(z_swa, swa_sinks[i])
            vres = None if i == 0 else (v_first, rwkv_vres_mu[i - 1], rwkv_v0[i - 1], rwkv_vres_up[i - 1])
            rw = _rwkv_mixer(z_rwkv, rwkv_mu[i], rwkv_w0[i], rwkv_w_up[i], rwkv_a0[i], rwkv_a_up[i],
                             rwkv_g_up[i], rwkv_k_k[i], rwkv_k_a[i], rwkv_r_k[i].reshape(-1),
                             rwkv_lnx_w[i], rwkv_lnx_b[i], vres)
            if i == 0:
                o_rwkv, v_first = rw
            else:
                o_rwkv = rw
            xt = _out_proj_ln(alpha, xt, (o_gla, o_hgrn, o_swa, o_rwkv), w_out[i], ln1_g[i], ln1_b[i])
            j = i // 2
            if i % 2 == 0:
                xt = _dense_ffn_tail(alpha, xt, p[i, bi], ffn_w_gate[j], ffn_w_up[j], ffn_w_down[j],
                                     ln2_g[i], ln2_b[i], ple_gate[i], ple_proj[i])
            else:
                xt = _moe_tail(alpha, xt, p[i, bi], moe_router[j], moe_w_gate[j], moe_w_up[j], moe_w_down[j],
                               ln2_g[i], ln2_b[i], ple_gate[i], ple_proj[i])
        outs.append(xt)
    return jnp.stack(outs, axis=0)
```

```python
import functools

import jax
import jax.numpy as jnp
from jax import lax
from jax.experimental import pallas as pl
from jax.experimental.pallas import tpu as pltpu

F32 = jnp.float32
BF16 = jnp.bfloat16
HIGHEST = lax.Precision.HIGHEST
MIX_DTYPE = BF16

D_MODEL = 1024
GROUP_WIDTH = 256
N_HEADS = 4
HEAD_DIM = 64
GLA_DK = 32
GLA_GATE_RANK = 16
GLA_GATE_NORMALIZER = 16.0
SWA_WINDOW = 128
RWKV_COLS = 3 * GROUP_WIDTH + 16 + 16 + 32
RWKV_V_RANK = 8
LN_EPS = 1e-5
RMS_EPS = 1e-6
RWKV_GN_EPS = 64e-5
D_FF = 2816
N_EXPERTS = 8
D_FF_EXPERT = 3584
PLE_DIM = 256

LANES = 128
GLA_W = 896
HGRN_W = 1024
SWA_W = 512
RWKV_W = 896
Z_W = GLA_W + HGRN_W + SWA_W + RWKV_W

GLA_SUB = 16
GLA_TILE = 128
GLA_STEP = 1024
GLA_STAGGER = 1
SWA_STAGGER = 1
RWKV_CHUNK = 64
RWKV_TILE = 512
RWKV_STAGGER = 2
ROW_TILE = 512
FF_CHUNK = 256
SWA_TILE = 256
MOE_ROWS = 512
MOE_DMA_FREE_CHUNKS = 4
GATHER_DMA_PRIORITY = 0
SCATTER_DMA_PRIORITY = 1
VMEM_LIMIT = 56 * 1024 * 1024


def _iota(shape, dim):
    return lax.broadcasted_iota(jnp.int32, shape, dim)


def _idiv(x, n):
    return jnp.right_shift(x, n.bit_length() - 1)


def _imod(x, n):
    return jnp.bitwise_and(x, n - 1)


def _dot(a, b, precision=None):
    return jnp.dot(a, b, preferred_element_type=F32, precision=precision)


def _dot_nt(a, b, precision=None):
    return lax.dot_general(a, b, (((1,), (1,)), ((), ())), preferred_element_type=F32, precision=precision)


def _dot_tn(a, b, precision=None):
    return lax.dot_general(a, b, (((0,), (0,)), ((), ())), preferred_element_type=F32, precision=precision)


def _bdot(a, b):
    return _dot(a.astype(BF16), b.astype(BF16))


def _bdot_nt(a, b):
    return _dot_nt(a.astype(BF16), b.astype(BF16))


def _dot_hilo(x, m):
    hi = x.astype(BF16)
    lo = (x - hi.astype(F32)).astype(BF16)
    return _dot(hi, m) + _dot(lo, m)


def _dot_mask(m, x):
    m = m.astype(BF16)
    x1 = x.astype(BF16)
    r1 = x - x1.astype(F32)
    x2 = r1.astype(BF16)
    x3 = (r1 - x2.astype(F32)).astype(BF16)
    return _dot(m, x1) + _dot(m, x2) + _dot(m, x3)


def _dot_3pass(x, w):
    x_hi = x.astype(BF16)
    x_lo = (x - x_hi.astype(F32)).astype(BF16)
    w_hi = w.astype(BF16)
    w_lo = (w - w_hi.astype(F32)).astype(BF16)
    return _dot(x_hi, w_hi) + _dot(x_lo, w_hi) + _dot(x_hi, w_lo)


def _sigmoid(x):
    return 1.0 / (1.0 + jnp.exp(-x))


def _silu(x):
    return x * _sigmoid(x)


def _log_sigmoid(x):
    return jnp.minimum(x, 0.0) - jnp.log1p(jnp.exp(-jnp.abs(x)))


def _layer_norm(y, g, b):
    mu = jnp.mean(y, axis=-1, keepdims=True)
    d = y - mu
    var = jnp.mean(d * d, axis=-1, keepdims=True)
    return d * lax.rsqrt(var + LN_EPS) * g + b


def _expand_heads(x, head_width):
    lane_head = _idiv(_iota(x.shape, 1), head_width)
    return jnp.concatenate([jnp.where(lane_head == h, x, 0.0) for h in range(N_HEADS)], axis=0)


def _head_group_matrix(width, head_width, value):
    same = _idiv(_iota((width, width), 0), head_width) == _idiv(_iota((width, width), 1), head_width)
    return jnp.where(same, value, 0.0).astype(F32)


def _resident(shape):
    nd = len(shape)
    return pl.BlockSpec(shape, lambda *_: (0,) * nd, pipeline_mode=pl.Buffered(1))


def _params(semantics):
    return pltpu.CompilerParams(dimension_semantics=semantics, vmem_limit_bytes=VMEM_LIMIT)


def _inproj_kernel(x_ref, w_ref, gla_ref, hgrn_ref, swa_ref, rwkv_ref):
    xb = x_ref[...].astype(BF16)
    o = 0
    for ref, width in ((gla_ref, GLA_W), (hgrn_ref, HGRN_W), (swa_ref, SWA_W), (rwkv_ref, RWKV_W)):
        ref[...] = _dot(xb, w_ref[:, o:o + width])
        o += width


def _in_proj(x, w):
    t = x.shape[0]
    widths = (GLA_W, HGRN_W, SWA_W, RWKV_W)
    return pl.pallas_call(
        _inproj_kernel,
        grid=(t // ROW_TILE,),
        in_specs=[pl.BlockSpec((ROW_TILE, D_MODEL), lambda i: (i, 0)), _resident((D_MODEL, Z_W))],
        out_specs=[pl.BlockSpec((ROW_TILE, w_), lambda i: (i, 0)) for w_ in widths],
        out_shape=[jax.ShapeDtypeStruct((t, w_), F32) for w_ in widths],
        compiler_params=_params(("parallel",)),
        name="in_proj",
    )(x, w)


def _group_in_weights(w_in, vres_down):
    gla, hgrn, swa, rwkv = jnp.split(w_in, (784, 784 + 1024, 784 + 1024 + 512), axis=1)
    if vres_down is not None:
        rwkv = jnp.concatenate([rwkv, vres_down], axis=1)
    pad = lambda a, w_: jnp.pad(a, ((0, 0), (0, w_ - a.shape[1])))
    return jnp.concatenate([pad(gla, GLA_W), hgrn, swa, pad(rwkv, RWKV_W)], axis=1).astype(BF16)


def _gated_linear_attention_tile(index, q, k, v, log_f, state_box, out_box):
    length, kw = q.shape
    head_k = kw // N_HEADS
    n_sub = length // GLA_SUB
    row = _iota((length, length), 0)
    col = _iota((length, length), 1)
    same_sub = _idiv(row, GLA_SUB) == _idiv(col, GLA_SUB)
    m_local = jnp.where(same_sub & (col <= row), 1.0, 0.0).astype(F32)
    m_prev = jnp.where(_idiv(col, GLA_SUB) < _idiv(row, GLA_SUB), 1.0, 0.0).astype(F32)
    b_local = _dot_mask(m_local, log_f)
    b_start = _dot_mask(m_prev, log_f)
    yield
    b_full = b_start + b_local
    q_local = q * jnp.exp(b_local)

    q_pos = _imod(_iota((N_HEADS * GLA_SUB, length), 0), GLA_SUB)
    s_pos = _iota((N_HEADS * GLA_SUB, length), 1)
    probs = []
    for c in range(n_sub):
        r0, r1 = c * GLA_SUB, (c + 1) * GLA_SUB
        expo = jnp.where(_iota((r1, kw), 0) < r0, b_start[r0:r0 + 1, :] - b_full[0:r1], -b_local[0:r1])
        k_ref = k[0:r1] * jnp.exp(expo)
        if r1 < length:
            k_ref = jnp.concatenate([k_ref, jnp.zeros((length - r1, kw), F32)], axis=0)
        q_heads = _expand_heads(q_local[r0:r1, :], head_k)
        s = _bdot_nt(q_heads, k_ref)
        probs.append(jnp.where(s_pos <= q_pos + r0, s, 0.0))
    b_total = b_full[length - 1:length, :]
    k_end = k * jnp.exp(b_total - b_full)
    upd = _dot_tn(v.astype(BF16), k_end.astype(BF16))
    yield
    o_heads = _bdot(jnp.concatenate(probs, axis=0), v)
    assert len(out_box) == index, "the previous tile must have replaced the state before it is read"
    state_t = state_box[0]
    o_state = _bdot_nt(q * jnp.exp(b_full), state_t)
    same_head = _idiv(_iota((GROUP_WIDTH, kw), 0), HEAD_DIM) == _idiv(_iota((GROUP_WIDTH, kw), 1), head_k)
    state_box[0] = state_t * jnp.exp(b_total) + jnp.where(same_head, upd, 0.0)
    yield
    v_head = _idiv(_iota((GLA_SUB, GROUP_WIDTH), 1), HEAD_DIM)
    rows = []
    for c in range(n_sub):
        base = c * N_HEADS * GLA_SUB
        acc = jnp.zeros((GLA_SUB, GROUP_WIDTH), F32)
        for h in range(N_HEADS):
            acc = acc + jnp.where(v_head == h, o_heads[base + h * GLA_SUB:base + (h + 1) * GLA_SUB, :], 0.0)
        rows.append(acc)
    out_box.append(jnp.concatenate(rows, axis=0) + o_state)


def _gated_linear_attention(q, k, v, log_f, state_ref):
    state_box = [state_ref[...]]
    tiles = []
    stages = []
    for n in range(q.shape[0] // GLA_TILE):
        rows = slice(n * GLA_TILE, (n + 1) * GLA_TILE)
        stages.append(_gated_linear_attention_tile(n, q[rows], k[rows], v[rows], log_f[rows], state_box, tiles))
    _run_staggered(stages, GLA_STAGGER)
    state_ref[...] = state_box[0]
    return jnp.concatenate(tiles, axis=0)


def _head_rms_gate(o, norm_w, gate):
    ms = _dot_hilo(o * o, _head_group_matrix(GROUP_WIDTH, HEAD_DIM, 1.0 / HEAD_DIM).astype(BF16))
    return o * lax.rsqrt(ms + RMS_EPS) * norm_w * _silu(gate)


def _gla_kernel(z_ref, gk_up_ref, gk_bias_ref, norm_w_ref, o_ref, state_ref):
    @pl.when(pl.program_id(0) == 0)
    def _():
        state_ref[...] = jnp.zeros_like(state_ref)

    z = z_ref[...]
    q = z[:, 0:128] * (GLA_DK ** -0.5)
    k = z[:, 128:256]
    v = z[:, 256:512]
    g = z[:, 512:768]
    gate_in = _dot_3pass(z[:, 768:896], gk_up_ref[...]) + gk_bias_ref[...]
    log_f = _log_sigmoid(gate_in) * (1.0 / GLA_GATE_NORMALIZER)
    o = _gated_linear_attention(q, k, v, log_f, state_ref)
    o_ref[...] = _head_rms_gate(o, norm_w_ref[...], g).astype(o_ref.dtype)


def _hgrn_kernel(z_ref, lb_ref, log_lb_ref, norm_w_ref, o_ref, state_ref):
    @pl.when(pl.program_id(0) == 0)
    def _():
        state_ref[...] = jnp.zeros_like(state_ref)

    z = z_ref[...]
    q = _silu(z[:, 0:256])
    f = z[:, 256:512]
    v = z[:, 512:768]
    g = z[:, 768:1024]
    lb = lb_ref[...]
    a = log_lb_ref[...]
    c = jnp.log1p(-lb) + _log_sigmoid(f)
    log_f = jnp.maximum(a, c) + jnp.log1p(jnp.exp(-jnp.abs(a - c)))
    k = (1.0 - lb) * _sigmoid(-f)
    o = _gated_linear_attention(q, k, v, log_f, state_ref)
    o_ref[...] = _head_rms_gate(o, norm_w_ref[...], g).astype(o_ref.dtype)


def _gla_mixer(z, gk_up, gk_bias, norm_w):
    t = z.shape[0]
    gk_up_pad = jnp.zeros((LANES, N_HEADS * GLA_DK), F32).at[:GLA_GATE_RANK].set(gk_up)
    return pl.pallas_call(
        _gla_kernel,
        grid=(t // GLA_STEP,),
        in_specs=[pl.BlockSpec((GLA_STEP, GLA_W), lambda i: (i, 0)),
                  _resident((LANES, N_HEADS * GLA_DK)), _resident((1, N_HEADS * GLA_DK)),
                  _resident((1, GROUP_WIDTH))],
        out_specs=pl.BlockSpec((GLA_STEP, GROUP_WIDTH), lambda i: (i, 0)),
        out_shape=jax.ShapeDtypeStruct((t, GROUP_WIDTH), MIX_DTYPE),
        scratch_shapes=[pltpu.VMEM((GROUP_WIDTH, N_HEADS * GLA_DK), F32)],
        compiler_params=_params(("arbitrary",)),
        name="gla_mixer",
    )(z, gk_up_pad, gk_bias.reshape(1, -1), jnp.tile(norm_w, N_HEADS).reshape(1, -1))


def _hgrn_mixer(z, lb, norm_w):
    t = z.shape[0]
    return pl.pallas_call(
        _hgrn_kernel,
        grid=(t // GLA_STEP,),
        in_specs=[pl.BlockSpec((GLA_STEP, HGRN_W), lambda i: (i, 0)),
                  _resident((1, GROUP_WIDTH)), _resident((1, GROUP_WIDTH)), _resident((1, GROUP_WIDTH))],
        out_specs=pl.BlockSpec((GLA_STEP, GROUP_WIDTH), lambda i: (i, 0)),
        out_shape=jax.ShapeDtypeStruct((t, GROUP_WIDTH), MIX_DTYPE),
        scratch_shapes=[pltpu.VMEM((GROUP_WIDTH, GROUP_WIDTH), F32)],
        compiler_params=_params(("arbitrary",)),
        name="hgrn_mixer",
    )(z, lb.reshape(1, -1), jnp.log(lb).reshape(1, -1), jnp.tile(norm_w, N_HEADS).reshape(1, -1))


def _swa_block(q, kw, vw, visible, sinks, out_box):
    n = q.shape[0]
    head_cols = lambda x, h: x[:, h * HEAD_DIM:(h + 1) * HEAD_DIM]
    v_lane_head = _idiv(_iota(vw.shape, 1), HEAD_DIM)
    outs = []
    for kv in range(N_HEADS // 2):
        heads = (2 * kv, 2 * kv + 1)
        scores = _bdot_nt(jnp.concatenate([head_cols(q, h) for h in heads], axis=0), head_cols(kw, kv))
        yield
        probs, sink_terms = [], []
        for half, h in enumerate(heads):
            s = jnp.where(visible, scores[half * n:(half + 1) * n], -jnp.inf)
            sink = sinks[:, h:h + 1]
            m = jnp.maximum(jnp.max(s, axis=-1, keepdims=True), sink)
            probs.append(jnp.exp(s - m))
            sink_terms.append(jnp.exp(sink - m))
        o = _bdot(jnp.concatenate(probs, axis=0), jnp.where(v_lane_head == kv, vw, 1.0))
        yield
        sums = head_cols(o, 1 - kv)[:, 0:1]
        outs += [head_cols(o, kv)[half * n:(half + 1) * n] / (sums[half * n:(half + 1) * n] + sink_terms[half])
                 for half in range(2)]
    out_box.append(jnp.concatenate(outs, axis=-1))


def _swa_kernel(q_ref, k_ref, v_ref, kp_ref, vp_ref, sink_ref, o_ref):
    w = SWA_WINDOW
    has_prev = pl.program_id(0) > 0
    q = q_ref[...] * (HEAD_DIM ** -0.5)
    k_all = jnp.concatenate([kp_ref[...], k_ref[...]], axis=0)
    v_all = jnp.concatenate([vp_ref[...], v_ref[...]], axis=0)
    q_pos = _iota((w, 2 * w), 0) + w
    k_pos = _iota((w, 2 * w), 1)
    dist = q_pos - k_pos
    in_window = (dist >= 0) & (dist < w)
    sinks = sink_ref[...]
    blocks = []
    stages = []
    for b in range(q.shape[0] // w):
        visible = in_window if b > 0 else in_window & ((k_pos >= w) | has_prev)
        stages.append(_swa_block(q[b * w:(b + 1) * w], k_all[b * w:(b + 2) * w], v_all[b * w:(b + 2) * w],
                                 visible, sinks, blocks))
    _run_staggered(stages, SWA_STAGGER)
    o_ref[...] = jnp.concatenate(blocks, axis=0).astype(o_ref.dtype)


def _swa_mixer(z, sinks):
    t = z.shape[0]
    w = SWA_WINDOW
    n = SWA_TILE
    prev = lambda col: (lambda i: (jnp.maximum(i * (n // w) - 1, 0), col))
    return pl.pallas_call(
        _swa_kernel,
        grid=(t // n,),
        in_specs=[pl.BlockSpec((n, 256), lambda i: (i, 0)),
                  pl.BlockSpec((n, 128), lambda i: (i, 2)), pl.BlockSpec((n, 128), lambda i: (i, 3)),
                  pl.BlockSpec((w, 128), prev(2)), pl.BlockSpec((w, 128), prev(3)),
                  _resident((1, N_HEADS))],
        out_specs=pl.BlockSpec((n, GROUP_WIDTH), lambda i: (i, 0)),
        out_shape=jax.ShapeDtypeStruct((t, GROUP_WIDTH), MIX_DTYPE),
        compiler_params=_params(("parallel",)),
        name="swa_mixer",
    )(z, z, z, z, z, sinks.reshape(1, -1))


def _run_staggered(stage_generators, stagger):
    live = dict(enumerate(stage_generators))
    rnd = 0
    while live:
        for n in sorted(live):
            if rnd >= n * stagger:
                try:
                    next(live[n])
                except StopIteration:
                    del live[n]
        rnd += 1


def _rwkv_chunk(index, r, k, v, a_vec, b_vec, log_w, state_box, out_box):
    c = r.shape[0]
    width = r.shape[1]
    tri = jnp.where(_iota((c, c), 1) <= _iota((c, c), 0), 1.0, 0.0).astype(F32)
    p = _dot_mask(tri, log_w)
    yield
    p_total = p[c - 1:c, :]
    decay_in = jnp.exp(p)
    decay_out = jnp.exp(-p)
    decay_end = jnp.exp(p_total - p)
    a_in = a_vec * jnp.exp(p - log_w)
    r_in = r * decay_in
    b_out = b_vec * decay_out
    k_out = k * decay_out
    b_end = b_vec * decay_end
    k_end = k * decay_end

    t_pos = _iota((c, width), 0)
    assert width == N_HEADS * c
    s_pos = _imod(_iota((c, width), 1), c)
    strict = s_pos < t_pos
    incl = s_pos <= t_pos
    expand = lambda x: _expand_heads(x, HEAD_DIM)

    scores = _bdot_nt(jnp.concatenate([a_in, r_in], axis=0),
                      jnp.concatenate([expand(b_out), expand(k_out)], axis=0))
    yield
    a_ab = jnp.where(strict, scores[0:c, 0:width], 0.0)
    a_ak = jnp.where(strict, scores[0:c, width:2 * width], 0.0)
    a_rb = jnp.where(incl, scores[c:2 * c, 0:width], 0.0)
    a_rk = jnp.where(incl, scores[c:2 * c, width:2 * width], 0.0)

    t_inv = jnp.where(s_pos == t_pos, 1.0, 0.0) + a_ab
    v_heads = expand(v)
    x1 = _bdot(a_ak, v_heads)
    power = _bdot(a_ab, expand(a_ab))
    yield
    n_factors = (c - 1).bit_length()
    for _ in range(n_factors - 2):
        power_heads = expand(power)
        t_inv = t_inv + _bdot(t_inv, power_heads)
        power = _bdot(power, power_heads)
        yield
    t_inv = t_inv + _bdot(t_inv, expand(power))
    yield
    sol = _bdot(t_inv, jnp.concatenate([expand(x1), expand(a_in)], axis=1))
    yield
    u0 = sol[:, 0:width]
    w_mat = sol[:, width:2 * width]

    assert len(out_box) == index, "the previous chunk must have replaced the state before it is read"
    state = state_box[0]
    from_state = _bdot_nt(jnp.concatenate([w_mat, r_in], axis=0), state)
    yield
    u = u0 + from_state[0:c]
    y = _bdot(a_rb, expand(u)) + _bdot(a_rk, v_heads) + from_state[c:2 * c]
    upd = _dot_tn(jnp.concatenate([u, v], axis=0).astype(BF16),
                  jnp.concatenate([b_end, k_end], axis=0).astype(BF16))
    same_head = _idiv(_iota((width, width), 0), HEAD_DIM) == _idiv(_iota((width, width), 1), HEAD_DIM)
    state_box[0] = state * jnp.exp(p_total) + jnp.where(same_head, upd, 0.0)
    out_box.append(y)


def _rwkv_kernel(has_vres, *refs):
    if has_vres:
        (z_ref, zp_ref, vfirst_ref, mu_ref, w0_ref, wup_ref, a0_ref, aup_ref, gup_ref, kk_ref, ka_ref,
         rk_ref, lnw_ref, lnb_ref, v0_ref, vup_ref, o_ref, state_ref) = refs
    else:
        (z_ref, zp_ref, mu_ref, w0_ref, wup_ref, a0_ref, aup_ref, gup_ref, kk_ref, ka_ref,
         rk_ref, lnw_ref, lnb_ref, o_ref, vout_ref, state_ref) = refs
    step = pl.program_id(0)

    @pl.when(step == 0)
    def _():
        state_ref[...] = jnp.zeros_like(state_ref)

    z = z_ref[...]
    last_prev = jnp.where(step > 0, zp_ref[7:8, :], 0.0)
    prev = jnp.where(_iota(z.shape, 0) == 0, last_prev, pltpu.roll(z, 1, axis=0))
    zr = z + (prev - z) * mu_ref[...]
    r = zr[:, 0:256]
    k = zr[:, 256:512]
    v = zr[:, 512:768]
    low = zr[:, 768:896]
    w_pre = w0_ref[...] + _dot_3pass(jnp.tanh(low), wup_ref[...])
    w_log = -(jnp.maximum(-w_pre, 0.0) + jnp.log1p(jnp.exp(-jnp.abs(w_pre)))) - 0.5
    log_w = -jnp.exp(w_log)
    a = _sigmoid(a0_ref[...] + _dot_3pass(low, aup_ref[...]))
    g = _dot_3pass(_sigmoid(low), gup_ref[...])
    if has_vres:
        v = v + (vfirst_ref[...] - v) * _sigmoid(v0_ref[...] + _dot_3pass(low, vup_ref[...]))
    else:
        vout_ref[...] = v
    head_sum = _head_group_matrix(GROUP_WIDTH, HEAD_DIM, 1.0).astype(BF16)
    kk = k * kk_ref[...]
    kk = kk / jnp.maximum(jnp.sqrt(_dot_hilo(kk * kk, head_sum)), 1e-12)
    k = k * (1.0 + (a - 1.0) * ka_ref[...])
    a_vec = -kk
    b_vec = kk * a

    c = RWKV_CHUNK
    state_box = [state_ref[...]]
    chunks = []
    stages = []
    for n in range(z.shape[0] // c):
        rows = slice(n * c, (n + 1) * c)
        stages.append(_rwkv_chunk(n, r[rows], k[rows], v[rows], a_vec[rows], b_vec[rows], log_w[rows],
                                  state_box, chunks))
    _run_staggered(stages, RWKV_STAGGER)
    state_ref[...] = state_box[0]
    y = jnp.concatenate(chunks, axis=0)

    head_mean = _head_group_matrix(GROUP_WIDTH, HEAD_DIM, 1.0 / HEAD_DIM).astype(BF16)
    mu_y = _dot_hilo(y, head_mean)
    d = y - mu_y
    var_y = _dot_hilo(d * d, head_mean)
    y = d * lax.rsqrt(var_y + RWKV_GN_EPS) * lnw_ref[...] + lnb_ref[...]
    bonus = _dot_hilo(r * k * rk_ref[...], head_sum) * v
    o_ref[...] = ((y + bonus) * g).astype(o_ref.dtype)


def _rwkv_mixer(z, mu, w0, w_up, a0, a_up, g_up, k_k, k_a, r_k, lnx_w, lnx_b, vres):
    t = z.shape[0]
    c = RWKV_TILE
    row = lambda a: a.reshape(1, -1)
    low_rows = lambda a, start: jnp.zeros((LANES, GROUP_WIDTH), F32).at[start:start + a.shape[0]].set(a)
    has_vres = vres is not None
    mu_full = jnp.zeros((RWKV_W,), F32).at[:RWKV_COLS].set(mu)
    vec = _resident((1, GROUP_WIDTH))
    mat = _resident((LANES, GROUP_WIDTH))
    tile = pl.BlockSpec((c, GROUP_WIDTH), lambda i: (i, 0))
    z_specs = [pl.BlockSpec((c, RWKV_W), lambda i: (i, 0)),
               pl.BlockSpec((8, RWKV_W), lambda i: (jnp.maximum(i * (c // 8) - 1, 0), 0))]
    common = [row(w0), low_rows(w_up, 0), row(a0), low_rows(a_up, 16), low_rows(g_up, 32),
              row(k_k), row(k_a), row(r_k), row(lnx_w), row(lnx_b)]
    common_specs = [vec, mat, vec, mat, mat, vec, vec, vec, vec, vec]
    if has_vres:
        v_first, vres_mu, v0, v_up = vres
        mu_full = mu_full.at[RWKV_COLS:RWKV_COLS + RWKV_V_RANK].set(vres_mu)
        args = [z, z, v_first, row(mu_full)] + common + [row(v0), low_rows(v_up, 64)]
        in_specs = z_specs + [tile, _resident((1, RWKV_W))] + common_specs + [vec, mat]
        out_specs = tile
        out_shape = jax.ShapeDtypeStruct((t, GROUP_WIDTH), MIX_DTYPE)
    else:
        args = [z, z, row(mu_full)] + common
        in_specs = z_specs + [_resident((1, RWKV_W))] + common_specs
        out_specs = [tile, tile]
        out_shape = [jax.ShapeDtypeStruct((t, GROUP_WIDTH), MIX_DTYPE), jax.ShapeDtypeStruct((t, GROUP_WIDTH), F32)]
    return pl.pallas_call(
        functools.partial(_rwkv_kernel, has_vres),
        grid=(t // c,),
        in_specs=in_specs,
        out_specs=out_specs,
        out_shape=out_shape,
        scratch_shapes=[pltpu.VMEM((GROUP_WIDTH, GROUP_WIDTH), F32)],
        compiler_params=_params(("arbitrary",)),
        name="rwkv_mixer",
    )(*args)


def _outproj_kernel(alpha, x_ref, o0_ref, o1_ref, o2_ref, o3_ref, w_ref, g_ref, b_ref, y_ref):
    acc = alpha * x_ref[...]
    for h, ref in enumerate((o0_ref, o1_ref, o2_ref, o3_ref)):
        acc = acc + _dot(ref[...], w_ref[h * GROUP_WIDTH:(h + 1) * GROUP_WIDTH, :])
    y_ref[...] = _layer_norm(acc, g_ref[...], b_ref[...])


def _out_proj_ln(alpha, x, mixes, w_out, g, b):
    t = x.shape[0]
    row_d = pl.BlockSpec((ROW_TILE, D_MODEL), lambda i: (i, 0))
    row_g = pl.BlockSpec((ROW_TILE, GROUP_WIDTH), lambda i: (i, 0))
    return pl.pallas_call(
        functools.partial(_outproj_kernel, alpha),
        grid=(t // ROW_TILE,),
        in_specs=[row_d, row_g, row_g, row_g, row_g, _resident((D_MODEL, D_MODEL)),
                  _resident((1, D_MODEL)), _resident((1, D_MODEL))],
        out_specs=row_d,
        out_shape=jax.ShapeDtypeStruct((t, D_MODEL), F32),
        compiler_params=_params(("parallel",)),
        name="out_proj_ln",
    )(x, *mixes, w_out.astype(BF16), g.reshape(1, -1), b.reshape(1, -1))


def _ln_embed(y, ln_g, ln_b, p, ple_gate, ple_proj):
    x = _layer_norm(y, ln_g, ln_b)
    gate = _sigmoid(_dot(x.astype(BF16), ple_gate))
    return x + gate * _dot(p.astype(BF16), ple_proj)


def _dense_ffn_kernel(alpha, x_ref, p_ref, wg_ref, wu_ref, wd_ref, g_ref, b_ref, pg_ref, pp_ref, y_ref, acc_ref):
    x = x_ref[...]
    xb = x.astype(BF16)
    acc_ref[...] = alpha * x
    for j in range(D_FF // FF_CHUNK):
        cols = slice(j * FF_CHUNK, (j + 1) * FF_CHUNK)
        h = _silu(_dot(xb, wg_ref[:, cols])) * _dot(xb, wu_ref[:, cols])
        acc_ref[...] += _dot(h.astype(BF16), wd_ref[cols, :])
    y_ref[...] = _ln_embed(acc_ref[...], g_ref[...], b_ref[...], p_ref[...], pg_ref[...], pp_ref[...])


def _dense_ffn_tail(alpha, x, p, w_gate, w_up, w_down, g, b, ple_gate, ple_proj):
    t = x.shape[0]
    row_d = pl.BlockSpec((ROW_TILE, D_MODEL), lambda i: (i, 0))
    return pl.pallas_call(
        functools.partial(_dense_ffn_kernel, alpha),
        grid=(t // ROW_TILE,),
        in_specs=[row_d, pl.BlockSpec((ROW_TILE, PLE_DIM), lambda i: (i, 0)),
                  _resident((D_MODEL, D_FF)), _resident((D_MODEL, D_FF)), _resident((D_FF, D_MODEL)),
                  _resident((1, D_MODEL)), _resident((1, D_MODEL)),
                  _resident((D_MODEL, D_MODEL)), _resident((PLE_DIM, D_MODEL))],
        out_specs=row_d,
        out_shape=jax.ShapeDtypeStruct((t, D_MODEL), F32),
        scratch_shapes=[pltpu.VMEM((ROW_TILE, D_MODEL), F32)],
        compiler_params=_params(("parallel",)),
        name="dense_ffn_tail",
    )(x, p, w_gate.astype(BF16), w_up.astype(BF16), w_down.astype(BF16), g.reshape(1, -1), b.reshape(1, -1),
      ple_gate.astype(BF16), ple_proj.astype(BF16))


def _router_kernel(x_ref, w_ref, o_ref):
    logits = _dot(x_ref[...], w_ref[...], HIGHEST)
    lane = _iota(logits.shape, 1).astype(F32)
    logits = jnp.where(lane < N_EXPERTS, logits, -jnp.inf)
    m1 = jnp.max(logits, axis=-1, keepdims=True)
    i1 = jnp.min(jnp.where(logits == m1, lane, LANES), axis=-1, keepdims=True)
    rest = jnp.where(lane == i1, -jnp.inf, logits)
    m2 = jnp.max(rest, axis=-1, keepdims=True)
    i2 = jnp.min(jnp.where(rest == m2, lane, LANES), axis=-1, keepdims=True)
    e2 = jnp.exp(m2 - m1)
    g1 = 1.0 / (1.0 + e2)
    g2 = e2 * g1
    o_ref[...] = jnp.where(lane == 0, i1, jnp.where(lane == 1, i2,
                           jnp.where(lane == 2, g1, jnp.where(lane == 3, g2, 0.0))))


def _route(x, router):
    t = x.shape[0]
    w = jnp.zeros((D_MODEL, LANES), F32).at[:, :N_EXPERTS].set(router)
    out = pl.pallas_call(
        _router_kernel,
        grid=(t // ROW_TILE,),
        in_specs=[pl.BlockSpec((ROW_TILE, D_MODEL), lambda i: (i, 0)), _resident((D_MODEL, LANES))],
        out_specs=pl.BlockSpec((ROW_TILE, LANES), lambda i: (i, 0)),
        out_shape=jax.ShapeDtypeStruct((t, LANES), F32),
        compiler_params=_params(("parallel",)),
        name="moe_router",
    )(x, w)
    return out[:, 0:2].astype(jnp.int32), out[:, 2:4]


def _expert_kernel(row_tok_ref, row_dst_ref, block_e_ref, x_hbm, wg_ref, wu_ref, wd_ref, y_hbm,
                   rows_ref, xb_ref, ybuf_ref, gather_sem, scatter_sem):
    i = pl.program_id(0)
    n = MOE_ROWS
    slot = lax.rem(i, 2)
    other = 1 - slot
    gather_row = lambda tok, s, r: pltpu.make_async_copy(
        x_hbm.at[pl.ds(tok, 1)], rows_ref.at[s, pl.ds(r, 1)], gather_sem.at[s])
    scatter_row = lambda dst, s, r: pltpu.make_async_copy(
        ybuf_ref.at[s, pl.ds(r, 1)], y_hbm.at[pl.ds(dst, 1)], scatter_sem.at[s])
    gather_block = lambda s: pltpu.make_async_copy(x_hbm.at[pl.ds(0, n)], rows_ref.at[s], gather_sem.at[s])
    scatter_block = lambda s: pltpu.make_async_copy(ybuf_ref.at[s], y_hbm.at[pl.ds(0, n)], scatter_sem.at[s])

    @pl.when(i == 0)
    def _():
        ybuf_ref[1] = jnp.zeros((n, D_MODEL), F32)

        def start(r, carry):
            gather_row(row_tok_ref[r], 0, r).start()
            return carry

        lax.fori_loop(0, n, start, 0)

    gather_block(slot).wait()
    xb_ref[...] = rows_ref[slot].astype(BF16)
    n_chunks = D_FF_EXPERT // FF_CHUNK
    rows_per_chunk = -(-n // (n_chunks - MOE_DMA_FREE_CHUNKS))
    for j in range(n_chunks):
        cols = slice(j * FF_CHUNK, (j + 1) * FF_CHUNK)
        xb = xb_ref[...]
        h = _silu(_dot(xb, wg_ref[0, :, cols])) * _dot(xb, wu_ref[0, :, cols])
        part = _dot(h.astype(BF16), wd_ref[0, cols, :])
        if j == 0:
            ybuf_ref[slot] = part
        else:
            ybuf_ref[slot] += part
        for r in range(j * rows_per_chunk, min((j + 1) * rows_per_chunk, n)):
            gather_row(row_tok_ref[(i + 1) * n + r], other, r).start(priority=GATHER_DMA_PRIORITY)
            scatter_row(row_dst_ref[i * n + r], other, r).start(priority=SCATTER_DMA_PRIORITY)
    scatter_block(other).wait()

    @pl.when(i == pl.num_programs(0) - 1)
    def _():
        gather_block(other).wait()

        def start(r, carry):
            scatter_row(row_dst_ref[(i + 1) * n + r], slot, r).start()
            return carry

        lax.fori_loop(0, n, start, 0)
        scatter_block(slot).wait()


def _expert_rows(x, row_tok, row_dst, block_e, n_out_rows, w_gate, w_up, w_down):
    n_blocks = block_e.shape[0]
    whole = lambda shape: pl.BlockSpec((1,) + shape, lambda i, tok, dst, be: (be[i], 0, 0),
                                       pipeline_mode=pl.Buffered(1))
    grid_spec = pltpu.PrefetchScalarGridSpec(
        num_scalar_prefetch=3,
        grid=(n_blocks,),
        in_specs=[pl.BlockSpec(memory_space=pl.ANY),
                  whole((D_MODEL, D_FF_EXPERT)), whole((D_MODEL, D_FF_EXPERT)), whole((D_FF_EXPERT, D_MODEL))],
        out_specs=pl.BlockSpec(memory_space=pl.ANY),
        scratch_shapes=[pltpu.VMEM((2, MOE_ROWS, D_MODEL), F32), pltpu.VMEM((MOE_ROWS, D_MODEL), BF16),
                        pltpu.VMEM((2, MOE_ROWS, D_MODEL), F32),
                        pltpu.SemaphoreType.DMA((2,)), pltpu.SemaphoreType.DMA((2,))],
    )
    return pl.pallas_call(
        _expert_kernel,
        grid_spec=grid_spec,
        out_shape=jax.ShapeDtypeStruct((n_out_rows, D_MODEL), F32),
        compiler_params=_params(("arbitrary",)),
        name="moe_experts",
    )(row_tok, row_dst, block_e, x, w_gate.astype(BF16), w_up.astype(BF16), w_down.astype(BF16))


def _combine_kernel(alpha, x_ref, y0_ref, y1_ref, gates_ref, p_ref, g_ref, b_ref, pg_ref, pp_ref, o_ref):
    gates = gates_ref[...]
    f = y0_ref[...] * gates[:, 0:1] + y1_ref[...] * gates[:, 1:2]
    o_ref[...] = _ln_embed(alpha * x_ref[...] + f, g_ref[...], b_ref[...], p_ref[...], pg_ref[...], pp_ref[...])


def _moe_combine_tail(alpha, x, p, y, gates, g, b, ple_gate, ple_proj):
    t = x.shape[0]
    n = ROW_TILE
    row_d = pl.BlockSpec((n, D_MODEL), lambda i: (i, 0))
    return pl.pallas_call(
        functools.partial(_combine_kernel, alpha),
        grid=(t // n,),
        in_specs=[row_d, row_d, pl.BlockSpec((n, D_MODEL), lambda i: (t // n + i, 0)),
                  pl.BlockSpec((n, 2), lambda i: (i, 0)), pl.BlockSpec((n, PLE_DIM), lambda i: (i, 0)),
                  _resident((1, D_MODEL)), _resident((1, D_MODEL)),
                  _resident((D_MODEL, D_MODEL)), _resident((PLE_DIM, D_MODEL))],
        out_specs=row_d,
        out_shape=jax.ShapeDtypeStruct((t, D_MODEL), F32),
        compiler_params=_params(("parallel",)),
        name="moe_combine_tail",
    )(x, y, y, gates, p, g.reshape(1, -1), b.reshape(1, -1), ple_gate.astype(BF16), ple_proj.astype(BF16))


def _moe_tail(alpha, x, p, router, w_gate, w_up, w_down, g, b, ple_gate, ple_proj):
    t = x.shape[0]
    experts, gates = _route(x, router)
    e_flat = experts.reshape(-1)
    onehot = (e_flat[:, None] == jnp.arange(N_EXPERTS, dtype=jnp.int32)[None, :]).astype(jnp.int32)
    running = jnp.cumsum(onehot, axis=0)
    rank = jnp.sum(onehot * (running - 1), axis=1)
    counts = running[-1]
    padded = (counts + MOE_ROWS - 1) // MOE_ROWS * MOE_ROWS
    pad_end = jnp.cumsum(padded)
    pad_start = pad_end - padded
    dest = (pad_start[e_flat] + rank).astype(jnp.int32)
    n_blocks = (2 * t) // MOE_ROWS + N_EXPERTS
    n_rows = n_blocks * MOE_ROWS
    assign = jnp.arange(2 * t, dtype=jnp.int32)
    row_assign = jnp.full((n_rows,), -1, jnp.int32).at[dest].set(assign)
    used = row_assign >= 0
    spare = 2 * t + jnp.cumsum(jnp.where(used, 0, 1).astype(jnp.int32)) - 1
    row_tok = jnp.where(used, row_assign // 2, 0)
    row_dst = jnp.where(used, (row_assign % 2) * t + row_assign // 2, spare)
    n_spare = n_rows - 2 * t
    first_dst = 2 * t + n_spare + jnp.arange(MOE_ROWS, dtype=jnp.int32)
    row_tok = jnp.concatenate([row_tok, jnp.zeros((MOE_ROWS,), jnp.int32)])
    row_dst = jnp.concatenate([first_dst, row_dst])
    block_start = jnp.arange(n_blocks, dtype=jnp.int32) * MOE_ROWS
    block_e = jnp.minimum(jnp.sum((block_start[:, None] >= pad_end[None, :]).astype(jnp.int32), axis=1),
                          N_EXPERTS - 1)
    y = _expert_rows(x, row_tok, row_dst, block_e, 2 * t + n_spare + MOE_ROWS, w_gate, w_up, w_down)
    return _moe_combine_tail(alpha, x, p, y, gates, g, b, ple_gate, ple_proj)


def kernel(x, p, w_in, w_out, gla_gk_up, gla_gk_bias, gla_norm_w, hgrn_lower_bounds, hgrn_norm_w, swa_sinks, rwkv_mu, rwkv_w0, rwkv_w_up, rwkv_a0, rwkv_a_up, rwkv_g_up, rwkv_k_k, rwkv_k_a, rwkv_r_k, rwkv_lnx_w, rwkv_lnx_b, rwkv_vres_down, rwkv_vres_mu, rwkv_v0, rwkv_vres_up, ln1_g, ln1_b, ln2_g, ln2_b, ffn_w_gate, ffn_w_up, ffn_w_down, moe_router, moe_w_gate, moe_w_up, moe_w_down, ple_proj, ple_gate):
    bsz, seq, d = x.shape
    depth = w_in.shape[0]
    alpha = (2.0 * depth) ** 0.25
    lbs = jnp.cumsum(jax.nn.softmax(hgrn_lower_bounds.astype(F32), axis=0), axis=0)
    lbs = lbs - lbs[0]
    outs = []
    for bi in range(bsz):
        xt = x[bi]
        v_first = None
        for i in range(depth):
            w = _group_in_weights(w_in[i], None if i == 0 else rwkv_vres_down[i - 1])
            z_gla, z_hgrn, z_swa, z_rwkv = _in_proj(xt, w)
            o_gla = _gla_mixer(z_gla, gla_gk_up[i], gla_gk_bias[i], gla_norm_w[i])
            o_hgrn = _hgrn_mixer(z_hgrn, lbs[i], hgrn_norm_w[i])
            o_swa = _swa_mixer(z_swa, swa_sinks[i])
            vres = None if i == 0 else (v_first, rwkv_vres_mu[i - 1], rwkv_v0[i - 1], rwkv_vres_up[i - 1])
            rw = _rwkv_mixer(z_rwkv, rwkv_mu[i], rwkv_w0[i], rwkv_w_up[i], rwkv_a0[i], rwkv_a_up[i],
                             rwkv_g_up[i], rwkv_k_k[i], rwkv_k_a[i], rwkv_r_k[i].reshape(-1),
                             rwkv_lnx_w[i], rwkv_lnx_b[i], vres)
            if i == 0:
                o_rwkv, v_first = rw
            else:
                o_rwkv = rw
            xt = _out_proj_ln(alpha, xt, (o_gla, o_hgrn, o_swa, o_rwkv), w_out[i], ln1_g[i], ln1_b[i])
            j = i // 2
            if i % 2 == 0:
                xt = _dense_ffn_tail(alpha, xt, p[i, bi], ffn_w_gate[j], ffn_w_up[j], ffn_w_down[j],
                                     ln2_g[i], ln2_b[i], ple_gate[i], ple_proj[i])
            else:
                xt = _moe_tail(alpha, xt, p[i, bi], moe_router[j], moe_w_gate[j], moe_w_up[j], moe_w_down[j],
                               ln2_g[i], ln2_b[i], ple_gate[i], ple_proj[i])
        outs.append(xt)
    return jnp.stack(outs, axis=0)
```

```python
import functools

import jax
import jax.numpy as jnp
from jax import lax
from jax.experimental import pallas as pl
from jax.experimental.pallas import tpu as pltpu

F32 = jnp.float32
BF16 = jnp.bfloat16
HIGHEST = lax.Precision.HIGHEST
MIX_DTYPE = BF16

D_MODEL = 1024
GROUP_WIDTH = 256
N_HEADS = 4
HEAD_DIM = 64
GLA_DK = 32
GLA_GATE_RANK = 16
GLA_GATE_NORMALIZER = 16.0
SWA_WINDOW = 128
RWKV_COLS = 3 * GROUP_WIDTH + 16 + 16 + 32
RWKV_V_RANK = 8
LN_EPS = 1e-5
RMS_EPS = 1e-6
RWKV_GN_EPS = 64e-5
D_FF = 2816
N_EXPERTS = 8
D_FF_EXPERT = 3584
PLE_DIM = 256

LANES = 128
GLA_W = 896
HGRN_W = 1024
SWA_W = 512
RWKV_W = 896
Z_W = GLA_W + HGRN_W + SWA_W + RWKV_W

GLA_SUB = 16
GLA_TILE = 128
GLA_STEP = 1024
GLA_STAGGER = 1
SWA_STAGGER = 1
RWKV_CHUNK = 64
RWKV_TILE = 512
RWKV_STAGGER = 2
ROW_TILE = 512
FF_CHUNK = 256
SWA_TILE = 256
MOE_ROWS = 512
MOE_DMA_FREE_CHUNKS = 8
GATHER_DMA_PRIORITY = 0
SCATTER_DMA_PRIORITY = 1
VMEM_LIMIT = 56 * 1024 * 1024


def _iota(shape, dim):
    return lax.broadcasted_iota(jnp.int32, shape, dim)


def _idiv(x, n):
    return jnp.right_shift(x, n.bit_length() - 1)


def _imod(x, n):
    return jnp.bitwise_and(x, n - 1)


def _dot(a, b, precision=None):
    return jnp.dot(a, b, preferred_element_type=F32, precision=precision)


def _dot_nt(a, b, precision=None):
    return lax.dot_general(a, b, (((1,), (1,)), ((), ())), preferred_element_type=F32, precision=precision)


def _dot_tn(a, b, precision=None):
    return lax.dot_general(a, b, (((0,), (0,)), ((), ())), preferred_element_type=F32, precision=precision)


def _bdot(a, b):
    return _dot(a.astype(BF16), b.astype(BF16))


def _bdot_nt(a, b):
    return _dot_nt(a.astype(BF16), b.astype(BF16))


def _dot_hilo(x, m):
    hi = x.astype(BF16)
    lo = (x - hi.astype(F32)).astype(BF16)
    return _dot(hi, m) + _dot(lo, m)


def _dot_mask(m, x):
    m = m.astype(BF16)
    x1 = x.astype(BF16)
    r1 = x - x1.astype(F32)
    x2 = r1.astype(BF16)
    x3 = (r1 - x2.astype(F32)).astype(BF16)
    return _dot(m, x1) + _dot(m, x2) + _dot(m, x3)


def _dot_3pass(x, w):
    x_hi = x.astype(BF16)
    x_lo = (x - x_hi.astype(F32)).astype(BF16)
    w_hi = w.astype(BF16)
    w_lo = (w - w_hi.astype(F32)).astype(BF16)
    return _dot(x_hi, w_hi) + _dot(x_lo, w_hi) + _dot(x_hi, w_lo)


def _sigmoid(x):
    return 1.0 / (1.0 + jnp.exp(-x))


def _silu(x):
    return x * _sigmoid(x)


def _log_sigmoid(x):
    return jnp.minimum(x, 0.0) - jnp.log1p(jnp.exp(-jnp.abs(x)))


def _layer_norm(y, g, b):
    mu = jnp.mean(y, axis=-1, keepdims=True)
    d = y - mu
    var = jnp.mean(d * d, axis=-1, keepdims=True)
    return d * lax.rsqrt(var + LN_EPS) * g + b


def _expand_heads(x, head_width):
    lane_head = _idiv(_iota(x.shape, 1), head_width)
    return jnp.concatenate([jnp.where(lane_head == h, x, 0.0) for h in range(N_HEADS)], axis=0)


def _head_group_matrix(width, head_width, value):
    same = _idiv(_iota((width, width), 0), head_width) == _idiv(_iota((width, width), 1), head_width)
    return jnp.where(same, value, 0.0).astype(F32)


def _resident(shape):
    nd = len(shape)
    return pl.BlockSpec(shape, lambda *_: (0,) * nd, pipeline_mode=pl.Buffered(1))


def _params(semantics):
    return pltpu.CompilerParams(dimension_semantics=semantics, vmem_limit_bytes=VMEM_LIMIT)


def _inproj_kernel(x_ref, w_ref, gla_ref, hgrn_ref, swa_ref, rwkv_ref):
    xb = x_ref[...].astype(BF16)
    o = 0
    for ref, width in ((gla_ref, GLA_W), (hgrn_ref, HGRN_W), (swa_ref, SWA_W), (rwkv_ref, RWKV_W)):
        ref[...] = _dot(xb, w_ref[:, o:o + width])
        o += width


def _in_proj(x, w):
    t = x.shape[0]
    widths = (GLA_W, HGRN_W, SWA_W, RWKV_W)
    return pl.pallas_call(
        _inproj_kernel,
        grid=(t // ROW_TILE,),
        in_specs=[pl.BlockSpec((ROW_TILE, D_MODEL), lambda i: (i, 0)), _resident((D_MODEL, Z_W))],
        out_specs=[pl.BlockSpec((ROW_TILE, w_), lambda i: (i, 0)) for w_ in widths],
        out_shape=[jax.ShapeDtypeStruct((t, w_), F32) for w_ in widths],
        compiler_params=_params(("parallel",)),
        name="in_proj",
    )(x, w)


def _group_in_weights(w_in, vres_down):
    gla, hgrn, swa, rwkv = jnp.split(w_in, (784, 784 + 1024, 784 + 1024 + 512), axis=1)
    if vres_down is not None:
        rwkv = jnp.concatenate([rwkv, vres_down], axis=1)
    pad = lambda a, w_: jnp.pad(a, ((0, 0), (0, w_ - a.shape[1])))
    return jnp.concatenate([pad(gla, GLA_W), hgrn, swa, pad(rwkv, RWKV_W)], axis=1).astype(BF16)


def _gated_linear_attention_tile(index, q, k, v, log_f, state_box, out_box):
    length, kw = q.shape
    head_k = kw // N_HEADS
    n_sub = length // GLA_SUB
    row = _iota((length, length), 0)
    col = _iota((length, length), 1)
    same_sub = _idiv(row, GLA_SUB) == _idiv(col, GLA_SUB)
    m_local = jnp.where(same_sub & (col <= row), 1.0, 0.0).astype(F32)
    m_prev = jnp.where(_idiv(col, GLA_SUB) < _idiv(row, GLA_SUB), 1.0, 0.0).astype(F32)
    b_local = _dot_mask(m_local, log_f)
    b_start = _dot_mask(m_prev, log_f)
    yield
    b_full = b_start + b_local
    q_local = q * jnp.exp(b_local)

    q_pos = _imod(_iota((N_HEADS * GLA_SUB, length), 0), GLA_SUB)
    s_pos = _iota((N_HEADS * GLA_SUB, length), 1)
    probs = []
    for c in range(n_sub):
        r0, r1 = c * GLA_SUB, (c + 1) * GLA_SUB
        expo = jnp.where(_iota((r1, kw), 0) < r0, b_start[r0:r0 + 1, :] - b_full[0:r1], -b_local[0:r1])
        k_ref = k[0:r1] * jnp.exp(expo)
        if r1 < length:
            k_ref = jnp.concatenate([k_ref, jnp.zeros((length - r1, kw), F32)], axis=0)
        q_heads = _expand_heads(q_local[r0:r1, :], head_k)
        s = _bdot_nt(q_heads, k_ref)
        probs.append(jnp.where(s_pos <= q_pos + r0, s, 0.0))
    b_total = b_full[length - 1:length, :]
    k_end = k * jnp.exp(b_total - b_full)
    upd = _dot_tn(v.astype(BF16), k_end.astype(BF16))
    yield
    o_heads = _bdot(jnp.concatenate(probs, axis=0), v)
    assert len(out_box) == index, "the previous tile must have replaced the state before it is read"
    state_t = state_box[0]
    o_state = _bdot_nt(q * jnp.exp(b_full), state_t)
    same_head = _idiv(_iota((GROUP_WIDTH, kw), 0), HEAD_DIM) == _idiv(_iota((GROUP_WIDTH, kw), 1), head_k)
    state_box[0] = state_t * jnp.exp(b_total) + jnp.where(same_head, upd, 0.0)
    yield
    v_head = _idiv(_iota((GLA_SUB, GROUP_WIDTH), 1), HEAD_DIM)
    rows = []
    for c in range(n_sub):
        base = c * N_HEADS * GLA_SUB
        acc = jnp.zeros((GLA_SUB, GROUP_WIDTH), F32)
        for h in range(N_HEADS):
            acc = acc + jnp.where(v_head == h, o_heads[base + h * GLA_SUB:base + (h + 1) * GLA_SUB, :], 0.0)
        rows.append(acc)
    out_box.append(jnp.concatenate(rows, axis=0) + o_state)


def _gated_linear_attention(q, k, v, log_f, state_ref):
    state_box = [state_ref[...]]
    tiles = []
    stages = []
    for n in range(q.shape[0] // GLA_TILE):
        rows = slice(n * GLA_TILE, (n + 1) * GLA_TILE)
        stages.append(_gated_linear_attention_tile(n, q[rows], k[rows], v[rows], log_f[rows], state_box, tiles))
    _run_staggered(stages, GLA_STAGGER)
    state_ref[...] = state_box[0]
    return jnp.concatenate(tiles, axis=0)


def _head_rms_gate(o, norm_w, gate):
    ms = _dot_hilo(o * o, _head_group_matrix(GROUP_WIDTH, HEAD_DIM, 1.0 / HEAD_DIM).astype(BF16))
    return o * lax.rsqrt(ms + RMS_EPS) * norm_w * _silu(gate)


def _gla_kernel(z_ref, gk_up_ref, gk_bias_ref, norm_w_ref, o_ref, state_ref):
    @pl.when(pl.program_id(0) == 0)
    def _():
        state_ref[...] = jnp.zeros_like(state_ref)

    z = z_ref[...]
    q = z[:, 0:128] * (GLA_DK ** -0.5)
    k = z[:, 128:256]
    v = z[:, 256:512]
    g = z[:, 512:768]
    gate_in = _dot_3pass(z[:, 768:896], gk_up_ref[...]) + gk_bias_ref[...]
    log_f = _log_sigmoid(gate_in) * (1.0 / GLA_GATE_NORMALIZER)
    o = _gated_linear_attention(q, k, v, log_f, state_ref)
    o_ref[...] = _head_rms_gate(o, norm_w_ref[...], g).astype(o_ref.dtype)


def _hgrn_kernel(z_ref, lb_ref, log_lb_ref, norm_w_ref, o_ref, state_ref):
    @pl.when(pl.program_id(0) == 0)
    def _():
        state_ref[...] = jnp.zeros_like(state_ref)

    z = z_ref[...]
    q = _silu(z[:, 0:256])
    f = z[:, 256:512]
    v = z[:, 512:768]
    g = z[:, 768:1024]
    lb = lb_ref[...]
    a = log_lb_ref[...]
    c = jnp.log1p(-lb) + _log_sigmoid(f)
    log_f = jnp.maximum(a, c) + jnp.log1p(jnp.exp(-jnp.abs(a - c)))
    k = (1.0 - lb) * _sigmoid(-f)
    o = _gated_linear_attention(q, k, v, log_f, state_ref)
    o_ref[...] = _head_rms_gate(o, norm_w_ref[...], g).astype(o_ref.dtype)


def _gla_mixer(z, gk_up, gk_bias, norm_w):
    t = z.shape[0]
    gk_up_pad = jnp.zeros((LANES, N_HEADS * GLA_DK), F32).at[:GLA_GATE_RANK].set(gk_up)
    return pl.pallas_call(
        _gla_kernel,
        grid=(t // GLA_STEP,),
        in_specs=[pl.BlockSpec((GLA_STEP, GLA_W), lambda i: (i, 0)),
                  _resident((LANES, N_HEADS * GLA_DK)), _resident((1, N_HEADS * GLA_DK)),
                  _resident((1, GROUP_WIDTH))],
        out_specs=pl.BlockSpec((GLA_STEP, GROUP_WIDTH), lambda i: (i, 0)),
        out_shape=jax.ShapeDtypeStruct((t, GROUP_WIDTH), MIX_DTYPE),
        scratch_shapes=[pltpu.VMEM((GROUP_WIDTH, N_HEADS * GLA_DK), F32)],
        compiler_params=_params(("arbitrary",)),
        name="gla_mixer",
    )(z, gk_up_pad, gk_bias.reshape(1, -1), jnp.tile(norm_w, N_HEADS).reshape(1, -1))


def _hgrn_mixer(z, lb, norm_w):
    t = z.shape[0]
    return pl.pallas_call(
        _hgrn_kernel,
        grid=(t // GLA_STEP,),
        in_specs=[pl.BlockSpec((GLA_STEP, HGRN_W), lambda i: (i, 0)),
                  _resident((1, GROUP_WIDTH)), _resident((1, GROUP_WIDTH)), _resident((1, GROUP_WIDTH))],
        out_specs=pl.BlockSpec((GLA_STEP, GROUP_WIDTH), lambda i: (i, 0)),
        out_shape=jax.ShapeDtypeStruct((t, GROUP_WIDTH), MIX_DTYPE),
        scratch_shapes=[pltpu.VMEM((GROUP_WIDTH, GROUP_WIDTH), F32)],
        compiler_params=_params(("arbitrary",)),
        name="hgrn_mixer",
    )(z, lb.reshape(1, -1), jnp.log(lb).reshape(1, -1), jnp.tile(norm_w, N_HEADS).reshape(1, -1))


def _swa_block(q, kw, vw, visible, sinks, out_box):
    n = q.shape[0]
    head_cols = lambda x, h: x[:, h * HEAD_DIM:(h + 1) * HEAD_DIM]
    v_lane_head = _idiv(_iota(vw.shape, 1), HEAD_DIM)
    outs = []
    for kv in range(N_HEADS // 2):
        heads = (2 * kv, 2 * kv + 1)
        scores = _bdot_nt(jnp.concatenate([head_cols(q, h) for h in heads], axis=0), head_cols(kw, kv))
        yield
        probs, sink_terms = [], []
        for half, h in enumerate(heads):
            s = jnp.where(visible, scores[half * n:(half + 1) * n], -jnp.inf)
            sink = sinks[:, h:h + 1]
            m = jnp.maximum(jnp.max(s, axis=-1, keepdims=True), sink)
            probs.append(jnp.exp(s - m))
            sink_terms.append(jnp.exp(sink - m))
        o = _bdot(jnp.concatenate(probs, axis=0), jnp.where(v_lane_head == kv, vw, 1.0))
        yield
        sums = head_cols(o, 1 - kv)[:, 0:1]
        outs += [head_cols(o, kv)[half * n:(half + 1) * n] / (sums[half * n:(half + 1) * n] + sink_terms[half])
                 for half in range(2)]
    out_box.append(jnp.concatenate(outs, axis=-1))


def _swa_kernel(q_ref, k_ref, v_ref, kp_ref, vp_ref, sink_ref, o_ref):
    w = SWA_WINDOW
    has_prev = pl.program_id(0) > 0
    q = q_ref[...] * (HEAD_DIM ** -0.5)
    k_all = jnp.concatenate([kp_ref[...], k_ref[...]], axis=0)
    v_all = jnp.concatenate([vp_ref[...], v_ref[...]], axis=0)
    q_pos = _iota((w, 2 * w), 0) + w
    k_pos = _iota((w, 2 * w), 1)
    dist = q_pos - k_pos
    in_window = (dist >= 0) & (dist < w)
    sinks = sink_ref[...]
    blocks = []
    stages = []
    for b in range(q.shape[0] // w):
        visible = in_window if b > 0 else in_window & ((k_pos >= w) | has_prev)
        stages.append(_swa_block(q[b * w:(b + 1) * w], k_all[b * w:(b + 2) * w], v_all[b * w:(b + 2) * w],
                                 visible, sinks, blocks))
    _run_staggered(stages, SWA_STAGGER)
    o_ref[...] = jnp.concatenate(blocks, axis=0).astype(o_ref.dtype)


def _swa_mixer(z, sinks):
    t = z.shape[0]
    w = SWA_WINDOW
    n = SWA_TILE
    prev = lambda col: (lambda i: (jnp.maximum(i * (n // w) - 1, 0), col))
    return pl.pallas_call(
        _swa_kernel,
        grid=(t // n,),
        in_specs=[pl.BlockSpec((n, 256), lambda i: (i, 0)),
                  pl.BlockSpec((n, 128), lambda i: (i, 2)), pl.BlockSpec((n, 128), lambda i: (i, 3)),
                  pl.BlockSpec((w, 128), prev(2)), pl.BlockSpec((w, 128), prev(3)),
                  _resident((1, N_HEADS))],
        out_specs=pl.BlockSpec((n, GROUP_WIDTH), lambda i: (i, 0)),
        out_shape=jax.ShapeDtypeStruct((t, GROUP_WIDTH), MIX_DTYPE),
        compiler_params=_params(("parallel",)),
        name="swa_mixer",
    )(z, z, z, z, z, sinks.reshape(1, -1))


def _run_staggered(stage_generators, stagger):
    live = dict(enumerate(stage_generators))
    rnd = 0
    while live:
        for n in sorted(live):
            if rnd >= n * stagger:
                try:
                    next(live[n])
                except StopIteration:
                    del live[n]
        rnd += 1


def _rwkv_chunk(index, r, k, v, a_vec, b_vec, log_w, state_box, out_box):
    c = r.shape[0]
    width = r.shape[1]
    tri = jnp.where(_iota((c, c), 1) <= _iota((c, c), 0), 1.0, 0.0).astype(F32)
    p = _dot_mask(tri, log_w)
    yield
    p_total = p[c - 1:c, :]
    decay_in = jnp.exp(p)
    decay_out = jnp.exp(-p)
    decay_end = jnp.exp(p_total - p)
    a_in = a_vec * jnp.exp(p - log_w)
    r_in = r * decay_in
    b_out = b_vec * decay_out
    k_out = k * decay_out
    b_end = b_vec * decay_end
    k_end = k * decay_end

    t_pos = _iota((c, width), 0)
    assert width == N_HEADS * c
    s_pos = _imod(_iota((c, width), 1), c)
    strict = s_pos < t_pos
    incl = s_pos <= t_pos
    expand = lambda x: _expand_heads(x, HEAD_DIM)

    scores = _bdot_nt(jnp.concatenate([a_in, r_in], axis=0),
                      jnp.concatenate([expand(b_out), expand(k_out)], axis=0))
    yield
    a_ab = jnp.where(strict, scores[0:c, 0:width], 0.0)
    a_ak = jnp.where(strict, scores[0:c, width:2 * width], 0.0)
    a_rb = jnp.where(incl, scores[c:2 * c, 0:width], 0.0)
    a_rk = jnp.where(incl, scores[c:2 * c, width:2 * width], 0.0)

    t_inv = jnp.where(s_pos == t_pos, 1.0, 0.0) + a_ab
    v_heads = expand(v)
    x1 = _bdot(a_ak, v_heads)
    power = _bdot(a_ab, expand(a_ab))
    yield
    n_factors = (c - 1).bit_length()
    for _ in range(n_factors - 2):
        power_heads = expand(power)
        t_inv = t_inv + _bdot(t_inv, power_heads)
        power = _bdot(power, power_heads)
        yield
    t_inv = t_inv + _bdot(t_inv, expand(power))
    yield
    sol = _bdot(t_inv, jnp.concatenate([expand(x1), expand(a_in)], axis=1))
    yield
    u0 = sol[:, 0:width]
    w_mat = sol[:, width:2 * width]

    assert len(out_box) == index, "the previous chunk must have replaced the state before it is read"
    state = state_box[0]
    from_state = _bdot_nt(jnp.concatenate([w_mat, r_in], axis=0), state)
    yield
    u = u0 + from_state[0:c]
    y = _bdot(a_rb, expand(u)) + _bdot(a_rk, v_heads) + from_state[c:2 * c]
    upd = _dot_tn(jnp.concatenate([u, v], axis=0).astype(BF16),
                  jnp.concatenate([b_end, k_end], axis=0).astype(BF16))
    same_head = _idiv(_iota((width, width), 0), HEAD_DIM) == _idiv(_iota((width, width), 1), HEAD_DIM)
    state_box[0] = state * jnp.exp(p_total) + jnp.where(same_head, upd, 0.0)
    out_box.append(y)


def _rwkv_kernel(has_vres, *refs):
    if has_vres:
        (z_ref, zp_ref, vfirst_ref, mu_ref, w0_ref, wup_ref, a0_ref, aup_ref, gup_ref, kk_ref, ka_ref,
         rk_ref, lnw_ref, lnb_ref, v0_ref, vup_ref, o_ref, state_ref) = refs
    else:
        (z_ref, zp_ref, mu_ref, w0_ref, wup_ref, a0_ref, aup_ref, gup_ref, kk_ref, ka_ref,
         rk_ref, lnw_ref, lnb_ref, o_ref, vout_ref, state_ref) = refs
    step = pl.program_id(0)

    @pl.when(step == 0)
    def _():
        state_ref[...] = jnp.zeros_like(state_ref)

    z = z_ref[...]
    last_prev = jnp.where(step > 0, zp_ref[7:8, :], 0.0)
    prev = jnp.where(_iota(z.shape, 0) == 0, last_prev, pltpu.roll(z, 1, axis=0))
    zr = z + (prev - z) * mu_ref[...]
    r = zr[:, 0:256]
    k = zr[:, 256:512]
    v = zr[:, 512:768]
    low = zr[:, 768:896]
    w_pre = w0_ref[...] + _dot_3pass(jnp.tanh(low), wup_ref[...])
    w_log = -(jnp.maximum(-w_pre, 0.0) + jnp.log1p(jnp.exp(-jnp.abs(w_pre)))) - 0.5
    log_w = -jnp.exp(w_log)
    a = _sigmoid(a0_ref[...] + _dot_3pass(low, aup_ref[...]))
    g = _dot_3pass(_sigmoid(low), gup_ref[...])
    if has_vres:
        v = v + (vfirst_ref[...] - v) * _sigmoid(v0_ref[...] + _dot_3pass(low, vup_ref[...]))
    else:
        vout_ref[...] = v
    head_sum = _head_group_matrix(GROUP_WIDTH, HEAD_DIM, 1.0).astype(BF16)
    kk = k * kk_ref[...]
    kk = kk / jnp.maximum(jnp.sqrt(_dot_hilo(kk * kk, head_sum)), 1e-12)
    k = k * (1.0 + (a - 1.0) * ka_ref[...])
    a_vec = -kk
    b_vec = kk * a

    c = RWKV_CHUNK
    state_box = [state_ref[...]]
    chunks = []
    stages = []
    for n in range(z.shape[0] // c):
        rows = slice(n * c, (n + 1) * c)
        stages.append(_rwkv_chunk(n, r[rows], k[rows], v[rows], a_vec[rows], b_vec[rows], log_w[rows],
                                  state_box, chunks))
    _run_staggered(stages, RWKV_STAGGER)
    state_ref[...] = state_box[0]
    y = jnp.concatenate(chunks, axis=0)

    head_mean = _head_group_matrix(GROUP_WIDTH, HEAD_DIM, 1.0 / HEAD_DIM).astype(BF16)
    mu_y = _dot_hilo(y, head_mean)
    d = y - mu_y
    var_y = _dot_hilo(d * d, head_mean)
    y = d * lax.rsqrt(var_y + RWKV_GN_EPS) * lnw_ref[...] + lnb_ref[...]
    bonus = _dot_hilo(r * k * rk_ref[...], head_sum) * v
    o_ref[...] = ((y + bonus) * g).astype(o_ref.dtype)


def _rwkv_mixer(z, mu, w0, w_up, a0, a_up, g_up, k_k, k_a, r_k, lnx_w, lnx_b, vres):
    t = z.shape[0]
    c = RWKV_TILE
    row = lambda a: a.reshape(1, -1)
    low_rows = lambda a, start: jnp.zeros((LANES, GROUP_WIDTH), F32).at[start:start + a.shape[0]].set(a)
    has_vres = vres is not None
    mu_full = jnp.zeros((RWKV_W,), F32).at[:RWKV_COLS].set(mu)
    vec = _resident((1, GROUP_WIDTH))
    mat = _resident((LANES, GROUP_WIDTH))
    tile = pl.BlockSpec((c, GROUP_WIDTH), lambda i: (i, 0))
    z_specs = [pl.BlockSpec((c, RWKV_W), lambda i: (i, 0)),
               pl.BlockSpec((8, RWKV_W), lambda i: (jnp.maximum(i * (c // 8) - 1, 0), 0))]
    common = [row(w0), low_rows(w_up, 0), row(a0), low_rows(a_up, 16), low_rows(g_up, 32),
              row(k_k), row(k_a), row(r_k), row(lnx_w), row(lnx_b)]
    common_specs = [vec, mat, vec, mat, mat, vec, vec, vec, vec, vec]
    if has_vres:
        v_first, vres_mu, v0, v_up = vres
        mu_full = mu_full.at[RWKV_COLS:RWKV_COLS + RWKV_V_RANK].set(vres_mu)
        args = [z, z, v_first, row(mu_full)] + common + [row(v0), low_rows(v_up, 64)]
        in_specs = z_specs + [tile, _resident((1, RWKV_W))] + common_specs + [vec, mat]
        out_specs = tile
        out_shape = jax.ShapeDtypeStruct((t, GROUP_WIDTH), MIX_DTYPE)
    else:
        args = [z, z, row(mu_full)] + common
        in_specs = z_specs + [_resident((1, RWKV_W))] + common_specs
        out_specs = [tile, tile]
        out_shape = [jax.ShapeDtypeStruct((t, GROUP_WIDTH), MIX_DTYPE), jax.ShapeDtypeStruct((t, GROUP_WIDTH), F32)]
    return pl.pallas_call(
        functools.partial(_rwkv_kernel, has_vres),
        grid=(t // c,),
        in_specs=in_specs,
        out_specs=out_specs,
        out_shape=out_shape,
        scratch_shapes=[pltpu.VMEM((GROUP_WIDTH, GROUP_WIDTH), F32)],
        compiler_params=_params(("arbitrary",)),
        name="rwkv_mixer",
    )(*args)


def _top2_route(logits):
    lane = _iota(logits.shape, 1).astype(F32)
    logits = jnp.where(lane < N_EXPERTS, logits, -jnp.inf)
    m1 = jnp.max(logits, axis=-1, keepdims=True)
    i1 = jnp.min(jnp.where(logits == m1, lane, LANES), axis=-1, keepdims=True)
    rest = jnp.where(lane == i1, -jnp.inf, logits)
    m2 = jnp.max(rest, axis=-1, keepdims=True)
    i2 = jnp.min(jnp.where(rest == m2, lane, LANES), axis=-1, keepdims=True)
    e2 = jnp.exp(m2 - m1)
    g1 = 1.0 / (1.0 + e2)
    g2 = e2 * g1
    return jnp.where(lane == 0, i1, jnp.where(lane == 1, i2, jnp.where(lane == 2, g1, jnp.where(lane == 3, g2, 0.0))))


def _outproj_kernel(alpha, with_router, x_ref, o0_ref, o1_ref, o2_ref, o3_ref, w_ref, g_ref, b_ref, *rest):
    acc = alpha * x_ref[...]
    for h, ref in enumerate((o0_ref, o1_ref, o2_ref, o3_ref)):
        acc = acc + _dot(ref[...], w_ref[h * GROUP_WIDTH:(h + 1) * GROUP_WIDTH, :])
    y = _layer_norm(acc, g_ref[...], b_ref[...])
    if with_router:
        router_ref, y_ref, route_ref = rest
        route_ref[...] = _top2_route(_dot_3pass(y, router_ref[...]))
    else:
        (y_ref,) = rest
    y_ref[...] = y


def _out_proj_ln(alpha, x, mixes, w_out, g, b, router=None):
    t = x.shape[0]
    row_d = pl.BlockSpec((ROW_TILE, D_MODEL), lambda i: (i, 0))
    row_g = pl.BlockSpec((ROW_TILE, GROUP_WIDTH), lambda i: (i, 0))
    in_specs = [row_d, row_g, row_g, row_g, row_g, _resident((D_MODEL, D_MODEL)),
                _resident((1, D_MODEL)), _resident((1, D_MODEL))]
    args = [x, *mixes, w_out.astype(BF16), g.reshape(1, -1), b.reshape(1, -1)]
    out_specs = row_d
    out_shape = jax.ShapeDtypeStruct((t, D_MODEL), F32)
    if router is not None:
        in_specs.append(_resident((D_MODEL, LANES)))
        args.append(jnp.zeros((D_MODEL, LANES), F32).at[:, :N_EXPERTS].set(router))
        out_specs = [row_d, pl.BlockSpec((ROW_TILE, LANES), lambda i: (i, 0))]
        out_shape = [out_shape, jax.ShapeDtypeStruct((t, LANES), F32)]
    return pl.pallas_call(
        functools.partial(_outproj_kernel, alpha, router is not None),
        grid=(t // ROW_TILE,),
        in_specs=in_specs,
        out_specs=out_specs,
        out_shape=out_shape,
        compiler_params=_params(("parallel",)),
        name="out_proj_ln",
    )(*args)


def _ln_embed(y, ln_g, ln_b, p, ple_gate, ple_proj):
    x = _layer_norm(y, ln_g, ln_b)
    gate = _sigmoid(_dot(x.astype(BF16), ple_gate))
    return x + gate * _dot(p.astype(BF16), ple_proj)


def _dense_ffn_kernel(alpha, x_ref, p_ref, wg_ref, wu_ref, wd_ref, g_ref, b_ref, pg_ref, pp_ref, y_ref, acc_ref):
    x = x_ref[...]
    xb = x.astype(BF16)
    acc_ref[...] = alpha * x
    for j in range(D_FF // FF_CHUNK):
        cols = slice(j * FF_CHUNK, (j + 1) * FF_CHUNK)
        h = _silu(_dot(xb, wg_ref[:, cols])) * _dot(xb, wu_ref[:, cols])
        acc_ref[...] += _dot(h.astype(BF16), wd_ref[cols, :])
    y_ref[...] = _ln_embed(acc_ref[...], g_ref[...], b_ref[...], p_ref[...], pg_ref[...], pp_ref[...])


def _dense_ffn_tail(alpha, x, p, w_gate, w_up, w_down, g, b, ple_gate, ple_proj):
    t = x.shape[0]
    row_d = pl.BlockSpec((ROW_TILE, D_MODEL), lambda i: (i, 0))
    return pl.pallas_call(
        functools.partial(_dense_ffn_kernel, alpha),
        grid=(t // ROW_TILE,),
        in_specs=[row_d, pl.BlockSpec((ROW_TILE, PLE_DIM), lambda i: (i, 0)),
                  _resident((D_MODEL, D_FF)), _resident((D_MODEL, D_FF)), _resident((D_FF, D_MODEL)),
                  _resident((1, D_MODEL)), _resident((1, D_MODEL)),
                  _resident((D_MODEL, D_MODEL)), _resident((PLE_DIM, D_MODEL))],
        out_specs=row_d,
        out_shape=jax.ShapeDtypeStruct((t, D_MODEL), F32),
        scratch_shapes=[pltpu.VMEM((ROW_TILE, D_MODEL), F32)],
        compiler_params=_params(("parallel",)),
        name="dense_ffn_tail",
    )(x, p, w_gate.astype(BF16), w_up.astype(BF16), w_down.astype(BF16), g.reshape(1, -1), b.reshape(1, -1),
      ple_gate.astype(BF16), ple_proj.astype(BF16))


def _expert_kernel(row_tok_ref, row_dst_ref, block_e_ref, n_used_ref, x_hbm, wg_ref, wu_ref, wd_ref, y_hbm,
                   rows_ref, xb_ref, ybuf_ref, gather_sem, scatter_sem):
    i = pl.program_id(0)
    n_used = n_used_ref[0]
    last = pl.num_programs(0) - 1
    n = MOE_ROWS
    slot = lax.rem(i, 2)
    other = 1 - slot
    gather_row = lambda tok, s, r: pltpu.make_async_copy(
        x_hbm.at[pl.ds(tok, 1)], rows_ref.at[s, pl.ds(r, 1)], gather_sem.at[s])
    scatter_row = lambda dst, s, r: pltpu.make_async_copy(
        ybuf_ref.at[s, pl.ds(r, 1)], y_hbm.at[pl.ds(dst, 1)], scatter_sem.at[s])
    gather_block = lambda s: pltpu.make_async_copy(x_hbm.at[pl.ds(0, n)], rows_ref.at[s], gather_sem.at[s])
    scatter_block = lambda s: pltpu.make_async_copy(ybuf_ref.at[s], y_hbm.at[pl.ds(0, n)], scatter_sem.at[s])

    @pl.when(i == 0)
    def _():
        ybuf_ref[1] = jnp.zeros((n, D_MODEL), F32)

        def start(r, carry):
            gather_row(row_tok_ref[r], 0, r).start()
            return carry

        lax.fori_loop(0, n, start, 0)

    def scatter_all(first_dst, s):
        def start(r, carry):
            scatter_row(row_dst_ref[first_dst + r], s, r).start()
            return carry

        lax.fori_loop(0, n, start, 0)
        scatter_block(s).wait()

    @pl.when(i < n_used)
    def _():
        gather_block(slot).wait()
        xb_ref[...] = rows_ref[slot].astype(BF16)
        n_chunks = D_FF_EXPERT // FF_CHUNK
        rows_per_chunk = -(-n // (n_chunks - MOE_DMA_FREE_CHUNKS))
        for j in range(n_chunks):
            cols = slice(j * FF_CHUNK, (j + 1) * FF_CHUNK)
            xb = xb_ref[...]
            h = _silu(_dot(xb, wg_ref[0, :, cols])) * _dot(xb, wu_ref[0, :, cols])
            part = _dot(h.astype(BF16), wd_ref[0, cols, :])
            if j == 0:
                ybuf_ref[slot] = part
            else:
                ybuf_ref[slot] += part
            for r in range(j * rows_per_chunk, min((j + 1) * rows_per_chunk, n)):
                gather_row(row_tok_ref[(i + 1) * n + r], other, r).start(priority=GATHER_DMA_PRIORITY)
                scatter_row(row_dst_ref[i * n + r], other, r).start(priority=SCATTER_DMA_PRIORITY)
        scatter_block(other).wait()

    @pl.when(i == n_used)
    def _():
        gather_block(slot).wait()
        scatter_all(i * n, other)

    @pl.when(i >= n_used)
    def _():
        rows_ref[slot] = jnp.zeros((n, D_MODEL), F32)
        fill = pltpu.make_async_copy(rows_ref.at[slot], y_hbm.at[pl.ds(i * n, n)], scatter_sem.at[slot])
        fill.start()
        fill.wait()

    @pl.when((i == last) & (i < n_used))
    def _():
        gather_block(other).wait()
        scatter_all((i + 1) * n, slot)


def _expert_rows(x, row_tok, row_dst, block_e, n_used, n_out_rows, w_gate, w_up, w_down):
    n_blocks = block_e.shape[0]
    whole = lambda shape: pl.BlockSpec((1,) + shape, lambda i, tok, dst, be, used: (be[i], 0, 0),
                                       pipeline_mode=pl.Buffered(1))
    grid_spec = pltpu.PrefetchScalarGridSpec(
        num_scalar_prefetch=4,
        grid=(n_blocks,),
        in_specs=[pl.BlockSpec(memory_space=pl.ANY),
                  whole((D_MODEL, D_FF_EXPERT)), whole((D_MODEL, D_FF_EXPERT)), whole((D_FF_EXPERT, D_MODEL))],
        out_specs=pl.BlockSpec(memory_space=pl.ANY),
        scratch_shapes=[pltpu.VMEM((2, MOE_ROWS, D_MODEL), F32), pltpu.VMEM((MOE_ROWS, D_MODEL), BF16),
                        pltpu.VMEM((2, MOE_ROWS, D_MODEL), F32),
                        pltpu.SemaphoreType.DMA((2,)), pltpu.SemaphoreType.DMA((2,))],
    )
    return pl.pallas_call(
        _expert_kernel,
        grid_spec=grid_spec,
        out_shape=jax.ShapeDtypeStruct((n_out_rows, D_MODEL), F32),
        compiler_params=_params(("arbitrary",)),
        name="moe_experts",
    )(row_tok, row_dst, block_e, n_used.reshape(1), x, w_gate.astype(BF16), w_up.astype(BF16),
      w_down.astype(BF16))


def _combine_kernel(alpha, x_ref, y0_ref, y1_ref, gates_ref, p_ref, g_ref, b_ref, pg_ref, pp_ref, o_ref):
    gates = gates_ref[...]
    f = y0_ref[...] * gates[:, 0:1] + y1_ref[...] * gates[:, 1:2]
    o_ref[...] = _ln_embed(alpha * x_ref[...] + f, g_ref[...], b_ref[...], p_ref[...], pg_ref[...], pp_ref[...])


def _moe_combine_tail(alpha, x, p, y, gates, g, b, ple_gate, ple_proj):
    t = x.shape[0]
    n = ROW_TILE
    row_d = pl.BlockSpec((n, D_MODEL), lambda i: (i, 0))
    return pl.pallas_call(
        functools.partial(_combine_kernel, alpha),
        grid=(t // n,),
        in_specs=[row_d, row_d, pl.BlockSpec((n, D_MODEL), lambda i: (t // n + i, 0)),
                  pl.BlockSpec((n, 2), lambda i: (i, 0)), pl.BlockSpec((n, PLE_DIM), lambda i: (i, 0)),
                  _resident((1, D_MODEL)), _resident((1, D_MODEL)),
                  _resident((D_MODEL, D_MODEL)), _resident((PLE_DIM, D_MODEL))],
        out_specs=row_d,
        out_shape=jax.ShapeDtypeStruct((t, D_MODEL), F32),
        compiler_params=_params(("parallel",)),
        name="moe_combine_tail",
    )(x, y, y, gates, p, g.reshape(1, -1), b.reshape(1, -1), ple_gate.astype(BF16), ple_proj.astype(BF16))


def _moe_tail(alpha, x, route, p, w_gate, w_up, w_down, g, b, ple_gate, ple_proj):
    t = x.shape[0]
    experts = route[:, 0:2].astype(jnp.int32)
    gates = route[:, 2:4]
    e_flat = experts.reshape(-1)
    onehot = (e_flat[:, None] == jnp.arange(N_EXPERTS, dtype=jnp.int32)[None, :]).astype(jnp.int32)
    running = jnp.cumsum(onehot, axis=0)
    rank = jnp.sum(onehot * (running - 1), axis=1)
    counts = running[-1]
    padded = (counts + MOE_ROWS - 1) // MOE_ROWS * MOE_ROWS
    pad_end = jnp.cumsum(padded)
    pad_start = pad_end - padded
    dest = (pad_start[e_flat] + rank).astype(jnp.int32)
    n_blocks = (2 * t) // MOE_ROWS + N_EXPERTS
    n_rows = n_blocks * MOE_ROWS
    assign = jnp.arange(2 * t, dtype=jnp.int32)
    row_assign = jnp.full((n_rows,), -1, jnp.int32).at[dest].set(assign)
    used = row_assign >= 0
    spare = 2 * t + jnp.cumsum(jnp.where(used, 0, 1).astype(jnp.int32)) - 1
    row_tok = jnp.where(used, row_assign // 2, 0)
    row_dst = jnp.where(used, (row_assign % 2) * t + row_assign // 2, spare)
    n_spare = n_rows - 2 * t
    first_dst = 2 * t + n_spare + jnp.arange(MOE_ROWS, dtype=jnp.int32)
    row_tok = jnp.concatenate([row_tok, jnp.zeros((MOE_ROWS,), jnp.int32)])
    row_dst = jnp.concatenate([first_dst, row_dst])
    block_start = jnp.arange(n_blocks, dtype=jnp.int32) * MOE_ROWS
    block_e = jnp.minimum(jnp.sum((block_start[:, None] >= pad_end[None, :]).astype(jnp.int32), axis=1),
                          N_EXPERTS - 1)
    n_used = (pad_end[-1] // MOE_ROWS).astype(jnp.int32)
    y = _expert_rows(x, row_tok, row_dst, block_e, n_used, 2 * t + n_spare + MOE_ROWS, w_gate, w_up, w_down)
    return _moe_combine_tail(alpha, x, p, y, gates, g, b, ple_gate, ple_proj)


def kernel(x, p, w_in, w_out, gla_gk_up, gla_gk_bias, gla_norm_w, hgrn_lower_bounds, hgrn_norm_w, swa_sinks, rwkv_mu, rwkv_w0, rwkv_w_up, rwkv_a0, rwkv_a_up, rwkv_g_up, rwkv_k_k, rwkv_k_a, rwkv_r_k, rwkv_lnx_w, rwkv_lnx_b, rwkv_vres_down, rwkv_vres_mu, rwkv_v0, rwkv_vres_up, ln1_g, ln1_b, ln2_g, ln2_b, ffn_w_gate, ffn_w_up, ffn_w_down, moe_router, moe_w_gate, moe_w_up, moe_w_down, ple_proj, ple_gate):
    bsz, seq, d = x.shape
    depth = w_in.shape[0]
    alpha = (2.0 * depth) ** 0.25
    lbs = jnp.cumsum(jax.nn.softmax(hgrn_lower_bounds.astype(F32), axis=0), axis=0)
    lbs = lbs - lbs[0]
    outs = []
    for bi in range(bsz):
        xt = x[bi]
        v_first = None
        for i in range(depth):
            w = _group_in_weights(w_in[i], None if i == 0 else rwkv_vres_down[i - 1])
            z_gla, z_hgrn, z_swa, z_rwkv = _in_proj(xt, w)
            o_gla = _gla_mixer(z_gla, gla_gk_up[i], gla_gk_bias[i], gla_norm_w[i])
            o_hgrn = _hgrn_mixer(z_hgrn, lbs[i], hgrn_norm_w[i])
            o_swa = _swa_mixer(z_swa, swa_sinks[i])
            vres = None if i == 0 else (v_first, rwkv_vres_mu[i - 1], rwkv_v0[i - 1], rwkv_vres_up[i - 1])
            rw = _rwkv_mixer(z_rwkv, rwkv_mu[i], rwkv_w0[i], rwkv_w_up[i], rwkv_a0[i], rwkv_a_up[i],
                             rwkv_g_up[i], rwkv_k_k[i], rwkv_k_a[i], rwkv_r_k[i].reshape(-1),
                             rwkv_lnx_w[i], rwkv_lnx_b[i], vres)
            if i == 0:
                o_rwkv, v_first = rw
            else:
                o_rwkv = rw
            mixes = (o_gla, o_hgrn, o_swa, o_rwkv)
            j = i // 2
            if i % 2 == 0:
                xt = _out_proj_ln(alpha, xt, mixes, w_out[i], ln1_g[i], ln1_b[i])
                xt = _dense_ffn_tail(alpha, xt, p[i, bi], ffn_w_gate[j], ffn_w_up[j], ffn_w_down[j],
                                     ln2_g[i], ln2_b[i], ple_gate[i], ple_proj[i])
            else:
                xt, route = _out_proj_ln(alpha, xt, mixes, w_out[i], ln1_g[i], ln1_b[i], moe_router[j])
                xt = _moe_tail(alpha, xt, route, p[i, bi], moe_w_gate[j], moe_w_up[j], moe_w_down[j],
                               ln2_g[i], ln2_b[i], ple_gate[i], ple_proj[i])
        outs.append(xt)
    return jnp.stack(outs, axis=0)
```

```python
import functools

import jax
import jax.numpy as jnp
from jax import lax
from jax.experimental import pallas as pl
from jax.experimental.pallas import tpu as pltpu

F32 = jnp.float32
BF16 = jnp.bfloat16
HIGHEST = lax.Precision.HIGHEST
MIX_DTYPE = BF16

D_MODEL = 1024
GROUP_WIDTH = 256
N_HEADS = 4
HEAD_DIM = 64
GLA_DK = 32
GLA_GATE_RANK = 16
GLA_GATE_NORMALIZER = 16.0
SWA_WINDOW = 128
RWKV_COLS = 3 * GROUP_WIDTH + 16 + 16 + 32
RWKV_V_RANK = 8
LN_EPS = 1e-5
RMS_EPS = 1e-6
RWKV_GN_EPS = 64e-5
D_FF = 2816
N_EXPERTS = 8
D_FF_EXPERT = 3584
PLE_DIM = 256

LANES = 128
GLA_W = 896
HGRN_W = 1024
SWA_W = 512
RWKV_W = 896
Z_W = GLA_W + HGRN_W + SWA_W + RWKV_W

GLA_SUB = 16
GLA_TILE = 128
GLA_STEP = 1024
GLA_STAGGER = 1
SWA_STAGGER = 1
RWKV_CHUNK = 64
RWKV_TILE = 1024
RWKV_STAGGER = 2
ROW_TILE = 512
FF_CHUNK = 256
SWA_TILE = 256
MOE_ROWS = 512
MOE_WEIGHT_CHUNK = 512
MOE_DMA_FREE_CHUNKS = 10
GATHER_DMA_PRIORITY = 0
SCATTER_DMA_PRIORITY = 1
VMEM_LIMIT = 56 * 1024 * 1024


def _iota(shape, dim):
    return lax.broadcasted_iota(jnp.int32, shape, dim)


def _idiv(x, n):
    return jnp.right_shift(x, n.bit_length() - 1)


def _imod(x, n):
    return jnp.bitwise_and(x, n - 1)


def _dot(a, b, precision=None):
    return jnp.dot(a, b, preferred_element_type=F32, precision=precision)


def _dot_nt(a, b, precision=None):
    return lax.dot_general(a, b, (((1,), (1,)), ((), ())), preferred_element_type=F32, precision=precision)


def _dot_tn(a, b, precision=None):
    return lax.dot_general(a, b, (((0,), (0,)), ((), ())), preferred_element_type=F32, precision=precision)


def _bdot(a, b):
    return _dot(a.astype(BF16), b.astype(BF16))


def _bdot_nt(a, b):
    return _dot_nt(a.astype(BF16), b.astype(BF16))


def _dot_hilo(x, m):
    hi = x.astype(BF16)
    lo = (x - hi.astype(F32)).astype(BF16)
    return _dot(hi, m) + _dot(lo, m)


def _dot_mask(m, x):
    m = m.astype(BF16)
    x1 = x.astype(BF16)
    r1 = x - x1.astype(F32)
    x2 = r1.astype(BF16)
    x3 = (r1 - x2.astype(F32)).astype(BF16)
    return _dot(m, x1) + _dot(m, x2) + _dot(m, x3)


def _dot_3pass(x, w):
    x_hi = x.astype(BF16)
    x_lo = (x - x_hi.astype(F32)).astype(BF16)
    w_hi = w.astype(BF16)
    w_lo = (w - w_hi.astype(F32)).astype(BF16)
    return _dot(x_hi, w_hi) + _dot(x_lo, w_hi) + _dot(x_hi, w_lo)


def _sigmoid(x):
    return 1.0 / (1.0 + jnp.exp(-x))


def _silu(x):
    return x * _sigmoid(x)


def _log_sigmoid(x):
    return jnp.minimum(x, 0.0) - jnp.log1p(jnp.exp(-jnp.abs(x)))


def _layer_norm(y, g, b):
    mu = jnp.mean(y, axis=-1, keepdims=True)
    d = y - mu
    var = jnp.mean(d * d, axis=-1, keepdims=True)
    return d * lax.rsqrt(var + LN_EPS) * g + b


def _expand_heads(x, head_width):
    lane_head = _idiv(_iota(x.shape, 1), head_width)
    return jnp.concatenate([jnp.where(lane_head == h, x, 0.0) for h in range(N_HEADS)], axis=0)


def _head_group_matrix(width, head_width, value):
    same = _idiv(_iota((width, width), 0), head_width) == _idiv(_iota((width, width), 1), head_width)
    return jnp.where(same, value, 0.0).astype(F32)


def _resident(shape):
    nd = len(shape)
    return pl.BlockSpec(shape, lambda *_: (0,) * nd, pipeline_mode=pl.Buffered(1))


def _params(semantics):
    return pltpu.CompilerParams(dimension_semantics=semantics, vmem_limit_bytes=VMEM_LIMIT)


def _inproj_kernel(x_ref, w_ref, gla_ref, hgrn_ref, swa_ref, rwkv_ref):
    xb = x_ref[...].astype(BF16)
    o = 0
    for ref, width in ((gla_ref, GLA_W), (hgrn_ref, HGRN_W), (swa_ref, SWA_W), (rwkv_ref, RWKV_W)):
        ref[...] = _dot(xb, w_ref[:, o:o + width])
        o += width


def _in_proj(x, w):
    t = x.shape[0]
    widths = (GLA_W, HGRN_W, SWA_W, RWKV_W)
    return pl.pallas_call(
        _inproj_kernel,
        grid=(t // ROW_TILE,),
        in_specs=[pl.BlockSpec((ROW_TILE, D_MODEL), lambda i: (i, 0)), _resident((D_MODEL, Z_W))],
        out_specs=[pl.BlockSpec((ROW_TILE, w_), lambda i: (i, 0)) for w_ in widths],
        out_shape=[jax.ShapeDtypeStruct((t, w_), F32) for w_ in widths],
        compiler_params=_params(("parallel",)),
        name="in_proj",
    )(x, w)


def _group_in_weights(w_in, vres_down):
    gla, hgrn, swa, rwkv = jnp.split(w_in, (784, 784 + 1024, 784 + 1024 + 512), axis=1)
    if vres_down is not None:
        rwkv = jnp.concatenate([rwkv, vres_down], axis=1)
    pad = lambda a, w_: jnp.pad(a, ((0, 0), (0, w_ - a.shape[1])))
    return jnp.concatenate([pad(gla, GLA_W), hgrn, swa, pad(rwkv, RWKV_W)], axis=1).astype(BF16)


def _gated_linear_attention_tile(index, q, k, v, log_f, state_box, out_box):
    length, kw = q.shape
    head_k = kw // N_HEADS
    n_sub = length // GLA_SUB
    row = _iota((length, length), 0)
    col = _iota((length, length), 1)
    same_sub = _idiv(row, GLA_SUB) == _idiv(col, GLA_SUB)
    m_local = jnp.where(same_sub & (col <= row), 1.0, 0.0).astype(F32)
    m_prev = jnp.where(_idiv(col, GLA_SUB) < _idiv(row, GLA_SUB), 1.0, 0.0).astype(F32)
    b_local = _dot_mask(m_local, log_f)
    b_start = _dot_mask(m_prev, log_f)
    yield
    b_full = b_start + b_local
    q_local = q * jnp.exp(b_local)

    q_pos = _imod(_iota((N_HEADS * GLA_SUB, length), 0), GLA_SUB)
    s_pos = _iota((N_HEADS * GLA_SUB, length), 1)
    probs = []
    for c in range(n_sub):
        r0, r1 = c * GLA_SUB, (c + 1) * GLA_SUB
        expo = jnp.where(_iota((r1, kw), 0) < r0, b_start[r0:r0 + 1, :] - b_full[0:r1], -b_local[0:r1])
        k_ref = k[0:r1] * jnp.exp(expo)
        if r1 < length:
            k_ref = jnp.concatenate([k_ref, jnp.zeros((length - r1, kw), F32)], axis=0)
        q_heads = _expand_heads(q_local[r0:r1, :], head_k)
        s = _bdot_nt(q_heads, k_ref)
        probs.append(jnp.where(s_pos <= q_pos + r0, s, 0.0))
    b_total = b_full[length - 1:length, :]
    k_end = k * jnp.exp(b_total - b_full)
    upd = _dot_tn(v.astype(BF16), k_end.astype(BF16))
    yield
    o_heads = _bdot(jnp.concatenate(probs, axis=0), v)
    assert len(out_box) == index, "the previous tile must have replaced the state before it is read"
    state_t = state_box[0]
    o_state = _bdot_nt(q * jnp.exp(b_full), state_t)
    same_head = _idiv(_iota((GROUP_WIDTH, kw), 0), HEAD_DIM) == _idiv(_iota((GROUP_WIDTH, kw), 1), head_k)
    state_box[0] = state_t * jnp.exp(b_total) + jnp.where(same_head, upd, 0.0)
    yield
    v_head = _idiv(_iota((GLA_SUB, GROUP_WIDTH), 1), HEAD_DIM)
    rows = []
    for c in range(n_sub):
        base = c * N_HEADS * GLA_SUB
        acc = jnp.zeros((GLA_SUB, GROUP_WIDTH), F32)
        for h in range(N_HEADS):
            acc = acc + jnp.where(v_head == h, o_heads[base + h * GLA_SUB:base + (h + 1) * GLA_SUB, :], 0.0)
        rows.append(acc)
    out_box.append(jnp.concatenate(rows, axis=0) + o_state)


def _gated_linear_attention(q, k, v, log_f, state_ref):
    state_box = [state_ref[...]]
    tiles = []
    stages = []
    for n in range(q.shape[0] // GLA_TILE):
        rows = slice(n * GLA_TILE, (n + 1) * GLA_TILE)
        stages.append(_gated_linear_attention_tile(n, q[rows], k[rows], v[rows], log_f[rows], state_box, tiles))
    _run_staggered(stages, GLA_STAGGER)
    state_ref[...] = state_box[0]
    return jnp.concatenate(tiles, axis=0)


def _head_rms_gate(o, norm_w, gate):
    ms = _dot_hilo(o * o, _head_group_matrix(GROUP_WIDTH, HEAD_DIM, 1.0 / HEAD_DIM).astype(BF16))
    return o * lax.rsqrt(ms + RMS_EPS) * norm_w * _silu(gate)


def _gla_kernel(z_ref, gk_up_ref, gk_bias_ref, norm_w_ref, o_ref, state_ref):
    @pl.when(pl.program_id(0) == 0)
    def _():
        state_ref[...] = jnp.zeros_like(state_ref)

    z = z_ref[...]
    q = z[:, 0:128] * (GLA_DK ** -0.5)
    k = z[:, 128:256]
    v = z[:, 256:512]
    g = z[:, 512:768]
    gate_in = _dot_3pass(z[:, 768:896], gk_up_ref[...]) + gk_bias_ref[...]
    log_f = _log_sigmoid(gate_in) * (1.0 / GLA_GATE_NORMALIZER)
    o = _gated_linear_attention(q, k, v, log_f, state_ref)
    o_ref[...] = _head_rms_gate(o, norm_w_ref[...], g).astype(o_ref.dtype)


def _hgrn_kernel(z_ref, lb_ref, log_lb_ref, norm_w_ref, o_ref, state_ref):
    @pl.when(pl.program_id(0) == 0)
    def _():
        state_ref[...] = jnp.zeros_like(state_ref)

    z = z_ref[...]
    q = _silu(z[:, 0:256])
    f = z[:, 256:512]
    v = z[:, 512:768]
    g = z[:, 768:1024]
    lb = lb_ref[...]
    a = log_lb_ref[...]
    c = jnp.log1p(-lb) + _log_sigmoid(f)
    log_f = jnp.maximum(a, c) + jnp.log1p(jnp.exp(-jnp.abs(a - c)))
    k = (1.0 - lb) * _sigmoid(-f)
    o = _gated_linear_attention(q, k, v, log_f, state_ref)
    o_ref[...] = _head_rms_gate(o, norm_w_ref[...], g).astype(o_ref.dtype)


def _gla_mixer(z, gk_up, gk_bias, norm_w):
    t = z.shape[0]
    gk_up_pad = jnp.zeros((LANES, N_HEADS * GLA_DK), F32).at[:GLA_GATE_RANK].set(gk_up)
    return pl.pallas_call(
        _gla_kernel,
        grid=(t // GLA_STEP,),
        in_specs=[pl.BlockSpec((GLA_STEP, GLA_W), lambda i: (i, 0)),
                  _resident((LANES, N_HEADS * GLA_DK)), _resident((1, N_HEADS * GLA_DK)),
                  _resident((1, GROUP_WIDTH))],
        out_specs=pl.BlockSpec((GLA_STEP, GROUP_WIDTH), lambda i: (i, 0)),
        out_shape=jax.ShapeDtypeStruct((t, GROUP_WIDTH), MIX_DTYPE),
        scratch_shapes=[pltpu.VMEM((GROUP_WIDTH, N_HEADS * GLA_DK), F32)],
        compiler_params=_params(("arbitrary",)),
        name="gla_mixer",
    )(z, gk_up_pad, gk_bias.reshape(1, -1), jnp.tile(norm_w, N_HEADS).reshape(1, -1))


def _hgrn_mixer(z, lb, norm_w):
    t = z.shape[0]
    return pl.pallas_call(
        _hgrn_kernel,
        grid=(t // GLA_STEP,),
        in_specs=[pl.BlockSpec((GLA_STEP, HGRN_W), lambda i: (i, 0)),
                  _resident((1, GROUP_WIDTH)), _resident((1, GROUP_WIDTH)), _resident((1, GROUP_WIDTH))],
        out_specs=pl.BlockSpec((GLA_STEP, GROUP_WIDTH), lambda i: (i, 0)),
        out_shape=jax.ShapeDtypeStruct((t, GROUP_WIDTH), MIX_DTYPE),
        scratch_shapes=[pltpu.VMEM((GROUP_WIDTH, GROUP_WIDTH), F32)],
        compiler_params=_params(("arbitrary",)),
        name="hgrn_mixer",
    )(z, lb.reshape(1, -1), jnp.log(lb).reshape(1, -1), jnp.tile(norm_w, N_HEADS).reshape(1, -1))


def _swa_block(q, kw, vw, visible, sinks, out_box):
    n = q.shape[0]
    head_cols = lambda x, h: x[:, h * HEAD_DIM:(h + 1) * HEAD_DIM]
    v_lane_head = _idiv(_iota(vw.shape, 1), HEAD_DIM)
    outs = []
    for kv in range(N_HEADS // 2):
        heads = (2 * kv, 2 * kv + 1)
        scores = _bdot_nt(jnp.concatenate([head_cols(q, h) for h in heads], axis=0), head_cols(kw, kv))
        yield
        probs, sink_terms = [], []
        for half, h in enumerate(heads):
            s = jnp.where(visible, scores[half * n:(half + 1) * n], -jnp.inf)
            sink = sinks[:, h:h + 1]
            m = jnp.maximum(jnp.max(s, axis=-1, keepdims=True), sink)
            probs.append(jnp.exp(s - m))
            sink_terms.append(jnp.exp(sink - m))
        o = _bdot(jnp.concatenate(probs, axis=0), jnp.where(v_lane_head == kv, vw, 1.0))
        yield
        sums = head_cols(o, 1 - kv)[:, 0:1]
        outs += [head_cols(o, kv)[half * n:(half + 1) * n] / (sums[half * n:(half + 1) * n] + sink_terms[half])
                 for half in range(2)]
    out_box.append(jnp.concatenate(outs, axis=-1))


def _swa_kernel(q_ref, k_ref, v_ref, kp_ref, vp_ref, sink_ref, o_ref):
    w = SWA_WINDOW
    has_prev = pl.program_id(0) > 0
    q = q_ref[...] * (HEAD_DIM ** -0.5)
    k_all = jnp.concatenate([kp_ref[...], k_ref[...]], axis=0)
    v_all = jnp.concatenate([vp_ref[...], v_ref[...]], axis=0)
    q_pos = _iota((w, 2 * w), 0) + w
    k_pos = _iota((w, 2 * w), 1)
    dist = q_pos - k_pos
    in_window = (dist >= 0) & (dist < w)
    sinks = sink_ref[...]
    blocks = []
    stages = []
    for b in range(q.shape[0] // w):
        visible = in_window if b > 0 else in_window & ((k_pos >= w) | has_prev)
        stages.append(_swa_block(q[b * w:(b + 1) * w], k_all[b * w:(b + 2) * w], v_all[b * w:(b + 2) * w],
                                 visible, sinks, blocks))
    _run_staggered(stages, SWA_STAGGER)
    o_ref[...] = jnp.concatenate(blocks, axis=0).astype(o_ref.dtype)


def _swa_mixer(z, sinks):
    t = z.shape[0]
    w = SWA_WINDOW
    n = SWA_TILE
    prev = lambda col: (lambda i: (jnp.maximum(i * (n // w) - 1, 0), col))
    return pl.pallas_call(
        _swa_kernel,
        grid=(t // n,),
        in_specs=[pl.BlockSpec((n, 256), lambda i: (i, 0)),
                  pl.BlockSpec((n, 128), lambda i: (i, 2)), pl.BlockSpec((n, 128), lambda i: (i, 3)),
                  pl.BlockSpec((w, 128), prev(2)), pl.BlockSpec((w, 128), prev(3)),
                  _resident((1, N_HEADS))],
        out_specs=pl.BlockSpec((n, GROUP_WIDTH), lambda i: (i, 0)),
        out_shape=jax.ShapeDtypeStruct((t, GROUP_WIDTH), MIX_DTYPE),
        compiler_params=_params(("parallel",)),
        name="swa_mixer",
    )(z, z, z, z, z, sinks.reshape(1, -1))


def _run_staggered(stage_generators, stagger):
    live = dict(enumerate(stage_generators))
    rnd = 0
    while live:
        for n in sorted(live):
            if rnd >= n * stagger:
                try:
                    next(live[n])
                except StopIteration:
                    del live[n]
        rnd += 1


def _rwkv_chunk(index, r, k, v, a_vec, b_vec, log_w, state_box, out_box):
    c = r.shape[0]
    width = r.shape[1]
    tri = jnp.where(_iota((c, c), 1) <= _iota((c, c), 0), 1.0, 0.0).astype(F32)
    p = _dot_mask(tri, log_w)
    yield
    p_total = p[c - 1:c, :]
    decay_in = jnp.exp(p)
    decay_out = jnp.exp(-p)
    decay_end = jnp.exp(p_total - p)
    a_in = a_vec * jnp.exp(p - log_w)
    r_in = r * decay_in
    b_out = b_vec * decay_out
    k_out = k * decay_out
    b_end = b_vec * decay_end
    k_end = k * decay_end

    t_pos = _iota((c, width), 0)
    assert width == N_HEADS * c
    s_pos = _imod(_iota((c, width), 1), c)
    strict = s_pos < t_pos
    incl = s_pos <= t_pos
    expand = lambda x: _expand_heads(x, HEAD_DIM)

    scores = _bdot_nt(jnp.concatenate([a_in, r_in], axis=0),
                      jnp.concatenate([expand(b_out), expand(k_out)], axis=0))
    yield
    a_ab = jnp.where(strict, scores[0:c, 0:width], 0.0)
    a_ak = jnp.where(strict, scores[0:c, width:2 * width], 0.0)
    a_rb = jnp.where(incl, scores[c:2 * c, 0:width], 0.0)
    a_rk = jnp.where(incl, scores[c:2 * c, width:2 * width], 0.0)

    t_inv = jnp.where(s_pos == t_pos, 1.0, 0.0) + a_ab
    v_heads = expand(v)
    x1 = _bdot(a_ak, v_heads)
    power = _bdot(a_ab, expand(a_ab))
    yield
    n_factors = (c - 1).bit_length()
    for _ in range(n_factors - 2):
        power_heads = expand(power)
        t_inv = t_inv + _bdot(t_inv, power_heads)
        power = _bdot(power, power_heads)
        yield
    t_inv = t_inv + _bdot(t_inv, expand(power))
    yield
    sol = _bdot(t_inv, jnp.concatenate([expand(x1), expand(a_in)], axis=1))
    yield
    u0 = sol[:, 0:width]
    w_mat = sol[:, width:2 * width]

    assert len(out_box) == index, "the previous chunk must have replaced the state before it is read"
    state = state_box[0]
    from_state = _bdot_nt(jnp.concatenate([w_mat, r_in], axis=0), state)
    yield
    u = u0 + from_state[0:c]
    y = _bdot(a_rb, expand(u)) + _bdot(a_rk, v_heads) + from_state[c:2 * c]
    upd = _dot_tn(jnp.concatenate([u, v], axis=0).astype(BF16),
                  jnp.concatenate([b_end, k_end], axis=0).astype(BF16))
    same_head = _idiv(_iota((width, width), 0), HEAD_DIM) == _idiv(_iota((width, width), 1), HEAD_DIM)
    state_box[0] = state * jnp.exp(p_total) + jnp.where(same_head, upd, 0.0)
    out_box.append(y)


def _rwkv_kernel(has_vres, *refs):
    if has_vres:
        (z_ref, zp_ref, vfirst_ref, mu_ref, w0_ref, wup_ref, a0_ref, aup_ref, gup_ref, kk_ref, ka_ref,
         rk_ref, lnw_ref, lnb_ref, v0_ref, vup_ref, o_ref, state_ref) = refs
    else:
        (z_ref, zp_ref, mu_ref, w0_ref, wup_ref, a0_ref, aup_ref, gup_ref, kk_ref, ka_ref,
         rk_ref, lnw_ref, lnb_ref, o_ref, vout_ref, state_ref) = refs
    step = pl.program_id(0)

    @pl.when(step == 0)
    def _():
        state_ref[...] = jnp.zeros_like(state_ref)

    z = z_ref[...]
    last_prev = jnp.where(step > 0, zp_ref[7:8, :], 0.0)
    prev = jnp.where(_iota(z.shape, 0) == 0, last_prev, pltpu.roll(z, 1, axis=0))
    zr = z + (prev - z) * mu_ref[...]
    r = zr[:, 0:256]
    k = zr[:, 256:512]
    v = zr[:, 512:768]
    low = zr[:, 768:896]
    w_pre = w0_ref[...] + _dot_3pass(jnp.tanh(low), wup_ref[...])
    w_log = -(jnp.maximum(-w_pre, 0.0) + jnp.log1p(jnp.exp(-jnp.abs(w_pre)))) - 0.5
    log_w = -jnp.exp(w_log)
    a = _sigmoid(a0_ref[...] + _dot_3pass(low, aup_ref[...]))
    g = _dot_3pass(_sigmoid(low), gup_ref[...])
    if has_vres:
        v = v + (vfirst_ref[...] - v) * _sigmoid(v0_ref[...] + _dot_3pass(low, vup_ref[...]))
    else:
        vout_ref[...] = v
    head_sum = _head_group_matrix(GROUP_WIDTH, HEAD_DIM, 1.0).astype(BF16)
    kk = k * kk_ref[...]
    kk = kk / jnp.maximum(jnp.sqrt(_dot_hilo(kk * kk, head_sum)), 1e-12)
    k = k * (1.0 + (a - 1.0) * ka_ref[...])
    a_vec = -kk
    b_vec = kk * a

    c = RWKV_CHUNK
    state_box = [state_ref[...]]
    chunks = []
    stages = []
    for n in range(z.shape[0] // c):
        rows = slice(n * c, (n + 1) * c)
        stages.append(_rwkv_chunk(n, r[rows], k[rows], v[rows], a_vec[rows], b_vec[rows], log_w[rows],
                                  state_box, chunks))
    _run_staggered(stages, RWKV_STAGGER)
    state_ref[...] = state_box[0]
    y = jnp.concatenate(chunks, axis=0)

    head_mean = _head_group_matrix(GROUP_WIDTH, HEAD_DIM, 1.0 / HEAD_DIM).astype(BF16)
    mu_y = _dot_hilo(y, head_mean)
    d = y - mu_y
    var_y = _dot_hilo(d * d, head_mean)
    y = d * lax.rsqrt(var_y + RWKV_GN_EPS) * lnw_ref[...] + lnb_ref[...]
    bonus = _dot_hilo(r * k * rk_ref[...], head_sum) * v
    o_ref[...] = ((y + bonus) * g).astype(o_ref.dtype)


def _rwkv_mixer(z, mu, w0, w_up, a0, a_up, g_up, k_k, k_a, r_k, lnx_w, lnx_b, vres):
    t = z.shape[0]
    c = RWKV_TILE
    row = lambda a: a.reshape(1, -1)
    low_rows = lambda a, start: jnp.zeros((LANES, GROUP_WIDTH), F32).at[start:start + a.shape[0]].set(a)
    has_vres = vres is not None
    mu_full = jnp.zeros((RWKV_W,), F32).at[:RWKV_COLS].set(mu)
    vec = _resident((1, GROUP_WIDTH))
    mat = _resident((LANES, GROUP_WIDTH))
    tile = pl.BlockSpec((c, GROUP_WIDTH), lambda i: (i, 0))
    z_specs = [pl.BlockSpec((c, RWKV_W), lambda i: (i, 0)),
               pl.BlockSpec((8, RWKV_W), lambda i: (jnp.maximum(i * (c // 8) - 1, 0), 0))]
    common = [row(w0), low_rows(w_up, 0), row(a0), low_rows(a_up, 16), low_rows(g_up, 32),
              row(k_k), row(k_a), row(r_k), row(lnx_w), row(lnx_b)]
    common_specs = [vec, mat, vec, mat, mat, vec, vec, vec, vec, vec]
    if has_vres:
        v_first, vres_mu, v0, v_up = vres
        mu_full = mu_full.at[RWKV_COLS:RWKV_COLS + RWKV_V_RANK].set(vres_mu)
        args = [z, z, v_first, row(mu_full)] + common + [row(v0), low_rows(v_up, 64)]
        in_specs = z_specs + [tile, _resident((1, RWKV_W))] + common_specs + [vec, mat]
        out_specs = tile
        out_shape = jax.ShapeDtypeStruct((t, GROUP_WIDTH), MIX_DTYPE)
    else:
        args = [z, z, row(mu_full)] + common
        in_specs = z_specs + [_resident((1, RWKV_W))] + common_specs
        out_specs = [tile, tile]
        out_shape = [jax.ShapeDtypeStruct((t, GROUP_WIDTH), MIX_DTYPE), jax.ShapeDtypeStruct((t, GROUP_WIDTH), F32)]
    return pl.pallas_call(
        functools.partial(_rwkv_kernel, has_vres),
        grid=(t // c,),
        in_specs=in_specs,
        out_specs=out_specs,
        out_shape=out_shape,
        scratch_shapes=[pltpu.VMEM((GROUP_WIDTH, GROUP_WIDTH), F32)],
        compiler_params=_params(("arbitrary",)),
        name="rwkv_mixer",
    )(*args)


def _top2_route(logits):
    lane = _iota(logits.shape, 1).astype(F32)
    logits = jnp.where(lane < N_EXPERTS, logits, -jnp.inf)
    m1 = jnp.max(logits, axis=-1, keepdims=True)
    i1 = jnp.min(jnp.where(logits == m1, lane, LANES), axis=-1, keepdims=True)
    rest = jnp.where(lane == i1, -jnp.inf, logits)
    m2 = jnp.max(rest, axis=-1, keepdims=True)
    i2 = jnp.min(jnp.where(rest == m2, lane, LANES), axis=-1, keepdims=True)
    e2 = jnp.exp(m2 - m1)
    g1 = 1.0 / (1.0 + e2)
    g2 = e2 * g1
    return jnp.where(lane == 0, i1, jnp.where(lane == 1, i2, jnp.where(lane == 2, g1, jnp.where(lane == 3, g2, 0.0))))


def _outproj_kernel(alpha, with_router, x_ref, o0_ref, o1_ref, o2_ref, o3_ref, w_ref, g_ref, b_ref, *rest):
    acc = alpha * x_ref[...]
    for h, ref in enumerate((o0_ref, o1_ref, o2_ref, o3_ref)):
        acc = acc + _dot(ref[...], w_ref[h * GROUP_WIDTH:(h + 1) * GROUP_WIDTH, :])
    y = _layer_norm(acc, g_ref[...], b_ref[...])
    if with_router:
        router_ref, y_ref, route_ref = rest
        route_ref[...] = _top2_route(_dot_3pass(y, router_ref[...]))
    else:
        (y_ref,) = rest
    y_ref[...] = y


def _out_proj_ln(alpha, x, mixes, w_out, g, b, router=None):
    t = x.shape[0]
    row_d = pl.BlockSpec((ROW_TILE, D_MODEL), lambda i: (i, 0))
    row_g = pl.BlockSpec((ROW_TILE, GROUP_WIDTH), lambda i: (i, 0))
    in_specs = [row_d, row_g, row_g, row_g, row_g, _resident((D_MODEL, D_MODEL)),
                _resident((1, D_MODEL)), _resident((1, D_MODEL))]
    args = [x, *mixes, w_out.astype(BF16), g.reshape(1, -1), b.reshape(1, -1)]
    out_specs = row_d
    out_shape = jax.ShapeDtypeStruct((t, D_MODEL), F32)
    if router is not None:
        in_specs.append(_resident((D_MODEL, LANES)))
        args.append(jnp.zeros((D_MODEL, LANES), F32).at[:, :N_EXPERTS].set(router))
        out_specs = [row_d, pl.BlockSpec((ROW_TILE, LANES), lambda i: (i, 0))]
        out_shape = [out_shape, jax.ShapeDtypeStruct((t, LANES), F32)]
    return pl.pallas_call(
        functools.partial(_outproj_kernel, alpha, router is not None),
        grid=(t // ROW_TILE,),
        in_specs=in_specs,
        out_specs=out_specs,
        out_shape=out_shape,
        compiler_params=_params(("parallel",)),
        name="out_proj_ln",
    )(*args)


def _ln_embed(y, ln_g, ln_b, p, ple_gate, ple_proj):
    x = _layer_norm(y, ln_g, ln_b)
    gate = _sigmoid(_dot(x.astype(BF16), ple_gate))
    return x + gate * _dot(p.astype(BF16), ple_proj)


def _dense_ffn_kernel(alpha, x_ref, p_ref, wg_ref, wu_ref, wd_ref, g_ref, b_ref, pg_ref, pp_ref, y_ref, acc_ref):
    x = x_ref[...]
    xb = x.astype(BF16)
    acc_ref[...] = alpha * x
    for j in range(D_FF // FF_CHUNK):
        cols = slice(j * FF_CHUNK, (j + 1) * FF_CHUNK)
        h = _silu(_dot(xb, wg_ref[:, cols])) * _dot(xb, wu_ref[:, cols])
        acc_ref[...] += _dot(h.astype(BF16), wd_ref[cols, :])
    y_ref[...] = _ln_embed(acc_ref[...], g_ref[...], b_ref[...], p_ref[...], pg_ref[...], pp_ref[...])


def _dense_ffn_tail(alpha, x, p, w_gate, w_up, w_down, g, b, ple_gate, ple_proj):
    t = x.shape[0]
    row_d = pl.BlockSpec((ROW_TILE, D_MODEL), lambda i: (i, 0))
    return pl.pallas_call(
        functools.partial(_dense_ffn_kernel, alpha),
        grid=(t // ROW_TILE,),
        in_specs=[row_d, pl.BlockSpec((ROW_TILE, PLE_DIM), lambda i: (i, 0)),
                  _resident((D_MODEL, D_FF)), _resident((D_MODEL, D_FF)), _resident((D_FF, D_MODEL)),
                  _resident((1, D_MODEL)), _resident((1, D_MODEL)),
                  _resident((D_MODEL, D_MODEL)), _resident((PLE_DIM, D_MODEL))],
        out_specs=row_d,
        out_shape=jax.ShapeDtypeStruct((t, D_MODEL), F32),
        scratch_shapes=[pltpu.VMEM((ROW_TILE, D_MODEL), F32)],
        compiler_params=_params(("parallel",)),
        name="dense_ffn_tail",
    )(x, p, w_gate.astype(BF16), w_up.astype(BF16), w_down.astype(BF16), g.reshape(1, -1), b.reshape(1, -1),
      ple_gate.astype(BF16), ple_proj.astype(BF16))


def _load_expert_weights(e, wg_hbm, wu_hbm, wd_hbm, wg_ref, wu_ref, wd_ref, stage_cols_ref, stage_rows_ref, sem):
    c = MOE_WEIGHT_CHUNK
    pieces = []
    for src, dst in ((wg_hbm, wg_ref), (wu_hbm, wu_ref)):
        for lo in range(0, D_FF_EXPERT, c):
            pieces.append((src.at[e, :, lo:lo + c], stage_cols_ref, dst, (slice(None), slice(lo, lo + c))))
    for lo in range(0, D_FF_EXPERT, c):
        pieces.append((wd_hbm.at[e, lo:lo + c, :], stage_rows_ref, wd_ref, (slice(lo, lo + c), slice(None))))
    copies = [pltpu.make_async_copy(src, stage.at[k % 2], sem.at[k % 2])
              for k, (src, stage, _, _) in enumerate(pieces)]
    copies[0].start()
    for k, (_, stage, dst, index) in enumerate(pieces):
        if k + 1 < len(pieces):
            copies[k + 1].start()
        copies[k].wait()
        dst[index] = stage[k % 2].astype(BF16)


def _expert_kernel(row_tok_ref, row_dst_ref, block_e_ref, n_used_ref, x_hbm, wg_hbm, wu_hbm, wd_hbm, y_hbm,
                   rows_ref, xb_ref, ybuf_ref, wg_ref, wu_ref, wd_ref, stage_cols_ref, stage_rows_ref,
                   gather_sem, scatter_sem, weight_sem):
    i = pl.program_id(0)
    n_used = n_used_ref[0]
    last = pl.num_programs(0) - 1
    n = MOE_ROWS
    slot = lax.rem(i, 2)
    other = 1 - slot
    gather_row = lambda tok, s, r: pltpu.make_async_copy(
        x_hbm.at[pl.ds(tok, 1)], rows_ref.at[s, pl.ds(r, 1)], gather_sem.at[s])
    scatter_row = lambda dst, s, r: pltpu.make_async_copy(
        ybuf_ref.at[s, pl.ds(r, 1)], y_hbm.at[pl.ds(dst, 1)], scatter_sem.at[s])
    gather_block = lambda s: pltpu.make_async_copy(x_hbm.at[pl.ds(0, n)], rows_ref.at[s], gather_sem.at[s])
    scatter_block = lambda s: pltpu.make_async_copy(ybuf_ref.at[s], y_hbm.at[pl.ds(0, n)], scatter_sem.at[s])

    @pl.when(i == 0)
    def _():
        ybuf_ref[1] = jnp.zeros((n, D_MODEL), F32)

        def start(r, carry):
            gather_row(row_tok_ref[r], 0, r).start()
            return carry

        lax.fori_loop(0, n, start, 0)

    def scatter_all(first_dst, s):
        def start(r, carry):
            scatter_row(row_dst_ref[first_dst + r], s, r).start()
            return carry

        lax.fori_loop(0, n, start, 0)
        scatter_block(s).wait()

    expert = block_e_ref[i]
    new_expert = (i == 0) | (expert != block_e_ref[jnp.maximum(i - 1, 0)])

    @pl.when((i < n_used) & new_expert)
    def _():
        _load_expert_weights(expert, wg_hbm, wu_hbm, wd_hbm, wg_ref, wu_ref, wd_ref,
                             stage_cols_ref, stage_rows_ref, weight_sem)

    @pl.when(i < n_used)
    def _():
        gather_block(slot).wait()
        xb_ref[...] = rows_ref[slot].astype(BF16)
        n_chunks = D_FF_EXPERT // FF_CHUNK
        rows_per_chunk = -(-n // (n_chunks - MOE_DMA_FREE_CHUNKS))
        for j in range(n_chunks):
            cols = slice(j * FF_CHUNK, (j + 1) * FF_CHUNK)
            xb = xb_ref[...]
            h = _silu(_dot(xb, wg_ref[:, cols])) * _dot(xb, wu_ref[:, cols])
            part = _dot(h.astype(BF16), wd_ref[cols, :])
            if j == 0:
                ybuf_ref[slot] = part
            else:
                ybuf_ref[slot] += part
            for r in range(j * rows_per_chunk, min((j + 1) * rows_per_chunk, n)):
                gather_row(row_tok_ref[(i + 1) * n + r], other, r).start(priority=GATHER_DMA_PRIORITY)
                scatter_row(row_dst_ref[i * n + r], other, r).start(priority=SCATTER_DMA_PRIORITY)
        scatter_block(other).wait()

    @pl.when(i == n_used)
    def _():
        gather_block(slot).wait()
        scatter_all(i * n, other)

    @pl.when(i >= n_used)
    def _():
        rows_ref[slot] = jnp.zeros((n, D_MODEL), F32)
        fill = pltpu.make_async_copy(rows_ref.at[slot], y_hbm.at[pl.ds(i * n, n)], scatter_sem.at[slot])
        fill.start()
        fill.wait()

    @pl.when((i == last) & (i < n_used))
    def _():
        gather_block(other).wait()
        scatter_all((i + 1) * n, slot)


def _expert_rows(x, row_tok, row_dst, block_e, n_used, n_out_rows, w_gate, w_up, w_down):
    n_blocks = block_e.shape[0]
    hbm = pl.BlockSpec(memory_space=pl.ANY)
    grid_spec = pltpu.PrefetchScalarGridSpec(
        num_scalar_prefetch=4,
        grid=(n_blocks,),
        in_specs=[hbm, hbm, hbm, hbm],
        out_specs=hbm,
        scratch_shapes=[pltpu.VMEM((2, MOE_ROWS, D_MODEL), F32), pltpu.VMEM((MOE_ROWS, D_MODEL), BF16),
                        pltpu.VMEM((2, MOE_ROWS, D_MODEL), F32),
                        pltpu.VMEM((D_MODEL, D_FF_EXPERT), BF16), pltpu.VMEM((D_MODEL, D_FF_EXPERT), BF16),
                        pltpu.VMEM((D_FF_EXPERT, D_MODEL), BF16),
                        pltpu.VMEM((2, D_MODEL, MOE_WEIGHT_CHUNK), F32),
                        pltpu.VMEM((2, MOE_WEIGHT_CHUNK, D_MODEL), F32),
                        pltpu.SemaphoreType.DMA((2,)), pltpu.SemaphoreType.DMA((2,)),
                        pltpu.SemaphoreType.DMA((2,))],
    )
    return pl.pallas_call(
        _expert_kernel,
        grid_spec=grid_spec,
        out_shape=jax.ShapeDtypeStruct((n_out_rows, D_MODEL), F32),
        compiler_params=_params(("arbitrary",)),
        name="moe_experts",
    )(row_tok, row_dst, block_e, n_used.reshape(1), x, w_gate, w_up, w_down)


def _combine_kernel(alpha, x_ref, y0_ref, y1_ref, gates_ref, p_ref, g_ref, b_ref, pg_ref, pp_ref, o_ref):
    gates = gates_ref[...]
    f = y0_ref[...] * gates[:, 0:1] + y1_ref[...] * gates[:, 1:2]
    o_ref[...] = _ln_embed(alpha * x_ref[...] + f, g_ref[...], b_ref[...], p_ref[...], pg_ref[...], pp_ref[...])


def _moe_combine_tail(alpha, x, p, y, gates, g, b, ple_gate, ple_proj):
    t = x.shape[0]
    n = ROW_TILE
    row_d = pl.BlockSpec((n, D_MODEL), lambda i: (i, 0))
    return pl.pallas_call(
        functools.partial(_combine_kernel, alpha),
        grid=(t // n,),
        in_specs=[row_d, row_d, pl.BlockSpec((n, D_MODEL), lambda i: (t // n + i, 0)),
                  pl.BlockSpec((n, 2), lambda i: (i, 0)), pl.BlockSpec((n, PLE_DIM), lambda i: (i, 0)),
                  _resident((1, D_MODEL)), _resident((1, D_MODEL)),
                  _resident((D_MODEL, D_MODEL)), _resident((PLE_DIM, D_MODEL))],
        out_specs=row_d,
        out_shape=jax.ShapeDtypeStruct((t, D_MODEL), F32),
        compiler_params=_params(("parallel",)),
        name="moe_combine_tail",
    )(x, y, y, gates, p, g.reshape(1, -1), b.reshape(1, -1), ple_gate.astype(BF16), ple_proj.astype(BF16))


def _moe_tail(alpha, x, route, p, w_gate, w_up, w_down, g, b, ple_gate, ple_proj):
    t = x.shape[0]
    experts = route[:, 0:2].astype(jnp.int32)
    gates = route[:, 2:4]
    e_flat = experts.reshape(-1)
    onehot = (e_flat[:, None] == jnp.arange(N_EXPERTS, dtype=jnp.int32)[None, :]).astype(jnp.int32)
    running = jnp.cumsum(onehot, axis=0)
    rank = jnp.sum(onehot * (running - 1), axis=1)
    counts = running[-1]
    padded = (counts + MOE_ROWS - 1) // MOE_ROWS * MOE_ROWS
    pad_end = jnp.cumsum(padded)
    pad_start = pad_end - padded
    dest = (pad_start[e_flat] + rank).astype(jnp.int32)
    n_blocks = (2 * t) // MOE_ROWS + N_EXPERTS
    n_rows = n_blocks * MOE_ROWS
    assign = jnp.arange(2 * t, dtype=jnp.int32)
    row_assign = jnp.full((n_rows,), -1, jnp.int32).at[dest].set(assign)
    used = row_assign >= 0
    spare = 2 * t + jnp.cumsum(jnp.where(used, 0, 1).astype(jnp.int32)) - 1
    row_tok = jnp.where(used, row_assign // 2, 0)
    row_dst = jnp.where(used, (row_assign % 2) * t + row_assign // 2, spare)
    n_spare = n_rows - 2 * t
    first_dst = 2 * t + n_spare + jnp.arange(MOE_ROWS, dtype=jnp.int32)
    row_tok = jnp.concatenate([row_tok, jnp.zeros((MOE_ROWS,), jnp.int32)])
    row_dst = jnp.concatenate([first_dst, row_dst])
    block_start = jnp.arange(n_blocks, dtype=jnp.int32) * MOE_ROWS
    block_e = jnp.minimum(jnp.sum((block_start[:, None] >= pad_end[None, :]).astype(jnp.int32), axis=1),
                          N_EXPERTS - 1)
    n_used = (pad_end[-1] // MOE_ROWS).astype(jnp.int32)
    y = _expert_rows(x, row_tok, row_dst, block_e, n_used, 2 * t + n_spare + MOE_ROWS, w_gate, w_up, w_down)
    return _moe_combine_tail(alpha, x, p, y, gates, g, b, ple_gate, ple_proj)


def kernel(x, p, w_in, w_out, gla_gk_up, gla_gk_bias, gla_norm_w, hgrn_lower_bounds, hgrn_norm_w, swa_sinks, rwkv_mu, rwkv_w0, rwkv_w_up, rwkv_a0, rwkv_a_up, rwkv_g_up, rwkv_k_k, rwkv_k_a, rwkv_r_k, rwkv_lnx_w, rwkv_lnx_b, rwkv_vres_down, rwkv_vres_mu, rwkv_v0, rwkv_vres_up, ln1_g, ln1_b, ln2_g, ln2_b, ffn_w_gate, ffn_w_up, ffn_w_down, moe_router, moe_w_gate, moe_w_up, moe_w_down, ple_proj, ple_gate):
    bsz, seq, d = x.shape
    depth = w_in.shape[0]
    alpha = (2.0 * depth) ** 0.25
    lbs = jnp.cumsum(jax.nn.softmax(hgrn_lower_bounds.astype(F32), axis=0), axis=0)
    lbs = lbs - lbs[0]
    outs = []
    for bi in range(bsz):
        xt = x[bi]
        v_first = None
        for i in range(depth):
            w = _group_in_weights(w_in[i], None if i == 0 else rwkv_vres_down[i - 1])
            z_gla, z_hgrn, z_swa, z_rwkv = _in_proj(xt, w)
            o_gla = _gla_mixer(z_gla, gla_gk_up[i], gla_gk_bias[i], gla_norm_w[i])
            o_hgrn = _hgrn_mixer(z_hgrn, lbs[i], hgrn_norm_w[i])
            o_swa = _swa_mixer(z_swa, swa_sinks[i])
            vres = None if i == 0 else (v_first, rwkv_vres_mu[i - 1], rwkv_v0[i - 1], rwkv_vres_up[i - 1])
            rw = _rwkv_mixer(z_rwkv, rwkv_mu[i], rwkv_w0[i], rwkv_w_up[i], rwkv_a0[i], rwkv_a_up[i],
                             rwkv_g_up[i], rwkv_k_k[i], rwkv_k_a[i], rwkv_r_k[i].reshape(-1),
                             rwkv_lnx_w[i], rwkv_lnx_b[i], vres)
            if i == 0:
                o_rwkv, v_first = rw
            else:
                o_rwkv = rw
            mixes = (o_gla, o_hgrn, o_swa, o_rwkv)
            j = i // 2
            if i % 2 == 0:
                xt = _out_proj_ln(alpha, xt, mixes, w_out[i], ln1_g[i], ln1_b[i])
                xt = _dense_ffn_tail(alpha, xt, p[i, bi], ffn_w_gate[j], ffn_w_up[j], ffn_w_down[j],
                                     ln2_g[i], ln2_b[i], ple_gate[i], ple_proj[i])
            else:
                xt, route = _out_proj_ln(alpha, xt, mixes, w_out[i], ln1_g[i], ln1_b[i], moe_router[j])
                xt = _moe_tail(alpha, xt, route, p[i, bi], moe_w_gate[j], moe_w_up[j], moe_w_down[j],
                               ln2_g[i], ln2_b[i], ple_gate[i], ple_proj[i])
        outs.append(xt)
    return jnp.stack(outs, axis=0)
```

```python
import functools

import jax
import jax.numpy as jnp
from jax import lax
from jax.experimental import pallas as pl
from jax.experimental.pallas import tpu as pltpu

F32 = jnp.float32
BF16 = jnp.bfloat16
HIGHEST = lax.Precision.HIGHEST
MIX_DTYPE = BF16

D_MODEL = 1024
GROUP_WIDTH = 256
N_HEADS = 4
HEAD_DIM = 64
GLA_DK = 32
GLA_GATE_RANK = 16
GLA_GATE_NORMALIZER = 16.0
SWA_WINDOW = 128
RWKV_COLS = 3 * GROUP_WIDTH + 16 + 16 + 32
RWKV_V_RANK = 8
LN_EPS = 1e-5
RMS_EPS = 1e-6
RWKV_GN_EPS = 64e-5
D_FF = 2816
N_EXPERTS = 8
D_FF_EXPERT = 3584
PLE_DIM = 256

LANES = 128
SUBLANES = 8
GLA_W = 896
HGRN_W = 1024
SWA_W = 512
RWKV_W = 896
Z_W = GLA_W + HGRN_W + SWA_W + RWKV_W

GLA_SUB = 16
GLA_TILE = 128
GLA_STEP = 1024
GLA_STAGGER = 1
SWA_STAGGER = 1
RWKV_CHUNK = 64
RWKV_TILE = 1024
RWKV_STAGGER = 2
ROW_TILE = 512
FF_CHUNK = 256
SWA_TILE = 256
MOE_ROWS = 512
MOE_WEIGHT_CHUNK = 512
MOE_DMA_FREE_CHUNKS = 10
GATHER_DMA_PRIORITY = 0
SCATTER_DMA_PRIORITY = 1
VMEM_LIMIT = 56 * 1024 * 1024


def _iota(shape, dim):
    return lax.broadcasted_iota(jnp.int32, shape, dim)


def _idiv(x, n):
    return jnp.right_shift(x, n.bit_length() - 1)


def _imod(x, n):
    return jnp.bitwise_and(x, n - 1)


def _dot(a, b, precision=None):
    return jnp.dot(a, b, preferred_element_type=F32, precision=precision)


def _dot_nt(a, b, precision=None):
    return lax.dot_general(a, b, (((1,), (1,)), ((), ())), preferred_element_type=F32, precision=precision)


def _dot_tn(a, b, precision=None):
    return lax.dot_general(a, b, (((0,), (0,)), ((), ())), preferred_element_type=F32, precision=precision)


def _bdot(a, b):
    return _dot(a.astype(BF16), b.astype(BF16))


def _bdot_nt(a, b):
    return _dot_nt(a.astype(BF16), b.astype(BF16))


def _dot_hilo(x, m):
    hi = x.astype(BF16)
    lo = (x - hi.astype(F32)).astype(BF16)
    return _dot(hi, m) + _dot(lo, m)


def _dot_mask(m, x):
    m = m.astype(BF16)
    x1 = x.astype(BF16)
    r1 = x - x1.astype(F32)
    x2 = r1.astype(BF16)
    x3 = (r1 - x2.astype(F32)).astype(BF16)
    return _dot(m, x1) + _dot(m, x2) + _dot(m, x3)


def _dot_3pass(x, w):
    x_hi = x.astype(BF16)
    x_lo = (x - x_hi.astype(F32)).astype(BF16)
    w_hi = w.astype(BF16)
    w_lo = (w - w_hi.astype(F32)).astype(BF16)
    return _dot(x_hi, w_hi) + _dot(x_lo, w_hi) + _dot(x_hi, w_lo)


def _sigmoid(x):
    return 1.0 / (1.0 + jnp.exp(-x))


def _silu(x):
    return x * _sigmoid(x)


def _log_sigmoid(x):
    return jnp.minimum(x, 0.0) - jnp.log1p(jnp.exp(-jnp.abs(x)))


def _layer_norm(y, g, b):
    mu = jnp.mean(y, axis=-1, keepdims=True)
    d = y - mu
    var = jnp.mean(d * d, axis=-1, keepdims=True)
    return d * lax.rsqrt(var + LN_EPS) * g + b


def _expand_heads(x, head_width):
    lane_head = _idiv(_iota(x.shape, 1), head_width)
    return jnp.concatenate([jnp.where(lane_head == h, x, 0.0) for h in range(N_HEADS)], axis=0)


def _head_group_matrix(width, head_width, value):
    same = _idiv(_iota((width, width), 0), head_width) == _idiv(_iota((width, width), 1), head_width)
    return jnp.where(same, value, 0.0).astype(F32)


def _resident(shape):
    nd = len(shape)
    return pl.BlockSpec(shape, lambda *_: (0,) * nd, pipeline_mode=pl.Buffered(1))


def _params(semantics):
    return pltpu.CompilerParams(dimension_semantics=semantics, vmem_limit_bytes=VMEM_LIMIT)


def _inproj_kernel(x_ref, w_ref, gla_ref, hgrn_ref, swa_ref, rwkv_ref):
    xb = x_ref[...].astype(BF16)
    o = 0
    for ref, width in ((gla_ref, GLA_W), (hgrn_ref, HGRN_W), (swa_ref, SWA_W), (rwkv_ref, RWKV_W)):
        ref[...] = _dot(xb, w_ref[:, o:o + width])
        o += width


def _in_proj(x, w):
    t = x.shape[0]
    widths = (GLA_W, HGRN_W, SWA_W, RWKV_W)
    return pl.pallas_call(
        _inproj_kernel,
        grid=(t // ROW_TILE,),
        in_specs=[pl.BlockSpec((ROW_TILE, D_MODEL), lambda i: (i, 0)), _resident((D_MODEL, Z_W))],
        out_specs=[pl.BlockSpec((ROW_TILE, w_), lambda i: (i, 0)) for w_ in widths],
        out_shape=[jax.ShapeDtypeStruct((t, w_), F32) for w_ in widths],
        compiler_params=_params(("parallel",)),
        name="in_proj",
    )(x, w)


def _group_in_weights(w_in, vres_down):
    gla, hgrn, swa, rwkv = jnp.split(w_in, (784, 784 + 1024, 784 + 1024 + 512), axis=1)
    if vres_down is not None:
        rwkv = jnp.concatenate([rwkv, vres_down], axis=1)
    pad = lambda a, w_: jnp.pad(a, ((0, 0), (0, w_ - a.shape[1])))
    return jnp.concatenate([pad(gla, GLA_W), hgrn, swa, pad(rwkv, RWKV_W)], axis=1).astype(BF16)


def _gated_linear_attention_tile(index, q, k, v, log_f, state_box, out_box):
    length, kw = q.shape
    head_k = kw // N_HEADS
    n_sub = length // GLA_SUB
    row = _iota((length, length), 0)
    col = _iota((length, length), 1)
    same_sub = _idiv(row, GLA_SUB) == _idiv(col, GLA_SUB)
    m_local = jnp.where(same_sub & (col <= row), 1.0, 0.0).astype(F32)
    m_prev = jnp.where(_idiv(col, GLA_SUB) < _idiv(row, GLA_SUB), 1.0, 0.0).astype(F32)
    b_local = _dot_mask(m_local, log_f)
    b_start = _dot_mask(m_prev, log_f)
    yield
    b_full = b_start + b_local
    q_local = q * jnp.exp(b_local)

    q_pos = _imod(_iota((N_HEADS * GLA_SUB, length), 0), GLA_SUB)
    s_pos = _iota((N_HEADS * GLA_SUB, length), 1)
    probs = []
    for c in range(n_sub):
        r0, r1 = c * GLA_SUB, (c + 1) * GLA_SUB
        expo = jnp.where(_iota((r1, kw), 0) < r0, b_start[r0:r0 + 1, :] - b_full[0:r1], -b_local[0:r1])
        k_ref = k[0:r1] * jnp.exp(expo)
        if r1 < length:
            k_ref = jnp.concatenate([k_ref, jnp.zeros((length - r1, kw), F32)], axis=0)
        q_heads = _expand_heads(q_local[r0:r1, :], head_k)
        s = _bdot_nt(q_heads, k_ref)
        probs.append(jnp.where(s_pos <= q_pos + r0, s, 0.0))
    b_total = b_full[length - 1:length, :]
    k_end = k * jnp.exp(b_total - b_full)
    upd = _dot_tn(v.astype(BF16), k_end.astype(BF16))
    yield
    o_heads = _bdot(jnp.concatenate(probs, axis=0), v)
    assert len(out_box) == index, "the previous tile must have replaced the state before it is read"
    state_t = state_box[0]
    o_state = _bdot_nt(q * jnp.exp(b_full), state_t)
    same_head = _idiv(_iota((GROUP_WIDTH, kw), 0), HEAD_DIM) == _idiv(_iota((GROUP_WIDTH, kw), 1), head_k)
    state_box[0] = state_t * jnp.exp(b_total) + jnp.where(same_head, upd, 0.0)
    yield
    v_head = _idiv(_iota((GLA_SUB, GROUP_WIDTH), 1), HEAD_DIM)
    rows = []
    for c in range(n_sub):
        base = c * N_HEADS * GLA_SUB
        acc = jnp.zeros((GLA_SUB, GROUP_WIDTH), F32)
        for h in range(N_HEADS):
            acc = acc + jnp.where(v_head == h, o_heads[base + h * GLA_SUB:base + (h + 1) * GLA_SUB, :], 0.0)
        rows.append(acc)
    out_box.append(jnp.concatenate(rows, axis=0) + o_state)


def _gated_linear_attention(q, k, v, log_f, state_ref):
    state_box = [state_ref[...]]
    tiles = []
    stages = []
    for n in range(q.shape[0] // GLA_TILE):
        rows = slice(n * GLA_TILE, (n + 1) * GLA_TILE)
        stages.append(_gated_linear_attention_tile(n, q[rows], k[rows], v[rows], log_f[rows], state_box, tiles))
    _run_staggered(stages, GLA_STAGGER)
    state_ref[...] = state_box[0]
    return jnp.concatenate(tiles, axis=0)


def _head_rms_gate(o, norm_w, gate):
    ms = _dot_hilo(o * o, _head_group_matrix(GROUP_WIDTH, HEAD_DIM, 1.0 / HEAD_DIM).astype(BF16))
    return o * lax.rsqrt(ms + RMS_EPS) * norm_w * _silu(gate)


def _gla_kernel(z_ref, gk_up_ref, gk_bias_ref, norm_w_ref, o_ref, state_ref):
    @pl.when(pl.program_id(0) == 0)
    def _():
        state_ref[...] = jnp.zeros_like(state_ref)

    z = z_ref[...]
    q = z[:, 0:128] * (GLA_DK ** -0.5)
    k = z[:, 128:256]
    v = z[:, 256:512]
    g = z[:, 512:768]
    gate_in = _dot_3pass(z[:, 768:896], gk_up_ref[...]) + gk_bias_ref[...]
    log_f = _log_sigmoid(gate_in) * (1.0 / GLA_GATE_NORMALIZER)
    o = _gated_linear_attention(q, k, v, log_f, state_ref)
    o_ref[...] = _head_rms_gate(o, norm_w_ref[...], g).astype(o_ref.dtype)


def _hgrn_kernel(z_ref, lb_ref, log_lb_ref, norm_w_ref, o_ref, state_ref):
    @pl.when(pl.program_id(0) == 0)
    def _():
        state_ref[...] = jnp.zeros_like(state_ref)

    z = z_ref[...]
    q = _silu(z[:, 0:256])
    f = z[:, 256:512]
    v = z[:, 512:768]
    g = z[:, 768:1024]
    lb = lb_ref[...]
    a = log_lb_ref[...]
    c = jnp.log1p(-lb) + _log_sigmoid(f)
    log_f = jnp.maximum(a, c) + jnp.log1p(jnp.exp(-jnp.abs(a - c)))
    k = (1.0 - lb) * _sigmoid(-f)
    o = _gated_linear_attention(q, k, v, log_f, state_ref)
    o_ref[...] = _head_rms_gate(o, norm_w_ref[...], g).astype(o_ref.dtype)


def _gla_mixer(z, gk_up, gk_bias, norm_w):
    t = z.shape[0]
    gk_up_pad = jnp.zeros((LANES, N_HEADS * GLA_DK), F32).at[:GLA_GATE_RANK].set(gk_up)
    return pl.pallas_call(
        _gla_kernel,
        grid=(t // GLA_STEP,),
        in_specs=[pl.BlockSpec((GLA_STEP, GLA_W), lambda i: (i, 0)),
                  _resident((LANES, N_HEADS * GLA_DK)), _resident((1, N_HEADS * GLA_DK)),
                  _resident((1, GROUP_WIDTH))],
        out_specs=pl.BlockSpec((GLA_STEP, GROUP_WIDTH), lambda i: (i, 0)),
        out_shape=jax.ShapeDtypeStruct((t, GROUP_WIDTH), MIX_DTYPE),
        scratch_shapes=[pltpu.VMEM((GROUP_WIDTH, N_HEADS * GLA_DK), F32)],
        compiler_params=_params(("arbitrary",)),
        name="gla_mixer",
    )(z, gk_up_pad, gk_bias.reshape(1, -1), jnp.tile(norm_w, N_HEADS).reshape(1, -1))


def _hgrn_mixer(z, lb, norm_w):
    t = z.shape[0]
    return pl.pallas_call(
        _hgrn_kernel,
        grid=(t // GLA_STEP,),
        in_specs=[pl.BlockSpec((GLA_STEP, HGRN_W), lambda i: (i, 0)),
                  _resident((1, GROUP_WIDTH)), _resident((1, GROUP_WIDTH)), _resident((1, GROUP_WIDTH))],
        out_specs=pl.BlockSpec((GLA_STEP, GROUP_WIDTH), lambda i: (i, 0)),
        out_shape=jax.ShapeDtypeStruct((t, GROUP_WIDTH), MIX_DTYPE),
        scratch_shapes=[pltpu.VMEM((GROUP_WIDTH, GROUP_WIDTH), F32)],
        compiler_params=_params(("arbitrary",)),
        name="hgrn_mixer",
    )(z, lb.reshape(1, -1), jnp.log(lb).reshape(1, -1), jnp.tile(norm_w, N_HEADS).reshape(1, -1))


def _swa_block(q, kw, vw, visible, sinks, out_box):
    n = q.shape[0]
    head_cols = lambda x, h: x[:, h * HEAD_DIM:(h + 1) * HEAD_DIM]
    v_lane_head = _idiv(_iota(vw.shape, 1), HEAD_DIM)
    outs = []
    for kv in range(N_HEADS // 2):
        heads = (2 * kv, 2 * kv + 1)
        scores = _bdot_nt(jnp.concatenate([head_cols(q, h) for h in heads], axis=0), head_cols(kw, kv))
        yield
        probs, sink_terms = [], []
        for half, h in enumerate(heads):
            s = jnp.where(visible, scores[half * n:(half + 1) * n], -jnp.inf)
            sink = sinks[:, h:h + 1]
            m = jnp.maximum(jnp.max(s, axis=-1, keepdims=True), sink)
            probs.append(jnp.exp(s - m))
            sink_terms.append(jnp.exp(sink - m))
        o = _bdot(jnp.concatenate(probs, axis=0), jnp.where(v_lane_head == kv, vw, 1.0))
        yield
        sums = head_cols(o, 1 - kv)[:, 0:1]
        outs += [head_cols(o, kv)[half * n:(half + 1) * n] / (sums[half * n:(half + 1) * n] + sink_terms[half])
                 for half in range(2)]
    out_box.append(jnp.concatenate(outs, axis=-1))


def _swa_kernel(q_ref, k_ref, v_ref, kp_ref, vp_ref, sink_ref, o_ref):
    w = SWA_WINDOW
    has_prev = pl.program_id(0) > 0
    q = q_ref[...] * (HEAD_DIM ** -0.5)
    k_all = jnp.concatenate([kp_ref[...], k_ref[...]], axis=0)
    v_all = jnp.concatenate([vp_ref[...], v_ref[...]], axis=0)
    q_pos = _iota((w, 2 * w), 0) + w
    k_pos = _iota((w, 2 * w), 1)
    dist = q_pos - k_pos
    in_window = (dist >= 0) & (dist < w)
    sinks = sink_ref[...]
    blocks = []
    stages = []
    for b in range(q.shape[0] // w):
        visible = in_window if b > 0 else in_window & ((k_pos >= w) | has_prev)
        stages.append(_swa_block(q[b * w:(b + 1) * w], k_all[b * w:(b + 2) * w], v_all[b * w:(b + 2) * w],
                                 visible, sinks, blocks))
    _run_staggered(stages, SWA_STAGGER)
    o_ref[...] = jnp.concatenate(blocks, axis=0).astype(o_ref.dtype)


def _swa_mixer(z, sinks):
    t = z.shape[0]
    w = SWA_WINDOW
    n = SWA_TILE
    prev = lambda col: (lambda i: (jnp.maximum(i * (n // w) - 1, 0), col))
    return pl.pallas_call(
        _swa_kernel,
        grid=(t // n,),
        in_specs=[pl.BlockSpec((n, 256), lambda i: (i, 0)),
                  pl.BlockSpec((n, 128), lambda i: (i, 2)), pl.BlockSpec((n, 128), lambda i: (i, 3)),
                  pl.BlockSpec((w, 128), prev(2)), pl.BlockSpec((w, 128), prev(3)),
                  _resident((1, N_HEADS))],
        out_specs=pl.BlockSpec((n, GROUP_WIDTH), lambda i: (i, 0)),
        out_shape=jax.ShapeDtypeStruct((t, GROUP_WIDTH), MIX_DTYPE),
        compiler_params=_params(("parallel",)),
        name="swa_mixer",
    )(z, z, z, z, z, sinks.reshape(1, -1))


def _run_staggered(stage_generators, stagger):
    live = dict(enumerate(stage_generators))
    rnd = 0
    while live:
        for n in sorted(live):
            if rnd >= n * stagger:
                try:
                    next(live[n])
                except StopIteration:
                    del live[n]
        rnd += 1


def _rwkv_chunk(index, r, k, v, a_vec, b_vec, log_w, state_box, out_box):
    c = r.shape[0]
    width = r.shape[1]
    tri = jnp.where(_iota((c, c), 1) <= _iota((c, c), 0), 1.0, 0.0).astype(F32)
    p = _dot_mask(tri, log_w)
    yield
    p_total = p[c - 1:c, :]
    decay_in = jnp.exp(p)
    decay_out = jnp.exp(-p)
    decay_end = jnp.exp(p_total - p)
    a_in = a_vec * jnp.exp(p - log_w)
    r_in = r * decay_in
    b_out = b_vec * decay_out
    k_out = k * decay_out
    b_end = b_vec * decay_end
    k_end = k * decay_end

    t_pos = _iota((c, width), 0)
    assert width == N_HEADS * c
    s_pos = _imod(_iota((c, width), 1), c)
    strict = s_pos < t_pos
    incl = s_pos <= t_pos
    expand = lambda x: _expand_heads(x, HEAD_DIM)

    scores = _bdot_nt(jnp.concatenate([a_in, r_in], axis=0),
                      jnp.concatenate([expand(b_out), expand(k_out)], axis=0))
    yield
    a_ab = jnp.where(strict, scores[0:c, 0:width], 0.0)
    a_ak = jnp.where(strict, scores[0:c, width:2 * width], 0.0)
    a_rb = jnp.where(incl, scores[c:2 * c, 0:width], 0.0)
    a_rk = jnp.where(incl, scores[c:2 * c, width:2 * width], 0.0)

    t_inv = jnp.where(s_pos == t_pos, 1.0, 0.0) + a_ab
    v_heads = expand(v)
    x1 = _bdot(a_ak, v_heads)
    power = _bdot(a_ab, expand(a_ab))
    yield
    n_factors = (c - 1).bit_length()
    for _ in range(n_factors - 2):
        power_heads = expand(power)
        t_inv = t_inv + _bdot(t_inv, power_heads)
        power = _bdot(power, power_heads)
        yield
    t_inv = t_inv + _bdot(t_inv, expand(power))
    yield
    sol = _bdot(t_inv, jnp.concatenate([expand(x1), expand(a_in)], axis=1))
    yield
    u0 = sol[:, 0:width]
    w_mat = sol[:, width:2 * width]

    assert len(out_box) == index, "the previous chunk must have replaced the state before it is read"
    state = state_box[0]
    from_state = _bdot_nt(jnp.concatenate([w_mat, r_in], axis=0), state)
    yield
    u = u0 + from_state[0:c]
    y = _bdot(a_rb, expand(u)) + _bdot(a_rk, v_heads) + from_state[c:2 * c]
    upd = _dot_tn(jnp.concatenate([u, v], axis=0).astype(BF16),
                  jnp.concatenate([b_end, k_end], axis=0).astype(BF16))
    same_head = _idiv(_iota((width, width), 0), HEAD_DIM) == _idiv(_iota((width, width), 1), HEAD_DIM)
    state_box[0] = state * jnp.exp(p_total) + jnp.where(same_head, upd, 0.0)
    out_box.append(y)


def _rwkv_kernel(has_vres, *refs):
    if has_vres:
        (z_ref, zp_ref, vfirst_ref, mu_ref, w0_ref, wup_ref, a0_ref, aup_ref, gup_ref, kk_ref, ka_ref,
         rk_ref, lnw_ref, lnb_ref, v0_ref, vup_ref, o_ref, state_ref) = refs
    else:
        (z_ref, zp_ref, mu_ref, w0_ref, wup_ref, a0_ref, aup_ref, gup_ref, kk_ref, ka_ref,
         rk_ref, lnw_ref, lnb_ref, o_ref, vout_ref, state_ref) = refs
    step = pl.program_id(0)

    @pl.when(step == 0)
    def _():
        state_ref[...] = jnp.zeros_like(state_ref)

    z = z_ref[...]
    last_prev = jnp.where(step > 0, zp_ref[7:8, :], 0.0)
    prev = jnp.where(_iota(z.shape, 0) == 0, last_prev, pltpu.roll(z, 1, axis=0))
    zr = z + (prev - z) * mu_ref[...]
    r = zr[:, 0:256]
    k = zr[:, 256:512]
    v = zr[:, 512:768]
    low = zr[:, 768:896]
    w_pre = w0_ref[...] + _dot_3pass(jnp.tanh(low), wup_ref[...])
    w_log = -(jnp.maximum(-w_pre, 0.0) + jnp.log1p(jnp.exp(-jnp.abs(w_pre)))) - 0.5
    log_w = -jnp.exp(w_log)
    a = _sigmoid(a0_ref[...] + _dot_3pass(low, aup_ref[...]))
    g = _dot_3pass(_sigmoid(low), gup_ref[...])
    if has_vres:
        v = v + (vfirst_ref[...] - v) * _sigmoid(v0_ref[...] + _dot_3pass(low, vup_ref[...]))
    else:
        vout_ref[...] = v
    head_sum = _head_group_matrix(GROUP_WIDTH, HEAD_DIM, 1.0).astype(BF16)
    kk = k * kk_ref[...]
    kk = kk / jnp.maximum(jnp.sqrt(_dot_hilo(kk * kk, head_sum)), 1e-12)
    k = k * (1.0 + (a - 1.0) * ka_ref[...])
    a_vec = -kk
    b_vec = kk * a

    c = RWKV_CHUNK
    state_box = [state_ref[...]]
    chunks = []
    stages = []
    for n in range(z.shape[0] // c):
        rows = slice(n * c, (n + 1) * c)
        stages.append(_rwkv_chunk(n, r[rows], k[rows], v[rows], a_vec[rows], b_vec[rows], log_w[rows],
                                  state_box, chunks))
    _run_staggered(stages, RWKV_STAGGER)
    state_ref[...] = state_box[0]
    y = jnp.concatenate(chunks, axis=0)

    head_mean = _head_group_matrix(GROUP_WIDTH, HEAD_DIM, 1.0 / HEAD_DIM).astype(BF16)
    mu_y = _dot_hilo(y, head_mean)
    d = y - mu_y
    var_y = _dot_hilo(d * d, head_mean)
    y = d * lax.rsqrt(var_y + RWKV_GN_EPS) * lnw_ref[...] + lnb_ref[...]
    bonus = _dot_hilo(r * k * rk_ref[...], head_sum) * v
    o_ref[...] = ((y + bonus) * g).astype(o_ref.dtype)


def _rwkv_mixer(z, mu, w0, w_up, a0, a_up, g_up, k_k, k_a, r_k, lnx_w, lnx_b, vres):
    t = z.shape[0]
    c = RWKV_TILE
    row = lambda a: a.reshape(1, -1)
    low_rows = lambda a, start: jnp.zeros((LANES, GROUP_WIDTH), F32).at[start:start + a.shape[0]].set(a)
    has_vres = vres is not None
    mu_full = jnp.zeros((RWKV_W,), F32).at[:RWKV_COLS].set(mu)
    vec = _resident((1, GROUP_WIDTH))
    mat = _resident((LANES, GROUP_WIDTH))
    tile = pl.BlockSpec((c, GROUP_WIDTH), lambda i: (i, 0))
    z_specs = [pl.BlockSpec((c, RWKV_W), lambda i: (i, 0)),
               pl.BlockSpec((8, RWKV_W), lambda i: (jnp.maximum(i * (c // 8) - 1, 0), 0))]
    common = [row(w0), low_rows(w_up, 0), row(a0), low_rows(a_up, 16), low_rows(g_up, 32),
              row(k_k), row(k_a), row(r_k), row(lnx_w), row(lnx_b)]
    common_specs = [vec, mat, vec, mat, mat, vec, vec, vec, vec, vec]
    if has_vres:
        v_first, vres_mu, v0, v_up = vres
        mu_full = mu_full.at[RWKV_COLS:RWKV_COLS + RWKV_V_RANK].set(vres_mu)
        args = [z, z, v_first, row(mu_full)] + common + [row(v0), low_rows(v_up, 64)]
        in_specs = z_specs + [tile, _resident((1, RWKV_W))] + common_specs + [vec, mat]
        out_specs = tile
        out_shape = jax.ShapeDtypeStruct((t, GROUP_WIDTH), MIX_DTYPE)
    else:
        args = [z, z, row(mu_full)] + common
        in_specs = z_specs + [_resident((1, RWKV_W))] + common_specs
        out_specs = [tile, tile]
        out_shape = [jax.ShapeDtypeStruct((t, GROUP_WIDTH), MIX_DTYPE), jax.ShapeDtypeStruct((t, GROUP_WIDTH), F32)]
    return pl.pallas_call(
        functools.partial(_rwkv_kernel, has_vres),
        grid=(t // c,),
        in_specs=in_specs,
        out_specs=out_specs,
        out_shape=out_shape,
        scratch_shapes=[pltpu.VMEM((GROUP_WIDTH, GROUP_WIDTH), F32)],
        compiler_params=_params(("arbitrary",)),
        name="rwkv_mixer",
    )(*args)


def _store_token_tiles(ref, index, x):
    n = x.shape[0]
    for j in range(SUBLANES):
        ref[(*index, pl.ds(j, n, stride=SUBLANES), slice(None))] = x[:, j * LANES:(j + 1) * LANES]


def _load_token_tiles(ref, index, n):
    return jnp.concatenate([ref[(*index, pl.ds(j, n, stride=SUBLANES), slice(None))] for j in range(SUBLANES)],
                           axis=-1)


def _top2_route(logits):
    lane = _iota(logits.shape, 1).astype(F32)
    logits = jnp.where(lane < N_EXPERTS, logits, -jnp.inf)
    m1 = jnp.max(logits, axis=-1, keepdims=True)
    i1 = jnp.min(jnp.where(logits == m1, lane, LANES), axis=-1, keepdims=True)
    rest = jnp.where(lane == i1, -jnp.inf, logits)
    m2 = jnp.max(rest, axis=-1, keepdims=True)
    i2 = jnp.min(jnp.where(rest == m2, lane, LANES), axis=-1, keepdims=True)
    e2 = jnp.exp(m2 - m1)
    g1 = 1.0 / (1.0 + e2)
    g2 = e2 * g1
    return jnp.where(lane == 0, i1, jnp.where(lane == 1, i2, jnp.where(lane == 2, g1, jnp.where(lane == 3, g2, 0.0))))


def _outproj_kernel(alpha, with_router, x_ref, o0_ref, o1_ref, o2_ref, o3_ref, w_ref, g_ref, b_ref, *rest):
    acc = alpha * x_ref[...]
    for h, ref in enumerate((o0_ref, o1_ref, o2_ref, o3_ref)):
        acc = acc + _dot(ref[...], w_ref[h * GROUP_WIDTH:(h + 1) * GROUP_WIDTH, :])
    y = _layer_norm(acc, g_ref[...], b_ref[...])
    if with_router:
        router_ref, tiles_ref, route_ref = rest
        route_ref[...] = _top2_route(_dot_3pass(y, router_ref[...]))
        _store_token_tiles(tiles_ref, (), y)
    else:
        (y_ref,) = rest
        y_ref[...] = y


def _out_proj_ln(alpha, x, mixes, w_out, g, b, router=None):
    t = x.shape[0]
    row_d = pl.BlockSpec((ROW_TILE, D_MODEL), lambda i: (i, 0))
    row_g = pl.BlockSpec((ROW_TILE, GROUP_WIDTH), lambda i: (i, 0))
    in_specs = [row_d, row_g, row_g, row_g, row_g, _resident((D_MODEL, D_MODEL)),
                _resident((1, D_MODEL)), _resident((1, D_MODEL))]
    args = [x, *mixes, w_out.astype(BF16), g.reshape(1, -1), b.reshape(1, -1)]
    out_specs = row_d
    out_shape = jax.ShapeDtypeStruct((t, D_MODEL), F32)
    if router is not None:
        in_specs.append(_resident((D_MODEL, LANES)))
        args.append(jnp.zeros((D_MODEL, LANES), F32).at[:, :N_EXPERTS].set(router))
        out_specs = [pl.BlockSpec((ROW_TILE * SUBLANES, LANES), lambda i: (i, 0)),
                     pl.BlockSpec((ROW_TILE, LANES), lambda i: (i, 0))]
        out_shape = [jax.ShapeDtypeStruct((t * SUBLANES, LANES), F32), jax.ShapeDtypeStruct((t, LANES), F32)]
    return pl.pallas_call(
        functools.partial(_outproj_kernel, alpha, router is not None),
        grid=(t // ROW_TILE,),
        in_specs=in_specs,
        out_specs=out_specs,
        out_shape=out_shape,
        compiler_params=_params(("parallel",)),
        name="out_proj_ln",
    )(*args)


def _ln_embed(y, ln_g, ln_b, p, ple_gate, ple_proj):
    x = _layer_norm(y, ln_g, ln_b)
    gate = _sigmoid(_dot(x.astype(BF16), ple_gate))
    return x + gate * _dot(p.astype(BF16), ple_proj)


def _dense_ffn_kernel(alpha, x_ref, p_ref, wg_ref, wu_ref, wd_ref, g_ref, b_ref, pg_ref, pp_ref, y_ref, acc_ref):
    x = x_ref[...]
    xb = x.astype(BF16)
    acc_ref[...] = alpha * x
    for j in range(D_FF // FF_CHUNK):
        cols = slice(j * FF_CHUNK, (j + 1) * FF_CHUNK)
        h = _silu(_dot(xb, wg_ref[:, cols])) * _dot(xb, wu_ref[:, cols])
        acc_ref[...] += _dot(h.astype(BF16), wd_ref[cols, :])
    y_ref[...] = _ln_embed(acc_ref[...], g_ref[...], b_ref[...], p_ref[...], pg_ref[...], pp_ref[...])


def _dense_ffn_tail(alpha, x, p, w_gate, w_up, w_down, g, b, ple_gate, ple_proj):
    t = x.shape[0]
    row_d = pl.BlockSpec((ROW_TILE, D_MODEL), lambda i: (i, 0))
    return pl.pallas_call(
        functools.partial(_dense_ffn_kernel, alpha),
        grid=(t // ROW_TILE,),
        in_specs=[row_d, pl.BlockSpec((ROW_TILE, PLE_DIM), lambda i: (i, 0)),
                  _resident((D_MODEL, D_FF)), _resident((D_MODEL, D_FF)), _resident((D_FF, D_MODEL)),
                  _resident((1, D_MODEL)), _resident((1, D_MODEL)),
                  _resident((D_MODEL, D_MODEL)), _resident((PLE_DIM, D_MODEL))],
        out_specs=row_d,
        out_shape=jax.ShapeDtypeStruct((t, D_MODEL), F32),
        scratch_shapes=[pltpu.VMEM((ROW_TILE, D_MODEL), F32)],
        compiler_params=_params(("parallel",)),
        name="dense_ffn_tail",
    )(x, p, w_gate.astype(BF16), w_up.astype(BF16), w_down.astype(BF16), g.reshape(1, -1), b.reshape(1, -1),
      ple_gate.astype(BF16), ple_proj.astype(BF16))


def _load_expert_weights(e, wg_hbm, wu_hbm, wd_hbm, wg_ref, wu_ref, wd_ref, stage_cols_ref, stage_rows_ref, sem):
    c = MOE_WEIGHT_CHUNK
    pieces = []
    for src, dst in ((wg_hbm, wg_ref), (wu_hbm, wu_ref)):
        for lo in range(0, D_FF_EXPERT, c):
            pieces.append((src.at[e, :, lo:lo + c], stage_cols_ref, dst, (slice(None), slice(lo, lo + c))))
    for lo in range(0, D_FF_EXPERT, c):
        pieces.append((wd_hbm.at[e, lo:lo + c, :], stage_rows_ref, wd_ref, (slice(lo, lo + c), slice(None))))
    copies = [pltpu.make_async_copy(src, stage.at[k % 2], sem.at[k % 2])
              for k, (src, stage, _, _) in enumerate(pieces)]
    copies[0].start()
    for k, (_, stage, dst, index) in enumerate(pieces):
        if k + 1 < len(pieces):
            copies[k + 1].start()
        copies[k].wait()
        dst[index] = stage[k % 2].astype(BF16)


def _expert_kernel(row_tok_ref, row_dst_ref, block_e_ref, n_used_ref, x_hbm, wg_hbm, wu_hbm, wd_hbm, y_hbm,
                   rows_ref, xb_ref, acc_ref, ybuf_ref, wg_ref, wu_ref, wd_ref, stage_cols_ref, stage_rows_ref,
                   gather_sem, scatter_sem, weight_sem):
    i = pl.program_id(0)
    n_used = n_used_ref[0]
    last = pl.num_programs(0) - 1
    n = MOE_ROWS
    slot = lax.rem(i, 2)
    other = 1 - slot
    tile = lambda first_row: pl.ds(pl.multiple_of(first_row, SUBLANES), SUBLANES)
    gather_row = lambda tok_row, s, r: pltpu.make_async_copy(
        x_hbm.at[tile(tok_row)], rows_ref.at[s, tile(r * SUBLANES)], gather_sem.at[s])
    scatter_row = lambda dst_row, s, r: pltpu.make_async_copy(
        ybuf_ref.at[s, tile(r * SUBLANES)], y_hbm.at[tile(dst_row)], scatter_sem.at[s])
    block_rows = n * SUBLANES
    gather_block = lambda s: pltpu.make_async_copy(x_hbm.at[pl.ds(0, block_rows)], rows_ref.at[s], gather_sem.at[s])
    scatter_block = lambda s: pltpu.make_async_copy(ybuf_ref.at[s], y_hbm.at[pl.ds(0, block_rows)],
                                                    scatter_sem.at[s])

    @pl.when(i == 0)
    def _():
        ybuf_ref[1] = jnp.zeros((block_rows, LANES), F32)

        def start(r, carry):
            gather_row(row_tok_ref[r], 0, r).start()
            return carry

        lax.fori_loop(0, n, start, 0)

    def scatter_all(first_dst, s):
        def start(r, carry):
            scatter_row(row_dst_ref[first_dst + r], s, r).start()
            return carry

        lax.fori_loop(0, n, start, 0)
        scatter_block(s).wait()

    expert = block_e_ref[i]
    new_expert = (i == 0) | (expert != block_e_ref[jnp.maximum(i - 1, 0)])

    @pl.when((i < n_used) & new_expert)
    def _():
        _load_expert_weights(expert, wg_hbm, wu_hbm, wd_hbm, wg_ref, wu_ref, wd_ref,
                             stage_cols_ref, stage_rows_ref, weight_sem)

    @pl.when(i < n_used)
    def _():
        gather_block(slot).wait()
        xb_ref[...] = _load_token_tiles(rows_ref, (slot,), n).astype(BF16)
        n_chunks = D_FF_EXPERT // FF_CHUNK
        rows_per_chunk = -(-n // (n_chunks - MOE_DMA_FREE_CHUNKS))
        for j in range(n_chunks):
            cols = slice(j * FF_CHUNK, (j + 1) * FF_CHUNK)
            xb = xb_ref[...]
            h = _silu(_dot(xb, wg_ref[:, cols])) * _dot(xb, wu_ref[:, cols])
            part = _dot(h.astype(BF16), wd_ref[cols, :])
            if j == 0:
                acc_ref[...] = part
            else:
                acc_ref[...] += part
            for r in range(j * rows_per_chunk, min((j + 1) * rows_per_chunk, n)):
                gather_row(row_tok_ref[(i + 1) * n + r], other, r).start(priority=GATHER_DMA_PRIORITY)
                scatter_row(row_dst_ref[i * n + r], other, r).start(priority=SCATTER_DMA_PRIORITY)
        _store_token_tiles(ybuf_ref, (slot,), acc_ref[...])
        scatter_block(other).wait()

    @pl.when(i == n_used)
    def _():
        gather_block(slot).wait()
        scatter_all(i * n, other)

    @pl.when(i >= n_used)
    def _():
        rows_ref[slot] = jnp.zeros((block_rows, LANES), F32)
        fill = pltpu.make_async_copy(rows_ref.at[slot], y_hbm.at[pl.ds(i * block_rows, block_rows)],
                                     scatter_sem.at[slot])
        fill.start()
        fill.wait()

    @pl.when((i == last) & (i < n_used))
    def _():
        gather_block(other).wait()
        scatter_all((i + 1) * n, slot)


def _expert_rows(x, row_tok, row_dst, block_e, n_used, n_out_rows, w_gate, w_up, w_down):
    n_blocks = block_e.shape[0]
    hbm = pl.BlockSpec(memory_space=pl.ANY)
    grid_spec = pltpu.PrefetchScalarGridSpec(
        num_scalar_prefetch=4,
        grid=(n_blocks,),
        in_specs=[hbm, hbm, hbm, hbm],
        out_specs=hbm,
        scratch_shapes=[pltpu.VMEM((2, MOE_ROWS * SUBLANES, LANES), F32), pltpu.VMEM((MOE_ROWS, D_MODEL), BF16),
                        pltpu.VMEM((MOE_ROWS, D_MODEL), F32), pltpu.VMEM((2, MOE_ROWS * SUBLANES, LANES), F32),
                        pltpu.VMEM((D_MODEL, D_FF_EXPERT), BF16), pltpu.VMEM((D_MODEL, D_FF_EXPERT), BF16),
                        pltpu.VMEM((D_FF_EXPERT, D_MODEL), BF16),
                        pltpu.VMEM((2, D_MODEL, MOE_WEIGHT_CHUNK), F32),
                        pltpu.VMEM((2, MOE_WEIGHT_CHUNK, D_MODEL), F32),
                        pltpu.SemaphoreType.DMA((2,)), pltpu.SemaphoreType.DMA((2,)),
                        pltpu.SemaphoreType.DMA((2,))],
    )
    return pl.pallas_call(
        _expert_kernel,
        grid_spec=grid_spec,
        out_shape=jax.ShapeDtypeStruct((n_out_rows * SUBLANES, LANES), F32),
        compiler_params=_params(("arbitrary",)),
        name="moe_experts",
    )(row_tok, row_dst, block_e, n_used.reshape(1), x, w_gate, w_up, w_down)


def _combine_kernel(alpha, x_ref, y0_ref, y1_ref, gates_ref, p_ref, g_ref, b_ref, pg_ref, pp_ref, o_ref):
    n = ROW_TILE
    gates = gates_ref[...]
    f = _load_token_tiles(y0_ref, (), n) * gates[:, 0:1] + _load_token_tiles(y1_ref, (), n) * gates[:, 1:2]
    o_ref[...] = _ln_embed(alpha * _load_token_tiles(x_ref, (), n) + f, g_ref[...], b_ref[...], p_ref[...],
                           pg_ref[...], pp_ref[...])


def _moe_combine_tail(alpha, x_tiles, p, y_tiles, gates, g, b, ple_gate, ple_proj):
    t = x_tiles.shape[0] // SUBLANES
    n = ROW_TILE
    row_d = pl.BlockSpec((n, D_MODEL), lambda i: (i, 0))
    tiles_d = pl.BlockSpec((n * SUBLANES, LANES), lambda i: (i, 0))
    return pl.pallas_call(
        functools.partial(_combine_kernel, alpha),
        grid=(t // n,),
        in_specs=[tiles_d, tiles_d, pl.BlockSpec((n * SUBLANES, LANES), lambda i: (t // n + i, 0)),
                  pl.BlockSpec((n, 2), lambda i: (i, 0)), pl.BlockSpec((n, PLE_DIM), lambda i: (i, 0)),
                  _resident((1, D_MODEL)), _resident((1, D_MODEL)),
                  _resident((D_MODEL, D_MODEL)), _resident((PLE_DIM, D_MODEL))],
        out_specs=row_d,
        out_shape=jax.ShapeDtypeStruct((t, D_MODEL), F32),
        compiler_params=_params(("parallel",)),
        name="moe_combine_tail",
    )(x_tiles, y_tiles, y_tiles, gates, p, g.reshape(1, -1), b.reshape(1, -1), ple_gate.astype(BF16),
      ple_proj.astype(BF16))


def _moe_tail(alpha, x_tiles, route, p, w_gate, w_up, w_down, g, b, ple_gate, ple_proj):
    t = route.shape[0]
    experts = route[:, 0:2].astype(jnp.int32)
    gates = route[:, 2:4]
    e_flat = experts.reshape(-1)
    onehot = (e_flat[:, None] == jnp.arange(N_EXPERTS, dtype=jnp.int32)[None, :]).astype(jnp.int32)
    running = jnp.cumsum(onehot, axis=0)
    rank = jnp.sum(onehot * (running - 1), axis=1)
    counts = running[-1]
    padded = (counts + MOE_ROWS - 1) // MOE_ROWS * MOE_ROWS
    pad_end = jnp.cumsum(padded)
    pad_start = pad_end - padded
    dest = (pad_start[e_flat] + rank).astype(jnp.int32)
    n_blocks = (2 * t) // MOE_ROWS + N_EXPERTS
    n_rows = n_blocks * MOE_ROWS
    assign = jnp.arange(2 * t, dtype=jnp.int32)
    row_assign = jnp.full((n_rows,), -1, jnp.int32).at[dest].set(assign)
    used = row_assign >= 0
    spare = 2 * t + jnp.cumsum(jnp.where(used, 0, 1).astype(jnp.int32)) - 1
    row_tok = jnp.where(used, row_assign // 2, 0)
    row_dst = jnp.where(used, (row_assign % 2) * t + row_assign // 2, spare)
    n_spare = n_rows - 2 * t
    first_dst = 2 * t + n_spare + jnp.arange(MOE_ROWS, dtype=jnp.int32)
    row_tok = jnp.concatenate([row_tok, jnp.zeros((MOE_ROWS,), jnp.int32)])
    row_dst = jnp.concatenate([first_dst, row_dst])
    block_start = jnp.arange(n_blocks, dtype=jnp.int32) * MOE_ROWS
    block_e = jnp.minimum(jnp.sum((block_start[:, None] >= pad_end[None, :]).astype(jnp.int32), axis=1),
                          N_EXPERTS - 1)
    n_used = (pad_end[-1] // MOE_ROWS).astype(jnp.int32)
    y_tiles = _expert_rows(x_tiles, row_tok * SUBLANES, row_dst * SUBLANES, block_e, n_used,
                           2 * t + n_spare + MOE_ROWS, w_gate, w_up, w_down)
    return _moe_combine_tail(alpha, x_tiles, p, y_tiles, gates, g, b, ple_gate, ple_proj)


def kernel(x, p, w_in, w_out, gla_gk_up, gla_gk_bias, gla_norm_w, hgrn_lower_bounds, hgrn_norm_w, swa_sinks, rwkv_mu, rwkv_w0, rwkv_w_up, rwkv_a0, rwkv_a_up, rwkv_g_up, rwkv_k_k, rwkv_k_a, rwkv_r_k, rwkv_lnx_w, rwkv_lnx_b, rwkv_vres_down, rwkv_vres_mu, rwkv_v0, rwkv_vres_up, ln1_g, ln1_b, ln2_g, ln2_b, ffn_w_gate, ffn_w_up, ffn_w_down, moe_router, moe_w_gate, moe_w_up, moe_w_down, ple_proj, ple_gate):
    bsz, seq, d = x.shape
    depth = w_in.shape[0]
    alpha = (2.0 * depth) ** 0.25
    lbs = jnp.cumsum(jax.nn.softmax(hgrn_lower_bounds.astype(F32), axis=0), axis=0)
    lbs = lbs - lbs[0]
    outs = []
    for bi in range(bsz):
        xt = x[bi]
        v_first = None
        for i in range(depth):
            w = _group_in_weights(w_in[i], None if i == 0 else rwkv_vres_down[i - 1])
            z_gla, z_hgrn, z_swa, z_rwkv = _in_proj(xt, w)
            o_gla = _gla_mixer(z_gla, gla_gk_up[i], gla_gk_bias[i], gla_norm_w[i])
            o_hgrn = _hgrn_mixer(z_hgrn, lbs[i], hgrn_norm_w[i])
            o_swa = _swa_mixer(z_swa, swa_sinks[i])
            vres = None if i == 0 else (v_first, rwkv_vres_mu[i - 1], rwkv_v0[i - 1], rwkv_vres_up[i - 1])
            rw = _rwkv_mixer(z_rwkv, rwkv_mu[i], rwkv_w0[i], rwkv_w_up[i], rwkv_a0[i], rwkv_a_up[i],
                             rwkv_g_up[i], rwkv_k_k[i], rwkv_k_a[i], rwkv_r_k[i].reshape(-1),
                             rwkv_lnx_w[i], rwkv_lnx_b[i], vres)
            if i == 0:
                o_rwkv, v_first = rw
            else:
                o_rwkv = rw
            mixes = (o_gla, o_hgrn, o_swa, o_rwkv)
            j = i // 2
            if i % 2 == 0:
                xt = _out_proj_ln(alpha, xt, mixes, w_out[i], ln1_g[i], ln1_b[i])
                xt = _dense_ffn_tail(alpha, xt, p[i, bi], ffn_w_gate[j], ffn_w_up[j], ffn_w_down[j],
                                     ln2_g[i], ln2_b[i], ple_gate[i], ple_proj[i])
            else:
                xt, route = _out_proj_ln(alpha, xt, mixes, w_out[i], ln1_g[i], ln1_b[i], moe_router[j])
                xt = _moe_tail(alpha, xt, route, p[i, bi], moe_w_gate[j], moe_w_up[j], moe_w_down[j],
                               ln2_g[i], ln2_b[i], ple_gate[i], ple_proj[i])
        outs.append(xt)
    return jnp.stack(outs, axis=0)
```

```python
import functools

import jax
import jax.numpy as jnp
from jax import lax
from jax.experimental import pallas as pl
from jax.experimental.pallas import tpu as pltpu

F32 = jnp.float32
BF16 = jnp.bfloat16
HIGHEST = lax.Precision.HIGHEST
MIX_DTYPE = BF16

D_MODEL = 1024
GROUP_WIDTH = 256
N_HEADS = 4
HEAD_DIM = 64
GLA_DK = 32
GLA_GATE_RANK = 16
GLA_GATE_NORMALIZER = 16.0
SWA_WINDOW = 128
RWKV_COLS = 3 * GROUP_WIDTH + 16 + 16 + 32
RWKV_V_RANK = 8
LN_EPS = 1e-5
RMS_EPS = 1e-6
RWKV_GN_EPS = 64e-5
D_FF = 2816
N_EXPERTS = 8
D_FF_EXPERT = 3584
PLE_DIM = 256

LANES = 128
SUBLANES = 8
GLA_W = 896
HGRN_W = 1024
SWA_W = 512
RWKV_W = 896
Z_W = GLA_W + HGRN_W + SWA_W + RWKV_W

GLA_SUB = 16
GLA_TILE = 128
GLA_STEP = 1024
GLA_STAGGER = 1
SWA_STAGGER = 1
RWKV_CHUNK = 64
RWKV_TILE = 1024
RWKV_STAGGER = 2
ROW_TILE = 512
FF_CHUNK = 256
SWA_TILE = 256
MOE_ROWS = 512
MOE_WEIGHT_CHUNK = 512
MOE_DMA_FREE_CHUNKS = 10
GATHER_DMA_PRIORITY = 0
SCATTER_DMA_PRIORITY = 1
VMEM_LIMIT = 56 * 1024 * 1024


def _iota(shape, dim):
    return lax.broadcasted_iota(jnp.int32, shape, dim)


def _idiv(x, n):
    return jnp.right_shift(x, n.bit_length() - 1)


def _imod(x, n):
    return jnp.bitwise_and(x, n - 1)


def _dot(a, b, precision=None):
    return jnp.dot(a, b, preferred_element_type=F32, precision=precision)


def _dot_nt(a, b, precision=None):
    return lax.dot_general(a, b, (((1,), (1,)), ((), ())), preferred_element_type=F32, precision=precision)


def _dot_tn(a, b, precision=None):
    return lax.dot_general(a, b, (((0,), (0,)), ((), ())), preferred_element_type=F32, precision=precision)


def _bdot(a, b):
    return _dot(a.astype(BF16), b.astype(BF16))


def _bdot_nt(a, b):
    return _dot_nt(a.astype(BF16), b.astype(BF16))


def _dot_hilo(x, m):
    hi = x.astype(BF16)
    lo = (x - hi.astype(F32)).astype(BF16)
    return _dot(hi, m) + _dot(lo, m)


def _dot_mask(m, x):
    m = m.astype(BF16)
    x1 = x.astype(BF16)
    r1 = x - x1.astype(F32)
    x2 = r1.astype(BF16)
    x3 = (r1 - x2.astype(F32)).astype(BF16)
    return _dot(m, x1) + _dot(m, x2) + _dot(m, x3)


def _dot_3pass_nt(x, w):
    x_hi = x.astype(BF16)
    x_lo = (x - x_hi.astype(F32)).astype(BF16)
    w_hi = w.astype(BF16)
    w_lo = (w - w_hi.astype(F32)).astype(BF16)
    return _dot_nt(x_hi, w_hi) + _dot_nt(x_lo, w_hi) + _dot_nt(x_hi, w_lo)


def _dot_3pass(x, w):
    x_hi = x.astype(BF16)
    x_lo = (x - x_hi.astype(F32)).astype(BF16)
    w_hi = w.astype(BF16)
    w_lo = (w - w_hi.astype(F32)).astype(BF16)
    return _dot(x_hi, w_hi) + _dot(x_lo, w_hi) + _dot(x_hi, w_lo)


def _sigmoid(x):
    return 1.0 / (1.0 + jnp.exp(-x))


def _silu(x):
    return x * _sigmoid(x)


def _log_sigmoid(x):
    return jnp.minimum(x, 0.0) - jnp.log1p(jnp.exp(-jnp.abs(x)))


def _layer_norm(y, g, b):
    mu = jnp.mean(y, axis=-1, keepdims=True)
    d = y - mu
    var = jnp.mean(d * d, axis=-1, keepdims=True)
    return d * lax.rsqrt(var + LN_EPS) * g + b


def _expand_heads(x, head_width):
    lane_head = _idiv(_iota(x.shape, 1), head_width)
    return jnp.concatenate([jnp.where(lane_head == h, x, 0.0) for h in range(N_HEADS)], axis=0)


def _head_group_matrix(width, head_width, value):
    same = _idiv(_iota((width, width), 0), head_width) == _idiv(_iota((width, width), 1), head_width)
    return jnp.where(same, value, 0.0).astype(F32)


def _resident(shape):
    nd = len(shape)
    return pl.BlockSpec(shape, lambda *_: (0,) * nd, pipeline_mode=pl.Buffered(1))


def _params(semantics):
    return pltpu.CompilerParams(dimension_semantics=semantics, vmem_limit_bytes=VMEM_LIMIT)


def _inproj_kernel(x_ref, w_ref, gla_ref, hgrn_ref, swa_ref, rwkv_ref):
    xb = x_ref[...].astype(BF16)
    o = 0
    for ref, width in ((gla_ref, GLA_W), (hgrn_ref, HGRN_W), (swa_ref, SWA_W), (rwkv_ref, RWKV_W)):
        ref[...] = _dot(xb, w_ref[:, o:o + width])
        o += width


def _in_proj(x, w):
    t = x.shape[0]
    widths = (GLA_W, HGRN_W, SWA_W, RWKV_W)
    return pl.pallas_call(
        _inproj_kernel,
        grid=(t // ROW_TILE,),
        in_specs=[pl.BlockSpec((ROW_TILE, D_MODEL), lambda i: (i, 0)), _resident((D_MODEL, Z_W))],
        out_specs=[pl.BlockSpec((ROW_TILE, w_), lambda i: (i, 0)) for w_ in widths],
        out_shape=[jax.ShapeDtypeStruct((t, w_), F32) for w_ in widths],
        compiler_params=_params(("parallel",)),
        name="in_proj",
    )(x, w)


def _group_in_weights(w_in, vres_down):
    gla, hgrn, swa, rwkv = jnp.split(w_in, (784, 784 + 1024, 784 + 1024 + 512), axis=1)
    if vres_down is not None:
        rwkv = jnp.concatenate([rwkv, vres_down], axis=1)
    pad = lambda a, w_: jnp.pad(a, ((0, 0), (0, w_ - a.shape[1])))
    return jnp.concatenate([pad(gla, GLA_W), hgrn, swa, pad(rwkv, RWKV_W)], axis=1).astype(BF16)


def _gated_linear_attention_tile(index, q, k, v, log_f, state_box, out_box):
    length, kw = q.shape
    head_k = kw // N_HEADS
    n_sub = length // GLA_SUB
    row = _iota((length, length), 0)
    col = _iota((length, length), 1)
    same_sub = _idiv(row, GLA_SUB) == _idiv(col, GLA_SUB)
    m_local = jnp.where(same_sub & (col <= row), 1.0, 0.0).astype(F32)
    m_prev = jnp.where(_idiv(col, GLA_SUB) < _idiv(row, GLA_SUB), 1.0, 0.0).astype(F32)
    b_local = _dot_mask(m_local, log_f)
    b_start = _dot_mask(m_prev, log_f)
    yield
    b_full = b_start + b_local
    q_local = q * jnp.exp(b_local)

    q_pos = _imod(_iota((N_HEADS * GLA_SUB, length), 0), GLA_SUB)
    s_pos = _iota((N_HEADS * GLA_SUB, length), 1)
    probs = []
    for c in range(n_sub):
        r0, r1 = c * GLA_SUB, (c + 1) * GLA_SUB
        expo = jnp.where(_iota((r1, kw), 0) < r0, b_start[r0:r0 + 1, :] - b_full[0:r1], -b_local[0:r1])
        k_ref = k[0:r1] * jnp.exp(expo)
        if r1 < length:
            k_ref = jnp.concatenate([k_ref, jnp.zeros((length - r1, kw), F32)], axis=0)
        q_heads = _expand_heads(q_local[r0:r1, :], head_k)
        s = _bdot_nt(q_heads, k_ref)
        probs.append(jnp.where(s_pos <= q_pos + r0, s, 0.0))
    b_total = b_full[length - 1:length, :]
    k_end = k * jnp.exp(b_total - b_full)
    upd = _dot_tn(v.astype(BF16), k_end.astype(BF16))
    yield
    o_heads = _bdot(jnp.concatenate(probs, axis=0), v)
    assert len(out_box) == index, "the previous tile must have replaced the state before it is read"
    state_t = state_box[0]
    o_state = _bdot_nt(q * jnp.exp(b_full), state_t)
    same_head = _idiv(_iota((GROUP_WIDTH, kw), 0), HEAD_DIM) == _idiv(_iota((GROUP_WIDTH, kw), 1), head_k)
    state_box[0] = state_t * jnp.exp(b_total) + jnp.where(same_head, upd, 0.0)
    yield
    v_head = _idiv(_iota((GLA_SUB, GROUP_WIDTH), 1), HEAD_DIM)
    rows = []
    for c in range(n_sub):
        base = c * N_HEADS * GLA_SUB
        acc = jnp.zeros((GLA_SUB, GROUP_WIDTH), F32)
        for h in range(N_HEADS):
            acc = acc + jnp.where(v_head == h, o_heads[base + h * GLA_SUB:base + (h + 1) * GLA_SUB, :], 0.0)
        rows.append(acc)
    out_box.append(jnp.concatenate(rows, axis=0) + o_state)


def _gated_linear_attention(q, k, v, log_f, state_ref):
    state_box = [state_ref[...]]
    tiles = []
    stages = []
    for n in range(q.shape[0] // GLA_TILE):
        rows = slice(n * GLA_TILE, (n + 1) * GLA_TILE)
        stages.append(_gated_linear_attention_tile(n, q[rows], k[rows], v[rows], log_f[rows], state_box, tiles))
    _run_staggered(stages, GLA_STAGGER)
    state_ref[...] = state_box[0]
    return jnp.concatenate(tiles, axis=0)


def _head_rms_gate(o, norm_w, gate):
    ms = _dot_hilo(o * o, _head_group_matrix(GROUP_WIDTH, HEAD_DIM, 1.0 / HEAD_DIM).astype(BF16))
    return o * lax.rsqrt(ms + RMS_EPS) * norm_w * _silu(gate)


def _gla_kernel(z_ref, gk_up_ref, gk_bias_ref, norm_w_ref, o_ref, state_ref):
    @pl.when(pl.program_id(0) == 0)
    def _():
        state_ref[...] = jnp.zeros_like(state_ref)

    z = z_ref[...]
    q = z[:, 0:128] * (GLA_DK ** -0.5)
    k = z[:, 128:256]
    v = z[:, 256:512]
    g = z[:, 512:768]
    gate_in = _dot_3pass(z[:, 768:896], gk_up_ref[...]) + gk_bias_ref[...]
    log_f = _log_sigmoid(gate_in) * (1.0 / GLA_GATE_NORMALIZER)
    o = _gated_linear_attention(q, k, v, log_f, state_ref)
    o_ref[...] = _head_rms_gate(o, norm_w_ref[...], g).astype(o_ref.dtype)


def _hgrn_kernel(z_ref, lb_ref, log_lb_ref, norm_w_ref, o_ref, state_ref):
    @pl.when(pl.program_id(0) == 0)
    def _():
        state_ref[...] = jnp.zeros_like(state_ref)

    z = z_ref[...]
    q = _silu(z[:, 0:256])
    f = z[:, 256:512]
    v = z[:, 512:768]
    g = z[:, 768:1024]
    lb = lb_ref[...]
    a = log_lb_ref[...]
    c = jnp.log1p(-lb) + _log_sigmoid(f)
    log_f = jnp.maximum(a, c) + jnp.log1p(jnp.exp(-jnp.abs(a - c)))
    k = (1.0 - lb) * _sigmoid(-f)
    o = _gated_linear_attention(q, k, v, log_f, state_ref)
    o_ref[...] = _head_rms_gate(o, norm_w_ref[...], g).astype(o_ref.dtype)


def _gla_mixer(z, gk_up, gk_bias, norm_w):
    t = z.shape[0]
    gk_up_pad = jnp.zeros((LANES, N_HEADS * GLA_DK), F32).at[:GLA_GATE_RANK].set(gk_up)
    return pl.pallas_call(
        _gla_kernel,
        grid=(t // GLA_STEP,),
        in_specs=[pl.BlockSpec((GLA_STEP, GLA_W), lambda i: (i, 0)),
                  _resident((LANES, N_HEADS * GLA_DK)), _resident((1, N_HEADS * GLA_DK)),
                  _resident((1, GROUP_WIDTH))],
        out_specs=pl.BlockSpec((GLA_STEP, GROUP_WIDTH), lambda i: (i, 0)),
        out_shape=jax.ShapeDtypeStruct((t, GROUP_WIDTH), MIX_DTYPE),
        scratch_shapes=[pltpu.VMEM((GROUP_WIDTH, N_HEADS * GLA_DK), F32)],
        compiler_params=_params(("arbitrary",)),
        name="gla_mixer",
    )(z, gk_up_pad, gk_bias.reshape(1, -1), jnp.tile(norm_w, N_HEADS).reshape(1, -1))


def _hgrn_mixer(z, lb, norm_w):
    t = z.shape[0]
    return pl.pallas_call(
        _hgrn_kernel,
        grid=(t // GLA_STEP,),
        in_specs=[pl.BlockSpec((GLA_STEP, HGRN_W), lambda i: (i, 0)),
                  _resident((1, GROUP_WIDTH)), _resident((1, GROUP_WIDTH)), _resident((1, GROUP_WIDTH))],
        out_specs=pl.BlockSpec((GLA_STEP, GROUP_WIDTH), lambda i: (i, 0)),
        out_shape=jax.ShapeDtypeStruct((t, GROUP_WIDTH), MIX_DTYPE),
        scratch_shapes=[pltpu.VMEM((GROUP_WIDTH, GROUP_WIDTH), F32)],
        compiler_params=_params(("arbitrary",)),
        name="hgrn_mixer",
    )(z, lb.reshape(1, -1), jnp.log(lb).reshape(1, -1), jnp.tile(norm_w, N_HEADS).reshape(1, -1))


def _swa_block(q, kw, vw, visible, sinks, out_box):
    n = q.shape[0]
    head_cols = lambda x, h: x[:, h * HEAD_DIM:(h + 1) * HEAD_DIM]
    v_lane_head = _idiv(_iota(vw.shape, 1), HEAD_DIM)
    outs = []
    for kv in range(N_HEADS // 2):
        heads = (2 * kv, 2 * kv + 1)
        scores = _bdot_nt(jnp.concatenate([head_cols(q, h) for h in heads], axis=0), head_cols(kw, kv))
        yield
        probs, sink_terms = [], []
        for half, h in enumerate(heads):
            s = jnp.where(visible, scores[half * n:(half + 1) * n], -jnp.inf)
            sink = sinks[:, h:h + 1]
            m = jnp.maximum(jnp.max(s, axis=-1, keepdims=True), sink)
            probs.append(jnp.exp(s - m))
            sink_terms.append(jnp.exp(sink - m))
        o = _bdot(jnp.concatenate(probs, axis=0), jnp.where(v_lane_head == kv, vw, 1.0))
        yield
        sums = head_cols(o, 1 - kv)[:, 0:1]
        outs += [head_cols(o, kv)[half * n:(half + 1) * n] / (sums[half * n:(half + 1) * n] + sink_terms[half])
                 for half in range(2)]
    out_box.append(jnp.concatenate(outs, axis=-1))


def _swa_kernel(q_ref, k_ref, v_ref, kp_ref, vp_ref, sink_ref, o_ref):
    w = SWA_WINDOW
    has_prev = pl.program_id(0) > 0
    q = q_ref[...] * (HEAD_DIM ** -0.5)
    k_all = jnp.concatenate([kp_ref[...], k_ref[...]], axis=0)
    v_all = jnp.concatenate([vp_ref[...], v_ref[...]], axis=0)
    q_pos = _iota((w, 2 * w), 0) + w
    k_pos = _iota((w, 2 * w), 1)
    dist = q_pos - k_pos
    in_window = (dist >= 0) & (dist < w)
    sinks = sink_ref[...]
    blocks = []
    stages = []
    for b in range(q.shape[0] // w):
        visible = in_window if b > 0 else in_window & ((k_pos >= w) | has_prev)
        stages.append(_swa_block(q[b * w:(b + 1) * w], k_all[b * w:(b + 2) * w], v_all[b * w:(b + 2) * w],
                                 visible, sinks, blocks))
    _run_staggered(stages, SWA_STAGGER)
    o_ref[...] = jnp.concatenate(blocks, axis=0).astype(o_ref.dtype)


def _swa_mixer(z, sinks):
    t = z.shape[0]
    w = SWA_WINDOW
    n = SWA_TILE
    prev = lambda col: (lambda i: (jnp.maximum(i * (n // w) - 1, 0), col))
    return pl.pallas_call(
        _swa_kernel,
        grid=(t // n,),
        in_specs=[pl.BlockSpec((n, 256), lambda i: (i, 0)),
                  pl.BlockSpec((n, 128), lambda i: (i, 2)), pl.BlockSpec((n, 128), lambda i: (i, 3)),
                  pl.BlockSpec((w, 128), prev(2)), pl.BlockSpec((w, 128), prev(3)),
                  _resident((1, N_HEADS))],
        out_specs=pl.BlockSpec((n, GROUP_WIDTH), lambda i: (i, 0)),
        out_shape=jax.ShapeDtypeStruct((t, GROUP_WIDTH), MIX_DTYPE),
        compiler_params=_params(("parallel",)),
        name="swa_mixer",
    )(z, z, z, z, z, sinks.reshape(1, -1))


def _run_staggered(stage_generators, stagger):
    live = dict(enumerate(stage_generators))
    rnd = 0
    while live:
        for n in sorted(live):
            if rnd >= n * stagger:
                try:
                    next(live[n])
                except StopIteration:
                    del live[n]
        rnd += 1


def _rwkv_chunk(index, r, k, v, a_vec, b_vec, log_w, state_box, out_box):
    c = r.shape[0]
    width = r.shape[1]
    tri = jnp.where(_iota((c, c), 1) <= _iota((c, c), 0), 1.0, 0.0).astype(F32)
    p = _dot_mask(tri, log_w)
    yield
    p_total = p[c - 1:c, :]
    decay_in = jnp.exp(p)
    decay_out = jnp.exp(-p)
    decay_end = jnp.exp(p_total - p)
    a_in = a_vec * jnp.exp(p - log_w)
    r_in = r * decay_in
    b_out = b_vec * decay_out
    k_out = k * decay_out
    b_end = b_vec * decay_end
    k_end = k * decay_end

    t_pos = _iota((c, width), 0)
    assert width == N_HEADS * c
    s_pos = _imod(_iota((c, width), 1), c)
    strict = s_pos < t_pos
    incl = s_pos <= t_pos
    expand = lambda x: _expand_heads(x, HEAD_DIM)

    scores = _bdot_nt(jnp.concatenate([a_in, r_in], axis=0),
                      jnp.concatenate([expand(b_out), expand(k_out)], axis=0))
    yield
    a_ab = jnp.where(strict, scores[0:c, 0:width], 0.0)
    a_ak = jnp.where(strict, scores[0:c, width:2 * width], 0.0)
    a_rb = jnp.where(incl, scores[c:2 * c, 0:width], 0.0)
    a_rk = jnp.where(incl, scores[c:2 * c, width:2 * width], 0.0)

    t_inv = jnp.where(s_pos == t_pos, 1.0, 0.0) + a_ab
    v_heads = expand(v)
    x1 = _bdot(a_ak, v_heads)
    power = _bdot(a_ab, expand(a_ab))
    yield
    n_factors = (c - 1).bit_length()
    for _ in range(n_factors - 2):
        power_heads = expand(power)
        t_inv = t_inv + _bdot(t_inv, power_heads)
        power = _bdot(power, power_heads)
        yield
    t_inv = t_inv + _bdot(t_inv, expand(power))
    yield
    sol = _bdot(t_inv, jnp.concatenate([expand(x1), expand(a_in)], axis=1))
    yield
    u0 = sol[:, 0:width]
    w_mat = sol[:, width:2 * width]

    assert len(out_box) == index, "the previous chunk must have replaced the state before it is read"
    state = state_box[0]
    from_state = _bdot_nt(jnp.concatenate([w_mat, r_in], axis=0), state)
    yield
    u = u0 + from_state[0:c]
    y = _bdot(a_rb, expand(u)) + _bdot(a_rk, v_heads) + from_state[c:2 * c]
    upd = _dot_tn(jnp.concatenate([u, v], axis=0).astype(BF16),
                  jnp.concatenate([b_end, k_end], axis=0).astype(BF16))
    same_head = _idiv(_iota((width, width), 0), HEAD_DIM) == _idiv(_iota((width, width), 1), HEAD_DIM)
    state_box[0] = state * jnp.exp(p_total) + jnp.where(same_head, upd, 0.0)
    out_box.append(y)


def _rwkv_kernel(has_vres, *refs):
    if has_vres:
        (z_ref, zp_ref, vfirst_ref, mu_ref, w0_ref, wup_ref, a0_ref, aup_ref, gup_ref, kk_ref, ka_ref,
         rk_ref, lnw_ref, lnb_ref, v0_ref, vup_ref, o_ref, state_ref) = refs
    else:
        (z_ref, zp_ref, mu_ref, w0_ref, wup_ref, a0_ref, aup_ref, gup_ref, kk_ref, ka_ref,
         rk_ref, lnw_ref, lnb_ref, o_ref, vout_ref, state_ref) = refs
    step = pl.program_id(0)

    @pl.when(step == 0)
    def _():
        state_ref[...] = jnp.zeros_like(state_ref)

    z = z_ref[...]
    last_prev = jnp.where(step > 0, zp_ref[7:8, :], 0.0)
    prev = jnp.where(_iota(z.shape, 0) == 0, last_prev, pltpu.roll(z, 1, axis=0))
    zr = z + (prev - z) * mu_ref[...]
    r = zr[:, 0:256]
    k = zr[:, 256:512]
    v = zr[:, 512:768]
    low = zr[:, 768:896]
    w_pre = w0_ref[...] + _dot_3pass(jnp.tanh(low), wup_ref[...])
    w_log = -(jnp.maximum(-w_pre, 0.0) + jnp.log1p(jnp.exp(-jnp.abs(w_pre)))) - 0.5
    log_w = -jnp.exp(w_log)
    a = _sigmoid(a0_ref[...] + _dot_3pass(low, aup_ref[...]))
    g = _dot_3pass(_sigmoid(low), gup_ref[...])
    if has_vres:
        v = v + (vfirst_ref[...] - v) * _sigmoid(v0_ref[...] + _dot_3pass(low, vup_ref[...]))
    else:
        vout_ref[...] = v
    head_sum = _head_group_matrix(GROUP_WIDTH, HEAD_DIM, 1.0).astype(BF16)
    kk = k * kk_ref[...]
    kk = kk / jnp.maximum(jnp.sqrt(_dot_hilo(kk * kk, head_sum)), 1e-12)
    k = k * (1.0 + (a - 1.0) * ka_ref[...])
    a_vec = -kk
    b_vec = kk * a

    c = RWKV_CHUNK
    state_box = [state_ref[...]]
    chunks = []
    stages = []
    for n in range(z.shape[0] // c):
        rows = slice(n * c, (n + 1) * c)
        stages.append(_rwkv_chunk(n, r[rows], k[rows], v[rows], a_vec[rows], b_vec[rows], log_w[rows],
                                  state_box, chunks))
    _run_staggered(stages, RWKV_STAGGER)
    state_ref[...] = state_box[0]
    y = jnp.concatenate(chunks, axis=0)

    head_mean = _head_group_matrix(GROUP_WIDTH, HEAD_DIM, 1.0 / HEAD_DIM).astype(BF16)
    mu_y = _dot_hilo(y, head_mean)
    d = y - mu_y
    var_y = _dot_hilo(d * d, head_mean)
    y = d * lax.rsqrt(var_y + RWKV_GN_EPS) * lnw_ref[...] + lnb_ref[...]
    bonus = _dot_hilo(r * k * rk_ref[...], head_sum) * v
    o_ref[...] = ((y + bonus) * g).astype(o_ref.dtype)


def _rwkv_mixer(z, mu, w0, w_up, a0, a_up, g_up, k_k, k_a, r_k, lnx_w, lnx_b, vres):
    t = z.shape[0]
    c = RWKV_TILE
    row = lambda a: a.reshape(1, -1)
    low_rows = lambda a, start: jnp.zeros((LANES, GROUP_WIDTH), F32).at[start:start + a.shape[0]].set(a)
    has_vres = vres is not None
    mu_full = jnp.zeros((RWKV_W,), F32).at[:RWKV_COLS].set(mu)
    vec = _resident((1, GROUP_WIDTH))
    mat = _resident((LANES, GROUP_WIDTH))
    tile = pl.BlockSpec((c, GROUP_WIDTH), lambda i: (i, 0))
    z_specs = [pl.BlockSpec((c, RWKV_W), lambda i: (i, 0)),
               pl.BlockSpec((8, RWKV_W), lambda i: (jnp.maximum(i * (c // 8) - 1, 0), 0))]
    common = [row(w0), low_rows(w_up, 0), row(a0), low_rows(a_up, 16), low_rows(g_up, 32),
              row(k_k), row(k_a), row(r_k), row(lnx_w), row(lnx_b)]
    common_specs = [vec, mat, vec, mat, mat, vec, vec, vec, vec, vec]
    if has_vres:
        v_first, vres_mu, v0, v_up = vres
        mu_full = mu_full.at[RWKV_COLS:RWKV_COLS + RWKV_V_RANK].set(vres_mu)
        args = [z, z, v_first, row(mu_full)] + common + [row(v0), low_rows(v_up, 64)]
        in_specs = z_specs + [tile, _resident((1, RWKV_W))] + common_specs + [vec, mat]
        out_specs = tile
        out_shape = jax.ShapeDtypeStruct((t, GROUP_WIDTH), MIX_DTYPE)
    else:
        args = [z, z, row(mu_full)] + common
        in_specs = z_specs + [_resident((1, RWKV_W))] + common_specs
        out_specs = [tile, tile]
        out_shape = [jax.ShapeDtypeStruct((t, GROUP_WIDTH), MIX_DTYPE), jax.ShapeDtypeStruct((t, GROUP_WIDTH), F32)]
    return pl.pallas_call(
        functools.partial(_rwkv_kernel, has_vres),
        grid=(t // c,),
        in_specs=in_specs,
        out_specs=out_specs,
        out_shape=out_shape,
        scratch_shapes=[pltpu.VMEM((GROUP_WIDTH, GROUP_WIDTH), F32)],
        compiler_params=_params(("arbitrary",)),
        name="rwkv_mixer",
    )(*args)


def _store_token_tiles(ref, index, x):
    n = x.shape[0]
    for j in range(SUBLANES):
        ref[(*index, pl.ds(j, n, stride=SUBLANES), slice(None))] = x[:, j * LANES:(j + 1) * LANES]


def _load_token_tiles(ref, index, n):
    return jnp.concatenate([ref[(*index, pl.ds(j, n, stride=SUBLANES), slice(None))] for j in range(SUBLANES)],
                           axis=-1)


def _top2_route(logits):
    row = _iota(logits.shape, 0).astype(F32)
    m1 = jnp.max(logits, axis=0, keepdims=True)
    i1 = jnp.min(jnp.where(logits == m1, row, N_EXPERTS), axis=0, keepdims=True)
    rest = jnp.where(row == i1, -jnp.inf, logits)
    m2 = jnp.max(rest, axis=0, keepdims=True)
    i2 = jnp.min(jnp.where(rest == m2, row, N_EXPERTS), axis=0, keepdims=True)
    e2 = jnp.exp(m2 - m1)
    g1 = 1.0 / (1.0 + e2)
    g2 = e2 * g1
    return jnp.where(row == 0, i1, jnp.where(row == 1, i2, jnp.where(row == 2, g1, jnp.where(row == 3, g2, 0.0))))


def _outproj_kernel(alpha, with_router, x_ref, o0_ref, o1_ref, o2_ref, o3_ref, w_ref, g_ref, b_ref, *rest):
    acc = alpha * x_ref[...]
    for h, ref in enumerate((o0_ref, o1_ref, o2_ref, o3_ref)):
        acc = acc + _dot(ref[...], w_ref[h * GROUP_WIDTH:(h + 1) * GROUP_WIDTH, :])
    y = _layer_norm(acc, g_ref[...], b_ref[...])
    if with_router:
        router_ref, tiles_ref, route_ref = rest
        route_ref[...] = _top2_route(_dot_3pass_nt(router_ref[...], y))
        _store_token_tiles(tiles_ref, (), y)
    else:
        (y_ref,) = rest
        y_ref[...] = y


def _out_proj_ln(alpha, x, mixes, w_out, g, b, router=None):
    t = x.shape[0]
    row_d = pl.BlockSpec((ROW_TILE, D_MODEL), lambda i: (i, 0))
    row_g = pl.BlockSpec((ROW_TILE, GROUP_WIDTH), lambda i: (i, 0))
    in_specs = [row_d, row_g, row_g, row_g, row_g, _resident((D_MODEL, D_MODEL)),
                _resident((1, D_MODEL)), _resident((1, D_MODEL))]
    args = [x, *mixes, w_out.astype(BF16), g.reshape(1, -1), b.reshape(1, -1)]
    out_specs = row_d
    out_shape = jax.ShapeDtypeStruct((t, D_MODEL), F32)
    if router is not None:
        in_specs.append(_resident((N_EXPERTS, D_MODEL)))
        args.append(router.T)
        out_specs = [pl.BlockSpec((ROW_TILE * SUBLANES, LANES), lambda i: (i, 0)),
                     pl.BlockSpec((N_EXPERTS, ROW_TILE), lambda i: (0, i))]
        out_shape = [jax.ShapeDtypeStruct((t * SUBLANES, LANES), F32), jax.ShapeDtypeStruct((N_EXPERTS, t), F32)]
    return pl.pallas_call(
        functools.partial(_outproj_kernel, alpha, router is not None),
        grid=(t // ROW_TILE,),
        in_specs=in_specs,
        out_specs=out_specs,
        out_shape=out_shape,
        compiler_params=_params(("parallel",)),
        name="out_proj_ln",
    )(*args)


def _ln_embed(y, ln_g, ln_b, p, ple_gate, ple_proj):
    x = _layer_norm(y, ln_g, ln_b)
    gate = _sigmoid(_dot(x.astype(BF16), ple_gate))
    return x + gate * _dot(p.astype(BF16), ple_proj)


def _dense_ffn_kernel(alpha, x_ref, p_ref, wg_ref, wu_ref, wd_ref, g_ref, b_ref, pg_ref, pp_ref, y_ref, acc_ref):
    x = x_ref[...]
    xb = x.astype(BF16)
    acc_ref[...] = alpha * x
    for j in range(D_FF // FF_CHUNK):
        cols = slice(j * FF_CHUNK, (j + 1) * FF_CHUNK)
        h = _silu(_dot(xb, wg_ref[:, cols])) * _dot(xb, wu_ref[:, cols])
        acc_ref[...] += _dot(h.astype(BF16), wd_ref[cols, :])
    y_ref[...] = _ln_embed(acc_ref[...], g_ref[...], b_ref[...], p_ref[...], pg_ref[...], pp_ref[...])


def _dense_ffn_tail(alpha, x, p, w_gate, w_up, w_down, g, b, ple_gate, ple_proj):
    t = x.shape[0]
    row_d = pl.BlockSpec((ROW_TILE, D_MODEL), lambda i: (i, 0))
    return pl.pallas_call(
        functools.partial(_dense_ffn_kernel, alpha),
        grid=(t // ROW_TILE,),
        in_specs=[row_d, pl.BlockSpec((ROW_TILE, PLE_DIM), lambda i: (i, 0)),
                  _resident((D_MODEL, D_FF)), _resident((D_MODEL, D_FF)), _resident((D_FF, D_MODEL)),
                  _resident((1, D_MODEL)), _resident((1, D_MODEL)),
                  _resident((D_MODEL, D_MODEL)), _resident((PLE_DIM, D_MODEL))],
        out_specs=row_d,
        out_shape=jax.ShapeDtypeStruct((t, D_MODEL), F32),
        scratch_shapes=[pltpu.VMEM((ROW_TILE, D_MODEL), F32)],
        compiler_params=_params(("parallel",)),
        name="dense_ffn_tail",
    )(x, p, w_gate.astype(BF16), w_up.astype(BF16), w_down.astype(BF16), g.reshape(1, -1), b.reshape(1, -1),
      ple_gate.astype(BF16), ple_proj.astype(BF16))


def _weight_group_copies(e, group, wg_hbm, wu_hbm, wd_hbm, stage_cols_ref, stage_rows_ref, sem):
    slot = group % 2
    lo, hi = group * MOE_WEIGHT_CHUNK, (group + 1) * MOE_WEIGHT_CHUNK
    return (pltpu.make_async_copy(wg_hbm.at[e, :, lo:hi], stage_cols_ref.at[slot, 0], sem.at[slot]),
            pltpu.make_async_copy(wu_hbm.at[e, :, lo:hi], stage_cols_ref.at[slot, 1], sem.at[slot]),
            pltpu.make_async_copy(wd_hbm.at[e, lo:hi, :], stage_rows_ref.at[slot], sem.at[slot]))


def _expert_kernel(row_tok_ref, row_dst_ref, block_e_ref, n_used_ref, x_hbm, wg_hbm, wu_hbm, wd_hbm, y_hbm,
                   rows_ref, xb_ref, acc_ref, ybuf_ref, wg_ref, wu_ref, wd_ref, stage_cols_ref, stage_rows_ref,
                   gather_sem, scatter_sem, weight_sem):
    i = pl.program_id(0)
    n_used = n_used_ref[0]
    last = pl.num_programs(0) - 1
    n = MOE_ROWS
    slot = lax.rem(i, 2)
    other = 1 - slot
    tile = lambda first_row: pl.ds(pl.multiple_of(first_row, SUBLANES), SUBLANES)
    gather_row = lambda tok_row, s, r: pltpu.make_async_copy(
        x_hbm.at[tile(tok_row)], rows_ref.at[s, tile(r * SUBLANES)], gather_sem.at[s])
    scatter_row = lambda dst_row, s, r: pltpu.make_async_copy(
        ybuf_ref.at[s, tile(r * SUBLANES)], y_hbm.at[tile(dst_row)], scatter_sem.at[s])
    block_rows = n * SUBLANES
    gather_block = lambda s: pltpu.make_async_copy(x_hbm.at[pl.ds(0, block_rows)], rows_ref.at[s], gather_sem.at[s])
    scatter_block = lambda s: pltpu.make_async_copy(ybuf_ref.at[s], y_hbm.at[pl.ds(0, block_rows)],
                                                    scatter_sem.at[s])

    @pl.when(i == 0)
    def _():
        ybuf_ref[1] = jnp.zeros((block_rows, LANES), F32)

        def start(r, carry):
            gather_row(row_tok_ref[r], 0, r).start()
            return carry

        lax.fori_loop(0, n, start, 0)

    def scatter_all(first_dst, s):
        def start(r, carry):
            scatter_row(row_dst_ref[first_dst + r], s, r).start()
            return carry

        lax.fori_loop(0, n, start, 0)
        scatter_block(s).wait()

    expert = block_e_ref[i]
    new_expert = (i == 0) | (expert != block_e_ref[jnp.maximum(i - 1, 0)])

    weight_group = lambda g: _weight_group_copies(expert, g, wg_hbm, wu_hbm, wd_hbm, stage_cols_ref,
                                                  stage_rows_ref, weight_sem)
    chunks_per_group = MOE_WEIGHT_CHUNK // FF_CHUNK
    n_groups = D_FF_EXPERT // MOE_WEIGHT_CHUNK

    def take_weight_group(g):
        if g + 1 < n_groups:
            for copy in weight_group(g + 1):
                copy.start()
        for copy in weight_group(g):
            copy.wait()
        piece = slice(g * MOE_WEIGHT_CHUNK, (g + 1) * MOE_WEIGHT_CHUNK)
        wg_ref[:, piece] = stage_cols_ref[g % 2, 0].astype(BF16)
        wu_ref[:, piece] = stage_cols_ref[g % 2, 1].astype(BF16)
        wd_ref[piece, :] = stage_rows_ref[g % 2].astype(BF16)

    @pl.when(i < n_used)
    def _():
        @pl.when(new_expert)
        def _():
            for copy in weight_group(0):
                copy.start()

        gather_block(slot).wait()
        xb_ref[...] = _load_token_tiles(rows_ref, (slot,), n).astype(BF16)
        n_chunks = D_FF_EXPERT // FF_CHUNK
        rows_per_chunk = -(-n // (n_chunks - MOE_DMA_FREE_CHUNKS))
        for j in range(n_chunks):
            if j % chunks_per_group == 0:
                pl.when(new_expert)(functools.partial(take_weight_group, j // chunks_per_group))
            cols = slice(j * FF_CHUNK, (j + 1) * FF_CHUNK)
            xb = xb_ref[...]
            h = _silu(_dot(xb, wg_ref[:, cols])) * _dot(xb, wu_ref[:, cols])
            part = _dot(h.astype(BF16), wd_ref[cols, :])
            if j == 0:
                acc_ref[...] = part
            else:
                acc_ref[...] += part
            for r in range(j * rows_per_chunk, min((j + 1) * rows_per_chunk, n)):
                gather_row(row_tok_ref[(i + 1) * n + r], other, r).start(priority=GATHER_DMA_PRIORITY)
                scatter_row(row_dst_ref[i * n + r], other, r).start(priority=SCATTER_DMA_PRIORITY)
        _store_token_tiles(ybuf_ref, (slot,), acc_ref[...])
        scatter_block(other).wait()

    @pl.when(i == n_used)
    def _():
        gather_block(slot).wait()
        scatter_all(i * n, other)

    @pl.when(i >= n_used)
    def _():
        rows_ref[slot] = jnp.zeros((block_rows, LANES), F32)
        fill = pltpu.make_async_copy(rows_ref.at[slot], y_hbm.at[pl.ds(i * block_rows, block_rows)],
                                     scatter_sem.at[slot])
        fill.start()
        fill.wait()

    @pl.when((i == last) & (i < n_used))
    def _():
        gather_block(other).wait()
        scatter_all((i + 1) * n, slot)


def _expert_rows(x, row_tok, row_dst, block_e, n_used, n_out_rows, w_gate, w_up, w_down):
    n_blocks = block_e.shape[0]
    hbm = pl.BlockSpec(memory_space=pl.ANY)
    grid_spec = pltpu.PrefetchScalarGridSpec(
        num_scalar_prefetch=4,
        grid=(n_blocks,),
        in_specs=[hbm, hbm, hbm, hbm],
        out_specs=hbm,
        scratch_shapes=[pltpu.VMEM((2, MOE_ROWS * SUBLANES, LANES), F32), pltpu.VMEM((MOE_ROWS, D_MODEL), BF16),
                        pltpu.VMEM((MOE_ROWS, D_MODEL), F32), pltpu.VMEM((2, MOE_ROWS * SUBLANES, LANES), F32),
                        pltpu.VMEM((D_MODEL, D_FF_EXPERT), BF16), pltpu.VMEM((D_MODEL, D_FF_EXPERT), BF16),
                        pltpu.VMEM((D_FF_EXPERT, D_MODEL), BF16),
                        pltpu.VMEM((2, 2, D_MODEL, MOE_WEIGHT_CHUNK), F32),
                        pltpu.VMEM((2, MOE_WEIGHT_CHUNK, D_MODEL), F32),
                        pltpu.SemaphoreType.DMA((2,)), pltpu.SemaphoreType.DMA((2,)),
                        pltpu.SemaphoreType.DMA((2,))],
    )
    return pl.pallas_call(
        _expert_kernel,
        grid_spec=grid_spec,
        out_shape=jax.ShapeDtypeStruct((n_out_rows * SUBLANES, LANES), F32),
        compiler_params=_params(("arbitrary",)),
        name="moe_experts",
    )(row_tok, row_dst, block_e, n_used.reshape(1), x, w_gate, w_up, w_down)


def _combine_kernel(alpha, x_ref, y0_ref, y1_ref, gates_ref, p_ref, g_ref, b_ref, pg_ref, pp_ref, o_ref):
    n = ROW_TILE
    gates = gates_ref[...]
    f = _load_token_tiles(y0_ref, (), n) * gates[:, 0:1] + _load_token_tiles(y1_ref, (), n) * gates[:, 1:2]
    o_ref[...] = _ln_embed(alpha * _load_token_tiles(x_ref, (), n) + f, g_ref[...], b_ref[...], p_ref[...],
                           pg_ref[...], pp_ref[...])


def _moe_combine_tail(alpha, x_tiles, p, y_tiles, gates, g, b, ple_gate, ple_proj):
    t = x_tiles.shape[0] // SUBLANES
    n = ROW_TILE
    row_d = pl.BlockSpec((n, D_MODEL), lambda i: (i, 0))
    tiles_d = pl.BlockSpec((n * SUBLANES, LANES), lambda i: (i, 0))
    return pl.pallas_call(
        functools.partial(_combine_kernel, alpha),
        grid=(t // n,),
        in_specs=[tiles_d, tiles_d, pl.BlockSpec((n * SUBLANES, LANES), lambda i: (t // n + i, 0)),
                  pl.BlockSpec((n, 2), lambda i: (i, 0)), pl.BlockSpec((n, PLE_DIM), lambda i: (i, 0)),
                  _resident((1, D_MODEL)), _resident((1, D_MODEL)),
                  _resident((D_MODEL, D_MODEL)), _resident((PLE_DIM, D_MODEL))],
        out_specs=row_d,
        out_shape=jax.ShapeDtypeStruct((t, D_MODEL), F32),
        compiler_params=_params(("parallel",)),
        name="moe_combine_tail",
    )(x_tiles, y_tiles, y_tiles, gates, p, g.reshape(1, -1), b.reshape(1, -1), ple_gate.astype(BF16),
      ple_proj.astype(BF16))


def _moe_tail(alpha, x_tiles, route, p, w_gate, w_up, w_down, g, b, ple_gate, ple_proj):
    t = route.shape[1]
    experts = route[0:2].T.astype(jnp.int32)
    gates = route[2:4].T
    e_flat = experts.reshape(-1)
    onehot = (e_flat[:, None] == jnp.arange(N_EXPERTS, dtype=jnp.int32)[None, :]).astype(jnp.int32)
    running = jnp.cumsum(onehot, axis=0)
    rank = jnp.sum(onehot * (running - 1), axis=1)
    counts = running[-1]
    padded = (counts + MOE_ROWS - 1) // MOE_ROWS * MOE_ROWS
    pad_end = jnp.cumsum(padded)
    pad_start = pad_end - padded
    dest = (pad_start[e_flat] + rank).astype(jnp.int32)
    n_blocks = (2 * t) // MOE_ROWS + N_EXPERTS
    n_rows = n_blocks * MOE_ROWS
    assign = jnp.arange(2 * t, dtype=jnp.int32)
    row_assign = jnp.full((n_rows,), -1, jnp.int32).at[dest].set(assign)
    used = row_assign >= 0
    spare = 2 * t + jnp.cumsum(jnp.where(used, 0, 1).astype(jnp.int32)) - 1
    row_tok = jnp.where(used, row_assign // 2, 0)
    row_dst = jnp.where(used, (row_assign % 2) * t + row_assign // 2, spare)
    n_spare = n_rows - 2 * t
    first_dst = 2 * t + n_spare + jnp.arange(MOE_ROWS, dtype=jnp.int32)
    row_tok = jnp.concatenate([row_tok, jnp.zeros((MOE_ROWS,), jnp.int32)])
    row_dst = jnp.concatenate([first_dst, row_dst])
    block_start = jnp.arange(n_blocks, dtype=jnp.int32) * MOE_ROWS
    block_e = jnp.minimum(jnp.sum((block_start[:, None] >= pad_end[None, :]).astype(jnp.int32), axis=1),
                          N_EXPERTS - 1)
    n_used = (pad_end[-1] // MOE_ROWS).astype(jnp.int32)
    y_tiles = _expert_rows(x_tiles, row_tok * SUBLANES, row_dst * SUBLANES, block_e, n_used,
                           2 * t + n_spare + MOE_ROWS, w_gate, w_up, w_down)
    return _moe_combine_tail(alpha, x_tiles, p, y_tiles, gates, g, b, ple_gate, ple_proj)


def kernel(x, p, w_in, w_out, gla_gk_up, gla_gk_bias, gla_norm_w, hgrn_lower_bounds, hgrn_norm_w, swa_sinks, rwkv_mu, rwkv_w0, rwkv_w_up, rwkv_a0, rwkv_a_up, rwkv_g_up, rwkv_k_k, rwkv_k_a, rwkv_r_k, rwkv_lnx_w, rwkv_lnx_b, rwkv_vres_down, rwkv_vres_mu, rwkv_v0, rwkv_vres_up, ln1_g, ln1_b, ln2_g, ln2_b, ffn_w_gate, ffn_w_up, ffn_w_down, moe_router, moe_w_gate, moe_w_up, moe_w_down, ple_proj, ple_gate):
    bsz, seq, d = x.shape
    depth = w_in.shape[0]
    alpha = (2.0 * depth) ** 0.25
    lbs = jnp.cumsum(jax.nn.softmax(hgrn_lower_bounds.astype(F32), axis=0), axis=0)
    lbs = lbs - lbs[0]
    outs = []
    for bi in range(bsz):
        xt = x[bi]
        v_first = None
        for i in range(depth):
            w = _group_in_weights(w_in[i], None if i == 0 else rwkv_vres_down[i - 1])
            z_gla, z_hgrn, z_swa, z_rwkv = _in_proj(xt, w)
            o_gla = _gla_mixer(z_gla, gla_gk_up[i], gla_gk_bias[i], gla_norm_w[i])
            o_hgrn = _hgrn_mixer(z_hgrn, lbs[i], hgrn_norm_w[i])
            o_swa = _swa_mixer(z_swa, swa_sinks[i])
            vres = None if i == 0 else (v_first, rwkv_vres_mu[i - 1], rwkv_v0[i - 1], rwkv_vres_up[i - 1])
            rw = _rwkv_mixer(z_rwkv, rwkv_mu[i], rwkv_w0[i], rwkv_w_up[i], rwkv_a0[i], rwkv_a_up[i],
                             rwkv_g_up[i], rwkv_k_k[i], rwkv_k_a[i], rwkv_r_k[i].reshape(-1),
                             rwkv_lnx_w[i], rwkv_lnx_b[i], vres)
            if i == 0:
                o_rwkv, v_first = rw
            else:
                o_rwkv = rw
            mixes = (o_gla, o_hgrn, o_swa, o_rwkv)
            j = i // 2
            if i % 2 == 0:
                xt = _out_proj_ln(alpha, xt, mixes, w_out[i], ln1_g[i], ln1_b[i])
                xt = _dense_ffn_tail(alpha, xt, p[i, bi], ffn_w_gate[j], ffn_w_up[j], ffn_w_down[j],
                                     ln2_g[i], ln2_b[i], ple_gate[i], ple_proj[i])
            else:
                xt, route = _out_proj_ln(alpha, xt, mixes, w_out[i], ln1_g[i], ln1_b[i], moe_router[j])
                xt = _moe_tail(alpha, xt, route, p[i, bi], moe_w_gate[j], moe_w_up[j], moe_w_down[j],
                               ln2_g[i], ln2_b[i], ple_gate[i], ple_proj[i])
        outs.append(xt)
    return jnp.stack(outs, axis=0)
```

```python
import functools

import jax
import jax.numpy as jnp
from jax import lax
from jax.experimental import pallas as pl
from jax.experimental.pallas import tpu as pltpu

F32 = jnp.float32
BF16 = jnp.bfloat16
HIGHEST = lax.Precision.HIGHEST
MIX_DTYPE = BF16

D_MODEL = 1024
GROUP_WIDTH = 256
N_HEADS = 4
HEAD_DIM = 64
GLA_DK = 32
GLA_GATE_RANK = 16
GLA_GATE_NORMALIZER = 16.0
SWA_WINDOW = 128
RWKV_COLS = 3 * GROUP_WIDTH + 16 + 16 + 32
RWKV_V_RANK = 8
LN_EPS = 1e-5
RMS_EPS = 1e-6
RWKV_GN_EPS = 64e-5
D_FF = 2816
N_EXPERTS = 8
D_FF_EXPERT = 3584
PLE_DIM = 256

LANES = 128
SUBLANES = 8
GLA_W = 896
HGRN_W = 1024
SWA_W = 512
RWKV_W = 896
Z_W = GLA_W + HGRN_W + SWA_W + RWKV_W

GLA_SUB = 16
GLA_TILE = 128
GLA_STEP = 1024
GLA_STAGGER = 1
SWA_STAGGER = 1
RWKV_CHUNK = 64
RWKV_TILE = 1024
RWKV_STAGGER = 2
ROW_TILE = 512
FF_CHUNK = 256
SWA_TILE = 256
MOE_ROWS = 512
MOE_WEIGHT_CHUNK = 512
MOE_DMA_FREE_CHUNKS = 10
GATHER_DMA_PRIORITY = 0
SCATTER_DMA_PRIORITY = 1
VMEM_LIMIT = 56 * 1024 * 1024


def _iota(shape, dim):
    return lax.broadcasted_iota(jnp.int32, shape, dim)


def _idiv(x, n):
    return jnp.right_shift(x, n.bit_length() - 1)


def _imod(x, n):
    return jnp.bitwise_and(x, n - 1)


def _dot(a, b, precision=None):
    return jnp.dot(a, b, preferred_element_type=F32, precision=precision)


def _dot_nt(a, b, precision=None):
    return lax.dot_general(a, b, (((1,), (1,)), ((), ())), preferred_element_type=F32, precision=precision)


def _dot_tn(a, b, precision=None):
    return lax.dot_general(a, b, (((0,), (0,)), ((), ())), preferred_element_type=F32, precision=precision)


def _bdot(a, b):
    return _dot(a.astype(BF16), b.astype(BF16))


def _bdot_nt(a, b):
    return _dot_nt(a.astype(BF16), b.astype(BF16))


def _dot_hilo(x, m):
    hi = x.astype(BF16)
    lo = (x - hi.astype(F32)).astype(BF16)
    return _dot(hi, m) + _dot(lo, m)


def _dot_mask(m, x):
    m = m.astype(BF16)
    x1 = x.astype(BF16)
    r1 = x - x1.astype(F32)
    x2 = r1.astype(BF16)
    x3 = (r1 - x2.astype(F32)).astype(BF16)
    return _dot(m, x1) + _dot(m, x2) + _dot(m, x3)


def _dot_3pass_nt(x, w):
    x_hi = x.astype(BF16)
    x_lo = (x - x_hi.astype(F32)).astype(BF16)
    w_hi = w.astype(BF16)
    w_lo = (w - w_hi.astype(F32)).astype(BF16)
    return _dot_nt(x_hi, w_hi) + _dot_nt(x_lo, w_hi) + _dot_nt(x_hi, w_lo)


def _dot_3pass(x, w):
    x_hi = x.astype(BF16)
    x_lo = (x - x_hi.astype(F32)).astype(BF16)
    w_hi = w.astype(BF16)
    w_lo = (w - w_hi.astype(F32)).astype(BF16)
    return _dot(x_hi, w_hi) + _dot(x_lo, w_hi) + _dot(x_hi, w_lo)


def _sigmoid(x):
    return 1.0 / (1.0 + jnp.exp(-x))


def _silu(x):
    return x * _sigmoid(x)


def _log_sigmoid(x):
    return jnp.minimum(x, 0.0) - jnp.log1p(jnp.exp(-jnp.abs(x)))


def _layer_norm(y, g, b):
    mu = jnp.mean(y, axis=-1, keepdims=True)
    d = y - mu
    var = jnp.mean(d * d, axis=-1, keepdims=True)
    return d * lax.rsqrt(var + LN_EPS) * g + b


def _expand_heads(x, head_width):
    lane_head = _idiv(_iota(x.shape, 1), head_width)
    return jnp.concatenate([jnp.where(lane_head == h, x, 0.0) for h in range(N_HEADS)], axis=0)


def _head_group_matrix(width, head_width, value):
    same = _idiv(_iota((width, width), 0), head_width) == _idiv(_iota((width, width), 1), head_width)
    return jnp.where(same, value, 0.0).astype(F32)


def _resident(shape):
    nd = len(shape)
    return pl.BlockSpec(shape, lambda *_: (0,) * nd, pipeline_mode=pl.Buffered(1))


def _params(semantics):
    return pltpu.CompilerParams(dimension_semantics=semantics, vmem_limit_bytes=VMEM_LIMIT)


def _inproj_kernel(x_ref, w_ref, gla_ref, hgrn_ref, swa_ref, rwkv_ref):
    xb = x_ref[...].astype(BF16)
    o = 0
    for ref, width in ((gla_ref, GLA_W), (hgrn_ref, HGRN_W), (swa_ref, SWA_W), (rwkv_ref, RWKV_W)):
        ref[...] = _dot(xb, w_ref[:, o:o + width])
        o += width


def _in_proj(x, w):
    t = x.shape[0]
    widths = (GLA_W, HGRN_W, SWA_W, RWKV_W)
    return pl.pallas_call(
        _inproj_kernel,
        grid=(t // ROW_TILE,),
        in_specs=[pl.BlockSpec((ROW_TILE, D_MODEL), lambda i: (i, 0)), _resident((D_MODEL, Z_W))],
        out_specs=[pl.BlockSpec((ROW_TILE, w_), lambda i: (i, 0)) for w_ in widths],
        out_shape=[jax.ShapeDtypeStruct((t, w_), F32) for w_ in widths],
        compiler_params=_params(("parallel",)),
        name="in_proj",
    )(x, w)


def _group_in_weights(w_in, vres_down):
    gla, hgrn, swa, rwkv = jnp.split(w_in, (784, 784 + 1024, 784 + 1024 + 512), axis=1)
    if vres_down is not None:
        rwkv = jnp.concatenate([rwkv, vres_down], axis=1)
    pad = lambda a, w_: jnp.pad(a, ((0, 0), (0, w_ - a.shape[1])))
    return jnp.concatenate([pad(gla, GLA_W), hgrn, swa, pad(rwkv, RWKV_W)], axis=1).astype(BF16)


def _gated_linear_attention_tile(index, q, k, v, log_f, state_box, out_box):
    length, kw = q.shape
    head_k = kw // N_HEADS
    n_sub = length // GLA_SUB
    row = _iota((length, length), 0)
    col = _iota((length, length), 1)
    same_sub = _idiv(row, GLA_SUB) == _idiv(col, GLA_SUB)
    m_local = jnp.where(same_sub & (col <= row), 1.0, 0.0).astype(F32)
    m_prev = jnp.where(_idiv(col, GLA_SUB) < _idiv(row, GLA_SUB), 1.0, 0.0).astype(F32)
    b_local = _dot_mask(m_local, log_f)
    b_start = _dot_mask(m_prev, log_f)
    yield
    b_full = b_start + b_local
    q_local = q * jnp.exp(b_local)

    q_pos = _imod(_iota((N_HEADS * GLA_SUB, length), 0), GLA_SUB)
    s_pos = _iota((N_HEADS * GLA_SUB, length), 1)
    probs = []
    for c in range(n_sub):
        r0, r1 = c * GLA_SUB, (c + 1) * GLA_SUB
        expo = jnp.where(_iota((r1, kw), 0) < r0, b_start[r0:r0 + 1, :] - b_full[0:r1], -b_local[0:r1])
        k_ref = k[0:r1] * jnp.exp(expo)
        if r1 < length:
            k_ref = jnp.concatenate([k_ref, jnp.zeros((length - r1, kw), F32)], axis=0)
        q_heads = _expand_heads(q_local[r0:r1, :], head_k)
        s = _bdot_nt(q_heads, k_ref)
        probs.append(jnp.where(s_pos <= q_pos + r0, s, 0.0))
    b_total = b_full[length - 1:length, :]
    k_end = k * jnp.exp(b_total - b_full)
    upd = _dot_tn(v.astype(BF16), k_end.astype(BF16))
    yield
    o_heads = _bdot(jnp.concatenate(probs, axis=0), v)
    assert len(out_box) == index, "the previous tile must have replaced the state before it is read"
    state_t = state_box[0]
    o_state = _bdot_nt(q * jnp.exp(b_full), state_t)
    same_head = _idiv(_iota((GROUP_WIDTH, kw), 0), HEAD_DIM) == _idiv(_iota((GROUP_WIDTH, kw), 1), head_k)
    state_box[0] = state_t * jnp.exp(b_total) + jnp.where(same_head, upd, 0.0)
    yield
    v_head = _idiv(_iota((GLA_SUB, GROUP_WIDTH), 1), HEAD_DIM)
    rows = []
    for c in range(n_sub):
        base = c * N_HEADS * GLA_SUB
        acc = jnp.zeros((GLA_SUB, GROUP_WIDTH), F32)
        for h in range(N_HEADS):
            acc = acc + jnp.where(v_head == h, o_heads[base + h * GLA_SUB:base + (h + 1) * GLA_SUB, :], 0.0)
        rows.append(acc)
    out_box.append(jnp.concatenate(rows, axis=0) + o_state)


def _gated_linear_attention(q, k, v, log_f, state_ref):
    state_box = [state_ref[...]]
    tiles = []
    stages = []
    for n in range(q.shape[0] // GLA_TILE):
        rows = slice(n * GLA_TILE, (n + 1) * GLA_TILE)
        stages.append(_gated_linear_attention_tile(n, q[rows], k[rows], v[rows], log_f[rows], state_box, tiles))
    _run_staggered(stages, GLA_STAGGER)
    state_ref[...] = state_box[0]
    return jnp.concatenate(tiles, axis=0)


def _head_rms_gate(o, norm_w, gate):
    ms = _dot_hilo(o * o, _head_group_matrix(GROUP_WIDTH, HEAD_DIM, 1.0 / HEAD_DIM).astype(BF16))
    return o * lax.rsqrt(ms + RMS_EPS) * norm_w * _silu(gate)


def _gla_kernel(z_ref, gk_up_ref, gk_bias_ref, norm_w_ref, o_ref, state_ref):
    @pl.when(pl.program_id(0) == 0)
    def _():
        state_ref[...] = jnp.zeros_like(state_ref)

    z = z_ref[...]
    q = z[:, 0:128] * (GLA_DK ** -0.5)
    k = z[:, 128:256]
    v = z[:, 256:512]
    g = z[:, 512:768]
    gate_in = _dot_3pass(z[:, 768:896], gk_up_ref[...]) + gk_bias_ref[...]
    log_f = _log_sigmoid(gate_in) * (1.0 / GLA_GATE_NORMALIZER)
    o = _gated_linear_attention(q, k, v, log_f, state_ref)
    o_ref[...] = _head_rms_gate(o, norm_w_ref[...], g).astype(o_ref.dtype)


def _hgrn_kernel(z_ref, lb_ref, log_lb_ref, norm_w_ref, o_ref, state_ref):
    @pl.when(pl.program_id(0) == 0)
    def _():
        state_ref[...] = jnp.zeros_like(state_ref)

    z = z_ref[...]
    q = _silu(z[:, 0:256])
    f = z[:, 256:512]
    v = z[:, 512:768]
    g = z[:, 768:1024]
    lb = lb_ref[...]
    a = log_lb_ref[...]
    c = jnp.log1p(-lb) + _log_sigmoid(f)
    log_f = jnp.maximum(a, c) + jnp.log1p(jnp.exp(-jnp.abs(a - c)))
    k = (1.0 - lb) * _sigmoid(-f)
    o = _gated_linear_attention(q, k, v, log_f, state_ref)
    o_ref[...] = _head_rms_gate(o, norm_w_ref[...], g).astype(o_ref.dtype)


def _gla_mixer(z, gk_up, gk_bias, norm_w):
    t = z.shape[0]
    gk_up_pad = jnp.zeros((LANES, N_HEADS * GLA_DK), F32).at[:GLA_GATE_RANK].set(gk_up)
    return pl.pallas_call(
        _gla_kernel,
        grid=(t // GLA_STEP,),
        in_specs=[pl.BlockSpec((GLA_STEP, GLA_W), lambda i: (i, 0)),
                  _resident((LANES, N_HEADS * GLA_DK)), _resident((1, N_HEADS * GLA_DK)),
                  _resident((1, GROUP_WIDTH))],
        out_specs=pl.BlockSpec((GLA_STEP, GROUP_WIDTH), lambda i: (i, 0)),
        out_shape=jax.ShapeDtypeStruct((t, GROUP_WIDTH), MIX_DTYPE),
        scratch_shapes=[pltpu.VMEM((GROUP_WIDTH, N_HEADS * GLA_DK), F32)],
        compiler_params=_params(("arbitrary",)),
        name="gla_mixer",
    )(z, gk_up_pad, gk_bias.reshape(1, -1), jnp.tile(norm_w, N_HEADS).reshape(1, -1))


def _hgrn_mixer(z, lb, norm_w):
    t = z.shape[0]
    return pl.pallas_call(
        _hgrn_kernel,
        grid=(t // GLA_STEP,),
        in_specs=[pl.BlockSpec((GLA_STEP, HGRN_W), lambda i: (i, 0)),
                  _resident((1, GROUP_WIDTH)), _resident((1, GROUP_WIDTH)), _resident((1, GROUP_WIDTH))],
        out_specs=pl.BlockSpec((GLA_STEP, GROUP_WIDTH), lambda i: (i, 0)),
        out_shape=jax.ShapeDtypeStruct((t, GROUP_WIDTH), MIX_DTYPE),
        scratch_shapes=[pltpu.VMEM((GROUP_WIDTH, GROUP_WIDTH), F32)],
        compiler_params=_params(("arbitrary",)),
        name="hgrn_mixer",
    )(z, lb.reshape(1, -1), jnp.log(lb).reshape(1, -1), jnp.tile(norm_w, N_HEADS).reshape(1, -1))


def _swa_block(q, kw, vw, visible, sinks, out_box):
    n = q.shape[0]
    head_cols = lambda x, h: x[:, h * HEAD_DIM:(h + 1) * HEAD_DIM]
    v_lane_head = _idiv(_iota(vw.shape, 1), HEAD_DIM)
    outs = []
    for kv in range(N_HEADS // 2):
        heads = (2 * kv, 2 * kv + 1)
        scores = _bdot_nt(jnp.concatenate([head_cols(q, h) for h in heads], axis=0), head_cols(kw, kv))
        yield
        probs, sink_terms = [], []
        for half, h in enumerate(heads):
            s = jnp.where(visible, scores[half * n:(half + 1) * n], -jnp.inf)
            sink = sinks[:, h:h + 1]
            m = jnp.maximum(jnp.max(s, axis=-1, keepdims=True), sink)
            probs.append(jnp.exp(s - m))
            sink_terms.append(jnp.exp(sink - m))
        o = _bdot(jnp.concatenate(probs, axis=0), jnp.where(v_lane_head == kv, vw, 1.0))
        yield
        sums = head_cols(o, 1 - kv)[:, 0:1]
        outs += [head_cols(o, kv)[half * n:(half + 1) * n] / (sums[half * n:(half + 1) * n] + sink_terms[half])
                 for half in range(2)]
    out_box.append(jnp.concatenate(outs, axis=-1))


def _swa_kernel(q_ref, k_ref, v_ref, kp_ref, vp_ref, sink_ref, o_ref):
    w = SWA_WINDOW
    has_prev = pl.program_id(0) > 0
    q = q_ref[...] * (HEAD_DIM ** -0.5)
    k_all = jnp.concatenate([kp_ref[...], k_ref[...]], axis=0)
    v_all = jnp.concatenate([vp_ref[...], v_ref[...]], axis=0)
    q_pos = _iota((w, 2 * w), 0) + w
    k_pos = _iota((w, 2 * w), 1)
    dist = q_pos - k_pos
    in_window = (dist >= 0) & (dist < w)
    sinks = sink_ref[...]
    blocks = []
    stages = []
    for b in range(q.shape[0] // w):
        visible = in_window if b > 0 else in_window & ((k_pos >= w) | has_prev)
        stages.append(_swa_block(q[b * w:(b + 1) * w], k_all[b * w:(b + 2) * w], v_all[b * w:(b + 2) * w],
                                 visible, sinks, blocks))
    _run_staggered(stages, SWA_STAGGER)
    o_ref[...] = jnp.concatenate(blocks, axis=0).astype(o_ref.dtype)


def _swa_mixer(z, sinks):
    t = z.shape[0]
    w = SWA_WINDOW
    n = SWA_TILE
    prev = lambda col: (lambda i: (jnp.maximum(i * (n // w) - 1, 0), col))
    return pl.pallas_call(
        _swa_kernel,
        grid=(t // n,),
        in_specs=[pl.BlockSpec((n, 256), lambda i: (i, 0)),
                  pl.BlockSpec((n, 128), lambda i: (i, 2)), pl.BlockSpec((n, 128), lambda i: (i, 3)),
                  pl.BlockSpec((w, 128), prev(2)), pl.BlockSpec((w, 128), prev(3)),
                  _resident((1, N_HEADS))],
        out_specs=pl.BlockSpec((n, GROUP_WIDTH), lambda i: (i, 0)),
        out_shape=jax.ShapeDtypeStruct((t, GROUP_WIDTH), MIX_DTYPE),
        compiler_params=_params(("parallel",)),
        name="swa_mixer",
    )(z, z, z, z, z, sinks.reshape(1, -1))


def _run_staggered(stage_generators, stagger):
    live = dict(enumerate(stage_generators))
    rnd = 0
    while live:
        for n in sorted(live):
            if rnd >= n * stagger:
                try:
                    next(live[n])
                except StopIteration:
                    del live[n]
        rnd += 1


def _rwkv_chunk(index, r, k, v, a_vec, b_vec, log_w, state_box, out_box):
    c = r.shape[0]
    width = r.shape[1]
    tri = jnp.where(_iota((c, c), 1) <= _iota((c, c), 0), 1.0, 0.0).astype(F32)
    p = _dot_mask(tri, log_w)
    yield
    p_total = p[c - 1:c, :]
    decay_in = jnp.exp(p)
    decay_out = jnp.exp(-p)
    decay_end = jnp.exp(p_total - p)
    a_in = a_vec * jnp.exp(p - log_w)
    r_in = r * decay_in
    b_out = b_vec * decay_out
    k_out = k * decay_out
    b_end = b_vec * decay_end
    k_end = k * decay_end

    t_pos = _iota((c, width), 0)
    assert width == N_HEADS * c
    s_pos = _imod(_iota((c, width), 1), c)
    strict = s_pos < t_pos
    incl = s_pos <= t_pos
    expand = lambda x: _expand_heads(x, HEAD_DIM)

    scores = _bdot_nt(jnp.concatenate([a_in, r_in], axis=0),
                      jnp.concatenate([expand(b_out), expand(k_out)], axis=0))
    yield
    a_ab = jnp.where(strict, scores[0:c, 0:width], 0.0)
    a_ak = jnp.where(strict, scores[0:c, width:2 * width], 0.0)
    a_rb = jnp.where(incl, scores[c:2 * c, 0:width], 0.0)
    a_rk = jnp.where(incl, scores[c:2 * c, width:2 * width], 0.0)

    t_inv = jnp.where(s_pos == t_pos, 1.0, 0.0) + a_ab
    from_v = _bdot(jnp.concatenate([a_ak, a_rk], axis=0), expand(v))
    x1 = from_v[0:c]
    y_from_v = from_v[c:2 * c]
    power = _bdot(a_ab, expand(a_ab))
    yield
    n_factors = (c - 1).bit_length()
    for _ in range(n_factors - 2):
        both = _bdot(jnp.concatenate([t_inv, power], axis=0), expand(power))
        t_inv = t_inv + both[0:c]
        power = both[c:2 * c]
        yield
    t_inv = t_inv + _bdot(t_inv, expand(power))
    yield
    sol = _bdot(t_inv, jnp.concatenate([expand(x1), expand(a_in)], axis=1))
    yield
    u0 = sol[:, 0:width]
    w_mat = sol[:, width:2 * width]

    assert len(out_box) == index, "the previous chunk must have replaced the state before it is read"
    state = state_box[0]
    from_state = _bdot_nt(jnp.concatenate([w_mat, r_in], axis=0), state)
    yield
    u = u0 + from_state[0:c]
    y = _bdot(a_rb, expand(u)) + y_from_v + from_state[c:2 * c]
    upd = _dot_tn(jnp.concatenate([u, v], axis=0).astype(BF16),
                  jnp.concatenate([b_end, k_end], axis=0).astype(BF16))
    same_head = _idiv(_iota((width, width), 0), HEAD_DIM) == _idiv(_iota((width, width), 1), HEAD_DIM)
    state_box[0] = state * jnp.exp(p_total) + jnp.where(same_head, upd, 0.0)
    out_box.append(y)


def _rwkv_kernel(has_vres, *refs):
    if has_vres:
        (z_ref, zp_ref, vfirst_ref, mu_ref, w0_ref, wup_ref, a0_ref, aup_ref, gup_ref, kk_ref, ka_ref,
         rk_ref, lnw_ref, lnb_ref, v0_ref, vup_ref, o_ref, state_ref) = refs
    else:
        (z_ref, zp_ref, mu_ref, w0_ref, wup_ref, a0_ref, aup_ref, gup_ref, kk_ref, ka_ref,
         rk_ref, lnw_ref, lnb_ref, o_ref, vout_ref, state_ref) = refs
    step = pl.program_id(0)

    @pl.when(step == 0)
    def _():
        state_ref[...] = jnp.zeros_like(state_ref)

    z = z_ref[...]
    last_prev = jnp.where(step > 0, zp_ref[7:8, :], 0.0)
    prev = jnp.where(_iota(z.shape, 0) == 0, last_prev, pltpu.roll(z, 1, axis=0))
    zr = z + (prev - z) * mu_ref[...]
    r = zr[:, 0:256]
    k = zr[:, 256:512]
    v = zr[:, 512:768]
    low = zr[:, 768:896]
    w_pre = w0_ref[...] + _dot_3pass(jnp.tanh(low), wup_ref[...])
    w_log = -(jnp.maximum(-w_pre, 0.0) + jnp.log1p(jnp.exp(-jnp.abs(w_pre)))) - 0.5
    log_w = -jnp.exp(w_log)
    a = _sigmoid(a0_ref[...] + _dot_3pass(low, aup_ref[...]))
    g = _dot_3pass(_sigmoid(low), gup_ref[...])
    if has_vres:
        v = v + (vfirst_ref[...] - v) * _sigmoid(v0_ref[...] + _dot_3pass(low, vup_ref[...]))
    else:
        vout_ref[...] = v
    head_sum = _head_group_matrix(GROUP_WIDTH, HEAD_DIM, 1.0).astype(BF16)
    kk = k * kk_ref[...]
    kk = kk / jnp.maximum(jnp.sqrt(_dot_hilo(kk * kk, head_sum)), 1e-12)
    k = k * (1.0 + (a - 1.0) * ka_ref[...])
    a_vec = -kk
    b_vec = kk * a

    c = RWKV_CHUNK
    state_box = [state_ref[...]]
    chunks = []
    stages = []
    for n in range(z.shape[0] // c):
        rows = slice(n * c, (n + 1) * c)
        stages.append(_rwkv_chunk(n, r[rows], k[rows], v[rows], a_vec[rows], b_vec[rows], log_w[rows],
                                  state_box, chunks))
    _run_staggered(stages, RWKV_STAGGER)
    state_ref[...] = state_box[0]
    y = jnp.concatenate(chunks, axis=0)

    head_mean = _head_group_matrix(GROUP_WIDTH, HEAD_DIM, 1.0 / HEAD_DIM).astype(BF16)
    mu_y = _dot_hilo(y, head_mean)
    d = y - mu_y
    var_y = _dot_hilo(d * d, head_mean)
    y = d * lax.rsqrt(var_y + RWKV_GN_EPS) * lnw_ref[...] + lnb_ref[...]
    bonus = _dot_hilo(r * k * rk_ref[...], head_sum) * v
    o_ref[...] = ((y + bonus) * g).astype(o_ref.dtype)


def _rwkv_mixer(z, mu, w0, w_up, a0, a_up, g_up, k_k, k_a, r_k, lnx_w, lnx_b, vres):
    t = z.shape[0]
    c = RWKV_TILE
    row = lambda a: a.reshape(1, -1)
    low_rows = lambda a, start: jnp.zeros((LANES, GROUP_WIDTH), F32).at[start:start + a.shape[0]].set(a)
    has_vres = vres is not None
    mu_full = jnp.zeros((RWKV_W,), F32).at[:RWKV_COLS].set(mu)
    vec = _resident((1, GROUP_WIDTH))
    mat = _resident((LANES, GROUP_WIDTH))
    tile = pl.BlockSpec((c, GROUP_WIDTH), lambda i: (i, 0))
    z_specs = [pl.BlockSpec((c, RWKV_W), lambda i: (i, 0)),
               pl.BlockSpec((8, RWKV_W), lambda i: (jnp.maximum(i * (c // 8) - 1, 0), 0))]
    common = [row(w0), low_rows(w_up, 0), row(a0), low_rows(a_up, 16), low_rows(g_up, 32),
              row(k_k), row(k_a), row(r_k), row(lnx_w), row(lnx_b)]
    common_specs = [vec, mat, vec, mat, mat, vec, vec, vec, vec, vec]
    if has_vres:
        v_first, vres_mu, v0, v_up = vres
        mu_full = mu_full.at[RWKV_COLS:RWKV_COLS + RWKV_V_RANK].set(vres_mu)
        args = [z, z, v_first, row(mu_full)] + common + [row(v0), low_rows(v_up, 64)]
        in_specs = z_specs + [tile, _resident((1, RWKV_W))] + common_specs + [vec, mat]
        out_specs = tile
        out_shape = jax.ShapeDtypeStruct((t, GROUP_WIDTH), MIX_DTYPE)
    else:
        args = [z, z, row(mu_full)] + common
        in_specs = z_specs + [_resident((1, RWKV_W))] + common_specs
        out_specs = [tile, tile]
        out_shape = [jax.ShapeDtypeStruct((t, GROUP_WIDTH), MIX_DTYPE), jax.ShapeDtypeStruct((t, GROUP_WIDTH), F32)]
    return pl.pallas_call(
        functools.partial(_rwkv_kernel, has_vres),
        grid=(t // c,),
        in_specs=in_specs,
        out_specs=out_specs,
        out_shape=out_shape,
        scratch_shapes=[pltpu.VMEM((GROUP_WIDTH, GROUP_WIDTH), F32)],
        compiler_params=_params(("arbitrary",)),
        name="rwkv_mixer",
    )(*args)


def _store_token_tiles(ref, index, x):
    n = x.shape[0]
    for j in range(SUBLANES):
        ref[(*index, pl.ds(j, n, stride=SUBLANES), slice(None))] = x[:, j * LANES:(j + 1) * LANES]


def _load_token_tiles(ref, index, n):
    return jnp.concatenate([ref[(*index, pl.ds(j, n, stride=SUBLANES), slice(None))] for j in range(SUBLANES)],
                           axis=-1)


def _top2_route(logits):
    row = _iota(logits.shape, 0).astype(F32)
    m1 = jnp.max(logits, axis=0, keepdims=True)
    i1 = jnp.min(jnp.where(logits == m1, row, N_EXPERTS), axis=0, keepdims=True)
    rest = jnp.where(row == i1, -jnp.inf, logits)
    m2 = jnp.max(rest, axis=0, keepdims=True)
    i2 = jnp.min(jnp.where(rest == m2, row, N_EXPERTS), axis=0, keepdims=True)
    e2 = jnp.exp(m2 - m1)
    g1 = 1.0 / (1.0 + e2)
    g2 = e2 * g1
    return jnp.where(row == 0, i1, jnp.where(row == 1, i2, jnp.where(row == 2, g1, jnp.where(row == 3, g2, 0.0))))


def _mix_residual_norm(alpha, x_ref, mix_refs, w_ref, g_ref, b_ref):
    acc = alpha * x_ref[...]
    for h, ref in enumerate(mix_refs):
        acc = acc + _dot(ref[...], w_ref[h * GROUP_WIDTH:(h + 1) * GROUP_WIDTH, :])
    return _layer_norm(acc, g_ref[...], b_ref[...])


def _outproj_route_kernel(alpha, x_ref, o0_ref, o1_ref, o2_ref, o3_ref, w_ref, g_ref, b_ref, router_ref,
                          tiles_ref, route_ref):
    y = _mix_residual_norm(alpha, x_ref, (o0_ref, o1_ref, o2_ref, o3_ref), w_ref, g_ref, b_ref)
    route_ref[...] = _top2_route(_dot_3pass_nt(router_ref[...], y))
    _store_token_tiles(tiles_ref, (), y)


def _out_proj_ln_route(alpha, x, mixes, w_out, g, b, router):
    t = x.shape[0]
    row_d = pl.BlockSpec((ROW_TILE, D_MODEL), lambda i: (i, 0))
    row_g = pl.BlockSpec((ROW_TILE, GROUP_WIDTH), lambda i: (i, 0))
    return pl.pallas_call(
        functools.partial(_outproj_route_kernel, alpha),
        grid=(t // ROW_TILE,),
        in_specs=[row_d, row_g, row_g, row_g, row_g, _resident((D_MODEL, D_MODEL)),
                  _resident((1, D_MODEL)), _resident((1, D_MODEL)), _resident((N_EXPERTS, D_MODEL))],
        out_specs=[pl.BlockSpec((ROW_TILE * SUBLANES, LANES), lambda i: (i, 0)),
                   pl.BlockSpec((N_EXPERTS, ROW_TILE), lambda i: (0, i))],
        out_shape=[jax.ShapeDtypeStruct((t * SUBLANES, LANES), F32), jax.ShapeDtypeStruct((N_EXPERTS, t), F32)],
        compiler_params=_params(("parallel",)),
        name="out_proj_ln_route",
    )(x, *mixes, w_out.astype(BF16), g.reshape(1, -1), b.reshape(1, -1), router.T)


def _ln_embed(y, ln_g, ln_b, p, ple_gate, ple_proj):
    x = _layer_norm(y, ln_g, ln_b)
    gate = _sigmoid(_dot(x.astype(BF16), ple_gate))
    return x + gate * _dot(p.astype(BF16), ple_proj)


def _dense_ffn_kernel(alpha, x_ref, o0_ref, o1_ref, o2_ref, o3_ref, wo_ref, g1_ref, b1_ref, p_ref, wg_ref, wu_ref,
                      wd_ref, g_ref, b_ref, pg_ref, pp_ref, y_ref, acc_ref):
    x = _mix_residual_norm(alpha, x_ref, (o0_ref, o1_ref, o2_ref, o3_ref), wo_ref, g1_ref, b1_ref)
    xb = x.astype(BF16)
    acc_ref[...] = alpha * x
    for j in range(D_FF // FF_CHUNK):
        cols = slice(j * FF_CHUNK, (j + 1) * FF_CHUNK)
        h = _silu(_dot(xb, wg_ref[:, cols])) * _dot(xb, wu_ref[:, cols])
        acc_ref[...] += _dot(h.astype(BF16), wd_ref[cols, :])
    y_ref[...] = _ln_embed(acc_ref[...], g_ref[...], b_ref[...], p_ref[...], pg_ref[...], pp_ref[...])


def _dense_layer_tail(alpha, x, mixes, w_out, g1, b1, p_all, p_block, w_gate, w_up, w_down, g, b, ple_gate, ple_proj):
    t = x.shape[0]
    row_d = pl.BlockSpec((ROW_TILE, D_MODEL), lambda i: (i, 0))
    row_g = pl.BlockSpec((ROW_TILE, GROUP_WIDTH), lambda i: (i, 0))
    return pl.pallas_call(
        functools.partial(_dense_ffn_kernel, alpha),
        grid=(t // ROW_TILE,),
        in_specs=[row_d, row_g, row_g, row_g, row_g, _resident((D_MODEL, D_MODEL)),
                  _resident((1, D_MODEL)), _resident((1, D_MODEL)),
                  pl.BlockSpec((ROW_TILE, PLE_DIM), lambda i: (p_block + i, 0)),
                  _resident((D_MODEL, D_FF)), _resident((D_MODEL, D_FF)), _resident((D_FF, D_MODEL)),
                  _resident((1, D_MODEL)), _resident((1, D_MODEL)),
                  _resident((D_MODEL, D_MODEL)), _resident((PLE_DIM, D_MODEL))],
        out_specs=row_d,
        out_shape=jax.ShapeDtypeStruct((t, D_MODEL), F32),
        scratch_shapes=[pltpu.VMEM((ROW_TILE, D_MODEL), F32)],
        compiler_params=_params(("parallel",)),
        name="dense_ffn_tail",
    )(x, *mixes, w_out.astype(BF16), g1.reshape(1, -1), b1.reshape(1, -1), p_all,
      w_gate.astype(BF16), w_up.astype(BF16), w_down.astype(BF16), g.reshape(1, -1), b.reshape(1, -1),
      ple_gate.astype(BF16), ple_proj.astype(BF16))


def _weight_group_copies(e, group, wg_hbm, wu_hbm, wd_hbm, stage_cols_ref, stage_rows_ref, sem):
    slot = group % 2
    lo, hi = group * MOE_WEIGHT_CHUNK, (group + 1) * MOE_WEIGHT_CHUNK
    return (pltpu.make_async_copy(wg_hbm.at[e, :, lo:hi], stage_cols_ref.at[slot, 0], sem.at[slot]),
            pltpu.make_async_copy(wu_hbm.at[e, :, lo:hi], stage_cols_ref.at[slot, 1], sem.at[slot]),
            pltpu.make_async_copy(wd_hbm.at[e, lo:hi, :], stage_rows_ref.at[slot], sem.at[slot]))


def _expert_kernel(row_tok_ref, row_dst_ref, block_e_ref, n_used_ref, x_hbm, wg_hbm, wu_hbm, wd_hbm, y_hbm,
                   rows_ref, xb_ref, acc_ref, ybuf_ref, wg_ref, wu_ref, wd_ref, stage_cols_ref, stage_rows_ref,
                   gather_sem, scatter_sem, weight_sem):
    i = pl.program_id(0)
    n_used = n_used_ref[0]
    last = pl.num_programs(0) - 1
    n = MOE_ROWS
    slot = lax.rem(i, 2)
    other = 1 - slot
    tile = lambda first_row: pl.ds(pl.multiple_of(first_row, SUBLANES), SUBLANES)
    gather_row = lambda tok_row, s, r: pltpu.make_async_copy(
        x_hbm.at[tile(tok_row)], rows_ref.at[s, tile(r * SUBLANES)], gather_sem.at[s])
    scatter_row = lambda dst_row, s, r: pltpu.make_async_copy(
        ybuf_ref.at[s, tile(r * SUBLANES)], y_hbm.at[tile(dst_row)], scatter_sem.at[s])
    block_rows = n * SUBLANES
    gather_block = lambda s: pltpu.make_async_copy(x_hbm.at[pl.ds(0, block_rows)], rows_ref.at[s], gather_sem.at[s])
    scatter_block = lambda s: pltpu.make_async_copy(ybuf_ref.at[s], y_hbm.at[pl.ds(0, block_rows)],
                                                    scatter_sem.at[s])

    @pl.when(i == 0)
    def _():
        ybuf_ref[1] = jnp.zeros((block_rows, LANES), F32)

        def start(r, carry):
            gather_row(row_tok_ref[r], 0, r).start()
            return carry

        lax.fori_loop(0, n, start, 0)

    def scatter_all(first_dst, s):
        def start(r, carry):
            scatter_row(row_dst_ref[first_dst + r], s, r).start()
            return carry

        lax.fori_loop(0, n, start, 0)
        scatter_block(s).wait()

    expert = block_e_ref[i]
    new_expert = (i == 0) | (expert != block_e_ref[jnp.maximum(i - 1, 0)])

    weight_group = lambda g: _weight_group_copies(expert, g, wg_hbm, wu_hbm, wd_hbm, stage_cols_ref,
                                                  stage_rows_ref, weight_sem)
    chunks_per_group = MOE_WEIGHT_CHUNK // FF_CHUNK
    n_groups = D_FF_EXPERT // MOE_WEIGHT_CHUNK

    def take_weight_group(g):
        if g + 1 < n_groups:
            for copy in weight_group(g + 1):
                copy.start()
        for copy in weight_group(g):
            copy.wait()
        piece = slice(g * MOE_WEIGHT_CHUNK, (g + 1) * MOE_WEIGHT_CHUNK)
        wg_ref[:, piece] = stage_cols_ref[g % 2, 0].astype(BF16)
        wu_ref[:, piece] = stage_cols_ref[g % 2, 1].astype(BF16)
        wd_ref[piece, :] = stage_rows_ref[g % 2].astype(BF16)

    @pl.when(i < n_used)
    def _():
        @pl.when(new_expert)
        def _():
            for copy in weight_group(0):
                copy.start()

        gather_block(slot).wait()
        xb_ref[...] = _load_token_tiles(rows_ref, (slot,), n).astype(BF16)
        n_chunks = D_FF_EXPERT // FF_CHUNK
        rows_per_chunk = -(-n // (n_chunks - MOE_DMA_FREE_CHUNKS))
        for j in range(n_chunks):
            if j % chunks_per_group == 0:
                pl.when(new_expert)(functools.partial(take_weight_group, j // chunks_per_group))
            cols = slice(j * FF_CHUNK, (j + 1) * FF_CHUNK)
            xb = xb_ref[...]
            h = _silu(_dot(xb, wg_ref[:, cols])) * _dot(xb, wu_ref[:, cols])
            part = _dot(h.astype(BF16), wd_ref[cols, :])
            if j == 0:
                acc_ref[...] = part
            else:
                acc_ref[...] += part
            for r in range(j * rows_per_chunk, min((j + 1) * rows_per_chunk, n)):
                gather_row(row_tok_ref[(i + 1) * n + r], other, r).start(priority=GATHER_DMA_PRIORITY)
                scatter_row(row_dst_ref[i * n + r], other, r).start(priority=SCATTER_DMA_PRIORITY)
        _store_token_tiles(ybuf_ref, (slot,), acc_ref[...])
        scatter_block(other).wait()

    @pl.when(i == n_used)
    def _():
        gather_block(slot).wait()
        scatter_all(i * n, other)

    @pl.when(i >= n_used)
    def _():
        rows_ref[slot] = jnp.zeros((block_rows, LANES), F32)
        fill = pltpu.make_async_copy(rows_ref.at[slot], y_hbm.at[pl.ds(i * block_rows, block_rows)],
                                     scatter_sem.at[slot])
        fill.start()
        fill.wait()

    @pl.when((i == last) & (i < n_used))
    def _():
        gather_block(other).wait()
        scatter_all((i + 1) * n, slot)


def _expert_rows(x, row_tok, row_dst, block_e, n_used, n_out_rows, w_gate, w_up, w_down):
    n_blocks = block_e.shape[0]
    hbm = pl.BlockSpec(memory_space=pl.ANY)
    grid_spec = pltpu.PrefetchScalarGridSpec(
        num_scalar_prefetch=4,
        grid=(n_blocks,),
        in_specs=[hbm, hbm, hbm, hbm],
        out_specs=hbm,
        scratch_shapes=[pltpu.VMEM((2, MOE_ROWS * SUBLANES, LANES), F32), pltpu.VMEM((MOE_ROWS, D_MODEL), BF16),
                        pltpu.VMEM((MOE_ROWS, D_MODEL), F32), pltpu.VMEM((2, MOE_ROWS * SUBLANES, LANES), F32),
                        pltpu.VMEM((D_MODEL, D_FF_EXPERT), BF16), pltpu.VMEM((D_MODEL, D_FF_EXPERT), BF16),
                        pltpu.VMEM((D_FF_EXPERT, D_MODEL), BF16),
                        pltpu.VMEM((2, 2, D_MODEL, MOE_WEIGHT_CHUNK), F32),
                        pltpu.VMEM((2, MOE_WEIGHT_CHUNK, D_MODEL), F32),
                        pltpu.SemaphoreType.DMA((2,)), pltpu.SemaphoreType.DMA((2,)),
                        pltpu.SemaphoreType.DMA((2,))],
    )
    return pl.pallas_call(
        _expert_kernel,
        grid_spec=grid_spec,
        out_shape=jax.ShapeDtypeStruct((n_out_rows * SUBLANES, LANES), F32),
        compiler_params=_params(("arbitrary",)),
        name="moe_experts",
    )(row_tok, row_dst, block_e, n_used.reshape(1), x, w_gate, w_up, w_down)


def _combine_kernel(alpha, x_ref, y0_ref, y1_ref, gates_ref, p_ref, g_ref, b_ref, pg_ref, pp_ref, o_ref):
    n = ROW_TILE
    gates = gates_ref[...]
    f = _load_token_tiles(y0_ref, (), n) * gates[:, 0:1] + _load_token_tiles(y1_ref, (), n) * gates[:, 1:2]
    o_ref[...] = _ln_embed(alpha * _load_token_tiles(x_ref, (), n) + f, g_ref[...], b_ref[...], p_ref[...],
                           pg_ref[...], pp_ref[...])


def _moe_combine_tail(alpha, x_tiles, p_all, p_block, y_tiles, gates, g, b, ple_gate, ple_proj):
    t = x_tiles.shape[0] // SUBLANES
    n = ROW_TILE
    row_d = pl.BlockSpec((n, D_MODEL), lambda i: (i, 0))
    tiles_d = pl.BlockSpec((n * SUBLANES, LANES), lambda i: (i, 0))
    return pl.pallas_call(
        functools.partial(_combine_kernel, alpha),
        grid=(t // n,),
        in_specs=[tiles_d, tiles_d, pl.BlockSpec((n * SUBLANES, LANES), lambda i: (t // n + i, 0)),
                  pl.BlockSpec((n, 2), lambda i: (i, 0)), pl.BlockSpec((n, PLE_DIM), lambda i: (p_block + i, 0)),
                  _resident((1, D_MODEL)), _resident((1, D_MODEL)),
                  _resident((D_MODEL, D_MODEL)), _resident((PLE_DIM, D_MODEL))],
        out_specs=row_d,
        out_shape=jax.ShapeDtypeStruct((t, D_MODEL), F32),
        compiler_params=_params(("parallel",)),
        name="moe_combine_tail",
    )(x_tiles, y_tiles, y_tiles, gates, p_all, g.reshape(1, -1), b.reshape(1, -1), ple_gate.astype(BF16),
      ple_proj.astype(BF16))


def _moe_tail(alpha, x_tiles, route, p_all, p_block, w_gate, w_up, w_down, g, b, ple_gate, ple_proj):
    t = route.shape[1]
    experts = route[0:2].T.astype(jnp.int32)
    gates = route[2:4].T
    e_flat = experts.reshape(-1)
    onehot = (e_flat[:, None] == jnp.arange(N_EXPERTS, dtype=jnp.int32)[None, :]).astype(jnp.int32)
    running = jnp.cumsum(onehot, axis=0)
    rank = jnp.sum(onehot * (running - 1), axis=1)
    counts = running[-1]
    padded = (counts + MOE_ROWS - 1) // MOE_ROWS * MOE_ROWS
    pad_end = jnp.cumsum(padded)
    pad_start = pad_end - padded
    dest = (pad_start[e_flat] + rank).astype(jnp.int32)
    n_blocks = (2 * t) // MOE_ROWS + N_EXPERTS
    n_rows = n_blocks * MOE_ROWS
    assign = jnp.arange(2 * t, dtype=jnp.int32)
    row_assign = jnp.full((n_rows,), -1, jnp.int32).at[dest].set(assign)
    used = row_assign >= 0
    spare = 2 * t + jnp.cumsum(jnp.where(used, 0, 1).astype(jnp.int32)) - 1
    row_tok = jnp.where(used, row_assign // 2, 0)
    row_dst = jnp.where(used, (row_assign % 2) * t + row_assign // 2, spare)
    n_spare = n_rows - 2 * t
    first_dst = 2 * t + n_spare + jnp.arange(MOE_ROWS, dtype=jnp.int32)
    row_tok = jnp.concatenate([row_tok, jnp.zeros((MOE_ROWS,), jnp.int32)])
    row_dst = jnp.concatenate([first_dst, row_dst])
    block_start = jnp.arange(n_blocks, dtype=jnp.int32) * MOE_ROWS
    block_e = jnp.minimum(jnp.sum((block_start[:, None] >= pad_end[None, :]).astype(jnp.int32), axis=1),
                          N_EXPERTS - 1)
    n_used = (pad_end[-1] // MOE_ROWS).astype(jnp.int32)
    y_tiles = _expert_rows(x_tiles, row_tok * SUBLANES, row_dst * SUBLANES, block_e, n_used,
                           2 * t + n_spare + MOE_ROWS, w_gate, w_up, w_down)
    return _moe_combine_tail(alpha, x_tiles, p_all, p_block, y_tiles, gates, g, b, ple_gate, ple_proj)


def kernel(x, p, w_in, w_out, gla_gk_up, gla_gk_bias, gla_norm_w, hgrn_lower_bounds, hgrn_norm_w, swa_sinks, rwkv_mu, rwkv_w0, rwkv_w_up, rwkv_a0, rwkv_a_up, rwkv_g_up, rwkv_k_k, rwkv_k_a, rwkv_r_k, rwkv_lnx_w, rwkv_lnx_b, rwkv_vres_down, rwkv_vres_mu, rwkv_v0, rwkv_vres_up, ln1_g, ln1_b, ln2_g, ln2_b, ffn_w_gate, ffn_w_up, ffn_w_down, moe_router, moe_w_gate, moe_w_up, moe_w_down, ple_proj, ple_gate):
    bsz, seq, d = x.shape
    depth = w_in.shape[0]
    alpha = (2.0 * depth) ** 0.25
    lbs = jnp.cumsum(jax.nn.softmax(hgrn_lower_bounds.astype(F32), axis=0), axis=0)
    lbs = lbs - lbs[0]
    p_all = p.reshape(-1, PLE_DIM)
    outs = []
    for bi in range(bsz):
        xt = x[bi]
        v_first = None
        for i in range(depth):
            w = _group_in_weights(w_in[i], None if i == 0 else rwkv_vres_down[i - 1])
            z_gla, z_hgrn, z_swa, z_rwkv = _in_proj(xt, w)
            o_gla = _gla_mixer(z_gla, gla_gk_up[i], gla_gk_bias[i], gla_norm_w[i])
            o_hgrn = _hgrn_mixer(z_hgrn, lbs[i], hgrn_norm_w[i])
            o_swa = _swa_mixer(z_swa, swa_sinks[i])
            vres = None if i == 0 else (v_first, rwkv_vres_mu[i - 1], rwkv_v0[i - 1], rwkv_vres_up[i - 1])
            rw = _rwkv_mixer(z_rwkv, rwkv_mu[i], rwkv_w0[i], rwkv_w_up[i], rwkv_a0[i], rwkv_a_up[i],
                             rwkv_g_up[i], rwkv_k_k[i], rwkv_k_a[i], rwkv_r_k[i].reshape(-1),
                             rwkv_lnx_w[i], rwkv_lnx_b[i], vres)
            if i == 0:
                o_rwkv, v_first = rw
            else:
                o_rwkv = rw
            mixes = (o_gla, o_hgrn, o_swa, o_rwkv)
            j = i // 2
            p_block = (i * bsz + bi) * (seq // ROW_TILE)
            if i % 2 == 0:
                xt = _dense_layer_tail(alpha, xt, mixes, w_out[i], ln1_g[i], ln1_b[i], p_all, p_block,
                                       ffn_w_gate[j], ffn_w_up[j], ffn_w_down[j], ln2_g[i], ln2_b[i],
                                       ple_gate[i], ple_proj[i])
            else:
                x_tiles, route = _out_proj_ln_route(alpha, xt, mixes, w_out[i], ln1_g[i], ln1_b[i], moe_router[j])
                xt = _moe_tail(alpha, x_tiles, route, p_all, p_block, moe_w_gate[j], moe_w_up[j], moe_w_down[j],
                               ln2_g[i], ln2_b[i], ple_gate[i], ple_proj[i])
        outs.append(xt)
    return jnp.stack(outs, axis=0)
```

```python
import functools
from typing import Callable, NamedTuple

import jax
import jax.numpy as jnp
from jax import lax
from jax.experimental import pallas as pl
from jax.experimental.pallas import tpu as pltpu

F32 = jnp.float32
BF16 = jnp.bfloat16
HIGHEST = lax.Precision.HIGHEST
MIX_DTYPE = BF16

D_MODEL = 1024
GROUP_WIDTH = 256
N_HEADS = 4
HEAD_DIM = 64
GLA_DK = 32
GLA_GATE_RANK = 16
GLA_GATE_NORMALIZER = 16.0
SWA_WINDOW = 128
RWKV_COLS = 3 * GROUP_WIDTH + 16 + 16 + 32
RWKV_V_RANK = 8
LN_EPS = 1e-5
RMS_EPS = 1e-6
RWKV_GN_EPS = 64e-5
D_FF = 2816
N_EXPERTS = 8
D_FF_EXPERT = 3584
PLE_DIM = 256

LANES = 128
SUBLANES = 8
GLA_W = 896
HGRN_W = 1024
SWA_W = 512
RWKV_W = 896
Z_W = GLA_W + HGRN_W + SWA_W + RWKV_W

GLA_SUB = 16
GLA_TILE = 128
GLA_STEP = 1024
GLA_STAGGER = 1
SWA_STAGGER = 1
MIX_STEP = 512
RWKV_CHUNK = 64
RWKV_TILE = 1024
RWKV_STAGGER = 2
ROW_TILE = 512
FF_CHUNK = 256
SWA_TILE = 256
MOE_ROWS = 512
MOE_WEIGHT_CHUNK = 512
MOE_DMA_FREE_CHUNKS = 10
GATHER_DMA_PRIORITY = 0
SCATTER_DMA_PRIORITY = 1
VMEM_LIMIT = 56 * 1024 * 1024


def _iota(shape, dim):
    return lax.broadcasted_iota(jnp.int32, shape, dim)


def _idiv(x, n):
    return jnp.right_shift(x, n.bit_length() - 1)


def _imod(x, n):
    return jnp.bitwise_and(x, n - 1)


def _dot(a, b, precision=None):
    return jnp.dot(a, b, preferred_element_type=F32, precision=precision)


def _dot_nt(a, b, precision=None):
    return lax.dot_general(a, b, (((1,), (1,)), ((), ())), preferred_element_type=F32, precision=precision)


def _dot_tn(a, b, precision=None):
    return lax.dot_general(a, b, (((0,), (0,)), ((), ())), preferred_element_type=F32, precision=precision)


def _bdot(a, b):
    return _dot(a.astype(BF16), b.astype(BF16))


def _bdot_nt(a, b):
    return _dot_nt(a.astype(BF16), b.astype(BF16))


def _dot_hilo(x, m):
    hi = x.astype(BF16)
    lo = (x - hi.astype(F32)).astype(BF16)
    return _dot(hi, m) + _dot(lo, m)


def _dot_mask(m, x):
    m = m.astype(BF16)
    x1 = x.astype(BF16)
    r1 = x - x1.astype(F32)
    x2 = r1.astype(BF16)
    x3 = (r1 - x2.astype(F32)).astype(BF16)
    return _dot(m, x1) + _dot(m, x2) + _dot(m, x3)


def _dot_3pass_nt(x, w):
    x_hi = x.astype(BF16)
    x_lo = (x - x_hi.astype(F32)).astype(BF16)
    w_hi = w.astype(BF16)
    w_lo = (w - w_hi.astype(F32)).astype(BF16)
    return _dot_nt(x_hi, w_hi) + _dot_nt(x_lo, w_hi) + _dot_nt(x_hi, w_lo)


def _dot_3pass(x, w):
    x_hi = x.astype(BF16)
    x_lo = (x - x_hi.astype(F32)).astype(BF16)
    w_hi = w.astype(BF16)
    w_lo = (w - w_hi.astype(F32)).astype(BF16)
    return _dot(x_hi, w_hi) + _dot(x_lo, w_hi) + _dot(x_hi, w_lo)


def _sigmoid(x):
    return 1.0 / (1.0 + jnp.exp(-x))


def _silu(x):
    return x * _sigmoid(x)


def _softplus_neg_abs(x):
    return jnp.log(1.0 + jnp.exp(-jnp.abs(x)))


def _log_sigmoid(x):
    return jnp.minimum(x, 0.0) - _softplus_neg_abs(x)


def _layer_norm(y, g, b):
    mu = jnp.mean(y, axis=-1, keepdims=True)
    d = y - mu
    var = jnp.mean(d * d, axis=-1, keepdims=True)
    return d * lax.rsqrt(var + LN_EPS) * g + b


def _expand_heads(x, head_width):
    lane_head = _idiv(_iota(x.shape, 1), head_width)
    return jnp.concatenate([jnp.where(lane_head == h, x, 0.0) for h in range(N_HEADS)], axis=0)


def _head_group_matrix(width, head_width, value):
    same = _idiv(_iota((width, width), 0), head_width) == _idiv(_iota((width, width), 1), head_width)
    return jnp.where(same, value, 0.0).astype(F32)


def _resident(shape):
    nd = len(shape)
    return pl.BlockSpec(shape, lambda *_: (0,) * nd, pipeline_mode=pl.Buffered(1))


def _params(semantics):
    return pltpu.CompilerParams(dimension_semantics=semantics, vmem_limit_bytes=VMEM_LIMIT)


def _inproj_kernel(x_ref, w_ref, gla_ref, hgrn_ref, swa_ref, rwkv_ref):
    xb = x_ref[...].astype(BF16)
    o = 0
    for ref, width in ((gla_ref, GLA_W), (hgrn_ref, HGRN_W), (swa_ref, SWA_W), (rwkv_ref, RWKV_W)):
        ref[...] = _dot(xb, w_ref[:, o:o + width])
        o += width


def _in_proj(x, w):
    t = x.shape[0]
    widths = (GLA_W, HGRN_W, SWA_W, RWKV_W)
    return pl.pallas_call(
        _inproj_kernel,
        grid=(t // ROW_TILE,),
        in_specs=[pl.BlockSpec((ROW_TILE, D_MODEL), lambda i: (i, 0)), _resident((D_MODEL, Z_W))],
        out_specs=[pl.BlockSpec((ROW_TILE, w_), lambda i: (i, 0)) for w_ in widths],
        out_shape=[jax.ShapeDtypeStruct((t, w_), F32) for w_ in widths],
        compiler_params=_params(("parallel",)),
        name="in_proj",
    )(x, w)


def _group_in_weights(w_in, vres_down):
    gla, hgrn, swa, rwkv = jnp.split(w_in, (784, 784 + 1024, 784 + 1024 + 512), axis=1)
    if vres_down is not None:
        rwkv = jnp.concatenate([rwkv, vres_down], axis=1)
    pad = lambda a, w_: jnp.pad(a, ((0, 0), (0, w_ - a.shape[1])))
    return jnp.concatenate([pad(gla, GLA_W), hgrn, swa, pad(rwkv, RWKV_W)], axis=1).astype(BF16)


def _gated_linear_attention_tile(index, q, k, v, log_f, state_box, out_box):
    length, kw = q.shape
    head_k = kw // N_HEADS
    n_sub = length // GLA_SUB
    row = _iota((length, length), 0)
    col = _iota((length, length), 1)
    same_sub = _idiv(row, GLA_SUB) == _idiv(col, GLA_SUB)
    m_local = jnp.where(same_sub & (col <= row), 1.0, 0.0).astype(F32)
    m_prev = jnp.where(_idiv(col, GLA_SUB) < _idiv(row, GLA_SUB), 1.0, 0.0).astype(F32)
    sums = _dot_mask(jnp.concatenate([m_local, m_prev], axis=0), log_f)
    yield
    b_local = sums[0:length]
    b_start = sums[length:2 * length]
    b_full = b_start + b_local
    q_local = q * jnp.exp(b_local)

    q_pos = _imod(_iota((N_HEADS * GLA_SUB, length), 0), GLA_SUB)
    s_pos = _iota((N_HEADS * GLA_SUB, length), 1)
    probs = []
    for c in range(n_sub):
        r0, r1 = c * GLA_SUB, (c + 1) * GLA_SUB
        expo = jnp.where(_iota((r1, kw), 0) < r0, b_start[r0:r0 + 1, :] - b_full[0:r1], -b_local[0:r1])
        k_ref = k[0:r1] * jnp.exp(expo)
        if r1 < length:
            k_ref = jnp.concatenate([k_ref, jnp.zeros((length - r1, kw), F32)], axis=0)
        q_heads = _expand_heads(q_local[r0:r1, :], head_k)
        s = _bdot_nt(q_heads, k_ref)
        probs.append(jnp.where(s_pos <= q_pos + r0, s, 0.0))
    b_total = b_full[length - 1:length, :]
    k_end = k * jnp.exp(b_total - b_full)
    upd = _dot_tn(v.astype(BF16), k_end.astype(BF16))
    yield
    o_heads = _bdot(jnp.concatenate(probs, axis=0), v)
    assert len(out_box) == index, "the previous tile must have replaced the state before it is read"
    state_t = state_box[0]
    o_state = _bdot_nt(q * jnp.exp(b_full), state_t)
    same_head = _idiv(_iota((GROUP_WIDTH, kw), 0), HEAD_DIM) == _idiv(_iota((GROUP_WIDTH, kw), 1), head_k)
    state_box[0] = state_t * jnp.exp(b_total) + jnp.where(same_head, upd, 0.0)
    yield
    v_head = _idiv(_iota((GLA_SUB, GROUP_WIDTH), 1), HEAD_DIM)
    rows = []
    for c in range(n_sub):
        base = c * N_HEADS * GLA_SUB
        acc = jnp.zeros((GLA_SUB, GROUP_WIDTH), F32)
        for h in range(N_HEADS):
            acc = acc + jnp.where(v_head == h, o_heads[base + h * GLA_SUB:base + (h + 1) * GLA_SUB, :], 0.0)
        rows.append(acc)
    out_box.append(jnp.concatenate(rows, axis=0) + o_state)


def _gated_linear_attention_plan(q, k, v, log_f, gate, norm_w, o_ref, state_ref):
    state_box = [state_ref[...]]
    tiles = []
    stages = []
    for n in range(q.shape[0] // GLA_TILE):
        rows = slice(n * GLA_TILE, (n + 1) * GLA_TILE)
        stages.append(_gated_linear_attention_tile(n, q[rows], k[rows], v[rows], log_f[rows], state_box, tiles))

    def finish():
        state_ref[...] = state_box[0]
        o = jnp.concatenate(tiles, axis=0)
        ms = _dot_hilo(o * o, _head_group_matrix(GROUP_WIDTH, HEAD_DIM, 1.0 / HEAD_DIM).astype(BF16))
        o_ref[...] = (o * lax.rsqrt(ms + RMS_EPS) * norm_w * _silu(gate)).astype(o_ref.dtype)

    return stages, GLA_STAGGER, finish


def _zero_at_first_step(*state_refs):
    @pl.when(pl.program_id(0) == 0)
    def _():
        for ref in state_refs:
            ref[...] = jnp.zeros_like(ref)


def _gla_plan(z_ref, gk_up_ref, gk_bias_ref, norm_w_ref, o_ref, state_ref):
    z = z_ref[...]
    q = z[:, 0:128] * (GLA_DK ** -0.5)
    k = z[:, 128:256]
    v = z[:, 256:512]
    g = z[:, 512:768]
    gate_in = _dot_3pass(z[:, 768:896], gk_up_ref[...]) + gk_bias_ref[...]
    log_f = _log_sigmoid(gate_in) * (1.0 / GLA_GATE_NORMALIZER)
    return _gated_linear_attention_plan(q, k, v, log_f, g, norm_w_ref[...], o_ref, state_ref)


def _hgrn_plan(z_ref, lb_ref, log_lb_ref, norm_w_ref, o_ref, state_ref):
    z = z_ref[...]
    q = _silu(z[:, 0:256])
    f = z[:, 256:512]
    v = z[:, 512:768]
    g = z[:, 768:1024]
    lb = lb_ref[...]
    a = log_lb_ref[...]
    c = jnp.log1p(-lb) + _log_sigmoid(f)
    log_f = jnp.maximum(a, c) + _softplus_neg_abs(a - c)
    k = (1.0 - lb) * _sigmoid(-f)
    return _gated_linear_attention_plan(q, k, v, log_f, g, norm_w_ref[...], o_ref, state_ref)


class _MixerParts(NamedTuple):
    plan: Callable
    args: list
    in_specs: list
    out_specs: list
    out_shape: list
    scratch: list


def _mixers_kernel(layout, *refs):
    n_in = sum(entry[1] for entry in layout)
    n_out = sum(entry[2] for entry in layout)
    ins, outs, scratch = list(refs[:n_in]), list(refs[n_in:n_in + n_out]), list(refs[n_in + n_out:])
    _zero_at_first_step(*scratch)
    plans = []
    for plan, n_i, n_o, n_s in layout:
        plans.append(plan(*ins[:n_i], *outs[:n_o], *scratch[:n_s]))
        del ins[:n_i], outs[:n_o], scratch[:n_s]
    _run_plans(plans)


def _call_mixers(parts_list, tokens, step, name):
    layout = tuple((p.plan, len(p.args), len(p.out_shape), len(p.scratch)) for p in parts_list)
    flat = lambda field: [item for p in parts_list for item in getattr(p, field)]
    return pl.pallas_call(
        functools.partial(_mixers_kernel, layout),
        grid=(tokens // step,),
        in_specs=flat("in_specs"),
        out_specs=flat("out_specs"),
        out_shape=flat("out_shape"),
        scratch_shapes=flat("scratch"),
        compiler_params=_params(("arbitrary",)),
        name=name,
    )(*flat("args"))


def _mix_out(tokens, step):
    return (pl.BlockSpec((step, GROUP_WIDTH), lambda i: (i, 0)),
            jax.ShapeDtypeStruct((tokens, GROUP_WIDTH), MIX_DTYPE))


def _gla_parts(z, gk_up, gk_bias, norm_w, step):
    out_spec, out_shape = _mix_out(z.shape[0], step)
    gk_up_pad = jnp.zeros((LANES, N_HEADS * GLA_DK), F32).at[:GLA_GATE_RANK].set(gk_up)
    return _MixerParts(
        plan=_gla_plan,
        args=[z, gk_up_pad, gk_bias.reshape(1, -1), jnp.tile(norm_w, N_HEADS).reshape(1, -1)],
        in_specs=[pl.BlockSpec((step, GLA_W), lambda i: (i, 0)),
                  _resident((LANES, N_HEADS * GLA_DK)), _resident((1, N_HEADS * GLA_DK)),
                  _resident((1, GROUP_WIDTH))],
        out_specs=[out_spec], out_shape=[out_shape],
        scratch=[pltpu.VMEM((GROUP_WIDTH, N_HEADS * GLA_DK), F32)])


def _hgrn_parts(z, lb, norm_w, step):
    out_spec, out_shape = _mix_out(z.shape[0], step)
    return _MixerParts(
        plan=_hgrn_plan,
        args=[z, lb.reshape(1, -1), jnp.log(lb).reshape(1, -1), jnp.tile(norm_w, N_HEADS).reshape(1, -1)],
        in_specs=[pl.BlockSpec((step, HGRN_W), lambda i: (i, 0)),
                  _resident((1, GROUP_WIDTH)), _resident((1, GROUP_WIDTH)), _resident((1, GROUP_WIDTH))],
        out_specs=[out_spec], out_shape=[out_shape],
        scratch=[pltpu.VMEM((GROUP_WIDTH, GROUP_WIDTH), F32)])


def _gla_mixer(z, gk_up, gk_bias, norm_w):
    return _call_mixers([_gla_parts(z, gk_up, gk_bias, norm_w, GLA_STEP)], z.shape[0], GLA_STEP, "gla_mixer")[0]


def _hgrn_mixer(z, lb, norm_w):
    return _call_mixers([_hgrn_parts(z, lb, norm_w, GLA_STEP)], z.shape[0], GLA_STEP, "hgrn_mixer")[0]


def _swa_block(q, kw, vw, visible, sinks, out_box):
    n = q.shape[0]
    head_cols = lambda x, h: x[:, h * HEAD_DIM:(h + 1) * HEAD_DIM]
    v_lane_head = _idiv(_iota(vw.shape, 1), HEAD_DIM)
    outs = []
    for kv in range(N_HEADS // 2):
        heads = (2 * kv, 2 * kv + 1)
        scores = _bdot_nt(jnp.concatenate([head_cols(q, h) for h in heads], axis=0), head_cols(kw, kv))
        yield
        probs, sink_terms = [], []
        for half, h in enumerate(heads):
            s = jnp.where(visible, scores[half * n:(half + 1) * n], -jnp.inf)
            sink = sinks[:, h:h + 1]
            m = jnp.maximum(jnp.max(s, axis=-1, keepdims=True), sink)
            probs.append(jnp.exp(s - m))
            sink_terms.append(jnp.exp(sink - m))
        o = _bdot(jnp.concatenate(probs, axis=0), jnp.where(v_lane_head == kv, vw, 1.0))
        yield
        sums = head_cols(o, 1 - kv)[:, 0:1]
        outs += [head_cols(o, kv)[half * n:(half + 1) * n] / (sums[half * n:(half + 1) * n] + sink_terms[half])
                 for half in range(2)]
    out_box.append(jnp.concatenate(outs, axis=-1))


def _swa_plan(q_ref, k_ref, v_ref, kp_ref, vp_ref, sink_ref, o_ref):
    w = SWA_WINDOW
    has_prev = pl.program_id(0) > 0
    q = q_ref[...] * (HEAD_DIM ** -0.5)
    k_all = jnp.concatenate([kp_ref[...], k_ref[...]], axis=0)
    v_all = jnp.concatenate([vp_ref[...], v_ref[...]], axis=0)
    q_pos = _iota((w, 2 * w), 0) + w
    k_pos = _iota((w, 2 * w), 1)
    dist = q_pos - k_pos
    in_window = (dist >= 0) & (dist < w)
    sinks = sink_ref[...]
    blocks = []
    stages = []
    for b in range(q.shape[0] // w):
        visible = in_window if b > 0 else in_window & ((k_pos >= w) | has_prev)
        stages.append(_swa_block(q[b * w:(b + 1) * w], k_all[b * w:(b + 2) * w], v_all[b * w:(b + 2) * w],
                                 visible, sinks, blocks))
    def finish():
        o_ref[...] = jnp.concatenate(blocks, axis=0).astype(o_ref.dtype)

    return stages, SWA_STAGGER, finish


def _swa_parts(z, sinks, step):
    out_spec, out_shape = _mix_out(z.shape[0], step)
    w = SWA_WINDOW
    prev = lambda col: (lambda i: (jnp.maximum(i * (step // w) - 1, 0), col))
    return _MixerParts(
        plan=_swa_plan,
        args=[z, z, z, z, z, sinks.reshape(1, -1)],
        in_specs=[pl.BlockSpec((step, 256), lambda i: (i, 0)),
                  pl.BlockSpec((step, 128), lambda i: (i, 2)), pl.BlockSpec((step, 128), lambda i: (i, 3)),
                  pl.BlockSpec((w, 128), prev(2)), pl.BlockSpec((w, 128), prev(3)),
                  _resident((1, N_HEADS))],
        out_specs=[out_spec], out_shape=[out_shape], scratch=[])


def _swa_mixer(z, sinks):
    return _call_mixers([_swa_parts(z, sinks, SWA_TILE)], z.shape[0], SWA_TILE, "swa_mixer")[0]


def _run_plans(plans):
    live = [dict(enumerate(stages)) for stages, _, _ in plans]
    rnd = 0
    while any(live):
        for group, (_, stagger, _) in zip(live, plans):
            for n in sorted(group):
                if rnd >= n * stagger:
                    try:
                        next(group[n])
                    except StopIteration:
                        del group[n]
        rnd += 1
    for _, _, finish in plans:
        finish()


def _rwkv_chunk(index, r, k, v, a_vec, b_vec, log_w, state_box, out_box):
    c = r.shape[0]
    width = r.shape[1]
    tri = jnp.where(_iota((c, c), 1) <= _iota((c, c), 0), 1.0, 0.0).astype(F32)
    p = _dot_mask(tri, log_w)
    yield
    p_total = p[c - 1:c, :]
    decay_in = jnp.exp(p)
    decay_out = jnp.exp(-p)
    decay_end = jnp.exp(p_total - p)
    a_in = a_vec * jnp.exp(p - log_w)
    r_in = r * decay_in
    b_out = b_vec * decay_out
    k_out = k * decay_out
    b_end = b_vec * decay_end
    k_end = k * decay_end

    t_pos = _iota((c, width), 0)
    assert width == N_HEADS * c
    s_pos = _imod(_iota((c, width), 1), c)
    strict = s_pos < t_pos
    incl = s_pos <= t_pos
    expand = lambda x: _expand_heads(x, HEAD_DIM)

    scores = _bdot_nt(jnp.concatenate([a_in, r_in], axis=0),
                      jnp.concatenate([expand(b_out), expand(k_out)], axis=0))
    yield
    a_ab = jnp.where(strict, scores[0:c, 0:width], 0.0)
    a_ak = jnp.where(strict, scores[0:c, width:2 * width], 0.0)
    a_rb = jnp.where(incl, scores[c:2 * c, 0:width], 0.0)
    a_rk = jnp.where(incl, scores[c:2 * c, width:2 * width], 0.0)

    t_inv = jnp.where(s_pos == t_pos, 1.0, 0.0) + a_ab
    from_v = _bdot(jnp.concatenate([a_ak, a_rk], axis=0), expand(v))
    x1 = from_v[0:c]
    y_from_v = from_v[c:2 * c]
    power = _bdot(a_ab, expand(a_ab))
    yield
    n_factors = (c - 1).bit_length()
    for _ in range(n_factors - 2):
        both = _bdot(jnp.concatenate([t_inv, power], axis=0), expand(power))
        t_inv = t_inv + both[0:c]
        power = both[c:2 * c]
        yield
    t_inv = t_inv + _bdot(t_inv, expand(power))
    yield
    sol = _bdot(t_inv, jnp.concatenate([expand(x1), expand(a_in)], axis=1))
    yield
    u0 = sol[:, 0:width]
    w_mat = sol[:, width:2 * width]

    assert len(out_box) == index, "the previous chunk must have replaced the state before it is read"
    state = state_box[0]
    from_state = _bdot_nt(jnp.concatenate([w_mat, r_in], axis=0), state)
    yield
    u = u0 + from_state[0:c]
    y = _bdot(a_rb, expand(u)) + y_from_v + from_state[c:2 * c]
    upd = _dot_tn(jnp.concatenate([u, v], axis=0).astype(BF16),
                  jnp.concatenate([b_end, k_end], axis=0).astype(BF16))
    same_head = _idiv(_iota((width, width), 0), HEAD_DIM) == _idiv(_iota((width, width), 1), HEAD_DIM)
    state_box[0] = state * jnp.exp(p_total) + jnp.where(same_head, upd, 0.0)
    out_box.append(y)


def _rwkv_plan(has_vres, *refs):
    if has_vres:
        (z_ref, zp_ref, vfirst_ref, mu_ref, w0_ref, wup_ref, a0_ref, aup_ref, gup_ref, kk_ref, ka_ref,
         rk_ref, lnw_ref, lnb_ref, v0_ref, vup_ref, o_ref, state_ref) = refs
    else:
        (z_ref, zp_ref, mu_ref, w0_ref, wup_ref, a0_ref, aup_ref, gup_ref, kk_ref, ka_ref,
         rk_ref, lnw_ref, lnb_ref, o_ref, vout_ref, state_ref) = refs
    step = pl.program_id(0)
    z = z_ref[...]
    last_prev = jnp.where(step > 0, zp_ref[7:8, :], 0.0)
    prev = jnp.where(_iota(z.shape, 0) == 0, last_prev, pltpu.roll(z, 1, axis=0))
    zr = z + (prev - z) * mu_ref[...]
    r = zr[:, 0:256]
    k = zr[:, 256:512]
    v = zr[:, 512:768]
    low = zr[:, 768:896]
    w_pre = w0_ref[...] + _dot_3pass(jnp.tanh(low), wup_ref[...])
    w_log = -(jnp.maximum(-w_pre, 0.0) + _softplus_neg_abs(w_pre)) - 0.5
    log_w = -jnp.exp(w_log)
    a = _sigmoid(a0_ref[...] + _dot_3pass(low, aup_ref[...]))
    g = _dot_3pass(_sigmoid(low), gup_ref[...])
    if has_vres:
        v = v + (vfirst_ref[...] - v) * _sigmoid(v0_ref[...] + _dot_3pass(low, vup_ref[...]))
    else:
        vout_ref[...] = v
    head_sum = _head_group_matrix(GROUP_WIDTH, HEAD_DIM, 1.0).astype(BF16)
    kk = k * kk_ref[...]
    kk = kk / jnp.maximum(jnp.sqrt(_dot_hilo(kk * kk, head_sum)), 1e-12)
    k = k * (1.0 + (a - 1.0) * ka_ref[...])
    a_vec = -kk
    b_vec = kk * a

    c = RWKV_CHUNK
    state_box = [state_ref[...]]
    chunks = []
    stages = []
    for n in range(z.shape[0] // c):
        rows = slice(n * c, (n + 1) * c)
        stages.append(_rwkv_chunk(n, r[rows], k[rows], v[rows], a_vec[rows], b_vec[rows], log_w[rows],
                                  state_box, chunks))
    def finish():
        state_ref[...] = state_box[0]
        y = jnp.concatenate(chunks, axis=0)
        head_mean = _head_group_matrix(GROUP_WIDTH, HEAD_DIM, 1.0 / HEAD_DIM).astype(BF16)
        mu_y = _dot_hilo(y, head_mean)
        d = y - mu_y
        var_y = _dot_hilo(d * d, head_mean)
        y = d * lax.rsqrt(var_y + RWKV_GN_EPS) * lnw_ref[...] + lnb_ref[...]
        bonus = _dot_hilo(r * k * rk_ref[...], head_sum) * v
        o_ref[...] = ((y + bonus) * g).astype(o_ref.dtype)

    return stages, RWKV_STAGGER, finish


def _rwkv_mixer(z, mu, w0, w_up, a0, a_up, g_up, k_k, k_a, r_k, lnx_w, lnx_b, vres):
    parts = _rwkv_parts(z, mu, w0, w_up, a0, a_up, g_up, k_k, k_a, r_k, lnx_w, lnx_b, vres, RWKV_TILE)
    outs = _call_mixers([parts], z.shape[0], RWKV_TILE, "rwkv_mixer")
    return outs[0] if vres is not None else outs


def _rwkv_parts(z, mu, w0, w_up, a0, a_up, g_up, k_k, k_a, r_k, lnx_w, lnx_b, vres, step):
    t = z.shape[0]
    c = step
    row = lambda a: a.reshape(1, -1)
    low_rows = lambda a, start: jnp.zeros((LANES, GROUP_WIDTH), F32).at[start:start + a.shape[0]].set(a)
    has_vres = vres is not None
    mu_full = jnp.zeros((RWKV_W,), F32).at[:RWKV_COLS].set(mu)
    vec = _resident((1, GROUP_WIDTH))
    mat = _resident((LANES, GROUP_WIDTH))
    tile = pl.BlockSpec((c, GROUP_WIDTH), lambda i: (i, 0))
    z_specs = [pl.BlockSpec((c, RWKV_W), lambda i: (i, 0)),
               pl.BlockSpec((8, RWKV_W), lambda i: (jnp.maximum(i * (c // 8) - 1, 0), 0))]
    common = [row(w0), low_rows(w_up, 0), row(a0), low_rows(a_up, 16), low_rows(g_up, 32),
              row(k_k), row(k_a), row(r_k), row(lnx_w), row(lnx_b)]
    common_specs = [vec, mat, vec, mat, mat, vec, vec, vec, vec, vec]
    if has_vres:
        v_first, vres_mu, v0, v_up = vres
        mu_full = mu_full.at[RWKV_COLS:RWKV_COLS + RWKV_V_RANK].set(vres_mu)
        args = [z, z, v_first, row(mu_full)] + common + [row(v0), low_rows(v_up, 64)]
        in_specs = z_specs + [tile, _resident((1, RWKV_W))] + common_specs + [vec, mat]
        out_specs = [tile]
        out_shape = [jax.ShapeDtypeStruct((t, GROUP_WIDTH), MIX_DTYPE)]
    else:
        args = [z, z, row(mu_full)] + common
        in_specs = z_specs + [_resident((1, RWKV_W))] + common_specs
        out_specs = [tile, tile]
        out_shape = [jax.ShapeDtypeStruct((t, GROUP_WIDTH), MIX_DTYPE), jax.ShapeDtypeStruct((t, GROUP_WIDTH), F32)]
    return _MixerParts(plan=functools.partial(_rwkv_plan, has_vres), args=args, in_specs=in_specs,
                       out_specs=out_specs, out_shape=out_shape,
                       scratch=[pltpu.VMEM((GROUP_WIDTH, GROUP_WIDTH), F32)])


def _store_token_tiles(ref, index, x):
    n = x.shape[0]
    for j in range(SUBLANES):
        ref[(*index, pl.ds(j, n, stride=SUBLANES), slice(None))] = x[:, j * LANES:(j + 1) * LANES]


def _load_token_tiles(ref, index, n):
    return jnp.concatenate([ref[(*index, pl.ds(j, n, stride=SUBLANES), slice(None))] for j in range(SUBLANES)],
                           axis=-1)


def _top2_route(logits):
    row = _iota(logits.shape, 0).astype(F32)
    m1 = jnp.max(logits, axis=0, keepdims=True)
    i1 = jnp.min(jnp.where(logits == m1, row, N_EXPERTS), axis=0, keepdims=True)
    rest = jnp.where(row == i1, -jnp.inf, logits)
    m2 = jnp.max(rest, axis=0, keepdims=True)
    i2 = jnp.min(jnp.where(rest == m2, row, N_EXPERTS), axis=0, keepdims=True)
    e2 = jnp.exp(m2 - m1)
    g1 = 1.0 / (1.0 + e2)
    g2 = e2 * g1
    return jnp.where(row == 0, i1, jnp.where(row == 1, i2, jnp.where(row == 2, g1, jnp.where(row == 3, g2, 0.0))))


def _mix_residual_norm(alpha, x_ref, mix_refs, w_ref, g_ref, b_ref):
    acc = alpha * x_ref[...]
    for h, ref in enumerate(mix_refs):
        acc = acc + _dot(ref[...], w_ref[h * GROUP_WIDTH:(h + 1) * GROUP_WIDTH, :])
    return _layer_norm(acc, g_ref[...], b_ref[...])


def _outproj_route_kernel(alpha, x_ref, o0_ref, o1_ref, o2_ref, o3_ref, w_ref, g_ref, b_ref, router_ref,
                          tiles_ref, route_ref):
    y = _mix_residual_norm(alpha, x_ref, (o0_ref, o1_ref, o2_ref, o3_ref), w_ref, g_ref, b_ref)
    route_ref[...] = _top2_route(_dot_3pass_nt(router_ref[...], y))
    _store_token_tiles(tiles_ref, (), y)


def _out_proj_ln_route(alpha, x, mixes, w_out, g, b, router):
    t = x.shape[0]
    row_d = pl.BlockSpec((ROW_TILE, D_MODEL), lambda i: (i, 0))
    row_g = pl.BlockSpec((ROW_TILE, GROUP_WIDTH), lambda i: (i, 0))
    return pl.pallas_call(
        functools.partial(_outproj_route_kernel, alpha),
        grid=(t // ROW_TILE,),
        in_specs=[row_d, row_g, row_g, row_g, row_g, _resident((D_MODEL, D_MODEL)),
                  _resident((1, D_MODEL)), _resident((1, D_MODEL)), _resident((N_EXPERTS, D_MODEL))],
        out_specs=[pl.BlockSpec((ROW_TILE * SUBLANES, LANES), lambda i: (i, 0)),
                   pl.BlockSpec((N_EXPERTS, ROW_TILE), lambda i: (0, i))],
        out_shape=[jax.ShapeDtypeStruct((t * SUBLANES, LANES), F32), jax.ShapeDtypeStruct((N_EXPERTS, t), F32)],
        compiler_params=_params(("parallel",)),
        name="out_proj_ln_route",
    )(x, *mixes, w_out.astype(BF16), g.reshape(1, -1), b.reshape(1, -1), router.T)


def _ln_embed(y, ln_g, ln_b, p, ple_gate, ple_proj):
    x = _layer_norm(y, ln_g, ln_b)
    gate = _sigmoid(_dot(x.astype(BF16), ple_gate))
    return x + gate * _dot(p.astype(BF16), ple_proj)


def _dense_ffn_kernel(alpha, x_ref, o0_ref, o1_ref, o2_ref, o3_ref, wo_ref, g1_ref, b1_ref, p_ref, wg_ref, wu_ref,
                      wd_ref, g_ref, b_ref, pg_ref, pp_ref, y_ref, acc_ref):
    x = _mix_residual_norm(alpha, x_ref, (o0_ref, o1_ref, o2_ref, o3_ref), wo_ref, g1_ref, b1_ref)
    xb = x.astype(BF16)
    acc_ref[...] = alpha * x
    for j in range(D_FF // FF_CHUNK):
        cols = slice(j * FF_CHUNK, (j + 1) * FF_CHUNK)
        h = _silu(_dot(xb, wg_ref[:, cols])) * _dot(xb, wu_ref[:, cols])
        acc_ref[...] += _dot(h.astype(BF16), wd_ref[cols, :])
    y_ref[...] = _ln_embed(acc_ref[...], g_ref[...], b_ref[...], p_ref[...], pg_ref[...], pp_ref[...])


def _dense_layer_tail(alpha, x, mixes, w_out, g1, b1, p_all, p_block, w_gate, w_up, w_down, g, b, ple_gate, ple_proj):
    t = x.shape[0]
    row_d = pl.BlockSpec((ROW_TILE, D_MODEL), lambda i: (i, 0))
    row_g = pl.BlockSpec((ROW_TILE, GROUP_WIDTH), lambda i: (i, 0))
    return pl.pallas_call(
        functools.partial(_dense_ffn_kernel, alpha),
        grid=(t // ROW_TILE,),
        in_specs=[row_d, row_g, row_g, row_g, row_g, _resident((D_MODEL, D_MODEL)),
                  _resident((1, D_MODEL)), _resident((1, D_MODEL)),
                  pl.BlockSpec((ROW_TILE, PLE_DIM), lambda i: (p_block + i, 0)),
                  _resident((D_MODEL, D_FF)), _resident((D_MODEL, D_FF)), _resident((D_FF, D_MODEL)),
                  _resident((1, D_MODEL)), _resident((1, D_MODEL)),
                  _resident((D_MODEL, D_MODEL)), _resident((PLE_DIM, D_MODEL))],
        out_specs=row_d,
        out_shape=jax.ShapeDtypeStruct((t, D_MODEL), F32),
        scratch_shapes=[pltpu.VMEM((ROW_TILE, D_MODEL), F32)],
        compiler_params=_params(("parallel",)),
        name="dense_ffn_tail",
    )(x, *mixes, w_out.astype(BF16), g1.reshape(1, -1), b1.reshape(1, -1), p_all,
      w_gate.astype(BF16), w_up.astype(BF16), w_down.astype(BF16), g.reshape(1, -1), b.reshape(1, -1),
      ple_gate.astype(BF16), ple_proj.astype(BF16))


def _weight_group_copies(e, group, wg_hbm, wu_hbm, wd_hbm, stage_cols_ref, stage_rows_ref, sem):
    slot = group % 2
    lo, hi = group * MOE_WEIGHT_CHUNK, (group + 1) * MOE_WEIGHT_CHUNK
    return (pltpu.make_async_copy(wg_hbm.at[e, :, lo:hi], stage_cols_ref.at[slot, 0], sem.at[slot]),
            pltpu.make_async_copy(wu_hbm.at[e, :, lo:hi], stage_cols_ref.at[slot, 1], sem.at[slot]),
            pltpu.make_async_copy(wd_hbm.at[e, lo:hi, :], stage_rows_ref.at[slot], sem.at[slot]))


def _expert_kernel(row_tok_ref, row_dst_ref, block_e_ref, n_used_ref, x_hbm, wg_hbm, wu_hbm, wd_hbm, y_hbm,
                   rows_ref, xb_ref, acc_ref, ybuf_ref, wg_ref, wu_ref, wd_ref, stage_cols_ref, stage_rows_ref,
                   gather_sem, scatter_sem, weight_sem):
    i = pl.program_id(0)
    n_used = n_used_ref[0]
    last = pl.num_programs(0) - 1
    n = MOE_ROWS
    slot = lax.rem(i, 2)
    other = 1 - slot
    tile = lambda first_row: pl.ds(pl.multiple_of(first_row, SUBLANES), SUBLANES)
    gather_row = lambda tok_row, s, r: pltpu.make_async_copy(
        x_hbm.at[tile(tok_row)], rows_ref.at[s, tile(r * SUBLANES)], gather_sem.at[s])
    scatter_row = lambda dst_row, s, r: pltpu.make_async_copy(
        ybuf_ref.at[s, tile(r * SUBLANES)], y_hbm.at[tile(dst_row)], scatter_sem.at[s])
    block_rows = n * SUBLANES
    gather_block = lambda s: pltpu.make_async_copy(x_hbm.at[pl.ds(0, block_rows)], rows_ref.at[s], gather_sem.at[s])
    scatter_block = lambda s: pltpu.make_async_copy(ybuf_ref.at[s], y_hbm.at[pl.ds(0, block_rows)],
                                                    scatter_sem.at[s])

    @pl.when(i == 0)
    def _():
        ybuf_ref[1] = jnp.zeros((block_rows, LANES), F32)

        def start(r, carry):
            gather_row(row_tok_ref[r], 0, r).start()
            return carry

        lax.fori_loop(0, n, start, 0)

    def scatter_all(first_dst, s):
        def start(r, carry):
            scatter_row(row_dst_ref[first_dst + r], s, r).start()
            return carry

        lax.fori_loop(0, n, start, 0)
        scatter_block(s).wait()

    expert = block_e_ref[i]
    new_expert = (i == 0) | (expert != block_e_ref[jnp.maximum(i - 1, 0)])

    weight_group = lambda g: _weight_group_copies(expert, g, wg_hbm, wu_hbm, wd_hbm, stage_cols_ref,
                                                  stage_rows_ref, weight_sem)
    chunks_per_group = MOE_WEIGHT_CHUNK // FF_CHUNK
    n_groups = D_FF_EXPERT // MOE_WEIGHT_CHUNK

    def take_weight_group(g):
        if g + 1 < n_groups:
            for copy in weight_group(g + 1):
                copy.start()
        for copy in weight_group(g):
            copy.wait()
        piece = slice(g * MOE_WEIGHT_CHUNK, (g + 1) * MOE_WEIGHT_CHUNK)
        wg_ref[:, piece] = stage_cols_ref[g % 2, 0].astype(BF16)
        wu_ref[:, piece] = stage_cols_ref[g % 2, 1].astype(BF16)
        wd_ref[piece, :] = stage_rows_ref[g % 2].astype(BF16)

    @pl.when(i < n_used)
    def _():
        @pl.when(new_expert)
        def _():
            for copy in weight_group(0):
                copy.start()

        gather_block(slot).wait()
        xb_ref[...] = _load_token_tiles(rows_ref, (slot,), n).astype(BF16)
        n_chunks = D_FF_EXPERT // FF_CHUNK
        rows_per_chunk = -(-n // (n_chunks - MOE_DMA_FREE_CHUNKS))
        for j in range(n_chunks):
            if j % chunks_per_group == 0:
                pl.when(new_expert)(functools.partial(take_weight_group, j // chunks_per_group))
            cols = slice(j * FF_CHUNK, (j + 1) * FF_CHUNK)
            xb = xb_ref[...]
            h = _silu(_dot(xb, wg_ref[:, cols])) * _dot(xb, wu_ref[:, cols])
            part = _dot(h.astype(BF16), wd_ref[cols, :])
            if j == 0:
                acc_ref[...] = part
            else:
                acc_ref[...] += part
            for r in range(j * rows_per_chunk, min((j + 1) * rows_per_chunk, n)):
                gather_row(row_tok_ref[(i + 1) * n + r], other, r).start(priority=GATHER_DMA_PRIORITY)
                scatter_row(row_dst_ref[i * n + r], other, r).start(priority=SCATTER_DMA_PRIORITY)
        _store_token_tiles(ybuf_ref, (slot,), acc_ref[...])
        scatter_block(other).wait()

    @pl.when(i == n_used)
    def _():
        gather_block(slot).wait()
        scatter_all(i * n, other)

    @pl.when(i >= n_used)
    def _():
        rows_ref[slot] = jnp.zeros((block_rows, LANES), F32)
        fill = pltpu.make_async_copy(rows_ref.at[slot], y_hbm.at[pl.ds(i * block_rows, block_rows)],
                                     scatter_sem.at[slot])
        fill.start()
        fill.wait()

    @pl.when((i == last) & (i < n_used))
    def _():
        gather_block(other).wait()
        scatter_all((i + 1) * n, slot)


def _expert_rows(x, row_tok, row_dst, block_e, n_used, n_out_rows, w_gate, w_up, w_down):
    n_blocks = block_e.shape[0]
    hbm = pl.BlockSpec(memory_space=pl.ANY)
    grid_spec = pltpu.PrefetchScalarGridSpec(
        num_scalar_prefetch=4,
        grid=(n_blocks,),
        in_specs=[hbm, hbm, hbm, hbm],
        out_specs=hbm,
        scratch_shapes=[pltpu.VMEM((2, MOE_ROWS * SUBLANES, LANES), F32), pltpu.VMEM((MOE_ROWS, D_MODEL), BF16),
                        pltpu.VMEM((MOE_ROWS, D_MODEL), F32), pltpu.VMEM((2, MOE_ROWS * SUBLANES, LANES), F32),
                        pltpu.VMEM((D_MODEL, D_FF_EXPERT), BF16), pltpu.VMEM((D_MODEL, D_FF_EXPERT), BF16),
                        pltpu.VMEM((D_FF_EXPERT, D_MODEL), BF16),
                        pltpu.VMEM((2, 2, D_MODEL, MOE_WEIGHT_CHUNK), F32),
                        pltpu.VMEM((2, MOE_WEIGHT_CHUNK, D_MODEL), F32),
                        pltpu.SemaphoreType.DMA((2,)), pltpu.SemaphoreType.DMA((2,)),
                        pltpu.SemaphoreType.DMA((2,))],
    )
    return pl.pallas_call(
        _expert_kernel,
        grid_spec=grid_spec,
        out_shape=jax.ShapeDtypeStruct((n_out_rows * SUBLANES, LANES), F32),
        compiler_params=_params(("arbitrary",)),
        name="moe_experts",
    )(row_tok, row_dst, block_e, n_used.reshape(1), x, w_gate, w_up, w_down)


def _combine_kernel(alpha, x_ref, y0_ref, y1_ref, gates_ref, p_ref, g_ref, b_ref, pg_ref, pp_ref, o_ref):
    n = ROW_TILE
    gates = gates_ref[...]
    f = _load_token_tiles(y0_ref, (), n) * gates[:, 0:1] + _load_token_tiles(y1_ref, (), n) * gates[:, 1:2]
    o_ref[...] = _ln_embed(alpha * _load_token_tiles(x_ref, (), n) + f, g_ref[...], b_ref[...], p_ref[...],
                           pg_ref[...], pp_ref[...])


def _moe_combine_tail(alpha, x_tiles, p_all, p_block, y_tiles, gates, g, b, ple_gate, ple_proj):
    t = x_tiles.shape[0] // SUBLANES
    n = ROW_TILE
    row_d = pl.BlockSpec((n, D_MODEL), lambda i: (i, 0))
    tiles_d = pl.BlockSpec((n * SUBLANES, LANES), lambda i: (i, 0))
    return pl.pallas_call(
        functools.partial(_combine_kernel, alpha),
        grid=(t // n,),
        in_specs=[tiles_d, tiles_d, pl.BlockSpec((n * SUBLANES, LANES), lambda i: (t // n + i, 0)),
                  pl.BlockSpec((n, 2), lambda i: (i, 0)), pl.BlockSpec((n, PLE_DIM), lambda i: (p_block + i, 0)),
                  _resident((1, D_MODEL)), _resident((1, D_MODEL)),
                  _resident((D_MODEL, D_MODEL)), _resident((PLE_DIM, D_MODEL))],
        out_specs=row_d,
        out_shape=jax.ShapeDtypeStruct((t, D_MODEL), F32),
        compiler_params=_params(("parallel",)),
        name="moe_combine_tail",
    )(x_tiles, y_tiles, y_tiles, gates, p_all, g.reshape(1, -1), b.reshape(1, -1), ple_gate.astype(BF16),
      ple_proj.astype(BF16))


def _moe_tail(alpha, x_tiles, route, p_all, p_block, w_gate, w_up, w_down, g, b, ple_gate, ple_proj):
    t = route.shape[1]
    experts = route[0:2].T.astype(jnp.int32)
    gates = route[2:4].T
    e_flat = experts.reshape(-1)
    onehot = (e_flat[:, None] == jnp.arange(N_EXPERTS, dtype=jnp.int32)[None, :]).astype(jnp.int32)
    running = jnp.cumsum(onehot, axis=0)
    rank = jnp.sum(onehot * (running - 1), axis=1)
    counts = running[-1]
    padded = (counts + MOE_ROWS - 1) // MOE_ROWS * MOE_ROWS
    pad_end = jnp.cumsum(padded)
    pad_start = pad_end - padded
    dest = (pad_start[e_flat] + rank).astype(jnp.int32)
    n_blocks = (2 * t) // MOE_ROWS + N_EXPERTS
    n_rows = n_blocks * MOE_ROWS
    assign = jnp.arange(2 * t, dtype=jnp.int32)
    row_assign = jnp.full((n_rows,), -1, jnp.int32).at[dest].set(assign)
    used = row_assign >= 0
    spare = 2 * t + jnp.cumsum(jnp.where(used, 0, 1).astype(jnp.int32)) - 1
    row_tok = jnp.where(used, row_assign // 2, 0)
    row_dst = jnp.where(used, (row_assign % 2) * t + row_assign // 2, spare)
    n_spare = n_rows - 2 * t
    first_dst = 2 * t + n_spare + jnp.arange(MOE_ROWS, dtype=jnp.int32)
    row_tok = jnp.concatenate([row_tok, jnp.zeros((MOE_ROWS,), jnp.int32)])
    row_dst = jnp.concatenate([first_dst, row_dst])
    block_start = jnp.arange(n_blocks, dtype=jnp.int32) * MOE_ROWS
    block_e = jnp.minimum(jnp.sum((block_start[:, None] >= pad_end[None, :]).astype(jnp.int32), axis=1),
                          N_EXPERTS - 1)
    n_used = (pad_end[-1] // MOE_ROWS).astype(jnp.int32)
    y_tiles = _expert_rows(x_tiles, row_tok * SUBLANES, row_dst * SUBLANES, block_e, n_used,
                           2 * t + n_spare + MOE_ROWS, w_gate, w_up, w_down)
    return _moe_combine_tail(alpha, x_tiles, p_all, p_block, y_tiles, gates, g, b, ple_gate, ple_proj)


def kernel(x, p, w_in, w_out, gla_gk_up, gla_gk_bias, gla_norm_w, hgrn_lower_bounds, hgrn_norm_w, swa_sinks, rwkv_mu, rwkv_w0, rwkv_w_up, rwkv_a0, rwkv_a_up, rwkv_g_up, rwkv_k_k, rwkv_k_a, rwkv_r_k, rwkv_lnx_w, rwkv_lnx_b, rwkv_vres_down, rwkv_vres_mu, rwkv_v0, rwkv_vres_up, ln1_g, ln1_b, ln2_g, ln2_b, ffn_w_gate, ffn_w_up, ffn_w_down, moe_router, moe_w_gate, moe_w_up, moe_w_down, ple_proj, ple_gate):
    bsz, seq, d = x.shape
    depth = w_in.shape[0]
    alpha = (2.0 * depth) ** 0.25
    lbs = jnp.cumsum(jax.nn.softmax(hgrn_lower_bounds.astype(F32), axis=0), axis=0)
    lbs = lbs - lbs[0]
    p_all = p.reshape(-1, PLE_DIM)
    outs = []
    for bi in range(bsz):
        xt = x[bi]
        v_first = None
        for i in range(depth):
            w = _group_in_weights(w_in[i], None if i == 0 else rwkv_vres_down[i - 1])
            z_gla, z_hgrn, z_swa, z_rwkv = _in_proj(xt, w)
            vres = None if i == 0 else (v_first, rwkv_vres_mu[i - 1], rwkv_v0[i - 1], rwkv_vres_up[i - 1])
            mixed = _call_mixers(
                [_gla_parts(z_gla, gla_gk_up[i], gla_gk_bias[i], gla_norm_w[i], MIX_STEP),
                 _hgrn_parts(z_hgrn, lbs[i], hgrn_norm_w[i], MIX_STEP),
                 _swa_parts(z_swa, swa_sinks[i], MIX_STEP),
                 _rwkv_parts(z_rwkv, rwkv_mu[i], rwkv_w0[i], rwkv_w_up[i], rwkv_a0[i], rwkv_a_up[i],
                             rwkv_g_up[i], rwkv_k_k[i], rwkv_k_a[i], rwkv_r_k[i].reshape(-1),
                             rwkv_lnx_w[i], rwkv_lnx_b[i], vres, MIX_STEP)],
                seq, MIX_STEP, "mixers")
            mixes = tuple(mixed[0:4])
            if i == 0:
                v_first = mixed[4]
            j = i // 2
            p_block = (i * bsz + bi) * (seq // ROW_TILE)
            if i % 2 == 0:
                xt = _dense_layer_tail(alpha, xt, mixes, w_out[i], ln1_g[i], ln1_b[i], p_all, p_block,
                                       ffn_w_gate[j], ffn_w_up[j], ffn_w_down[j], ln2_g[i], ln2_b[i],
                                       ple_gate[i], ple_proj[i])
            else:
                x_tiles, route = _out_proj_ln_route(alpha, xt, mixes, w_out[i], ln1_g[i], ln1_b[i], moe_router[j])
                xt = _moe_tail(alpha, x_tiles, route, p_all, p_block, moe_w_gate[j], moe_w_up[j], moe_w_down[j],
                               ln2_g[i], ln2_b[i], ple_gate[i], ple_proj[i])
        outs.append(xt)
    return jnp.stack(outs, axis=0)
```

```python
import functools
from typing import Callable, NamedTuple

import jax
import jax.numpy as jnp
from jax import lax
from jax.experimental import pallas as pl
from jax.experimental.pallas import tpu as pltpu

F32 = jnp.float32
BF16 = jnp.bfloat16
HIGHEST = lax.Precision.HIGHEST
MIX_DTYPE = BF16

D_MODEL = 1024
GROUP_WIDTH = 256
N_HEADS = 4
HEAD_DIM = 64
GLA_DK = 32
GLA_GATE_RANK = 16
GLA_GATE_NORMALIZER = 16.0
SWA_WINDOW = 128
RWKV_COLS = 3 * GROUP_WIDTH + 16 + 16 + 32
RWKV_V_RANK = 8
LN_EPS = 1e-5
RMS_EPS = 1e-6
RWKV_GN_EPS = 64e-5
D_FF = 2816
N_EXPERTS = 8
D_FF_EXPERT = 3584
PLE_DIM = 256

LANES = 128
SUBLANES = 8
GLA_W = 896
HGRN_W = 1024
SWA_W = 512
RWKV_W = 896
Z_W = GLA_W + HGRN_W + SWA_W + RWKV_W

GLA_SUB = 16
GLA_TILE = 128
GLA_STEP = 1024
GLA_STAGGER = 5
SWA_STAGGER = 5
MIX_STEP = 1024
RWKV_CHUNK = 64
RWKV_TILE = 1024
RWKV_STAGGER = 2
ROW_TILE = 512
FF_CHUNK = 256
SWA_TILE = 256
MOE_ROWS = 512
MOE_WEIGHT_CHUNK = 512
MOE_DMA_FREE_CHUNKS = 10
GATHER_DMA_PRIORITY = 0
SCATTER_DMA_PRIORITY = 1
VMEM_LIMIT = 56 * 1024 * 1024


def _iota(shape, dim):
    return lax.broadcasted_iota(jnp.int32, shape, dim)


def _idiv(x, n):
    return jnp.right_shift(x, n.bit_length() - 1)


def _imod(x, n):
    return jnp.bitwise_and(x, n - 1)


def _dot(a, b, precision=None):
    return jnp.dot(a, b, preferred_element_type=F32, precision=precision)


def _dot_nt(a, b, precision=None):
    return lax.dot_general(a, b, (((1,), (1,)), ((), ())), preferred_element_type=F32, precision=precision)


def _dot_tn(a, b, precision=None):
    return lax.dot_general(a, b, (((0,), (0,)), ((), ())), preferred_element_type=F32, precision=precision)


def _bdot(a, b):
    return _dot(a.astype(BF16), b.astype(BF16))


def _bdot_nt(a, b):
    return _dot_nt(a.astype(BF16), b.astype(BF16))


def _dot_hilo(x, m):
    hi = x.astype(BF16)
    lo = (x - hi.astype(F32)).astype(BF16)
    return _dot(hi, m) + _dot(lo, m)


def _dot_mask(m, x):
    m = m.astype(BF16)
    x1 = x.astype(BF16)
    r1 = x - x1.astype(F32)
    x2 = r1.astype(BF16)
    x3 = (r1 - x2.astype(F32)).astype(BF16)
    return _dot(m, x1) + _dot(m, x2) + _dot(m, x3)


def _dot_3pass_nt(x, w):
    x_hi = x.astype(BF16)
    x_lo = (x - x_hi.astype(F32)).astype(BF16)
    w_hi = w.astype(BF16)
    w_lo = (w - w_hi.astype(F32)).astype(BF16)
    return _dot_nt(x_hi, w_hi) + _dot_nt(x_lo, w_hi) + _dot_nt(x_hi, w_lo)


def _dot_3pass(x, w):
    x_hi = x.astype(BF16)
    x_lo = (x - x_hi.astype(F32)).astype(BF16)
    w_hi = w.astype(BF16)
    w_lo = (w - w_hi.astype(F32)).astype(BF16)
    return _dot(x_hi, w_hi) + _dot(x_lo, w_hi) + _dot(x_hi, w_lo)


def _sigmoid(x):
    return 1.0 / (1.0 + jnp.exp(-x))


def _silu(x):
    return x * _sigmoid(x)


def _softplus_neg_abs(x):
    return jnp.log(1.0 + jnp.exp(-jnp.abs(x)))


def _log_sigmoid(x):
    return jnp.minimum(x, 0.0) - _softplus_neg_abs(x)


def _layer_norm(y, g, b):
    mu = jnp.mean(y, axis=-1, keepdims=True)
    d = y - mu
    var = jnp.mean(d * d, axis=-1, keepdims=True)
    return d * lax.rsqrt(var + LN_EPS) * g + b


def _expand_heads(x, head_width):
    lane_head = _idiv(_iota(x.shape, 1), head_width)
    return jnp.concatenate([jnp.where(lane_head == h, x, 0.0) for h in range(N_HEADS)], axis=0)


def _head_group_matrix(width, head_width, value):
    same = _idiv(_iota((width, width), 0), head_width) == _idiv(_iota((width, width), 1), head_width)
    return jnp.where(same, value, 0.0).astype(F32)


def _resident(shape):
    nd = len(shape)
    return pl.BlockSpec(shape, lambda *_: (0,) * nd, pipeline_mode=pl.Buffered(1))


def _params(semantics):
    return pltpu.CompilerParams(dimension_semantics=semantics, vmem_limit_bytes=VMEM_LIMIT)


def _inproj_kernel(x_ref, w_ref, gla_ref, hgrn_ref, swa_ref, rwkv_ref):
    xb = x_ref[...].astype(BF16)
    o = 0
    for ref, width in ((gla_ref, GLA_W), (hgrn_ref, HGRN_W), (swa_ref, SWA_W), (rwkv_ref, RWKV_W)):
        ref[...] = _dot(xb, w_ref[:, o:o + width])
        o += width


def _in_proj(x, w):
    t = x.shape[0]
    widths = (GLA_W, HGRN_W, SWA_W, RWKV_W)
    return pl.pallas_call(
        _inproj_kernel,
        grid=(t // ROW_TILE,),
        in_specs=[pl.BlockSpec((ROW_TILE, D_MODEL), lambda i: (i, 0)), _resident((D_MODEL, Z_W))],
        out_specs=[pl.BlockSpec((ROW_TILE, w_), lambda i: (i, 0)) for w_ in widths],
        out_shape=[jax.ShapeDtypeStruct((t, w_), F32) for w_ in widths],
        compiler_params=_params(("parallel",)),
        name="in_proj",
    )(x, w)


def _group_in_weights(w_in, vres_down):
    gla, hgrn, swa, rwkv = jnp.split(w_in, (784, 784 + 1024, 784 + 1024 + 512), axis=1)
    if vres_down is not None:
        rwkv = jnp.concatenate([rwkv, vres_down], axis=1)
    pad = lambda a, w_: jnp.pad(a, ((0, 0), (0, w_ - a.shape[1])))
    return jnp.concatenate([pad(gla, GLA_W), hgrn, swa, pad(rwkv, RWKV_W)], axis=1).astype(BF16)


def _gated_linear_attention_tile(index, q, k, v, log_f, state_box, out_box):
    length, kw = q.shape
    head_k = kw // N_HEADS
    n_sub = length // GLA_SUB
    row = _iota((length, length), 0)
    col = _iota((length, length), 1)
    same_sub = _idiv(row, GLA_SUB) == _idiv(col, GLA_SUB)
    m_local = jnp.where(same_sub & (col <= row), 1.0, 0.0).astype(F32)
    m_prev = jnp.where(_idiv(col, GLA_SUB) < _idiv(row, GLA_SUB), 1.0, 0.0).astype(F32)
    sums = _dot_mask(jnp.concatenate([m_local, m_prev], axis=0), log_f)
    yield
    b_local = sums[0:length]
    b_start = sums[length:2 * length]
    b_full = b_start + b_local
    q_local = q * jnp.exp(b_local)

    q_pos = _imod(_iota((N_HEADS * GLA_SUB, length), 0), GLA_SUB)
    s_pos = _iota((N_HEADS * GLA_SUB, length), 1)
    probs = []
    for c in range(n_sub):
        r0, r1 = c * GLA_SUB, (c + 1) * GLA_SUB
        expo = jnp.where(_iota((r1, kw), 0) < r0, b_start[r0:r0 + 1, :] - b_full[0:r1], -b_local[0:r1])
        k_ref = k[0:r1] * jnp.exp(expo)
        if r1 < length:
            k_ref = jnp.concatenate([k_ref, jnp.zeros((length - r1, kw), F32)], axis=0)
        q_heads = _expand_heads(q_local[r0:r1, :], head_k)
        s = _bdot_nt(q_heads, k_ref)
        probs.append(jnp.where(s_pos <= q_pos + r0, s, 0.0))
    b_total = b_full[length - 1:length, :]
    k_end = k * jnp.exp(b_total - b_full)
    upd = _dot_tn(v.astype(BF16), k_end.astype(BF16))
    yield
    o_heads = _bdot(jnp.concatenate(probs, axis=0), v)
    assert len(out_box) == index, "the previous tile must have replaced the state before it is read"
    state_t = state_box[0]
    o_state = _bdot_nt(q * jnp.exp(b_full), state_t)
    same_head = _idiv(_iota((GROUP_WIDTH, kw), 0), HEAD_DIM) == _idiv(_iota((GROUP_WIDTH, kw), 1), head_k)
    state_box[0] = state_t * jnp.exp(b_total) + jnp.where(same_head, upd, 0.0)
    yield
    v_head = _idiv(_iota((GLA_SUB, GROUP_WIDTH), 1), HEAD_DIM)
    rows = []
    for c in range(n_sub):
        base = c * N_HEADS * GLA_SUB
        acc = jnp.zeros((GLA_SUB, GROUP_WIDTH), F32)
        for h in range(N_HEADS):
            acc = acc + jnp.where(v_head == h, o_heads[base + h * GLA_SUB:base + (h + 1) * GLA_SUB, :], 0.0)
        rows.append(acc)
    out_box.append(jnp.concatenate(rows, axis=0) + o_state)


def _gated_linear_attention_plan(q, k, v, log_f, gate, norm_w, o_ref, state_ref):
    state_box = [state_ref[...]]
    tiles = []
    stages = []
    for n in range(q.shape[0] // GLA_TILE):
        rows = slice(n * GLA_TILE, (n + 1) * GLA_TILE)
        stages.append(_gated_linear_attention_tile(n, q[rows], k[rows], v[rows], log_f[rows], state_box, tiles))

    def finish():
        state_ref[...] = state_box[0]
        o = jnp.concatenate(tiles, axis=0)
        ms = _dot_hilo(o * o, _head_group_matrix(GROUP_WIDTH, HEAD_DIM, 1.0 / HEAD_DIM).astype(BF16))
        o_ref[...] = (o * lax.rsqrt(ms + RMS_EPS) * norm_w * _silu(gate)).astype(o_ref.dtype)

    return stages, GLA_STAGGER, finish


def _zero_at_first_step(*state_refs):
    @pl.when(pl.program_id(0) == 0)
    def _():
        for ref in state_refs:
            ref[...] = jnp.zeros_like(ref)


def _gla_plan(z_ref, gk_up_ref, gk_bias_ref, norm_w_ref, o_ref, state_ref):
    z = z_ref[...]
    q = z[:, 0:128] * (GLA_DK ** -0.5)
    k = z[:, 128:256]
    v = z[:, 256:512]
    g = z[:, 512:768]
    gate_in = _dot_3pass(z[:, 768:896], gk_up_ref[...]) + gk_bias_ref[...]
    log_f = _log_sigmoid(gate_in) * (1.0 / GLA_GATE_NORMALIZER)
    return _gated_linear_attention_plan(q, k, v, log_f, g, norm_w_ref[...], o_ref, state_ref)


def _hgrn_plan(z_ref, lb_ref, log_lb_ref, norm_w_ref, o_ref, state_ref):
    z = z_ref[...]
    q = _silu(z[:, 0:256])
    f = z[:, 256:512]
    v = z[:, 512:768]
    g = z[:, 768:1024]
    lb = lb_ref[...]
    a = log_lb_ref[...]
    c = jnp.log1p(-lb) + _log_sigmoid(f)
    log_f = jnp.maximum(a, c) + _softplus_neg_abs(a - c)
    k = (1.0 - lb) * _sigmoid(-f)
    return _gated_linear_attention_plan(q, k, v, log_f, g, norm_w_ref[...], o_ref, state_ref)


class _MixerParts(NamedTuple):
    plan: Callable
    args: list
    in_specs: list
    out_specs: list
    out_shape: list
    scratch: list


def _mixers_kernel(layout, *refs):
    n_in = sum(entry[1] for entry in layout)
    n_out = sum(entry[2] for entry in layout)
    ins, outs, scratch = list(refs[:n_in]), list(refs[n_in:n_in + n_out]), list(refs[n_in + n_out:])
    _zero_at_first_step(*scratch)
    plans = []
    for plan, n_i, n_o, n_s in layout:
        plans.append(plan(*ins[:n_i], *outs[:n_o], *scratch[:n_s]))
        del ins[:n_i], outs[:n_o], scratch[:n_s]
    _run_plans(plans)


def _call_mixers(parts_list, tokens, step, name):
    layout = tuple((p.plan, len(p.args), len(p.out_shape), len(p.scratch)) for p in parts_list)
    flat = lambda field: [item for p in parts_list for item in getattr(p, field)]
    return pl.pallas_call(
        functools.partial(_mixers_kernel, layout),
        grid=(tokens // step,),
        in_specs=flat("in_specs"),
        out_specs=flat("out_specs"),
        out_shape=flat("out_shape"),
        scratch_shapes=flat("scratch"),
        compiler_params=_params(("arbitrary",)),
        name=name,
    )(*flat("args"))


def _mix_out(tokens, step):
    return (pl.BlockSpec((step, GROUP_WIDTH), lambda i: (i, 0)),
            jax.ShapeDtypeStruct((tokens, GROUP_WIDTH), MIX_DTYPE))


def _gla_parts(z, gk_up, gk_bias, norm_w, step):
    out_spec, out_shape = _mix_out(z.shape[0], step)
    gk_up_pad = jnp.zeros((LANES, N_HEADS * GLA_DK), F32).at[:GLA_GATE_RANK].set(gk_up)
    return _MixerParts(
        plan=_gla_plan,
        args=[z, gk_up_pad, gk_bias.reshape(1, -1), jnp.tile(norm_w, N_HEADS).reshape(1, -1)],
        in_specs=[pl.BlockSpec((step, GLA_W), lambda i: (i, 0)),
                  _resident((LANES, N_HEADS * GLA_DK)), _resident((1, N_HEADS * GLA_DK)),
                  _resident((1, GROUP_WIDTH))],
        out_specs=[out_spec], out_shape=[out_shape],
        scratch=[pltpu.VMEM((GROUP_WIDTH, N_HEADS * GLA_DK), F32)])


def _hgrn_parts(z, lb, norm_w, step):
    out_spec, out_shape = _mix_out(z.shape[0], step)
    return _MixerParts(
        plan=_hgrn_plan,
        args=[z, lb.reshape(1, -1), jnp.log(lb).reshape(1, -1), jnp.tile(norm_w, N_HEADS).reshape(1, -1)],
        in_specs=[pl.BlockSpec((step, HGRN_W), lambda i: (i, 0)),
                  _resident((1, GROUP_WIDTH)), _resident((1, GROUP_WIDTH)), _resident((1, GROUP_WIDTH))],
        out_specs=[out_spec], out_shape=[out_shape],
        scratch=[pltpu.VMEM((GROUP_WIDTH, GROUP_WIDTH), F32)])


def _gla_mixer(z, gk_up, gk_bias, norm_w):
    return _call_mixers([_gla_parts(z, gk_up, gk_bias, norm_w, GLA_STEP)], z.shape[0], GLA_STEP, "gla_mixer")[0]


def _hgrn_mixer(z, lb, norm_w):
    return _call_mixers([_hgrn_parts(z, lb, norm_w, GLA_STEP)], z.shape[0], GLA_STEP, "hgrn_mixer")[0]


def _swa_block(q, kw, vw, visible, sinks, out_box):
    n = q.shape[0]
    head_cols = lambda x, h: x[:, h * HEAD_DIM:(h + 1) * HEAD_DIM]
    v_lane_head = _idiv(_iota(vw.shape, 1), HEAD_DIM)
    outs = []
    for kv in range(N_HEADS // 2):
        heads = (2 * kv, 2 * kv + 1)
        scores = _bdot_nt(jnp.concatenate([head_cols(q, h) for h in heads], axis=0), head_cols(kw, kv))
        yield
        probs, sink_terms = [], []
        for half, h in enumerate(heads):
            s = jnp.where(visible, scores[half * n:(half + 1) * n], -jnp.inf)
            sink = sinks[:, h:h + 1]
            m = jnp.maximum(jnp.max(s, axis=-1, keepdims=True), sink)
            probs.append(jnp.exp(s - m))
            sink_terms.append(jnp.exp(sink - m))
        o = _bdot(jnp.concatenate(probs, axis=0), jnp.where(v_lane_head == kv, vw, 1.0))
        yield
        sums = head_cols(o, 1 - kv)[:, 0:1]
        outs += [head_cols(o, kv)[half * n:(half + 1) * n] / (sums[half * n:(half + 1) * n] + sink_terms[half])
                 for half in range(2)]
    out_box.append(jnp.concatenate(outs, axis=-1))


def _swa_plan(q_ref, k_ref, v_ref, kp_ref, vp_ref, sink_ref, o_ref):
    w = SWA_WINDOW
    has_prev = pl.program_id(0) > 0
    q = q_ref[...] * (HEAD_DIM ** -0.5)
    k_all = jnp.concatenate([kp_ref[...], k_ref[...]], axis=0)
    v_all = jnp.concatenate([vp_ref[...], v_ref[...]], axis=0)
    q_pos = _iota((w, 2 * w), 0) + w
    k_pos = _iota((w, 2 * w), 1)
    dist = q_pos - k_pos
    in_window = (dist >= 0) & (dist < w)
    sinks = sink_ref[...]
    blocks = []
    stages = []
    for b in range(q.shape[0] // w):
        visible = in_window if b > 0 else in_window & ((k_pos >= w) | has_prev)
        stages.append(_swa_block(q[b * w:(b + 1) * w], k_all[b * w:(b + 2) * w], v_all[b * w:(b + 2) * w],
                                 visible, sinks, blocks))
    def finish():
        o_ref[...] = jnp.concatenate(blocks, axis=0).astype(o_ref.dtype)

    return stages, SWA_STAGGER, finish


def _swa_parts(z, sinks, step):
    out_spec, out_shape = _mix_out(z.shape[0], step)
    w = SWA_WINDOW
    prev = lambda col: (lambda i: (jnp.maximum(i * (step // w) - 1, 0), col))
    return _MixerParts(
        plan=_swa_plan,
        args=[z, z, z, z, z, sinks.reshape(1, -1)],
        in_specs=[pl.BlockSpec((step, 256), lambda i: (i, 0)),
                  pl.BlockSpec((step, 128), lambda i: (i, 2)), pl.BlockSpec((step, 128), lambda i: (i, 3)),
                  pl.BlockSpec((w, 128), prev(2)), pl.BlockSpec((w, 128), prev(3)),
                  _resident((1, N_HEADS))],
        out_specs=[out_spec], out_shape=[out_shape], scratch=[])


def _swa_mixer(z, sinks):
    return _call_mixers([_swa_parts(z, sinks, SWA_TILE)], z.shape[0], SWA_TILE, "swa_mixer")[0]


def _run_plans(plans):
    live = [dict(enumerate(stages)) for stages, _, _ in plans]
    rnd = 0
    while any(live):
        for group, (_, stagger, _) in zip(live, plans):
            for n in sorted(group):
                if rnd >= n * stagger:
                    try:
                        next(group[n])
                    except StopIteration:
                        del group[n]
        rnd += 1
    for _, _, finish in plans:
        finish()


def _rwkv_chunk(index, r, k, v, a_vec, b_vec, log_w, state_box, out_box):
    c = r.shape[0]
    width = r.shape[1]
    tri = jnp.where(_iota((c, c), 1) <= _iota((c, c), 0), 1.0, 0.0).astype(F32)
    p = _dot_mask(tri, log_w)
    yield
    p_total = p[c - 1:c, :]
    decay_in = jnp.exp(p)
    decay_out = jnp.exp(-p)
    decay_end = jnp.exp(p_total - p)
    a_in = a_vec * jnp.exp(p - log_w)
    r_in = r * decay_in
    b_out = b_vec * decay_out
    k_out = k * decay_out
    b_end = b_vec * decay_end
    k_end = k * decay_end

    t_pos = _iota((c, width), 0)
    assert width == N_HEADS * c
    s_pos = _imod(_iota((c, width), 1), c)
    strict = s_pos < t_pos
    incl = s_pos <= t_pos
    expand = lambda x: _expand_heads(x, HEAD_DIM)

    scores = _bdot_nt(jnp.concatenate([a_in, r_in], axis=0),
                      jnp.concatenate([expand(b_out), expand(k_out)], axis=0))
    yield
    a_ab = jnp.where(strict, scores[0:c, 0:width], 0.0)
    a_ak = jnp.where(strict, scores[0:c, width:2 * width], 0.0)
    a_rb = jnp.where(incl, scores[c:2 * c, 0:width], 0.0)
    a_rk = jnp.where(incl, scores[c:2 * c, width:2 * width], 0.0)

    t_inv = jnp.where(s_pos == t_pos, 1.0, 0.0) + a_ab
    from_v = _bdot(jnp.concatenate([a_ak, a_rk], axis=0), expand(v))
    x1 = from_v[0:c]
    y_from_v = from_v[c:2 * c]
    power = _bdot(a_ab, expand(a_ab))
    yield
    n_factors = (c - 1).bit_length()
    for _ in range(n_factors - 2):
        both = _bdot(jnp.concatenate([t_inv, power], axis=0), expand(power))
        t_inv = t_inv + both[0:c]
        power = both[c:2 * c]
        yield
    t_inv = t_inv + _bdot(t_inv, expand(power))
    yield
    sol = _bdot(t_inv, jnp.concatenate([expand(x1), expand(a_in)], axis=1))
    yield
    u0 = sol[:, 0:width]
    w_mat = sol[:, width:2 * width]

    assert len(out_box) == index, "the previous chunk must have replaced the state before it is read"
    state = state_box[0]
    from_state = _bdot_nt(jnp.concatenate([w_mat, r_in], axis=0), state)
    yield
    u = u0 + from_state[0:c]
    y = _bdot(a_rb, expand(u)) + y_from_v + from_state[c:2 * c]
    upd = _dot_tn(jnp.concatenate([u, v], axis=0).astype(BF16),
                  jnp.concatenate([b_end, k_end], axis=0).astype(BF16))
    same_head = _idiv(_iota((width, width), 0), HEAD_DIM) == _idiv(_iota((width, width), 1), HEAD_DIM)
    state_box[0] = state * jnp.exp(p_total) + jnp.where(same_head, upd, 0.0)
    out_box.append(y)


def _rwkv_plan(has_vres, *refs):
    if has_vres:
        (z_ref, zp_ref, vfirst_ref, mu_ref, w0_ref, wup_ref, a0_ref, aup_ref, gup_ref, kk_ref, ka_ref,
         rk_ref, lnw_ref, lnb_ref, v0_ref, vup_ref, o_ref, state_ref) = refs
    else:
        (z_ref, zp_ref, mu_ref, w0_ref, wup_ref, a0_ref, aup_ref, gup_ref, kk_ref, ka_ref,
         rk_ref, lnw_ref, lnb_ref, o_ref, vout_ref, state_ref) = refs
    step = pl.program_id(0)
    z = z_ref[...]
    last_prev = jnp.where(step > 0, zp_ref[7:8, :], 0.0)
    prev = jnp.where(_iota(z.shape, 0) == 0, last_prev, pltpu.roll(z, 1, axis=0))
    zr = z + (prev - z) * mu_ref[...]
    r = zr[:, 0:256]
    k = zr[:, 256:512]
    v = zr[:, 512:768]
    low = zr[:, 768:896]
    w_pre = w0_ref[...] + _dot_3pass(jnp.tanh(low), wup_ref[...])
    w_log = -(jnp.maximum(-w_pre, 0.0) + _softplus_neg_abs(w_pre)) - 0.5
    log_w = -jnp.exp(w_log)
    a = _sigmoid(a0_ref[...] + _dot_3pass(low, aup_ref[...]))
    g = _dot_3pass(_sigmoid(low), gup_ref[...])
    if has_vres:
        v = v + (vfirst_ref[...] - v) * _sigmoid(v0_ref[...] + _dot_3pass(low, vup_ref[...]))
    else:
        vout_ref[...] = v
    head_sum = _head_group_matrix(GROUP_WIDTH, HEAD_DIM, 1.0).astype(BF16)
    kk = k * kk_ref[...]
    kk = kk / jnp.maximum(jnp.sqrt(_dot_hilo(kk * kk, head_sum)), 1e-12)
    k = k * (1.0 + (a - 1.0) * ka_ref[...])
    a_vec = -kk
    b_vec = kk * a

    c = RWKV_CHUNK
    state_box = [state_ref[...]]
    chunks = []
    stages = []
    for n in range(z.shape[0] // c):
        rows = slice(n * c, (n + 1) * c)
        stages.append(_rwkv_chunk(n, r[rows], k[rows], v[rows], a_vec[rows], b_vec[rows], log_w[rows],
                                  state_box, chunks))
    def finish():
        state_ref[...] = state_box[0]
        y = jnp.concatenate(chunks, axis=0)
        head_mean = _head_group_matrix(GROUP_WIDTH, HEAD_DIM, 1.0 / HEAD_DIM).astype(BF16)
        mu_y = _dot_hilo(y, head_mean)
        d = y - mu_y
        var_y = _dot_hilo(d * d, head_mean)
        y = d * lax.rsqrt(var_y + RWKV_GN_EPS) * lnw_ref[...] + lnb_ref[...]
        bonus = _dot_hilo(r * k * rk_ref[...], head_sum) * v
        o_ref[...] = ((y + bonus) * g).astype(o_ref.dtype)

    return stages, RWKV_STAGGER, finish


def _rwkv_mixer(z, mu, w0, w_up, a0, a_up, g_up, k_k, k_a, r_k, lnx_w, lnx_b, vres):
    parts = _rwkv_parts(z, mu, w0, w_up, a0, a_up, g_up, k_k, k_a, r_k, lnx_w, lnx_b, vres, RWKV_TILE)
    outs = _call_mixers([parts], z.shape[0], RWKV_TILE, "rwkv_mixer")
    return outs[0] if vres is not None else outs


def _rwkv_parts(z, mu, w0, w_up, a0, a_up, g_up, k_k, k_a, r_k, lnx_w, lnx_b, vres, step):
    t = z.shape[0]
    c = step
    row = lambda a: a.reshape(1, -1)
    low_rows = lambda a, start: jnp.zeros((LANES, GROUP_WIDTH), F32).at[start:start + a.shape[0]].set(a)
    has_vres = vres is not None
    mu_full = jnp.zeros((RWKV_W,), F32).at[:RWKV_COLS].set(mu)
    vec = _resident((1, GROUP_WIDTH))
    mat = _resident((LANES, GROUP_WIDTH))
    tile = pl.BlockSpec((c, GROUP_WIDTH), lambda i: (i, 0))
    z_specs = [pl.BlockSpec((c, RWKV_W), lambda i: (i, 0)),
               pl.BlockSpec((8, RWKV_W), lambda i: (jnp.maximum(i * (c // 8) - 1, 0), 0))]
    common = [row(w0), low_rows(w_up, 0), row(a0), low_rows(a_up, 16), low_rows(g_up, 32),
              row(k_k), row(k_a), row(r_k), row(lnx_w), row(lnx_b)]
    common_specs = [vec, mat, vec, mat, mat, vec, vec, vec, vec, vec]
    if has_vres:
        v_first, vres_mu, v0, v_up = vres
        mu_full = mu_full.at[RWKV_COLS:RWKV_COLS + RWKV_V_RANK].set(vres_mu)
        args = [z, z, v_first, row(mu_full)] + common + [row(v0), low_rows(v_up, 64)]
        in_specs = z_specs + [tile, _resident((1, RWKV_W))] + common_specs + [vec, mat]
        out_specs = [tile]
        out_shape = [jax.ShapeDtypeStruct((t, GROUP_WIDTH), MIX_DTYPE)]
    else:
        args = [z, z, row(mu_full)] + common
        in_specs = z_specs + [_resident((1, RWKV_W))] + common_specs
        out_specs = [tile, tile]
        out_shape = [jax.ShapeDtypeStruct((t, GROUP_WIDTH), MIX_DTYPE), jax.ShapeDtypeStruct((t, GROUP_WIDTH), F32)]
    return _MixerParts(plan=functools.partial(_rwkv_plan, has_vres), args=args, in_specs=in_specs,
                       out_specs=out_specs, out_shape=out_shape,
                       scratch=[pltpu.VMEM((GROUP_WIDTH, GROUP_WIDTH), F32)])


def _store_token_tiles(ref, index, x):
    n = x.shape[0]
    for j in range(SUBLANES):
        ref[(*index, pl.ds(j, n, stride=SUBLANES), slice(None))] = x[:, j * LANES:(j + 1) * LANES]


def _load_token_tiles(ref, index, n):
    return jnp.concatenate([ref[(*index, pl.ds(j, n, stride=SUBLANES), slice(None))] for j in range(SUBLANES)],
                           axis=-1)


def _top2_route(logits):
    row = _iota(logits.shape, 0).astype(F32)
    m1 = jnp.max(logits, axis=0, keepdims=True)
    i1 = jnp.min(jnp.where(logits == m1, row, N_EXPERTS), axis=0, keepdims=True)
    rest = jnp.where(row == i1, -jnp.inf, logits)
    m2 = jnp.max(rest, axis=0, keepdims=True)
    i2 = jnp.min(jnp.where(rest == m2, row, N_EXPERTS), axis=0, keepdims=True)
    e2 = jnp.exp(m2 - m1)
    g1 = 1.0 / (1.0 + e2)
    g2 = e2 * g1
    return jnp.where(row == 0, i1, jnp.where(row == 1, i2, jnp.where(row == 2, g1, jnp.where(row == 3, g2, 0.0))))


def _mix_residual_norm(alpha, x_ref, mix_refs, w_ref, g_ref, b_ref):
    acc = alpha * x_ref[...]
    for h, ref in enumerate(mix_refs):
        acc = acc + _dot(ref[...], w_ref[h * GROUP_WIDTH:(h + 1) * GROUP_WIDTH, :])
    return _layer_norm(acc, g_ref[...], b_ref[...])


def _outproj_route_kernel(alpha, x_ref, o0_ref, o1_ref, o2_ref, o3_ref, w_ref, g_ref, b_ref, router_ref,
                          tiles_ref, route_ref):
    y = _mix_residual_norm(alpha, x_ref, (o0_ref, o1_ref, o2_ref, o3_ref), w_ref, g_ref, b_ref)
    route_ref[...] = _top2_route(_dot_3pass_nt(router_ref[...], y))
    _store_token_tiles(tiles_ref, (), y)


def _out_proj_ln_route(alpha, x, mixes, w_out, g, b, router):
    t = x.shape[0]
    row_d = pl.BlockSpec((ROW_TILE, D_MODEL), lambda i: (i, 0))
    row_g = pl.BlockSpec((ROW_TILE, GROUP_WIDTH), lambda i: (i, 0))
    return pl.pallas_call(
        functools.partial(_outproj_route_kernel, alpha),
        grid=(t // ROW_TILE,),
        in_specs=[row_d, row_g, row_g, row_g, row_g, _resident((D_MODEL, D_MODEL)),
                  _resident((1, D_MODEL)), _resident((1, D_MODEL)), _resident((N_EXPERTS, D_MODEL))],
        out_specs=[pl.BlockSpec((ROW_TILE * SUBLANES, LANES), lambda i: (i, 0)),
                   pl.BlockSpec((N_EXPERTS, ROW_TILE), lambda i: (0, i))],
        out_shape=[jax.ShapeDtypeStruct((t * SUBLANES, LANES), F32), jax.ShapeDtypeStruct((N_EXPERTS, t), F32)],
        compiler_params=_params(("parallel",)),
        name="out_proj_ln_route",
    )(x, *mixes, w_out.astype(BF16), g.reshape(1, -1), b.reshape(1, -1), router.T)


def _ln_embed(y, ln_g, ln_b, p, ple_gate, ple_proj):
    x = _layer_norm(y, ln_g, ln_b)
    gate = _sigmoid(_dot(x.astype(BF16), ple_gate))
    return x + gate * _dot(p.astype(BF16), ple_proj)


def _dense_ffn_kernel(alpha, x_ref, o0_ref, o1_ref, o2_ref, o3_ref, wo_ref, g1_ref, b1_ref, p_ref, wg_ref, wu_ref,
                      wd_ref, g_ref, b_ref, pg_ref, pp_ref, y_ref, acc_ref):
    x = _mix_residual_norm(alpha, x_ref, (o0_ref, o1_ref, o2_ref, o3_ref), wo_ref, g1_ref, b1_ref)
    xb = x.astype(BF16)
    acc_ref[...] = alpha * x
    for j in range(D_FF // FF_CHUNK):
        cols = slice(j * FF_CHUNK, (j + 1) * FF_CHUNK)
        h = _silu(_dot(xb, wg_ref[:, cols])) * _dot(xb, wu_ref[:, cols])
        acc_ref[...] += _dot(h.astype(BF16), wd_ref[cols, :])
    y_ref[...] = _ln_embed(acc_ref[...], g_ref[...], b_ref[...], p_ref[...], pg_ref[...], pp_ref[...])


def _dense_layer_tail(alpha, x, mixes, w_out, g1, b1, p_all, p_block, w_gate, w_up, w_down, g, b, ple_gate, ple_proj):
    t = x.shape[0]
    row_d = pl.BlockSpec((ROW_TILE, D_MODEL), lambda i: (i, 0))
    row_g = pl.BlockSpec((ROW_TILE, GROUP_WIDTH), lambda i: (i, 0))
    return pl.pallas_call(
        functools.partial(_dense_ffn_kernel, alpha),
        grid=(t // ROW_TILE,),
        in_specs=[row_d, row_g, row_g, row_g, row_g, _resident((D_MODEL, D_MODEL)),
                  _resident((1, D_MODEL)), _resident((1, D_MODEL)),
                  pl.BlockSpec((ROW_TILE, PLE_DIM), lambda i: (p_block + i, 0)),
                  _resident((D_MODEL, D_FF)), _resident((D_MODEL, D_FF)), _resident((D_FF, D_MODEL)),
                  _resident((1, D_MODEL)), _resident((1, D_MODEL)),
                  _resident((D_MODEL, D_MODEL)), _resident((PLE_DIM, D_MODEL))],
        out_specs=row_d,
        out_shape=jax.ShapeDtypeStruct((t, D_MODEL), F32),
        scratch_shapes=[pltpu.VMEM((ROW_TILE, D_MODEL), F32)],
        compiler_params=_params(("parallel",)),
        name="dense_ffn_tail",
    )(x, *mixes, w_out.astype(BF16), g1.reshape(1, -1), b1.reshape(1, -1), p_all,
      w_gate.astype(BF16), w_up.astype(BF16), w_down.astype(BF16), g.reshape(1, -1), b.reshape(1, -1),
      ple_gate.astype(BF16), ple_proj.astype(BF16))


def _weight_group_copies(e, group, wg_hbm, wu_hbm, wd_hbm, stage_cols_ref, stage_rows_ref, sem):
    slot = group % 2
    lo, hi = group * MOE_WEIGHT_CHUNK, (group + 1) * MOE_WEIGHT_CHUNK
    return (pltpu.make_async_copy(wg_hbm.at[e, :, lo:hi], stage_cols_ref.at[slot, 0], sem.at[slot]),
            pltpu.make_async_copy(wu_hbm.at[e, :, lo:hi], stage_cols_ref.at[slot, 1], sem.at[slot]),
            pltpu.make_async_copy(wd_hbm.at[e, lo:hi, :], stage_rows_ref.at[slot], sem.at[slot]))


def _expert_kernel(row_tok_ref, row_dst_ref, block_e_ref, n_used_ref, x_hbm, wg_hbm, wu_hbm, wd_hbm, y_hbm,
                   rows_ref, xb_ref, acc_ref, ybuf_ref, wg_ref, wu_ref, wd_ref, stage_cols_ref, stage_rows_ref,
                   gather_sem, scatter_sem, weight_sem):
    i = pl.program_id(0)
    n_used = n_used_ref[0]
    last = pl.num_programs(0) - 1
    n = MOE_ROWS
    slot = lax.rem(i, 2)
    other = 1 - slot
    tile = lambda first_row: pl.ds(pl.multiple_of(first_row, SUBLANES), SUBLANES)
    gather_row = lambda tok_row, s, r: pltpu.make_async_copy(
        x_hbm.at[tile(tok_row)], rows_ref.at[s, tile(r * SUBLANES)], gather_sem.at[s])
    scatter_row = lambda dst_row, s, r: pltpu.make_async_copy(
        ybuf_ref.at[s, tile(r * SUBLANES)], y_hbm.at[tile(dst_row)], scatter_sem.at[s])
    block_rows = n * SUBLANES
    gather_block = lambda s: pltpu.make_async_copy(x_hbm.at[pl.ds(0, block_rows)], rows_ref.at[s], gather_sem.at[s])
    scatter_block = lambda s: pltpu.make_async_copy(ybuf_ref.at[s], y_hbm.at[pl.ds(0, block_rows)],
                                                    scatter_sem.at[s])

    @pl.when(i == 0)
    def _():
        ybuf_ref[1] = jnp.zeros((block_rows, LANES), F32)

        def start(r, carry):
            gather_row(row_tok_ref[r], 0, r).start()
            return carry

        lax.fori_loop(0, n, start, 0)

    def scatter_all(first_dst, s):
        def start(r, carry):
            scatter_row(row_dst_ref[first_dst + r], s, r).start()
            return carry

        lax.fori_loop(0, n, start, 0)
        scatter_block(s).wait()

    expert = block_e_ref[i]
    new_expert = (i == 0) | (expert != block_e_ref[jnp.maximum(i - 1, 0)])

    weight_group = lambda g: _weight_group_copies(expert, g, wg_hbm, wu_hbm, wd_hbm, stage_cols_ref,
                                                  stage_rows_ref, weight_sem)
    chunks_per_group = MOE_WEIGHT_CHUNK // FF_CHUNK
    n_groups = D_FF_EXPERT // MOE_WEIGHT_CHUNK

    def take_weight_group(g):
        if g + 1 < n_groups:
            for copy in weight_group(g + 1):
                copy.start()
        for copy in weight_group(g):
            copy.wait()
        piece = slice(g * MOE_WEIGHT_CHUNK, (g + 1) * MOE_WEIGHT_CHUNK)
        wg_ref[:, piece] = stage_cols_ref[g % 2, 0].astype(BF16)
        wu_ref[:, piece] = stage_cols_ref[g % 2, 1].astype(BF16)
        wd_ref[piece, :] = stage_rows_ref[g % 2].astype(BF16)

    @pl.when(i < n_used)
    def _():
        @pl.when(new_expert)
        def _():
            for copy in weight_group(0):
                copy.start()

        gather_block(slot).wait()
        xb_ref[...] = _load_token_tiles(rows_ref, (slot,), n).astype(BF16)
        n_chunks = D_FF_EXPERT // FF_CHUNK
        rows_per_chunk = -(-n // (n_chunks - MOE_DMA_FREE_CHUNKS))
        for j in range(n_chunks):
            if j % chunks_per_group == 0:
                pl.when(new_expert)(functools.partial(take_weight_group, j // chunks_per_group))
            cols = slice(j * FF_CHUNK, (j + 1) * FF_CHUNK)
            xb = xb_ref[...]
            h = _silu(_dot(xb, wg_ref[:, cols])) * _dot(xb, wu_ref[:, cols])
            part = _dot(h.astype(BF16), wd_ref[cols, :])
            if j == 0:
                acc_ref[...] = part
            else:
                acc_ref[...] += part
            for r in range(j * rows_per_chunk, min((j + 1) * rows_per_chunk, n)):
                gather_row(row_tok_ref[(i + 1) * n + r], other, r).start(priority=GATHER_DMA_PRIORITY)
                scatter_row(row_dst_ref[i * n + r], other, r).start(priority=SCATTER_DMA_PRIORITY)
        _store_token_tiles(ybuf_ref, (slot,), acc_ref[...])
        scatter_block(other).wait()

    @pl.when(i == n_used)
    def _():
        gather_block(slot).wait()
        scatter_all(i * n, other)

    @pl.when(i >= n_used)
    def _():
        rows_ref[slot] = jnp.zeros((block_rows, LANES), F32)
        fill = pltpu.make_async_copy(rows_ref.at[slot], y_hbm.at[pl.ds(i * block_rows, block_rows)],
                                     scatter_sem.at[slot])
        fill.start()
        fill.wait()

    @pl.when((i == last) & (i < n_used))
    def _():
        gather_block(other).wait()
        scatter_all((i + 1) * n, slot)


def _expert_rows(x, row_tok, row_dst, block_e, n_used, n_out_rows, w_gate, w_up, w_down):
    n_blocks = block_e.shape[0]
    hbm = pl.BlockSpec(memory_space=pl.ANY)
    grid_spec = pltpu.PrefetchScalarGridSpec(
        num_scalar_prefetch=4,
        grid=(n_blocks,),
        in_specs=[hbm, hbm, hbm, hbm],
        out_specs=hbm,
        scratch_shapes=[pltpu.VMEM((2, MOE_ROWS * SUBLANES, LANES), F32), pltpu.VMEM((MOE_ROWS, D_MODEL), BF16),
                        pltpu.VMEM((MOE_ROWS, D_MODEL), F32), pltpu.VMEM((2, MOE_ROWS * SUBLANES, LANES), F32),
                        pltpu.VMEM((D_MODEL, D_FF_EXPERT), BF16), pltpu.VMEM((D_MODEL, D_FF_EXPERT), BF16),
                        pltpu.VMEM((D_FF_EXPERT, D_MODEL), BF16),
                        pltpu.VMEM((2, 2, D_MODEL, MOE_WEIGHT_CHUNK), F32),
                        pltpu.VMEM((2, MOE_WEIGHT_CHUNK, D_MODEL), F32),
                        pltpu.SemaphoreType.DMA((2,)), pltpu.SemaphoreType.DMA((2,)),
                        pltpu.SemaphoreType.DMA((2,))],
    )
    return pl.pallas_call(
        _expert_kernel,
        grid_spec=grid_spec,
        out_shape=jax.ShapeDtypeStruct((n_out_rows * SUBLANES, LANES), F32),
        compiler_params=_params(("arbitrary",)),
        name="moe_experts",
    )(row_tok, row_dst, block_e, n_used.reshape(1), x, w_gate, w_up, w_down)


def _combine_kernel(alpha, x_ref, y0_ref, y1_ref, gates_ref, p_ref, g_ref, b_ref, pg_ref, pp_ref, o_ref):
    n = ROW_TILE
    gates = gates_ref[...]
    f = _load_token_tiles(y0_ref, (), n) * gates[:, 0:1] + _load_token_tiles(y1_ref, (), n) * gates[:, 1:2]
    o_ref[...] = _ln_embed(alpha * _load_token_tiles(x_ref, (), n) + f, g_ref[...], b_ref[...], p_ref[...],
                           pg_ref[...], pp_ref[...])


def _moe_combine_tail(alpha, x_tiles, p_all, p_block, y_tiles, gates, g, b, ple_gate, ple_proj):
    t = x_tiles.shape[0] // SUBLANES
    n = ROW_TILE
    row_d = pl.BlockSpec((n, D_MODEL), lambda i: (i, 0))
    tiles_d = pl.BlockSpec((n * SUBLANES, LANES), lambda i: (i, 0))
    return pl.pallas_call(
        functools.partial(_combine_kernel, alpha),
        grid=(t // n,),
        in_specs=[tiles_d, tiles_d, pl.BlockSpec((n * SUBLANES, LANES), lambda i: (t // n + i, 0)),
                  pl.BlockSpec((n, 2), lambda i: (i, 0)), pl.BlockSpec((n, PLE_DIM), lambda i: (p_block + i, 0)),
                  _resident((1, D_MODEL)), _resident((1, D_MODEL)),
                  _resident((D_MODEL, D_MODEL)), _resident((PLE_DIM, D_MODEL))],
        out_specs=row_d,
        out_shape=jax.ShapeDtypeStruct((t, D_MODEL), F32),
        compiler_params=_params(("parallel",)),
        name="moe_combine_tail",
    )(x_tiles, y_tiles, y_tiles, gates, p_all, g.reshape(1, -1), b.reshape(1, -1), ple_gate.astype(BF16),
      ple_proj.astype(BF16))


def _moe_tail(alpha, x_tiles, route, p_all, p_block, w_gate, w_up, w_down, g, b, ple_gate, ple_proj):
    t = route.shape[1]
    experts = route[0:2].T.astype(jnp.int32)
    gates = route[2:4].T
    e_flat = experts.reshape(-1)
    onehot = (e_flat[:, None] == jnp.arange(N_EXPERTS, dtype=jnp.int32)[None, :]).astype(jnp.int32)
    running = jnp.cumsum(onehot, axis=0)
    rank = jnp.sum(onehot * (running - 1), axis=1)
    counts = running[-1]
    padded = (counts + MOE_ROWS - 1) // MOE_ROWS * MOE_ROWS
    pad_end = jnp.cumsum(padded)
    pad_start = pad_end - padded
    dest = (pad_start[e_flat] + rank).astype(jnp.int32)
    n_blocks = (2 * t) // MOE_ROWS + N_EXPERTS
    n_rows = n_blocks * MOE_ROWS
    assign = jnp.arange(2 * t, dtype=jnp.int32)
    row_assign = jnp.full((n_rows,), -1, jnp.int32).at[dest].set(assign)
    used = row_assign >= 0
    spare = 2 * t + jnp.cumsum(jnp.where(used, 0, 1).astype(jnp.int32)) - 1
    row_tok = jnp.where(used, row_assign // 2, 0)
    row_dst = jnp.where(used, (row_assign % 2) * t + row_assign // 2, spare)
    n_spare = n_rows - 2 * t
    first_dst = 2 * t + n_spare + jnp.arange(MOE_ROWS, dtype=jnp.int32)
    row_tok = jnp.concatenate([row_tok, jnp.zeros((MOE_ROWS,), jnp.int32)])
    row_dst = jnp.concatenate([first_dst, row_dst])
    block_start = jnp.arange(n_blocks, dtype=jnp.int32) * MOE_ROWS
    block_e = jnp.minimum(jnp.sum((block_start[:, None] >= pad_end[None, :]).astype(jnp.int32), axis=1),
                          N_EXPERTS - 1)
    n_used = (pad_end[-1] // MOE_ROWS).astype(jnp.int32)
    y_tiles = _expert_rows(x_tiles, row_tok * SUBLANES, row_dst * SUBLANES, block_e, n_used,
                           2 * t + n_spare + MOE_ROWS, w_gate, w_up, w_down)
    return _moe_combine_tail(alpha, x_tiles, p_all, p_block, y_tiles, gates, g, b, ple_gate, ple_proj)


def kernel(x, p, w_in, w_out, gla_gk_up, gla_gk_bias, gla_norm_w, hgrn_lower_bounds, hgrn_norm_w, swa_sinks, rwkv_mu, rwkv_w0, rwkv_w_up, rwkv_a0, rwkv_a_up, rwkv_g_up, rwkv_k_k, rwkv_k_a, rwkv_r_k, rwkv_lnx_w, rwkv_lnx_b, rwkv_vres_down, rwkv_vres_mu, rwkv_v0, rwkv_vres_up, ln1_g, ln1_b, ln2_g, ln2_b, ffn_w_gate, ffn_w_up, ffn_w_down, moe_router, moe_w_gate, moe_w_up, moe_w_down, ple_proj, ple_gate):
    bsz, seq, d = x.shape
    depth = w_in.shape[0]
    alpha = (2.0 * depth) ** 0.25
    lbs = jnp.cumsum(jax.nn.softmax(hgrn_lower_bounds.astype(F32), axis=0), axis=0)
    lbs = lbs - lbs[0]
    p_all = p.reshape(-1, PLE_DIM)
    outs = []
    for bi in range(bsz):
        xt = x[bi]
        v_first = None
        for i in range(depth):
            w = _group_in_weights(w_in[i], None if i == 0 else rwkv_vres_down[i - 1])
            z_gla, z_hgrn, z_swa, z_rwkv = _in_proj(xt, w)
            vres = None if i == 0 else (v_first, rwkv_vres_mu[i - 1], rwkv_v0[i - 1], rwkv_vres_up[i - 1])
            mixed = _call_mixers(
                [_gla_parts(z_gla, gla_gk_up[i], gla_gk_bias[i], gla_norm_w[i], MIX_STEP),
                 _hgrn_parts(z_hgrn, lbs[i], hgrn_norm_w[i], MIX_STEP),
                 _swa_parts(z_swa, swa_sinks[i], MIX_STEP),
                 _rwkv_parts(z_rwkv, rwkv_mu[i], rwkv_w0[i], rwkv_w_up[i], rwkv_a0[i], rwkv_a_up[i],
                             rwkv_g_up[i], rwkv_k_k[i], rwkv_k_a[i], rwkv_r_k[i].reshape(-1),
                             rwkv_lnx_w[i], rwkv_lnx_b[i], vres, MIX_STEP)],
                seq, MIX_STEP, "mixers")
            mixes = tuple(mixed[0:4])
            if i == 0:
                v_first = mixed[4]
            j = i // 2
            p_block = (i * bsz + bi) * (seq // ROW_TILE)
            if i % 2 == 0:
                xt = _dense_layer_tail(alpha, xt, mixes, w_out[i], ln1_g[i], ln1_b[i], p_all, p_block,
                                       ffn_w_gate[j], ffn_w_up[j], ffn_w_down[j], ln2_g[i], ln2_b[i],
                                       ple_gate[i], ple_proj[i])
            else:
                x_tiles, route = _out_proj_ln_route(alpha, xt, mixes, w_out[i], ln1_g[i], ln1_b[i], moe_router[j])
                xt = _moe_tail(alpha, x_tiles, route, p_all, p_block, moe_w_gate[j], moe_w_up[j], moe_w_down[j],
                               ln2_g[i], ln2_b[i], ple_gate[i], ple_proj[i])
        outs.append(xt)
    return jnp.stack(outs, axis=0)
```

```python
import functools
from typing import Callable, NamedTuple

import jax
import jax.numpy as jnp
from jax import lax
from jax.experimental import pallas as pl
from jax.experimental.pallas import tpu as pltpu

F32 = jnp.float32
BF16 = jnp.bfloat16
MIX_DTYPE = BF16

D_MODEL = 1024
GROUP_WIDTH = 256
N_HEADS = 4
HEAD_DIM = 64
GLA_DK = 32
GLA_GATE_RANK = 16
GLA_GATE_NORMALIZER = 16.0
SWA_WINDOW = 128
RWKV_COLS = 3 * GROUP_WIDTH + 16 + 16 + 32
RWKV_V_RANK = 8
LN_EPS = 1e-5
RMS_EPS = 1e-6
RWKV_GN_EPS = 64e-5
D_FF = 2816
N_EXPERTS = 8
D_FF_EXPERT = 3584
PLE_DIM = 256

LANES = 128
SUBLANES = 8
GLA_W = 896
HGRN_W = 1024
SWA_W = 512
RWKV_W = 896
Z_W = GLA_W + HGRN_W + SWA_W + RWKV_W

GLA_SUB = 16
GLA_TILE = 128
GLA_STAGGER = 5
SWA_STAGGER = 5
MIX_STEP = 1024
RWKV_CHUNK = 64
RWKV_STAGGER = 2
ROW_TILE = 512
FF_CHUNK = 256
MOE_ROWS = 512
MOE_WEIGHT_CHUNK = 512
MOE_DMA_FREE_CHUNKS = 10
GATHER_DMA_PRIORITY = 0
SCATTER_DMA_PRIORITY = 1
VMEM_LIMIT = 56 * 1024 * 1024


def _iota(shape, dim):
    return lax.broadcasted_iota(jnp.int32, shape, dim)


def _idiv(x, n):
    return jnp.right_shift(x, n.bit_length() - 1)


def _imod(x, n):
    return jnp.bitwise_and(x, n - 1)


def _dot(a, b):
    return jnp.dot(a, b, preferred_element_type=F32)


def _dot_nt(a, b):
    return lax.dot_general(a, b, (((1,), (1,)), ((), ())), preferred_element_type=F32)


def _dot_tn(a, b):
    return lax.dot_general(a, b, (((0,), (0,)), ((), ())), preferred_element_type=F32)


def _bdot(a, b):
    return _dot(a.astype(BF16), b.astype(BF16))


def _bdot_nt(a, b):
    return _dot_nt(a.astype(BF16), b.astype(BF16))


def _dot_hilo(x, m):
    hi = x.astype(BF16)
    lo = (x - hi.astype(F32)).astype(BF16)
    return _dot(hi, m) + _dot(lo, m)


def _dot_mask(m, x):
    m = m.astype(BF16)
    x1 = x.astype(BF16)
    r1 = x - x1.astype(F32)
    x2 = r1.astype(BF16)
    x3 = (r1 - x2.astype(F32)).astype(BF16)
    return _dot(m, x1) + _dot(m, x2) + _dot(m, x3)


def _dot_3pass_nt(x, w):
    x_hi = x.astype(BF16)
    x_lo = (x - x_hi.astype(F32)).astype(BF16)
    w_hi = w.astype(BF16)
    w_lo = (w - w_hi.astype(F32)).astype(BF16)
    return _dot_nt(x_hi, w_hi) + _dot_nt(x_lo, w_hi) + _dot_nt(x_hi, w_lo)


def _dot_3pass(x, w):
    x_hi = x.astype(BF16)
    x_lo = (x - x_hi.astype(F32)).astype(BF16)
    w_hi = w.astype(BF16)
    w_lo = (w - w_hi.astype(F32)).astype(BF16)
    return _dot(x_hi, w_hi) + _dot(x_lo, w_hi) + _dot(x_hi, w_lo)


def _sigmoid(x):
    return 1.0 / (1.0 + jnp.exp(-x))


def _silu(x):
    return x * _sigmoid(x)


def _softplus_neg_abs(x):
    return jnp.log(1.0 + jnp.exp(-jnp.abs(x)))


def _log_sigmoid(x):
    return jnp.minimum(x, 0.0) - _softplus_neg_abs(x)


def _layer_norm(y, g, b):
    mu = jnp.mean(y, axis=-1, keepdims=True)
    d = y - mu
    var = jnp.mean(d * d, axis=-1, keepdims=True)
    return d * lax.rsqrt(var + LN_EPS) * g + b


def _expand_heads(x, head_width):
    lane_head = _idiv(_iota(x.shape, 1), head_width)
    return jnp.concatenate([jnp.where(lane_head == h, x, 0.0) for h in range(N_HEADS)], axis=0)


def _head_group_matrix(width, head_width, value):
    same = _idiv(_iota((width, width), 0), head_width) == _idiv(_iota((width, width), 1), head_width)
    return jnp.where(same, value, 0.0).astype(F32)


def _resident(shape):
    nd = len(shape)
    return pl.BlockSpec(shape, lambda *_: (0,) * nd, pipeline_mode=pl.Buffered(1))


def _params(semantics):
    return pltpu.CompilerParams(dimension_semantics=semantics, vmem_limit_bytes=VMEM_LIMIT)


def _run_plans(plans):
    live = [dict(enumerate(stages)) for stages, _, _ in plans]
    rnd = 0
    while any(live):
        for group, (_, stagger, _) in zip(live, plans):
            for n in sorted(group):
                if rnd >= n * stagger:
                    try:
                        next(group[n])
                    except StopIteration:
                        del group[n]
        rnd += 1
    for _, _, finish in plans:
        finish()


def _inproj_kernel(x_ref, w_ref, gla_ref, hgrn_ref, swa_ref, rwkv_ref):
    xb = x_ref[...].astype(BF16)
    o = 0
    for ref, width in ((gla_ref, GLA_W), (hgrn_ref, HGRN_W), (swa_ref, SWA_W), (rwkv_ref, RWKV_W)):
        ref[...] = _dot(xb, w_ref[:, o:o + width])
        o += width


def _in_proj(x, w):
    t = x.shape[0]
    widths = (GLA_W, HGRN_W, SWA_W, RWKV_W)
    return pl.pallas_call(
        _inproj_kernel,
        grid=(t // ROW_TILE,),
        in_specs=[pl.BlockSpec((ROW_TILE, D_MODEL), lambda i: (i, 0)), _resident((D_MODEL, Z_W))],
        out_specs=[pl.BlockSpec((ROW_TILE, w_), lambda i: (i, 0)) for w_ in widths],
        out_shape=[jax.ShapeDtypeStruct((t, w_), F32) for w_ in widths],
        compiler_params=_params(("parallel",)),
        name="in_proj",
    )(x, w)


def _group_in_weights(w_in, vres_down):
    gla, hgrn, swa, rwkv = jnp.split(w_in, (784, 784 + 1024, 784 + 1024 + 512), axis=1)
    if vres_down is not None:
        rwkv = jnp.concatenate([rwkv, vres_down], axis=1)
    pad = lambda a, w_: jnp.pad(a, ((0, 0), (0, w_ - a.shape[1])))
    return jnp.concatenate([pad(gla, GLA_W), hgrn, swa, pad(rwkv, RWKV_W)], axis=1).astype(BF16)


class _MixerParts(NamedTuple):
    plan: Callable
    args: list
    in_specs: list
    out_specs: list
    out_shape: list
    scratch: list


def _mixers_kernel(layout, *refs):
    n_in = sum(entry[1] for entry in layout)
    n_out = sum(entry[2] for entry in layout)
    ins, outs, scratch = list(refs[:n_in]), list(refs[n_in:n_in + n_out]), list(refs[n_in + n_out:])

    @pl.when(pl.program_id(0) == 0)
    def _():
        for ref in scratch:
            ref[...] = jnp.zeros_like(ref)

    plans = []
    for plan, n_i, n_o, n_s in layout:
        plans.append(plan(*ins[:n_i], *outs[:n_o], *scratch[:n_s]))
        del ins[:n_i], outs[:n_o], scratch[:n_s]
    _run_plans(plans)


def _call_mixers(parts_list, tokens, step):
    layout = tuple((p.plan, len(p.args), len(p.out_shape), len(p.scratch)) for p in parts_list)
    flat = lambda field: [item for p in parts_list for item in getattr(p, field)]
    return pl.pallas_call(
        functools.partial(_mixers_kernel, layout),
        grid=(tokens // step,),
        in_specs=flat("in_specs"),
        out_specs=flat("out_specs"),
        out_shape=flat("out_shape"),
        scratch_shapes=flat("scratch"),
        compiler_params=_params(("arbitrary",)),
        name="mixers",
    )(*flat("args"))


def _mix_out(tokens, step):
    return (pl.BlockSpec((step, GROUP_WIDTH), lambda i: (i, 0)),
            jax.ShapeDtypeStruct((tokens, GROUP_WIDTH), MIX_DTYPE))


def _gated_linear_attention_tile(index, q, k, v, log_f, state_box, out_box):
    length, kw = q.shape
    head_k = kw // N_HEADS
    n_sub = length // GLA_SUB
    row = _iota((length, length), 0)
    col = _iota((length, length), 1)
    same_sub = _idiv(row, GLA_SUB) == _idiv(col, GLA_SUB)
    m_local = jnp.where(same_sub & (col <= row), 1.0, 0.0).astype(F32)
    m_prev = jnp.where(_idiv(col, GLA_SUB) < _idiv(row, GLA_SUB), 1.0, 0.0).astype(F32)
    sums = _dot_mask(jnp.concatenate([m_local, m_prev], axis=0), log_f)
    yield
    b_local = sums[0:length]
    b_start = sums[length:2 * length]
    b_full = b_start + b_local
    q_local = q * jnp.exp(b_local)

    q_pos = _imod(_iota((N_HEADS * GLA_SUB, length), 0), GLA_SUB)
    s_pos = _iota((N_HEADS * GLA_SUB, length), 1)
    probs = []
    for c in range(n_sub):
        r0, r1 = c * GLA_SUB, (c + 1) * GLA_SUB
        expo = jnp.where(_iota((r1, kw), 0) < r0, b_start[r0:r0 + 1, :] - b_full[0:r1], -b_local[0:r1])
        k_ref = k[0:r1] * jnp.exp(expo)
        if r1 < length:
            k_ref = jnp.concatenate([k_ref, jnp.zeros((length - r1, kw), F32)], axis=0)
        q_heads = _expand_heads(q_local[r0:r1, :], head_k)
        s = _bdot_nt(q_heads, k_ref)
        probs.append(jnp.where(s_pos <= q_pos + r0, s, 0.0))
    b_total = b_full[length - 1:length, :]
    k_end = k * jnp.exp(b_total - b_full)
    upd = _dot_tn(v.astype(BF16), k_end.astype(BF16))
    yield
    o_heads = _bdot(jnp.concatenate(probs, axis=0), v)
    assert len(out_box) == index, "the previous tile must have replaced the state before it is read"
    state_t = state_box[0]
    o_state = _bdot_nt(q * jnp.exp(b_full), state_t)
    same_head = _idiv(_iota((GROUP_WIDTH, kw), 0), HEAD_DIM) == _idiv(_iota((GROUP_WIDTH, kw), 1), head_k)
    state_box[0] = state_t * jnp.exp(b_total) + jnp.where(same_head, upd, 0.0)
    yield
    v_head = _idiv(_iota((GLA_SUB, GROUP_WIDTH), 1), HEAD_DIM)
    rows = []
    for c in range(n_sub):
        base = c * N_HEADS * GLA_SUB
        acc = jnp.zeros((GLA_SUB, GROUP_WIDTH), F32)
        for h in range(N_HEADS):
            acc = acc + jnp.where(v_head == h, o_heads[base + h * GLA_SUB:base + (h + 1) * GLA_SUB, :], 0.0)
        rows.append(acc)
    out_box.append(jnp.concatenate(rows, axis=0) + o_state)


def _gated_linear_attention_plan(q, k, v, log_f, gate, norm_w, o_ref, state_ref):
    state_box = [state_ref[...]]
    tiles = []
    stages = []
    for n in range(q.shape[0] // GLA_TILE):
        rows = slice(n * GLA_TILE, (n + 1) * GLA_TILE)
        stages.append(_gated_linear_attention_tile(n, q[rows], k[rows], v[rows], log_f[rows], state_box, tiles))

    def finish():
        state_ref[...] = state_box[0]
        o = jnp.concatenate(tiles, axis=0)
        ms = _dot_hilo(o * o, _head_group_matrix(GROUP_WIDTH, HEAD_DIM, 1.0 / HEAD_DIM).astype(BF16))
        o_ref[...] = (o * lax.rsqrt(ms + RMS_EPS) * norm_w * _silu(gate)).astype(o_ref.dtype)

    return stages, GLA_STAGGER, finish


def _gla_plan(z_ref, gk_up_ref, gk_bias_ref, norm_w_ref, o_ref, state_ref):
    z = z_ref[...]
    q = z[:, 0:128] * (GLA_DK ** -0.5)
    k = z[:, 128:256]
    v = z[:, 256:512]
    g = z[:, 512:768]
    gate_in = _dot_3pass(z[:, 768:896], gk_up_ref[...]) + gk_bias_ref[...]
    log_f = _log_sigmoid(gate_in) * (1.0 / GLA_GATE_NORMALIZER)
    return _gated_linear_attention_plan(q, k, v, log_f, g, norm_w_ref[...], o_ref, state_ref)


def _hgrn_plan(z_ref, lb_ref, log_lb_ref, norm_w_ref, o_ref, state_ref):
    z = z_ref[...]
    q = _silu(z[:, 0:256])
    f = z[:, 256:512]
    v = z[:, 512:768]
    g = z[:, 768:1024]
    lb = lb_ref[...]
    a = log_lb_ref[...]
    e = jnp.exp(-jnp.abs(f))
    inv = 1.0 / (1.0 + e)
    c = jnp.log1p(-lb) + jnp.minimum(f, 0.0) - jnp.log(1.0 + e)
    log_f = jnp.maximum(a, c) + _softplus_neg_abs(a - c)
    k = (1.0 - lb) * jnp.where(f >= 0.0, e * inv, inv)
    return _gated_linear_attention_plan(q, k, v, log_f, g, norm_w_ref[...], o_ref, state_ref)


def _gla_parts(z, gk_up, gk_bias, norm_w, step):
    out_spec, out_shape = _mix_out(z.shape[0], step)
    gk_up_pad = jnp.zeros((LANES, N_HEADS * GLA_DK), F32).at[:GLA_GATE_RANK].set(gk_up)
    return _MixerParts(
        plan=_gla_plan,
        args=[z, gk_up_pad, gk_bias.reshape(1, -1), jnp.tile(norm_w, N_HEADS).reshape(1, -1)],
        in_specs=[pl.BlockSpec((step, GLA_W), lambda i: (i, 0)),
                  _resident((LANES, N_HEADS * GLA_DK)), _resident((1, N_HEADS * GLA_DK)),
                  _resident((1, GROUP_WIDTH))],
        out_specs=[out_spec], out_shape=[out_shape],
        scratch=[pltpu.VMEM((GROUP_WIDTH, N_HEADS * GLA_DK), F32)])


def _hgrn_parts(z, lb, norm_w, step):
    out_spec, out_shape = _mix_out(z.shape[0], step)
    return _MixerParts(
        plan=_hgrn_plan,
        args=[z, lb.reshape(1, -1), jnp.log(lb).reshape(1, -1), jnp.tile(norm_w, N_HEADS).reshape(1, -1)],
        in_specs=[pl.BlockSpec((step, HGRN_W), lambda i: (i, 0)),
                  _resident((1, GROUP_WIDTH)), _resident((1, GROUP_WIDTH)), _resident((1, GROUP_WIDTH))],
        out_specs=[out_spec], out_shape=[out_shape],
        scratch=[pltpu.VMEM((GROUP_WIDTH, GROUP_WIDTH), F32)])


def _swa_block(q, kw, vw, visible, sinks, out_box):
    n = q.shape[0]
    head_cols = lambda x, h: x[:, h * HEAD_DIM:(h + 1) * HEAD_DIM]
    v_lane_head = _idiv(_iota(vw.shape, 1), HEAD_DIM)
    outs = []
    for kv in range(N_HEADS // 2):
        heads = (2 * kv, 2 * kv + 1)
        scores = _bdot_nt(jnp.concatenate([head_cols(q, h) for h in heads], axis=0), head_cols(kw, kv))
        yield
        probs, sink_terms = [], []
        for half, h in enumerate(heads):
            s = jnp.where(visible, scores[half * n:(half + 1) * n], -jnp.inf)
            sink = sinks[:, h:h + 1]
            m = jnp.maximum(jnp.max(s, axis=-1, keepdims=True), sink)
            probs.append(jnp.exp(s - m))
            sink_terms.append(jnp.exp(sink - m))
        o = _bdot(jnp.concatenate(probs, axis=0), jnp.where(v_lane_head == kv, vw, 1.0))
        yield
        sums = head_cols(o, 1 - kv)[:, 0:1]
        outs += [head_cols(o, kv)[half * n:(half + 1) * n] / (sums[half * n:(half + 1) * n] + sink_terms[half])
                 for half in range(2)]
    out_box.append(jnp.concatenate(outs, axis=-1))


def _swa_plan(q_ref, k_ref, v_ref, kp_ref, vp_ref, sink_ref, o_ref):
    w = SWA_WINDOW
    has_prev = pl.program_id(0) > 0
    q = q_ref[...] * (HEAD_DIM ** -0.5)
    k_all = jnp.concatenate([kp_ref[...], k_ref[...]], axis=0)
    v_all = jnp.concatenate([vp_ref[...], v_ref[...]], axis=0)
    q_pos = _iota((w, 2 * w), 0) + w
    k_pos = _iota((w, 2 * w), 1)
    dist = q_pos - k_pos
    in_window = (dist >= 0) & (dist < w)
    sinks = sink_ref[...]
    blocks = []
    stages = []
    for b in range(q.shape[0] // w):
        visible = in_window if b > 0 else in_window & ((k_pos >= w) | has_prev)
        stages.append(_swa_block(q[b * w:(b + 1) * w], k_all[b * w:(b + 2) * w], v_all[b * w:(b + 2) * w],
                                 visible, sinks, blocks))

    def finish():
        o_ref[...] = jnp.concatenate(blocks, axis=0).astype(o_ref.dtype)

    return stages, SWA_STAGGER, finish


def _swa_parts(z, sinks, step):
    out_spec, out_shape = _mix_out(z.shape[0], step)
    w = SWA_WINDOW
    prev = lambda col: (lambda i: (jnp.maximum(i * (step // w) - 1, 0), col))
    return _MixerParts(
        plan=_swa_plan,
        args=[z, z, z, z, z, sinks.reshape(1, -1)],
        in_specs=[pl.BlockSpec((step, 256), lambda i: (i, 0)),
                  pl.BlockSpec((step, 128), lambda i: (i, 2)), pl.BlockSpec((step, 128), lambda i: (i, 3)),
                  pl.BlockSpec((w, 128), prev(2)), pl.BlockSpec((w, 128), prev(3)),
                  _resident((1, N_HEADS))],
        out_specs=[out_spec], out_shape=[out_shape], scratch=[])


def _rwkv_chunk(index, r, k, v, a_vec, b_vec, log_w, state_box, out_box):
    c = r.shape[0]
    width = r.shape[1]
    tri = jnp.where(_iota((c, c), 1) <= _iota((c, c), 0), 1.0, 0.0).astype(F32)
    p = _dot_mask(tri, log_w)
    yield
    p_total = p[c - 1:c, :]
    decay_in = jnp.exp(p)
    decay_out = jnp.exp(-p)
    decay_end = jnp.exp(p_total - p)
    a_in = a_vec * jnp.exp(p - log_w)
    r_in = r * decay_in
    b_out = b_vec * decay_out
    k_out = k * decay_out
    b_end = b_vec * decay_end
    k_end = k * decay_end

    t_pos = _iota((c, width), 0)
    assert width == N_HEADS * c
    s_pos = _imod(_iota((c, width), 1), c)
    strict = s_pos < t_pos
    incl = s_pos <= t_pos
    expand = lambda x: _expand_heads(x, HEAD_DIM)

    scores = _bdot_nt(jnp.concatenate([a_in, r_in], axis=0),
                      jnp.concatenate([expand(b_out), expand(k_out)], axis=0))
    yield
    a_ab = jnp.where(strict, scores[0:c, 0:width], 0.0)
    a_ak = jnp.where(strict, scores[0:c, width:2 * width], 0.0)
    a_rb = jnp.where(incl, scores[c:2 * c, 0:width], 0.0)
    a_rk = jnp.where(incl, scores[c:2 * c, width:2 * width], 0.0)

    t_inv = jnp.where(s_pos == t_pos, 1.0, 0.0) + a_ab
    from_v = _bdot(jnp.concatenate([a_ak, a_rk], axis=0), expand(v))
    x1 = from_v[0:c]
    y_from_v = from_v[c:2 * c]
    power = _bdot(a_ab, expand(a_ab))
    yield
    n_factors = (c - 1).bit_length()
    for _ in range(n_factors - 2):
        both = _bdot(jnp.concatenate([t_inv, power], axis=0), expand(power))
        t_inv = t_inv + both[0:c]
        power = both[c:2 * c]
        yield
    t_inv = t_inv + _bdot(t_inv, expand(power))
    yield
    sol = _bdot(t_inv, jnp.concatenate([expand(x1), expand(a_in)], axis=1))
    yield
    u0 = sol[:, 0:width]
    w_mat = sol[:, width:2 * width]

    assert len(out_box) == index, "the previous chunk must have replaced the state before it is read"
    state = state_box[0]
    from_state = _bdot_nt(jnp.concatenate([w_mat, r_in], axis=0), state)
    yield
    u = u0 + from_state[0:c]
    y = _bdot(a_rb, expand(u)) + y_from_v + from_state[c:2 * c]
    upd = _dot_tn(jnp.concatenate([u, v], axis=0).astype(BF16),
                  jnp.concatenate([b_end, k_end], axis=0).astype(BF16))
    same_head = _idiv(_iota((width, width), 0), HEAD_DIM) == _idiv(_iota((width, width), 1), HEAD_DIM)
    state_box[0] = state * jnp.exp(p_total) + jnp.where(same_head, upd, 0.0)
    out_box.append(y)


def _rwkv_plan(has_vres, *refs):
    if has_vres:
        (z_ref, zp_ref, vfirst_ref, mu_ref, w0_ref, wup_ref, a0_ref, aup_ref, gup_ref, kk_ref, ka_ref,
         rk_ref, lnw_ref, lnb_ref, v0_ref, vup_ref, o_ref, state_ref) = refs
    else:
        (z_ref, zp_ref, mu_ref, w0_ref, wup_ref, a0_ref, aup_ref, gup_ref, kk_ref, ka_ref,
         rk_ref, lnw_ref, lnb_ref, o_ref, vout_ref, state_ref) = refs
    step = pl.program_id(0)
    z = z_ref[...]
    last_prev = jnp.where(step > 0, zp_ref[7:8, :], 0.0)
    prev = jnp.where(_iota(z.shape, 0) == 0, last_prev, pltpu.roll(z, 1, axis=0))
    zr = z + (prev - z) * mu_ref[...]
    r = zr[:, 0:256]
    k = zr[:, 256:512]
    v = zr[:, 512:768]
    low = zr[:, 768:896]
    w_pre = w0_ref[...] + _dot_3pass(jnp.tanh(low), wup_ref[...])
    w_log = -(jnp.maximum(-w_pre, 0.0) + _softplus_neg_abs(w_pre)) - 0.5
    log_w = -jnp.exp(w_log)
    a = _sigmoid(a0_ref[...] + _dot_3pass(low, aup_ref[...]))
    g = _dot_3pass(_sigmoid(low), gup_ref[...])
    if has_vres:
        v = v + (vfirst_ref[...] - v) * _sigmoid(v0_ref[...] + _dot_3pass(low, vup_ref[...]))
    else:
        vout_ref[...] = v
    head_sum = _head_group_matrix(GROUP_WIDTH, HEAD_DIM, 1.0).astype(BF16)
    kk = k * kk_ref[...]
    kk = kk / jnp.maximum(jnp.sqrt(_dot_hilo(kk * kk, head_sum)), 1e-12)
    k = k * (1.0 + (a - 1.0) * ka_ref[...])
    a_vec = -kk
    b_vec = kk * a

    c = RWKV_CHUNK
    state_box = [state_ref[...]]
    chunks = []
    stages = []
    for n in range(z.shape[0] // c):
        rows = slice(n * c, (n + 1) * c)
        stages.append(_rwkv_chunk(n, r[rows], k[rows], v[rows], a_vec[rows], b_vec[rows], log_w[rows],
                                  state_box, chunks))

    def finish():
        state_ref[...] = state_box[0]
        y = jnp.concatenate(chunks, axis=0)
        head_mean = _head_group_matrix(GROUP_WIDTH, HEAD_DIM, 1.0 / HEAD_DIM).astype(BF16)
        mu_y = _dot_hilo(y, head_mean)
        d = y - mu_y
        var_y = _dot_hilo(d * d, head_mean)
        y = d * lax.rsqrt(var_y + RWKV_GN_EPS) * lnw_ref[...] + lnb_ref[...]
        bonus = _dot_hilo(r * k * rk_ref[...], head_sum) * v
        o_ref[...] = ((y + bonus) * g).astype(o_ref.dtype)

    return stages, RWKV_STAGGER, finish


def _rwkv_parts(z, mu, w0, w_up, a0, a_up, g_up, k_k, k_a, r_k, lnx_w, lnx_b, vres, step):
    t = z.shape[0]
    c = step
    row = lambda a: a.reshape(1, -1)
    low_rows = lambda a, start: jnp.zeros((LANES, GROUP_WIDTH), F32).at[start:start + a.shape[0]].set(a)
    has_vres = vres is not None
    mu_full = jnp.zeros((RWKV_W,), F32).at[:RWKV_COLS].set(mu)
    vec = _resident((1, GROUP_WIDTH))
    mat = _resident((LANES, GROUP_WIDTH))
    tile = pl.BlockSpec((c, GROUP_WIDTH), lambda i: (i, 0))
    z_specs = [pl.BlockSpec((c, RWKV_W), lambda i: (i, 0)),
               pl.BlockSpec((8, RWKV_W), lambda i: (jnp.maximum(i * (c // 8) - 1, 0), 0))]
    common = [row(w0), low_rows(w_up, 0), row(a0), low_rows(a_up, 16), low_rows(g_up, 32),
              row(k_k), row(k_a), row(r_k), row(lnx_w), row(lnx_b)]
    common_specs = [vec, mat, vec, mat, mat, vec, vec, vec, vec, vec]
    if has_vres:
        v_first, vres_mu, v0, v_up = vres
        mu_full = mu_full.at[RWKV_COLS:RWKV_COLS + RWKV_V_RANK].set(vres_mu)
        args = [z, z, v_first, row(mu_full)] + common + [row(v0), low_rows(v_up, 64)]
        in_specs = z_specs + [tile, _resident((1, RWKV_W))] + common_specs + [vec, mat]
        out_specs = [tile]
        out_shape = [jax.ShapeDtypeStruct((t, GROUP_WIDTH), MIX_DTYPE)]
    else:
        args = [z, z, row(mu_full)] + common
        in_specs = z_specs + [_resident((1, RWKV_W))] + common_specs
        out_specs = [tile, tile]
        out_shape = [jax.ShapeDtypeStruct((t, GROUP_WIDTH), MIX_DTYPE), jax.ShapeDtypeStruct((t, GROUP_WIDTH), F32)]
    return _MixerParts(plan=functools.partial(_rwkv_plan, has_vres), args=args, in_specs=in_specs,
                       out_specs=out_specs, out_shape=out_shape,
                       scratch=[pltpu.VMEM((GROUP_WIDTH, GROUP_WIDTH), F32)])


def _store_token_tiles(ref, index, x):
    n = x.shape[0]
    for j in range(SUBLANES):
        ref[(*index, pl.ds(j, n, stride=SUBLANES), slice(None))] = x[:, j * LANES:(j + 1) * LANES]


def _load_token_tiles(ref, index, n):
    return jnp.concatenate([ref[(*index, pl.ds(j, n, stride=SUBLANES), slice(None))] for j in range(SUBLANES)],
                           axis=-1)


def _top2_route(logits):
    row = _iota(logits.shape, 0).astype(F32)
    m1 = jnp.max(logits, axis=0, keepdims=True)
    i1 = jnp.min(jnp.where(logits == m1, row, N_EXPERTS), axis=0, keepdims=True)
    rest = jnp.where(row == i1, -jnp.inf, logits)
    m2 = jnp.max(rest, axis=0, keepdims=True)
    i2 = jnp.min(jnp.where(rest == m2, row, N_EXPERTS), axis=0, keepdims=True)
    e2 = jnp.exp(m2 - m1)
    g1 = 1.0 / (1.0 + e2)
    g2 = e2 * g1
    return jnp.where(row == 0, i1, jnp.where(row == 1, i2, jnp.where(row == 2, g1, jnp.where(row == 3, g2, 0.0))))


def _mix_residual_norm(alpha, x_ref, mix_refs, w_ref, g_ref, b_ref):
    acc = alpha * x_ref[...]
    for h, ref in enumerate(mix_refs):
        acc = acc + _dot(ref[...], w_ref[h * GROUP_WIDTH:(h + 1) * GROUP_WIDTH, :])
    return _layer_norm(acc, g_ref[...], b_ref[...])


def _outproj_route_kernel(alpha, x_ref, o0_ref, o1_ref, o2_ref, o3_ref, w_ref, g_ref, b_ref, router_ref,
                          tiles_ref, route_ref):
    y = _mix_residual_norm(alpha, x_ref, (o0_ref, o1_ref, o2_ref, o3_ref), w_ref, g_ref, b_ref)
    route_ref[...] = _top2_route(_dot_3pass_nt(router_ref[...], y))
    _store_token_tiles(tiles_ref, (), y)


def _out_proj_ln_route(alpha, x, mixes, w_out, g, b, router):
    t = x.shape[0]
    row_d = pl.BlockSpec((ROW_TILE, D_MODEL), lambda i: (i, 0))
    row_g = pl.BlockSpec((ROW_TILE, GROUP_WIDTH), lambda i: (i, 0))
    return pl.pallas_call(
        functools.partial(_outproj_route_kernel, alpha),
        grid=(t // ROW_TILE,),
        in_specs=[row_d, row_g, row_g, row_g, row_g, _resident((D_MODEL, D_MODEL)),
                  _resident((1, D_MODEL)), _resident((1, D_MODEL)), _resident((N_EXPERTS, D_MODEL))],
        out_specs=[pl.BlockSpec((ROW_TILE * SUBLANES, LANES), lambda i: (i, 0)),
                   pl.BlockSpec((N_EXPERTS, ROW_TILE), lambda i: (0, i))],
        out_shape=[jax.ShapeDtypeStruct((t * SUBLANES, LANES), F32), jax.ShapeDtypeStruct((N_EXPERTS, t), F32)],
        compiler_params=_params(("parallel",)),
        name="out_proj_ln_route",
    )(x, *mixes, w_out.astype(BF16), g.reshape(1, -1), b.reshape(1, -1), router.T)


def _ln_embed(y, ln_g, ln_b, p, ple_gate, ple_proj):
    x = _layer_norm(y, ln_g, ln_b)
    gate = _sigmoid(_dot(x.astype(BF16), ple_gate))
    return x + gate * _dot(p.astype(BF16), ple_proj)


def _dense_ffn_kernel(alpha, x_ref, o0_ref, o1_ref, o2_ref, o3_ref, wo_ref, g1_ref, b1_ref, p_ref, wg_ref, wu_ref,
                      wd_ref, g_ref, b_ref, pg_ref, pp_ref, y_ref, acc_ref):
    x = _mix_residual_norm(alpha, x_ref, (o0_ref, o1_ref, o2_ref, o3_ref), wo_ref, g1_ref, b1_ref)
    xb = x.astype(BF16)
    acc_ref[...] = alpha * x
    for j in range(D_FF // FF_CHUNK):
        cols = slice(j * FF_CHUNK, (j + 1) * FF_CHUNK)
        h = _silu(_dot(xb, wg_ref[:, cols])) * _dot(xb, wu_ref[:, cols])
        acc_ref[...] += _dot(h.astype(BF16), wd_ref[cols, :])
    y_ref[...] = _ln_embed(acc_ref[...], g_ref[...], b_ref[...], p_ref[...], pg_ref[...], pp_ref[...])


def _dense_layer_tail(alpha, x, mixes, w_out, g1, b1, p_all, p_block, w_gate, w_up, w_down, g, b, ple_gate, ple_proj):
    t = x.shape[0]
    row_d = pl.BlockSpec((ROW_TILE, D_MODEL), lambda i: (i, 0))
    row_g = pl.BlockSpec((ROW_TILE, GROUP_WIDTH), lambda i: (i, 0))
    return pl.pallas_call(
        functools.partial(_dense_ffn_kernel, alpha),
        grid=(t // ROW_TILE,),
        in_specs=[row_d, row_g, row_g, row_g, row_g, _resident((D_MODEL, D_MODEL)),
                  _resident((1, D_MODEL)), _resident((1, D_MODEL)),
                  pl.BlockSpec((ROW_TILE, PLE_DIM), lambda i: (p_block + i, 0)),
                  _resident((D_MODEL, D_FF)), _resident((D_MODEL, D_FF)), _resident((D_FF, D_MODEL)),
                  _resident((1, D_MODEL)), _resident((1, D_MODEL)),
                  _resident((D_MODEL, D_MODEL)), _resident((PLE_DIM, D_MODEL))],
        out_specs=row_d,
        out_shape=jax.ShapeDtypeStruct((t, D_MODEL), F32),
        scratch_shapes=[pltpu.VMEM((ROW_TILE, D_MODEL), F32)],
        compiler_params=_params(("parallel",)),
        name="dense_ffn_tail",
    )(x, *mixes, w_out.astype(BF16), g1.reshape(1, -1), b1.reshape(1, -1), p_all,
      w_gate.astype(BF16), w_up.astype(BF16), w_down.astype(BF16), g.reshape(1, -1), b.reshape(1, -1),
      ple_gate.astype(BF16), ple_proj.astype(BF16))


def _weight_group_copies(e, group, wg_hbm, wu_hbm, wd_hbm, stage_cols_ref, stage_rows_ref, sem):
    slot = group % 2
    lo, hi = group * MOE_WEIGHT_CHUNK, (group + 1) * MOE_WEIGHT_CHUNK
    return (pltpu.make_async_copy(wg_hbm.at[e, :, lo:hi], stage_cols_ref.at[slot, 0], sem.at[slot]),
            pltpu.make_async_copy(wu_hbm.at[e, :, lo:hi], stage_cols_ref.at[slot, 1], sem.at[slot]),
            pltpu.make_async_copy(wd_hbm.at[e, lo:hi, :], stage_rows_ref.at[slot], sem.at[slot]))


def _expert_kernel(row_tok_ref, row_dst_ref, block_e_ref, n_used_ref, x_hbm, wg_hbm, wu_hbm, wd_hbm, y_hbm,
                   rows_ref, xb_ref, acc_ref, ybuf_ref, wg_ref, wu_ref, wd_ref, stage_cols_ref, stage_rows_ref,
                   gather_sem, scatter_sem, weight_sem):
    i = pl.program_id(0)
    n_used = n_used_ref[0]
    last = pl.num_programs(0) - 1
    n = MOE_ROWS
    slot = lax.rem(i, 2)
    other = 1 - slot
    tile = lambda first_row: pl.ds(pl.multiple_of(first_row, SUBLANES), SUBLANES)
    gather_row = lambda tok_row, s, r: pltpu.make_async_copy(
        x_hbm.at[tile(tok_row)], rows_ref.at[s, tile(r * SUBLANES)], gather_sem.at[s])
    scatter_row = lambda dst_row, s, r: pltpu.make_async_copy(
        ybuf_ref.at[s, tile(r * SUBLANES)], y_hbm.at[tile(dst_row)], scatter_sem.at[s])
    block_rows = n * SUBLANES
    gather_block = lambda s: pltpu.make_async_copy(x_hbm.at[pl.ds(0, block_rows)], rows_ref.at[s], gather_sem.at[s])
    scatter_block = lambda s: pltpu.make_async_copy(ybuf_ref.at[s], y_hbm.at[pl.ds(0, block_rows)],
                                                    scatter_sem.at[s])

    @pl.when(i == 0)
    def _():
        ybuf_ref[1] = jnp.zeros((block_rows, LANES), F32)

        def start(r, carry):
            gather_row(row_tok_ref[r], 0, r).start()
            return carry

        lax.fori_loop(0, n, start, 0)

    def scatter_all(first_dst, s):
        def start(r, carry):
            scatter_row(row_dst_ref[first_dst + r], s, r).start()
            return carry

        lax.fori_loop(0, n, start, 0)
        scatter_block(s).wait()

    expert = block_e_ref[i]
    new_expert = (i == 0) | (expert != block_e_ref[jnp.maximum(i - 1, 0)])

    weight_group = lambda g: _weight_group_copies(expert, g, wg_hbm, wu_hbm, wd_hbm, stage_cols_ref,
                                                  stage_rows_ref, weight_sem)
    chunks_per_group = MOE_WEIGHT_CHUNK // FF_CHUNK
    n_groups = D_FF_EXPERT // MOE_WEIGHT_CHUNK

    def take_weight_group(g):
        if g + 1 < n_groups:
            for copy in weight_group(g + 1):
                copy.start()
        for copy in weight_group(g):
            copy.wait()
        piece = slice(g * MOE_WEIGHT_CHUNK, (g + 1) * MOE_WEIGHT_CHUNK)
        wg_ref[:, piece] = stage_cols_ref[g % 2, 0].astype(BF16)
        wu_ref[:, piece] = stage_cols_ref[g % 2, 1].astype(BF16)
        wd_ref[piece, :] = stage_rows_ref[g % 2].astype(BF16)

    @pl.when(i < n_used)
    def _():
        @pl.when(new_expert)
        def _():
            for copy in weight_group(0):
                copy.start()

        gather_block(slot).wait()
        xb_ref[...] = _load_token_tiles(rows_ref, (slot,), n).astype(BF16)
        n_chunks = D_FF_EXPERT // FF_CHUNK
        rows_per_chunk = -(-n // (n_chunks - MOE_DMA_FREE_CHUNKS))
        for j in range(n_chunks):
            if j % chunks_per_group == 0:
                pl.when(new_expert)(functools.partial(take_weight_group, j // chunks_per_group))
            cols = slice(j * FF_CHUNK, (j + 1) * FF_CHUNK)
            xb = xb_ref[...]
            h = _silu(_dot(xb, wg_ref[:, cols])) * _dot(xb, wu_ref[:, cols])
            part = _dot(h.astype(BF16), wd_ref[cols, :])
            if j == 0:
                acc_ref[...] = part
            else:
                acc_ref[...] += part
            for r in range(j * rows_per_chunk, min((j + 1) * rows_per_chunk, n)):
                gather_row(row_tok_ref[(i + 1) * n + r], other, r).start(priority=GATHER_DMA_PRIORITY)
                scatter_row(row_dst_ref[i * n + r], other, r).start(priority=SCATTER_DMA_PRIORITY)
        _store_token_tiles(ybuf_ref, (slot,), acc_ref[...])
        scatter_block(other).wait()

    @pl.when(i == n_used)
    def _():
        gather_block(slot).wait()
        scatter_all(i * n, other)

    @pl.when(i >= n_used)
    def _():
        rows_ref[slot] = jnp.zeros((block_rows, LANES), F32)
        fill = pltpu.make_async_copy(rows_ref.at[slot], y_hbm.at[pl.ds(i * block_rows, block_rows)],
                                     scatter_sem.at[slot])
        fill.start()
        fill.wait()

    @pl.when((i == last) & (i < n_used))
    def _():
        gather_block(other).wait()
        scatter_all((i + 1) * n, slot)


def _expert_rows(x, row_tok, row_dst, block_e, n_used, n_out_rows, w_gate, w_up, w_down):
    n_blocks = block_e.shape[0]
    hbm = pl.BlockSpec(memory_space=pl.ANY)
    grid_spec = pltpu.PrefetchScalarGridSpec(
        num_scalar_prefetch=4,
        grid=(n_blocks,),
        in_specs=[hbm, hbm, hbm, hbm],
        out_specs=hbm,
        scratch_shapes=[pltpu.VMEM((2, MOE_ROWS * SUBLANES, LANES), F32), pltpu.VMEM((MOE_ROWS, D_MODEL), BF16),
                        pltpu.VMEM((MOE_ROWS, D_MODEL), F32), pltpu.VMEM((2, MOE_ROWS * SUBLANES, LANES), F32),
                        pltpu.VMEM((D_MODEL, D_FF_EXPERT), BF16), pltpu.VMEM((D_MODEL, D_FF_EXPERT), BF16),
                        pltpu.VMEM((D_FF_EXPERT, D_MODEL), BF16),
                        pltpu.VMEM((2, 2, D_MODEL, MOE_WEIGHT_CHUNK), F32),
                        pltpu.VMEM((2, MOE_WEIGHT_CHUNK, D_MODEL), F32),
                        pltpu.SemaphoreType.DMA((2,)), pltpu.SemaphoreType.DMA((2,)),
                        pltpu.SemaphoreType.DMA((2,))],
    )
    return pl.pallas_call(
        _expert_kernel,
        grid_spec=grid_spec,
        out_shape=jax.ShapeDtypeStruct((n_out_rows * SUBLANES, LANES), F32),
        compiler_params=_params(("arbitrary",)),
        name="moe_experts",
    )(row_tok, row_dst, block_e, n_used.reshape(1), x, w_gate, w_up, w_down)


def _combine_kernel(alpha, x_ref, y0_ref, y1_ref, gates_ref, p_ref, g_ref, b_ref, pg_ref, pp_ref, o_ref):
    n = ROW_TILE
    gates = gates_ref[...]
    f = _load_token_tiles(y0_ref, (), n) * gates[:, 0:1] + _load_token_tiles(y1_ref, (), n) * gates[:, 1:2]
    o_ref[...] = _ln_embed(alpha * _load_token_tiles(x_ref, (), n) + f, g_ref[...], b_ref[...], p_ref[...],
                           pg_ref[...], pp_ref[...])


def _moe_combine_tail(alpha, x_tiles, p_all, p_block, y_tiles, gates, g, b, ple_gate, ple_proj):
    t = x_tiles.shape[0] // SUBLANES
    n = ROW_TILE
    row_d = pl.BlockSpec((n, D_MODEL), lambda i: (i, 0))
    tiles_d = pl.BlockSpec((n * SUBLANES, LANES), lambda i: (i, 0))
    return pl.pallas_call(
        functools.partial(_combine_kernel, alpha),
        grid=(t // n,),
        in_specs=[tiles_d, tiles_d, pl.BlockSpec((n * SUBLANES, LANES), lambda i: (t // n + i, 0)),
                  pl.BlockSpec((n, 2), lambda i: (i, 0)), pl.BlockSpec((n, PLE_DIM), lambda i: (p_block + i, 0)),
                  _resident((1, D_MODEL)), _resident((1, D_MODEL)),
                  _resident((D_MODEL, D_MODEL)), _resident((PLE_DIM, D_MODEL))],
        out_specs=row_d,
        out_shape=jax.ShapeDtypeStruct((t, D_MODEL), F32),
        compiler_params=_params(("parallel",)),
        name="moe_combine_tail",
    )(x_tiles, y_tiles, y_tiles, gates, p_all, g.reshape(1, -1), b.reshape(1, -1), ple_gate.astype(BF16),
      ple_proj.astype(BF16))


def _moe_tail(alpha, x_tiles, route, p_all, p_block, w_gate, w_up, w_down, g, b, ple_gate, ple_proj):
    t = route.shape[1]
    experts = route[0:2].T.astype(jnp.int32)
    gates = route[2:4].T
    e_flat = experts.reshape(-1)
    onehot = (e_flat[:, None] == jnp.arange(N_EXPERTS, dtype=jnp.int32)[None, :]).astype(jnp.int32)
    running = jnp.cumsum(onehot, axis=0)
    rank = jnp.sum(onehot * (running - 1), axis=1)
    counts = running[-1]
    padded = (counts + MOE_ROWS - 1) // MOE_ROWS * MOE_ROWS
    pad_end = jnp.cumsum(padded)
    pad_start = pad_end - padded
    dest = (pad_start[e_flat] + rank).astype(jnp.int32)
    n_blocks = (2 * t) // MOE_ROWS + N_EXPERTS
    n_rows = n_blocks * MOE_ROWS
    assign = jnp.arange(2 * t, dtype=jnp.int32)
    row_assign = jnp.full((n_rows,), -1, jnp.int32).at[dest].set(assign)
    used = row_assign >= 0
    spare = 2 * t + jnp.cumsum(jnp.where(used, 0, 1).astype(jnp.int32)) - 1
    row_tok = jnp.where(used, row_assign // 2, 0)
    row_dst = jnp.where(used, (row_assign % 2) * t + row_assign // 2, spare)
    n_spare = n_rows - 2 * t
    first_dst = 2 * t + n_spare + jnp.arange(MOE_ROWS, dtype=jnp.int32)
    row_tok = jnp.concatenate([row_tok, jnp.zeros((MOE_ROWS,), jnp.int32)])
    row_dst = jnp.concatenate([first_dst, row_dst])
    block_start = jnp.arange(n_blocks, dtype=jnp.int32) * MOE_ROWS
    block_e = jnp.minimum(jnp.sum((block_start[:, None] >= pad_end[None, :]).astype(jnp.int32), axis=1),
                          N_EXPERTS - 1)
    n_used = (pad_end[-1] // MOE_ROWS).astype(jnp.int32)
    y_tiles = _expert_rows(x_tiles, row_tok * SUBLANES, row_dst * SUBLANES, block_e, n_used,
                           2 * t + n_spare + MOE_ROWS, w_gate, w_up, w_down)
    return _moe_combine_tail(alpha, x_tiles, p_all, p_block, y_tiles, gates, g, b, ple_gate, ple_proj)


def kernel(x, p, w_in, w_out, gla_gk_up, gla_gk_bias, gla_norm_w, hgrn_lower_bounds, hgrn_norm_w, swa_sinks, rwkv_mu, rwkv_w0, rwkv_w_up, rwkv_a0, rwkv_a_up, rwkv_g_up, rwkv_k_k, rwkv_k_a, rwkv_r_k, rwkv_lnx_w, rwkv_lnx_b, rwkv_vres_down, rwkv_vres_mu, rwkv_v0, rwkv_vres_up, ln1_g, ln1_b, ln2_g, ln2_b, ffn_w_gate, ffn_w_up, ffn_w_down, moe_router, moe_w_gate, moe_w_up, moe_w_down, ple_proj, ple_gate):
    bsz, seq, d = x.shape
    depth = w_in.shape[0]
    alpha = (2.0 * depth) ** 0.25
    lbs = jnp.cumsum(jax.nn.softmax(hgrn_lower_bounds.astype(F32), axis=0), axis=0)
    lbs = lbs - lbs[0]
    p_all = p.reshape(-1, PLE_DIM)
    outs = []
    for bi in range(bsz):
        xt = x[bi]
        v_first = None
        for i in range(depth):
            w = _group_in_weights(w_in[i], None if i == 0 else rwkv_vres_down[i - 1])
            z_gla, z_hgrn, z_swa, z_rwkv = _in_proj(xt, w)
            vres = None if i == 0 else (v_first, rwkv_vres_mu[i - 1], rwkv_v0[i - 1], rwkv_vres_up[i - 1])
            mixed = _call_mixers(
                [_gla_parts(z_gla, gla_gk_up[i], gla_gk_bias[i], gla_norm_w[i], MIX_STEP),
                 _hgrn_parts(z_hgrn, lbs[i], hgrn_norm_w[i], MIX_STEP),
                 _swa_parts(z_swa, swa_sinks[i], MIX_STEP),
                 _rwkv_parts(z_rwkv, rwkv_mu[i], rwkv_w0[i], rwkv_w_up[i], rwkv_a0[i], rwkv_a_up[i],
                             rwkv_g_up[i], rwkv_k_k[i], rwkv_k_a[i], rwkv_r_k[i].reshape(-1),
                             rwkv_lnx_w[i], rwkv_lnx_b[i], vres, MIX_STEP)],
                seq, MIX_STEP)
            mixes = tuple(mixed[0:4])
            if i == 0:
                v_first = mixed[4]
            j = i // 2
            p_block = (i * bsz + bi) * (seq // ROW_TILE)
            if i % 2 == 0:
                xt = _dense_layer_tail(alpha, xt, mixes, w_out[i], ln1_g[i], ln1_b[i], p_all, p_block,
                                       ffn_w_gate[j], ffn_w_up[j], ffn_w_down[j], ln2_g[i], ln2_b[i],
                                       ple_gate[i], ple_proj[i])
            else:
                x_tiles, route = _out_proj_ln_route(alpha, xt, mixes, w_out[i], ln1_g[i], ln1_b[i], moe_router[j])
                xt = _moe_tail(alpha, x_tiles, route, p_all, p_block, moe_w_gate[j], moe_w_up[j], moe_w_down[j],
                               ln2_g[i], ln2_b[i], ple_gate[i], ple_proj[i])
        outs.append(xt)
    return jnp.stack(outs, axis=0)
```

```python
import functools
from typing import Callable, NamedTuple

import jax
import jax.numpy as jnp
from jax import lax
from jax.experimental import pallas as pl
from jax.experimental.pallas import tpu as pltpu

F32 = jnp.float32
BF16 = jnp.bfloat16
MIX_DTYPE = BF16

D_MODEL = 1024
GROUP_WIDTH = 256
N_HEADS = 4
HEAD_DIM = 64
GLA_DK = 32
GLA_GATE_RANK = 16
GLA_GATE_NORMALIZER = 16.0
SWA_WINDOW = 128
RWKV_COLS = 3 * GROUP_WIDTH + 16 + 16 + 32
RWKV_V_RANK = 8
LN_EPS = 1e-5
RMS_EPS = 1e-6
RWKV_GN_EPS = 64e-5
D_FF = 2816
N_EXPERTS = 8
D_FF_EXPERT = 3584
PLE_DIM = 256

LANES = 128
SUBLANES = 8
GLA_W = 896
HGRN_W = 1024
SWA_W = 512
RWKV_W = 896
Z_W = GLA_W + HGRN_W + SWA_W + RWKV_W

GLA_SUB = 16
GLA_TILE = 128
GLA_STAGGER = 5
SWA_STAGGER = 5
MIX_STEP = 1024
RWKV_CHUNK = 64
RWKV_STAGGER = 2
ROW_TILE = 512
FF_CHUNK = 256
MOE_ROWS = 512
MOE_WEIGHT_CHUNK = 512
MOE_DMA_FREE_CHUNKS = 10
GATHER_DMA_PRIORITY = 0
SCATTER_DMA_PRIORITY = 1
VMEM_LIMIT = 56 * 1024 * 1024


def _iota(shape, dim):
    return lax.broadcasted_iota(jnp.int32, shape, dim)


def _idiv(x, n):
    return jnp.right_shift(x, n.bit_length() - 1)


def _imod(x, n):
    return jnp.bitwise_and(x, n - 1)


def _dot(a, b):
    return jnp.dot(a, b, preferred_element_type=F32)


def _dot_nt(a, b):
    return lax.dot_general(a, b, (((1,), (1,)), ((), ())), preferred_element_type=F32)


def _dot_tn(a, b):
    return lax.dot_general(a, b, (((0,), (0,)), ((), ())), preferred_element_type=F32)


def _bdot(a, b):
    return _dot(a.astype(BF16), b.astype(BF16))


def _bdot_nt(a, b):
    return _dot_nt(a.astype(BF16), b.astype(BF16))


def _dot_hilo(x, m):
    hi = x.astype(BF16)
    lo = (x - hi.astype(F32)).astype(BF16)
    return _dot(hi, m) + _dot(lo, m)


def _dot_mask(m, x):
    m = m.astype(BF16)
    x1 = x.astype(BF16)
    r1 = x - x1.astype(F32)
    x2 = r1.astype(BF16)
    x3 = (r1 - x2.astype(F32)).astype(BF16)
    return _dot(m, x1) + _dot(m, x2) + _dot(m, x3)


def _dot_3pass_nt(x, w):
    x_hi = x.astype(BF16)
    x_lo = (x - x_hi.astype(F32)).astype(BF16)
    w_hi = w.astype(BF16)
    w_lo = (w - w_hi.astype(F32)).astype(BF16)
    return _dot_nt(x_hi, w_hi) + _dot_nt(x_lo, w_hi) + _dot_nt(x_hi, w_lo)


def _dot_3pass(x, w):
    x_hi = x.astype(BF16)
    x_lo = (x - x_hi.astype(F32)).astype(BF16)
    w_hi = w.astype(BF16)
    w_lo = (w - w_hi.astype(F32)).astype(BF16)
    return _dot(x_hi, w_hi) + _dot(x_lo, w_hi) + _dot(x_hi, w_lo)


def _sigmoid(x):
    return 1.0 / (1.0 + jnp.exp(-x))


def _silu(x):
    return x * _sigmoid(x)


def _softplus_neg_abs(x):
    return jnp.log(1.0 + jnp.exp(-jnp.abs(x)))


def _log_sigmoid(x):
    return jnp.minimum(x, 0.0) - _softplus_neg_abs(x)


def _layer_norm(y, g, b):
    mu = jnp.mean(y, axis=-1, keepdims=True)
    d = y - mu
    var = jnp.mean(d * d, axis=-1, keepdims=True)
    return d * lax.rsqrt(var + LN_EPS) * g + b


def _expand_heads(x, head_width):
    lane_head = _idiv(_iota(x.shape, 1), head_width)
    return jnp.concatenate([jnp.where(lane_head == h, x, 0.0) for h in range(N_HEADS)], axis=0)


def _head_group_matrix(width, head_width, value):
    same = _idiv(_iota((width, width), 0), head_width) == _idiv(_iota((width, width), 1), head_width)
    return jnp.where(same, value, 0.0).astype(F32)


def _resident(shape):
    nd = len(shape)
    return pl.BlockSpec(shape, lambda *_: (0,) * nd, pipeline_mode=pl.Buffered(1))


def _params(semantics):
    return pltpu.CompilerParams(dimension_semantics=semantics, vmem_limit_bytes=VMEM_LIMIT)


def _run_plans(plans):
    live = [dict(enumerate(stages)) for stages, _, _ in plans]
    rnd = 0
    while any(live):
        for group, (_, stagger, _) in zip(live, plans):
            for n in sorted(group):
                if rnd >= n * stagger:
                    try:
                        next(group[n])
                    except StopIteration:
                        del group[n]
        rnd += 1
    for _, _, finish in plans:
        finish()


def _inproj_kernel(x_ref, w_ref, gla_ref, hgrn_ref, swa_ref, rwkv_ref):
    xb = x_ref[...].astype(BF16)
    o = 0
    for ref, width in ((gla_ref, GLA_W), (hgrn_ref, HGRN_W), (swa_ref, SWA_W), (rwkv_ref, RWKV_W)):
        ref[...] = _dot(xb, w_ref[:, o:o + width])
        o += width


def _in_proj(x, w):
    t = x.shape[0]
    widths = (GLA_W, HGRN_W, SWA_W, RWKV_W)
    return pl.pallas_call(
        _inproj_kernel,
        grid=(t // ROW_TILE,),
        in_specs=[pl.BlockSpec((ROW_TILE, D_MODEL), lambda i: (i, 0)), _resident((D_MODEL, Z_W))],
        out_specs=[pl.BlockSpec((ROW_TILE, w_), lambda i: (i, 0)) for w_ in widths],
        out_shape=[jax.ShapeDtypeStruct((t, w_), F32) for w_ in widths],
        compiler_params=_params(("parallel",)),
        name="in_proj",
    )(x, w)


def _group_in_weights(w_in, vres_down):
    gla, hgrn, swa, rwkv = jnp.split(w_in, (784, 784 + 1024, 784 + 1024 + 512), axis=1)
    if vres_down is not None:
        rwkv = jnp.concatenate([rwkv, vres_down], axis=1)
    pad = lambda a, w_: jnp.pad(a, ((0, 0), (0, w_ - a.shape[1])))
    return jnp.concatenate([pad(gla, GLA_W), hgrn, swa, pad(rwkv, RWKV_W)], axis=1).astype(BF16)


class _MixerParts(NamedTuple):
    plan: Callable
    args: list
    in_specs: list
    out_specs: list
    out_shape: list
    scratch: list


def _mixers_kernel(layout, *refs):
    n_in = sum(entry[1] for entry in layout)
    n_out = sum(entry[2] for entry in layout)
    ins, outs, scratch = list(refs[:n_in]), list(refs[n_in:n_in + n_out]), list(refs[n_in + n_out:])

    @pl.when(pl.program_id(0) == 0)
    def _():
        for ref in scratch:
            ref[...] = jnp.zeros_like(ref)

    plans = []
    for plan, n_i, n_o, n_s in layout:
        plans.append(plan(*ins[:n_i], *outs[:n_o], *scratch[:n_s]))
        del ins[:n_i], outs[:n_o], scratch[:n_s]
    _run_plans(plans)


def _call_mixers(parts_list, tokens, step):
    layout = tuple((p.plan, len(p.args), len(p.out_shape), len(p.scratch)) for p in parts_list)
    flat = lambda field: [item for p in parts_list for item in getattr(p, field)]
    return pl.pallas_call(
        functools.partial(_mixers_kernel, layout),
        grid=(tokens // step,),
        in_specs=flat("in_specs"),
        out_specs=flat("out_specs"),
        out_shape=flat("out_shape"),
        scratch_shapes=flat("scratch"),
        compiler_params=_params(("arbitrary",)),
        name="mixers",
    )(*flat("args"))


def _mix_out(tokens, step):
    return (pl.BlockSpec((step, GROUP_WIDTH), lambda i: (i, 0)),
            jax.ShapeDtypeStruct((tokens, GROUP_WIDTH), MIX_DTYPE))


def _gated_linear_attention_tile(index, q, k, v, log_f, state_box, out_box):
    length, kw = q.shape
    head_k = kw // N_HEADS
    n_sub = length // GLA_SUB
    row = _iota((length, length), 0)
    col = _iota((length, length), 1)
    same_sub = _idiv(row, GLA_SUB) == _idiv(col, GLA_SUB)
    m_local = jnp.where(same_sub & (col <= row), 1.0, 0.0).astype(F32)
    m_prev = jnp.where(_idiv(col, GLA_SUB) < _idiv(row, GLA_SUB), 1.0, 0.0).astype(F32)
    sums = _dot_mask(jnp.concatenate([m_local, m_prev], axis=0), log_f)
    yield
    b_local = sums[0:length]
    b_start = sums[length:2 * length]
    b_full = b_start + b_local
    q_local = q * jnp.exp(b_local)

    q_pos = _imod(_iota((N_HEADS * GLA_SUB, length), 0), GLA_SUB)
    s_pos = _iota((N_HEADS * GLA_SUB, length), 1)
    probs = []
    for c in range(n_sub):
        r0, r1 = c * GLA_SUB, (c + 1) * GLA_SUB
        expo = jnp.where(_iota((r1, kw), 0) < r0, b_start[r0:r0 + 1, :] - b_full[0:r1], -b_local[0:r1])
        k_ref = k[0:r1] * jnp.exp(expo)
        if r1 < length:
            k_ref = jnp.concatenate([k_ref, jnp.zeros((length - r1, kw), F32)], axis=0)
        q_heads = _expand_heads(q_local[r0:r1, :], head_k)
        s = _bdot_nt(q_heads, k_ref)
        probs.append(jnp.where(s_pos <= q_pos + r0, s, 0.0))
    b_total = b_full[length - 1:length, :]
    k_end = k * jnp.exp(b_total - b_full)
    upd = _dot_tn(v.astype(BF16), k_end.astype(BF16))
    yield
    o_heads = _bdot(jnp.concatenate(probs, axis=0), v)
    assert len(out_box) == index, "the previous tile must have replaced the state before it is read"
    state_t = state_box[0]
    o_state = _bdot_nt(q * jnp.exp(b_full), state_t)
    same_head = _idiv(_iota((GROUP_WIDTH, kw), 0), HEAD_DIM) == _idiv(_iota((GROUP_WIDTH, kw), 1), head_k)
    state_box[0] = state_t * jnp.exp(b_total) + jnp.where(same_head, upd, 0.0)
    yield
    v_head = _idiv(_iota((GLA_SUB, GROUP_WIDTH), 1), HEAD_DIM)
    rows = []
    for c in range(n_sub):
        base = c * N_HEADS * GLA_SUB
        acc = jnp.zeros((GLA_SUB, GROUP_WIDTH), F32)
        for h in range(N_HEADS):
            acc = acc + jnp.where(v_head == h, o_heads[base + h * GLA_SUB:base + (h + 1) * GLA_SUB, :], 0.0)
        rows.append(acc)
    out_box.append(jnp.concatenate(rows, axis=0) + o_state)


def _gated_linear_attention_plan(q, k, v, log_f, gate, norm_w, o_ref, state_ref):
    state_box = [state_ref[...]]
    tiles = []
    stages = []
    for n in range(q.shape[0] // GLA_TILE):
        rows = slice(n * GLA_TILE, (n + 1) * GLA_TILE)
        stages.append(_gated_linear_attention_tile(n, q[rows], k[rows], v[rows], log_f[rows], state_box, tiles))

    def finish():
        state_ref[...] = state_box[0]
        o = jnp.concatenate(tiles, axis=0)
        ms = _dot_hilo(o * o, _head_group_matrix(GROUP_WIDTH, HEAD_DIM, 1.0 / HEAD_DIM).astype(BF16))
        o_ref[...] = (o * lax.rsqrt(ms + RMS_EPS) * norm_w * _silu(gate)).astype(o_ref.dtype)

    return stages, GLA_STAGGER, finish


def _gla_plan(z_ref, gk_up_ref, gk_bias_ref, norm_w_ref, o_ref, state_ref):
    z = z_ref[...]
    q = z[:, 0:128] * (GLA_DK ** -0.5)
    k = z[:, 128:256]
    v = z[:, 256:512]
    g = z[:, 512:768]
    gate_in = _dot_3pass(z[:, 768:896], gk_up_ref[...]) + gk_bias_ref[...]
    log_f = _log_sigmoid(gate_in) * (1.0 / GLA_GATE_NORMALIZER)
    return _gated_linear_attention_plan(q, k, v, log_f, g, norm_w_ref[...], o_ref, state_ref)


def _hgrn_plan(z_ref, lb_ref, log_lb_ref, norm_w_ref, o_ref, state_ref):
    z = z_ref[...]
    q = _silu(z[:, 0:256])
    f = z[:, 256:512]
    v = z[:, 512:768]
    g = z[:, 768:1024]
    lb = lb_ref[...]
    a = log_lb_ref[...]
    e = jnp.exp(-jnp.abs(f))
    inv = 1.0 / (1.0 + e)
    c = jnp.log1p(-lb) + jnp.minimum(f, 0.0) - jnp.log(1.0 + e)
    log_f = jnp.maximum(a, c) + _softplus_neg_abs(a - c)
    k = (1.0 - lb) * jnp.where(f >= 0.0, e * inv, inv)
    return _gated_linear_attention_plan(q, k, v, log_f, g, norm_w_ref[...], o_ref, state_ref)


def _gla_parts(z, gk_up, gk_bias, norm_w, step):
    out_spec, out_shape = _mix_out(z.shape[0], step)
    gk_up_pad = jnp.zeros((LANES, N_HEADS * GLA_DK), F32).at[:GLA_GATE_RANK].set(gk_up)
    return _MixerParts(
        plan=_gla_plan,
        args=[z, gk_up_pad, gk_bias.reshape(1, -1), jnp.tile(norm_w, N_HEADS).reshape(1, -1)],
        in_specs=[pl.BlockSpec((step, GLA_W), lambda i: (i, 0)),
                  _resident((LANES, N_HEADS * GLA_DK)), _resident((1, N_HEADS * GLA_DK)),
                  _resident((1, GROUP_WIDTH))],
        out_specs=[out_spec], out_shape=[out_shape],
        scratch=[pltpu.VMEM((GROUP_WIDTH, N_HEADS * GLA_DK), F32)])


def _hgrn_parts(z, lb, norm_w, step):
    out_spec, out_shape = _mix_out(z.shape[0], step)
    return _MixerParts(
        plan=_hgrn_plan,
        args=[z, lb.reshape(1, -1), jnp.log(lb).reshape(1, -1), jnp.tile(norm_w, N_HEADS).reshape(1, -1)],
        in_specs=[pl.BlockSpec((step, HGRN_W), lambda i: (i, 0)),
                  _resident((1, GROUP_WIDTH)), _resident((1, GROUP_WIDTH)), _resident((1, GROUP_WIDTH))],
        out_specs=[out_spec], out_shape=[out_shape],
        scratch=[pltpu.VMEM((GROUP_WIDTH, GROUP_WIDTH), F32)])


def _swa_block(q, kw, vw, visible, sinks, out_box):
    n = q.shape[0]
    head_cols = lambda x, h: x[:, h * HEAD_DIM:(h + 1) * HEAD_DIM]
    v_lane_head = _idiv(_iota(vw.shape, 1), HEAD_DIM)
    outs = []
    for kv in range(N_HEADS // 2):
        heads = (2 * kv, 2 * kv + 1)
        scores = _bdot_nt(jnp.concatenate([head_cols(q, h) for h in heads], axis=0), head_cols(kw, kv))
        yield
        probs, sink_terms = [], []
        for half, h in enumerate(heads):
            s = jnp.where(visible, scores[half * n:(half + 1) * n], -jnp.inf)
            sink = sinks[:, h:h + 1]
            m = jnp.maximum(jnp.max(s, axis=-1, keepdims=True), sink)
            probs.append(jnp.exp(s - m))
            sink_terms.append(jnp.exp(sink - m))
        o = _bdot(jnp.concatenate(probs, axis=0), jnp.where(v_lane_head == kv, vw, 1.0))
        yield
        sums = head_cols(o, 1 - kv)[:, 0:1]
        outs += [head_cols(o, kv)[half * n:(half + 1) * n] / (sums[half * n:(half + 1) * n] + sink_terms[half])
                 for half in range(2)]
    out_box.append(jnp.concatenate(outs, axis=-1))


def _swa_plan(q_ref, k_ref, v_ref, kp_ref, vp_ref, sink_ref, o_ref):
    w = SWA_WINDOW
    has_prev = pl.program_id(0) > 0
    q = q_ref[...] * (HEAD_DIM ** -0.5)
    k_all = jnp.concatenate([kp_ref[...], k_ref[...]], axis=0)
    v_all = jnp.concatenate([vp_ref[...], v_ref[...]], axis=0)
    q_pos = _iota((w, 2 * w), 0) + w
    k_pos = _iota((w, 2 * w), 1)
    dist = q_pos - k_pos
    in_window = (dist >= 0) & (dist < w)
    sinks = sink_ref[...]
    blocks = []
    stages = []
    for b in range(q.shape[0] // w):
        visible = in_window if b > 0 else in_window & ((k_pos >= w) | has_prev)
        stages.append(_swa_block(q[b * w:(b + 1) * w], k_all[b * w:(b + 2) * w], v_all[b * w:(b + 2) * w],
                                 visible, sinks, blocks))

    def finish():
        o_ref[...] = jnp.concatenate(blocks, axis=0).astype(o_ref.dtype)

    return stages, SWA_STAGGER, finish


def _swa_parts(z, sinks, step):
    out_spec, out_shape = _mix_out(z.shape[0], step)
    w = SWA_WINDOW
    prev = lambda col: (lambda i: (jnp.maximum(i * (step // w) - 1, 0), col))
    return _MixerParts(
        plan=_swa_plan,
        args=[z, z, z, z, z, sinks.reshape(1, -1)],
        in_specs=[pl.BlockSpec((step, 256), lambda i: (i, 0)),
                  pl.BlockSpec((step, 128), lambda i: (i, 2)), pl.BlockSpec((step, 128), lambda i: (i, 3)),
                  pl.BlockSpec((w, 128), prev(2)), pl.BlockSpec((w, 128), prev(3)),
                  _resident((1, N_HEADS))],
        out_specs=[out_spec], out_shape=[out_shape], scratch=[])


def _rwkv_chunk(index, r, k, v, a_vec, b_vec, log_w, state_box, out_box):
    c = r.shape[0]
    width = r.shape[1]
    tri = jnp.where(_iota((c, c), 1) <= _iota((c, c), 0), 1.0, 0.0).astype(F32)
    p = _dot_mask(tri, log_w)
    yield
    p_total = p[c - 1:c, :]
    decay_in = jnp.exp(p)
    decay_out = jnp.exp(-p)
    decay_end = jnp.exp(p_total - p)
    a_in = a_vec * jnp.exp(p - log_w)
    r_in = r * decay_in
    b_out = b_vec * decay_out
    k_out = k * decay_out
    b_end = b_vec * decay_end
    k_end = k * decay_end

    t_pos = _iota((c, width), 0)
    assert width == N_HEADS * c
    s_pos = _imod(_iota((c, width), 1), c)
    strict = s_pos < t_pos
    incl = s_pos <= t_pos
    expand = lambda x: _expand_heads(x, HEAD_DIM)

    scores = _bdot_nt(jnp.concatenate([a_in, r_in], axis=0),
                      jnp.concatenate([expand(b_out), expand(k_out)], axis=0))
    yield
    a_ab = jnp.where(strict, scores[0:c, 0:width], 0.0)
    a_ak = jnp.where(strict, scores[0:c, width:2 * width], 0.0)
    a_rb = jnp.where(incl, scores[c:2 * c, 0:width], 0.0)
    a_rk = jnp.where(incl, scores[c:2 * c, width:2 * width], 0.0)

    t_inv = jnp.where(s_pos == t_pos, 1.0, 0.0) + a_ab
    from_v = _bdot(jnp.concatenate([a_ak, a_rk], axis=0), expand(v))
    x1 = from_v[0:c]
    y_from_v = from_v[c:2 * c]
    power = _bdot(a_ab, expand(a_ab))
    yield
    n_factors = (c - 1).bit_length()
    for _ in range(n_factors - 2):
        both = _bdot(jnp.concatenate([t_inv, power], axis=0), expand(power))
        t_inv = t_inv + both[0:c]
        power = both[c:2 * c]
        yield
    t_inv = t_inv + _bdot(t_inv, expand(power))
    yield
    sol = _bdot(t_inv, jnp.concatenate([expand(x1), expand(a_in)], axis=1))
    yield
    u0 = sol[:, 0:width]
    w_mat = sol[:, width:2 * width]

    assert len(out_box) == index, "the previous chunk must have replaced the state before it is read"
    state = state_box[0]
    from_state = _bdot_nt(jnp.concatenate([w_mat, r_in], axis=0), state)
    yield
    u = u0 + from_state[0:c]
    y = _bdot(a_rb, expand(u)) + y_from_v + from_state[c:2 * c]
    upd = _dot_tn(jnp.concatenate([u, v], axis=0).astype(BF16),
                  jnp.concatenate([b_end, k_end], axis=0).astype(BF16))
    same_head = _idiv(_iota((width, width), 0), HEAD_DIM) == _idiv(_iota((width, width), 1), HEAD_DIM)
    state_box[0] = state * jnp.exp(p_total) + jnp.where(same_head, upd, 0.0)
    out_box.append(y)


def _rwkv_plan(has_vres, *refs):
    if has_vres:
        (z_ref, zp_ref, vfirst_ref, mu_ref, w0_ref, wup_ref, a0_ref, aup_ref, gup_ref, kk_ref, ka_ref,
         rk_ref, lnw_ref, lnb_ref, v0_ref, vup_ref, o_ref, state_ref) = refs
    else:
        (z_ref, zp_ref, mu_ref, w0_ref, wup_ref, a0_ref, aup_ref, gup_ref, kk_ref, ka_ref,
         rk_ref, lnw_ref, lnb_ref, o_ref, vout_ref, state_ref) = refs
    step = pl.program_id(0)
    z = z_ref[...]
    last_prev = jnp.where(step > 0, zp_ref[7:8, :], 0.0)
    prev = jnp.where(_iota(z.shape, 0) == 0, last_prev, pltpu.roll(z, 1, axis=0))
    zr = z + (prev - z) * mu_ref[...]
    r = zr[:, 0:256]
    k = zr[:, 256:512]
    v = zr[:, 512:768]
    low = zr[:, 768:896]
    w_pre = w0_ref[...] + _dot_3pass(jnp.tanh(low), wup_ref[...])
    w_log = -(jnp.maximum(-w_pre, 0.0) + _softplus_neg_abs(w_pre)) - 0.5
    log_w = -jnp.exp(w_log)
    a = _sigmoid(a0_ref[...] + _dot_3pass(low, aup_ref[...]))
    g = _dot_3pass(_sigmoid(low), gup_ref[...])
    if has_vres:
        v = v + (vfirst_ref[...] - v) * _sigmoid(v0_ref[...] + _dot_3pass(low, vup_ref[...]))
    else:
        vout_ref[...] = v
    head_sum = _head_group_matrix(GROUP_WIDTH, HEAD_DIM, 1.0).astype(BF16)
    kk = k * kk_ref[...]
    kk = kk / jnp.maximum(jnp.sqrt(_dot_hilo(kk * kk, head_sum)), 1e-12)
    k = k * (1.0 + (a - 1.0) * ka_ref[...])
    a_vec = -kk
    b_vec = kk * a

    c = RWKV_CHUNK
    state_box = [state_ref[...]]
    chunks = []
    stages = []
    for n in range(z.shape[0] // c):
        rows = slice(n * c, (n + 1) * c)
        stages.append(_rwkv_chunk(n, r[rows], k[rows], v[rows], a_vec[rows], b_vec[rows], log_w[rows],
                                  state_box, chunks))

    def finish():
        state_ref[...] = state_box[0]
        y = jnp.concatenate(chunks, axis=0)
        head_mean = _head_group_matrix(GROUP_WIDTH, HEAD_DIM, 1.0 / HEAD_DIM).astype(BF16)
        mu_y = _dot_hilo(y, head_mean)
        d = y - mu_y
        var_y = _dot_hilo(d * d, head_mean)
        y = d * lax.rsqrt(var_y + RWKV_GN_EPS) * lnw_ref[...] + lnb_ref[...]
        bonus = _dot_hilo(r * k * rk_ref[...], head_sum) * v
        o_ref[...] = ((y + bonus) * g).astype(o_ref.dtype)

    return stages, RWKV_STAGGER, finish


def _rwkv_parts(z, mu, w0, w_up, a0, a_up, g_up, k_k, k_a, r_k, lnx_w, lnx_b, vres, step):
    t = z.shape[0]
    c = step
    row = lambda a: a.reshape(1, -1)
    low_rows = lambda a, start: jnp.zeros((LANES, GROUP_WIDTH), F32).at[start:start + a.shape[0]].set(a)
    has_vres = vres is not None
    mu_full = jnp.zeros((RWKV_W,), F32).at[:RWKV_COLS].set(mu)
    vec = _resident((1, GROUP_WIDTH))
    mat = _resident((LANES, GROUP_WIDTH))
    tile = pl.BlockSpec((c, GROUP_WIDTH), lambda i: (i, 0))
    z_specs = [pl.BlockSpec((c, RWKV_W), lambda i: (i, 0)),
               pl.BlockSpec((8, RWKV_W), lambda i: (jnp.maximum(i * (c // 8) - 1, 0), 0))]
    common = [row(w0), low_rows(w_up, 0), row(a0), low_rows(a_up, 16), low_rows(g_up, 32),
              row(k_k), row(k_a), row(r_k), row(lnx_w), row(lnx_b)]
    common_specs = [vec, mat, vec, mat, mat, vec, vec, vec, vec, vec]
    if has_vres:
        v_first, vres_mu, v0, v_up = vres
        mu_full = mu_full.at[RWKV_COLS:RWKV_COLS + RWKV_V_RANK].set(vres_mu)
        args = [z, z, v_first, row(mu_full)] + common + [row(v0), low_rows(v_up, 64)]
        in_specs = z_specs + [tile, _resident((1, RWKV_W))] + common_specs + [vec, mat]
        out_specs = [tile]
        out_shape = [jax.ShapeDtypeStruct((t, GROUP_WIDTH), MIX_DTYPE)]
    else:
        args = [z, z, row(mu_full)] + common
        in_specs = z_specs + [_resident((1, RWKV_W))] + common_specs
        out_specs = [tile, tile]
        out_shape = [jax.ShapeDtypeStruct((t, GROUP_WIDTH), MIX_DTYPE), jax.ShapeDtypeStruct((t, GROUP_WIDTH), F32)]
    return _MixerParts(plan=functools.partial(_rwkv_plan, has_vres), args=args, in_specs=in_specs,
                       out_specs=out_specs, out_shape=out_shape,
                       scratch=[pltpu.VMEM((GROUP_WIDTH, GROUP_WIDTH), F32)])


def _store_token_tiles(ref, index, x):
    n = x.shape[0]
    for j in range(SUBLANES):
        ref[(*index, pl.ds(j, n, stride=SUBLANES), slice(None))] = x[:, j * LANES:(j + 1) * LANES]


def _load_token_tiles(ref, index, n):
    return jnp.concatenate([ref[(*index, pl.ds(j, n, stride=SUBLANES), slice(None))] for j in range(SUBLANES)],
                           axis=-1)


def _top2_route(logits):
    row = _iota(logits.shape, 0).astype(F32)
    m1 = jnp.max(logits, axis=0, keepdims=True)
    i1 = jnp.min(jnp.where(logits == m1, row, N_EXPERTS), axis=0, keepdims=True)
    rest = jnp.where(row == i1, -jnp.inf, logits)
    m2 = jnp.max(rest, axis=0, keepdims=True)
    i2 = jnp.min(jnp.where(rest == m2, row, N_EXPERTS), axis=0, keepdims=True)
    e2 = jnp.exp(m2 - m1)
    g1 = 1.0 / (1.0 + e2)
    g2 = e2 * g1
    return jnp.where(row == 0, i1, jnp.where(row == 1, i2, jnp.where(row == 2, g1, jnp.where(row == 3, g2, 0.0))))


def _mix_residual_norm(alpha, x_ref, mix_refs, w_ref, g_ref, b_ref):
    acc = alpha * x_ref[...]
    for h, ref in enumerate(mix_refs):
        acc = acc + _dot(ref[...], w_ref[h * GROUP_WIDTH:(h + 1) * GROUP_WIDTH, :])
    return _layer_norm(acc, g_ref[...], b_ref[...])


def _outproj_route_kernel(alpha, x_ref, o0_ref, o1_ref, o2_ref, o3_ref, w_ref, g_ref, b_ref, router_ref,
                          tiles_ref, route_ref):
    y = _mix_residual_norm(alpha, x_ref, (o0_ref, o1_ref, o2_ref, o3_ref), w_ref, g_ref, b_ref)
    route_ref[...] = _top2_route(_dot_3pass_nt(router_ref[...], y))
    _store_token_tiles(tiles_ref, (), y)


def _out_proj_ln_route(alpha, x, mixes, w_out, g, b, router):
    t = x.shape[0]
    row_d = pl.BlockSpec((ROW_TILE, D_MODEL), lambda i: (i, 0))
    row_g = pl.BlockSpec((ROW_TILE, GROUP_WIDTH), lambda i: (i, 0))
    return pl.pallas_call(
        functools.partial(_outproj_route_kernel, alpha),
        grid=(t // ROW_TILE,),
        in_specs=[row_d, row_g, row_g, row_g, row_g, _resident((D_MODEL, D_MODEL)),
                  _resident((1, D_MODEL)), _resident((1, D_MODEL)), _resident((N_EXPERTS, D_MODEL))],
        out_specs=[pl.BlockSpec((ROW_TILE * SUBLANES, LANES), lambda i: (i, 0)),
                   pl.BlockSpec((N_EXPERTS, ROW_TILE), lambda i: (0, i))],
        out_shape=[jax.ShapeDtypeStruct((t * SUBLANES, LANES), F32), jax.ShapeDtypeStruct((N_EXPERTS, t), F32)],
        compiler_params=_params(("parallel",)),
        name="out_proj_ln_route",
    )(x, *mixes, w_out.astype(BF16), g.reshape(1, -1), b.reshape(1, -1), router.T)


def _ln_embed(y, ln_g, ln_b, p, ple_gate, ple_proj):
    x = _layer_norm(y, ln_g, ln_b)
    gate = _sigmoid(_dot(x.astype(BF16), ple_gate))
    return x + gate * _dot(p.astype(BF16), ple_proj)


def _dense_ffn_kernel(alpha, x_ref, o0_ref, o1_ref, o2_ref, o3_ref, wo_ref, g1_ref, b1_ref, p_ref, wg_ref, wu_ref,
                      wd_ref, g_ref, b_ref, pg_ref, pp_ref, y_ref, acc_ref):
    x = _mix_residual_norm(alpha, x_ref, (o0_ref, o1_ref, o2_ref, o3_ref), wo_ref, g1_ref, b1_ref)
    xb = x.astype(BF16)
    acc_ref[...] = alpha * x
    for j in range(D_FF // FF_CHUNK):
        cols = slice(j * FF_CHUNK, (j + 1) * FF_CHUNK)
        h = _silu(_dot(xb, wg_ref[:, cols])) * _dot(xb, wu_ref[:, cols])
        acc_ref[...] += _dot(h.astype(BF16), wd_ref[cols, :])
    y_ref[...] = _ln_embed(acc_ref[...], g_ref[...], b_ref[...], p_ref[...], pg_ref[...], pp_ref[...])


def _dense_layer_tail(alpha, x, mixes, w_out, g1, b1, p_all, p_block, w_gate, w_up, w_down, g, b, ple_gate, ple_proj):
    t = x.shape[0]
    row_d = pl.BlockSpec((ROW_TILE, D_MODEL), lambda i: (i, 0))
    row_g = pl.BlockSpec((ROW_TILE, GROUP_WIDTH), lambda i: (i, 0))
    return pl.pallas_call(
        functools.partial(_dense_ffn_kernel, alpha),
        grid=(t // ROW_TILE,),
        in_specs=[row_d, row_g, row_g, row_g, row_g, _resident((D_MODEL, D_MODEL)),
                  _resident((1, D_MODEL)), _resident((1, D_MODEL)),
                  pl.BlockSpec((ROW_TILE, PLE_DIM), lambda i: (p_block + i, 0)),
                  _resident((D_MODEL, D_FF)), _resident((D_MODEL, D_FF)), _resident((D_FF, D_MODEL)),
                  _resident((1, D_MODEL)), _resident((1, D_MODEL)),
                  _resident((D_MODEL, D_MODEL)), _resident((PLE_DIM, D_MODEL))],
        out_specs=row_d,
        out_shape=jax.ShapeDtypeStruct((t, D_MODEL), F32),
        scratch_shapes=[pltpu.VMEM((ROW_TILE, D_MODEL), F32)],
        compiler_params=_params(("parallel",)),
        name="dense_ffn_tail",
    )(x, *mixes, w_out.astype(BF16), g1.reshape(1, -1), b1.reshape(1, -1), p_all,
      w_gate.astype(BF16), w_up.astype(BF16), w_down.astype(BF16), g.reshape(1, -1), b.reshape(1, -1),
      ple_gate.astype(BF16), ple_proj.astype(BF16))


def _weight_group_copies(e, group, wg_hbm, wu_hbm, wd_hbm, stage_cols_ref, stage_rows_ref, sem):
    slot = group % 2
    lo, hi = group * MOE_WEIGHT_CHUNK, (group + 1) * MOE_WEIGHT_CHUNK
    return (pltpu.make_async_copy(wg_hbm.at[e, :, lo:hi], stage_cols_ref.at[slot, 0], sem.at[slot]),
            pltpu.make_async_copy(wu_hbm.at[e, :, lo:hi], stage_cols_ref.at[slot, 1], sem.at[slot]),
            pltpu.make_async_copy(wd_hbm.at[e, lo:hi, :], stage_rows_ref.at[slot], sem.at[slot]))


def _expert_kernel(row_tok_ref, row_dst_ref, block_e_ref, n_used_ref, x_hbm, wg_hbm, wu_hbm, wd_hbm, y_hbm,
                   rows_ref, xb_ref, acc_ref, ybuf_ref, wg_ref, wu_ref, wd_ref, stage_cols_ref, stage_rows_ref,
                   gather_sem, scatter_sem, weight_sem):
    i = pl.program_id(0)
    n_used = n_used_ref[0]
    last = pl.num_programs(0) - 1
    n = MOE_ROWS
    slot = lax.rem(i, 2)
    other = 1 - slot
    tile = lambda first_row: pl.ds(pl.multiple_of(first_row, SUBLANES), SUBLANES)
    gather_row = lambda tok_row, s, r: pltpu.make_async_copy(
        x_hbm.at[tile(tok_row)], rows_ref.at[s, tile(r * SUBLANES)], gather_sem.at[s])
    scatter_row = lambda dst_row, s, r: pltpu.make_async_copy(
        ybuf_ref.at[s, tile(r * SUBLANES)], y_hbm.at[tile(dst_row)], scatter_sem.at[s])
    block_rows = n * SUBLANES
    gather_block = lambda s: pltpu.make_async_copy(x_hbm.at[pl.ds(0, block_rows)], rows_ref.at[s], gather_sem.at[s])
    scatter_block = lambda s: pltpu.make_async_copy(ybuf_ref.at[s], y_hbm.at[pl.ds(0, block_rows)],
                                                    scatter_sem.at[s])

    @pl.when(i == 0)
    def _():
        ybuf_ref[1] = jnp.zeros((block_rows, LANES), F32)

        def start(r, carry):
            gather_row(row_tok_ref[r], 0, r).start()
            return carry

        lax.fori_loop(0, n, start, 0)

    def scatter_all(first_dst, s):
        def start(r, carry):
            scatter_row(row_dst_ref[first_dst + r], s, r).start()
            return carry

        lax.fori_loop(0, n, start, 0)
        scatter_block(s).wait()

    expert = block_e_ref[i]
    new_expert = (i == 0) | (expert != block_e_ref[jnp.maximum(i - 1, 0)])

    weight_group = lambda g: _weight_group_copies(expert, g, wg_hbm, wu_hbm, wd_hbm, stage_cols_ref,
                                                  stage_rows_ref, weight_sem)
    chunks_per_group = MOE_WEIGHT_CHUNK // FF_CHUNK
    n_groups = D_FF_EXPERT // MOE_WEIGHT_CHUNK

    def take_weight_group(g):
        if g + 1 < n_groups:
            for copy in weight_group(g + 1):
                copy.start()
        for copy in weight_group(g):
            copy.wait()
        piece = slice(g * MOE_WEIGHT_CHUNK, (g + 1) * MOE_WEIGHT_CHUNK)
        wg_ref[:, piece] = stage_cols_ref[g % 2, 0].astype(BF16)
        wu_ref[:, piece] = stage_cols_ref[g % 2, 1].astype(BF16)
        wd_ref[piece, :] = stage_rows_ref[g % 2].astype(BF16)

    @pl.when(i < n_used)
    def _():
        @pl.when(new_expert)
        def _():
            for copy in weight_group(0):
                copy.start()

        gather_block(slot).wait()
        xb_ref[...] = _load_token_tiles(rows_ref, (slot,), n).astype(BF16)
        n_chunks = D_FF_EXPERT // FF_CHUNK
        rows_per_chunk = -(-n // (n_chunks - MOE_DMA_FREE_CHUNKS))
        for j in range(n_chunks):
            if j % chunks_per_group == 0:
                pl.when(new_expert)(functools.partial(take_weight_group, j // chunks_per_group))
            cols = slice(j * FF_CHUNK, (j + 1) * FF_CHUNK)
            xb = xb_ref[...]
            h = _silu(_dot(xb, wg_ref[:, cols])) * _dot(xb, wu_ref[:, cols])
            part = _dot(h.astype(BF16), wd_ref[cols, :])
            if j == 0:
                acc_ref[...] = part
            else:
                acc_ref[...] += part
            for r in range(j * rows_per_chunk, min((j + 1) * rows_per_chunk, n)):
                gather_row(row_tok_ref[(i + 1) * n + r], other, r).start(priority=GATHER_DMA_PRIORITY)
                scatter_row(row_dst_ref[i * n + r], other, r).start(priority=SCATTER_DMA_PRIORITY)
        _store_token_tiles(ybuf_ref, (slot,), acc_ref[...])
        scatter_block(other).wait()

    @pl.when(i == n_used)
    def _():
        gather_block(slot).wait()
        scatter_all(i * n, other)

    @pl.when(i >= n_used)
    def _():
        rows_ref[slot] = jnp.zeros((block_rows, LANES), F32)
        fill = pltpu.make_async_copy(rows_ref.at[slot], y_hbm.at[pl.ds(i * block_rows, block_rows)],
                                     scatter_sem.at[slot])
        fill.start()
        fill.wait()

    @pl.when((i == last) & (i < n_used))
    def _():
        gather_block(other).wait()
        scatter_all((i + 1) * n, slot)


def _expert_rows(x, row_tok, row_dst, block_e, n_used, n_out_rows, w_gate, w_up, w_down):
    n_blocks = block_e.shape[0]
    hbm = pl.BlockSpec(memory_space=pl.ANY)
    grid_spec = pltpu.PrefetchScalarGridSpec(
        num_scalar_prefetch=4,
        grid=(n_blocks,),
        in_specs=[hbm, hbm, hbm, hbm],
        out_specs=hbm,
        scratch_shapes=[pltpu.VMEM((2, MOE_ROWS * SUBLANES, LANES), F32), pltpu.VMEM((MOE_ROWS, D_MODEL), BF16),
                        pltpu.VMEM((MOE_ROWS, D_MODEL), F32), pltpu.VMEM((2, MOE_ROWS * SUBLANES, LANES), F32),
                        pltpu.VMEM((D_MODEL, D_FF_EXPERT), BF16), pltpu.VMEM((D_MODEL, D_FF_EXPERT), BF16),
                        pltpu.VMEM((D_FF_EXPERT, D_MODEL), BF16),
                        pltpu.VMEM((2, 2, D_MODEL, MOE_WEIGHT_CHUNK), F32),
                        pltpu.VMEM((2, MOE_WEIGHT_CHUNK, D_MODEL), F32),
                        pltpu.SemaphoreType.DMA((2,)), pltpu.SemaphoreType.DMA((2,)),
                        pltpu.SemaphoreType.DMA((2,))],
    )
    return pl.pallas_call(
        _expert_kernel,
        grid_spec=grid_spec,
        out_shape=jax.ShapeDtypeStruct((n_out_rows * SUBLANES, LANES), F32),
        compiler_params=_params(("arbitrary",)),
        name="moe_experts",
    )(row_tok, row_dst, block_e, n_used.reshape(1), x, w_gate, w_up, w_down)


def _combine_kernel(alpha, x_ref, y0_ref, y1_ref, gates_ref, p_ref, g_ref, b_ref, pg_ref, pp_ref, o_ref):
    n = ROW_TILE
    gates = gates_ref[...]
    f = _load_token_tiles(y0_ref, (), n) * gates[:, 0:1] + _load_token_tiles(y1_ref, (), n) * gates[:, 1:2]
    o_ref[...] = _ln_embed(alpha * _load_token_tiles(x_ref, (), n) + f, g_ref[...], b_ref[...], p_ref[...],
                           pg_ref[...], pp_ref[...])


def _moe_combine_tail(alpha, x_tiles, p_all, p_block, y_tiles, gates, g, b, ple_gate, ple_proj):
    t = x_tiles.shape[0] // SUBLANES
    n = ROW_TILE
    row_d = pl.BlockSpec((n, D_MODEL), lambda i: (i, 0))
    tiles_d = pl.BlockSpec((n * SUBLANES, LANES), lambda i: (i, 0))
    return pl.pallas_call(
        functools.partial(_combine_kernel, alpha),
        grid=(t // n,),
        in_specs=[tiles_d, tiles_d, pl.BlockSpec((n * SUBLANES, LANES), lambda i: (t // n + i, 0)),
                  pl.BlockSpec((n, 2), lambda i: (i, 0)), pl.BlockSpec((n, PLE_DIM), lambda i: (p_block + i, 0)),
                  _resident((1, D_MODEL)), _resident((1, D_MODEL)),
                  _resident((D_MODEL, D_MODEL)), _resident((PLE_DIM, D_MODEL))],
        out_specs=row_d,
        out_shape=jax.ShapeDtypeStruct((t, D_MODEL), F32),
        compiler_params=_params(("parallel",)),
        name="moe_combine_tail",
    )(x_tiles, y_tiles, y_tiles, gates, p_all, g.reshape(1, -1), b.reshape(1, -1), ple_gate.astype(BF16),
      ple_proj.astype(BF16))


def _moe_tail(alpha, x_tiles, route, p_all, p_block, w_gate, w_up, w_down, g, b, ple_gate, ple_proj):
    t = route.shape[1]
    experts = route[0:2].T.astype(jnp.int32)
    gates = route[2:4].T
    e_flat = experts.reshape(-1)
    onehot = (e_flat[:, None] == jnp.arange(N_EXPERTS, dtype=jnp.int32)[None, :]).astype(jnp.int32)
    counts = jnp.sum(onehot, axis=0)
    padded = (counts + MOE_ROWS - 1) // MOE_ROWS * MOE_ROWS
    pad_end = jnp.cumsum(padded)
    pad_start = pad_end - padded
    n_blocks = (2 * t) // MOE_ROWS + N_EXPERTS
    n_rows = n_blocks * MOE_ROWS
    block_start = jnp.arange(n_blocks, dtype=jnp.int32) * MOE_ROWS
    block_e = jnp.minimum(jnp.sum((block_start[:, None] >= pad_end[None, :]).astype(jnp.int32), axis=1),
                          N_EXPERTS - 1)
    order = jnp.argsort(e_flat, stable=True).astype(jnp.int32)
    seg_end = jnp.cumsum(counts)
    seg_start = seg_end - counts
    per_row = lambda per_expert: jnp.repeat(per_expert[block_e], MOE_ROWS)
    place = jnp.arange(n_rows, dtype=jnp.int32) - per_row(pad_start - seg_start)
    used = place < per_row(seg_end)
    row_assign = jnp.where(used, order[jnp.clip(place, 0, 2 * t - 1)], -1)
    spare = 2 * t + jnp.cumsum(jnp.where(used, 0, 1).astype(jnp.int32)) - 1
    row_tok = jnp.where(used, row_assign // 2, 0)
    row_dst = jnp.where(used, (row_assign % 2) * t + row_assign // 2, spare)
    n_spare = n_rows - 2 * t
    first_dst = 2 * t + n_spare + jnp.arange(MOE_ROWS, dtype=jnp.int32)
    row_tok = jnp.concatenate([row_tok, jnp.zeros((MOE_ROWS,), jnp.int32)])
    row_dst = jnp.concatenate([first_dst, row_dst])
    n_used = (pad_end[-1] // MOE_ROWS).astype(jnp.int32)
    y_tiles = _expert_rows(x_tiles, row_tok * SUBLANES, row_dst * SUBLANES, block_e, n_used,
                           2 * t + n_spare + MOE_ROWS, w_gate, w_up, w_down)
    return _moe_combine_tail(alpha, x_tiles, p_all, p_block, y_tiles, gates, g, b, ple_gate, ple_proj)


def kernel(x, p, w_in, w_out, gla_gk_up, gla_gk_bias, gla_norm_w, hgrn_lower_bounds, hgrn_norm_w, swa_sinks, rwkv_mu, rwkv_w0, rwkv_w_up, rwkv_a0, rwkv_a_up, rwkv_g_up, rwkv_k_k, rwkv_k_a, rwkv_r_k, rwkv_lnx_w, rwkv_lnx_b, rwkv_vres_down, rwkv_vres_mu, rwkv_v0, rwkv_vres_up, ln1_g, ln1_b, ln2_g, ln2_b, ffn_w_gate, ffn_w_up, ffn_w_down, moe_router, moe_w_gate, moe_w_up, moe_w_down, ple_proj, ple_gate):
    bsz, seq, d = x.shape
    depth = w_in.shape[0]
    alpha = (2.0 * depth) ** 0.25
    lbs = jnp.cumsum(jax.nn.softmax(hgrn_lower_bounds.astype(F32), axis=0), axis=0)
    lbs = lbs - lbs[0]
    p_all = p.reshape(-1, PLE_DIM)
    outs = []
    for bi in range(bsz):
        xt = x[bi]
        v_first = None
        for i in range(depth):
            w = _group_in_weights(w_in[i], None if i == 0 else rwkv_vres_down[i - 1])
            z_gla, z_hgrn, z_swa, z_rwkv = _in_proj(xt, w)
            vres = None if i == 0 else (v_first, rwkv_vres_mu[i - 1], rwkv_v0[i - 1], rwkv_vres_up[i - 1])
            mixed = _call_mixers(
                [_gla_parts(z_gla, gla_gk_up[i], gla_gk_bias[i], gla_norm_w[i], MIX_STEP),
                 _hgrn_parts(z_hgrn, lbs[i], hgrn_norm_w[i], MIX_STEP),
                 _swa_parts(z_swa, swa_sinks[i], MIX_STEP),
                 _rwkv_parts(z_rwkv, rwkv_mu[i], rwkv_w0[i], rwkv_w_up[i], rwkv_a0[i], rwkv_a_up[i],
                             rwkv_g_up[i], rwkv_k_k[i], rwkv_k_a[i], rwkv_r_k[i].reshape(-1),
                             rwkv_lnx_w[i], rwkv_lnx_b[i], vres, MIX_STEP)],
                seq, MIX_STEP)
            mixes = tuple(mixed[0:4])
            if i == 0:
                v_first = mixed[4]
            j = i // 2
            p_block = (i * bsz + bi) * (seq // ROW_TILE)
            if i % 2 == 0:
                xt = _dense_layer_tail(alpha, xt, mixes, w_out[i], ln1_g[i], ln1_b[i], p_all, p_block,
                                       ffn_w_gate[j], ffn_w_up[j], ffn_w_down[j], ln2_g[i], ln2_b[i],
                                       ple_gate[i], ple_proj[i])
            else:
                x_tiles, route = _out_proj_ln_route(alpha, xt, mixes, w_out[i], ln1_g[i], ln1_b[i], moe_router[j])
                xt = _moe_tail(alpha, x_tiles, route, p_all, p_block, moe_w_gate[j], moe_w_up[j], moe_w_down[j],
                               ln2_g[i], ln2_b[i], ple_gate[i], ple_proj[i])
        outs.append(xt)
    return jnp.stack(outs, axis=0)
```

```python
import functools
from typing import Callable, NamedTuple

import jax
import jax.numpy as jnp
from jax import lax
from jax.experimental import pallas as pl
from jax.experimental.pallas import tpu as pltpu

F32 = jnp.float32
BF16 = jnp.bfloat16
MIX_DTYPE = BF16

D_MODEL = 1024
GROUP_WIDTH = 256
N_HEADS = 4
HEAD_DIM = 64
GLA_DK = 32
GLA_GATE_RANK = 16
GLA_GATE_NORMALIZER = 16.0
SWA_WINDOW = 128
RWKV_COLS = 3 * GROUP_WIDTH + 16 + 16 + 32
RWKV_V_RANK = 8
LN_EPS = 1e-5
RMS_EPS = 1e-6
RWKV_GN_EPS = 64e-5
D_FF = 2816
N_EXPERTS = 8
D_FF_EXPERT = 3584
PLE_DIM = 256

LANES = 128
SUBLANES = 8
GLA_W = 896
HGRN_W = 1024
SWA_W = 512
RWKV_W = 896
Z_W = GLA_W + HGRN_W + SWA_W + RWKV_W

GLA_SUB = 16
GLA_TILE = 128
GLA_STAGGER = 5
SWA_STAGGER = 5
MIX_STEP = 1024
RWKV_CHUNK = 64
RWKV_STAGGER = 2
ROW_TILE = 512
WIDE_ROW_TILE = 1024
FF_CHUNK = 256
MOE_ROWS = 512
MOE_WEIGHT_CHUNK = 512
MOE_DMA_FREE_CHUNKS = 10
GATHER_DMA_PRIORITY = 0
SCATTER_DMA_PRIORITY = 1
VMEM_LIMIT = 56 * 1024 * 1024


def _iota(shape, dim):
    return lax.broadcasted_iota(jnp.int32, shape, dim)


def _idiv(x, n):
    return jnp.right_shift(x, n.bit_length() - 1)


def _imod(x, n):
    return jnp.bitwise_and(x, n - 1)


def _dot(a, b):
    return jnp.dot(a, b, preferred_element_type=F32)


def _dot_nt(a, b):
    return lax.dot_general(a, b, (((1,), (1,)), ((), ())), preferred_element_type=F32)


def _dot_tn(a, b):
    return lax.dot_general(a, b, (((0,), (0,)), ((), ())), preferred_element_type=F32)


def _bdot(a, b):
    return _dot(a.astype(BF16), b.astype(BF16))


def _bdot_nt(a, b):
    return _dot_nt(a.astype(BF16), b.astype(BF16))


def _dot_hilo(x, m):
    hi = x.astype(BF16)
    lo = (x - hi.astype(F32)).astype(BF16)
    return _dot(hi, m) + _dot(lo, m)


def _dot_mask(m, x):
    m = m.astype(BF16)
    x1 = x.astype(BF16)
    r1 = x - x1.astype(F32)
    x2 = r1.astype(BF16)
    x3 = (r1 - x2.astype(F32)).astype(BF16)
    return _dot(m, x1) + _dot(m, x2) + _dot(m, x3)


def _dot_3pass_nt(x, w):
    x_hi = x.astype(BF16)
    x_lo = (x - x_hi.astype(F32)).astype(BF16)
    w_hi = w.astype(BF16)
    w_lo = (w - w_hi.astype(F32)).astype(BF16)
    return _dot_nt(x_hi, w_hi) + _dot_nt(x_lo, w_hi) + _dot_nt(x_hi, w_lo)


def _dot_3pass(x, w):
    x_hi = x.astype(BF16)
    x_lo = (x - x_hi.astype(F32)).astype(BF16)
    w_hi = w.astype(BF16)
    w_lo = (w - w_hi.astype(F32)).astype(BF16)
    return _dot(x_hi, w_hi) + _dot(x_lo, w_hi) + _dot(x_hi, w_lo)


def _sigmoid(x):
    return 1.0 / (1.0 + jnp.exp(-x))


def _silu(x):
    return x * _sigmoid(x)


def _softplus_neg_abs(x):
    return jnp.log(1.0 + jnp.exp(-jnp.abs(x)))


def _log_sigmoid(x):
    return jnp.minimum(x, 0.0) - _softplus_neg_abs(x)


def _layer_norm(y, g, b):
    mu = jnp.mean(y, axis=-1, keepdims=True)
    d = y - mu
    var = jnp.mean(d * d, axis=-1, keepdims=True)
    return d * lax.rsqrt(var + LN_EPS) * g + b


def _expand_heads(x, head_width):
    lane_head = _idiv(_iota(x.shape, 1), head_width)
    return jnp.concatenate([jnp.where(lane_head == h, x, 0.0) for h in range(N_HEADS)], axis=0)


def _head_group_matrix(width, head_width, value):
    same = _idiv(_iota((width, width), 0), head_width) == _idiv(_iota((width, width), 1), head_width)
    return jnp.where(same, value, 0.0).astype(F32)


def _resident(shape):
    nd = len(shape)
    return pl.BlockSpec(shape, lambda *_: (0,) * nd, pipeline_mode=pl.Buffered(1))


def _params(semantics):
    return pltpu.CompilerParams(dimension_semantics=semantics, vmem_limit_bytes=VMEM_LIMIT)


def _run_plans(plans):
    live = [dict(enumerate(stages)) for stages, _, _ in plans]
    rnd = 0
    while any(live):
        for group, (_, stagger, _) in zip(live, plans):
            for n in sorted(group):
                if rnd >= n * stagger:
                    try:
                        next(group[n])
                    except StopIteration:
                        del group[n]
        rnd += 1
    for _, _, finish in plans:
        finish()


def _inproj_kernel(x_ref, w_ref, gla_ref, hgrn_ref, swa_ref, rwkv_ref):
    xb = x_ref[...].astype(BF16)
    o = 0
    for ref, width in ((gla_ref, GLA_W), (hgrn_ref, HGRN_W), (swa_ref, SWA_W), (rwkv_ref, RWKV_W)):
        ref[...] = _dot(xb, w_ref[:, o:o + width])
        o += width


def _in_proj(x, w):
    t = x.shape[0]
    widths = (GLA_W, HGRN_W, SWA_W, RWKV_W)
    return pl.pallas_call(
        _inproj_kernel,
        grid=(t // WIDE_ROW_TILE,),
        in_specs=[pl.BlockSpec((WIDE_ROW_TILE, D_MODEL), lambda i: (i, 0)), _resident((D_MODEL, Z_W))],
        out_specs=[pl.BlockSpec((WIDE_ROW_TILE, w_), lambda i: (i, 0)) for w_ in widths],
        out_shape=[jax.ShapeDtypeStruct((t, w_), F32) for w_ in widths],
        compiler_params=_params(("parallel",)),
        name="in_proj",
    )(x, w)


def _group_in_weights(w_in, vres_down):
    gla, hgrn, swa, rwkv = jnp.split(w_in, (784, 784 + 1024, 784 + 1024 + 512), axis=1)
    if vres_down is not None:
        rwkv = jnp.concatenate([rwkv, vres_down], axis=1)
    pad = lambda a, w_: jnp.pad(a, ((0, 0), (0, w_ - a.shape[1])))
    return jnp.concatenate([pad(gla, GLA_W), hgrn, swa, pad(rwkv, RWKV_W)], axis=1).astype(BF16)


class _MixerParts(NamedTuple):
    plan: Callable
    args: list
    in_specs: list
    out_specs: list
    out_shape: list
    scratch: list


def _mixers_kernel(layout, *refs):
    n_in = sum(entry[1] for entry in layout)
    n_out = sum(entry[2] for entry in layout)
    ins, outs, scratch = list(refs[:n_in]), list(refs[n_in:n_in + n_out]), list(refs[n_in + n_out:])

    @pl.when(pl.program_id(0) == 0)
    def _():
        for ref in scratch:
            ref[...] = jnp.zeros_like(ref)

    plans = []
    for plan, n_i, n_o, n_s in layout:
        plans.append(plan(*ins[:n_i], *outs[:n_o], *scratch[:n_s]))
        del ins[:n_i], outs[:n_o], scratch[:n_s]
    _run_plans(plans)


def _call_mixers(parts_list, tokens, step):
    layout = tuple((p.plan, len(p.args), len(p.out_shape), len(p.scratch)) for p in parts_list)
    flat = lambda field: [item for p in parts_list for item in getattr(p, field)]
    return pl.pallas_call(
        functools.partial(_mixers_kernel, layout),
        grid=(tokens // step,),
        in_specs=flat("in_specs"),
        out_specs=flat("out_specs"),
        out_shape=flat("out_shape"),
        scratch_shapes=flat("scratch"),
        compiler_params=_params(("arbitrary",)),
        name="mixers",
    )(*flat("args"))


def _mix_out(tokens, step):
    return (pl.BlockSpec((step, GROUP_WIDTH), lambda i: (i, 0)),
            jax.ShapeDtypeStruct((tokens, GROUP_WIDTH), MIX_DTYPE))


def _gated_linear_attention_tile(index, q, k, v, log_f, state_box, out_box):
    length, kw = q.shape
    head_k = kw // N_HEADS
    n_sub = length // GLA_SUB
    row = _iota((length, length), 0)
    col = _iota((length, length), 1)
    same_sub = _idiv(row, GLA_SUB) == _idiv(col, GLA_SUB)
    m_local = jnp.where(same_sub & (col <= row), 1.0, 0.0).astype(F32)
    m_prev = jnp.where(_idiv(col, GLA_SUB) < _idiv(row, GLA_SUB), 1.0, 0.0).astype(F32)
    sums = _dot_mask(jnp.concatenate([m_local, m_prev], axis=0), log_f)
    yield
    b_local = sums[0:length]
    b_start = sums[length:2 * length]
    b_full = b_start + b_local
    q_local = q * jnp.exp(b_local)

    q_pos = _imod(_iota((N_HEADS * GLA_SUB, length), 0), GLA_SUB)
    s_pos = _iota((N_HEADS * GLA_SUB, length), 1)
    probs = []
    for c in range(n_sub):
        r0, r1 = c * GLA_SUB, (c + 1) * GLA_SUB
        expo = jnp.where(_iota((r1, kw), 0) < r0, b_start[r0:r0 + 1, :] - b_full[0:r1], -b_local[0:r1])
        k_ref = k[0:r1] * jnp.exp(expo)
        if r1 < length:
            k_ref = jnp.concatenate([k_ref, jnp.zeros((length - r1, kw), F32)], axis=0)
        q_heads = _expand_heads(q_local[r0:r1, :], head_k)
        s = _bdot_nt(q_heads, k_ref)
        probs.append(jnp.where(s_pos <= q_pos + r0, s, 0.0))
    b_total = b_full[length - 1:length, :]
    k_end = k * jnp.exp(b_total - b_full)
    upd = _dot_tn(v.astype(BF16), k_end.astype(BF16))
    yield
    o_heads = _bdot(jnp.concatenate(probs, axis=0), v)
    assert len(out_box) == index, "the previous tile must have replaced the state before it is read"
    state_t = state_box[0]
    o_state = _bdot_nt(q * jnp.exp(b_full), state_t)
    same_head = _idiv(_iota((GROUP_WIDTH, kw), 0), HEAD_DIM) == _idiv(_iota((GROUP_WIDTH, kw), 1), head_k)
    state_box[0] = state_t * jnp.exp(b_total) + jnp.where(same_head, upd, 0.0)
    yield
    v_head = _idiv(_iota((GLA_SUB, GROUP_WIDTH), 1), HEAD_DIM)
    rows = []
    for c in range(n_sub):
        base = c * N_HEADS * GLA_SUB
        acc = jnp.zeros((GLA_SUB, GROUP_WIDTH), F32)
        for h in range(N_HEADS):
            acc = acc + jnp.where(v_head == h, o_heads[base + h * GLA_SUB:base + (h + 1) * GLA_SUB, :], 0.0)
        rows.append(acc)
    out_box.append(jnp.concatenate(rows, axis=0) + o_state)


def _gated_linear_attention_plan(q, k, v, log_f, gate, norm_w, o_ref, state_ref):
    state_box = [state_ref[...]]
    tiles = []
    stages = []
    for n in range(q.shape[0] // GLA_TILE):
        rows = slice(n * GLA_TILE, (n + 1) * GLA_TILE)
        stages.append(_gated_linear_attention_tile(n, q[rows], k[rows], v[rows], log_f[rows], state_box, tiles))

    def finish():
        state_ref[...] = state_box[0]
        o = jnp.concatenate(tiles, axis=0)
        ms = _dot_hilo(o * o, _head_group_matrix(GROUP_WIDTH, HEAD_DIM, 1.0 / HEAD_DIM).astype(BF16))
        o_ref[...] = (o * lax.rsqrt(ms + RMS_EPS) * norm_w * _silu(gate)).astype(o_ref.dtype)

    return stages, GLA_STAGGER, finish


def _gla_plan(z_ref, gk_up_ref, gk_bias_ref, norm_w_ref, o_ref, state_ref):
    z = z_ref[...]
    q = z[:, 0:128] * (GLA_DK ** -0.5)
    k = z[:, 128:256]
    v = z[:, 256:512]
    g = z[:, 512:768]
    gate_in = _dot_3pass(z[:, 768:896], gk_up_ref[...]) + gk_bias_ref[...]
    log_f = _log_sigmoid(gate_in) * (1.0 / GLA_GATE_NORMALIZER)
    return _gated_linear_attention_plan(q, k, v, log_f, g, norm_w_ref[...], o_ref, state_ref)


def _hgrn_plan(z_ref, lb_ref, log_lb_ref, norm_w_ref, o_ref, state_ref):
    z = z_ref[...]
    q = _silu(z[:, 0:256])
    f = z[:, 256:512]
    v = z[:, 512:768]
    g = z[:, 768:1024]
    lb = lb_ref[...]
    a = log_lb_ref[...]
    e = jnp.exp(-jnp.abs(f))
    inv = 1.0 / (1.0 + e)
    c = jnp.log1p(-lb) + jnp.minimum(f, 0.0) - jnp.log(1.0 + e)
    log_f = jnp.maximum(a, c) + _softplus_neg_abs(a - c)
    k = (1.0 - lb) * jnp.where(f >= 0.0, e * inv, inv)
    return _gated_linear_attention_plan(q, k, v, log_f, g, norm_w_ref[...], o_ref, state_ref)


def _gla_parts(z, gk_up, gk_bias, norm_w, step):
    out_spec, out_shape = _mix_out(z.shape[0], step)
    gk_up_pad = jnp.zeros((LANES, N_HEADS * GLA_DK), F32).at[:GLA_GATE_RANK].set(gk_up)
    return _MixerParts(
        plan=_gla_plan,
        args=[z, gk_up_pad, gk_bias.reshape(1, -1), jnp.tile(norm_w, N_HEADS).reshape(1, -1)],
        in_specs=[pl.BlockSpec((step, GLA_W), lambda i: (i, 0)),
                  _resident((LANES, N_HEADS * GLA_DK)), _resident((1, N_HEADS * GLA_DK)),
                  _resident((1, GROUP_WIDTH))],
        out_specs=[out_spec], out_shape=[out_shape],
        scratch=[pltpu.VMEM((GROUP_WIDTH, N_HEADS * GLA_DK), F32)])


def _hgrn_parts(z, lb, norm_w, step):
    out_spec, out_shape = _mix_out(z.shape[0], step)
    return _MixerParts(
        plan=_hgrn_plan,
        args=[z, lb.reshape(1, -1), jnp.log(lb).reshape(1, -1), jnp.tile(norm_w, N_HEADS).reshape(1, -1)],
        in_specs=[pl.BlockSpec((step, HGRN_W), lambda i: (i, 0)),
                  _resident((1, GROUP_WIDTH)), _resident((1, GROUP_WIDTH)), _resident((1, GROUP_WIDTH))],
        out_specs=[out_spec], out_shape=[out_shape],
        scratch=[pltpu.VMEM((GROUP_WIDTH, GROUP_WIDTH), F32)])


def _swa_block(q, kw, vw, visible, sinks, out_box):
    n = q.shape[0]
    head_cols = lambda x, h: x[:, h * HEAD_DIM:(h + 1) * HEAD_DIM]
    v_lane_head = _idiv(_iota(vw.shape, 1), HEAD_DIM)
    outs = []
    for kv in range(N_HEADS // 2):
        heads = (2 * kv, 2 * kv + 1)
        scores = _bdot_nt(jnp.concatenate([head_cols(q, h) for h in heads], axis=0), head_cols(kw, kv))
        yield
        probs, sink_terms = [], []
        for half, h in enumerate(heads):
            s = jnp.where(visible, scores[half * n:(half + 1) * n], -jnp.inf)
            sink = sinks[:, h:h + 1]
            m = jnp.maximum(jnp.max(s, axis=-1, keepdims=True), sink)
            probs.append(jnp.exp(s - m))
            sink_terms.append(jnp.exp(sink - m))
        o = _bdot(jnp.concatenate(probs, axis=0), jnp.where(v_lane_head == kv, vw, 1.0))
        yield
        sums = head_cols(o, 1 - kv)[:, 0:1]
        outs += [head_cols(o, kv)[half * n:(half + 1) * n] / (sums[half * n:(half + 1) * n] + sink_terms[half])
                 for half in range(2)]
    out_box.append(jnp.concatenate(outs, axis=-1))


def _swa_plan(q_ref, k_ref, v_ref, kp_ref, vp_ref, sink_ref, o_ref):
    w = SWA_WINDOW
    has_prev = pl.program_id(0) > 0
    q = q_ref[...] * (HEAD_DIM ** -0.5)
    k_all = jnp.concatenate([kp_ref[...], k_ref[...]], axis=0)
    v_all = jnp.concatenate([vp_ref[...], v_ref[...]], axis=0)
    q_pos = _iota((w, 2 * w), 0) + w
    k_pos = _iota((w, 2 * w), 1)
    dist = q_pos - k_pos
    in_window = (dist >= 0) & (dist < w)
    sinks = sink_ref[...]
    blocks = []
    stages = []
    for b in range(q.shape[0] // w):
        visible = in_window if b > 0 else in_window & ((k_pos >= w) | has_prev)
        stages.append(_swa_block(q[b * w:(b + 1) * w], k_all[b * w:(b + 2) * w], v_all[b * w:(b + 2) * w],
                                 visible, sinks, blocks))

    def finish():
        o_ref[...] = jnp.concatenate(blocks, axis=0).astype(o_ref.dtype)

    return stages, SWA_STAGGER, finish


def _swa_parts(z, sinks, step):
    out_spec, out_shape = _mix_out(z.shape[0], step)
    w = SWA_WINDOW
    prev = lambda col: (lambda i: (jnp.maximum(i * (step // w) - 1, 0), col))
    return _MixerParts(
        plan=_swa_plan,
        args=[z, z, z, z, z, sinks.reshape(1, -1)],
        in_specs=[pl.BlockSpec((step, 256), lambda i: (i, 0)),
                  pl.BlockSpec((step, 128), lambda i: (i, 2)), pl.BlockSpec((step, 128), lambda i: (i, 3)),
                  pl.BlockSpec((w, 128), prev(2)), pl.BlockSpec((w, 128), prev(3)),
                  _resident((1, N_HEADS))],
        out_specs=[out_spec], out_shape=[out_shape], scratch=[])


def _rwkv_chunk(index, r, k, v, a_vec, b_vec, log_w, state_box, out_box):
    c = r.shape[0]
    width = r.shape[1]
    tri = jnp.where(_iota((c, c), 1) <= _iota((c, c), 0), 1.0, 0.0).astype(F32)
    p = _dot_mask(tri, log_w)
    yield
    p_total = p[c - 1:c, :]
    decay_in = jnp.exp(p)
    decay_out = jnp.exp(-p)
    decay_end = jnp.exp(p_total - p)
    a_in = a_vec * jnp.exp(p - log_w)
    r_in = r * decay_in
    b_out = b_vec * decay_out
    k_out = k * decay_out
    b_end = b_vec * decay_end
    k_end = k * decay_end

    t_pos = _iota((c, width), 0)
    assert width == N_HEADS * c
    s_pos = _imod(_iota((c, width), 1), c)
    strict = s_pos < t_pos
    incl = s_pos <= t_pos
    expand = lambda x: _expand_heads(x, HEAD_DIM)

    scores = _bdot_nt(jnp.concatenate([a_in, r_in], axis=0),
                      jnp.concatenate([expand(b_out), expand(k_out)], axis=0))
    yield
    a_ab = jnp.where(strict, scores[0:c, 0:width], 0.0)
    a_ak = jnp.where(strict, scores[0:c, width:2 * width], 0.0)
    a_rb = jnp.where(incl, scores[c:2 * c, 0:width], 0.0)
    a_rk = jnp.where(incl, scores[c:2 * c, width:2 * width], 0.0)

    t_inv = jnp.where(s_pos == t_pos, 1.0, 0.0) + a_ab
    from_v = _bdot(jnp.concatenate([a_ak, a_rk], axis=0), expand(v))
    x1 = from_v[0:c]
    y_from_v = from_v[c:2 * c]
    power = _bdot(a_ab, expand(a_ab))
    yield
    n_factors = (c - 1).bit_length()
    for _ in range(n_factors - 2):
        both = _bdot(jnp.concatenate([t_inv, power], axis=0), expand(power))
        t_inv = t_inv + both[0:c]
        power = both[c:2 * c]
        yield
    t_inv = t_inv + _bdot(t_inv, expand(power))
    yield
    sol = _bdot(t_inv, jnp.concatenate([expand(x1), expand(a_in)], axis=1))
    yield
    u0 = sol[:, 0:width]
    w_mat = sol[:, width:2 * width]

    assert len(out_box) == index, "the previous chunk must have replaced the state before it is read"
    state = state_box[0]
    from_state = _bdot_nt(jnp.concatenate([w_mat, r_in], axis=0), state)
    yield
    u = u0 + from_state[0:c]
    y = _bdot(a_rb, expand(u)) + y_from_v + from_state[c:2 * c]
    upd = _dot_tn(jnp.concatenate([u, v], axis=0).astype(BF16),
                  jnp.concatenate([b_end, k_end], axis=0).astype(BF16))
    same_head = _idiv(_iota((width, width), 0), HEAD_DIM) == _idiv(_iota((width, width), 1), HEAD_DIM)
    state_box[0] = state * jnp.exp(p_total) + jnp.where(same_head, upd, 0.0)
    out_box.append(y)


def _rwkv_plan(has_vres, *refs):
    if has_vres:
        (z_ref, zp_ref, vfirst_ref, mu_ref, w0_ref, wup_ref, a0_ref, aup_ref, gup_ref, kk_ref, ka_ref,
         rk_ref, lnw_ref, lnb_ref, v0_ref, vup_ref, o_ref, state_ref) = refs
    else:
        (z_ref, zp_ref, mu_ref, w0_ref, wup_ref, a0_ref, aup_ref, gup_ref, kk_ref, ka_ref,
         rk_ref, lnw_ref, lnb_ref, o_ref, vout_ref, state_ref) = refs
    step = pl.program_id(0)
    z = z_ref[...]
    last_prev = jnp.where(step > 0, zp_ref[7:8, :], 0.0)
    prev = jnp.where(_iota(z.shape, 0) == 0, last_prev, pltpu.roll(z, 1, axis=0))
    zr = z + (prev - z) * mu_ref[...]
    r = zr[:, 0:256]
    k = zr[:, 256:512]
    v = zr[:, 512:768]
    low = zr[:, 768:896]
    w_pre = w0_ref[...] + _dot_3pass(jnp.tanh(low), wup_ref[...])
    w_log = -(jnp.maximum(-w_pre, 0.0) + _softplus_neg_abs(w_pre)) - 0.5
    log_w = -jnp.exp(w_log)
    a = _sigmoid(a0_ref[...] + _dot_3pass(low, aup_ref[...]))
    g = _dot_3pass(_sigmoid(low), gup_ref[...])
    if has_vres:
        v = v + (vfirst_ref[...] - v) * _sigmoid(v0_ref[...] + _dot_3pass(low, vup_ref[...]))
    else:
        vout_ref[...] = v
    head_sum = _head_group_matrix(GROUP_WIDTH, HEAD_DIM, 1.0).astype(BF16)
    kk = k * kk_ref[...]
    kk = kk / jnp.maximum(jnp.sqrt(_dot_hilo(kk * kk, head_sum)), 1e-12)
    k = k * (1.0 + (a - 1.0) * ka_ref[...])
    a_vec = -kk
    b_vec = kk * a

    c = RWKV_CHUNK
    state_box = [state_ref[...]]
    chunks = []
    stages = []
    for n in range(z.shape[0] // c):
        rows = slice(n * c, (n + 1) * c)
        stages.append(_rwkv_chunk(n, r[rows], k[rows], v[rows], a_vec[rows], b_vec[rows], log_w[rows],
                                  state_box, chunks))

    def finish():
        state_ref[...] = state_box[0]
        y = jnp.concatenate(chunks, axis=0)
        head_mean = _head_group_matrix(GROUP_WIDTH, HEAD_DIM, 1.0 / HEAD_DIM).astype(BF16)
        mu_y = _dot_hilo(y, head_mean)
        d = y - mu_y
        var_y = _dot_hilo(d * d, head_mean)
        y = d * lax.rsqrt(var_y + RWKV_GN_EPS) * lnw_ref[...] + lnb_ref[...]
        bonus = _dot_hilo(r * k * rk_ref[...], head_sum) * v
        o_ref[...] = ((y + bonus) * g).astype(o_ref.dtype)

    return stages, RWKV_STAGGER, finish


def _rwkv_parts(z, mu, w0, w_up, a0, a_up, g_up, k_k, k_a, r_k, lnx_w, lnx_b, vres, step):
    t = z.shape[0]
    c = step
    row = lambda a: a.reshape(1, -1)
    low_rows = lambda a, start: jnp.zeros((LANES, GROUP_WIDTH), F32).at[start:start + a.shape[0]].set(a)
    has_vres = vres is not None
    mu_full = jnp.zeros((RWKV_W,), F32).at[:RWKV_COLS].set(mu)
    vec = _resident((1, GROUP_WIDTH))
    mat = _resident((LANES, GROUP_WIDTH))
    tile = pl.BlockSpec((c, GROUP_WIDTH), lambda i: (i, 0))
    z_specs = [pl.BlockSpec((c, RWKV_W), lambda i: (i, 0)),
               pl.BlockSpec((8, RWKV_W), lambda i: (jnp.maximum(i * (c // 8) - 1, 0), 0))]
    common = [row(w0), low_rows(w_up, 0), row(a0), low_rows(a_up, 16), low_rows(g_up, 32),
              row(k_k), row(k_a), row(r_k), row(lnx_w), row(lnx_b)]
    common_specs = [vec, mat, vec, mat, mat, vec, vec, vec, vec, vec]
    if has_vres:
        v_first, vres_mu, v0, v_up = vres
        mu_full = mu_full.at[RWKV_COLS:RWKV_COLS + RWKV_V_RANK].set(vres_mu)
        args = [z, z, v_first, row(mu_full)] + common + [row(v0), low_rows(v_up, 64)]
        in_specs = z_specs + [tile, _resident((1, RWKV_W))] + common_specs + [vec, mat]
        out_specs = [tile]
        out_shape = [jax.ShapeDtypeStruct((t, GROUP_WIDTH), MIX_DTYPE)]
    else:
        args = [z, z, row(mu_full)] + common
        in_specs = z_specs + [_resident((1, RWKV_W))] + common_specs
        out_specs = [tile, tile]
        out_shape = [jax.ShapeDtypeStruct((t, GROUP_WIDTH), MIX_DTYPE), jax.ShapeDtypeStruct((t, GROUP_WIDTH), F32)]
    return _MixerParts(plan=functools.partial(_rwkv_plan, has_vres), args=args, in_specs=in_specs,
                       out_specs=out_specs, out_shape=out_shape,
                       scratch=[pltpu.VMEM((GROUP_WIDTH, GROUP_WIDTH), F32)])


def _store_token_tiles(ref, index, x):
    n = x.shape[0]
    for j in range(SUBLANES):
        ref[(*index, pl.ds(j, n, stride=SUBLANES), slice(None))] = x[:, j * LANES:(j + 1) * LANES]


def _load_token_tiles(ref, index, n):
    return jnp.concatenate([ref[(*index, pl.ds(j, n, stride=SUBLANES), slice(None))] for j in range(SUBLANES)],
                           axis=-1)


def _top2_route(logits):
    row = _iota(logits.shape, 0).astype(F32)
    m1 = jnp.max(logits, axis=0, keepdims=True)
    i1 = jnp.min(jnp.where(logits == m1, row, N_EXPERTS), axis=0, keepdims=True)
    rest = jnp.where(row == i1, -jnp.inf, logits)
    m2 = jnp.max(rest, axis=0, keepdims=True)
    i2 = jnp.min(jnp.where(rest == m2, row, N_EXPERTS), axis=0, keepdims=True)
    e2 = jnp.exp(m2 - m1)
    g1 = 1.0 / (1.0 + e2)
    g2 = e2 * g1
    return jnp.where(row == 0, i1, jnp.where(row == 1, i2, jnp.where(row == 2, g1, jnp.where(row == 3, g2, 0.0))))


def _mix_residual_norm(alpha, x_ref, mix_refs, w_ref, g_ref, b_ref):
    acc = alpha * x_ref[...]
    for h, ref in enumerate(mix_refs):
        acc = acc + _dot(ref[...], w_ref[h * GROUP_WIDTH:(h + 1) * GROUP_WIDTH, :])
    return _layer_norm(acc, g_ref[...], b_ref[...])


def _outproj_route_kernel(alpha, x_ref, o0_ref, o1_ref, o2_ref, o3_ref, w_ref, g_ref, b_ref, router_ref,
                          tiles_ref, route_ref):
    y = _mix_residual_norm(alpha, x_ref, (o0_ref, o1_ref, o2_ref, o3_ref), w_ref, g_ref, b_ref)
    route_ref[...] = _top2_route(_dot_3pass_nt(router_ref[...], y))
    _store_token_tiles(tiles_ref, (), y)


def _out_proj_ln_route(alpha, x, mixes, w_out, g, b, router):
    t = x.shape[0]
    n = WIDE_ROW_TILE
    row_d = pl.BlockSpec((n, D_MODEL), lambda i: (i, 0))
    row_g = pl.BlockSpec((n, GROUP_WIDTH), lambda i: (i, 0))
    return pl.pallas_call(
        functools.partial(_outproj_route_kernel, alpha),
        grid=(t // n,),
        in_specs=[row_d, row_g, row_g, row_g, row_g, _resident((D_MODEL, D_MODEL)),
                  _resident((1, D_MODEL)), _resident((1, D_MODEL)), _resident((N_EXPERTS, D_MODEL))],
        out_specs=[pl.BlockSpec((n * SUBLANES, LANES), lambda i: (i, 0)),
                   pl.BlockSpec((N_EXPERTS, n), lambda i: (0, i))],
        out_shape=[jax.ShapeDtypeStruct((t * SUBLANES, LANES), F32), jax.ShapeDtypeStruct((N_EXPERTS, t), F32)],
        compiler_params=_params(("parallel",)),
        name="out_proj_ln_route",
    )(x, *mixes, w_out.astype(BF16), g.reshape(1, -1), b.reshape(1, -1), router.T)


def _ln_embed(y, ln_g, ln_b, p, ple_gate, ple_proj):
    x = _layer_norm(y, ln_g, ln_b)
    gate = _sigmoid(_dot(x.astype(BF16), ple_gate))
    return x + gate * _dot(p.astype(BF16), ple_proj)


def _dense_ffn_kernel(alpha, x_ref, o0_ref, o1_ref, o2_ref, o3_ref, wo_ref, g1_ref, b1_ref, p_ref, wg_ref, wu_ref,
                      wd_ref, g_ref, b_ref, pg_ref, pp_ref, y_ref, acc_ref):
    x = _mix_residual_norm(alpha, x_ref, (o0_ref, o1_ref, o2_ref, o3_ref), wo_ref, g1_ref, b1_ref)
    xb = x.astype(BF16)
    acc_ref[...] = alpha * x
    for j in range(D_FF // FF_CHUNK):
        cols = slice(j * FF_CHUNK, (j + 1) * FF_CHUNK)
        h = _silu(_dot(xb, wg_ref[:, cols])) * _dot(xb, wu_ref[:, cols])
        acc_ref[...] += _dot(h.astype(BF16), wd_ref[cols, :])
    y_ref[...] = _ln_embed(acc_ref[...], g_ref[...], b_ref[...], p_ref[...], pg_ref[...], pp_ref[...])


def _dense_layer_tail(alpha, x, mixes, w_out, g1, b1, p_all, p_block, w_gate, w_up, w_down, g, b, ple_gate, ple_proj):
    t = x.shape[0]
    row_d = pl.BlockSpec((ROW_TILE, D_MODEL), lambda i: (i, 0))
    row_g = pl.BlockSpec((ROW_TILE, GROUP_WIDTH), lambda i: (i, 0))
    return pl.pallas_call(
        functools.partial(_dense_ffn_kernel, alpha),
        grid=(t // ROW_TILE,),
        in_specs=[row_d, row_g, row_g, row_g, row_g, _resident((D_MODEL, D_MODEL)),
                  _resident((1, D_MODEL)), _resident((1, D_MODEL)),
                  pl.BlockSpec((ROW_TILE, PLE_DIM), lambda i: (p_block + i, 0)),
                  _resident((D_MODEL, D_FF)), _resident((D_MODEL, D_FF)), _resident((D_FF, D_MODEL)),
                  _resident((1, D_MODEL)), _resident((1, D_MODEL)),
                  _resident((D_MODEL, D_MODEL)), _resident((PLE_DIM, D_MODEL))],
        out_specs=row_d,
        out_shape=jax.ShapeDtypeStruct((t, D_MODEL), F32),
        scratch_shapes=[pltpu.VMEM((ROW_TILE, D_MODEL), F32)],
        compiler_params=_params(("parallel",)),
        name="dense_ffn_tail",
    )(x, *mixes, w_out.astype(BF16), g1.reshape(1, -1), b1.reshape(1, -1), p_all,
      w_gate.astype(BF16), w_up.astype(BF16), w_down.astype(BF16), g.reshape(1, -1), b.reshape(1, -1),
      ple_gate.astype(BF16), ple_proj.astype(BF16))


def _weight_group_copies(e, group, wg_hbm, wu_hbm, wd_hbm, stage_cols_ref, stage_rows_ref, sem):
    slot = group % 2
    lo, hi = group * MOE_WEIGHT_CHUNK, (group + 1) * MOE_WEIGHT_CHUNK
    return (pltpu.make_async_copy(wg_hbm.at[e, :, lo:hi], stage_cols_ref.at[slot, 0], sem.at[slot]),
            pltpu.make_async_copy(wu_hbm.at[e, :, lo:hi], stage_cols_ref.at[slot, 1], sem.at[slot]),
            pltpu.make_async_copy(wd_hbm.at[e, lo:hi, :], stage_rows_ref.at[slot], sem.at[slot]))


def _expert_kernel(row_tok_ref, row_dst_ref, block_e_ref, n_used_ref, x_hbm, wg_hbm, wu_hbm, wd_hbm, y_hbm,
                   rows_ref, xb_ref, acc_ref, ybuf_ref, wg_ref, wu_ref, wd_ref, stage_cols_ref, stage_rows_ref,
                   gather_sem, scatter_sem, weight_sem):
    i = pl.program_id(0)
    n_used = n_used_ref[0]
    last = pl.num_programs(0) - 1
    n = MOE_ROWS
    slot = lax.rem(i, 2)
    other = 1 - slot
    tile = lambda first_row: pl.ds(pl.multiple_of(first_row, SUBLANES), SUBLANES)
    gather_row = lambda tok_row, s, r: pltpu.make_async_copy(
        x_hbm.at[tile(tok_row)], rows_ref.at[s, tile(r * SUBLANES)], gather_sem.at[s])
    scatter_row = lambda dst_row, s, r: pltpu.make_async_copy(
        ybuf_ref.at[s, tile(r * SUBLANES)], y_hbm.at[tile(dst_row)], scatter_sem.at[s])
    block_rows = n * SUBLANES
    gather_block = lambda s: pltpu.make_async_copy(x_hbm.at[pl.ds(0, block_rows)], rows_ref.at[s], gather_sem.at[s])
    scatter_block = lambda s: pltpu.make_async_copy(ybuf_ref.at[s], y_hbm.at[pl.ds(0, block_rows)],
                                                    scatter_sem.at[s])

    @pl.when(i == 0)
    def _():
        ybuf_ref[1] = jnp.zeros((block_rows, LANES), F32)

        def start(r, carry):
            gather_row(row_tok_ref[r], 0, r).start()
            return carry

        lax.fori_loop(0, n, start, 0)

    def scatter_all(first_dst, s):
        def start(r, carry):
            scatter_row(row_dst_ref[first_dst + r], s, r).start()
            return carry

        lax.fori_loop(0, n, start, 0)
        scatter_block(s).wait()

    expert = block_e_ref[i]
    new_expert = (i == 0) | (expert != block_e_ref[jnp.maximum(i - 1, 0)])

    weight_group = lambda g: _weight_group_copies(expert, g, wg_hbm, wu_hbm, wd_hbm, stage_cols_ref,
                                                  stage_rows_ref, weight_sem)
    chunks_per_group = MOE_WEIGHT_CHUNK // FF_CHUNK
    n_groups = D_FF_EXPERT // MOE_WEIGHT_CHUNK

    def take_weight_group(g):
        if g + 1 < n_groups:
            for copy in weight_group(g + 1):
                copy.start()
        for copy in weight_group(g):
            copy.wait()
        piece = slice(g * MOE_WEIGHT_CHUNK, (g + 1) * MOE_WEIGHT_CHUNK)
        wg_ref[:, piece] = stage_cols_ref[g % 2, 0].astype(BF16)
        wu_ref[:, piece] = stage_cols_ref[g % 2, 1].astype(BF16)
        wd_ref[piece, :] = stage_rows_ref[g % 2].astype(BF16)

    @pl.when(i < n_used)
    def _():
        @pl.when(new_expert)
        def _():
            for copy in weight_group(0):
                copy.start()

        gather_block(slot).wait()
        xb_ref[...] = _load_token_tiles(rows_ref, (slot,), n).astype(BF16)
        n_chunks = D_FF_EXPERT // FF_CHUNK
        rows_per_chunk = -(-n // (n_chunks - MOE_DMA_FREE_CHUNKS))
        for j in range(n_chunks):
            if j % chunks_per_group == 0:
                pl.when(new_expert)(functools.partial(take_weight_group, j // chunks_per_group))
            cols = slice(j * FF_CHUNK, (j + 1) * FF_CHUNK)
            xb = xb_ref[...]
            h = _silu(_dot(xb, wg_ref[:, cols])) * _dot(xb, wu_ref[:, cols])
            part = _dot(h.astype(BF16), wd_ref[cols, :])
            if j == 0:
                acc_ref[...] = part
            else:
                acc_ref[...] += part
            for r in range(j * rows_per_chunk, min((j + 1) * rows_per_chunk, n)):
                gather_row(row_tok_ref[(i + 1) * n + r], other, r).start(priority=GATHER_DMA_PRIORITY)
                scatter_row(row_dst_ref[i * n + r], other, r).start(priority=SCATTER_DMA_PRIORITY)
        _store_token_tiles(ybuf_ref, (slot,), acc_ref[...])
        scatter_block(other).wait()

    @pl.when(i == n_used)
    def _():
        gather_block(slot).wait()
        scatter_all(i * n, other)

    @pl.when(i >= n_used)
    def _():
        rows_ref[slot] = jnp.zeros((block_rows, LANES), F32)
        fill = pltpu.make_async_copy(rows_ref.at[slot], y_hbm.at[pl.ds(i * block_rows, block_rows)],
                                     scatter_sem.at[slot])
        fill.start()
        fill.wait()

    @pl.when((i == last) & (i < n_used))
    def _():
        gather_block(other).wait()
        scatter_all((i + 1) * n, slot)


def _expert_rows(x, row_tok, row_dst, block_e, n_used, n_out_rows, w_gate, w_up, w_down):
    n_blocks = block_e.shape[0]
    hbm = pl.BlockSpec(memory_space=pl.ANY)
    grid_spec = pltpu.PrefetchScalarGridSpec(
        num_scalar_prefetch=4,
        grid=(n_blocks,),
        in_specs=[hbm, hbm, hbm, hbm],
        out_specs=hbm,
        scratch_shapes=[pltpu.VMEM((2, MOE_ROWS * SUBLANES, LANES), F32), pltpu.VMEM((MOE_ROWS, D_MODEL), BF16),
                        pltpu.VMEM((MOE_ROWS, D_MODEL), F32), pltpu.VMEM((2, MOE_ROWS * SUBLANES, LANES), F32),
                        pltpu.VMEM((D_MODEL, D_FF_EXPERT), BF16), pltpu.VMEM((D_MODEL, D_FF_EXPERT), BF16),
                        pltpu.VMEM((D_FF_EXPERT, D_MODEL), BF16),
                        pltpu.VMEM((2, 2, D_MODEL, MOE_WEIGHT_CHUNK), F32),
                        pltpu.VMEM((2, MOE_WEIGHT_CHUNK, D_MODEL), F32),
                        pltpu.SemaphoreType.DMA((2,)), pltpu.SemaphoreType.DMA((2,)),
                        pltpu.SemaphoreType.DMA((2,))],
    )
    return pl.pallas_call(
        _expert_kernel,
        grid_spec=grid_spec,
        out_shape=jax.ShapeDtypeStruct((n_out_rows * SUBLANES, LANES), F32),
        compiler_params=_params(("arbitrary",)),
        name="moe_experts",
    )(row_tok, row_dst, block_e, n_used.reshape(1), x, w_gate, w_up, w_down)


def _combine_kernel(alpha, x_ref, y0_ref, y1_ref, gates_ref, p_ref, g_ref, b_ref, pg_ref, pp_ref, o_ref):
    n = WIDE_ROW_TILE
    gates = gates_ref[...]
    f = _load_token_tiles(y0_ref, (), n) * gates[:, 0:1] + _load_token_tiles(y1_ref, (), n) * gates[:, 1:2]
    o_ref[...] = _ln_embed(alpha * _load_token_tiles(x_ref, (), n) + f, g_ref[...], b_ref[...], p_ref[...],
                           pg_ref[...], pp_ref[...])


def _moe_combine_tail(alpha, x_tiles, p_all, p_block, y_tiles, gates, g, b, ple_gate, ple_proj):
    t = x_tiles.shape[0] // SUBLANES
    n = WIDE_ROW_TILE
    row_d = pl.BlockSpec((n, D_MODEL), lambda i: (i, 0))
    tiles_d = pl.BlockSpec((n * SUBLANES, LANES), lambda i: (i, 0))
    return pl.pallas_call(
        functools.partial(_combine_kernel, alpha),
        grid=(t // n,),
        in_specs=[tiles_d, tiles_d, pl.BlockSpec((n * SUBLANES, LANES), lambda i: (t // n + i, 0)),
                  pl.BlockSpec((n, 2), lambda i: (i, 0)), pl.BlockSpec((n, PLE_DIM), lambda i: (p_block + i, 0)),
                  _resident((1, D_MODEL)), _resident((1, D_MODEL)),
                  _resident((D_MODEL, D_MODEL)), _resident((PLE_DIM, D_MODEL))],
        out_specs=row_d,
        out_shape=jax.ShapeDtypeStruct((t, D_MODEL), F32),
        compiler_params=_params(("parallel",)),
        name="moe_combine_tail",
    )(x_tiles, y_tiles, y_tiles, gates, p_all, g.reshape(1, -1), b.reshape(1, -1), ple_gate.astype(BF16),
      ple_proj.astype(BF16))


def _moe_tail(alpha, x_tiles, route, p_all, p_block, w_gate, w_up, w_down, g, b, ple_gate, ple_proj):
    t = route.shape[1]
    experts = route[0:2].T.astype(jnp.int32)
    gates = route[2:4].T
    e_flat = experts.reshape(-1)
    onehot = (e_flat[:, None] == jnp.arange(N_EXPERTS, dtype=jnp.int32)[None, :]).astype(jnp.int32)
    counts = jnp.sum(onehot, axis=0)
    padded = (counts + MOE_ROWS - 1) // MOE_ROWS * MOE_ROWS
    pad_end = jnp.cumsum(padded)
    pad_start = pad_end - padded
    n_blocks = (2 * t) // MOE_ROWS + N_EXPERTS
    n_rows = n_blocks * MOE_ROWS
    block_start = jnp.arange(n_blocks, dtype=jnp.int32) * MOE_ROWS
    block_e = jnp.minimum(jnp.sum((block_start[:, None] >= pad_end[None, :]).astype(jnp.int32), axis=1),
                          N_EXPERTS - 1)
    order = jnp.argsort(e_flat, stable=True).astype(jnp.int32)
    seg_end = jnp.cumsum(counts)
    seg_start = seg_end - counts
    per_row = lambda per_expert: jnp.repeat(per_expert[block_e], MOE_ROWS)
    place = jnp.arange(n_rows, dtype=jnp.int32) - per_row(pad_start - seg_start)
    used = place < per_row(seg_end)
    row_assign = jnp.where(used, order[jnp.clip(place, 0, 2 * t - 1)], -1)
    spare = 2 * t + jnp.cumsum(jnp.where(used, 0, 1).astype(jnp.int32)) - 1
    row_tok = jnp.where(used, row_assign // 2, 0)
    row_dst = jnp.where(used, (row_assign % 2) * t + row_assign // 2, spare)
    n_spare = n_rows - 2 * t
    first_dst = 2 * t + n_spare + jnp.arange(MOE_ROWS, dtype=jnp.int32)
    row_tok = jnp.concatenate([row_tok, jnp.zeros((MOE_ROWS,), jnp.int32)])
    row_dst = jnp.concatenate([first_dst, row_dst])
    n_used = (pad_end[-1] // MOE_ROWS).astype(jnp.int32)
    y_tiles = _expert_rows(x_tiles, row_tok * SUBLANES, row_dst * SUBLANES, block_e, n_used,
                           2 * t + n_spare + MOE_ROWS, w_gate, w_up, w_down)
    return _moe_combine_tail(alpha, x_tiles, p_all, p_block, y_tiles, gates, g, b, ple_gate, ple_proj)


def kernel(x, p, w_in, w_out, gla_gk_up, gla_gk_bias, gla_norm_w, hgrn_lower_bounds, hgrn_norm_w, swa_sinks, rwkv_mu, rwkv_w0, rwkv_w_up, rwkv_a0, rwkv_a_up, rwkv_g_up, rwkv_k_k, rwkv_k_a, rwkv_r_k, rwkv_lnx_w, rwkv_lnx_b, rwkv_vres_down, rwkv_vres_mu, rwkv_v0, rwkv_vres_up, ln1_g, ln1_b, ln2_g, ln2_b, ffn_w_gate, ffn_w_up, ffn_w_down, moe_router, moe_w_gate, moe_w_up, moe_w_down, ple_proj, ple_gate):
    bsz, seq, d = x.shape
    depth = w_in.shape[0]
    alpha = (2.0 * depth) ** 0.25
    lbs = jnp.cumsum(jax.nn.softmax(hgrn_lower_bounds.astype(F32), axis=0), axis=0)
    lbs = lbs - lbs[0]
    p_all = p.reshape(-1, PLE_DIM)
    outs = []
    for bi in range(bsz):
        xt = x[bi]
        v_first = None
        for i in range(depth):
            w = _group_in_weights(w_in[i], None if i == 0 else rwkv_vres_down[i - 1])
            z_gla, z_hgrn, z_swa, z_rwkv = _in_proj(xt, w)
            vres = None if i == 0 else (v_first, rwkv_vres_mu[i - 1], rwkv_v0[i - 1], rwkv_vres_up[i - 1])
            mixed = _call_mixers(
                [_gla_parts(z_gla, gla_gk_up[i], gla_gk_bias[i], gla_norm_w[i], MIX_STEP),
                 _hgrn_parts(z_hgrn, lbs[i], hgrn_norm_w[i], MIX_STEP),
                 _swa_parts(z_swa, swa_sinks[i], MIX_STEP),
                 _rwkv_parts(z_rwkv, rwkv_mu[i], rwkv_w0[i], rwkv_w_up[i], rwkv_a0[i], rwkv_a_up[i],
                             rwkv_g_up[i], rwkv_k_k[i], rwkv_k_a[i], rwkv_r_k[i].reshape(-1),
                             rwkv_lnx_w[i], rwkv_lnx_b[i], vres, MIX_STEP)],
                seq, MIX_STEP)
            mixes = tuple(mixed[0:4])
            if i == 0:
                v_first = mixed[4]
            j = i // 2
            p_row = (i * bsz + bi) * seq
            if i % 2 == 0:
                xt = _dense_layer_tail(alpha, xt, mixes, w_out[i], ln1_g[i], ln1_b[i], p_all, p_row // ROW_TILE,
                                       ffn_w_gate[j], ffn_w_up[j], ffn_w_down[j], ln2_g[i], ln2_b[i],
                                       ple_gate[i], ple_proj[i])
            else:
                x_tiles, route = _out_proj_ln_route(alpha, xt, mixes, w_out[i], ln1_g[i], ln1_b[i], moe_router[j])
                xt = _moe_tail(alpha, x_tiles, route, p_all, p_row // WIDE_ROW_TILE, moe_w_gate[j], moe_w_up[j],
                               moe_w_down[j],
                               ln2_g[i], ln2_b[i], ple_gate[i], ple_proj[i])
        outs.append(xt)
    return jnp.stack(outs, axis=0)
```

```python
import functools
from typing import Callable, NamedTuple

import jax
import jax.numpy as jnp
from jax import lax
from jax.experimental import pallas as pl
from jax.experimental.pallas import tpu as pltpu

F32 = jnp.float32
BF16 = jnp.bfloat16
MIX_DTYPE = BF16

D_MODEL = 1024
GROUP_WIDTH = 256
N_HEADS = 4
HEAD_DIM = 64
GLA_DK = 32
GLA_GATE_RANK = 16
GLA_GATE_NORMALIZER = 16.0
SWA_WINDOW = 128
RWKV_COLS = 3 * GROUP_WIDTH + 16 + 16 + 32
RWKV_V_RANK = 8
LN_EPS = 1e-5
RMS_EPS = 1e-6
RWKV_GN_EPS = 64e-5
D_FF = 2816
N_EXPERTS = 8
D_FF_EXPERT = 3584
PLE_DIM = 256

LANES = 128
SUBLANES = 8
GLA_W = 896
HGRN_W = 1024
SWA_W = 512
RWKV_W = 896
Z_W = GLA_W + HGRN_W + SWA_W + RWKV_W

GLA_SUB = 16
GLA_TILE = 128
GLA_STAGGER = 5
SWA_STAGGER = 5
MIX_STEP = 1024
RWKV_CHUNK = 64
RWKV_STAGGER = 2
ROW_TILE = 512
WIDE_ROW_TILE = 1024
FF_CHUNK = 256
MOE_ROWS = 512
MOE_WEIGHT_CHUNK = 512
MOE_DMA_FREE_CHUNKS = 4
GATHER_DMA_PRIORITY = 0
SCATTER_DMA_PRIORITY = 1
VMEM_LIMIT = 56 * 1024 * 1024


def _iota(shape, dim):
    return lax.broadcasted_iota(jnp.int32, shape, dim)


def _idiv(x, n):
    return jnp.right_shift(x, n.bit_length() - 1)


def _imod(x, n):
    return jnp.bitwise_and(x, n - 1)


def _dot(a, b):
    return jnp.dot(a, b, preferred_element_type=F32)


def _dot_nt(a, b):
    return lax.dot_general(a, b, (((1,), (1,)), ((), ())), preferred_element_type=F32)


def _dot_tn(a, b):
    return lax.dot_general(a, b, (((0,), (0,)), ((), ())), preferred_element_type=F32)


def _bdot(a, b):
    return _dot(a.astype(BF16), b.astype(BF16))


def _bdot_nt(a, b):
    return _dot_nt(a.astype(BF16), b.astype(BF16))


def _dot_hilo(x, m):
    hi = x.astype(BF16)
    lo = (x - hi.astype(F32)).astype(BF16)
    return _dot(hi, m) + _dot(lo, m)


def _dot_mask(m, x):
    m = m.astype(BF16)
    x1 = x.astype(BF16)
    r1 = x - x1.astype(F32)
    x2 = r1.astype(BF16)
    x3 = (r1 - x2.astype(F32)).astype(BF16)
    return _dot(m, x1) + _dot(m, x2) + _dot(m, x3)


def _dot_3pass_nt(x, w):
    x_hi = x.astype(BF16)
    x_lo = (x - x_hi.astype(F32)).astype(BF16)
    w_hi = w.astype(BF16)
    w_lo = (w - w_hi.astype(F32)).astype(BF16)
    return _dot_nt(x_hi, w_hi) + _dot_nt(x_lo, w_hi) + _dot_nt(x_hi, w_lo)


def _dot_3pass(x, w):
    x_hi = x.astype(BF16)
    x_lo = (x - x_hi.astype(F32)).astype(BF16)
    w_hi = w.astype(BF16)
    w_lo = (w - w_hi.astype(F32)).astype(BF16)
    return _dot(x_hi, w_hi) + _dot(x_lo, w_hi) + _dot(x_hi, w_lo)


def _sigmoid(x):
    return 1.0 / (1.0 + jnp.exp(-x))


def _silu(x):
    return x * _sigmoid(x)


def _softplus_neg_abs(x):
    return jnp.log(1.0 + jnp.exp(-jnp.abs(x)))


def _log_sigmoid(x):
    return jnp.minimum(x, 0.0) - _softplus_neg_abs(x)


def _layer_norm(y, g, b):
    mu = jnp.mean(y, axis=-1, keepdims=True)
    d = y - mu
    var = jnp.mean(d * d, axis=-1, keepdims=True)
    return d * lax.rsqrt(var + LN_EPS) * g + b


def _expand_heads(x, head_width):
    lane_head = _idiv(_iota(x.shape, 1), head_width)
    return jnp.concatenate([jnp.where(lane_head == h, x, 0.0) for h in range(N_HEADS)], axis=0)


def _head_group_matrix(width, head_width, value):
    same = _idiv(_iota((width, width), 0), head_width) == _idiv(_iota((width, width), 1), head_width)
    return jnp.where(same, value, 0.0).astype(F32)


def _resident(shape):
    nd = len(shape)
    return pl.BlockSpec(shape, lambda *_: (0,) * nd, pipeline_mode=pl.Buffered(1))


def _params(semantics):
    return pltpu.CompilerParams(dimension_semantics=semantics, vmem_limit_bytes=VMEM_LIMIT)


def _run_plans(plans):
    live = [dict(enumerate(stages)) for stages, _, _ in plans]
    rnd = 0
    while any(live):
        for group, (_, stagger, _) in zip(live, plans):
            for n in sorted(group):
                if rnd >= n * stagger:
                    try:
                        next(group[n])
                    except StopIteration:
                        del group[n]
        rnd += 1
    for _, _, finish in plans:
        finish()


def _inproj_kernel(x_ref, w_ref, gla_ref, hgrn_ref, swa_ref, rwkv_ref):
    xb = x_ref[...].astype(BF16)
    o = 0
    for ref, width in ((gla_ref, GLA_W), (hgrn_ref, HGRN_W), (swa_ref, SWA_W), (rwkv_ref, RWKV_W)):
        ref[...] = _dot(xb, w_ref[:, o:o + width])
        o += width


def _in_proj(x, w):
    t = x.shape[0]
    widths = (GLA_W, HGRN_W, SWA_W, RWKV_W)
    return pl.pallas_call(
        _inproj_kernel,
        grid=(t // WIDE_ROW_TILE,),
        in_specs=[pl.BlockSpec((WIDE_ROW_TILE, D_MODEL), lambda i: (i, 0)), _resident((D_MODEL, Z_W))],
        out_specs=[pl.BlockSpec((WIDE_ROW_TILE, w_), lambda i: (i, 0)) for w_ in widths],
        out_shape=[jax.ShapeDtypeStruct((t, w_), F32) for w_ in widths],
        compiler_params=_params(("parallel",)),
        name="in_proj",
    )(x, w)


def _group_in_weights(w_in, vres_down):
    gla, hgrn, swa, rwkv = jnp.split(w_in, (784, 784 + 1024, 784 + 1024 + 512), axis=1)
    if vres_down is not None:
        rwkv = jnp.concatenate([rwkv, vres_down], axis=1)
    pad = lambda a, w_: jnp.pad(a, ((0, 0), (0, w_ - a.shape[1])))
    return jnp.concatenate([pad(gla, GLA_W), hgrn, swa, pad(rwkv, RWKV_W)], axis=1).astype(BF16)


class _MixerParts(NamedTuple):
    plan: Callable
    args: list
    in_specs: list
    out_specs: list
    out_shape: list
    scratch: list


def _mixers_kernel(layout, *refs):
    n_in = sum(entry[1] for entry in layout)
    n_out = sum(entry[2] for entry in layout)
    ins, outs, scratch = list(refs[:n_in]), list(refs[n_in:n_in + n_out]), list(refs[n_in + n_out:])

    @pl.when(pl.program_id(0) == 0)
    def _():
        for ref in scratch:
            ref[...] = jnp.zeros_like(ref)

    plans = []
    for plan, n_i, n_o, n_s in layout:
        plans.append(plan(*ins[:n_i], *outs[:n_o], *scratch[:n_s]))
        del ins[:n_i], outs[:n_o], scratch[:n_s]
    _run_plans(plans)


def _call_mixers(parts_list, tokens, step):
    layout = tuple((p.plan, len(p.args), len(p.out_shape), len(p.scratch)) for p in parts_list)
    flat = lambda field: [item for p in parts_list for item in getattr(p, field)]
    return pl.pallas_call(
        functools.partial(_mixers_kernel, layout),
        grid=(tokens // step,),
        in_specs=flat("in_specs"),
        out_specs=flat("out_specs"),
        out_shape=flat("out_shape"),
        scratch_shapes=flat("scratch"),
        compiler_params=_params(("arbitrary",)),
        name="mixers",
    )(*flat("args"))


def _mix_out(tokens, step):
    return (pl.BlockSpec((step, GROUP_WIDTH), lambda i: (i, 0)),
            jax.ShapeDtypeStruct((tokens, GROUP_WIDTH), MIX_DTYPE))


def _gated_linear_attention_tile(index, q, k, v, log_f, state_box, out_box):
    length, kw = q.shape
    head_k = kw // N_HEADS
    n_sub = length // GLA_SUB
    row = _iota((length, length), 0)
    col = _iota((length, length), 1)
    same_sub = _idiv(row, GLA_SUB) == _idiv(col, GLA_SUB)
    m_local = jnp.where(same_sub & (col <= row), 1.0, 0.0).astype(F32)
    m_prev = jnp.where(_idiv(col, GLA_SUB) < _idiv(row, GLA_SUB), 1.0, 0.0).astype(F32)
    sums = _dot_mask(jnp.concatenate([m_local, m_prev], axis=0), log_f)
    yield
    b_local = sums[0:length]
    b_start = sums[length:2 * length]
    b_full = b_start + b_local
    q_local = q * jnp.exp(b_local)

    q_pos = _imod(_iota((N_HEADS * GLA_SUB, length), 0), GLA_SUB)
    s_pos = _iota((N_HEADS * GLA_SUB, length), 1)
    probs = []
    for c in range(n_sub):
        r0, r1 = c * GLA_SUB, (c + 1) * GLA_SUB
        expo = jnp.where(_iota((r1, kw), 0) < r0, b_start[r0:r0 + 1, :] - b_full[0:r1], -b_local[0:r1])
        k_ref = k[0:r1] * jnp.exp(expo)
        if r1 < length:
            k_ref = jnp.concatenate([k_ref, jnp.zeros((length - r1, kw), F32)], axis=0)
        q_heads = _expand_heads(q_local[r0:r1, :], head_k)
        s = _bdot_nt(q_heads, k_ref)
        probs.append(jnp.where(s_pos <= q_pos + r0, s, 0.0))
    b_total = b_full[length - 1:length, :]
    k_end = k * jnp.exp(b_total - b_full)
    upd = _dot_tn(v.astype(BF16), k_end.astype(BF16))
    yield
    o_heads = _bdot(jnp.concatenate(probs, axis=0), v)
    assert len(out_box) == index, "the previous tile must have replaced the state before it is read"
    state_t = state_box[0]
    o_state = _bdot_nt(q * jnp.exp(b_full), state_t)
    same_head = _idiv(_iota((GROUP_WIDTH, kw), 0), HEAD_DIM) == _idiv(_iota((GROUP_WIDTH, kw), 1), head_k)
    state_box[0] = state_t * jnp.exp(b_total) + jnp.where(same_head, upd, 0.0)
    yield
    v_head = _idiv(_iota((GLA_SUB, GROUP_WIDTH), 1), HEAD_DIM)
    rows = []
    for c in range(n_sub):
        base = c * N_HEADS * GLA_SUB
        acc = jnp.zeros((GLA_SUB, GROUP_WIDTH), F32)
        for h in range(N_HEADS):
            acc = acc + jnp.where(v_head == h, o_heads[base + h * GLA_SUB:base + (h + 1) * GLA_SUB, :], 0.0)
        rows.append(acc)
    out_box.append(jnp.concatenate(rows, axis=0) + o_state)


def _gated_linear_attention_plan(q, k, v, log_f, gate, norm_w, o_ref, state_ref):
    state_box = [state_ref[...]]
    tiles = []
    stages = []
    for n in range(q.shape[0] // GLA_TILE):
        rows = slice(n * GLA_TILE, (n + 1) * GLA_TILE)
        stages.append(_gated_linear_attention_tile(n, q[rows], k[rows], v[rows], log_f[rows], state_box, tiles))

    def finish():
        state_ref[...] = state_box[0]
        o = jnp.concatenate(tiles, axis=0)
        ms = _dot_hilo(o * o, _head_group_matrix(GROUP_WIDTH, HEAD_DIM, 1.0 / HEAD_DIM).astype(BF16))
        o_ref[...] = (o * lax.rsqrt(ms + RMS_EPS) * norm_w * _silu(gate)).astype(o_ref.dtype)

    return stages, GLA_STAGGER, finish


def _gla_plan(z_ref, gk_up_ref, gk_bias_ref, norm_w_ref, o_ref, state_ref):
    z = z_ref[...]
    q = z[:, 0:128] * (GLA_DK ** -0.5)
    k = z[:, 128:256]
    v = z[:, 256:512]
    g = z[:, 512:768]
    gate_in = _dot_3pass(z[:, 768:896], gk_up_ref[...]) + gk_bias_ref[...]
    log_f = _log_sigmoid(gate_in) * (1.0 / GLA_GATE_NORMALIZER)
    return _gated_linear_attention_plan(q, k, v, log_f, g, norm_w_ref[...], o_ref, state_ref)


def _hgrn_plan(z_ref, lb_ref, log_lb_ref, norm_w_ref, o_ref, state_ref):
    z = z_ref[...]
    q = _silu(z[:, 0:256])
    f = z[:, 256:512]
    v = z[:, 512:768]
    g = z[:, 768:1024]
    lb = lb_ref[...]
    a = log_lb_ref[...]
    e = jnp.exp(-jnp.abs(f))
    inv = 1.0 / (1.0 + e)
    c = jnp.log1p(-lb) + jnp.minimum(f, 0.0) - jnp.log(1.0 + e)
    log_f = jnp.maximum(a, c) + _softplus_neg_abs(a - c)
    k = (1.0 - lb) * jnp.where(f >= 0.0, e * inv, inv)
    return _gated_linear_attention_plan(q, k, v, log_f, g, norm_w_ref[...], o_ref, state_ref)


def _gla_parts(z, gk_up, gk_bias, norm_w, step):
    out_spec, out_shape = _mix_out(z.shape[0], step)
    gk_up_pad = jnp.zeros((LANES, N_HEADS * GLA_DK), F32).at[:GLA_GATE_RANK].set(gk_up)
    return _MixerParts(
        plan=_gla_plan,
        args=[z, gk_up_pad, gk_bias.reshape(1, -1), jnp.tile(norm_w, N_HEADS).reshape(1, -1)],
        in_specs=[pl.BlockSpec((step, GLA_W), lambda i: (i, 0)),
                  _resident((LANES, N_HEADS * GLA_DK)), _resident((1, N_HEADS * GLA_DK)),
                  _resident((1, GROUP_WIDTH))],
        out_specs=[out_spec], out_shape=[out_shape],
        scratch=[pltpu.VMEM((GROUP_WIDTH, N_HEADS * GLA_DK), F32)])


def _hgrn_parts(z, lb, norm_w, step):
    out_spec, out_shape = _mix_out(z.shape[0], step)
    return _MixerParts(
        plan=_hgrn_plan,
        args=[z, lb.reshape(1, -1), jnp.log(lb).reshape(1, -1), jnp.tile(norm_w, N_HEADS).reshape(1, -1)],
        in_specs=[pl.BlockSpec((step, HGRN_W), lambda i: (i, 0)),
                  _resident((1, GROUP_WIDTH)), _resident((1, GROUP_WIDTH)), _resident((1, GROUP_WIDTH))],
        out_specs=[out_spec], out_shape=[out_shape],
        scratch=[pltpu.VMEM((GROUP_WIDTH, GROUP_WIDTH), F32)])


def _swa_block(q, kw, vw, visible, sinks, out_box):
    n = q.shape[0]
    head_cols = lambda x, h: x[:, h * HEAD_DIM:(h + 1) * HEAD_DIM]
    v_lane_head = _idiv(_iota(vw.shape, 1), HEAD_DIM)
    outs = []
    for kv in range(N_HEADS // 2):
        heads = (2 * kv, 2 * kv + 1)
        scores = _bdot_nt(jnp.concatenate([head_cols(q, h) for h in heads], axis=0), head_cols(kw, kv))
        yield
        probs, sink_terms = [], []
        for half, h in enumerate(heads):
            s = jnp.where(visible, scores[half * n:(half + 1) * n], -jnp.inf)
            sink = sinks[:, h:h + 1]
            m = jnp.maximum(jnp.max(s, axis=-1, keepdims=True), sink)
            probs.append(jnp.exp(s - m))
            sink_terms.append(jnp.exp(sink - m))
        o = _bdot(jnp.concatenate(probs, axis=0), jnp.where(v_lane_head == kv, vw, 1.0))
        yield
        sums = head_cols(o, 1 - kv)[:, 0:1]
        outs += [head_cols(o, kv)[half * n:(half + 1) * n] / (sums[half * n:(half + 1) * n] + sink_terms[half])
                 for half in range(2)]
    out_box.append(jnp.concatenate(outs, axis=-1))


def _swa_plan(q_ref, k_ref, v_ref, kp_ref, vp_ref, sink_ref, o_ref):
    w = SWA_WINDOW
    has_prev = pl.program_id(0) > 0
    q = q_ref[...] * (HEAD_DIM ** -0.5)
    k_all = jnp.concatenate([kp_ref[...], k_ref[...]], axis=0)
    v_all = jnp.concatenate([vp_ref[...], v_ref[...]], axis=0)
    q_pos = _iota((w, 2 * w), 0) + w
    k_pos = _iota((w, 2 * w), 1)
    dist = q_pos - k_pos
    in_window = (dist >= 0) & (dist < w)
    sinks = sink_ref[...]
    blocks = []
    stages = []
    for b in range(q.shape[0] // w):
        visible = in_window if b > 0 else in_window & ((k_pos >= w) | has_prev)
        stages.append(_swa_block(q[b * w:(b + 1) * w], k_all[b * w:(b + 2) * w], v_all[b * w:(b + 2) * w],
                                 visible, sinks, blocks))

    def finish():
        o_ref[...] = jnp.concatenate(blocks, axis=0).astype(o_ref.dtype)

    return stages, SWA_STAGGER, finish


def _swa_parts(z, sinks, step):
    out_spec, out_shape = _mix_out(z.shape[0], step)
    w = SWA_WINDOW
    prev = lambda col: (lambda i: (jnp.maximum(i * (step // w) - 1, 0), col))
    return _MixerParts(
        plan=_swa_plan,
        args=[z, z, z, z, z, sinks.reshape(1, -1)],
        in_specs=[pl.BlockSpec((step, 256), lambda i: (i, 0)),
                  pl.BlockSpec((step, 128), lambda i: (i, 2)), pl.BlockSpec((step, 128), lambda i: (i, 3)),
                  pl.BlockSpec((w, 128), prev(2)), pl.BlockSpec((w, 128), prev(3)),
                  _resident((1, N_HEADS))],
        out_specs=[out_spec], out_shape=[out_shape], scratch=[])


def _rwkv_chunk(index, r, k, v, a_vec, b_vec, log_w, state_box, out_box):
    c = r.shape[0]
    width = r.shape[1]
    tri = jnp.where(_iota((c, c), 1) <= _iota((c, c), 0), 1.0, 0.0).astype(F32)
    p = _dot_mask(tri, log_w)
    yield
    p_total = p[c - 1:c, :]
    decay_in = jnp.exp(p)
    decay_out = jnp.exp(-p)
    decay_end = jnp.exp(p_total - p)
    a_in = a_vec * jnp.exp(p - log_w)
    r_in = r * decay_in
    b_out = b_vec * decay_out
    k_out = k * decay_out
    b_end = b_vec * decay_end
    k_end = k * decay_end

    t_pos = _iota((c, width), 0)
    assert width == N_HEADS * c
    s_pos = _imod(_iota((c, width), 1), c)
    strict = s_pos < t_pos
    incl = s_pos <= t_pos
    expand = lambda x: _expand_heads(x, HEAD_DIM)

    scores = _bdot_nt(jnp.concatenate([a_in, r_in], axis=0),
                      jnp.concatenate([expand(b_out), expand(k_out)], axis=0))
    yield
    a_ab = jnp.where(strict, scores[0:c, 0:width], 0.0)
    a_ak = jnp.where(strict, scores[0:c, width:2 * width], 0.0)
    a_rb = jnp.where(incl, scores[c:2 * c, 0:width], 0.0)
    a_rk = jnp.where(incl, scores[c:2 * c, width:2 * width], 0.0)

    t_inv = jnp.where(s_pos == t_pos, 1.0, 0.0) + a_ab
    from_v = _bdot(jnp.concatenate([a_ak, a_rk], axis=0), expand(v))
    x1 = from_v[0:c]
    y_from_v = from_v[c:2 * c]
    power = _bdot(a_ab, expand(a_ab))
    yield
    n_factors = (c - 1).bit_length()
    for _ in range(n_factors - 2):
        both = _bdot(jnp.concatenate([t_inv, power], axis=0), expand(power))
        t_inv = t_inv + both[0:c]
        power = both[c:2 * c]
        yield
    t_inv = t_inv + _bdot(t_inv, expand(power))
    yield
    sol = _bdot(t_inv, jnp.concatenate([expand(x1), expand(a_in)], axis=1))
    yield
    u0 = sol[:, 0:width]
    w_mat = sol[:, width:2 * width]

    assert len(out_box) == index, "the previous chunk must have replaced the state before it is read"
    state = state_box[0]
    from_state = _bdot_nt(jnp.concatenate([w_mat, r_in], axis=0), state)
    yield
    u = u0 + from_state[0:c]
    y = _bdot(a_rb, expand(u)) + y_from_v + from_state[c:2 * c]
    upd = _dot_tn(jnp.concatenate([u, v], axis=0).astype(BF16),
                  jnp.concatenate([b_end, k_end], axis=0).astype(BF16))
    same_head = _idiv(_iota((width, width), 0), HEAD_DIM) == _idiv(_iota((width, width), 1), HEAD_DIM)
    state_box[0] = state * jnp.exp(p_total) + jnp.where(same_head, upd, 0.0)
    out_box.append(y)


def _rwkv_plan(has_vres, *refs):
    if has_vres:
        (z_ref, zp_ref, vfirst_ref, mu_ref, w0_ref, wup_ref, a0_ref, aup_ref, gup_ref, kk_ref, ka_ref,
         rk_ref, lnw_ref, lnb_ref, v0_ref, vup_ref, o_ref, state_ref) = refs
    else:
        (z_ref, zp_ref, mu_ref, w0_ref, wup_ref, a0_ref, aup_ref, gup_ref, kk_ref, ka_ref,
         rk_ref, lnw_ref, lnb_ref, o_ref, vout_ref, state_ref) = refs
    step = pl.program_id(0)
    z = z_ref[...]
    last_prev = jnp.where(step > 0, zp_ref[7:8, :], 0.0)
    prev = jnp.where(_iota(z.shape, 0) == 0, last_prev, pltpu.roll(z, 1, axis=0))
    zr = z + (prev - z) * mu_ref[...]
    r = zr[:, 0:256]
    k = zr[:, 256:512]
    v = zr[:, 512:768]
    low = zr[:, 768:896]
    w_pre = w0_ref[...] + _dot_3pass(jnp.tanh(low), wup_ref[...])
    w_log = -(jnp.maximum(-w_pre, 0.0) + _softplus_neg_abs(w_pre)) - 0.5
    log_w = -jnp.exp(w_log)
    a = _sigmoid(a0_ref[...] + _dot_3pass(low, aup_ref[...]))
    g = _dot_3pass(_sigmoid(low), gup_ref[...])
    if has_vres:
        v = v + (vfirst_ref[...] - v) * _sigmoid(v0_ref[...] + _dot_3pass(low, vup_ref[...]))
    else:
        vout_ref[...] = v
    head_sum = _head_group_matrix(GROUP_WIDTH, HEAD_DIM, 1.0).astype(BF16)
    kk = k * kk_ref[...]
    kk = kk / jnp.maximum(jnp.sqrt(_dot_hilo(kk * kk, head_sum)), 1e-12)
    k = k * (1.0 + (a - 1.0) * ka_ref[...])
    a_vec = -kk
    b_vec = kk * a

    c = RWKV_CHUNK
    state_box = [state_ref[...]]
    chunks = []
    stages = []
    for n in range(z.shape[0] // c):
        rows = slice(n * c, (n + 1) * c)
        stages.append(_rwkv_chunk(n, r[rows], k[rows], v[rows], a_vec[rows], b_vec[rows], log_w[rows],
                                  state_box, chunks))

    def finish():
        state_ref[...] = state_box[0]
        y = jnp.concatenate(chunks, axis=0)
        head_mean = _head_group_matrix(GROUP_WIDTH, HEAD_DIM, 1.0 / HEAD_DIM).astype(BF16)
        mu_y = _dot_hilo(y, head_mean)
        d = y - mu_y
        var_y = _dot_hilo(d * d, head_mean)
        y = d * lax.rsqrt(var_y + RWKV_GN_EPS) * lnw_ref[...] + lnb_ref[...]
        bonus = _dot_hilo(r * k * rk_ref[...], head_sum) * v
        o_ref[...] = ((y + bonus) * g).astype(o_ref.dtype)

    return stages, RWKV_STAGGER, finish


def _rwkv_parts(z, mu, w0, w_up, a0, a_up, g_up, k_k, k_a, r_k, lnx_w, lnx_b, vres, step):
    t = z.shape[0]
    c = step
    row = lambda a: a.reshape(1, -1)
    low_rows = lambda a, start: jnp.zeros((LANES, GROUP_WIDTH), F32).at[start:start + a.shape[0]].set(a)
    has_vres = vres is not None
    mu_full = jnp.zeros((RWKV_W,), F32).at[:RWKV_COLS].set(mu)
    vec = _resident((1, GROUP_WIDTH))
    mat = _resident((LANES, GROUP_WIDTH))
    tile = pl.BlockSpec((c, GROUP_WIDTH), lambda i: (i, 0))
    z_specs = [pl.BlockSpec((c, RWKV_W), lambda i: (i, 0)),
               pl.BlockSpec((8, RWKV_W), lambda i: (jnp.maximum(i * (c // 8) - 1, 0), 0))]
    common = [row(w0), low_rows(w_up, 0), row(a0), low_rows(a_up, 16), low_rows(g_up, 32),
              row(k_k), row(k_a), row(r_k), row(lnx_w), row(lnx_b)]
    common_specs = [vec, mat, vec, mat, mat, vec, vec, vec, vec, vec]
    if has_vres:
        v_first, vres_mu, v0, v_up = vres
        mu_full = mu_full.at[RWKV_COLS:RWKV_COLS + RWKV_V_RANK].set(vres_mu)
        args = [z, z, v_first, row(mu_full)] + common + [row(v0), low_rows(v_up, 64)]
        in_specs = z_specs + [tile, _resident((1, RWKV_W))] + common_specs + [vec, mat]
        out_specs = [tile]
        out_shape = [jax.ShapeDtypeStruct((t, GROUP_WIDTH), MIX_DTYPE)]
    else:
        args = [z, z, row(mu_full)] + common
        in_specs = z_specs + [_resident((1, RWKV_W))] + common_specs
        out_specs = [tile, tile]
        out_shape = [jax.ShapeDtypeStruct((t, GROUP_WIDTH), MIX_DTYPE), jax.ShapeDtypeStruct((t, GROUP_WIDTH), F32)]
    return _MixerParts(plan=functools.partial(_rwkv_plan, has_vres), args=args, in_specs=in_specs,
                       out_specs=out_specs, out_shape=out_shape,
                       scratch=[pltpu.VMEM((GROUP_WIDTH, GROUP_WIDTH), F32)])


def _store_token_tiles(ref, index, x):
    n = x.shape[0]
    for j in range(SUBLANES):
        ref[(*index, pl.ds(j, n, stride=SUBLANES), slice(None))] = x[:, j * LANES:(j + 1) * LANES]


def _load_token_tiles(ref, index, n):
    return jnp.concatenate([ref[(*index, pl.ds(j, n, stride=SUBLANES), slice(None))] for j in range(SUBLANES)],
                           axis=-1)


def _top2_route(logits):
    row = _iota(logits.shape, 0).astype(F32)
    m1 = jnp.max(logits, axis=0, keepdims=True)
    i1 = jnp.min(jnp.where(logits == m1, row, N_EXPERTS), axis=0, keepdims=True)
    rest = jnp.where(row == i1, -jnp.inf, logits)
    m2 = jnp.max(rest, axis=0, keepdims=True)
    i2 = jnp.min(jnp.where(rest == m2, row, N_EXPERTS), axis=0, keepdims=True)
    e2 = jnp.exp(m2 - m1)
    g1 = 1.0 / (1.0 + e2)
    g2 = e2 * g1
    return jnp.where(row == 0, i1, jnp.where(row == 1, i2, jnp.where(row == 2, g1, jnp.where(row == 3, g2, 0.0))))


def _mix_residual_norm(alpha, x_ref, mix_refs, w_ref, g_ref, b_ref):
    acc = alpha * x_ref[...]
    for h, ref in enumerate(mix_refs):
        acc = acc + _dot(ref[...], w_ref[h * GROUP_WIDTH:(h + 1) * GROUP_WIDTH, :])
    return _layer_norm(acc, g_ref[...], b_ref[...])


def _outproj_route_kernel(alpha, x_ref, o0_ref, o1_ref, o2_ref, o3_ref, w_ref, g_ref, b_ref, router_ref,
                          tiles_ref, route_ref):
    y = _mix_residual_norm(alpha, x_ref, (o0_ref, o1_ref, o2_ref, o3_ref), w_ref, g_ref, b_ref)
    route_ref[...] = _top2_route(_dot_3pass_nt(router_ref[...], y))
    _store_token_tiles(tiles_ref, (), y)


def _out_proj_ln_route(alpha, x, mixes, w_out, g, b, router):
    t = x.shape[0]
    n = WIDE_ROW_TILE
    row_d = pl.BlockSpec((n, D_MODEL), lambda i: (i, 0))
    row_g = pl.BlockSpec((n, GROUP_WIDTH), lambda i: (i, 0))
    return pl.pallas_call(
        functools.partial(_outproj_route_kernel, alpha),
        grid=(t // n,),
        in_specs=[row_d, row_g, row_g, row_g, row_g, _resident((D_MODEL, D_MODEL)),
                  _resident((1, D_MODEL)), _resident((1, D_MODEL)), _resident((N_EXPERTS, D_MODEL))],
        out_specs=[pl.BlockSpec((n * SUBLANES, LANES), lambda i: (i, 0)),
                   pl.BlockSpec((N_EXPERTS, n), lambda i: (0, i))],
        out_shape=[jax.ShapeDtypeStruct((t * SUBLANES, LANES), F32), jax.ShapeDtypeStruct((N_EXPERTS, t), F32)],
        compiler_params=_params(("parallel",)),
        name="out_proj_ln_route",
    )(x, *mixes, w_out.astype(BF16), g.reshape(1, -1), b.reshape(1, -1), router.T)


def _ln_embed(y, ln_g, ln_b, p, ple_gate, ple_proj):
    x = _layer_norm(y, ln_g, ln_b)
    gate = _sigmoid(_dot(x.astype(BF16), ple_gate))
    return x + gate * _dot(p.astype(BF16), ple_proj)


def _dense_ffn_kernel(alpha, x_ref, o0_ref, o1_ref, o2_ref, o3_ref, wo_ref, g1_ref, b1_ref, p_ref, wg_ref, wu_ref,
                      wd_ref, g_ref, b_ref, pg_ref, pp_ref, y_ref, acc_ref):
    x = _mix_residual_norm(alpha, x_ref, (o0_ref, o1_ref, o2_ref, o3_ref), wo_ref, g1_ref, b1_ref)
    xb = x.astype(BF16)
    acc_ref[...] = alpha * x
    for j in range(D_FF // FF_CHUNK):
        cols = slice(j * FF_CHUNK, (j + 1) * FF_CHUNK)
        h = _silu(_dot(xb, wg_ref[:, cols])) * _dot(xb, wu_ref[:, cols])
        acc_ref[...] += _dot(h.astype(BF16), wd_ref[cols, :])
    y_ref[...] = _ln_embed(acc_ref[...], g_ref[...], b_ref[...], p_ref[...], pg_ref[...], pp_ref[...])


def _dense_layer_tail(alpha, x, mixes, w_out, g1, b1, p_all, p_block, w_gate, w_up, w_down, g, b, ple_gate, ple_proj):
    t = x.shape[0]
    row_d = pl.BlockSpec((ROW_TILE, D_MODEL), lambda i: (i, 0))
    row_g = pl.BlockSpec((ROW_TILE, GROUP_WIDTH), lambda i: (i, 0))
    return pl.pallas_call(
        functools.partial(_dense_ffn_kernel, alpha),
        grid=(t // ROW_TILE,),
        in_specs=[row_d, row_g, row_g, row_g, row_g, _resident((D_MODEL, D_MODEL)),
                  _resident((1, D_MODEL)), _resident((1, D_MODEL)),
                  pl.BlockSpec((ROW_TILE, PLE_DIM), lambda i: (p_block + i, 0)),
                  _resident((D_MODEL, D_FF)), _resident((D_MODEL, D_FF)), _resident((D_FF, D_MODEL)),
                  _resident((1, D_MODEL)), _resident((1, D_MODEL)),
                  _resident((D_MODEL, D_MODEL)), _resident((PLE_DIM, D_MODEL))],
        out_specs=row_d,
        out_shape=jax.ShapeDtypeStruct((t, D_MODEL), F32),
        scratch_shapes=[pltpu.VMEM((ROW_TILE, D_MODEL), F32)],
        compiler_params=_params(("parallel",)),
        name="dense_ffn_tail",
    )(x, *mixes, w_out.astype(BF16), g1.reshape(1, -1), b1.reshape(1, -1), p_all,
      w_gate.astype(BF16), w_up.astype(BF16), w_down.astype(BF16), g.reshape(1, -1), b.reshape(1, -1),
      ple_gate.astype(BF16), ple_proj.astype(BF16))


def _weight_group_copies(e, group, wg_hbm, wu_hbm, wd_hbm, stage_cols_ref, stage_rows_ref, sem):
    slot = group % 2
    lo, hi = group * MOE_WEIGHT_CHUNK, (group + 1) * MOE_WEIGHT_CHUNK
    return (pltpu.make_async_copy(wg_hbm.at[e, :, lo:hi], stage_cols_ref.at[slot, 0], sem.at[slot]),
            pltpu.make_async_copy(wu_hbm.at[e, :, lo:hi], stage_cols_ref.at[slot, 1], sem.at[slot]),
            pltpu.make_async_copy(wd_hbm.at[e, lo:hi, :], stage_rows_ref.at[slot], sem.at[slot]))


def _expert_kernel(row_tok_ref, row_dst_ref, block_e_ref, n_used_ref, x_hbm, wg_hbm, wu_hbm, wd_hbm, y_hbm,
                   rows_ref, xb_ref, acc_ref, ybuf_ref, wg_ref, wu_ref, wd_ref, stage_cols_ref, stage_rows_ref,
                   gather_sem, scatter_sem, weight_sem):
    i = pl.program_id(0)
    n_used = n_used_ref[0]
    last = pl.num_programs(0) - 1
    n = MOE_ROWS
    slot = lax.rem(i, 2)
    other = 1 - slot
    tile = lambda first_row: pl.ds(pl.multiple_of(first_row, SUBLANES), SUBLANES)
    gather_row = lambda tok_row, s, r: pltpu.make_async_copy(
        x_hbm.at[tile(tok_row)], rows_ref.at[s, tile(r * SUBLANES)], gather_sem.at[s])
    scatter_row = lambda dst_row, s, r: pltpu.make_async_copy(
        ybuf_ref.at[s, tile(r * SUBLANES)], y_hbm.at[tile(dst_row)], scatter_sem.at[s])
    block_rows = n * SUBLANES
    gather_block = lambda s: pltpu.make_async_copy(x_hbm.at[pl.ds(0, block_rows)], rows_ref.at[s], gather_sem.at[s])
    scatter_block = lambda s: pltpu.make_async_copy(ybuf_ref.at[s], y_hbm.at[pl.ds(0, block_rows)],
                                                    scatter_sem.at[s])

    @pl.when(i == 0)
    def _():
        ybuf_ref[1] = jnp.zeros((block_rows, LANES), F32)

        def start(r, carry):
            gather_row(row_tok_ref[r], 0, r).start()
            return carry

        lax.fori_loop(0, n, start, 0)

    def scatter_all(first_dst, s):
        def start(r, carry):
            scatter_row(row_dst_ref[first_dst + r], s, r).start()
            return carry

        lax.fori_loop(0, n, start, 0)
        scatter_block(s).wait()

    expert = block_e_ref[i]
    new_expert = (i == 0) | (expert != block_e_ref[jnp.maximum(i - 1, 0)])

    weight_group = lambda g: _weight_group_copies(expert, g, wg_hbm, wu_hbm, wd_hbm, stage_cols_ref,
                                                  stage_rows_ref, weight_sem)
    chunks_per_group = MOE_WEIGHT_CHUNK // FF_CHUNK
    n_groups = D_FF_EXPERT // MOE_WEIGHT_CHUNK

    def take_weight_group(g):
        if g + 1 < n_groups:
            for copy in weight_group(g + 1):
                copy.start()
        for copy in weight_group(g):
            copy.wait()
        piece = slice(g * MOE_WEIGHT_CHUNK, (g + 1) * MOE_WEIGHT_CHUNK)
        wg_ref[:, piece] = stage_cols_ref[g % 2, 0].astype(BF16)
        wu_ref[:, piece] = stage_cols_ref[g % 2, 1].astype(BF16)
        wd_ref[piece, :] = stage_rows_ref[g % 2].astype(BF16)

    @pl.when(i < n_used)
    def _():
        @pl.when(new_expert)
        def _():
            for copy in weight_group(0):
                copy.start()

        gather_block(slot).wait()
        xb_ref[...] = _load_token_tiles(rows_ref, (slot,), n).astype(BF16)
        n_chunks = D_FF_EXPERT // FF_CHUNK
        rows_per_chunk = -(-n // (n_chunks - MOE_DMA_FREE_CHUNKS))
        for j in range(n_chunks):
            if j % chunks_per_group == 0:
                pl.when(new_expert)(functools.partial(take_weight_group, j // chunks_per_group))
            cols = slice(j * FF_CHUNK, (j + 1) * FF_CHUNK)
            xb = xb_ref[...]
            h = _silu(_dot(xb, wg_ref[:, cols])) * _dot(xb, wu_ref[:, cols])
            part = _dot(h.astype(BF16), wd_ref[cols, :])
            if j == 0:
                acc_ref[...] = part
            else:
                acc_ref[...] += part
            for r in range(j * rows_per_chunk, min((j + 1) * rows_per_chunk, n)):
                gather_row(row_tok_ref[(i + 1) * n + r], other, r).start(priority=GATHER_DMA_PRIORITY)
                scatter_row(row_dst_ref[i * n + r], other, r).start(priority=SCATTER_DMA_PRIORITY)
        _store_token_tiles(ybuf_ref, (slot,), acc_ref[...])
        scatter_block(other).wait()

    @pl.when(i == n_used)
    def _():
        gather_block(slot).wait()
        scatter_all(i * n, other)

    @pl.when(i >= n_used)
    def _():
        rows_ref[slot] = jnp.zeros((block_rows, LANES), F32)
        fill = pltpu.make_async_copy(rows_ref.at[slot], y_hbm.at[pl.ds(i * block_rows, block_rows)],
                                     scatter_sem.at[slot])
        fill.start()
        fill.wait()

    @pl.when((i == last) & (i < n_used))
    def _():
        gather_block(other).wait()
        scatter_all((i + 1) * n, slot)


def _expert_rows(x, row_tok, row_dst, block_e, n_used, n_out_rows, w_gate, w_up, w_down):
    n_blocks = block_e.shape[0]
    hbm = pl.BlockSpec(memory_space=pl.ANY)
    grid_spec = pltpu.PrefetchScalarGridSpec(
        num_scalar_prefetch=4,
        grid=(n_blocks,),
        in_specs=[hbm, hbm, hbm, hbm],
        out_specs=hbm,
        scratch_shapes=[pltpu.VMEM((2, MOE_ROWS * SUBLANES, LANES), F32), pltpu.VMEM((MOE_ROWS, D_MODEL), BF16),
                        pltpu.VMEM((MOE_ROWS, D_MODEL), F32), pltpu.VMEM((2, MOE_ROWS * SUBLANES, LANES), F32),
                        pltpu.VMEM((D_MODEL, D_FF_EXPERT), BF16), pltpu.VMEM((D_MODEL, D_FF_EXPERT), BF16),
                        pltpu.VMEM((D_FF_EXPERT, D_MODEL), BF16),
                        pltpu.VMEM((2, 2, D_MODEL, MOE_WEIGHT_CHUNK), F32),
                        pltpu.VMEM((2, MOE_WEIGHT_CHUNK, D_MODEL), F32),
                        pltpu.SemaphoreType.DMA((2,)), pltpu.SemaphoreType.DMA((2,)),
                        pltpu.SemaphoreType.DMA((2,))],
    )
    return pl.pallas_call(
        _expert_kernel,
        grid_spec=grid_spec,
        out_shape=jax.ShapeDtypeStruct((n_out_rows * SUBLANES, LANES), F32),
        compiler_params=_params(("arbitrary",)),
        name="moe_experts",
    )(row_tok, row_dst, block_e, n_used.reshape(1), x, w_gate, w_up, w_down)


def _combine_kernel(alpha, x_ref, y0_ref, y1_ref, gates_ref, p_ref, g_ref, b_ref, pg_ref, pp_ref, o_ref):
    n = WIDE_ROW_TILE
    gates = gates_ref[...]
    f = _load_token_tiles(y0_ref, (), n) * gates[:, 0:1] + _load_token_tiles(y1_ref, (), n) * gates[:, 1:2]
    o_ref[...] = _ln_embed(alpha * _load_token_tiles(x_ref, (), n) + f, g_ref[...], b_ref[...], p_ref[...],
                           pg_ref[...], pp_ref[...])


def _moe_combine_tail(alpha, x_tiles, p_all, p_block, y_tiles, gates, g, b, ple_gate, ple_proj):
    t = x_tiles.shape[0] // SUBLANES
    n = WIDE_ROW_TILE
    row_d = pl.BlockSpec((n, D_MODEL), lambda i: (i, 0))
    tiles_d = pl.BlockSpec((n * SUBLANES, LANES), lambda i: (i, 0))
    return pl.pallas_call(
        functools.partial(_combine_kernel, alpha),
        grid=(t // n,),
        in_specs=[tiles_d, tiles_d, pl.BlockSpec((n * SUBLANES, LANES), lambda i: (t // n + i, 0)),
                  pl.BlockSpec((n, 2), lambda i: (i, 0)), pl.BlockSpec((n, PLE_DIM), lambda i: (p_block + i, 0)),
                  _resident((1, D_MODEL)), _resident((1, D_MODEL)),
                  _resident((D_MODEL, D_MODEL)), _resident((PLE_DIM, D_MODEL))],
        out_specs=row_d,
        out_shape=jax.ShapeDtypeStruct((t, D_MODEL), F32),
        compiler_params=_params(("parallel",)),
        name="moe_combine_tail",
    )(x_tiles, y_tiles, y_tiles, gates, p_all, g.reshape(1, -1), b.reshape(1, -1), ple_gate.astype(BF16),
      ple_proj.astype(BF16))


def _moe_tail(alpha, x_tiles, route, p_all, p_block, w_gate, w_up, w_down, g, b, ple_gate, ple_proj):
    t = route.shape[1]
    experts = route[0:2].T.astype(jnp.int32)
    gates = route[2:4].T
    e_flat = experts.reshape(-1)
    onehot = (e_flat[:, None] == jnp.arange(N_EXPERTS, dtype=jnp.int32)[None, :]).astype(jnp.int32)
    counts = jnp.sum(onehot, axis=0)
    padded = (counts + MOE_ROWS - 1) // MOE_ROWS * MOE_ROWS
    pad_end = jnp.cumsum(padded)
    pad_start = pad_end - padded
    n_blocks = (2 * t) // MOE_ROWS + N_EXPERTS
    n_rows = n_blocks * MOE_ROWS
    block_start = jnp.arange(n_blocks, dtype=jnp.int32) * MOE_ROWS
    block_e = jnp.minimum(jnp.sum((block_start[:, None] >= pad_end[None, :]).astype(jnp.int32), axis=1),
                          N_EXPERTS - 1)
    order = jnp.argsort(e_flat, stable=True).astype(jnp.int32)
    seg_end = jnp.cumsum(counts)
    seg_start = seg_end - counts
    per_row = lambda per_expert: jnp.repeat(per_expert[block_e], MOE_ROWS)
    place = jnp.arange(n_rows, dtype=jnp.int32) - per_row(pad_start - seg_start)
    used = place < per_row(seg_end)
    row_assign = jnp.where(used, order[jnp.clip(place, 0, 2 * t - 1)], -1)
    spare = 2 * t + jnp.cumsum(jnp.where(used, 0, 1).astype(jnp.int32)) - 1
    row_tok = jnp.where(used, row_assign // 2, 0)
    row_dst = jnp.where(used, (row_assign % 2) * t + row_assign // 2, spare)
    n_spare = n_rows - 2 * t
    first_dst = 2 * t + n_spare + jnp.arange(MOE_ROWS, dtype=jnp.int32)
    row_tok = jnp.concatenate([row_tok, jnp.zeros((MOE_ROWS,), jnp.int32)])
    row_dst = jnp.concatenate([first_dst, row_dst])
    n_used = (pad_end[-1] // MOE_ROWS).astype(jnp.int32)
    y_tiles = _expert_rows(x_tiles, row_tok * SUBLANES, row_dst * SUBLANES, block_e, n_used,
                           2 * t + n_spare + MOE_ROWS, w_gate, w_up, w_down)
    return _moe_combine_tail(alpha, x_tiles, p_all, p_block, y_tiles, gates, g, b, ple_gate, ple_proj)


def kernel(x, p, w_in, w_out, gla_gk_up, gla_gk_bias, gla_norm_w, hgrn_lower_bounds, hgrn_norm_w, swa_sinks, rwkv_mu, rwkv_w0, rwkv_w_up, rwkv_a0, rwkv_a_up, rwkv_g_up, rwkv_k_k, rwkv_k_a, rwkv_r_k, rwkv_lnx_w, rwkv_lnx_b, rwkv_vres_down, rwkv_vres_mu, rwkv_v0, rwkv_vres_up, ln1_g, ln1_b, ln2_g, ln2_b, ffn_w_gate, ffn_w_up, ffn_w_down, moe_router, moe_w_gate, moe_w_up, moe_w_down, ple_proj, ple_gate):
    bsz, seq, d = x.shape
    depth = w_in.shape[0]
    alpha = (2.0 * depth) ** 0.25
    lbs = jnp.cumsum(jax.nn.softmax(hgrn_lower_bounds.astype(F32), axis=0), axis=0)
    lbs = lbs - lbs[0]
    p_all = p.reshape(-1, PLE_DIM)
    outs = []
    for bi in range(bsz):
        xt = x[bi]
        v_first = None
        for i in range(depth):
            w = _group_in_weights(w_in[i], None if i == 0 else rwkv_vres_down[i - 1])
            z_gla, z_hgrn, z_swa, z_rwkv = _in_proj(xt, w)
            vres = None if i == 0 else (v_first, rwkv_vres_mu[i - 1], rwkv_v0[i - 1], rwkv_vres_up[i - 1])
            mixed = _call_mixers(
                [_gla_parts(z_gla, gla_gk_up[i], gla_gk_bias[i], gla_norm_w[i], MIX_STEP),
                 _hgrn_parts(z_hgrn, lbs[i], hgrn_norm_w[i], MIX_STEP),
                 _swa_parts(z_swa, swa_sinks[i], MIX_STEP),
                 _rwkv_parts(z_rwkv, rwkv_mu[i], rwkv_w0[i], rwkv_w_up[i], rwkv_a0[i], rwkv_a_up[i],
                             rwkv_g_up[i], rwkv_k_k[i], rwkv_k_a[i], rwkv_r_k[i].reshape(-1),
                             rwkv_lnx_w[i], rwkv_lnx_b[i], vres, MIX_STEP)],
                seq, MIX_STEP)
            mixes = tuple(mixed[0:4])
            if i == 0:
                v_first = mixed[4]
            j = i // 2
            p_row = (i * bsz + bi) * seq
            if i % 2 == 0:
                xt = _dense_layer_tail(alpha, xt, mixes, w_out[i], ln1_g[i], ln1_b[i], p_all, p_row // ROW_TILE,
                                       ffn_w_gate[j], ffn_w_up[j], ffn_w_down[j], ln2_g[i], ln2_b[i],
                                       ple_gate[i], ple_proj[i])
            else:
                x_tiles, route = _out_proj_ln_route(alpha, xt, mixes, w_out[i], ln1_g[i], ln1_b[i], moe_router[j])
                xt = _moe_tail(alpha, x_tiles, route, p_all, p_row // WIDE_ROW_TILE, moe_w_gate[j], moe_w_up[j],
                               moe_w_down[j],
                               ln2_g[i], ln2_b[i], ple_gate[i], ple_proj[i])
        outs.append(xt)
    return jnp.stack(outs, axis=0)
```

```python
import functools
from typing import Callable, NamedTuple

import jax
import jax.numpy as jnp
from jax import lax
from jax.experimental import pallas as pl
from jax.experimental.pallas import tpu as pltpu

F32 = jnp.float32
BF16 = jnp.bfloat16
MIX_DTYPE = BF16

D_MODEL = 1024
GROUP_WIDTH = 256
N_HEADS = 4
HEAD_DIM = 64
GLA_DK = 32
GLA_GATE_RANK = 16
GLA_GATE_NORMALIZER = 16.0
SWA_WINDOW = 128
RWKV_COLS = 3 * GROUP_WIDTH + 16 + 16 + 32
RWKV_V_RANK = 8
LN_EPS = 1e-5
RMS_EPS = 1e-6
RWKV_GN_EPS = 64e-5
D_FF = 2816
N_EXPERTS = 8
D_FF_EXPERT = 3584
PLE_DIM = 256

LANES = 128
SUBLANES = 8
GLA_W = 896
HGRN_W = 1024
SWA_W = 512
RWKV_W = 896
Z_W = GLA_W + HGRN_W + SWA_W + RWKV_W

GLA_SUB = 16
GLA_TILE = 128
GLA_STAGGER = 5
SWA_STAGGER = 5
MIX_STEP = 1024
RWKV_CHUNK = 64
RWKV_STAGGER = 2
ROW_TILE = 512
WIDE_ROW_TILE = 1024
FF_CHUNK = 256
MOE_ROWS = 512
MOE_WEIGHT_CHUNK = 512
MOE_DMA_FREE_CHUNKS = 10
GATHER_DMA_PRIORITY = 0
SCATTER_DMA_PRIORITY = 1
VMEM_LIMIT = 56 * 1024 * 1024


def _iota(shape, dim):
    return lax.broadcasted_iota(jnp.int32, shape, dim)


def _idiv(x, n):
    return jnp.right_shift(x, n.bit_length() - 1)


def _imod(x, n):
    return jnp.bitwise_and(x, n - 1)


def _dot(a, b):
    return jnp.dot(a, b, preferred_element_type=F32)


def _dot_nt(a, b):
    return lax.dot_general(a, b, (((1,), (1,)), ((), ())), preferred_element_type=F32)


def _dot_tn(a, b):
    return lax.dot_general(a, b, (((0,), (0,)), ((), ())), preferred_element_type=F32)


def _bdot(a, b):
    return _dot(a.astype(BF16), b.astype(BF16))


def _bdot_nt(a, b):
    return _dot_nt(a.astype(BF16), b.astype(BF16))


def _dot_hilo(x, m):
    hi = x.astype(BF16)
    lo = (x - hi.astype(F32)).astype(BF16)
    return _dot(hi, m) + _dot(lo, m)


def _dot_mask(m, x):
    m = m.astype(BF16)
    x1 = x.astype(BF16)
    r1 = x - x1.astype(F32)
    x2 = r1.astype(BF16)
    x3 = (r1 - x2.astype(F32)).astype(BF16)
    return _dot(m, x1) + _dot(m, x2) + _dot(m, x3)


def _dot_3pass_nt(x, w):
    x_hi = x.astype(BF16)
    x_lo = (x - x_hi.astype(F32)).astype(BF16)
    w_hi = w.astype(BF16)
    w_lo = (w - w_hi.astype(F32)).astype(BF16)
    return _dot_nt(x_hi, w_hi) + _dot_nt(x_lo, w_hi) + _dot_nt(x_hi, w_lo)


def _dot_3pass(x, w):
    x_hi = x.astype(BF16)
    x_lo = (x - x_hi.astype(F32)).astype(BF16)
    w_hi = w.astype(BF16)
    w_lo = (w - w_hi.astype(F32)).astype(BF16)
    return _dot(x_hi, w_hi) + _dot(x_lo, w_hi) + _dot(x_hi, w_lo)


def _sigmoid(x):
    return 1.0 / (1.0 + jnp.exp(-x))


def _silu(x):
    return x * _sigmoid(x)


def _softplus_neg_abs(x):
    return jnp.log(1.0 + jnp.exp(-jnp.abs(x)))


def _log_sigmoid(x):
    return jnp.minimum(x, 0.0) - _softplus_neg_abs(x)


def _layer_norm(y, g, b):
    mu = jnp.mean(y, axis=-1, keepdims=True)
    d = y - mu
    var = jnp.mean(d * d, axis=-1, keepdims=True)
    return d * lax.rsqrt(var + LN_EPS) * g + b


def _expand_heads(x, head_width):
    lane_head = _idiv(_iota(x.shape, 1), head_width)
    return jnp.concatenate([jnp.where(lane_head == h, x, 0.0) for h in range(N_HEADS)], axis=0)


def _head_group_matrix(width, head_width, value):
    same = _idiv(_iota((width, width), 0), head_width) == _idiv(_iota((width, width), 1), head_width)
    return jnp.where(same, value, 0.0).astype(F32)


def _resident(shape):
    nd = len(shape)
    return pl.BlockSpec(shape, lambda *_: (0,) * nd, pipeline_mode=pl.Buffered(1))


def _params(semantics):
    return pltpu.CompilerParams(dimension_semantics=semantics, vmem_limit_bytes=VMEM_LIMIT)


def _run_plans(plans):
    live = [dict(enumerate(stages)) for stages, _, _ in plans]
    rnd = 0
    while any(live):
        for group, (_, stagger, _) in zip(live, plans):
            for n in sorted(group):
                if rnd >= n * stagger:
                    try:
                        next(group[n])
                    except StopIteration:
                        del group[n]
        rnd += 1
    for _, _, finish in plans:
        finish()


def _inproj_kernel(x_ref, w_ref, gla_ref, hgrn_ref, swa_ref, rwkv_ref):
    xb = x_ref[...].astype(BF16)
    o = 0
    for ref, width in ((gla_ref, GLA_W), (hgrn_ref, HGRN_W), (swa_ref, SWA_W), (rwkv_ref, RWKV_W)):
        ref[...] = _dot(xb, w_ref[:, o:o + width])
        o += width


def _in_proj(x, w):
    t = x.shape[0]
    widths = (GLA_W, HGRN_W, SWA_W, RWKV_W)
    return pl.pallas_call(
        _inproj_kernel,
        grid=(t // WIDE_ROW_TILE,),
        in_specs=[pl.BlockSpec((WIDE_ROW_TILE, D_MODEL), lambda i: (i, 0)), _resident((D_MODEL, Z_W))],
        out_specs=[pl.BlockSpec((WIDE_ROW_TILE, w_), lambda i: (i, 0)) for w_ in widths],
        out_shape=[jax.ShapeDtypeStruct((t, w_), F32) for w_ in widths],
        compiler_params=_params(("parallel",)),
        name="in_proj",
    )(x, w)


def _group_in_weights(w_in, vres_down):
    gla, hgrn, swa, rwkv = jnp.split(w_in, (784, 784 + 1024, 784 + 1024 + 512), axis=1)
    if vres_down is not None:
        rwkv = jnp.concatenate([rwkv, vres_down], axis=1)
    pad = lambda a, w_: jnp.pad(a, ((0, 0), (0, w_ - a.shape[1])))
    return jnp.concatenate([pad(gla, GLA_W), hgrn, swa, pad(rwkv, RWKV_W)], axis=1).astype(BF16)


class _MixerParts(NamedTuple):
    plan: Callable
    args: list
    in_specs: list
    out_specs: list
    out_shape: list
    scratch: list


def _mixers_kernel(layout, *refs):
    n_in = sum(entry[1] for entry in layout)
    n_out = sum(entry[2] for entry in layout)
    ins, outs, scratch = list(refs[:n_in]), list(refs[n_in:n_in + n_out]), list(refs[n_in + n_out:])

    @pl.when(pl.program_id(0) == 0)
    def _():
        for ref in scratch:
            ref[...] = jnp.zeros_like(ref)

    plans = []
    for plan, n_i, n_o, n_s in layout:
        plans.append(plan(*ins[:n_i], *outs[:n_o], *scratch[:n_s]))
        del ins[:n_i], outs[:n_o], scratch[:n_s]
    _run_plans(plans)


def _call_mixers(parts_list, tokens, step):
    layout = tuple((p.plan, len(p.args), len(p.out_shape), len(p.scratch)) for p in parts_list)
    flat = lambda field: [item for p in parts_list for item in getattr(p, field)]
    return pl.pallas_call(
        functools.partial(_mixers_kernel, layout),
        grid=(tokens // step,),
        in_specs=flat("in_specs"),
        out_specs=flat("out_specs"),
        out_shape=flat("out_shape"),
        scratch_shapes=flat("scratch"),
        compiler_params=_params(("arbitrary",)),
        name="mixers",
    )(*flat("args"))


def _mix_out(tokens, step):
    return (pl.BlockSpec((step, GROUP_WIDTH), lambda i: (i, 0)),
            jax.ShapeDtypeStruct((tokens, GROUP_WIDTH), MIX_DTYPE))


def _gated_linear_attention_tile(index, q, k, v, log_f, state_box, out_box):
    length, kw = q.shape
    head_k = kw // N_HEADS
    n_sub = length // GLA_SUB
    row = _iota((length, length), 0)
    col = _iota((length, length), 1)
    same_sub = _idiv(row, GLA_SUB) == _idiv(col, GLA_SUB)
    m_local = jnp.where(same_sub & (col <= row), 1.0, 0.0).astype(F32)
    m_prev = jnp.where(_idiv(col, GLA_SUB) < _idiv(row, GLA_SUB), 1.0, 0.0).astype(F32)
    sums = _dot_mask(jnp.concatenate([m_local, m_prev], axis=0), log_f)
    yield
    b_local = sums[0:length]
    b_start = sums[length:2 * length]
    b_full = b_start + b_local
    q_local = q * jnp.exp(b_local)

    q_pos = _imod(_iota((N_HEADS * GLA_SUB, length), 0), GLA_SUB)
    s_pos = _iota((N_HEADS * GLA_SUB, length), 1)
    probs = []
    for c in range(n_sub):
        r0, r1 = c * GLA_SUB, (c + 1) * GLA_SUB
        expo = jnp.where(_iota((r1, kw), 0) < r0, b_start[r0:r0 + 1, :] - b_full[0:r1], -b_local[0:r1])
        k_ref = k[0:r1] * jnp.exp(expo)
        if r1 < length:
            k_ref = jnp.concatenate([k_ref, jnp.zeros((length - r1, kw), F32)], axis=0)
        q_heads = _expand_heads(q_local[r0:r1, :], head_k)
        s = _bdot_nt(q_heads, k_ref)
        probs.append(jnp.where(s_pos <= q_pos + r0, s, 0.0))
    b_total = b_full[length - 1:length, :]
    k_end = k * jnp.exp(b_total - b_full)
    upd = _dot_tn(v.astype(BF16), k_end.astype(BF16))
    yield
    o_heads = _bdot(jnp.concatenate(probs, axis=0), v)
    assert len(out_box) == index, "the previous tile must have replaced the state before it is read"
    state_t = state_box[0]
    o_state = _bdot_nt(q * jnp.exp(b_full), state_t)
    same_head = _idiv(_iota((GROUP_WIDTH, kw), 0), HEAD_DIM) == _idiv(_iota((GROUP_WIDTH, kw), 1), head_k)
    state_box[0] = state_t * jnp.exp(b_total) + jnp.where(same_head, upd, 0.0)
    yield
    v_head = _idiv(_iota((GLA_SUB, GROUP_WIDTH), 1), HEAD_DIM)
    rows = []
    for c in range(n_sub):
        base = c * N_HEADS * GLA_SUB
        acc = jnp.zeros((GLA_SUB, GROUP_WIDTH), F32)
        for h in range(N_HEADS):
            acc = acc + jnp.where(v_head == h, o_heads[base + h * GLA_SUB:base + (h + 1) * GLA_SUB, :], 0.0)
        rows.append(acc)
    out_box.append(jnp.concatenate(rows, axis=0) + o_state)


def _gated_linear_attention_plan(q, k, v, log_f, gate, norm_w, o_ref, state_ref):
    state_box = [state_ref[...]]
    tiles = []
    stages = []
    for n in range(q.shape[0] // GLA_TILE):
        rows = slice(n * GLA_TILE, (n + 1) * GLA_TILE)
        stages.append(_gated_linear_attention_tile(n, q[rows], k[rows], v[rows], log_f[rows], state_box, tiles))

    def finish():
        state_ref[...] = state_box[0]
        o = jnp.concatenate(tiles, axis=0)
        ms = _dot_hilo(o * o, _head_group_matrix(GROUP_WIDTH, HEAD_DIM, 1.0 / HEAD_DIM).astype(BF16))
        o_ref[...] = (o * lax.rsqrt(ms + RMS_EPS) * norm_w * _silu(gate)).astype(o_ref.dtype)

    return stages, GLA_STAGGER, finish


def _gla_plan(z_ref, gk_up_ref, gk_bias_ref, norm_w_ref, o_ref, state_ref):
    z = z_ref[...]
    q = z[:, 0:128] * (GLA_DK ** -0.5)
    k = z[:, 128:256]
    v = z[:, 256:512]
    g = z[:, 512:768]
    gate_in = _dot_3pass(z[:, 768:896], gk_up_ref[...]) + gk_bias_ref[...]
    log_f = _log_sigmoid(gate_in) * (1.0 / GLA_GATE_NORMALIZER)
    return _gated_linear_attention_plan(q, k, v, log_f, g, norm_w_ref[...], o_ref, state_ref)


def _hgrn_plan(z_ref, lb_ref, log_lb_ref, norm_w_ref, o_ref, state_ref):
    z = z_ref[...]
    q = _silu(z[:, 0:256])
    f = z[:, 256:512]
    v = z[:, 512:768]
    g = z[:, 768:1024]
    lb = lb_ref[...]
    a = log_lb_ref[...]
    e = jnp.exp(-jnp.abs(f))
    inv = 1.0 / (1.0 + e)
    c = jnp.log1p(-lb) + jnp.minimum(f, 0.0) - jnp.log(1.0 + e)
    log_f = jnp.maximum(a, c) + _softplus_neg_abs(a - c)
    k = (1.0 - lb) * jnp.where(f >= 0.0, e * inv, inv)
    return _gated_linear_attention_plan(q, k, v, log_f, g, norm_w_ref[...], o_ref, state_ref)


def _gla_parts(z, gk_up, gk_bias, norm_w, step):
    out_spec, out_shape = _mix_out(z.shape[0], step)
    gk_up_pad = jnp.zeros((LANES, N_HEADS * GLA_DK), F32).at[:GLA_GATE_RANK].set(gk_up)
    return _MixerParts(
        plan=_gla_plan,
        args=[z, gk_up_pad, gk_bias.reshape(1, -1), jnp.tile(norm_w, N_HEADS).reshape(1, -1)],
        in_specs=[pl.BlockSpec((step, GLA_W), lambda i: (i, 0)),
                  _resident((LANES, N_HEADS * GLA_DK)), _resident((1, N_HEADS * GLA_DK)),
                  _resident((1, GROUP_WIDTH))],
        out_specs=[out_spec], out_shape=[out_shape],
        scratch=[pltpu.VMEM((GROUP_WIDTH, N_HEADS * GLA_DK), F32)])


def _hgrn_parts(z, lb, norm_w, step):
    out_spec, out_shape = _mix_out(z.shape[0], step)
    return _MixerParts(
        plan=_hgrn_plan,
        args=[z, lb.reshape(1, -1), jnp.log(lb).reshape(1, -1), jnp.tile(norm_w, N_HEADS).reshape(1, -1)],
        in_specs=[pl.BlockSpec((step, HGRN_W), lambda i: (i, 0)),
                  _resident((1, GROUP_WIDTH)), _resident((1, GROUP_WIDTH)), _resident((1, GROUP_WIDTH))],
        out_specs=[out_spec], out_shape=[out_shape],
        scratch=[pltpu.VMEM((GROUP_WIDTH, GROUP_WIDTH), F32)])


def _swa_block(q, kw, vw, visible, sinks, out_box):
    n = q.shape[0]
    head_cols = lambda x, h: x[:, h * HEAD_DIM:(h + 1) * HEAD_DIM]
    v_lane_head = _idiv(_iota(vw.shape, 1), HEAD_DIM)
    outs = []
    for kv in range(N_HEADS // 2):
        heads = (2 * kv, 2 * kv + 1)
        scores = _bdot_nt(jnp.concatenate([head_cols(q, h) for h in heads], axis=0), head_cols(kw, kv))
        yield
        probs, sink_terms = [], []
        for half, h in enumerate(heads):
            s = jnp.where(visible, scores[half * n:(half + 1) * n], -jnp.inf)
            sink = sinks[:, h:h + 1]
            m = jnp.maximum(jnp.max(s, axis=-1, keepdims=True), sink)
            probs.append(jnp.exp(s - m))
            sink_terms.append(jnp.exp(sink - m))
        o = _bdot(jnp.concatenate(probs, axis=0), jnp.where(v_lane_head == kv, vw, 1.0))
        yield
        sums = head_cols(o, 1 - kv)[:, 0:1]
        outs += [head_cols(o, kv)[half * n:(half + 1) * n] / (sums[half * n:(half + 1) * n] + sink_terms[half])
                 for half in range(2)]
    out_box.append(jnp.concatenate(outs, axis=-1))


def _swa_plan(q_ref, k_ref, v_ref, kp_ref, vp_ref, sink_ref, o_ref):
    w = SWA_WINDOW
    has_prev = pl.program_id(0) > 0
    q = q_ref[...] * (HEAD_DIM ** -0.5)
    k_all = jnp.concatenate([kp_ref[...], k_ref[...]], axis=0)
    v_all = jnp.concatenate([vp_ref[...], v_ref[...]], axis=0)
    q_pos = _iota((w, 2 * w), 0) + w
    k_pos = _iota((w, 2 * w), 1)
    dist = q_pos - k_pos
    in_window = (dist >= 0) & (dist < w)
    sinks = sink_ref[...]
    blocks = []
    stages = []
    for b in range(q.shape[0] // w):
        visible = in_window if b > 0 else in_window & ((k_pos >= w) | has_prev)
        stages.append(_swa_block(q[b * w:(b + 1) * w], k_all[b * w:(b + 2) * w], v_all[b * w:(b + 2) * w],
                                 visible, sinks, blocks))

    def finish():
        o_ref[...] = jnp.concatenate(blocks, axis=0).astype(o_ref.dtype)

    return stages, SWA_STAGGER, finish


def _swa_parts(z, sinks, step):
    out_spec, out_shape = _mix_out(z.shape[0], step)
    w = SWA_WINDOW
    prev = lambda col: (lambda i: (jnp.maximum(i * (step // w) - 1, 0), col))
    return _MixerParts(
        plan=_swa_plan,
        args=[z, z, z, z, z, sinks.reshape(1, -1)],
        in_specs=[pl.BlockSpec((step, 256), lambda i: (i, 0)),
                  pl.BlockSpec((step, 128), lambda i: (i, 2)), pl.BlockSpec((step, 128), lambda i: (i, 3)),
                  pl.BlockSpec((w, 128), prev(2)), pl.BlockSpec((w, 128), prev(3)),
                  _resident((1, N_HEADS))],
        out_specs=[out_spec], out_shape=[out_shape], scratch=[])


def _rwkv_chunk(index, r, k, v, a_vec, b_vec, log_w, state_box, out_box):
    c = r.shape[0]
    width = r.shape[1]
    tri = jnp.where(_iota((c, c), 1) <= _iota((c, c), 0), 1.0, 0.0).astype(F32)
    p = _dot_mask(tri, log_w)
    yield
    p_total = p[c - 1:c, :]
    decay_in = jnp.exp(p)
    decay_out = jnp.exp(-p)
    decay_end = jnp.exp(p_total - p)
    a_in = a_vec * jnp.exp(p - log_w)
    r_in = r * decay_in
    b_out = b_vec * decay_out
    k_out = k * decay_out
    b_end = b_vec * decay_end
    k_end = k * decay_end

    t_pos = _iota((c, width), 0)
    assert width == N_HEADS * c
    s_pos = _imod(_iota((c, width), 1), c)
    strict = s_pos < t_pos
    incl = s_pos <= t_pos
    expand = lambda x: _expand_heads(x, HEAD_DIM)

    scores = _bdot_nt(jnp.concatenate([a_in, r_in], axis=0),
                      jnp.concatenate([expand(b_out), expand(k_out)], axis=0))
    yield
    a_ab = jnp.where(strict, scores[0:c, 0:width], 0.0)
    a_ak = jnp.where(strict, scores[0:c, width:2 * width], 0.0)
    a_rb = jnp.where(incl, scores[c:2 * c, 0:width], 0.0)
    a_rk = jnp.where(incl, scores[c:2 * c, width:2 * width], 0.0)

    t_inv = jnp.where(s_pos == t_pos, 1.0, 0.0) + a_ab
    from_v = _bdot(jnp.concatenate([a_ak, a_rk], axis=0), expand(v))
    x1 = from_v[0:c]
    y_from_v = from_v[c:2 * c]
    power = _bdot(a_ab, expand(a_ab))
    yield
    n_factors = (c - 1).bit_length()
    for _ in range(n_factors - 2):
        both = _bdot(jnp.concatenate([t_inv, power], axis=0), expand(power))
        t_inv = t_inv + both[0:c]
        power = both[c:2 * c]
        yield
    t_inv = t_inv + _bdot(t_inv, expand(power))
    yield
    sol = _bdot(t_inv, jnp.concatenate([expand(x1), expand(a_in)], axis=1))
    yield
    u0 = sol[:, 0:width]
    w_mat = sol[:, width:2 * width]

    assert len(out_box) == index, "the previous chunk must have replaced the state before it is read"
    state = state_box[0]
    from_state = _bdot_nt(jnp.concatenate([w_mat, r_in], axis=0), state)
    yield
    u = u0 + from_state[0:c]
    y = _bdot(a_rb, expand(u)) + y_from_v + from_state[c:2 * c]
    upd = _dot_tn(jnp.concatenate([u, v], axis=0).astype(BF16),
                  jnp.concatenate([b_end, k_end], axis=0).astype(BF16))
    same_head = _idiv(_iota((width, width), 0), HEAD_DIM) == _idiv(_iota((width, width), 1), HEAD_DIM)
    state_box[0] = state * jnp.exp(p_total) + jnp.where(same_head, upd, 0.0)
    out_box.append(y)


def _rwkv_plan(has_vres, *refs):
    if has_vres:
        (z_ref, zp_ref, vfirst_ref, mu_ref, w0_ref, wup_ref, a0_ref, aup_ref, gup_ref, kk_ref, ka_ref,
         rk_ref, lnw_ref, lnb_ref, v0_ref, vup_ref, o_ref, state_ref) = refs
    else:
        (z_ref, zp_ref, mu_ref, w0_ref, wup_ref, a0_ref, aup_ref, gup_ref, kk_ref, ka_ref,
         rk_ref, lnw_ref, lnb_ref, o_ref, vout_ref, state_ref) = refs
    step = pl.program_id(0)
    z = z_ref[...]
    last_prev = jnp.where(step > 0, zp_ref[7:8, :], 0.0)
    prev = jnp.where(_iota(z.shape, 0) == 0, last_prev, pltpu.roll(z, 1, axis=0))
    zr = z + (prev - z) * mu_ref[...]
    r = zr[:, 0:256]
    k = zr[:, 256:512]
    v = zr[:, 512:768]
    low = zr[:, 768:896]
    w_pre = w0_ref[...] + _dot_3pass(jnp.tanh(low), wup_ref[...])
    w_log = -(jnp.maximum(-w_pre, 0.0) + _softplus_neg_abs(w_pre)) - 0.5
    log_w = -jnp.exp(w_log)
    a = _sigmoid(a0_ref[...] + _dot_3pass(low, aup_ref[...]))
    g = _dot_3pass(_sigmoid(low), gup_ref[...])
    if has_vres:
        v = v + (vfirst_ref[...] - v) * _sigmoid(v0_ref[...] + _dot_3pass(low, vup_ref[...]))
    else:
        vout_ref[...] = v
    head_sum = _head_group_matrix(GROUP_WIDTH, HEAD_DIM, 1.0).astype(BF16)
    kk = k * kk_ref[...]
    kk = kk / jnp.maximum(jnp.sqrt(_dot_hilo(kk * kk, head_sum)), 1e-12)
    k = k * (1.0 + (a - 1.0) * ka_ref[...])
    a_vec = -kk
    b_vec = kk * a

    c = RWKV_CHUNK
    state_box = [state_ref[...]]
    chunks = []
    stages = []
    for n in range(z.shape[0] // c):
        rows = slice(n * c, (n + 1) * c)
        stages.append(_rwkv_chunk(n, r[rows], k[rows], v[rows], a_vec[rows], b_vec[rows], log_w[rows],
                                  state_box, chunks))

    def finish():
        state_ref[...] = state_box[0]
        y = jnp.concatenate(chunks, axis=0)
        head_mean = _head_group_matrix(GROUP_WIDTH, HEAD_DIM, 1.0 / HEAD_DIM).astype(BF16)
        mu_y = _dot_hilo(y, head_mean)
        d = y - mu_y
        var_y = _dot_hilo(d * d, head_mean)
        y = d * lax.rsqrt(var_y + RWKV_GN_EPS) * lnw_ref[...] + lnb_ref[...]
        bonus = _dot_hilo(r * k * rk_ref[...], head_sum) * v
        o_ref[...] = ((y + bonus) * g).astype(o_ref.dtype)

    return stages, RWKV_STAGGER, finish


def _rwkv_parts(z, mu, w0, w_up, a0, a_up, g_up, k_k, k_a, r_k, lnx_w, lnx_b, vres, step):
    t = z.shape[0]
    c = step
    row = lambda a: a.reshape(1, -1)
    low_rows = lambda a, start: jnp.zeros((LANES, GROUP_WIDTH), F32).at[start:start + a.shape[0]].set(a)
    has_vres = vres is not None
    mu_full = jnp.zeros((RWKV_W,), F32).at[:RWKV_COLS].set(mu)
    vec = _resident((1, GROUP_WIDTH))
    mat = _resident((LANES, GROUP_WIDTH))
    tile = pl.BlockSpec((c, GROUP_WIDTH), lambda i: (i, 0))
    z_specs = [pl.BlockSpec((c, RWKV_W), lambda i: (i, 0)),
               pl.BlockSpec((8, RWKV_W), lambda i: (jnp.maximum(i * (c // 8) - 1, 0), 0))]
    common = [row(w0), low_rows(w_up, 0), row(a0), low_rows(a_up, 16), low_rows(g_up, 32),
              row(k_k), row(k_a), row(r_k), row(lnx_w), row(lnx_b)]
    common_specs = [vec, mat, vec, mat, mat, vec, vec, vec, vec, vec]
    if has_vres:
        v_first, vres_mu, v0, v_up = vres
        mu_full = mu_full.at[RWKV_COLS:RWKV_COLS + RWKV_V_RANK].set(vres_mu)
        args = [z, z, v_first, row(mu_full)] + common + [row(v0), low_rows(v_up, 64)]
        in_specs = z_specs + [tile, _resident((1, RWKV_W))] + common_specs + [vec, mat]
        out_specs = [tile]
        out_shape = [jax.ShapeDtypeStruct((t, GROUP_WIDTH), MIX_DTYPE)]
    else:
        args = [z, z, row(mu_full)] + common
        in_specs = z_specs + [_resident((1, RWKV_W))] + common_specs
        out_specs = [tile, tile]
        out_shape = [jax.ShapeDtypeStruct((t, GROUP_WIDTH), MIX_DTYPE), jax.ShapeDtypeStruct((t, GROUP_WIDTH), F32)]
    return _MixerParts(plan=functools.partial(_rwkv_plan, has_vres), args=args, in_specs=in_specs,
                       out_specs=out_specs, out_shape=out_shape,
                       scratch=[pltpu.VMEM((GROUP_WIDTH, GROUP_WIDTH), F32)])


def _store_token_tiles(ref, index, x):
    n = x.shape[0]
    for j in range(SUBLANES):
        ref[(*index, pl.ds(j, n, stride=SUBLANES), slice(None))] = x[:, j * LANES:(j + 1) * LANES]


def _load_token_tiles(ref, index, n):
    return jnp.concatenate([ref[(*index, pl.ds(j, n, stride=SUBLANES), slice(None))] for j in range(SUBLANES)],
                           axis=-1)


def _top2_route(logits):
    row = _iota(logits.shape, 0).astype(F32)
    m1 = jnp.max(logits, axis=0, keepdims=True)
    i1 = jnp.min(jnp.where(logits == m1, row, N_EXPERTS), axis=0, keepdims=True)
    rest = jnp.where(row == i1, -jnp.inf, logits)
    m2 = jnp.max(rest, axis=0, keepdims=True)
    i2 = jnp.min(jnp.where(rest == m2, row, N_EXPERTS), axis=0, keepdims=True)
    e2 = jnp.exp(m2 - m1)
    g1 = 1.0 / (1.0 + e2)
    g2 = e2 * g1
    route = jnp.where(row == 0, i1, jnp.where(row == 1, i2, jnp.where(row == 2, g1, jnp.where(row == 3, g2, 0.0))))
    chosen = jnp.where(row == i1, 1.0, 0.0) + jnp.where(row == i2, 1.0, 0.0)
    return route, jnp.sum(chosen, axis=1, keepdims=True)


def _mix_residual_norm(alpha, x_ref, mix_refs, w_ref, g_ref, b_ref):
    acc = alpha * x_ref[...]
    for h, ref in enumerate(mix_refs):
        acc = acc + _dot(ref[...], w_ref[h * GROUP_WIDTH:(h + 1) * GROUP_WIDTH, :])
    return _layer_norm(acc, g_ref[...], b_ref[...])


def _outproj_route_kernel(alpha, x_ref, o0_ref, o1_ref, o2_ref, o3_ref, w_ref, g_ref, b_ref, router_ref,
                          tiles_ref, route_ref, counts_ref):
    y = _mix_residual_norm(alpha, x_ref, (o0_ref, o1_ref, o2_ref, o3_ref), w_ref, g_ref, b_ref)
    route, counts = _top2_route(_dot_3pass_nt(router_ref[...], y))
    route_ref[...] = route
    counts_ref[...] = jnp.broadcast_to(counts, counts_ref.shape)
    _store_token_tiles(tiles_ref, (), y)


def _out_proj_ln_route(alpha, x, mixes, w_out, g, b, router):
    t = x.shape[0]
    n = WIDE_ROW_TILE
    row_d = pl.BlockSpec((n, D_MODEL), lambda i: (i, 0))
    row_g = pl.BlockSpec((n, GROUP_WIDTH), lambda i: (i, 0))
    tiles, route, tile_counts = pl.pallas_call(
        functools.partial(_outproj_route_kernel, alpha),
        grid=(t // n,),
        in_specs=[row_d, row_g, row_g, row_g, row_g, _resident((D_MODEL, D_MODEL)),
                  _resident((1, D_MODEL)), _resident((1, D_MODEL)), _resident((N_EXPERTS, D_MODEL))],
        out_specs=[pl.BlockSpec((n * SUBLANES, LANES), lambda i: (i, 0)),
                   pl.BlockSpec((N_EXPERTS, n), lambda i: (0, i)),
                   pl.BlockSpec((N_EXPERTS, LANES), lambda i: (0, i))],
        out_shape=[jax.ShapeDtypeStruct((t * SUBLANES, LANES), F32), jax.ShapeDtypeStruct((N_EXPERTS, t), F32),
                   jax.ShapeDtypeStruct((N_EXPERTS, (t // n) * LANES), F32)],
        compiler_params=_params(("parallel",)),
        name="out_proj_ln_route",
    )(x, *mixes, w_out.astype(BF16), g.reshape(1, -1), b.reshape(1, -1), router.T)
    counts = jnp.sum(tile_counts[:, ::LANES], axis=1).astype(jnp.int32)
    return tiles, route, counts


def _ln_embed(y, ln_g, ln_b, p, ple_gate, ple_proj):
    x = _layer_norm(y, ln_g, ln_b)
    gate = _sigmoid(_dot(x.astype(BF16), ple_gate))
    return x + gate * _dot(p.astype(BF16), ple_proj)


def _dense_ffn_kernel(alpha, x_ref, o0_ref, o1_ref, o2_ref, o3_ref, wo_ref, g1_ref, b1_ref, p_ref, wg_ref, wu_ref,
                      wd_ref, g_ref, b_ref, pg_ref, pp_ref, y_ref, acc_ref):
    x = _mix_residual_norm(alpha, x_ref, (o0_ref, o1_ref, o2_ref, o3_ref), wo_ref, g1_ref, b1_ref)
    xb = x.astype(BF16)
    acc_ref[...] = alpha * x
    for j in range(D_FF // FF_CHUNK):
        cols = slice(j * FF_CHUNK, (j + 1) * FF_CHUNK)
        h = _silu(_dot(xb, wg_ref[:, cols])) * _dot(xb, wu_ref[:, cols])
        acc_ref[...] += _dot(h.astype(BF16), wd_ref[cols, :])
    y_ref[...] = _ln_embed(acc_ref[...], g_ref[...], b_ref[...], p_ref[...], pg_ref[...], pp_ref[...])


def _dense_layer_tail(alpha, x, mixes, w_out, g1, b1, p_all, p_block, w_gate, w_up, w_down, g, b, ple_gate, ple_proj):
    t = x.shape[0]
    row_d = pl.BlockSpec((ROW_TILE, D_MODEL), lambda i: (i, 0))
    row_g = pl.BlockSpec((ROW_TILE, GROUP_WIDTH), lambda i: (i, 0))
    return pl.pallas_call(
        functools.partial(_dense_ffn_kernel, alpha),
        grid=(t // ROW_TILE,),
        in_specs=[row_d, row_g, row_g, row_g, row_g, _resident((D_MODEL, D_MODEL)),
                  _resident((1, D_MODEL)), _resident((1, D_MODEL)),
                  pl.BlockSpec((ROW_TILE, PLE_DIM), lambda i: (p_block + i, 0)),
                  _resident((D_MODEL, D_FF)), _resident((D_MODEL, D_FF)), _resident((D_FF, D_MODEL)),
                  _resident((1, D_MODEL)), _resident((1, D_MODEL)),
                  _resident((D_MODEL, D_MODEL)), _resident((PLE_DIM, D_MODEL))],
        out_specs=row_d,
        out_shape=jax.ShapeDtypeStruct((t, D_MODEL), F32),
        scratch_shapes=[pltpu.VMEM((ROW_TILE, D_MODEL), F32)],
        compiler_params=_params(("parallel",)),
        name="dense_ffn_tail",
    )(x, *mixes, w_out.astype(BF16), g1.reshape(1, -1), b1.reshape(1, -1), p_all,
      w_gate.astype(BF16), w_up.astype(BF16), w_down.astype(BF16), g.reshape(1, -1), b.reshape(1, -1),
      ple_gate.astype(BF16), ple_proj.astype(BF16))


def _weight_group_copies(e, group, wg_hbm, wu_hbm, wd_hbm, stage_cols_ref, stage_rows_ref, sem):
    slot = group % 2
    lo, hi = group * MOE_WEIGHT_CHUNK, (group + 1) * MOE_WEIGHT_CHUNK
    return (pltpu.make_async_copy(wg_hbm.at[e, :, lo:hi], stage_cols_ref.at[slot, 0], sem.at[slot]),
            pltpu.make_async_copy(wu_hbm.at[e, :, lo:hi], stage_cols_ref.at[slot, 1], sem.at[slot]),
            pltpu.make_async_copy(wd_hbm.at[e, lo:hi, :], stage_rows_ref.at[slot], sem.at[slot]))


def _expert_kernel(row_tok_ref, row_dst_ref, block_e_ref, n_used_ref, x_hbm, wg_hbm, wu_hbm, wd_hbm, y_hbm,
                   rows_ref, xb_ref, acc_ref, ybuf_ref, wg_ref, wu_ref, wd_ref, stage_cols_ref, stage_rows_ref,
                   gather_sem, scatter_sem, weight_sem):
    i = pl.program_id(0)
    n_used = n_used_ref[0]
    last = pl.num_programs(0) - 1
    n = MOE_ROWS
    slot = lax.rem(i, 2)
    other = 1 - slot
    tile = lambda first_row: pl.ds(pl.multiple_of(first_row, SUBLANES), SUBLANES)
    gather_row = lambda tok_row, s, r: pltpu.make_async_copy(
        x_hbm.at[tile(tok_row)], rows_ref.at[s, tile(r * SUBLANES)], gather_sem.at[s])
    scatter_row = lambda dst_row, s, r: pltpu.make_async_copy(
        ybuf_ref.at[s, tile(r * SUBLANES)], y_hbm.at[tile(dst_row)], scatter_sem.at[s])
    block_rows = n * SUBLANES
    gather_block = lambda s: pltpu.make_async_copy(x_hbm.at[pl.ds(0, block_rows)], rows_ref.at[s], gather_sem.at[s])
    scatter_block = lambda s: pltpu.make_async_copy(ybuf_ref.at[s], y_hbm.at[pl.ds(0, block_rows)],
                                                    scatter_sem.at[s])

    @pl.when(i == 0)
    def _():
        ybuf_ref[1] = jnp.zeros((block_rows, LANES), F32)

        def start(r, carry):
            gather_row(row_tok_ref[r], 0, r).start()
            return carry

        lax.fori_loop(0, n, start, 0)

    def scatter_all(first_dst, s):
        def start(r, carry):
            scatter_row(row_dst_ref[first_dst + r], s, r).start()
            return carry

        lax.fori_loop(0, n, start, 0)
        scatter_block(s).wait()

    expert = block_e_ref[i]
    new_expert = (i == 0) | (expert != block_e_ref[jnp.maximum(i - 1, 0)])

    weight_group = lambda g: _weight_group_copies(expert, g, wg_hbm, wu_hbm, wd_hbm, stage_cols_ref,
                                                  stage_rows_ref, weight_sem)
    chunks_per_group = MOE_WEIGHT_CHUNK // FF_CHUNK
    n_groups = D_FF_EXPERT // MOE_WEIGHT_CHUNK

    def take_weight_group(g):
        if g + 1 < n_groups:
            for copy in weight_group(g + 1):
                copy.start()
        for copy in weight_group(g):
            copy.wait()
        piece = slice(g * MOE_WEIGHT_CHUNK, (g + 1) * MOE_WEIGHT_CHUNK)
        wg_ref[:, piece] = stage_cols_ref[g % 2, 0].astype(BF16)
        wu_ref[:, piece] = stage_cols_ref[g % 2, 1].astype(BF16)
        wd_ref[piece, :] = stage_rows_ref[g % 2].astype(BF16)

    @pl.when(i < n_used)
    def _():
        @pl.when(new_expert)
        def _():
            for copy in weight_group(0):
                copy.start()

        gather_block(slot).wait()
        xb_ref[...] = _load_token_tiles(rows_ref, (slot,), n).astype(BF16)
        n_chunks = D_FF_EXPERT // FF_CHUNK
        rows_per_chunk = -(-n // (n_chunks - MOE_DMA_FREE_CHUNKS))
        for j in range(n_chunks):
            if j % chunks_per_group == 0:
                pl.when(new_expert)(functools.partial(take_weight_group, j // chunks_per_group))
            cols = slice(j * FF_CHUNK, (j + 1) * FF_CHUNK)
            xb = xb_ref[...]
            h = _silu(_dot(xb, wg_ref[:, cols])) * _dot(xb, wu_ref[:, cols])
            part = _dot(h.astype(BF16), wd_ref[cols, :])
            if j == 0:
                acc_ref[...] = part
            else:
                acc_ref[...] += part
            for r in range(j * rows_per_chunk, min((j + 1) * rows_per_chunk, n)):
                gather_row(row_tok_ref[(i + 1) * n + r], other, r).start(priority=GATHER_DMA_PRIORITY)
                scatter_row(row_dst_ref[i * n + r], other, r).start(priority=SCATTER_DMA_PRIORITY)
        _store_token_tiles(ybuf_ref, (slot,), acc_ref[...])
        scatter_block(other).wait()

    @pl.when(i == n_used)
    def _():
        gather_block(slot).wait()
        scatter_all(i * n, other)

    @pl.when(i >= n_used)
    def _():
        rows_ref[slot] = jnp.zeros((block_rows, LANES), F32)
        fill = pltpu.make_async_copy(rows_ref.at[slot], y_hbm.at[pl.ds(i * block_rows, block_rows)],
                                     scatter_sem.at[slot])
        fill.start()
        fill.wait()

    @pl.when((i == last) & (i < n_used))
    def _():
        gather_block(other).wait()
        scatter_all((i + 1) * n, slot)


def _expert_rows(x, row_tok, row_dst, block_e, n_used, n_out_rows, w_gate, w_up, w_down):
    n_blocks = block_e.shape[0]
    hbm = pl.BlockSpec(memory_space=pl.ANY)
    grid_spec = pltpu.PrefetchScalarGridSpec(
        num_scalar_prefetch=4,
        grid=(n_blocks,),
        in_specs=[hbm, hbm, hbm, hbm],
        out_specs=hbm,
        scratch_shapes=[pltpu.VMEM((2, MOE_ROWS * SUBLANES, LANES), F32), pltpu.VMEM((MOE_ROWS, D_MODEL), BF16),
                        pltpu.VMEM((MOE_ROWS, D_MODEL), F32), pltpu.VMEM((2, MOE_ROWS * SUBLANES, LANES), F32),
                        pltpu.VMEM((D_MODEL, D_FF_EXPERT), BF16), pltpu.VMEM((D_MODEL, D_FF_EXPERT), BF16),
                        pltpu.VMEM((D_FF_EXPERT, D_MODEL), BF16),
                        pltpu.VMEM((2, 2, D_MODEL, MOE_WEIGHT_CHUNK), F32),
                        pltpu.VMEM((2, MOE_WEIGHT_CHUNK, D_MODEL), F32),
                        pltpu.SemaphoreType.DMA((2,)), pltpu.SemaphoreType.DMA((2,)),
                        pltpu.SemaphoreType.DMA((2,))],
    )
    return pl.pallas_call(
        _expert_kernel,
        grid_spec=grid_spec,
        out_shape=jax.ShapeDtypeStruct((n_out_rows * SUBLANES, LANES), F32),
        compiler_params=_params(("arbitrary",)),
        name="moe_experts",
    )(row_tok, row_dst, block_e, n_used.reshape(1), x, w_gate, w_up, w_down)


def _combine_kernel(alpha, x_ref, y0_ref, y1_ref, gates_ref, p_ref, g_ref, b_ref, pg_ref, pp_ref, o_ref):
    n = WIDE_ROW_TILE
    gates = gates_ref[...]
    f = _load_token_tiles(y0_ref, (), n) * gates[:, 0:1] + _load_token_tiles(y1_ref, (), n) * gates[:, 1:2]
    o_ref[...] = _ln_embed(alpha * _load_token_tiles(x_ref, (), n) + f, g_ref[...], b_ref[...], p_ref[...],
                           pg_ref[...], pp_ref[...])


def _moe_combine_tail(alpha, x_tiles, p_all, p_block, y_tiles, gates, g, b, ple_gate, ple_proj):
    t = x_tiles.shape[0] // SUBLANES
    n = WIDE_ROW_TILE
    row_d = pl.BlockSpec((n, D_MODEL), lambda i: (i, 0))
    tiles_d = pl.BlockSpec((n * SUBLANES, LANES), lambda i: (i, 0))
    return pl.pallas_call(
        functools.partial(_combine_kernel, alpha),
        grid=(t // n,),
        in_specs=[tiles_d, tiles_d, pl.BlockSpec((n * SUBLANES, LANES), lambda i: (t // n + i, 0)),
                  pl.BlockSpec((n, 2), lambda i: (i, 0)), pl.BlockSpec((n, PLE_DIM), lambda i: (p_block + i, 0)),
                  _resident((1, D_MODEL)), _resident((1, D_MODEL)),
                  _resident((D_MODEL, D_MODEL)), _resident((PLE_DIM, D_MODEL))],
        out_specs=row_d,
        out_shape=jax.ShapeDtypeStruct((t, D_MODEL), F32),
        compiler_params=_params(("parallel",)),
        name="moe_combine_tail",
    )(x_tiles, y_tiles, y_tiles, gates, p_all, g.reshape(1, -1), b.reshape(1, -1), ple_gate.astype(BF16),
      ple_proj.astype(BF16))


def _moe_tail(alpha, x_tiles, route, counts, p_all, p_block, w_gate, w_up, w_down, g, b, ple_gate, ple_proj):
    t = route.shape[1]
    experts = route[0:2].T.astype(jnp.int32)
    gates = route[2:4].T
    e_flat = experts.reshape(-1)
    padded = (counts + MOE_ROWS - 1) // MOE_ROWS * MOE_ROWS
    pad_end = jnp.cumsum(padded)
    pad_start = pad_end - padded
    n_blocks = (2 * t) // MOE_ROWS + N_EXPERTS
    n_rows = n_blocks * MOE_ROWS
    block_start = jnp.arange(n_blocks, dtype=jnp.int32) * MOE_ROWS
    block_e = jnp.minimum(jnp.sum((block_start[:, None] >= pad_end[None, :]).astype(jnp.int32), axis=1),
                          N_EXPERTS - 1)
    order = jnp.argsort(e_flat, stable=True).astype(jnp.int32)
    seg_end = jnp.cumsum(counts)
    seg_start = seg_end - counts
    per_row = lambda per_expert: jnp.repeat(per_expert[block_e], MOE_ROWS)
    place = jnp.arange(n_rows, dtype=jnp.int32) - per_row(pad_start - seg_start)
    used = place < per_row(seg_end)
    row_assign = jnp.where(used, order[jnp.clip(place, 0, 2 * t - 1)], -1)
    spare = 2 * t + jnp.cumsum(jnp.where(used, 0, 1).astype(jnp.int32)) - 1
    row_tok = jnp.where(used, row_assign // 2, 0)
    row_dst = jnp.where(used, (row_assign % 2) * t + row_assign // 2, spare)
    n_spare = n_rows - 2 * t
    first_dst = 2 * t + n_spare + jnp.arange(MOE_ROWS, dtype=jnp.int32)
    row_tok = jnp.concatenate([row_tok, jnp.zeros((MOE_ROWS,), jnp.int32)])
    row_dst = jnp.concatenate([first_dst, row_dst])
    n_used = (pad_end[-1] // MOE_ROWS).astype(jnp.int32)
    y_tiles = _expert_rows(x_tiles, row_tok * SUBLANES, row_dst * SUBLANES, block_e, n_used,
                           2 * t + n_spare + MOE_ROWS, w_gate, w_up, w_down)
    return _moe_combine_tail(alpha, x_tiles, p_all, p_block, y_tiles, gates, g, b, ple_gate, ple_proj)


def kernel(x, p, w_in, w_out, gla_gk_up, gla_gk_bias, gla_norm_w, hgrn_lower_bounds, hgrn_norm_w, swa_sinks, rwkv_mu, rwkv_w0, rwkv_w_up, rwkv_a0, rwkv_a_up, rwkv_g_up, rwkv_k_k, rwkv_k_a, rwkv_r_k, rwkv_lnx_w, rwkv_lnx_b, rwkv_vres_down, rwkv_vres_mu, rwkv_v0, rwkv_vres_up, ln1_g, ln1_b, ln2_g, ln2_b, ffn_w_gate, ffn_w_up, ffn_w_down, moe_router, moe_w_gate, moe_w_up, moe_w_down, ple_proj, ple_gate):
    bsz, seq, d = x.shape
    depth = w_in.shape[0]
    alpha = (2.0 * depth) ** 0.25
    lbs = jnp.cumsum(jax.nn.softmax(hgrn_lower_bounds.astype(F32), axis=0), axis=0)
    lbs = lbs - lbs[0]
    p_all = p.reshape(-1, PLE_DIM)
    outs = []
    for bi in range(bsz):
        xt = x[bi]
        v_first = None
        for i in range(depth):
            w = _group_in_weights(w_in[i], None if i == 0 else rwkv_vres_down[i - 1])
            z_gla, z_hgrn, z_swa, z_rwkv = _in_proj(xt, w)
            vres = None if i == 0 else (v_first, rwkv_vres_mu[i - 1], rwkv_v0[i - 1], rwkv_vres_up[i - 1])
            mixed = _call_mixers(
                [_gla_parts(z_gla, gla_gk_up[i], gla_gk_bias[i], gla_norm_w[i], MIX_STEP),
                 _hgrn_parts(z_hgrn, lbs[i], hgrn_norm_w[i], MIX_STEP),
                 _swa_parts(z_swa, swa_sinks[i], MIX_STEP),
                 _rwkv_parts(z_rwkv, rwkv_mu[i], rwkv_w0[i], rwkv_w_up[i], rwkv_a0[i], rwkv_a_up[i],
                             rwkv_g_up[i], rwkv_k_k[i], rwkv_k_a[i], rwkv_r_k[i].reshape(-1),
                             rwkv_lnx_w[i], rwkv_lnx_b[i], vres, MIX_STEP)],
                seq, MIX_STEP)
            mixes = tuple(mixed[0:4])
            if i == 0:
                v_first = mixed[4]
            j = i // 2
            p_row = (i * bsz + bi) * seq
            if i % 2 == 0:
                xt = _dense_layer_tail(alpha, xt, mixes, w_out[i], ln1_g[i], ln1_b[i], p_all, p_row // ROW_TILE,
                                       ffn_w_gate[j], ffn_w_up[j], ffn_w_down[j], ln2_g[i], ln2_b[i],
                                       ple_gate[i], ple_proj[i])
            else:
                x_tiles, route, counts = _out_proj_ln_route(alpha, xt, mixes, w_out[i], ln1_g[i], ln1_b[i],
                                                            moe_router[j])
                xt = _moe_tail(alpha, x_tiles, route, counts, p_all, p_row // WIDE_ROW_TILE, moe_w_gate[j],
                               moe_w_up[j], moe_w_down[j], ln2_g[i], ln2_b[i], ple_gate[i], ple_proj[i])
        outs.append(xt)
    return jnp.stack(outs, axis=0)
```

```python
import functools
from typing import Callable, NamedTuple

import jax
import jax.numpy as jnp
from jax import lax
from jax.experimental import pallas as pl
from jax.experimental.pallas import tpu as pltpu

F32 = jnp.float32
BF16 = jnp.bfloat16
MIX_DTYPE = BF16

D_MODEL = 1024
GROUP_WIDTH = 256
N_HEADS = 4
HEAD_DIM = 64
GLA_DK = 32
GLA_GATE_RANK = 16
GLA_GATE_NORMALIZER = 16.0
SWA_WINDOW = 128
RWKV_COLS = 3 * GROUP_WIDTH + 16 + 16 + 32
RWKV_V_RANK = 8
LN_EPS = 1e-5
RMS_EPS = 1e-6
RWKV_GN_EPS = 64e-5
D_FF = 2816
N_EXPERTS = 8
D_FF_EXPERT = 3584
PLE_DIM = 256

LANES = 128
SUBLANES = 8
GLA_W = 896
HGRN_W = 1024
SWA_W = 512
RWKV_W = 896
Z_W = GLA_W + HGRN_W + SWA_W + RWKV_W

GLA_SUB = 16
GLA_TILE = 128
GLA_STAGGER = 5
SWA_STAGGER = 5
MIX_STEP = 1024
RWKV_CHUNK = 64
RWKV_STAGGER = 2
ROW_TILE = 512
WIDE_ROW_TILE = 1024
FF_CHUNK = 256
MOE_ROWS = 512
MOE_WEIGHT_CHUNK = 512
MOE_FF_CHUNK = 512
MOE_DMA_FREE_CHUNKS = 5
GATHER_DMA_PRIORITY = 0
SCATTER_DMA_PRIORITY = 1
VMEM_LIMIT = 56 * 1024 * 1024


def _iota(shape, dim):
    return lax.broadcasted_iota(jnp.int32, shape, dim)


def _idiv(x, n):
    return jnp.right_shift(x, n.bit_length() - 1)


def _imod(x, n):
    return jnp.bitwise_and(x, n - 1)


def _dot(a, b):
    return jnp.dot(a, b, preferred_element_type=F32)


def _dot_nt(a, b):
    return lax.dot_general(a, b, (((1,), (1,)), ((), ())), preferred_element_type=F32)


def _dot_tn(a, b):
    return lax.dot_general(a, b, (((0,), (0,)), ((), ())), preferred_element_type=F32)


def _bdot(a, b):
    return _dot(a.astype(BF16), b.astype(BF16))


def _bdot_nt(a, b):
    return _dot_nt(a.astype(BF16), b.astype(BF16))


def _dot_hilo(x, m):
    hi = x.astype(BF16)
    lo = (x - hi.astype(F32)).astype(BF16)
    return _dot(hi, m) + _dot(lo, m)


def _dot_mask(m, x):
    m = m.astype(BF16)
    x1 = x.astype(BF16)
    r1 = x - x1.astype(F32)
    x2 = r1.astype(BF16)
    x3 = (r1 - x2.astype(F32)).astype(BF16)
    return _dot(m, x1) + _dot(m, x2) + _dot(m, x3)


def _dot_3pass_nt(x, w):
    x_hi = x.astype(BF16)
    x_lo = (x - x_hi.astype(F32)).astype(BF16)
    w_hi = w.astype(BF16)
    w_lo = (w - w_hi.astype(F32)).astype(BF16)
    return _dot_nt(x_hi, w_hi) + _dot_nt(x_lo, w_hi) + _dot_nt(x_hi, w_lo)


def _dot_3pass(x, w):
    x_hi = x.astype(BF16)
    x_lo = (x - x_hi.astype(F32)).astype(BF16)
    w_hi = w.astype(BF16)
    w_lo = (w - w_hi.astype(F32)).astype(BF16)
    return _dot(x_hi, w_hi) + _dot(x_lo, w_hi) + _dot(x_hi, w_lo)


def _sigmoid(x):
    return 1.0 / (1.0 + jnp.exp(-x))


def _silu(x):
    return x * _sigmoid(x)


def _softplus_neg_abs(x):
    return jnp.log(1.0 + jnp.exp(-jnp.abs(x)))


def _log_sigmoid(x):
    return jnp.minimum(x, 0.0) - _softplus_neg_abs(x)


def _layer_norm(y, g, b):
    mu = jnp.mean(y, axis=-1, keepdims=True)
    d = y - mu
    var = jnp.mean(d * d, axis=-1, keepdims=True)
    return d * lax.rsqrt(var + LN_EPS) * g + b


def _expand_heads(x, head_width):
    lane_head = _idiv(_iota(x.shape, 1), head_width)
    return jnp.concatenate([jnp.where(lane_head == h, x, 0.0) for h in range(N_HEADS)], axis=0)


def _head_group_matrix(width, head_width, value):
    same = _idiv(_iota((width, width), 0), head_width) == _idiv(_iota((width, width), 1), head_width)
    return jnp.where(same, value, 0.0).astype(F32)


def _resident(shape):
    nd = len(shape)
    return pl.BlockSpec(shape, lambda *_: (0,) * nd, pipeline_mode=pl.Buffered(1))


def _params(semantics):
    return pltpu.CompilerParams(dimension_semantics=semantics, vmem_limit_bytes=VMEM_LIMIT)


def _run_plans(plans):
    live = [dict(enumerate(stages)) for stages, _, _ in plans]
    rnd = 0
    while any(live):
        for group, (_, stagger, _) in zip(live, plans):
            for n in sorted(group):
                if rnd >= n * stagger:
                    try:
                        next(group[n])
                    except StopIteration:
                        del group[n]
        rnd += 1
    for _, _, finish in plans:
        finish()


def _inproj_kernel(x_ref, w_ref, gla_ref, hgrn_ref, swa_ref, rwkv_ref):
    xb = x_ref[...].astype(BF16)
    o = 0
    for ref, width in ((gla_ref, GLA_W), (hgrn_ref, HGRN_W), (swa_ref, SWA_W), (rwkv_ref, RWKV_W)):
        ref[...] = _dot(xb, w_ref[:, o:o + width])
        o += width


def _in_proj(x, w):
    t = x.shape[0]
    widths = (GLA_W, HGRN_W, SWA_W, RWKV_W)
    return pl.pallas_call(
        _inproj_kernel,
        grid=(t // WIDE_ROW_TILE,),
        in_specs=[pl.BlockSpec((WIDE_ROW_TILE, D_MODEL), lambda i: (i, 0)), _resident((D_MODEL, Z_W))],
        out_specs=[pl.BlockSpec((WIDE_ROW_TILE, w_), lambda i: (i, 0)) for w_ in widths],
        out_shape=[jax.ShapeDtypeStruct((t, w_), F32) for w_ in widths],
        compiler_params=_params(("parallel",)),
        name="in_proj",
    )(x, w)


def _group_in_weights(w_in, vres_down):
    gla, hgrn, swa, rwkv = jnp.split(w_in, (784, 784 + 1024, 784 + 1024 + 512), axis=1)
    if vres_down is not None:
        rwkv = jnp.concatenate([rwkv, vres_down], axis=1)
    pad = lambda a, w_: jnp.pad(a, ((0, 0), (0, w_ - a.shape[1])))
    return jnp.concatenate([pad(gla, GLA_W), hgrn, swa, pad(rwkv, RWKV_W)], axis=1).astype(BF16)


class _MixerParts(NamedTuple):
    plan: Callable
    args: list
    in_specs: list
    out_specs: list
    out_shape: list
    scratch: list


def _mixers_kernel(layout, *refs):
    n_in = sum(entry[1] for entry in layout)
    n_out = sum(entry[2] for entry in layout)
    ins, outs, scratch = list(refs[:n_in]), list(refs[n_in:n_in + n_out]), list(refs[n_in + n_out:])

    @pl.when(pl.program_id(0) == 0)
    def _():
        for ref in scratch:
            ref[...] = jnp.zeros_like(ref)

    plans = []
    for plan, n_i, n_o, n_s in layout:
        plans.append(plan(*ins[:n_i], *outs[:n_o], *scratch[:n_s]))
        del ins[:n_i], outs[:n_o], scratch[:n_s]
    _run_plans(plans)


def _call_mixers(parts_list, tokens, step):
    layout = tuple((p.plan, len(p.args), len(p.out_shape), len(p.scratch)) for p in parts_list)
    flat = lambda field: [item for p in parts_list for item in getattr(p, field)]
    return pl.pallas_call(
        functools.partial(_mixers_kernel, layout),
        grid=(tokens // step,),
        in_specs=flat("in_specs"),
        out_specs=flat("out_specs"),
        out_shape=flat("out_shape"),
        scratch_shapes=flat("scratch"),
        compiler_params=_params(("arbitrary",)),
        name="mixers",
    )(*flat("args"))


def _mix_out(tokens, step):
    return (pl.BlockSpec((step, GROUP_WIDTH), lambda i: (i, 0)),
            jax.ShapeDtypeStruct((tokens, GROUP_WIDTH), MIX_DTYPE))


def _gated_linear_attention_tile(index, q, k, v, log_f, state_box, out_box):
    length, kw = q.shape
    head_k = kw // N_HEADS
    n_sub = length // GLA_SUB
    row = _iota((length, length), 0)
    col = _iota((length, length), 1)
    same_sub = _idiv(row, GLA_SUB) == _idiv(col, GLA_SUB)
    m_local = jnp.where(same_sub & (col <= row), 1.0, 0.0).astype(F32)
    m_prev = jnp.where(_idiv(col, GLA_SUB) < _idiv(row, GLA_SUB), 1.0, 0.0).astype(F32)
    sums = _dot_mask(jnp.concatenate([m_local, m_prev], axis=0), log_f)
    yield
    b_local = sums[0:length]
    b_start = sums[length:2 * length]
    b_full = b_start + b_local
    q_local = q * jnp.exp(b_local)

    q_pos = _imod(_iota((N_HEADS * GLA_SUB, length), 0), GLA_SUB)
    s_pos = _iota((N_HEADS * GLA_SUB, length), 1)
    probs = []
    for c in range(n_sub):
        r0, r1 = c * GLA_SUB, (c + 1) * GLA_SUB
        expo = jnp.where(_iota((r1, kw), 0) < r0, b_start[r0:r0 + 1, :] - b_full[0:r1], -b_local[0:r1])
        k_ref = k[0:r1] * jnp.exp(expo)
        if r1 < length:
            k_ref = jnp.concatenate([k_ref, jnp.zeros((length - r1, kw), F32)], axis=0)
        q_heads = _expand_heads(q_local[r0:r1, :], head_k)
        s = _bdot_nt(q_heads, k_ref)
        probs.append(jnp.where(s_pos <= q_pos + r0, s, 0.0))
    b_total = b_full[length - 1:length, :]
    k_end = k * jnp.exp(b_total - b_full)
    upd = _dot_tn(v.astype(BF16), k_end.astype(BF16))
    yield
    o_heads = _bdot(jnp.concatenate(probs, axis=0), v)
    assert len(out_box) == index, "the previous tile must have replaced the state before it is read"
    state_t = state_box[0]
    o_state = _bdot_nt(q * jnp.exp(b_full), state_t)
    same_head = _idiv(_iota((GROUP_WIDTH, kw), 0), HEAD_DIM) == _idiv(_iota((GROUP_WIDTH, kw), 1), head_k)
    state_box[0] = state_t * jnp.exp(b_total) + jnp.where(same_head, upd, 0.0)
    yield
    v_head = _idiv(_iota((GLA_SUB, GROUP_WIDTH), 1), HEAD_DIM)
    rows = []
    for c in range(n_sub):
        base = c * N_HEADS * GLA_SUB
        acc = jnp.zeros((GLA_SUB, GROUP_WIDTH), F32)
        for h in range(N_HEADS):
            acc = acc + jnp.where(v_head == h, o_heads[base + h * GLA_SUB:base + (h + 1) * GLA_SUB, :], 0.0)
        rows.append(acc)
    out_box.append(jnp.concatenate(rows, axis=0) + o_state)


def _gated_linear_attention_plan(q, k, v, log_f, gate, norm_w, o_ref, state_ref):
    state_box = [state_ref[...]]
    tiles = []
    stages = []
    for n in range(q.shape[0] // GLA_TILE):
        rows = slice(n * GLA_TILE, (n + 1) * GLA_TILE)
        stages.append(_gated_linear_attention_tile(n, q[rows], k[rows], v[rows], log_f[rows], state_box, tiles))

    def finish():
        state_ref[...] = state_box[0]
        o = jnp.concatenate(tiles, axis=0)
        ms = _dot_hilo(o * o, _head_group_matrix(GROUP_WIDTH, HEAD_DIM, 1.0 / HEAD_DIM).astype(BF16))
        o_ref[...] = (o * lax.rsqrt(ms + RMS_EPS) * norm_w * _silu(gate)).astype(o_ref.dtype)

    return stages, GLA_STAGGER, finish


def _gla_plan(z_ref, gk_up_ref, gk_bias_ref, norm_w_ref, o_ref, state_ref):
    z = z_ref[...]
    q = z[:, 0:128] * (GLA_DK ** -0.5)
    k = z[:, 128:256]
    v = z[:, 256:512]
    g = z[:, 512:768]
    gate_in = _dot_3pass(z[:, 768:896], gk_up_ref[...]) + gk_bias_ref[...]
    log_f = _log_sigmoid(gate_in) * (1.0 / GLA_GATE_NORMALIZER)
    return _gated_linear_attention_plan(q, k, v, log_f, g, norm_w_ref[...], o_ref, state_ref)


def _hgrn_plan(z_ref, lb_ref, log_lb_ref, norm_w_ref, o_ref, state_ref):
    z = z_ref[...]
    q = _silu(z[:, 0:256])
    f = z[:, 256:512]
    v = z[:, 512:768]
    g = z[:, 768:1024]
    lb = lb_ref[...]
    a = log_lb_ref[...]
    e = jnp.exp(-jnp.abs(f))
    inv = 1.0 / (1.0 + e)
    c = jnp.log1p(-lb) + jnp.minimum(f, 0.0) - jnp.log(1.0 + e)
    log_f = jnp.maximum(a, c) + _softplus_neg_abs(a - c)
    k = (1.0 - lb) * jnp.where(f >= 0.0, e * inv, inv)
    return _gated_linear_attention_plan(q, k, v, log_f, g, norm_w_ref[...], o_ref, state_ref)


def _gla_parts(z, gk_up, gk_bias, norm_w, step):
    out_spec, out_shape = _mix_out(z.shape[0], step)
    gk_up_pad = jnp.zeros((LANES, N_HEADS * GLA_DK), F32).at[:GLA_GATE_RANK].set(gk_up)
    return _MixerParts(
        plan=_gla_plan,
        args=[z, gk_up_pad, gk_bias.reshape(1, -1), jnp.tile(norm_w, N_HEADS).reshape(1, -1)],
        in_specs=[pl.BlockSpec((step, GLA_W), lambda i: (i, 0)),
                  _resident((LANES, N_HEADS * GLA_DK)), _resident((1, N_HEADS * GLA_DK)),
                  _resident((1, GROUP_WIDTH))],
        out_specs=[out_spec], out_shape=[out_shape],
        scratch=[pltpu.VMEM((GROUP_WIDTH, N_HEADS * GLA_DK), F32)])


def _hgrn_parts(z, lb, norm_w, step):
    out_spec, out_shape = _mix_out(z.shape[0], step)
    return _MixerParts(
        plan=_hgrn_plan,
        args=[z, lb.reshape(1, -1), jnp.log(lb).reshape(1, -1), jnp.tile(norm_w, N_HEADS).reshape(1, -1)],
        in_specs=[pl.BlockSpec((step, HGRN_W), lambda i: (i, 0)),
                  _resident((1, GROUP_WIDTH)), _resident((1, GROUP_WIDTH)), _resident((1, GROUP_WIDTH))],
        out_specs=[out_spec], out_shape=[out_shape],
        scratch=[pltpu.VMEM((GROUP_WIDTH, GROUP_WIDTH), F32)])


def _swa_block(q, kw, vw, visible, sinks, out_box):
    n = q.shape[0]
    head_cols = lambda x, h: x[:, h * HEAD_DIM:(h + 1) * HEAD_DIM]
    v_lane_head = _idiv(_iota(vw.shape, 1), HEAD_DIM)
    outs = []
    for kv in range(N_HEADS // 2):
        heads = (2 * kv, 2 * kv + 1)
        scores = _bdot_nt(jnp.concatenate([head_cols(q, h) for h in heads], axis=0), head_cols(kw, kv))
        yield
        probs, sink_terms = [], []
        for half, h in enumerate(heads):
            s = jnp.where(visible, scores[half * n:(half + 1) * n], -jnp.inf)
            sink = sinks[:, h:h + 1]
            m = jnp.maximum(jnp.max(s, axis=-1, keepdims=True), sink)
            probs.append(jnp.exp(s - m))
            sink_terms.append(jnp.exp(sink - m))
        o = _bdot(jnp.concatenate(probs, axis=0), jnp.where(v_lane_head == kv, vw, 1.0))
        yield
        sums = head_cols(o, 1 - kv)[:, 0:1]
        outs += [head_cols(o, kv)[half * n:(half + 1) * n] / (sums[half * n:(half + 1) * n] + sink_terms[half])
                 for half in range(2)]
    out_box.append(jnp.concatenate(outs, axis=-1))


def _swa_plan(q_ref, k_ref, v_ref, kp_ref, vp_ref, sink_ref, o_ref):
    w = SWA_WINDOW
    has_prev = pl.program_id(0) > 0
    q = q_ref[...] * (HEAD_DIM ** -0.5)
    k_all = jnp.concatenate([kp_ref[...], k_ref[...]], axis=0)
    v_all = jnp.concatenate([vp_ref[...], v_ref[...]], axis=0)
    q_pos = _iota((w, 2 * w), 0) + w
    k_pos = _iota((w, 2 * w), 1)
    dist = q_pos - k_pos
    in_window = (dist >= 0) & (dist < w)
    sinks = sink_ref[...]
    blocks = []
    stages = []
    for b in range(q.shape[0] // w):
        visible = in_window if b > 0 else in_window & ((k_pos >= w) | has_prev)
        stages.append(_swa_block(q[b * w:(b + 1) * w], k_all[b * w:(b + 2) * w], v_all[b * w:(b + 2) * w],
                                 visible, sinks, blocks))

    def finish():
        o_ref[...] = jnp.concatenate(blocks, axis=0).astype(o_ref.dtype)

    return stages, SWA_STAGGER, finish


def _swa_parts(z, sinks, step):
    out_spec, out_shape = _mix_out(z.shape[0], step)
    w = SWA_WINDOW
    prev = lambda col: (lambda i: (jnp.maximum(i * (step // w) - 1, 0), col))
    return _MixerParts(
        plan=_swa_plan,
        args=[z, z, z, z, z, sinks.reshape(1, -1)],
        in_specs=[pl.BlockSpec((step, 256), lambda i: (i, 0)),
                  pl.BlockSpec((step, 128), lambda i: (i, 2)), pl.BlockSpec((step, 128), lambda i: (i, 3)),
                  pl.BlockSpec((w, 128), prev(2)), pl.BlockSpec((w, 128), prev(3)),
                  _resident((1, N_HEADS))],
        out_specs=[out_spec], out_shape=[out_shape], scratch=[])


def _rwkv_chunk(index, r, k, v, a_vec, b_vec, log_w, state_box, out_box):
    c = r.shape[0]
    width = r.shape[1]
    tri = jnp.where(_iota((c, c), 1) <= _iota((c, c), 0), 1.0, 0.0).astype(F32)
    p = _dot_mask(tri, log_w)
    yield
    p_total = p[c - 1:c, :]
    decay_in = jnp.exp(p)
    decay_out = jnp.exp(-p)
    decay_end = jnp.exp(p_total - p)
    a_in = a_vec * jnp.exp(p - log_w)
    r_in = r * decay_in
    b_out = b_vec * decay_out
    k_out = k * decay_out
    b_end = b_vec * decay_end
    k_end = k * decay_end

    t_pos = _iota((c, width), 0)
    assert width == N_HEADS * c
    s_pos = _imod(_iota((c, width), 1), c)
    strict = s_pos < t_pos
    incl = s_pos <= t_pos
    expand = lambda x: _expand_heads(x, HEAD_DIM)

    scores = _bdot_nt(jnp.concatenate([a_in, r_in], axis=0),
                      jnp.concatenate([expand(b_out), expand(k_out)], axis=0))
    yield
    a_ab = jnp.where(strict, scores[0:c, 0:width], 0.0)
    a_ak = jnp.where(strict, scores[0:c, width:2 * width], 0.0)
    a_rb = jnp.where(incl, scores[c:2 * c, 0:width], 0.0)
    a_rk = jnp.where(incl, scores[c:2 * c, width:2 * width], 0.0)

    t_inv = jnp.where(s_pos == t_pos, 1.0, 0.0) + a_ab
    from_v = _bdot(jnp.concatenate([a_ak, a_rk], axis=0), expand(v))
    x1 = from_v[0:c]
    y_from_v = from_v[c:2 * c]
    power = _bdot(a_ab, expand(a_ab))
    yield
    n_factors = (c - 1).bit_length()
    for _ in range(n_factors - 2):
        both = _bdot(jnp.concatenate([t_inv, power], axis=0), expand(power))
        t_inv = t_inv + both[0:c]
        power = both[c:2 * c]
        yield
    t_inv = t_inv + _bdot(t_inv, expand(power))
    yield
    sol = _bdot(t_inv, jnp.concatenate([expand(x1), expand(a_in)], axis=1))
    yield
    u0 = sol[:, 0:width]
    w_mat = sol[:, width:2 * width]

    assert len(out_box) == index, "the previous chunk must have replaced the state before it is read"
    state = state_box[0]
    from_state = _bdot_nt(jnp.concatenate([w_mat, r_in], axis=0), state)
    yield
    u = u0 + from_state[0:c]
    y = _bdot(a_rb, expand(u)) + y_from_v + from_state[c:2 * c]
    upd = _dot_tn(jnp.concatenate([u, v], axis=0).astype(BF16),
                  jnp.concatenate([b_end, k_end], axis=0).astype(BF16))
    same_head = _idiv(_iota((width, width), 0), HEAD_DIM) == _idiv(_iota((width, width), 1), HEAD_DIM)
    state_box[0] = state * jnp.exp(p_total) + jnp.where(same_head, upd, 0.0)
    out_box.append(y)


def _rwkv_plan(has_vres, *refs):
    if has_vres:
        (z_ref, zp_ref, vfirst_ref, mu_ref, w0_ref, wup_ref, a0_ref, aup_ref, gup_ref, kk_ref, ka_ref,
         rk_ref, lnw_ref, lnb_ref, v0_ref, vup_ref, o_ref, state_ref) = refs
    else:
        (z_ref, zp_ref, mu_ref, w0_ref, wup_ref, a0_ref, aup_ref, gup_ref, kk_ref, ka_ref,
         rk_ref, lnw_ref, lnb_ref, o_ref, vout_ref, state_ref) = refs
    step = pl.program_id(0)
    z = z_ref[...]
    last_prev = jnp.where(step > 0, zp_ref[7:8, :], 0.0)
    prev = jnp.where(_iota(z.shape, 0) == 0, last_prev, pltpu.roll(z, 1, axis=0))
    zr = z + (prev - z) * mu_ref[...]
    r = zr[:, 0:256]
    k = zr[:, 256:512]
    v = zr[:, 512:768]
    low = zr[:, 768:896]
    w_pre = w0_ref[...] + _dot_3pass(jnp.tanh(low), wup_ref[...])
    w_log = -(jnp.maximum(-w_pre, 0.0) + _softplus_neg_abs(w_pre)) - 0.5
    log_w = -jnp.exp(w_log)
    a = _sigmoid(a0_ref[...] + _dot_3pass(low, aup_ref[...]))
    g = _dot_3pass(_sigmoid(low), gup_ref[...])
    if has_vres:
        v = v + (vfirst_ref[...] - v) * _sigmoid(v0_ref[...] + _dot_3pass(low, vup_ref[...]))
    else:
        vout_ref[...] = v
    head_sum = _head_group_matrix(GROUP_WIDTH, HEAD_DIM, 1.0).astype(BF16)
    kk = k * kk_ref[...]
    kk = kk / jnp.maximum(jnp.sqrt(_dot_hilo(kk * kk, head_sum)), 1e-12)
    k = k * (1.0 + (a - 1.0) * ka_ref[...])
    a_vec = -kk
    b_vec = kk * a

    c = RWKV_CHUNK
    state_box = [state_ref[...]]
    chunks = []
    stages = []
    for n in range(z.shape[0] // c):
        rows = slice(n * c, (n + 1) * c)
        stages.append(_rwkv_chunk(n, r[rows], k[rows], v[rows], a_vec[rows], b_vec[rows], log_w[rows],
                                  state_box, chunks))

    def finish():
        state_ref[...] = state_box[0]
        y = jnp.concatenate(chunks, axis=0)
        head_mean = _head_group_matrix(GROUP_WIDTH, HEAD_DIM, 1.0 / HEAD_DIM).astype(BF16)
        mu_y = _dot_hilo(y, head_mean)
        d = y - mu_y
        var_y = _dot_hilo(d * d, head_mean)
        y = d * lax.rsqrt(var_y + RWKV_GN_EPS) * lnw_ref[...] + lnb_ref[...]
        bonus = _dot_hilo(r * k * rk_ref[...], head_sum) * v
        o_ref[...] = ((y + bonus) * g).astype(o_ref.dtype)

    return stages, RWKV_STAGGER, finish


def _rwkv_parts(z, mu, w0, w_up, a0, a_up, g_up, k_k, k_a, r_k, lnx_w, lnx_b, vres, step):
    t = z.shape[0]
    c = step
    row = lambda a: a.reshape(1, -1)
    low_rows = lambda a, start: jnp.zeros((LANES, GROUP_WIDTH), F32).at[start:start + a.shape[0]].set(a)
    has_vres = vres is not None
    mu_full = jnp.zeros((RWKV_W,), F32).at[:RWKV_COLS].set(mu)
    vec = _resident((1, GROUP_WIDTH))
    mat = _resident((LANES, GROUP_WIDTH))
    tile = pl.BlockSpec((c, GROUP_WIDTH), lambda i: (i, 0))
    z_specs = [pl.BlockSpec((c, RWKV_W), lambda i: (i, 0)),
               pl.BlockSpec((8, RWKV_W), lambda i: (jnp.maximum(i * (c // 8) - 1, 0), 0))]
    common = [row(w0), low_rows(w_up, 0), row(a0), low_rows(a_up, 16), low_rows(g_up, 32),
              row(k_k), row(k_a), row(r_k), row(lnx_w), row(lnx_b)]
    common_specs = [vec, mat, vec, mat, mat, vec, vec, vec, vec, vec]
    if has_vres:
        v_first, vres_mu, v0, v_up = vres
        mu_full = mu_full.at[RWKV_COLS:RWKV_COLS + RWKV_V_RANK].set(vres_mu)
        args = [z, z, v_first, row(mu_full)] + common + [row(v0), low_rows(v_up, 64)]
        in_specs = z_specs + [tile, _resident((1, RWKV_W))] + common_specs + [vec, mat]
        out_specs = [tile]
        out_shape = [jax.ShapeDtypeStruct((t, GROUP_WIDTH), MIX_DTYPE)]
    else:
        args = [z, z, row(mu_full)] + common
        in_specs = z_specs + [_resident((1, RWKV_W))] + common_specs
        out_specs = [tile, tile]
        out_shape = [jax.ShapeDtypeStruct((t, GROUP_WIDTH), MIX_DTYPE), jax.ShapeDtypeStruct((t, GROUP_WIDTH), F32)]
    return _MixerParts(plan=functools.partial(_rwkv_plan, has_vres), args=args, in_specs=in_specs,
                       out_specs=out_specs, out_shape=out_shape,
                       scratch=[pltpu.VMEM((GROUP_WIDTH, GROUP_WIDTH), F32)])


def _store_token_tiles(ref, index, x):
    n = x.shape[0]
    for j in range(SUBLANES):
        ref[(*index, pl.ds(j, n, stride=SUBLANES), slice(None))] = x[:, j * LANES:(j + 1) * LANES]


def _load_token_tiles(ref, index, n):
    return jnp.concatenate([ref[(*index, pl.ds(j, n, stride=SUBLANES), slice(None))] for j in range(SUBLANES)],
                           axis=-1)


def _top2_route(logits):
    row = _iota(logits.shape, 0).astype(F32)
    m1 = jnp.max(logits, axis=0, keepdims=True)
    i1 = jnp.min(jnp.where(logits == m1, row, N_EXPERTS), axis=0, keepdims=True)
    rest = jnp.where(row == i1, -jnp.inf, logits)
    m2 = jnp.max(rest, axis=0, keepdims=True)
    i2 = jnp.min(jnp.where(rest == m2, row, N_EXPERTS), axis=0, keepdims=True)
    e2 = jnp.exp(m2 - m1)
    g1 = 1.0 / (1.0 + e2)
    g2 = e2 * g1
    return jnp.where(row == 0, i1, jnp.where(row == 1, i2, jnp.where(row == 2, g1, jnp.where(row == 3, g2, 0.0))))


def _mix_residual_norm(alpha, x_ref, mix_refs, w_ref, g_ref, b_ref):
    acc = alpha * x_ref[...]
    for h, ref in enumerate(mix_refs):
        acc = acc + _dot(ref[...], w_ref[h * GROUP_WIDTH:(h + 1) * GROUP_WIDTH, :])
    return _layer_norm(acc, g_ref[...], b_ref[...])


def _outproj_route_kernel(alpha, x_ref, o0_ref, o1_ref, o2_ref, o3_ref, w_ref, g_ref, b_ref, router_ref,
                          tiles_ref, route_ref):
    y = _mix_residual_norm(alpha, x_ref, (o0_ref, o1_ref, o2_ref, o3_ref), w_ref, g_ref, b_ref)
    route_ref[...] = _top2_route(_dot_3pass_nt(router_ref[...], y))
    _store_token_tiles(tiles_ref, (), y)


def _out_proj_ln_route(alpha, x, mixes, w_out, g, b, router):
    t = x.shape[0]
    n = WIDE_ROW_TILE
    row_d = pl.BlockSpec((n, D_MODEL), lambda i: (i, 0))
    row_g = pl.BlockSpec((n, GROUP_WIDTH), lambda i: (i, 0))
    return pl.pallas_call(
        functools.partial(_outproj_route_kernel, alpha),
        grid=(t // n,),
        in_specs=[row_d, row_g, row_g, row_g, row_g, _resident((D_MODEL, D_MODEL)),
                  _resident((1, D_MODEL)), _resident((1, D_MODEL)), _resident((N_EXPERTS, D_MODEL))],
        out_specs=[pl.BlockSpec((n * SUBLANES, LANES), lambda i: (i, 0)),
                   pl.BlockSpec((N_EXPERTS, n), lambda i: (0, i))],
        out_shape=[jax.ShapeDtypeStruct((t * SUBLANES, LANES), F32), jax.ShapeDtypeStruct((N_EXPERTS, t), F32)],
        compiler_params=_params(("parallel",)),
        name="out_proj_ln_route",
    )(x, *mixes, w_out.astype(BF16), g.reshape(1, -1), b.reshape(1, -1), router.T)


def _ln_embed(y, ln_g, ln_b, p, ple_gate, ple_proj):
    x = _layer_norm(y, ln_g, ln_b)
    gate = _sigmoid(_dot(x.astype(BF16), ple_gate))
    return x + gate * _dot(p.astype(BF16), ple_proj)


def _dense_ffn_kernel(alpha, x_ref, o0_ref, o1_ref, o2_ref, o3_ref, wo_ref, g1_ref, b1_ref, p_ref, wg_ref, wu_ref,
                      wd_ref, g_ref, b_ref, pg_ref, pp_ref, y_ref, acc_ref):
    x = _mix_residual_norm(alpha, x_ref, (o0_ref, o1_ref, o2_ref, o3_ref), wo_ref, g1_ref, b1_ref)
    xb = x.astype(BF16)
    acc_ref[...] = alpha * x
    for j in range(D_FF // FF_CHUNK):
        cols = slice(j * FF_CHUNK, (j + 1) * FF_CHUNK)
        h = _silu(_dot(xb, wg_ref[:, cols])) * _dot(xb, wu_ref[:, cols])
        acc_ref[...] += _dot(h.astype(BF16), wd_ref[cols, :])
    y_ref[...] = _ln_embed(acc_ref[...], g_ref[...], b_ref[...], p_ref[...], pg_ref[...], pp_ref[...])


def _dense_layer_tail(alpha, x, mixes, w_out, g1, b1, p_all, p_block, w_gate, w_up, w_down, g, b, ple_gate, ple_proj):
    t = x.shape[0]
    row_d = pl.BlockSpec((ROW_TILE, D_MODEL), lambda i: (i, 0))
    row_g = pl.BlockSpec((ROW_TILE, GROUP_WIDTH), lambda i: (i, 0))
    return pl.pallas_call(
        functools.partial(_dense_ffn_kernel, alpha),
        grid=(t // ROW_TILE,),
        in_specs=[row_d, row_g, row_g, row_g, row_g, _resident((D_MODEL, D_MODEL)),
                  _resident((1, D_MODEL)), _resident((1, D_MODEL)),
                  pl.BlockSpec((ROW_TILE, PLE_DIM), lambda i: (p_block + i, 0)),
                  _resident((D_MODEL, D_FF)), _resident((D_MODEL, D_FF)), _resident((D_FF, D_MODEL)),
                  _resident((1, D_MODEL)), _resident((1, D_MODEL)),
                  _resident((D_MODEL, D_MODEL)), _resident((PLE_DIM, D_MODEL))],
        out_specs=row_d,
        out_shape=jax.ShapeDtypeStruct((t, D_MODEL), F32),
        scratch_shapes=[pltpu.VMEM((ROW_TILE, D_MODEL), F32)],
        compiler_params=_params(("parallel",)),
        name="dense_ffn_tail",
    )(x, *mixes, w_out.astype(BF16), g1.reshape(1, -1), b1.reshape(1, -1), p_all,
      w_gate.astype(BF16), w_up.astype(BF16), w_down.astype(BF16), g.reshape(1, -1), b.reshape(1, -1),
      ple_gate.astype(BF16), ple_proj.astype(BF16))


def _weight_group_copies(e, group, wg_hbm, wu_hbm, wd_hbm, stage_cols_ref, stage_rows_ref, sem):
    slot = group % 2
    lo, hi = group * MOE_WEIGHT_CHUNK, (group + 1) * MOE_WEIGHT_CHUNK
    return (pltpu.make_async_copy(wg_hbm.at[e, :, lo:hi], stage_cols_ref.at[slot, 0], sem.at[slot]),
            pltpu.make_async_copy(wu_hbm.at[e, :, lo:hi], stage_cols_ref.at[slot, 1], sem.at[slot]),
            pltpu.make_async_copy(wd_hbm.at[e, lo:hi, :], stage_rows_ref.at[slot], sem.at[slot]))


def _expert_kernel(row_tok_ref, row_dst_ref, block_e_ref, n_used_ref, x_hbm, wg_hbm, wu_hbm, wd_hbm, y_hbm,
                   rows_ref, xb_ref, acc_ref, ybuf_ref, wg_ref, wu_ref, wd_ref, stage_cols_ref, stage_rows_ref,
                   gather_sem, scatter_sem, weight_sem):
    i = pl.program_id(0)
    n_used = n_used_ref[0]
    last = pl.num_programs(0) - 1
    n = MOE_ROWS
    slot = lax.rem(i, 2)
    other = 1 - slot
    tile = lambda first_row: pl.ds(pl.multiple_of(first_row, SUBLANES), SUBLANES)
    gather_row = lambda tok_row, s, r: pltpu.make_async_copy(
        x_hbm.at[tile(tok_row)], rows_ref.at[s, tile(r * SUBLANES)], gather_sem.at[s])
    scatter_row = lambda dst_row, s, r: pltpu.make_async_copy(
        ybuf_ref.at[s, tile(r * SUBLANES)], y_hbm.at[tile(dst_row)], scatter_sem.at[s])
    block_rows = n * SUBLANES
    gather_block = lambda s: pltpu.make_async_copy(x_hbm.at[pl.ds(0, block_rows)], rows_ref.at[s], gather_sem.at[s])
    scatter_block = lambda s: pltpu.make_async_copy(ybuf_ref.at[s], y_hbm.at[pl.ds(0, block_rows)],
                                                    scatter_sem.at[s])

    @pl.when(i == 0)
    def _():
        ybuf_ref[1] = jnp.zeros((block_rows, LANES), F32)

        def start(r, carry):
            gather_row(row_tok_ref[r], 0, r).start()
            return carry

        lax.fori_loop(0, n, start, 0)

    def scatter_all(first_dst, s):
        def start(r, carry):
            scatter_row(row_dst_ref[first_dst + r], s, r).start()
            return carry

        lax.fori_loop(0, n, start, 0)
        scatter_block(s).wait()

    expert = block_e_ref[i]
    new_expert = (i == 0) | (expert != block_e_ref[jnp.maximum(i - 1, 0)])

    weight_group = lambda g: _weight_group_copies(expert, g, wg_hbm, wu_hbm, wd_hbm, stage_cols_ref,
                                                  stage_rows_ref, weight_sem)
    chunks_per_group = MOE_WEIGHT_CHUNK // MOE_FF_CHUNK
    n_groups = D_FF_EXPERT // MOE_WEIGHT_CHUNK

    def take_weight_group(g):
        if g + 1 < n_groups:
            for copy in weight_group(g + 1):
                copy.start()
        for copy in weight_group(g):
            copy.wait()
        piece = slice(g * MOE_WEIGHT_CHUNK, (g + 1) * MOE_WEIGHT_CHUNK)
        wg_ref[:, piece] = stage_cols_ref[g % 2, 0].astype(BF16)
        wu_ref[:, piece] = stage_cols_ref[g % 2, 1].astype(BF16)
        wd_ref[piece, :] = stage_rows_ref[g % 2].astype(BF16)

    @pl.when(i < n_used)
    def _():
        @pl.when(new_expert)
        def _():
            for copy in weight_group(0):
                copy.start()

        gather_block(slot).wait()
        xb_ref[...] = _load_token_tiles(rows_ref, (slot,), n).astype(BF16)
        n_chunks = D_FF_EXPERT // MOE_FF_CHUNK
        rows_per_chunk = -(-n // (n_chunks - MOE_DMA_FREE_CHUNKS))
        for j in range(n_chunks):
            if j % chunks_per_group == 0:
                pl.when(new_expert)(functools.partial(take_weight_group, j // chunks_per_group))
            cols = slice(j * MOE_FF_CHUNK, (j + 1) * MOE_FF_CHUNK)
            xb = xb_ref[...]
            h = _silu(_dot(xb, wg_ref[:, cols])) * _dot(xb, wu_ref[:, cols])
            part = _dot(h.astype(BF16), wd_ref[cols, :])
            if j == 0:
                acc_ref[...] = part
            else:
                acc_ref[...] += part
            for r in range(j * rows_per_chunk, min((j + 1) * rows_per_chunk, n)):
                gather_row(row_tok_ref[(i + 1) * n + r], other, r).start(priority=GATHER_DMA_PRIORITY)
                scatter_row(row_dst_ref[i * n + r], other, r).start(priority=SCATTER_DMA_PRIORITY)
        _store_token_tiles(ybuf_ref, (slot,), acc_ref[...])
        scatter_block(other).wait()

    @pl.when(i == n_used)
    def _():
        gather_block(slot).wait()
        scatter_all(i * n, other)

    @pl.when(i >= n_used)
    def _():
        rows_ref[slot] = jnp.zeros((block_rows, LANES), F32)
        fill = pltpu.make_async_copy(rows_ref.at[slot], y_hbm.at[pl.ds(i * block_rows, block_rows)],
                                     scatter_sem.at[slot])
        fill.start()
        fill.wait()

    @pl.when((i == last) & (i < n_used))
    def _():
        gather_block(other).wait()
        scatter_all((i + 1) * n, slot)


def _expert_rows(x, row_tok, row_dst, block_e, n_used, n_out_rows, w_gate, w_up, w_down):
    n_blocks = block_e.shape[0]
    hbm = pl.BlockSpec(memory_space=pl.ANY)
    grid_spec = pltpu.PrefetchScalarGridSpec(
        num_scalar_prefetch=4,
        grid=(n_blocks,),
        in_specs=[hbm, hbm, hbm, hbm],
        out_specs=hbm,
        scratch_shapes=[pltpu.VMEM((2, MOE_ROWS * SUBLANES, LANES), F32), pltpu.VMEM((MOE_ROWS, D_MODEL), BF16),
                        pltpu.VMEM((MOE_ROWS, D_MODEL), F32), pltpu.VMEM((2, MOE_ROWS * SUBLANES, LANES), F32),
                        pltpu.VMEM((D_MODEL, D_FF_EXPERT), BF16), pltpu.VMEM((D_MODEL, D_FF_EXPERT), BF16),
                        pltpu.VMEM((D_FF_EXPERT, D_MODEL), BF16),
                        pltpu.VMEM((2, 2, D_MODEL, MOE_WEIGHT_CHUNK), F32),
                        pltpu.VMEM((2, MOE_WEIGHT_CHUNK, D_MODEL), F32),
                        pltpu.SemaphoreType.DMA((2,)), pltpu.SemaphoreType.DMA((2,)),
                        pltpu.SemaphoreType.DMA((2,))],
    )
    return pl.pallas_call(
        _expert_kernel,
        grid_spec=grid_spec,
        out_shape=jax.ShapeDtypeStruct((n_out_rows * SUBLANES, LANES), F32),
        compiler_params=_params(("arbitrary",)),
        name="moe_experts",
    )(row_tok, row_dst, block_e, n_used.reshape(1), x, w_gate, w_up, w_down)


def _combine_kernel(alpha, x_ref, y0_ref, y1_ref, gates_ref, p_ref, g_ref, b_ref, pg_ref, pp_ref, o_ref):
    n = WIDE_ROW_TILE
    gates = gates_ref[...]
    f = _load_token_tiles(y0_ref, (), n) * gates[:, 0:1] + _load_token_tiles(y1_ref, (), n) * gates[:, 1:2]
    o_ref[...] = _ln_embed(alpha * _load_token_tiles(x_ref, (), n) + f, g_ref[...], b_ref[...], p_ref[...],
                           pg_ref[...], pp_ref[...])


def _moe_combine_tail(alpha, x_tiles, p_all, p_block, y_tiles, gates, g, b, ple_gate, ple_proj):
    t = x_tiles.shape[0] // SUBLANES
    n = WIDE_ROW_TILE
    row_d = pl.BlockSpec((n, D_MODEL), lambda i: (i, 0))
    tiles_d = pl.BlockSpec((n * SUBLANES, LANES), lambda i: (i, 0))
    return pl.pallas_call(
        functools.partial(_combine_kernel, alpha),
        grid=(t // n,),
        in_specs=[tiles_d, tiles_d, pl.BlockSpec((n * SUBLANES, LANES), lambda i: (t // n + i, 0)),
                  pl.BlockSpec((n, 2), lambda i: (i, 0)), pl.BlockSpec((n, PLE_DIM), lambda i: (p_block + i, 0)),
                  _resident((1, D_MODEL)), _resident((1, D_MODEL)),
                  _resident((D_MODEL, D_MODEL)), _resident((PLE_DIM, D_MODEL))],
        out_specs=row_d,
        out_shape=jax.ShapeDtypeStruct((t, D_MODEL), F32),
        compiler_params=_params(("parallel",)),
        name="moe_combine_tail",
    )(x_tiles, y_tiles, y_tiles, gates, p_all, g.reshape(1, -1), b.reshape(1, -1), ple_gate.astype(BF16),
      ple_proj.astype(BF16))


def _moe_tail(alpha, x_tiles, route, p_all, p_block, w_gate, w_up, w_down, g, b, ple_gate, ple_proj):
    t = route.shape[1]
    experts = route[0:2].T.astype(jnp.int32)
    gates = route[2:4].T
    e_flat = experts.reshape(-1)
    onehot = (e_flat[:, None] == jnp.arange(N_EXPERTS, dtype=jnp.int32)[None, :]).astype(jnp.int32)
    counts = jnp.sum(onehot, axis=0)
    padded = (counts + MOE_ROWS - 1) // MOE_ROWS * MOE_ROWS
    pad_end = jnp.cumsum(padded)
    pad_start = pad_end - padded
    n_blocks = (2 * t) // MOE_ROWS + N_EXPERTS
    n_rows = n_blocks * MOE_ROWS
    block_start = jnp.arange(n_blocks, dtype=jnp.int32) * MOE_ROWS
    block_e = jnp.minimum(jnp.sum((block_start[:, None] >= pad_end[None, :]).astype(jnp.int32), axis=1),
                          N_EXPERTS - 1)
    order = jnp.argsort(e_flat, stable=True).astype(jnp.int32)
    seg_end = jnp.cumsum(counts)
    seg_start = seg_end - counts
    per_row = lambda per_expert: jnp.repeat(per_expert[block_e], MOE_ROWS)
    place = jnp.arange(n_rows, dtype=jnp.int32) - per_row(pad_start - seg_start)
    used = place < per_row(seg_end)
    row_assign = jnp.where(used, order[jnp.clip(place, 0, 2 * t - 1)], -1)
    spare = 2 * t + jnp.cumsum(jnp.where(used, 0, 1).astype(jnp.int32)) - 1
    row_tok = jnp.where(used, row_assign // 2, 0)
    row_dst = jnp.where(used, (row_assign % 2) * t + row_assign // 2, spare)
    n_spare = n_rows - 2 * t
    first_dst = 2 * t + n_spare + jnp.arange(MOE_ROWS, dtype=jnp.int32)
    row_tok = jnp.concatenate([row_tok, jnp.zeros((MOE_ROWS,), jnp.int32)])
    row_dst = jnp.concatenate([first_dst, row_dst])
    n_used = (pad_end[-1] // MOE_ROWS).astype(jnp.int32)
    y_tiles = _expert_rows(x_tiles, row_tok * SUBLANES, row_dst * SUBLANES, block_e, n_used,
                           2 * t + n_spare + MOE_ROWS, w_gate, w_up, w_down)
    return _moe_combine_tail(alpha, x_tiles, p_all, p_block, y_tiles, gates, g, b, ple_gate, ple_proj)


def kernel(x, p, w_in, w_out, gla_gk_up, gla_gk_bias, gla_norm_w, hgrn_lower_bounds, hgrn_norm_w, swa_sinks, rwkv_mu, rwkv_w0, rwkv_w_up, rwkv_a0, rwkv_a_up, rwkv_g_up, rwkv_k_k, rwkv_k_a, rwkv_r_k, rwkv_lnx_w, rwkv_lnx_b, rwkv_vres_down, rwkv_vres_mu, rwkv_v0, rwkv_vres_up, ln1_g, ln1_b, ln2_g, ln2_b, ffn_w_gate, ffn_w_up, ffn_w_down, moe_router, moe_w_gate, moe_w_up, moe_w_down, ple_proj, ple_gate):
    bsz, seq, d = x.shape
    depth = w_in.shape[0]
    alpha = (2.0 * depth) ** 0.25
    lbs = jnp.cumsum(jax.nn.softmax(hgrn_lower_bounds.astype(F32), axis=0), axis=0)
    lbs = lbs - lbs[0]
    p_all = p.reshape(-1, PLE_DIM)
    outs = []
    for bi in range(bsz):
        xt = x[bi]
        v_first = None
        for i in range(depth):
            w = _group_in_weights(w_in[i], None if i == 0 else rwkv_vres_down[i - 1])
            z_gla, z_hgrn, z_swa, z_rwkv = _in_proj(xt, w)
            vres = None if i == 0 else (v_first, rwkv_vres_mu[i - 1], rwkv_v0[i - 1], rwkv_vres_up[i - 1])
            mixed = _call_mixers(
                [_gla_parts(z_gla, gla_gk_up[i], gla_gk_bias[i], gla_norm_w[i], MIX_STEP),
                 _hgrn_parts(z_hgrn, lbs[i], hgrn_norm_w[i], MIX_STEP),
                 _swa_parts(z_swa, swa_sinks[i], MIX_STEP),
                 _rwkv_parts(z_rwkv, rwkv_mu[i], rwkv_w0[i], rwkv_w_up[i], rwkv_a0[i], rwkv_a_up[i],
                             rwkv_g_up[i], rwkv_k_k[i], rwkv_k_a[i], rwkv_r_k[i].reshape(-1),
                             rwkv_lnx_w[i], rwkv_lnx_b[i], vres, MIX_STEP)],
                seq, MIX_STEP)
            mixes = tuple(mixed[0:4])
            if i == 0:
                v_first = mixed[4]
            j = i // 2
            p_row = (i * bsz + bi) * seq
            if i % 2 == 0:
                xt = _dense_layer_tail(alpha, xt, mixes, w_out[i], ln1_g[i], ln1_b[i], p_all, p_row // ROW_TILE,
                                       ffn_w_gate[j], ffn_w_up[j], ffn_w_down[j], ln2_g[i], ln2_b[i],
                                       ple_gate[i], ple_proj[i])
            else:
                x_tiles, route = _out_proj_ln_route(alpha, xt, mixes, w_out[i], ln1_g[i], ln1_b[i], moe_router[j])
                xt = _moe_tail(alpha, x_tiles, route, p_all, p_row // WIDE_ROW_TILE, moe_w_gate[j], moe_w_up[j],
                               moe_w_down[j],
                               ln2_g[i], ln2_b[i], ple_gate[i], ple_proj[i])
        outs.append(xt)
    return jnp.stack(outs, axis=0)
```

```python
import functools
from typing import Callable, NamedTuple

import jax
import jax.numpy as jnp
from jax import lax
from jax.experimental import pallas as pl
from jax.experimental.pallas import tpu as pltpu

F32 = jnp.float32
BF16 = jnp.bfloat16
MIX_DTYPE = BF16

D_MODEL = 1024
GROUP_WIDTH = 256
N_HEADS = 4
HEAD_DIM = 64
GLA_DK = 32
GLA_GATE_RANK = 16
GLA_GATE_NORMALIZER = 16.0
SWA_WINDOW = 128
RWKV_COLS = 3 * GROUP_WIDTH + 16 + 16 + 32
RWKV_V_RANK = 8
LN_EPS = 1e-5
RMS_EPS = 1e-6
RWKV_GN_EPS = 64e-5
D_FF = 2816
N_EXPERTS = 8
D_FF_EXPERT = 3584
PLE_DIM = 256

LANES = 128
SUBLANES = 8
GLA_W = 896
HGRN_W = 1024
SWA_W = 512
RWKV_W = 896
Z_W = GLA_W + HGRN_W + SWA_W + RWKV_W

GLA_SUB = 16
GLA_TILE = 128
GLA_STAGGER = 5
SWA_STAGGER = 5
MIX_STEP = 1024
RWKV_CHUNK = 64
RWKV_STAGGER = 2
ROW_TILE = 512
WIDE_ROW_TILE = 1024
FF_CHUNK = 256
MOE_ROWS = 512
MOE_WEIGHT_CHUNK = 512
MOE_DMA_FREE_CHUNKS = 10
GATHER_DMA_PRIORITY = 0
SCATTER_DMA_PRIORITY = 1
VMEM_LIMIT = 56 * 1024 * 1024


def _iota(shape, dim):
    return lax.broadcasted_iota(jnp.int32, shape, dim)


def _idiv(x, n):
    return jnp.right_shift(x, n.bit_length() - 1)


def _imod(x, n):
    return jnp.bitwise_and(x, n - 1)


def _dot(a, b):
    return jnp.dot(a, b, preferred_element_type=F32)


def _dot_nt(a, b):
    return lax.dot_general(a, b, (((1,), (1,)), ((), ())), preferred_element_type=F32)


def _dot_tn(a, b):
    return lax.dot_general(a, b, (((0,), (0,)), ((), ())), preferred_element_type=F32)


def _bdot(a, b):
    return _dot(a.astype(BF16), b.astype(BF16))


def _bdot_nt(a, b):
    return _dot_nt(a.astype(BF16), b.astype(BF16))


def _dot_hilo(x, m):
    hi = x.astype(BF16)
    lo = (x - hi.astype(F32)).astype(BF16)
    return _dot(hi, m) + _dot(lo, m)


def _dot_mask(m, x):
    m = m.astype(BF16)
    x1 = x.astype(BF16)
    r1 = x - x1.astype(F32)
    x2 = r1.astype(BF16)
    x3 = (r1 - x2.astype(F32)).astype(BF16)
    return _dot(m, x1) + _dot(m, x2) + _dot(m, x3)


def _dot_3pass_nt(x, w):
    x_hi = x.astype(BF16)
    x_lo = (x - x_hi.astype(F32)).astype(BF16)
    w_hi = w.astype(BF16)
    w_lo = (w - w_hi.astype(F32)).astype(BF16)
    return _dot_nt(x_hi, w_hi) + _dot_nt(x_lo, w_hi) + _dot_nt(x_hi, w_lo)


def _dot_3pass(x, w):
    x_hi = x.astype(BF16)
    x_lo = (x - x_hi.astype(F32)).astype(BF16)
    w_hi = w.astype(BF16)
    w_lo = (w - w_hi.astype(F32)).astype(BF16)
    return _dot(x_hi, w_hi) + _dot(x_lo, w_hi) + _dot(x_hi, w_lo)


def _sigmoid(x):
    return 1.0 / (1.0 + jnp.exp(-x))


def _silu(x):
    return x * _sigmoid(x)


def _softplus_neg_abs(x):
    return jnp.log(1.0 + jnp.exp(-jnp.abs(x)))


def _log_sigmoid(x):
    return jnp.minimum(x, 0.0) - _softplus_neg_abs(x)


def _layer_norm(y, g, b):
    mu = jnp.mean(y, axis=-1, keepdims=True)
    d = y - mu
    var = jnp.mean(d * d, axis=-1, keepdims=True)
    return d * lax.rsqrt(var + LN_EPS) * g + b


def _expand_heads(x, head_width):
    lane_head = _idiv(_iota(x.shape, 1), head_width)
    return jnp.concatenate([jnp.where(lane_head == h, x, 0.0) for h in range(N_HEADS)], axis=0)


def _head_group_matrix(width, head_width, value):
    same = _idiv(_iota((width, width), 0), head_width) == _idiv(_iota((width, width), 1), head_width)
    return jnp.where(same, value, 0.0).astype(F32)


def _resident(shape):
    nd = len(shape)
    return pl.BlockSpec(shape, lambda *_: (0,) * nd, pipeline_mode=pl.Buffered(1))


def _params(semantics):
    return pltpu.CompilerParams(dimension_semantics=semantics, vmem_limit_bytes=VMEM_LIMIT)


def _run_plans(plans):
    live = [dict(enumerate(stages)) for stages, _, _ in plans]
    rnd = 0
    while any(live):
        for group, (_, stagger, _) in zip(live, plans):
            for n in sorted(group):
                if rnd >= n * stagger:
                    try:
                        next(group[n])
                    except StopIteration:
                        del group[n]
        rnd += 1
    for _, _, finish in plans:
        finish()


def _inproj_kernel(x_ref, w_ref, gla_ref, hgrn_ref, swa_ref, rwkv_ref):
    xb = x_ref[...].astype(BF16)
    o = 0
    for ref, width in ((gla_ref, GLA_W), (hgrn_ref, HGRN_W), (swa_ref, SWA_W), (rwkv_ref, RWKV_W)):
        ref[...] = _dot(xb, w_ref[:, o:o + width])
        o += width


def _in_proj(x, w):
    t = x.shape[0]
    widths = (GLA_W, HGRN_W, SWA_W, RWKV_W)
    return pl.pallas_call(
        _inproj_kernel,
        grid=(t // WIDE_ROW_TILE,),
        in_specs=[pl.BlockSpec((WIDE_ROW_TILE, D_MODEL), lambda i: (i, 0)), _resident((D_MODEL, Z_W))],
        out_specs=[pl.BlockSpec((WIDE_ROW_TILE, w_), lambda i: (i, 0)) for w_ in widths],
        out_shape=[jax.ShapeDtypeStruct((t, w_), F32) for w_ in widths],
        compiler_params=_params(("parallel",)),
        name="in_proj",
    )(x, w)


def _group_in_weights(w_in, vres_down):
    gla, hgrn, swa, rwkv = jnp.split(w_in, (784, 784 + 1024, 784 + 1024 + 512), axis=1)
    if vres_down is not None:
        rwkv = jnp.concatenate([rwkv, vres_down], axis=1)
    pad = lambda a, w_: jnp.pad(a, ((0, 0), (0, w_ - a.shape[1])))
    return jnp.concatenate([pad(gla, GLA_W), hgrn, swa, pad(rwkv, RWKV_W)], axis=1).astype(BF16)


class _MixerParts(NamedTuple):
    plan: Callable
    args: list
    in_specs: list
    out_specs: list
    out_shape: list
    scratch: list


def _mixers_kernel(layout, *refs):
    n_in = sum(entry[1] for entry in layout)
    n_out = sum(entry[2] for entry in layout)
    ins, outs, scratch = list(refs[:n_in]), list(refs[n_in:n_in + n_out]), list(refs[n_in + n_out:])

    @pl.when(pl.program_id(0) == 0)
    def _():
        for ref in scratch:
            ref[...] = jnp.zeros_like(ref)

    plans = []
    for plan, n_i, n_o, n_s in layout:
        plans.append(plan(*ins[:n_i], *outs[:n_o], *scratch[:n_s]))
        del ins[:n_i], outs[:n_o], scratch[:n_s]
    _run_plans(plans)


def _call_mixers(parts_list, tokens, step):
    layout = tuple((p.plan, len(p.args), len(p.out_shape), len(p.scratch)) for p in parts_list)
    flat = lambda field: [item for p in parts_list for item in getattr(p, field)]
    return pl.pallas_call(
        functools.partial(_mixers_kernel, layout),
        grid=(tokens // step,),
        in_specs=flat("in_specs"),
        out_specs=flat("out_specs"),
        out_shape=flat("out_shape"),
        scratch_shapes=flat("scratch"),
        compiler_params=_params(("arbitrary",)),
        name="mixers",
    )(*flat("args"))


def _mix_out(tokens, step):
    return (pl.BlockSpec((step, GROUP_WIDTH), lambda i: (i, 0)),
            jax.ShapeDtypeStruct((tokens, GROUP_WIDTH), MIX_DTYPE))


def _gated_linear_attention_tile(index, q, k, v, log_f, state_box, out_box):
    length, kw = q.shape
    head_k = kw // N_HEADS
    n_sub = length // GLA_SUB
    row = _iota((length, length), 0)
    col = _iota((length, length), 1)
    same_sub = _idiv(row, GLA_SUB) == _idiv(col, GLA_SUB)
    m_local = jnp.where(same_sub & (col <= row), 1.0, 0.0).astype(F32)
    m_prev = jnp.where(_idiv(col, GLA_SUB) < _idiv(row, GLA_SUB), 1.0, 0.0).astype(F32)
    sums = _dot_mask(jnp.concatenate([m_local, m_prev], axis=0), log_f)
    yield
    b_local = sums[0:length]
    b_start = sums[length:2 * length]
    b_full = b_start + b_local
    q_local = q * jnp.exp(b_local)

    q_pos = _imod(_iota((N_HEADS * GLA_SUB, length), 0), GLA_SUB)
    s_pos = _iota((N_HEADS * GLA_SUB, length), 1)
    probs = []
    for c in range(n_sub):
        r0, r1 = c * GLA_SUB, (c + 1) * GLA_SUB
        expo = jnp.where(_iota((r1, kw), 0) < r0, b_start[r0:r0 + 1, :] - b_full[0:r1], -b_local[0:r1])
        k_ref = k[0:r1] * jnp.exp(expo)
        if r1 < length:
            k_ref = jnp.concatenate([k_ref, jnp.zeros((length - r1, kw), F32)], axis=0)
        q_heads = _expand_heads(q_local[r0:r1, :], head_k)
        s = _bdot_nt(q_heads, k_ref)
        probs.append(jnp.where(s_pos <= q_pos + r0, s, 0.0))
    b_total = b_full[length - 1:length, :]
    k_end = k * jnp.exp(b_total - b_full)
    upd = _dot_tn(v.astype(BF16), k_end.astype(BF16))
    yield
    o_heads = _bdot(jnp.concatenate(probs, axis=0), v)
    assert len(out_box) == index, "the previous tile must have replaced the state before it is read"
    state_t = state_box[0]
    o_state = _bdot_nt(q * jnp.exp(b_full), state_t)
    same_head = _idiv(_iota((GROUP_WIDTH, kw), 0), HEAD_DIM) == _idiv(_iota((GROUP_WIDTH, kw), 1), head_k)
    state_box[0] = state_t * jnp.exp(b_total) + jnp.where(same_head, upd, 0.0)
    yield
    v_head = _idiv(_iota((GLA_SUB, GROUP_WIDTH), 1), HEAD_DIM)
    rows = []
    for c in range(n_sub):
        base = c * N_HEADS * GLA_SUB
        acc = jnp.zeros((GLA_SUB, GROUP_WIDTH), F32)
        for h in range(N_HEADS):
            acc = acc + jnp.where(v_head == h, o_heads[base + h * GLA_SUB:base + (h + 1) * GLA_SUB, :], 0.0)
        rows.append(acc)
    out_box.append(jnp.concatenate(rows, axis=0) + o_state)


def _gated_linear_attention_plan(q, k, v, log_f, gate, norm_w, o_ref, state_ref):
    state_box = [state_ref[...]]
    tiles = []
    stages = []
    for n in range(q.shape[0] // GLA_TILE):
        rows = slice(n * GLA_TILE, (n + 1) * GLA_TILE)
        stages.append(_gated_linear_attention_tile(n, q[rows], k[rows], v[rows], log_f[rows], state_box, tiles))

    def finish():
        state_ref[...] = state_box[0]
        o = jnp.concatenate(tiles, axis=0)
        ms = _dot_hilo(o * o, _head_group_matrix(GROUP_WIDTH, HEAD_DIM, 1.0 / HEAD_DIM).astype(BF16))
        o_ref[...] = (o * lax.rsqrt(ms + RMS_EPS) * norm_w * _silu(gate)).astype(o_ref.dtype)

    return stages, GLA_STAGGER, finish


def _gla_plan(z_ref, gk_up_ref, gk_bias_ref, norm_w_ref, o_ref, state_ref):
    z = z_ref[...]
    q = z[:, 0:128] * (GLA_DK ** -0.5)
    k = z[:, 128:256]
    v = z[:, 256:512]
    g = z[:, 512:768]
    gate_in = _dot_3pass(z[:, 768:896], gk_up_ref[...]) + gk_bias_ref[...]
    log_f = _log_sigmoid(gate_in) * (1.0 / GLA_GATE_NORMALIZER)
    return _gated_linear_attention_plan(q, k, v, log_f, g, norm_w_ref[...], o_ref, state_ref)


def _hgrn_plan(z_ref, lb_ref, log_lb_ref, norm_w_ref, o_ref, state_ref):
    z = z_ref[...]
    q = _silu(z[:, 0:256])
    f = z[:, 256:512]
    v = z[:, 512:768]
    g = z[:, 768:1024]
    lb = lb_ref[...]
    a = log_lb_ref[...]
    e = jnp.exp(-jnp.abs(f))
    inv = 1.0 / (1.0 + e)
    c = jnp.log1p(-lb) + jnp.minimum(f, 0.0) - jnp.log(1.0 + e)
    log_f = jnp.maximum(a, c) + _softplus_neg_abs(a - c)
    k = (1.0 - lb) * jnp.where(f >= 0.0, e * inv, inv)
    return _gated_linear_attention_plan(q, k, v, log_f, g, norm_w_ref[...], o_ref, state_ref)


def _gla_parts(z, gk_up, gk_bias, norm_w, step):
    out_spec, out_shape = _mix_out(z.shape[0], step)
    gk_up_pad = jnp.zeros((LANES, N_HEADS * GLA_DK), F32).at[:GLA_GATE_RANK].set(gk_up)
    return _MixerParts(
        plan=_gla_plan,
        args=[z, gk_up_pad, gk_bias.reshape(1, -1), jnp.tile(norm_w, N_HEADS).reshape(1, -1)],
        in_specs=[pl.BlockSpec((step, GLA_W), lambda i: (i, 0)),
                  _resident((LANES, N_HEADS * GLA_DK)), _resident((1, N_HEADS * GLA_DK)),
                  _resident((1, GROUP_WIDTH))],
        out_specs=[out_spec], out_shape=[out_shape],
        scratch=[pltpu.VMEM((GROUP_WIDTH, N_HEADS * GLA_DK), F32)])


def _hgrn_parts(z, lb, norm_w, step):
    out_spec, out_shape = _mix_out(z.shape[0], step)
    return _MixerParts(
        plan=_hgrn_plan,
        args=[z, lb.reshape(1, -1), jnp.log(lb).reshape(1, -1), jnp.tile(norm_w, N_HEADS).reshape(1, -1)],
        in_specs=[pl.BlockSpec((step, HGRN_W), lambda i: (i, 0)),
                  _resident((1, GROUP_WIDTH)), _resident((1, GROUP_WIDTH)), _resident((1, GROUP_WIDTH))],
        out_specs=[out_spec], out_shape=[out_shape],
        scratch=[pltpu.VMEM((GROUP_WIDTH, GROUP_WIDTH), F32)])


def _swa_block(q, kw, vw, visible, sinks, out_box):
    n = q.shape[0]
    head_cols = lambda x, h: x[:, h * HEAD_DIM:(h + 1) * HEAD_DIM]
    v_lane_head = _idiv(_iota(vw.shape, 1), HEAD_DIM)
    outs = []
    for kv in range(N_HEADS // 2):
        heads = (2 * kv, 2 * kv + 1)
        scores = _bdot_nt(jnp.concatenate([head_cols(q, h) for h in heads], axis=0), head_cols(kw, kv))
        yield
        probs, sink_terms = [], []
        for half, h in enumerate(heads):
            s = jnp.where(visible, scores[half * n:(half + 1) * n], -jnp.inf)
            sink = sinks[:, h:h + 1]
            m = jnp.maximum(jnp.max(s, axis=-1, keepdims=True), sink)
            probs.append(jnp.exp(s - m))
            sink_terms.append(jnp.exp(sink - m))
        o = _bdot(jnp.concatenate(probs, axis=0), jnp.where(v_lane_head == kv, vw, 1.0))
        yield
        sums = head_cols(o, 1 - kv)[:, 0:1]
        outs += [head_cols(o, kv)[half * n:(half + 1) * n] / (sums[half * n:(half + 1) * n] + sink_terms[half])
                 for half in range(2)]
    out_box.append(jnp.concatenate(outs, axis=-1))


def _swa_plan(q_ref, k_ref, v_ref, kp_ref, vp_ref, sink_ref, o_ref):
    w = SWA_WINDOW
    has_prev = pl.program_id(0) > 0
    q = q_ref[...] * (HEAD_DIM ** -0.5)
    k_all = jnp.concatenate([kp_ref[...], k_ref[...]], axis=0)
    v_all = jnp.concatenate([vp_ref[...], v_ref[...]], axis=0)
    q_pos = _iota((w, 2 * w), 0) + w
    k_pos = _iota((w, 2 * w), 1)
    dist = q_pos - k_pos
    in_window = (dist >= 0) & (dist < w)
    sinks = sink_ref[...]
    blocks = []
    stages = []
    for b in range(q.shape[0] // w):
        visible = in_window if b > 0 else in_window & ((k_pos >= w) | has_prev)
        stages.append(_swa_block(q[b * w:(b + 1) * w], k_all[b * w:(b + 2) * w], v_all[b * w:(b + 2) * w],
                                 visible, sinks, blocks))

    def finish():
        o_ref[...] = jnp.concatenate(blocks, axis=0).astype(o_ref.dtype)

    return stages, SWA_STAGGER, finish


def _swa_parts(z, sinks, step):
    out_spec, out_shape = _mix_out(z.shape[0], step)
    w = SWA_WINDOW
    prev = lambda col: (lambda i: (jnp.maximum(i * (step // w) - 1, 0), col))
    return _MixerParts(
        plan=_swa_plan,
        args=[z, z, z, z, z, sinks.reshape(1, -1)],
        in_specs=[pl.BlockSpec((step, 256), lambda i: (i, 0)),
                  pl.BlockSpec((step, 128), lambda i: (i, 2)), pl.BlockSpec((step, 128), lambda i: (i, 3)),
                  pl.BlockSpec((w, 128), prev(2)), pl.BlockSpec((w, 128), prev(3)),
                  _resident((1, N_HEADS))],
        out_specs=[out_spec], out_shape=[out_shape], scratch=[])


def _rwkv_chunk(index, r, k, v, a_vec, b_vec, log_w, state_box, out_box):
    c = r.shape[0]
    width = r.shape[1]
    tri = jnp.where(_iota((c, c), 1) <= _iota((c, c), 0), 1.0, 0.0).astype(F32)
    p = _dot_mask(tri, log_w)
    yield
    p_total = p[c - 1:c, :]
    decay_in = jnp.exp(p)
    decay_out = jnp.exp(-p)
    decay_end = jnp.exp(p_total - p)
    a_in = a_vec * jnp.exp(p - log_w)
    r_in = r * decay_in
    b_out = b_vec * decay_out
    k_out = k * decay_out
    b_end = b_vec * decay_end
    k_end = k * decay_end

    t_pos = _iota((c, width), 0)
    assert width == N_HEADS * c
    s_pos = _imod(_iota((c, width), 1), c)
    strict = s_pos < t_pos
    incl = s_pos <= t_pos
    expand = lambda x: _expand_heads(x, HEAD_DIM)

    scores = _bdot_nt(jnp.concatenate([a_in, r_in], axis=0),
                      jnp.concatenate([expand(b_out), expand(k_out)], axis=0))
    yield
    a_ab = jnp.where(strict, scores[0:c, 0:width], 0.0)
    a_ak = jnp.where(strict, scores[0:c, width:2 * width], 0.0)
    a_rb = jnp.where(incl, scores[c:2 * c, 0:width], 0.0)
    a_rk = jnp.where(incl, scores[c:2 * c, width:2 * width], 0.0)

    t_inv = jnp.where(s_pos == t_pos, 1.0, 0.0) + a_ab
    from_v = _bdot(jnp.concatenate([a_ak, a_rk], axis=0), expand(v))
    x1 = from_v[0:c]
    y_from_v = from_v[c:2 * c]
    power = _bdot(a_ab, expand(a_ab))
    yield
    n_factors = (c - 1).bit_length()
    for _ in range(n_factors - 2):
        both = _bdot(jnp.concatenate([t_inv, power], axis=0), expand(power))
        t_inv = t_inv + both[0:c]
        power = both[c:2 * c]
        yield
    t_inv = t_inv + _bdot(t_inv, expand(power))
    yield
    sol = _bdot(t_inv, jnp.concatenate([expand(x1), expand(a_in)], axis=1))
    yield
    u0 = sol[:, 0:width]
    w_mat = sol[:, width:2 * width]

    assert len(out_box) == index, "the previous chunk must have replaced the state before it is read"
    state = state_box[0]
    from_state = _bdot_nt(jnp.concatenate([w_mat, r_in], axis=0), state)
    yield
    u = u0 + from_state[0:c]
    y = _bdot(a_rb, expand(u)) + y_from_v + from_state[c:2 * c]
    upd = _dot_tn(jnp.concatenate([u, v], axis=0).astype(BF16),
                  jnp.concatenate([b_end, k_end], axis=0).astype(BF16))
    same_head = _idiv(_iota((width, width), 0), HEAD_DIM) == _idiv(_iota((width, width), 1), HEAD_DIM)
    state_box[0] = state * jnp.exp(p_total) + jnp.where(same_head, upd, 0.0)
    out_box.append(y)


def _rwkv_plan(has_vres, *refs):
    if has_vres:
        (z_ref, zp_ref, vfirst_ref, mu_ref, w0_ref, wup_ref, a0_ref, aup_ref, gup_ref, kk_ref, ka_ref,
         rk_ref, lnw_ref, lnb_ref, v0_ref, vup_ref, o_ref, state_ref) = refs
    else:
        (z_ref, zp_ref, mu_ref, w0_ref, wup_ref, a0_ref, aup_ref, gup_ref, kk_ref, ka_ref,
         rk_ref, lnw_ref, lnb_ref, o_ref, vout_ref, state_ref) = refs
    step = pl.program_id(0)
    z = z_ref[...]
    last_prev = jnp.where(step > 0, zp_ref[7:8, :], 0.0)
    prev = jnp.where(_iota(z.shape, 0) == 0, last_prev, pltpu.roll(z, 1, axis=0))
    zr = z + (prev - z) * mu_ref[...]
    r = zr[:, 0:256]
    k = zr[:, 256:512]
    v = zr[:, 512:768]
    low = zr[:, 768:896]
    w_pre = w0_ref[...] + _dot_3pass(jnp.tanh(low), wup_ref[...])
    w_log = -(jnp.maximum(-w_pre, 0.0) + _softplus_neg_abs(w_pre)) - 0.5
    log_w = -jnp.exp(w_log)
    a = _sigmoid(a0_ref[...] + _dot_3pass(low, aup_ref[...]))
    g = _dot_3pass(_sigmoid(low), gup_ref[...])
    if has_vres:
        v = v + (vfirst_ref[...] - v) * _sigmoid(v0_ref[...] + _dot_3pass(low, vup_ref[...]))
    else:
        vout_ref[...] = v
    head_sum = _head_group_matrix(GROUP_WIDTH, HEAD_DIM, 1.0).astype(BF16)
    kk = k * kk_ref[...]
    kk = kk / jnp.maximum(jnp.sqrt(_dot_hilo(kk * kk, head_sum)), 1e-12)
    k = k * (1.0 + (a - 1.0) * ka_ref[...])
    a_vec = -kk
    b_vec = kk * a

    c = RWKV_CHUNK
    state_box = [state_ref[...]]
    chunks = []
    stages = []
    for n in range(z.shape[0] // c):
        rows = slice(n * c, (n + 1) * c)
        stages.append(_rwkv_chunk(n, r[rows], k[rows], v[rows], a_vec[rows], b_vec[rows], log_w[rows],
                                  state_box, chunks))

    def finish():
        state_ref[...] = state_box[0]
        y = jnp.concatenate(chunks, axis=0)
        head_mean = _head_group_matrix(GROUP_WIDTH, HEAD_DIM, 1.0 / HEAD_DIM).astype(BF16)
        mu_y = _dot_hilo(y, head_mean)
        d = y - mu_y
        var_y = _dot_hilo(d * d, head_mean)
        y = d * lax.rsqrt(var_y + RWKV_GN_EPS) * lnw_ref[...] + lnb_ref[...]
        bonus = _dot_hilo(r * k * rk_ref[...], head_sum) * v
        o_ref[...] = ((y + bonus) * g).astype(o_ref.dtype)

    return stages, RWKV_STAGGER, finish


def _rwkv_parts(z, mu, w0, w_up, a0, a_up, g_up, k_k, k_a, r_k, lnx_w, lnx_b, vres, step):
    t = z.shape[0]
    c = step
    row = lambda a: a.reshape(1, -1)
    low_rows = lambda a, start: jnp.zeros((LANES, GROUP_WIDTH), F32).at[start:start + a.shape[0]].set(a)
    has_vres = vres is not None
    mu_full = jnp.zeros((RWKV_W,), F32).at[:RWKV_COLS].set(mu)
    vec = _resident((1, GROUP_WIDTH))
    mat = _resident((LANES, GROUP_WIDTH))
    tile = pl.BlockSpec((c, GROUP_WIDTH), lambda i: (i, 0))
    z_specs = [pl.BlockSpec((c, RWKV_W), lambda i: (i, 0)),
               pl.BlockSpec((8, RWKV_W), lambda i: (jnp.maximum(i * (c // 8) - 1, 0), 0))]
    common = [row(w0), low_rows(w_up, 0), row(a0), low_rows(a_up, 16), low_rows(g_up, 32),
              row(k_k), row(k_a), row(r_k), row(lnx_w), row(lnx_b)]
    common_specs = [vec, mat, vec, mat, mat, vec, vec, vec, vec, vec]
    if has_vres:
        v_first, vres_mu, v0, v_up = vres
        mu_full = mu_full.at[RWKV_COLS:RWKV_COLS + RWKV_V_RANK].set(vres_mu)
        args = [z, z, v_first, row(mu_full)] + common + [row(v0), low_rows(v_up, 64)]
        in_specs = z_specs + [tile, _resident((1, RWKV_W))] + common_specs + [vec, mat]
        out_specs = [tile]
        out_shape = [jax.ShapeDtypeStruct((t, GROUP_WIDTH), MIX_DTYPE)]
    else:
        args = [z, z, row(mu_full)] + common
        in_specs = z_specs + [_resident((1, RWKV_W))] + common_specs
        out_specs = [tile, tile]
        out_shape = [jax.ShapeDtypeStruct((t, GROUP_WIDTH), MIX_DTYPE), jax.ShapeDtypeStruct((t, GROUP_WIDTH), F32)]
    return _MixerParts(plan=functools.partial(_rwkv_plan, has_vres), args=args, in_specs=in_specs,
                       out_specs=out_specs, out_shape=out_shape,
                       scratch=[pltpu.VMEM((GROUP_WIDTH, GROUP_WIDTH), F32)])


def _store_token_tiles(ref, index, x):
    n = x.shape[0]
    for j in range(SUBLANES):
        ref[(*index, pl.ds(j, n, stride=SUBLANES), slice(None))] = x[:, j * LANES:(j + 1) * LANES]


def _load_token_tiles(ref, index, n):
    return jnp.concatenate([ref[(*index, pl.ds(j, n, stride=SUBLANES), slice(None))] for j in range(SUBLANES)],
                           axis=-1)


def _top2_route(logits):
    row = _iota(logits.shape, 0).astype(F32)
    m1 = jnp.max(logits, axis=0, keepdims=True)
    i1 = jnp.min(jnp.where(logits == m1, row, N_EXPERTS), axis=0, keepdims=True)
    rest = jnp.where(row == i1, -jnp.inf, logits)
    m2 = jnp.max(rest, axis=0, keepdims=True)
    i2 = jnp.min(jnp.where(rest == m2, row, N_EXPERTS), axis=0, keepdims=True)
    e2 = jnp.exp(m2 - m1)
    g1 = 1.0 / (1.0 + e2)
    g2 = e2 * g1
    return jnp.where(row == 0, i1, jnp.where(row == 1, i2, jnp.where(row == 2, g1, jnp.where(row == 3, g2, 0.0))))


def _mix_residual_norm(alpha, x_ref, mix_refs, w_ref, g_ref, b_ref):
    acc = alpha * x_ref[...]
    for h, ref in enumerate(mix_refs):
        acc = acc + _dot(ref[...], w_ref[h * GROUP_WIDTH:(h + 1) * GROUP_WIDTH, :])
    return _layer_norm(acc, g_ref[...], b_ref[...])


def _outproj_route_kernel(alpha, x_ref, o0_ref, o1_ref, o2_ref, o3_ref, w_ref, g_ref, b_ref, router_ref,
                          tiles_ref, route_ref):
    y = _mix_residual_norm(alpha, x_ref, (o0_ref, o1_ref, o2_ref, o3_ref), w_ref, g_ref, b_ref)
    route_ref[...] = _top2_route(_dot_3pass_nt(router_ref[...], y))
    _store_token_tiles(tiles_ref, (), y)


def _out_proj_ln_route(alpha, x, mixes, w_out, g, b, router):
    t = x.shape[0]
    n = WIDE_ROW_TILE
    row_d = pl.BlockSpec((n, D_MODEL), lambda i: (i, 0))
    row_g = pl.BlockSpec((n, GROUP_WIDTH), lambda i: (i, 0))
    return pl.pallas_call(
        functools.partial(_outproj_route_kernel, alpha),
        grid=(t // n,),
        in_specs=[row_d, row_g, row_g, row_g, row_g, _resident((D_MODEL, D_MODEL)),
                  _resident((1, D_MODEL)), _resident((1, D_MODEL)), _resident((N_EXPERTS, D_MODEL))],
        out_specs=[pl.BlockSpec((n * SUBLANES, LANES), lambda i: (i, 0)),
                   pl.BlockSpec((N_EXPERTS, n), lambda i: (0, i))],
        out_shape=[jax.ShapeDtypeStruct((t * SUBLANES, LANES), F32), jax.ShapeDtypeStruct((N_EXPERTS, t), F32)],
        compiler_params=_params(("parallel",)),
        name="out_proj_ln_route",
    )(x, *mixes, w_out.astype(BF16), g.reshape(1, -1), b.reshape(1, -1), router.T)


def _ln_embed(y, ln_g, ln_b, p, ple_gate, ple_proj):
    x = _layer_norm(y, ln_g, ln_b)
    gate = _sigmoid(_dot(x.astype(BF16), ple_gate))
    return x + gate * _dot(p.astype(BF16), ple_proj)


def _dense_ffn_kernel(alpha, x_ref, o0_ref, o1_ref, o2_ref, o3_ref, wo_ref, g1_ref, b1_ref, p_ref, wg_ref, wu_ref,
                      wd_ref, g_ref, b_ref, pg_ref, pp_ref, y_ref, acc_ref):
    x = _mix_residual_norm(alpha, x_ref, (o0_ref, o1_ref, o2_ref, o3_ref), wo_ref, g1_ref, b1_ref)
    xb = x.astype(BF16)
    acc_ref[...] = alpha * x
    for j in range(D_FF // FF_CHUNK):
        cols = slice(j * FF_CHUNK, (j + 1) * FF_CHUNK)
        h = _silu(_dot(xb, wg_ref[:, cols])) * _dot(xb, wu_ref[:, cols])
        acc_ref[...] += _dot(h.astype(BF16), wd_ref[cols, :])
    y_ref[...] = _ln_embed(acc_ref[...], g_ref[...], b_ref[...], p_ref[...], pg_ref[...], pp_ref[...])


def _dense_layer_tail(alpha, x, mixes, w_out, g1, b1, p_all, p_block, w_gate, w_up, w_down, g, b, ple_gate, ple_proj):
    t = x.shape[0]
    row_d = pl.BlockSpec((ROW_TILE, D_MODEL), lambda i: (i, 0))
    row_g = pl.BlockSpec((ROW_TILE, GROUP_WIDTH), lambda i: (i, 0))
    return pl.pallas_call(
        functools.partial(_dense_ffn_kernel, alpha),
        grid=(t // ROW_TILE,),
        in_specs=[row_d, row_g, row_g, row_g, row_g, _resident((D_MODEL, D_MODEL)),
                  _resident((1, D_MODEL)), _resident((1, D_MODEL)),
                  pl.BlockSpec((ROW_TILE, PLE_DIM), lambda i: (p_block + i, 0)),
                  _resident((D_MODEL, D_FF)), _resident((D_MODEL, D_FF)), _resident((D_FF, D_MODEL)),
                  _resident((1, D_MODEL)), _resident((1, D_MODEL)),
                  _resident((D_MODEL, D_MODEL)), _resident((PLE_DIM, D_MODEL))],
        out_specs=row_d,
        out_shape=jax.ShapeDtypeStruct((t, D_MODEL), F32),
        scratch_shapes=[pltpu.VMEM((ROW_TILE, D_MODEL), F32)],
        compiler_params=_params(("parallel",)),
        name="dense_ffn_tail",
    )(x, *mixes, w_out.astype(BF16), g1.reshape(1, -1), b1.reshape(1, -1), p_all,
      w_gate.astype(BF16), w_up.astype(BF16), w_down.astype(BF16), g.reshape(1, -1), b.reshape(1, -1),
      ple_gate.astype(BF16), ple_proj.astype(BF16))


def _weight_group_copies(e, group, wg_hbm, wu_hbm, wd_hbm, stage_cols_ref, stage_rows_ref, sem):
    slot = group % 2
    lo, hi = group * MOE_WEIGHT_CHUNK, (group + 1) * MOE_WEIGHT_CHUNK
    return (pltpu.make_async_copy(wg_hbm.at[e, :, lo:hi], stage_cols_ref.at[slot, 0], sem.at[slot]),
            pltpu.make_async_copy(wu_hbm.at[e, :, lo:hi], stage_cols_ref.at[slot, 1], sem.at[slot]),
            pltpu.make_async_copy(wd_hbm.at[e, lo:hi, :], stage_rows_ref.at[slot], sem.at[slot]))


def _expert_kernel(row_tok_ref, row_dst_ref, block_e_ref, n_used_ref, x_hbm, wg_hbm, wu_hbm, wd_hbm, y_hbm,
                   rows_ref, xb_ref, acc_ref, ybuf_ref, wg_ref, wu_ref, wd_ref, stage_cols_ref, stage_rows_ref,
                   gather_sem, scatter_sem, weight_sem):
    i = pl.program_id(0)
    n_used = n_used_ref[0]
    last = pl.num_programs(0) - 1
    n = MOE_ROWS
    slot = lax.rem(i, 2)
    other = 1 - slot
    tile = lambda first_row: pl.ds(pl.multiple_of(first_row, SUBLANES), SUBLANES)
    gather_row = lambda tok_row, s, r: pltpu.make_async_copy(
        x_hbm.at[tile(tok_row)], rows_ref.at[s, tile(r * SUBLANES)], gather_sem.at[s])
    scatter_row = lambda dst_row, s, r: pltpu.make_async_copy(
        ybuf_ref.at[s, tile(r * SUBLANES)], y_hbm.at[tile(dst_row)], scatter_sem.at[s])
    block_rows = n * SUBLANES
    gather_block = lambda s: pltpu.make_async_copy(x_hbm.at[pl.ds(0, block_rows)], rows_ref.at[s], gather_sem.at[s])
    scatter_block = lambda s: pltpu.make_async_copy(ybuf_ref.at[s], y_hbm.at[pl.ds(0, block_rows)],
                                                    scatter_sem.at[s])

    @pl.when(i == 0)
    def _():
        ybuf_ref[1] = jnp.zeros((block_rows, LANES), F32)

        def start(r, carry):
            gather_row(row_tok_ref[r], 0, r).start()
            return carry

        lax.fori_loop(0, n, start, 0)

    def scatter_all(first_dst, s):
        def start(r, carry):
            scatter_row(row_dst_ref[first_dst + r], s, r).start()
            return carry

        lax.fori_loop(0, n, start, 0)
        scatter_block(s).wait()

    expert = block_e_ref[i]
    new_expert = (i == 0) | (expert != block_e_ref[jnp.maximum(i - 1, 0)])

    weight_group = lambda g: _weight_group_copies(expert, g, wg_hbm, wu_hbm, wd_hbm, stage_cols_ref,
                                                  stage_rows_ref, weight_sem)
    chunks_per_group = MOE_WEIGHT_CHUNK // FF_CHUNK
    n_groups = D_FF_EXPERT // MOE_WEIGHT_CHUNK

    def take_weight_group(g):
        if g + 1 < n_groups:
            for copy in weight_group(g + 1):
                copy.start()
        for copy in weight_group(g):
            copy.wait()
        piece = slice(g * MOE_WEIGHT_CHUNK, (g + 1) * MOE_WEIGHT_CHUNK)
        wg_ref[:, piece] = stage_cols_ref[g % 2, 0].astype(BF16)
        wu_ref[:, piece] = stage_cols_ref[g % 2, 1].astype(BF16)
        wd_ref[piece, :] = stage_rows_ref[g % 2].astype(BF16)

    @pl.when(i < n_used)
    def _():
        @pl.when(new_expert)
        def _():
            for copy in weight_group(0):
                copy.start()

        gather_block(slot).wait()
        xb_ref[...] = _load_token_tiles(rows_ref, (slot,), n).astype(BF16)
        n_chunks = D_FF_EXPERT // FF_CHUNK
        rows_per_chunk = -(-n // (n_chunks - MOE_DMA_FREE_CHUNKS))
        for j in range(n_chunks):
            if j % chunks_per_group == 0:
                pl.when(new_expert)(functools.partial(take_weight_group, j // chunks_per_group))
            cols = slice(j * FF_CHUNK, (j + 1) * FF_CHUNK)
            xb = xb_ref[...]
            h = _silu(_dot(xb, wg_ref[:, cols])) * _dot(xb, wu_ref[:, cols])
            part = _dot(h.astype(BF16), wd_ref[cols, :])
            if j == 0:
                acc_ref[...] = part
            else:
                acc_ref[...] += part
            for r in range(j * rows_per_chunk, min((j + 1) * rows_per_chunk, n)):
                gather_row(row_tok_ref[(i + 1) * n + r], other, r).start(priority=GATHER_DMA_PRIORITY)
                scatter_row(row_dst_ref[i * n + r], other, r).start(priority=SCATTER_DMA_PRIORITY)
        _store_token_tiles(ybuf_ref, (slot,), acc_ref[...])
        scatter_block(other).wait()

    @pl.when(i == n_used)
    def _():
        gather_block(slot).wait()
        scatter_all(i * n, other)

    @pl.when(i >= n_used)
    def _():
        rows_ref[slot] = jnp.zeros((block_rows, LANES), F32)
        fill = pltpu.make_async_copy(rows_ref.at[slot], y_hbm.at[pl.ds(i * block_rows, block_rows)],
                                     scatter_sem.at[slot])
        fill.start()
        fill.wait()

    @pl.when((i == last) & (i < n_used))
    def _():
        gather_block(other).wait()
        scatter_all((i + 1) * n, slot)


def _expert_rows(x, row_tok, row_dst, block_e, n_used, n_out_rows, w_gate, w_up, w_down):
    n_blocks = block_e.shape[0]
    hbm = pl.BlockSpec(memory_space=pl.ANY)
    grid_spec = pltpu.PrefetchScalarGridSpec(
        num_scalar_prefetch=4,
        grid=(n_blocks,),
        in_specs=[hbm, hbm, hbm, hbm],
        out_specs=hbm,
        scratch_shapes=[pltpu.VMEM((2, MOE_ROWS * SUBLANES, LANES), F32), pltpu.VMEM((MOE_ROWS, D_MODEL), BF16),
                        pltpu.VMEM((MOE_ROWS, D_MODEL), F32), pltpu.VMEM((2, MOE_ROWS * SUBLANES, LANES), F32),
                        pltpu.VMEM((D_MODEL, D_FF_EXPERT), BF16), pltpu.VMEM((D_MODEL, D_FF_EXPERT), BF16),
                        pltpu.VMEM((D_FF_EXPERT, D_MODEL), BF16),
                        pltpu.VMEM((2, 2, D_MODEL, MOE_WEIGHT_CHUNK), F32),
                        pltpu.VMEM((2, MOE_WEIGHT_CHUNK, D_MODEL), F32),
                        pltpu.SemaphoreType.DMA((2,)), pltpu.SemaphoreType.DMA((2,)),
                        pltpu.SemaphoreType.DMA((2,))],
    )
    return pl.pallas_call(
        _expert_kernel,
        grid_spec=grid_spec,
        out_shape=jax.ShapeDtypeStruct((n_out_rows * SUBLANES, LANES), F32),
        compiler_params=_params(("arbitrary",)),
        name="moe_experts",
    )(row_tok, row_dst, block_e, n_used.reshape(1), x, w_gate, w_up, w_down)


def _combine_kernel(alpha, x_ref, y0_ref, y1_ref, gates_ref, p_ref, g_ref, b_ref, pg_ref, pp_ref, o_ref):
    n = WIDE_ROW_TILE
    gates = gates_ref[...]
    f = _load_token_tiles(y0_ref, (), n) * gates[:, 0:1] + _load_token_tiles(y1_ref, (), n) * gates[:, 1:2]
    o_ref[...] = _ln_embed(alpha * _load_token_tiles(x_ref, (), n) + f, g_ref[...], b_ref[...], p_ref[...],
                           pg_ref[...], pp_ref[...])


def _moe_combine_tail(alpha, x_tiles, p_all, p_block, y_tiles, gates, g, b, ple_gate, ple_proj):
    t = x_tiles.shape[0] // SUBLANES
    n = WIDE_ROW_TILE
    row_d = pl.BlockSpec((n, D_MODEL), lambda i: (i, 0))
    tiles_d = pl.BlockSpec((n * SUBLANES, LANES), lambda i: (i, 0))
    return pl.pallas_call(
        functools.partial(_combine_kernel, alpha),
        grid=(t // n,),
        in_specs=[tiles_d, tiles_d, pl.BlockSpec((n * SUBLANES, LANES), lambda i: (t // n + i, 0)),
                  pl.BlockSpec((n, 2), lambda i: (i, 0)), pl.BlockSpec((n, PLE_DIM), lambda i: (p_block + i, 0)),
                  _resident((1, D_MODEL)), _resident((1, D_MODEL)),
                  _resident((D_MODEL, D_MODEL)), _resident((PLE_DIM, D_MODEL))],
        out_specs=row_d,
        out_shape=jax.ShapeDtypeStruct((t, D_MODEL), F32),
        compiler_params=_params(("parallel",)),
        name="moe_combine_tail",
    )(x_tiles, y_tiles, y_tiles, gates, p_all, g.reshape(1, -1), b.reshape(1, -1), ple_gate.astype(BF16),
      ple_proj.astype(BF16))


def _moe_tail(alpha, x_tiles, route, p_all, p_block, w_gate, w_up, w_down, g, b, ple_gate, ple_proj):
    t = route.shape[1]
    experts = route[0:2].T.astype(jnp.int32)
    gates = route[2:4].T
    e_flat = experts.reshape(-1)
    onehot = (e_flat[:, None] == jnp.arange(N_EXPERTS, dtype=jnp.int32)[None, :]).astype(jnp.int32)
    counts = jnp.sum(onehot, axis=0)
    padded = (counts + MOE_ROWS - 1) // MOE_ROWS * MOE_ROWS
    pad_end = jnp.cumsum(padded)
    pad_start = pad_end - padded
    n_blocks = (2 * t) // MOE_ROWS + N_EXPERTS
    n_rows = n_blocks * MOE_ROWS
    block_start = jnp.arange(n_blocks, dtype=jnp.int32) * MOE_ROWS
    block_e = jnp.minimum(jnp.sum((block_start[:, None] >= pad_end[None, :]).astype(jnp.int32), axis=1),
                          N_EXPERTS - 1)
    order = jnp.argsort(e_flat, stable=True).astype(jnp.int32)
    seg_end = jnp.cumsum(counts)
    seg_start = seg_end - counts
    per_row = lambda per_expert: jnp.repeat(per_expert[block_e], MOE_ROWS)
    place = jnp.arange(n_rows, dtype=jnp.int32) - per_row(pad_start - seg_start)
    used = place < per_row(seg_end)
    row_assign = jnp.where(used, order[jnp.clip(place, 0, 2 * t - 1)], -1)
    spare = 2 * t + jnp.cumsum(jnp.where(used, 0, 1).astype(jnp.int32)) - 1
    row_tok = jnp.where(used, row_assign // 2, 0)
    row_dst = jnp.where(used, (row_assign % 2) * t + row_assign // 2, spare)
    n_spare = n_rows - 2 * t
    first_dst = 2 * t + n_spare + jnp.arange(MOE_ROWS, dtype=jnp.int32)
    row_tok = jnp.concatenate([row_tok, jnp.zeros((MOE_ROWS,), jnp.int32)])
    row_dst = jnp.concatenate([first_dst, row_dst])
    n_used = (pad_end[-1] // MOE_ROWS).astype(jnp.int32)
    y_tiles = _expert_rows(x_tiles, row_tok * SUBLANES, row_dst * SUBLANES, block_e, n_used,
                           2 * t + n_spare + MOE_ROWS, w_gate, w_up, w_down)
    return _moe_combine_tail(alpha, x_tiles, p_all, p_block, y_tiles, gates, g, b, ple_gate, ple_proj)


def kernel(x, p, w_in, w_out, gla_gk_up, gla_gk_bias, gla_norm_w, hgrn_lower_bounds, hgrn_norm_w, swa_sinks, rwkv_mu, rwkv_w0, rwkv_w_up, rwkv_a0, rwkv_a_up, rwkv_g_up, rwkv_k_k, rwkv_k_a, rwkv_r_k, rwkv_lnx_w, rwkv_lnx_b, rwkv_vres_down, rwkv_vres_mu, rwkv_v0, rwkv_vres_up, ln1_g, ln1_b, ln2_g, ln2_b, ffn_w_gate, ffn_w_up, ffn_w_down, moe_router, moe_w_gate, moe_w_up, moe_w_down, ple_proj, ple_gate):
    bsz, seq, d = x.shape
    depth = w_in.shape[0]
    alpha = (2.0 * depth) ** 0.25
    lbs = jnp.cumsum(jax.nn.softmax(hgrn_lower_bounds.astype(F32), axis=0), axis=0)
    lbs = lbs - lbs[0]
    p_all = p.reshape(-1, PLE_DIM)
    outs = []
    for bi in range(bsz):
        xt = x[bi]
        v_first = None
        for i in range(depth):
            w = _group_in_weights(w_in[i], None if i == 0 else rwkv_vres_down[i - 1])
            z_gla, z_hgrn, z_swa, z_rwkv = _in_proj(xt, w)
            vres = None if i == 0 else (v_first, rwkv_vres_mu[i - 1], rwkv_v0[i - 1], rwkv_vres_up[i - 1])
            mixed = _call_mixers(
                [_rwkv_parts(z_rwkv, rwkv_mu[i], rwkv_w0[i], rwkv_w_up[i], rwkv_a0[i], rwkv_a_up[i],
                             rwkv_g_up[i], rwkv_k_k[i], rwkv_k_a[i], rwkv_r_k[i].reshape(-1),
                             rwkv_lnx_w[i], rwkv_lnx_b[i], vres, MIX_STEP),
                 _gla_parts(z_gla, gla_gk_up[i], gla_gk_bias[i], gla_norm_w[i], MIX_STEP),
                 _hgrn_parts(z_hgrn, lbs[i], hgrn_norm_w[i], MIX_STEP),
                 _swa_parts(z_swa, swa_sinks[i], MIX_STEP)],
                seq, MIX_STEP)
            mixed = list(mixed)
            if i == 0:
                v_first = mixed.pop(1)
            o_rwkv, o_gla, o_hgrn, o_swa = mixed
            mixes = (o_gla, o_hgrn, o_swa, o_rwkv)
            j = i // 2
            p_row = (i * bsz + bi) * seq
            if i % 2 == 0:
                xt = _dense_layer_tail(alpha, xt, mixes, w_out[i], ln1_g[i], ln1_b[i], p_all, p_row // ROW_TILE,
                                       ffn_w_gate[j], ffn_w_up[j], ffn_w_down[j], ln2_g[i], ln2_b[i],
                                       ple_gate[i], ple_proj[i])
            else:
                x_tiles, route = _out_proj_ln_route(alpha, xt, mixes, w_out[i], ln1_g[i], ln1_b[i], moe_router[j])
                xt = _moe_tail(alpha, x_tiles, route, p_all, p_row // WIDE_ROW_TILE, moe_w_gate[j], moe_w_up[j],
                               moe_w_down[j],
                               ln2_g[i], ln2_b[i], ple_gate[i], ple_proj[i])
        outs.append(xt)
    return jnp.stack(outs, axis=0)
```
